```python
import jax, jax.numpy as jnp
from jax import lax
import numpy as np

D_MODEL = 1024
BATCH = 8
SEQ = 8192
DEPTH = 2

N_MIXERS = 2
N_HGRN = (DEPTH + N_MIXERS - 1) // N_MIXERS
N_CONV = DEPTH // N_MIXERS
HGRN_EXPAND = 128
HGRN_HEADS = D_MODEL // HGRN_EXPAND
HGRN_DK = HGRN_EXPAND
HGRN_FDIM = HGRN_HEADS * HGRN_DK
HGRN_DV = D_MODEL // HGRN_HEADS
HGRN_CHUNK = 64
CONV_WIDTH = 31
FFN_CONV_WIDTH = 3
D_FF = 2816
N_MOD = 6
EPS = 1e-6

kernel_name = "hybrid_hgrn2_conformer_convffn_adaln"


def rmsnorm(x, g):
    xf = x.astype(jnp.float32)
    y = xf * lax.rsqrt(jnp.mean(xf * xf, axis=-1, keepdims=True) + EPS)
    return (y * g).astype(x.dtype)


def layernorm(x, g, b):
    xf = x.astype(jnp.float32)
    mu = jnp.mean(xf, axis=-1, keepdims=True)
    var = jnp.mean(jnp.square(xf - mu), axis=-1, keepdims=True)
    y = (xf - mu) * lax.rsqrt(var + EPS)
    return (y * g + b).astype(x.dtype)


def causal_dwconv(x, w, b):
    width, ch = w.shape
    y = lax.conv_general_dilated(
        x, w[:, None, :].astype(x.dtype), window_strides=(1,), padding=[(width - 1, 0)],
        dimension_numbers=("NWC", "WIO", "NWC"), feature_group_count=ch)
    return y + b


def hgrn2_mixer(h, w_in, lb, gnorm_g, w_out):
    B, T, _ = h.shape
    nc = T // HGRN_CHUNK
    proj = h @ w_in
    q, fz, i, g = jnp.split(proj, [HGRN_FDIM, 2 * HGRN_FDIM, 2 * HGRN_FDIM + D_MODEL], axis=-1)
    q = jax.nn.silu(q.astype(jnp.float32))
    f = lb + (1.0 - lb) * jax.nn.sigmoid(fz.astype(jnp.float32))
    k = 1.0 - f
    logf = jnp.log(f)
    v = i.astype(jnp.float32)

    def to_chunks(a, d):
        a = a.reshape(B, nc, HGRN_CHUNK, HGRN_HEADS, d)
        return jnp.transpose(a, (1, 0, 3, 2, 4))

    qc, kc, gc = to_chunks(q, HGRN_DK), to_chunks(k, HGRN_DK), to_chunks(logf, HGRN_DK)
    vc = to_chunks(v, HGRN_DV)
    causal = jnp.tril(jnp.ones((HGRN_CHUNK, HGRN_CHUNK), dtype=bool))[:, :, None]

    def step(S, inp):
        qb, kb, vb, gb = inp
        bcum = jnp.cumsum(gb, axis=2)
        diff = bcum[:, :, :, None, :] - bcum[:, :, None, :, :]
        decay = jnp.exp(jnp.where(causal, diff, -jnp.inf))
        scores = jnp.einsum("bhtd,bhtsd,bhsd->bhts", qb, decay, kb)
        o = jnp.einsum("bhts,bhsv->bhtv", scores, vb)
        o = o + jnp.einsum("bhtd,bhdv->bhtv", qb * jnp.exp(bcum), S)
        bend = bcum[:, :, -1]
        S_new = jnp.exp(bend)[..., None] * S + jnp.einsum(
            "bhsd,bhsv->bhdv", kb * jnp.exp(bend[:, :, None, :] - bcum), vb)
        return S_new, o

    S0 = jnp.zeros((B, HGRN_HEADS, HGRN_DK, HGRN_DV), jnp.float32)
    _, oc = lax.scan(step, S0, (qc, kc, vc, gc))
    o = jnp.transpose(oc, (1, 0, 3, 2, 4)).reshape(B, T, HGRN_HEADS, HGRN_DV)
    o = rmsnorm(o, gnorm_g).reshape(B, T, D_MODEL)
    o = o * jax.nn.silu(g.astype(jnp.float32))
    return o.astype(h.dtype) @ w_out


def conformer_conv_mixer(h, w_in, b_in, dw_w, dw_b, ln_g, ln_b, w_out, b_out):
    u = h @ w_in + b_in
    a, gt = jnp.split(u, 2, axis=-1)
    u = a * jax.nn.sigmoid(gt)
    u = causal_dwconv(u, dw_w, dw_b)
    u = jax.nn.silu(layernorm(u, ln_g, ln_b))
    return u @ w_out + b_out


def conv_ffn(h, w_up, dw_w, dw_b, w_down):
    u = causal_dwconv(h @ w_up, dw_w, dw_b)
    a, b = jnp.split(u, 2, axis=-1)
    return (jax.nn.silu(a) * b) @ w_down


def _fwd_setup_inputs(seed: int = 0) -> dict:
    key = jax.random.key(seed)
    ks = jax.random.split(key, 24)
    D, F = D_MODEL, D_FF
    nrm = lambda k, shape, s: jax.random.normal(k, shape, jnp.float32) * s
    return {
        "x": nrm(ks[0], (BATCH, SEQ, D), 1.0),
        "c": nrm(ks[1], (BATCH, D), 1.0),
        "ada_w": nrm(ks[2], (DEPTH, D, N_MOD * D), 0.5 * D ** -0.5),
        "ada_b": nrm(ks[3], (DEPTH, N_MOD * D), 0.02),
        "pre_mix_g": 1.0 + nrm(ks[4], (DEPTH, D), 0.02),
        "post_mix_g": 1.0 + nrm(ks[5], (DEPTH, D), 0.02),
        "pre_ffn_g": 1.0 + nrm(ks[6], (DEPTH, D), 0.02),
        "post_ffn_g": 1.0 + nrm(ks[7], (DEPTH, D), 0.02),
        "hgrn_w_in": nrm(ks[8], (N_HGRN, D, 2 * HGRN_FDIM + 2 * D), D ** -0.5),
        "hgrn_lb_logits": nrm(ks[9], (DEPTH + 1, HGRN_FDIM), 1.0),
        "hgrn_gnorm_g": 1.0 + nrm(ks[10], (N_HGRN, HGRN_DV), 0.02),
        "hgrn_w_out": nrm(ks[11], (N_HGRN, D, D), D ** -0.5),
        "conv_w_in": nrm(ks[12], (N_CONV, D, 2 * D), D ** -0.5),
        "conv_b_in": nrm(ks[13], (N_CONV, 2 * D), 0.02),
        "conv_dw_w": nrm(ks[14], (N_CONV, CONV_WIDTH, D), CONV_WIDTH ** -0.5),
        "conv_dw_b": nrm(ks[15], (N_CONV, D), 0.02),
        "conv_ln_g": 1.0 + nrm(ks[16], (N_CONV, D), 0.02),
        "conv_ln_b": nrm(ks[17], (N_CONV, D), 0.02),
        "conv_w_out": nrm(ks[18], (N_CONV, D, D), D ** -0.5),
        "conv_b_out": nrm(ks[19], (N_CONV, D), 0.02),
        "ffn_w_up": nrm(ks[20], (DEPTH, D, 2 * F), D ** -0.5),
        "ffn_dw_w": nrm(ks[21], (DEPTH, FFN_CONV_WIDTH, 2 * F), FFN_CONV_WIDTH ** -0.5),
        "ffn_dw_b": nrm(ks[22], (DEPTH, 2 * F), 0.02),
        "ffn_w_down": nrm(ks[23], (DEPTH, F, D), F ** -0.5),
    }


def _fwd_reference(x, c, ada_w, ada_b, pre_mix_g, post_mix_g, pre_ffn_g, post_ffn_g,
              hgrn_w_in, hgrn_lb_logits, hgrn_gnorm_g, hgrn_w_out,
              conv_w_in, conv_b_in, conv_dw_w, conv_dw_b, conv_ln_g, conv_ln_b,
              conv_w_out, conv_b_out, ffn_w_up, ffn_dw_w, ffn_dw_b, ffn_w_down):
    lb_all = jnp.cumsum(jax.nn.softmax(hgrn_lb_logits.astype(jnp.float32), axis=0), axis=0)
    cond = jax.nn.silu(c)
    for i in range(DEPTH):
        mod = (cond @ ada_w[i] + ada_b[i])[:, None, :]
        sh1, sc1, g1, sh2, sc2, g2 = jnp.split(mod, N_MOD, axis=-1)
        h = rmsnorm(x, pre_mix_g[i]) * (1.0 + sc1) + sh1
        j = i // N_MIXERS
        if i % N_MIXERS == 0:
            y = hgrn2_mixer(h, hgrn_w_in[j], lb_all[i], hgrn_gnorm_g[j], hgrn_w_out[j])
        else:
            y = conformer_conv_mixer(h, conv_w_in[j], conv_b_in[j], conv_dw_w[j], conv_dw_b[j],
                                     conv_ln_g[j], conv_ln_b[j], conv_w_out[j], conv_b_out[j])
        x = x + g1 * rmsnorm(y, post_mix_g[i])
        h = rmsnorm(x, pre_ffn_g[i]) * (1.0 + sc2) + sh2
        y = conv_ffn(h, ffn_w_up[i], ffn_dw_w[i], ffn_dw_b[i], ffn_w_down[i])
        x = x + g2 * rmsnorm(y, post_ffn_g[i])
    return x


import jax as _jax
import jax.numpy as _jnp

TWIN_FORMAT = 'train_step'
FWD_PARAMS = ['x', 'c', 'ada_w', 'ada_b', 'pre_mix_g', 'post_mix_g', 'pre_ffn_g', 'post_ffn_g', 'hgrn_w_in', 'hgrn_lb_logits', 'hgrn_gnorm_g', 'hgrn_w_out', 'conv_w_in', 'conv_b_in', 'conv_dw_w', 'conv_dw_b', 'conv_ln_g', 'conv_ln_b', 'conv_w_out', 'conv_b_out', 'ffn_w_up', 'ffn_dw_w', 'ffn_dw_b', 'ffn_w_down']
TWIN_WEIGHTS = ['ada_w', 'ada_b', 'pre_mix_g', 'post_mix_g', 'pre_ffn_g', 'post_ffn_g', 'hgrn_w_in', 'hgrn_lb_logits', 'hgrn_gnorm_g', 'hgrn_w_out', 'conv_w_in', 'conv_b_in', 'conv_dw_w', 'conv_dw_b', 'conv_ln_g', 'conv_ln_b', 'conv_w_out', 'conv_b_out', 'ffn_w_up', 'ffn_dw_w', 'ffn_dw_b', 'ffn_w_down']
TWIN_DIFF_INPUT = 'x'
TWIN_INPUTS = ['x', 'c', 'ada_w', 'ada_b', 'pre_mix_g', 'post_mix_g', 'pre_ffn_g', 'post_ffn_g', 'hgrn_w_in', 'hgrn_lb_logits', 'hgrn_gnorm_g', 'hgrn_w_out', 'conv_w_in', 'conv_b_in', 'conv_dw_w', 'conv_dw_b', 'conv_ln_g', 'conv_ln_b', 'conv_w_out', 'conv_b_out', 'ffn_w_up', 'ffn_dw_w', 'ffn_dw_b', 'ffn_w_down', 'loss_target', 'm_ada_w', 'm_ada_b', 'm_pre_mix_g', 'm_post_mix_g', 'm_pre_ffn_g', 'm_post_ffn_g', 'm_hgrn_w_in', 'm_hgrn_lb_logits', 'm_hgrn_gnorm_g', 'm_hgrn_w_out', 'm_conv_w_in', 'm_conv_b_in', 'm_conv_dw_w', 'm_conv_dw_b', 'm_conv_ln_g', 'm_conv_ln_b', 'm_conv_w_out', 'm_conv_b_out', 'm_ffn_w_up', 'm_ffn_dw_w', 'm_ffn_dw_b', 'm_ffn_w_down', 'v_ada_w', 'v_ada_b', 'v_pre_mix_g', 'v_post_mix_g', 'v_pre_ffn_g', 'v_post_ffn_g', 'v_hgrn_w_in', 'v_hgrn_lb_logits', 'v_hgrn_gnorm_g', 'v_hgrn_w_out', 'v_conv_w_in', 'v_conv_b_in', 'v_conv_dw_w', 'v_conv_dw_b', 'v_conv_ln_g', 'v_conv_ln_b', 'v_conv_w_out', 'v_conv_b_out', 'v_ffn_w_up', 'v_ffn_dw_w', 'v_ffn_dw_b', 'v_ffn_w_down']
TWIN_OUTPUTS = ['loss', 'grad_x', 'grad_ada_w', 'grad_ada_b', 'grad_pre_mix_g', 'grad_post_mix_g', 'grad_pre_ffn_g', 'grad_post_ffn_g', 'grad_hgrn_w_in', 'grad_hgrn_lb_logits', 'grad_hgrn_gnorm_g', 'grad_hgrn_w_out', 'grad_conv_w_in', 'grad_conv_b_in', 'grad_conv_dw_w', 'grad_conv_dw_b', 'grad_conv_ln_g', 'grad_conv_ln_b', 'grad_conv_w_out', 'grad_conv_b_out', 'grad_ffn_w_up', 'grad_ffn_dw_w', 'grad_ffn_dw_b', 'grad_ffn_w_down', 'delta_ada_w', 'delta_ada_b', 'delta_pre_mix_g', 'delta_post_mix_g', 'delta_pre_ffn_g', 'delta_post_ffn_g', 'delta_hgrn_w_in', 'delta_hgrn_lb_logits', 'delta_hgrn_gnorm_g', 'delta_hgrn_w_out', 'delta_conv_w_in', 'delta_conv_b_in', 'delta_conv_dw_w', 'delta_conv_dw_b', 'delta_conv_ln_g', 'delta_conv_ln_b', 'delta_conv_w_out', 'delta_conv_b_out', 'delta_ffn_w_up', 'delta_ffn_dw_w', 'delta_ffn_dw_b', 'delta_ffn_w_down', 'new_m_ada_w', 'new_m_ada_b', 'new_m_pre_mix_g', 'new_m_post_mix_g', 'new_m_pre_ffn_g', 'new_m_post_ffn_g', 'new_m_hgrn_w_in', 'new_m_hgrn_lb_logits', 'new_m_hgrn_gnorm_g', 'new_m_hgrn_w_out', 'new_m_conv_w_in', 'new_m_conv_b_in', 'new_m_conv_dw_w', 'new_m_conv_dw_b', 'new_m_conv_ln_g', 'new_m_conv_ln_b', 'new_m_conv_w_out', 'new_m_conv_b_out', 'new_m_ffn_w_up', 'new_m_ffn_dw_w', 'new_m_ffn_dw_b', 'new_m_ffn_w_down', 'new_v_ada_w', 'new_v_ada_b', 'new_v_pre_mix_g', 'new_v_post_mix_g', 'new_v_pre_ffn_g', 'new_v_post_ffn_g', 'new_v_hgrn_w_in', 'new_v_hgrn_lb_logits', 'new_v_hgrn_gnorm_g', 'new_v_hgrn_w_out', 'new_v_conv_w_in', 'new_v_conv_b_in', 'new_v_conv_dw_w', 'new_v_conv_dw_b', 'new_v_conv_ln_g', 'new_v_conv_ln_b', 'new_v_conv_w_out', 'new_v_conv_b_out', 'new_v_ffn_w_up', 'new_v_ffn_dw_w', 'new_v_ffn_dw_b', 'new_v_ffn_w_down']
TWIN_LEAF_KINDS = {'loss': 'loss', 'grad_x': 'grad_x', 'grad_ada_w': 'grad_w', 'grad_ada_b': 'grad_w', 'grad_pre_mix_g': 'grad_w', 'grad_post_mix_g': 'grad_w', 'grad_pre_ffn_g': 'grad_w', 'grad_post_ffn_g': 'grad_w', 'grad_hgrn_w_in': 'grad_w', 'grad_hgrn_lb_logits': 'grad_w', 'grad_hgrn_gnorm_g': 'grad_w', 'grad_hgrn_w_out': 'grad_w', 'grad_conv_w_in': 'grad_w', 'grad_conv_b_in': 'grad_w', 'grad_conv_dw_w': 'grad_w', 'grad_conv_dw_b': 'grad_w', 'grad_conv_ln_g': 'grad_w', 'grad_conv_ln_b': 'grad_w', 'grad_conv_w_out': 'grad_w', 'grad_conv_b_out': 'grad_w', 'grad_ffn_w_up': 'grad_w', 'grad_ffn_dw_w': 'grad_w', 'grad_ffn_dw_b': 'grad_w', 'grad_ffn_w_down': 'grad_w', 'delta_ada_w': 'delta_w', 'delta_ada_b': 'delta_w', 'delta_pre_mix_g': 'delta_w', 'delta_post_mix_g': 'delta_w', 'delta_pre_ffn_g': 'delta_w', 'delta_post_ffn_g': 'delta_w', 'delta_hgrn_w_in': 'delta_w', 'delta_hgrn_lb_logits': 'delta_w', 'delta_hgrn_gnorm_g': 'delta_w', 'delta_hgrn_w_out': 'delta_w', 'delta_conv_w_in': 'delta_w', 'delta_conv_b_in': 'delta_w', 'delta_conv_dw_w': 'delta_w', 'delta_conv_dw_b': 'delta_w', 'delta_conv_ln_g': 'delta_w', 'delta_conv_ln_b': 'delta_w', 'delta_conv_w_out': 'delta_w', 'delta_conv_b_out': 'delta_w', 'delta_ffn_w_up': 'delta_w', 'delta_ffn_dw_w': 'delta_w', 'delta_ffn_dw_b': 'delta_w', 'delta_ffn_w_down': 'delta_w', 'new_m_ada_w': 'new_m', 'new_m_ada_b': 'new_m', 'new_m_pre_mix_g': 'new_m', 'new_m_post_mix_g': 'new_m', 'new_m_pre_ffn_g': 'new_m', 'new_m_post_ffn_g': 'new_m', 'new_m_hgrn_w_in': 'new_m', 'new_m_hgrn_lb_logits': 'new_m', 'new_m_hgrn_gnorm_g': 'new_m', 'new_m_hgrn_w_out': 'new_m', 'new_m_conv_w_in': 'new_m', 'new_m_conv_b_in': 'new_m', 'new_m_conv_dw_w': 'new_m', 'new_m_conv_dw_b': 'new_m', 'new_m_conv_ln_g': 'new_m', 'new_m_conv_ln_b': 'new_m', 'new_m_conv_w_out': 'new_m', 'new_m_conv_b_out': 'new_m', 'new_m_ffn_w_up': 'new_m', 'new_m_ffn_dw_w': 'new_m', 'new_m_ffn_dw_b': 'new_m', 'new_m_ffn_w_down': 'new_m', 'new_v_ada_w': 'new_v', 'new_v_ada_b': 'new_v', 'new_v_pre_mix_g': 'new_v', 'new_v_post_mix_g': 'new_v', 'new_v_pre_ffn_g': 'new_v', 'new_v_post_ffn_g': 'new_v', 'new_v_hgrn_w_in': 'new_v', 'new_v_hgrn_lb_logits': 'new_v', 'new_v_hgrn_gnorm_g': 'new_v', 'new_v_hgrn_w_out': 'new_v', 'new_v_conv_w_in': 'new_v', 'new_v_conv_b_in': 'new_v', 'new_v_conv_dw_w': 'new_v', 'new_v_conv_dw_b': 'new_v', 'new_v_conv_ln_g': 'new_v', 'new_v_conv_ln_b': 'new_v', 'new_v_conv_w_out': 'new_v', 'new_v_conv_b_out': 'new_v', 'new_v_ffn_w_up': 'new_v', 'new_v_ffn_dw_w': 'new_v', 'new_v_ffn_dw_b': 'new_v', 'new_v_ffn_w_down': 'new_v'}


def _forward(args):
    return _fwd_reference(*[args[k] for k in FWD_PARAMS])


def _output_shape():
    def fwd():
        inp = _fwd_setup_inputs(0)
        return _fwd_reference(*[inp[k] for k in FWD_PARAMS])
    out = _jax.eval_shape(fwd)
    return out.shape, out.dtype

N_MICROBATCH = 1
ADAM_LR = 0.001
ADAM_B1 = 0.9
ADAM_B2 = 0.999
ADAM_EPS = 1e-08
ADAM_WD = 0.01
ADAM_STEP = 10
PER_EXAMPLE_BATCH_AXIS = {'x': 0, 'c': 0, 'loss_target': 0}
SHARED_INPUTS = []
_WEIGHT_DTYPES = {'ada_w': _jnp.float32, 'ada_b': _jnp.float32, 'pre_mix_g': _jnp.float32, 'post_mix_g': _jnp.float32, 'pre_ffn_g': _jnp.float32, 'post_ffn_g': _jnp.float32, 'hgrn_w_in': _jnp.float32, 'hgrn_lb_logits': _jnp.float32, 'hgrn_gnorm_g': _jnp.float32, 'hgrn_w_out': _jnp.float32, 'conv_w_in': _jnp.float32, 'conv_b_in': _jnp.float32, 'conv_dw_w': _jnp.float32, 'conv_dw_b': _jnp.float32, 'conv_ln_g': _jnp.float32, 'conv_ln_b': _jnp.float32, 'conv_w_out': _jnp.float32, 'conv_b_out': _jnp.float32, 'ffn_w_up': _jnp.float32, 'ffn_dw_w': _jnp.float32, 'ffn_dw_b': _jnp.float32, 'ffn_w_down': _jnp.float32}
MOMENT_SCALE = {'ada_w': 2.435662e+00, 'ada_b': 5.344472e+00, 'pre_mix_g': 1.662613e-01, 'post_mix_g': 6.701304e+00, 'pre_ffn_g': 1.762840e-01, 'post_ffn_g': 6.583096e+00, 'hgrn_w_in': 1.524265e-01, 'hgrn_lb_logits': 5.998458e-03, 'hgrn_gnorm_g': 5.812133e-01, 'hgrn_w_out': 2.550532e-01, 'conv_w_in': 1.688281e-01, 'conv_b_in': 6.450515e-01, 'conv_dw_w': 2.571855e-01, 'conv_dw_b': 1.579939e+00, 'conv_ln_g': 6.703760e-01, 'conv_ln_b': 1.019216e+00, 'conv_w_out': 4.205870e-01, 'conv_b_out': 1.996247e+00, 'ffn_w_up': 9.264977e-02, 'ffn_dw_w': 9.741245e-02, 'ffn_dw_b': 1.869318e-01, 'ffn_w_down': 1.773880e-01}


def _to_microbatches(a, axis):
    t = _jnp.moveaxis(a, axis, 0)
    t = t.reshape((N_MICROBATCH, t.shape[0] // N_MICROBATCH) + t.shape[1:])
    return _jnp.moveaxis(t, 1, axis + 1)


def setup_inputs(seed: int = 0) -> dict:
    inp = _fwd_setup_inputs(seed)
    key = _jax.random.fold_in(_jax.random.key(seed), 7919)
    shape, _ = _output_shape()
    out = dict(inp)
    out["loss_target"] = _jax.random.normal(_jax.random.fold_in(key, 0), shape, _jnp.float32)
    for i, name in enumerate(TWIN_WEIGHTS):
        w = inp[name].astype(_jnp.float32)
        if MOMENT_SCALE is None:
            s = _jnp.sqrt(_jnp.mean(_jnp.square(w)) + 1e-30)
        else:
            s = MOMENT_SCALE[name]
        km, kv = _jax.random.split(_jax.random.fold_in(key, i + 1))
        out[name] = w
        out["m_" + name] = s * _jax.random.normal(km, w.shape, _jnp.float32)
        out["v_" + name] = (s * s) * _jax.random.uniform(kv, w.shape, _jnp.float32, 0.5, 1.5)
    if N_MICROBATCH > 1:
        for name, axis in PER_EXAMPLE_BATCH_AXIS.items():
            out[name] = _to_microbatches(out[name], axis)
    return {'x': out['x'], 'c': out['c'], 'ada_w': out['ada_w'], 'ada_b': out['ada_b'], 'pre_mix_g': out['pre_mix_g'], 'post_mix_g': out['post_mix_g'], 'pre_ffn_g': out['pre_ffn_g'], 'post_ffn_g': out['post_ffn_g'], 'hgrn_w_in': out['hgrn_w_in'], 'hgrn_lb_logits': out['hgrn_lb_logits'], 'hgrn_gnorm_g': out['hgrn_gnorm_g'], 'hgrn_w_out': out['hgrn_w_out'], 'conv_w_in': out['conv_w_in'], 'conv_b_in': out['conv_b_in'], 'conv_dw_w': out['conv_dw_w'], 'conv_dw_b': out['conv_dw_b'], 'conv_ln_g': out['conv_ln_g'], 'conv_ln_b': out['conv_ln_b'], 'conv_w_out': out['conv_w_out'], 'conv_b_out': out['conv_b_out'], 'ffn_w_up': out['ffn_w_up'], 'ffn_dw_w': out['ffn_dw_w'], 'ffn_dw_b': out['ffn_dw_b'], 'ffn_w_down': out['ffn_w_down'], 'loss_target': out['loss_target'], 'm_ada_w': out['m_ada_w'], 'm_ada_b': out['m_ada_b'], 'm_pre_mix_g': out['m_pre_mix_g'], 'm_post_mix_g': out['m_post_mix_g'], 'm_pre_ffn_g': out['m_pre_ffn_g'], 'm_post_ffn_g': out['m_post_ffn_g'], 'm_hgrn_w_in': out['m_hgrn_w_in'], 'm_hgrn_lb_logits': out['m_hgrn_lb_logits'], 'm_hgrn_gnorm_g': out['m_hgrn_gnorm_g'], 'm_hgrn_w_out': out['m_hgrn_w_out'], 'm_conv_w_in': out['m_conv_w_in'], 'm_conv_b_in': out['m_conv_b_in'], 'm_conv_dw_w': out['m_conv_dw_w'], 'm_conv_dw_b': out['m_conv_dw_b'], 'm_conv_ln_g': out['m_conv_ln_g'], 'm_conv_ln_b': out['m_conv_ln_b'], 'm_conv_w_out': out['m_conv_w_out'], 'm_conv_b_out': out['m_conv_b_out'], 'm_ffn_w_up': out['m_ffn_w_up'], 'm_ffn_dw_w': out['m_ffn_dw_w'], 'm_ffn_dw_b': out['m_ffn_dw_b'], 'm_ffn_w_down': out['m_ffn_w_down'], 'v_ada_w': out['v_ada_w'], 'v_ada_b': out['v_ada_b'], 'v_pre_mix_g': out['v_pre_mix_g'], 'v_post_mix_g': out['v_post_mix_g'], 'v_pre_ffn_g': out['v_pre_ffn_g'], 'v_post_ffn_g': out['v_post_ffn_g'], 'v_hgrn_w_in': out['v_hgrn_w_in'], 'v_hgrn_lb_logits': out['v_hgrn_lb_logits'], 'v_hgrn_gnorm_g': out['v_hgrn_gnorm_g'], 'v_hgrn_w_out': out['v_hgrn_w_out'], 'v_conv_w_in': out['v_conv_w_in'], 'v_conv_b_in': out['v_conv_b_in'], 'v_conv_dw_w': out['v_conv_dw_w'], 'v_conv_dw_b': out['v_conv_dw_b'], 'v_conv_ln_g': out['v_conv_ln_g'], 'v_conv_ln_b': out['v_conv_ln_b'], 'v_conv_w_out': out['v_conv_w_out'], 'v_conv_b_out': out['v_conv_b_out'], 'v_ffn_w_up': out['v_ffn_w_up'], 'v_ffn_dw_w': out['v_ffn_dw_w'], 'v_ffn_dw_b': out['v_ffn_dw_b'], 'v_ffn_w_down': out['v_ffn_w_down']}


def _loss(weights, diff, rest, loss_target):
    with _jax.named_scope("forward"):
        args = {**rest, TWIN_DIFF_INPUT: diff, **{k: w.astype(_WEIGHT_DTYPES[k]) for k, w in weights.items()}}
        y = _forward(args)
    with _jax.named_scope("loss_head"):
        err = _jnp.square(y.astype(_jnp.float32) - loss_target)
        return 0.5 * _jnp.sum(_jnp.mean(err, axis=-1)) if err.ndim else 0.5 * err


def _adamw(w, g, m, v):
    m = ADAM_B1 * m + (1.0 - ADAM_B1) * g
    v = ADAM_B2 * v + (1.0 - ADAM_B2) * _jnp.square(g)
    m_hat = m / (1.0 - ADAM_B1 ** ADAM_STEP)
    v_hat = v / (1.0 - ADAM_B2 ** ADAM_STEP)
    delta = -ADAM_LR * (m_hat / (_jnp.sqrt(v_hat) + ADAM_EPS) + ADAM_WD * w)
    return delta, m, v


def reference(x, c, ada_w, ada_b, pre_mix_g, post_mix_g, pre_ffn_g, post_ffn_g, hgrn_w_in, hgrn_lb_logits, hgrn_gnorm_g, hgrn_w_out, conv_w_in, conv_b_in, conv_dw_w, conv_dw_b, conv_ln_g, conv_ln_b, conv_w_out, conv_b_out, ffn_w_up, ffn_dw_w, ffn_dw_b, ffn_w_down, loss_target, m_ada_w, m_ada_b, m_pre_mix_g, m_post_mix_g, m_pre_ffn_g, m_post_ffn_g, m_hgrn_w_in, m_hgrn_lb_logits, m_hgrn_gnorm_g, m_hgrn_w_out, m_conv_w_in, m_conv_b_in, m_conv_dw_w, m_conv_dw_b, m_conv_ln_g, m_conv_ln_b, m_conv_w_out, m_conv_b_out, m_ffn_w_up, m_ffn_dw_w, m_ffn_dw_b, m_ffn_w_down, v_ada_w, v_ada_b, v_pre_mix_g, v_post_mix_g, v_pre_ffn_g, v_post_ffn_g, v_hgrn_w_in, v_hgrn_lb_logits, v_hgrn_gnorm_g, v_hgrn_w_out, v_conv_w_in, v_conv_b_in, v_conv_dw_w, v_conv_dw_b, v_conv_ln_g, v_conv_ln_b, v_conv_w_out, v_conv_b_out, v_ffn_w_up, v_ffn_dw_w, v_ffn_dw_b, v_ffn_w_down):
    given = dict(x=x, c=c, ada_w=ada_w, ada_b=ada_b, pre_mix_g=pre_mix_g, post_mix_g=post_mix_g, pre_ffn_g=pre_ffn_g, post_ffn_g=post_ffn_g, hgrn_w_in=hgrn_w_in, hgrn_lb_logits=hgrn_lb_logits, hgrn_gnorm_g=hgrn_gnorm_g, hgrn_w_out=hgrn_w_out, conv_w_in=conv_w_in, conv_b_in=conv_b_in, conv_dw_w=conv_dw_w, conv_dw_b=conv_dw_b, conv_ln_g=conv_ln_g, conv_ln_b=conv_ln_b, conv_w_out=conv_w_out, conv_b_out=conv_b_out, ffn_w_up=ffn_w_up, ffn_dw_w=ffn_dw_w, ffn_dw_b=ffn_dw_b, ffn_w_down=ffn_w_down, loss_target=loss_target, m_ada_w=m_ada_w, m_ada_b=m_ada_b, m_pre_mix_g=m_pre_mix_g, m_post_mix_g=m_post_mix_g, m_pre_ffn_g=m_pre_ffn_g, m_post_ffn_g=m_post_ffn_g, m_hgrn_w_in=m_hgrn_w_in, m_hgrn_lb_logits=m_hgrn_lb_logits, m_hgrn_gnorm_g=m_hgrn_gnorm_g, m_hgrn_w_out=m_hgrn_w_out, m_conv_w_in=m_conv_w_in, m_conv_b_in=m_conv_b_in, m_conv_dw_w=m_conv_dw_w, m_conv_dw_b=m_conv_dw_b, m_conv_ln_g=m_conv_ln_g, m_conv_ln_b=m_conv_ln_b, m_conv_w_out=m_conv_w_out, m_conv_b_out=m_conv_b_out, m_ffn_w_up=m_ffn_w_up, m_ffn_dw_w=m_ffn_dw_w, m_ffn_dw_b=m_ffn_dw_b, m_ffn_w_down=m_ffn_w_down, v_ada_w=v_ada_w, v_ada_b=v_ada_b, v_pre_mix_g=v_pre_mix_g, v_post_mix_g=v_post_mix_g, v_pre_ffn_g=v_pre_ffn_g, v_post_ffn_g=v_post_ffn_g, v_hgrn_w_in=v_hgrn_w_in, v_hgrn_lb_logits=v_hgrn_lb_logits, v_hgrn_gnorm_g=v_hgrn_gnorm_g, v_hgrn_w_out=v_hgrn_w_out, v_conv_w_in=v_conv_w_in, v_conv_b_in=v_conv_b_in, v_conv_dw_w=v_conv_dw_w, v_conv_dw_b=v_conv_dw_b, v_conv_ln_g=v_conv_ln_g, v_conv_ln_b=v_conv_ln_b, v_conv_w_out=v_conv_w_out, v_conv_b_out=v_conv_b_out, v_ffn_w_up=v_ffn_w_up, v_ffn_dw_w=v_ffn_dw_w, v_ffn_dw_b=v_ffn_dw_b, v_ffn_w_down=v_ffn_w_down)
    weights = {n: given[n] for n in TWIN_WEIGHTS}
    shared = {n: given[n] for n in SHARED_INPUTS}
    per_example = {n: given[n] for n in ['x', 'c']}
    grad_fn = _jax.value_and_grad(_loss, argnums=(0, 1))

    def one_microbatch(ex, loss_target):
        ex = dict(ex)
        diff = ex.pop(TWIN_DIFF_INPUT)
        return grad_fn(weights, diff, {**shared, **ex}, loss_target)

    if N_MICROBATCH == 1:
        loss, (grad_w, grad_x) = one_microbatch(per_example, given["loss_target"])
    else:
        def body(carry, xs):
            loss_sum, grad_sum = carry
            l_k, (gw_k, gx_k) = one_microbatch(xs[0], xs[1])
            with _jax.named_scope("update"):
                return (loss_sum + l_k, _jax.tree.map(_jnp.add, grad_sum, gw_k)), gx_k

        init = (_jnp.zeros((), _jnp.float32), _jax.tree.map(_jnp.zeros_like, weights))
        (loss, grad_w), grad_x = _jax.lax.scan(body, init, (per_example, given["loss_target"]))
    with _jax.named_scope("update"):
        delta_w, new_m, new_v = {}, {}, {}
        for n in TWIN_WEIGHTS:
            delta_w[n], new_m[n], new_v[n] = _adamw(weights[n], grad_w[n], given["m_" + n], given["v_" + n])
    return (loss, grad_x, *[grad_w[n] for n in TWIN_WEIGHTS], *[delta_w[n] for n in TWIN_WEIGHTS],
            *[new_m[n] for n in TWIN_WEIGHTS], *[new_v[n] for n in TWIN_WEIGHTS])
```

```python
import functools

import jax
import jax.numpy as jnp
from jax import lax
from jax.experimental import pallas as pl
from jax.experimental.pallas import tpu as pltpu

F32 = jnp.float32
BF16 = jnp.bfloat16
EPS = 1e-6
LANES = 128
N_DEV = 8
VMEM_LIMIT = 48 * 1024 * 1024
HG_CHUNK = 128
HG_SUB = 16
EXP_CLAMP = 80.0
CONV_HALO = 32
FFN_HALO = 8
CONV_ROWS = 32
ADAM_LR, ADAM_B1, ADAM_B2, ADAM_EPS, ADAM_WD, ADAM_STEP = 0.001, 0.9, 0.999, 1e-08, 0.01, 10
MESH = pl.DeviceIdType.MESH
ANY = pl.BlockSpec(memory_space=pl.ANY)


def _cp(*sem):
    return pltpu.CompilerParams(dimension_semantics=sem, vmem_limit_bytes=VMEM_LIMIT)


def _rt(n, t):
    t = min(n, t)
    assert n % t == 0, (n, t)
    return t


def _sig(x):
    return jax.nn.sigmoid(x)


def _nt(a, b):
    return lax.dot_general(a, b, (((1,), (1,)), ((), ())), preferred_element_type=F32)


def _tn(a, b):
    return lax.dot_general(a, b, (((0,), (0,)), ((), ())), preferred_element_type=F32)


def _nn(a, b):
    return jnp.dot(a, b, preferred_element_type=F32)


def _bf(x):
    return x.astype(BF16)


_HI = lax.Precision.HIGHEST


def _nn_hi(a, b):
    return jnp.dot(a, b, preferred_element_type=F32, precision=_HI)


def _nt_hi(a, b):
    return lax.dot_general(a, b, (((1,), (1,)), ((), ())), preferred_element_type=F32, precision=_HI)


def _tn_hi(a, b):
    return lax.dot_general(a, b, (((0,), (0,)), ((), ())), preferred_element_type=F32, precision=_HI)


def _row(d):
    return pl.BlockSpec((1, d), lambda *_: (0, 0))


def prenorm_mod(x, g, sc, sh, name):
    T, D = x.shape
    tm = _rt(T, 512)

    def body(x_ref, g_ref, sc_ref, sh_ref, h_ref):
        xv = x_ref[...]
        r = lax.rsqrt(jnp.mean(xv * xv, axis=-1, keepdims=True) + EPS)
        h_ref[...] = _bf((xv * r) * g_ref[...] * (1.0 + sc_ref[...]) + sh_ref[...])

    tile = pl.BlockSpec((tm, D), lambda i: (i, 0))
    return pl.pallas_call(
        body, name=name, grid=(T // tm,), in_specs=[tile, _row(D), _row(D), _row(D)], out_specs=tile,
        out_shape=jax.ShapeDtypeStruct((T, D), BF16), compiler_params=_cp("parallel"))(x, g, sc, sh)


def prenorm_bwd(dh, x, dxres, g, sc, name):
    T, D = x.shape
    tm = _rt(T, 512)

    def body(dh_ref, x_ref, dr_ref, g_ref, sc_ref, dx_ref, da_ref, dsh_ref):
        @pl.when(pl.program_id(0) == 0)
        def _():
            da_ref[...] = jnp.zeros_like(da_ref)
            dsh_ref[...] = jnp.zeros_like(dsh_ref)
        xv = x_ref[...]
        dh = dh_ref[...]
        r = lax.rsqrt(jnp.mean(xv * xv, axis=-1, keepdims=True) + EPS)
        n = xv * r
        da_ref[...] += jnp.sum(dh * n, axis=0, keepdims=True)
        dsh_ref[...] += jnp.sum(dh, axis=0, keepdims=True)
        dn = dh * (g_ref[...] * (1.0 + sc_ref[...]))
        dx_ref[...] = dr_ref[...] + r * (dn - n * jnp.mean(dn * n, axis=-1, keepdims=True))

    tile = pl.BlockSpec((tm, D), lambda i: (i, 0))
    vec = jax.ShapeDtypeStruct((1, D), F32)
    return pl.pallas_call(
        body, name=name, grid=(T // tm,), in_specs=[tile, tile, tile, _row(D), _row(D)],
        out_specs=[tile, _row(D), _row(D)], out_shape=[jax.ShapeDtypeStruct((T, D), F32), vec, vec],
        compiler_params=_cp("arbitrary"))(dh, x, dxres, g, sc)


def postnorm_bwd(dxo, y, gate, pg, name):
    T, D = y.shape
    tm = _rt(T, 512)

    def body(d_ref, y_ref, gt_ref, pg_ref, dy_ref, dgate_ref, dpg_ref, dys_ref):
        @pl.when(pl.program_id(0) == 0)
        def _():
            dgate_ref[...] = jnp.zeros_like(dgate_ref)
            dpg_ref[...] = jnp.zeros_like(dpg_ref)
            dys_ref[...] = jnp.zeros_like(dys_ref)
        yv = y_ref[...]
        d = d_ref[...]
        r = lax.rsqrt(jnp.mean(yv * yv, axis=-1, keepdims=True) + EPS)
        n = yv * r
        dgate_ref[...] += jnp.sum(d * n, axis=0, keepdims=True) * pg_ref[...]
        dpg_ref[...] += jnp.sum(d * n, axis=0, keepdims=True) * gt_ref[...]
        dn = d * (gt_ref[...] * pg_ref[...])
        dy = r * (dn - n * jnp.mean(dn * n, axis=-1, keepdims=True))
        dy_ref[...] = dy
        dys_ref[...] += jnp.sum(dy, axis=0, keepdims=True)

    tile = pl.BlockSpec((tm, D), lambda i: (i, 0))
    vec = jax.ShapeDtypeStruct((1, D), F32)
    return pl.pallas_call(
        body, name=name, grid=(T // tm,), in_specs=[tile, tile, _row(D), _row(D)],
        out_specs=[tile, _row(D), _row(D), _row(D)], out_shape=[jax.ShapeDtypeStruct((T, D), F32), vec, vec, vec],
        compiler_params=_cp("arbitrary"))(dxo, y, gate, pg)


def loss_grad(xo, tgt, name):
    T, D = xo.shape
    tm = _rt(T, 512)

    def body(x_ref, t_ref, dx_ref, part_ref):
        @pl.when(pl.program_id(0) == 0)
        def _():
            part_ref[...] = jnp.zeros_like(part_ref)
        e = x_ref[...] - t_ref[...]
        dx_ref[...] = e * (1.0 / D)
        e2 = (e * e).reshape(tm // 8, 8, D).sum(axis=0)
        acc = e2[:, 0:LANES]
        for j in range(1, D // LANES):
            acc = acc + e2[:, j * LANES:(j + 1) * LANES]
        part_ref[...] += acc

    tile = pl.BlockSpec((tm, D), lambda i: (i, 0))
    return pl.pallas_call(
        body, name=name, grid=(T // tm,), in_specs=[tile, tile],
        out_specs=[tile, pl.BlockSpec((8, LANES), lambda i: (0, 0))],
        out_shape=[jax.ShapeDtypeStruct((T, D), F32), jax.ShapeDtypeStruct((8, LANES), F32)],
        compiler_params=_cp("arbitrary"))(xo, tgt)


def mm_nn(a, bg, bias, name):
    M, K = a.shape
    G, _, n = bg.shape
    tm = _rt(M, 512)

    def body(a_ref, b_ref, bias_ref, o_ref):
        o_ref[...] = _nn(a_ref[...], b_ref[...]) + bias_ref[...]

    return pl.pallas_call(
        body, name=name, grid=(M // tm, G),
        in_specs=[pl.BlockSpec((tm, K), lambda i, j: (i, 0)), pl.BlockSpec((None, K, n), lambda i, j: (j, 0, 0)),
                  pl.BlockSpec((1, n), lambda i, j: (0, j))],
        out_specs=pl.BlockSpec((tm, n), lambda i, j: (i, j)),
        out_shape=jax.ShapeDtypeStruct((M, G * n), F32), compiler_params=_cp("parallel", "arbitrary"))(a, bg, bias)


def mm_post(a, b, bias, x, gate, pg, name):
    T, K = a.shape
    D = b.shape[1]
    tm = _rt(T, 512)

    def body(a_ref, b_ref, bias_ref, x_ref, gt_ref, pg_ref, y_ref, xo_ref):
        y = _nn(a_ref[...], b_ref[...]) + bias_ref[...]
        y_ref[...] = y
        r = lax.rsqrt(jnp.mean(y * y, axis=-1, keepdims=True) + EPS)
        xo_ref[...] = x_ref[...] + gt_ref[...] * ((y * r) * pg_ref[...])

    tile = pl.BlockSpec((tm, D), lambda i: (i, 0))
    out = jax.ShapeDtypeStruct((T, D), F32)
    return pl.pallas_call(
        body, name=name, grid=(T // tm,),
        in_specs=[pl.BlockSpec((tm, K), lambda i: (i, 0)), pl.BlockSpec((K, D), lambda i: (0, 0)), _row(D), tile,
                  _row(D), _row(D)],
        out_specs=[tile, tile], out_shape=[out, out], compiler_params=_cp("parallel"))(a, b, bias, x, gate, pg)


def mm_nt(dy, wg, name, natural):
    G, K, n = wg.shape
    T = dy.shape[0] if natural else dy.shape[1]
    tm = _rt(T, 512)

    def body(dy_ref, w_ref, o_ref, acc):
        j = pl.program_id(1)

        @pl.when(j == 0)
        def _():
            acc[...] = jnp.zeros_like(acc)
        acc[...] += _nt(_bf(dy_ref[...]), w_ref[...])

        @pl.when(j == G - 1)
        def _():
            o_ref[...] = acc[...]

    dspec = (pl.BlockSpec((tm, n), lambda i, j: (i, j)) if natural
             else pl.BlockSpec((None, tm, n), lambda i, j: (j, i, 0)))
    return pl.pallas_call(
        body, name=name, grid=(T // tm, G),
        in_specs=[dspec, pl.BlockSpec((None, K, n), lambda i, j: (j, 0, 0))],
        out_specs=pl.BlockSpec((tm, K), lambda i, j: (i, 0)), out_shape=jax.ShapeDtypeStruct((T, K), F32),
        scratch_shapes=[pltpu.VMEM((tm, K), F32)], compiler_params=_cp("parallel", "arbitrary"))(dy, wg)


def mm_tn(xm, gm, name, x_natural, g_natural, kx, n):
    if x_natural:
        T, Gx = xm.shape[0], xm.shape[1] // kx
    else:
        Gx, T = xm.shape[0], xm.shape[1]
    tt = _rt(T, 512)
    if x_natural:
        xspec = pl.BlockSpec((tt, kx), lambda g, k, t: (t, k))
    else:
        xspec = pl.BlockSpec((None, tt, kx), lambda g, k, t: (k, t, 0))
    if g_natural:
        Gg = gm.shape[1] // n
        gspec = pl.BlockSpec((tt, n), lambda g, k, t: (t, g))
    else:
        Gg = gm.shape[0]
        gspec = pl.BlockSpec((None, tt, n), lambda g, k, t: (g, t, 0))

    def body(x_ref, g_ref, o_ref):
        @pl.when(pl.program_id(2) == 0)
        def _():
            o_ref[...] = jnp.zeros_like(o_ref)
        o_ref[...] += _tn(_bf(x_ref[...]), _bf(g_ref[...]))

    return pl.pallas_call(
        body, name=name, grid=(Gg, Gx, T // tt), in_specs=[xspec, gspec],
        out_specs=pl.BlockSpec((None, kx, n), lambda g, k, t: (g, k, 0)),
        out_shape=jax.ShapeDtypeStruct((Gg, Gx * kx, n), F32),
        compiler_params=_cp("parallel", "parallel", "arbitrary"))(xm, gm)


def _split3(x):
    h = _bf(x)
    r = x - h.astype(F32)
    m = _bf(r)
    lo = _bf(r - m.astype(F32))
    return h, m, lo


def _tri_mm(tri, x):
    h, m, lo = _split3(x)
    return _nn(tri, lo) + _nn(tri, m) + _nn(tri, h)


def _hgrn_gates(qr, fz, lb):
    sq = _sig(qr)
    q = qr * sq
    sig = _sig(fz)
    f = lb + (1.0 - lb) * sig
    k = 1.0 - f
    return sq, q, sig, f, k, jnp.log(f)


def hgrn_fwd(proj, lb, gg, name):
    T, D4 = proj.shape
    D = D4 // 4
    H = D // LANES
    C = _rt(T, HG_CHUNK)
    SB = min(HG_SUB, C)
    NC = T // C

    def body(q_ref, f_ref, v_ref, g_ref, lb_ref, gg_ref, o_ref, og_ref, st_ref, ST, bex):
        @pl.when(pl.program_id(0) == 0)
        def _():
            ST[...] = jnp.zeros_like(ST)
        r_i = lax.broadcasted_iota(jnp.int32, (C, C), 0)
        c_i = lax.broadcasted_iota(jnp.int32, (C, C), 1)
        low = _bf((r_i >= c_i).astype(F32))
        rs = lax.broadcasted_iota(jnp.int32, (SB, C), 0)
        cs = lax.broadcasted_iota(jnp.int32, (SB, C), 1)
        for hd in range(H):
            sl = slice(hd * LANES, (hd + 1) * LANES)
            _, q, _, _, k, lf = _hgrn_gates(q_ref[:, sl], f_ref[:, sl], lb_ref[:, sl])
            v = v_ref[:, sl]
            b = _tri_mm(low, lf)
            bex[hd] = b - lf
            bend = jnp.sum(lf, axis=0, keepdims=True)
            blocks = []
            for I in range(C // SB):
                p = bex[hd, pl.ds(SB * I, 1), :]
                rows = slice(SB * I, SB * (I + 1))
                qI = _bf(q[rows] * jnp.exp(b[rows] - p))
                kI = _bf(k * jnp.exp(jnp.minimum(p - b, EXP_CLAMP)))
                blocks.append(jnp.where(rs + SB * I >= cs, _nt(qI, kI), 0.0))
            A = jnp.concatenate(blocks, axis=0)
            st0 = ST[hd]
            st_ref[0, hd] = st0
            o = _nn(_bf(A), _bf(v)) + _nt(_bf(q * jnp.exp(b)), _bf(st0))
            ST[hd] = st0 * jnp.exp(bend) + _tn_hi(v, k * jnp.exp(bend - b))
            o_ref[:, sl] = o
            r = lax.rsqrt(jnp.mean(o * o, axis=-1, keepdims=True) + EPS)
            gr = g_ref[:, sl]
            og_ref[:, sl] = _bf((o * r) * gg_ref[...] * (gr * _sig(gr)))

    col = lambda j: pl.BlockSpec((C, D), lambda c, j=j: (c, j))
    tile = pl.BlockSpec((C, D), lambda c: (c, 0))
    return pl.pallas_call(
        body, name=name, grid=(NC,),
        in_specs=[col(0), col(1), col(2), col(3), _row(D), _row(LANES)],
        out_specs=[tile, tile, pl.BlockSpec((1, H, LANES, LANES), lambda c: (c, 0, 0, 0))],
        out_shape=[jax.ShapeDtypeStruct((T, D), F32), jax.ShapeDtypeStruct((T, D), BF16),
                   jax.ShapeDtypeStruct((NC, H, LANES, LANES), F32)],
        scratch_shapes=[pltpu.VMEM((H, LANES, LANES), F32), pltpu.VMEM((H, C, LANES), F32)],
        compiler_params=_cp("arbitrary"))(proj, proj, proj, proj, lb, gg)


def hgrn_bwd(proj, o, dog, st, lb, gg, name):
    T, D4 = proj.shape
    D = D4 // 4
    H = D // LANES
    C = _rt(T, HG_CHUNK)
    SB = min(HG_SUB, C)
    NC = T // C

    def body(q_ref, f_ref, v_ref, g_ref, o_ref, dog_ref, st_ref, lb_ref, gg_ref, dp_ref, dlb_ref, dgg_ref,
             DST, RS, bex):
        @pl.when(pl.program_id(0) == 0)
        def _():
            DST[...] = jnp.zeros_like(DST)
            RS[...] = jnp.zeros_like(RS)
            dlb_ref[...] = jnp.zeros_like(dlb_ref)
            dgg_ref[...] = jnp.zeros_like(dgg_ref)
        r_i = lax.broadcasted_iota(jnp.int32, (C, C), 0)
        c_i = lax.broadcasted_iota(jnp.int32, (C, C), 1)
        causal = r_i >= c_i
        low = _bf(causal.astype(F32))
        upp = _bf((r_i <= c_i).astype(F32))
        rs = lax.broadcasted_iota(jnp.int32, (SB, C), 0)
        cs = lax.broadcasted_iota(jnp.int32, (SB, C), 1)
        rfull = lax.broadcasted_iota(jnp.int32, (C, LANES), 0)
        ggv = gg_ref[...]
        for hd in range(H):
            sl = slice(hd * LANES, (hd + 1) * LANES)
            qr = q_ref[:, sl]
            lbv = lb_ref[:, sl]
            sq, q, sig, f, k, lf = _hgrn_gates(qr, f_ref[:, sl], lbv)
            v = v_ref[:, sl]
            oh = o_ref[:, sl]
            r = lax.rsqrt(jnp.mean(oh * oh, axis=-1, keepdims=True) + EPS)
            n = oh * r
            gr = g_ref[:, sl]
            sg = _sig(gr)
            dgo = dog_ref[:, sl]
            dgr = dgo * (n * ggv) * (sg * (1.0 + gr * (1.0 - sg)))
            don = dgo * (gr * sg)
            dgg_ref[...] += jnp.sum(don * n, axis=0, keepdims=True)
            dn = don * ggv
            do = r * (dn - n * jnp.mean(dn * n, axis=-1, keepdims=True))
            b = _tri_mm(low, lf)
            bex[hd] = b - lf
            bend = jnp.sum(lf, axis=0, keepdims=True)
            dob = _bf(do)
            dA = jnp.where(causal, _nt_hi(do, v), 0.0)
            dAT = jnp.where(r_i <= c_i, _nt_hi(v, do), 0.0)
            blocks, dqs = [], []
            dk = jnp.zeros((C, LANES), F32)
            for I in range(C // SB):
                p = bex[hd, pl.ds(SB * I, 1), :]
                rows = slice(SB * I, SB * (I + 1))
                eq = jnp.exp(b[rows] - p)
                qI = q[rows] * eq
                ek = jnp.exp(jnp.minimum(p - b, EXP_CLAMP))
                kI = k * ek
                blocks.append(jnp.where(rs + SB * I >= cs, _nt(_bf(qI), _bf(kI)), 0.0))
                dqs.append(_nn_hi(dA[rows], kI) * eq)
                inblk = (rfull >= SB * I) & (rfull < SB * (I + 1))
                qfull = jnp.where(inblk, q * jnp.exp(jnp.minimum(b - p, 0.0)), 0.0)
                dk = dk + _nn_hi(dAT, qfull) * ek
            A = jnp.concatenate(blocks, axis=0)
            dst = DST[hd]
            eb = jnp.exp(b)
            ee = jnp.exp(bend - b)
            dv = _tn(_bf(A), dob) + _nt(_bf(k * ee), _bf(dst))
            dq = jnp.concatenate(dqs, axis=0) + eb * _nn_hi(do, st_ref[0, hd])
            dk = dk + ee * _nn_hi(v, dst)
            DST[hd] = dst * jnp.exp(bend) + _tn_hi(do, q * eb)
            gsum = q * dq - k * dk
            dlf = _tri_mm(upp, gsum) + RS[:, sl]
            RS[:, sl] += jnp.sum(gsum, axis=0, keepdims=True)
            df = dlf / f - dk
            dlb_ref[:, sl] += jnp.sum(df * (1.0 - sig), axis=0, keepdims=True)
            dp_ref[:, sl] = _bf(dq * (sq * (1.0 + qr * (1.0 - sq))))
            dp_ref[:, D + hd * LANES:D + (hd + 1) * LANES] = _bf(df * (1.0 - lbv) * (sig * (1.0 - sig)))
            dp_ref[:, 2 * D + hd * LANES:2 * D + (hd + 1) * LANES] = _bf(dv)
            dp_ref[:, 3 * D + hd * LANES:3 * D + (hd + 1) * LANES] = _bf(dgr)

    col = lambda j: pl.BlockSpec((C, D), lambda c, j=j: (NC - 1 - c, j))
    tile = pl.BlockSpec((C, D), lambda c: (NC - 1 - c, 0))
    return pl.pallas_call(
        body, name=name, grid=(NC,),
        in_specs=[col(0), col(1), col(2), col(3), tile, tile,
                  pl.BlockSpec((1, H, LANES, LANES), lambda c: (NC - 1 - c, 0, 0, 0)), _row(D), _row(LANES)],
        out_specs=[pl.BlockSpec((C, D4), lambda c: (NC - 1 - c, 0)), _row(D), _row(LANES)],
        out_shape=[jax.ShapeDtypeStruct((T, D4), BF16), jax.ShapeDtypeStruct((1, D), F32),
                   jax.ShapeDtypeStruct((1, LANES), F32)],
        scratch_shapes=[pltpu.VMEM((H, LANES, LANES), F32), pltpu.VMEM((1, D), F32), pltpu.VMEM((H, C, LANES), F32)],
        compiler_params=_cp("arbitrary"))(proj, proj, proj, proj, o, dog, st, lb, gg)


def _prev_blk(tm, hb):
    return lambda i: (jnp.maximum(i * (tm // hb) - 1, 0), 0)


def _next_blk(tm, hb, T):
    return lambda i: (jnp.minimum((i + 1) * (tm // hb), T // hb - 1), 0)


def _glu_to_ext(uc_ref, ucp_ref, ext, D, first):
    def glu(u):
        return u[:, :D] * _sig(u[:, D:])
    gh = jnp.where(first, 0.0, glu(ucp_ref[...]))
    gm = glu(uc_ref[...])
    for c in range(D // LANES):
        ext[c, 0:CONV_HALO, :] = gh[:, c * LANES:(c + 1) * LANES]
        ext[c, CONV_HALO:, :] = gm[:, c * LANES:(c + 1) * LANES]


def conv_mid_fwd(uc, cw, cb, lg, lbias, name):
    T, D2 = uc.shape
    D = D2 // 2
    NCH = D // LANES
    W = 31
    tm = _rt(T, 256)
    RS = min(CONV_ROWS, tm)

    def body(uc_ref, ucp_ref, cw_ref, cb_ref, lg_ref, lb_ref, cv_ref, va_ref, ext, outs):
        _glu_to_ext(uc_ref, ucp_ref, ext, D, pl.program_id(0) == 0)

        def chunk(c, carry):
            for rb in range(tm // RS):
                acc = jnp.zeros((RS, LANES), F32)
                for kk in range(W):
                    acc = acc + cw_ref[c, pl.ds(kk, 1), :] * ext[c, pl.ds(rb * RS + CONV_HALO - (W - 1) + kk, RS), :]
                outs[c, pl.ds(rb * RS, RS), :] = acc
            return carry
        lax.fori_loop(0, NCH, chunk, 0)
        cv = jnp.concatenate([outs[c] for c in range(NCH)], axis=1) + cb_ref[...]
        cv_ref[...] = cv
        mu = jnp.mean(cv, axis=-1, keepdims=True)
        xc = cv - mu
        rstd = lax.rsqrt(jnp.mean(xc * xc, axis=-1, keepdims=True) + EPS)
        l = xc * rstd * lg_ref[...] + lb_ref[...]
        va_ref[...] = _bf(l * _sig(l))

    tile = pl.BlockSpec((tm, D), lambda i: (i, 0))
    return pl.pallas_call(
        body, name=name, grid=(T // tm,),
        in_specs=[pl.BlockSpec((tm, D2), lambda i: (i, 0)), pl.BlockSpec((CONV_HALO, D2), _prev_blk(tm, CONV_HALO)),
                  pl.BlockSpec((NCH, 32, LANES), lambda i: (0, 0, 0)), _row(D), _row(D), _row(D)],
        out_specs=[tile, tile],
        out_shape=[jax.ShapeDtypeStruct((T, D), F32), jax.ShapeDtypeStruct((T, D), BF16)],
        scratch_shapes=[pltpu.VMEM((NCH, tm + CONV_HALO, LANES), F32), pltpu.VMEM((NCH, tm, LANES), F32)],
        compiler_params=_cp("parallel"))(uc, uc, cw, cb, lg, lbias)


def conv_bwd_a(dva, cv, uc, lg, lbias, name):
    T, D2 = uc.shape
    D = D2 // 2
    NCH = D // LANES
    W = 31
    tm = _rt(T, 256)
    RS = min(CONV_ROWS, tm)

    def body(dva_ref, cv_ref, uc_ref, ucp_ref, lg_ref, lb_ref, dcv_ref, dlg_ref, dlb_ref, taps_ref, ext, dcs):
        @pl.when(pl.program_id(0) == 0)
        def _():
            dlg_ref[...] = jnp.zeros_like(dlg_ref)
            dlb_ref[...] = jnp.zeros_like(dlb_ref)
            taps_ref[...] = jnp.zeros_like(taps_ref)
        _glu_to_ext(uc_ref, ucp_ref, ext, D, pl.program_id(0) == 0)
        cv = cv_ref[...]
        mu = jnp.mean(cv, axis=-1, keepdims=True)
        xc = cv - mu
        rstd = lax.rsqrt(jnp.mean(xc * xc, axis=-1, keepdims=True) + EPS)
        yn = xc * rstd
        l = yn * lg_ref[...] + lb_ref[...]
        s = _sig(l)
        dl = dva_ref[...] * (s * (1.0 + l * (1.0 - s)))
        dlg_ref[...] += jnp.sum(dl * yn, axis=0, keepdims=True)
        dlb_ref[...] += jnp.sum(dl, axis=0, keepdims=True)
        dyn = dl * lg_ref[...]
        dcv = rstd * (dyn - jnp.mean(dyn, axis=-1, keepdims=True) - yn * jnp.mean(dyn * yn, axis=-1, keepdims=True))
        dcv_ref[...] = dcv
        for c in range(NCH):
            dcs[c] = dcv[:, c * LANES:(c + 1) * LANES]

        def fold(p):
            return p.reshape(RS // 8, 8, LANES).sum(axis=0)

        def chunk(c, carry):
            for kk in range(W):
                tot = jnp.zeros((8, LANES), F32)
                for rb in range(tm // RS):
                    tot = tot + fold(dcs[c, pl.ds(rb * RS, RS), :]
                                     * ext[c, pl.ds(rb * RS + CONV_HALO - (W - 1) + kk, RS), :])
                taps_ref[c, pl.ds(8 * kk, 8), :] += tot
            tot = jnp.zeros((8, LANES), F32)
            for rb in range(tm // RS):
                tot = tot + fold(dcs[c, pl.ds(rb * RS, RS), :])
            taps_ref[c, pl.ds(8 * W, 8), :] += tot
            return carry
        lax.fori_loop(0, NCH, chunk, 0)

    tile = pl.BlockSpec((tm, D), lambda i: (i, 0))
    vec = jax.ShapeDtypeStruct((1, D), F32)
    return pl.pallas_call(
        body, name=name, grid=(T // tm,),
        in_specs=[tile, tile, pl.BlockSpec((tm, D2), lambda i: (i, 0)),
                  pl.BlockSpec((CONV_HALO, D2), _prev_blk(tm, CONV_HALO)), _row(D), _row(D)],
        out_specs=[tile, _row(D), _row(D), pl.BlockSpec((NCH, 256, LANES), lambda i: (0, 0, 0))],
        out_shape=[jax.ShapeDtypeStruct((T, D), F32), vec, vec, jax.ShapeDtypeStruct((NCH, 256, LANES), F32)],
        scratch_shapes=[pltpu.VMEM((NCH, tm + CONV_HALO, LANES), F32), pltpu.VMEM((NCH, tm, LANES), F32)],
        compiler_params=_cp("arbitrary"))(dva, cv, uc, uc, lg, lbias)


def conv_bwd_b(dcv, uc, cw, name):
    T, D2 = uc.shape
    D = D2 // 2
    NCH = D // LANES
    W = 31
    tm = _rt(T, 256)
    RS = min(CONV_ROWS, tm)
    NT = T // tm

    def body(dcv_ref, dcn_ref, uc_ref, cw_ref, du_ref, db_ref, ext, outs):
        i = pl.program_id(0)

        @pl.when(i == 0)
        def _():
            db_ref[...] = jnp.zeros_like(db_ref)
        dm = dcv_ref[...]
        dn = jnp.where(i == NT - 1, 0.0, dcn_ref[...])
        for c in range(NCH):
            ext[c, 0:tm, :] = dm[:, c * LANES:(c + 1) * LANES]
            ext[c, tm:, :] = dn[:, c * LANES:(c + 1) * LANES]

        def chunk(c, carry):
            for rb in range(tm // RS):
                acc = jnp.zeros((RS, LANES), F32)
                for kk in range(W):
                    acc = acc + cw_ref[c, pl.ds(kk, 1), :] * ext[c, pl.ds(rb * RS + (W - 1) - kk, RS), :]
                outs[c, pl.ds(rb * RS, RS), :] = acc
            return carry
        lax.fori_loop(0, NCH, chunk, 0)
        dglu = jnp.concatenate([outs[c] for c in range(NCH)], axis=1)
        u = uc_ref[...]
        a = u[:, :D]
        sg = _sig(u[:, D:])
        da = dglu * sg
        dg = dglu * a * (sg * (1.0 - sg))
        du_ref[:, :D] = _bf(da)
        du_ref[:, D:] = _bf(dg)
        db_ref[:, :D] += jnp.sum(da, axis=0, keepdims=True)
        db_ref[:, D:] += jnp.sum(dg, axis=0, keepdims=True)

    tile = pl.BlockSpec((tm, D), lambda i: (i, 0))
    wide = pl.BlockSpec((tm, D2), lambda i: (i, 0))
    return pl.pallas_call(
        body, name=name, grid=(NT,),
        in_specs=[tile, pl.BlockSpec((CONV_HALO, D), _next_blk(tm, CONV_HALO, T)), wide,
                  pl.BlockSpec((NCH, 32, LANES), lambda i: (0, 0, 0))],
        out_specs=[wide, _row(D2)],
        out_shape=[jax.ShapeDtypeStruct((T, D2), BF16), jax.ShapeDtypeStruct((1, D2), F32)],
        scratch_shapes=[pltpu.VMEM((NCH, tm + CONV_HALO, LANES), F32), pltpu.VMEM((NCH, tm, LANES), F32)],
        compiler_params=_cp("arbitrary"))(dcv, dcv, uc, cw)


def _conv3(ext, u, uh, dw_ref, bias_ref, tm):
    ext[0:FFN_HALO, :] = uh
    ext[FFN_HALO:, :] = u
    return (dw_ref[pl.ds(0, 1), :] * ext[pl.ds(FFN_HALO - 2, tm), :]
            + dw_ref[pl.ds(1, 1), :] * ext[pl.ds(FFN_HALO - 1, tm), :]
            + dw_ref[pl.ds(2, 1), :] * u + bias_ref[...])


def ffn_fwd(h, wup, dww, dwb, wdn, x, gate, pg, name):
    T, D = h.shape
    fs = wup.shape[2]
    NJ = wup.shape[0] // 2
    tm = _rt(T, 512)

    def body(h_ref, hp_ref, wa_ref, wb_ref, dwa_ref, dwb_ref, ba_ref, bb_ref, wd_ref, x_ref, gt_ref, pg_ref,
             ua_ref, ub_ref, y_ref, xo_ref, acc, exta, extb):
        i, j = pl.program_id(0), pl.program_id(1)

        @pl.when(j == 0)
        def _():
            acc[...] = jnp.zeros_like(acc)
        hb = h_ref[...]
        hp = hp_ref[...]

        def half(w_ref, dw_ref, b_ref, u_ref, ext):
            u = _nn(hb, w_ref[...])
            uh = jnp.where(i == 0, 0.0, _nn(hp, w_ref[...]))
            u_ref[...] = u
            return _conv3(ext, u, uh, dw_ref, b_ref, tm)
        a = half(wa_ref, dwa_ref, ba_ref, ua_ref, exta)
        b = half(wb_ref, dwb_ref, bb_ref, ub_ref, extb)
        acc[...] += _nn(_bf(a * _sig(a) * b), wd_ref[...])

        @pl.when(j == NJ - 1)
        def _():
            y = acc[...]
            y_ref[...] = y
            r = lax.rsqrt(jnp.mean(y * y, axis=-1, keepdims=True) + EPS)
            xo_ref[...] = x_ref[...] + gt_ref[...] * ((y * r) * pg_ref[...])

    tile = pl.BlockSpec((tm, D), lambda i, j: (i, 0))
    shard = lambda off, r: pl.BlockSpec((None, r, fs), lambda i, j: (j + off, 0, 0))
    ush = pl.BlockSpec((None, tm, fs), lambda i, j: (j, i, 0))
    vec = pl.BlockSpec((1, D), lambda i, j: (0, 0))
    return pl.pallas_call(
        body, name=name, grid=(T // tm, NJ),
        in_specs=[tile, pl.BlockSpec((FFN_HALO, D), lambda i, j: (jnp.maximum(i * (tm // FFN_HALO) - 1, 0), 0)),
                  shard(0, D), shard(NJ, D), shard(0, 3), shard(NJ, 3), shard(0, 1), shard(NJ, 1),
                  pl.BlockSpec((None, fs, D), lambda i, j: (j, 0, 0)), tile, vec, vec],
        out_specs=[ush, ush, tile, tile],
        out_shape=[jax.ShapeDtypeStruct((NJ, T, fs), F32), jax.ShapeDtypeStruct((NJ, T, fs), F32),
                   jax.ShapeDtypeStruct((T, D), F32), jax.ShapeDtypeStruct((T, D), F32)],
        scratch_shapes=[pltpu.VMEM((tm, D), F32), pltpu.VMEM((tm + FFN_HALO, fs), F32),
                        pltpu.VMEM((tm + FFN_HALO, fs), F32)],
        compiler_params=_cp("parallel", "arbitrary"))(h, h, wup, wup, dww, dww, dwb, dwb, wdn, x, gate, pg)


def ffn_bwd_a(dy, ua, ub, dww, dwb, wdn, name):
    NJ, T, fs = ua.shape
    D = dy.shape[1]
    tm = _rt(T, 512)

    def body(dy_ref, ua_ref, uap_ref, ub_ref, ubp_ref, dwa_ref, dwb_ref, ba_ref, bb_ref, wd_ref,
             z_ref, da_ref, db_ref, ga_ref, gb_ref, exta, extb):
        i = pl.program_id(1)

        @pl.when(i == 0)
        def _():
            ga_ref[...] = jnp.zeros_like(ga_ref)
            gb_ref[...] = jnp.zeros_like(gb_ref)
        dz = _nt(_bf(dy_ref[...]), wd_ref[...])
        ua = ua_ref[...]
        ub = ub_ref[...]
        a = _conv3(exta, ua, jnp.where(i == 0, 0.0, uap_ref[...]), dwa_ref, ba_ref, tm)
        b = _conv3(extb, ub, jnp.where(i == 0, 0.0, ubp_ref[...]), dwb_ref, bb_ref, tm)
        sa = _sig(a)
        z_ref[...] = _bf(a * sa * b)
        da = dz * b * (sa * (1.0 + a * (1.0 - sa)))
        db = dz * (a * sa)
        da_ref[...] = da
        db_ref[...] = db

        def taps(g_ref, d, ext, u):
            g_ref[pl.ds(0, 1), :] += jnp.sum(d * ext[pl.ds(FFN_HALO - 2, tm), :], axis=0, keepdims=True)
            g_ref[pl.ds(1, 1), :] += jnp.sum(d * ext[pl.ds(FFN_HALO - 1, tm), :], axis=0, keepdims=True)
            g_ref[pl.ds(2, 1), :] += jnp.sum(d * u, axis=0, keepdims=True)
            g_ref[pl.ds(3, 1), :] += jnp.sum(d, axis=0, keepdims=True)
        taps(ga_ref, da, exta, ua)
        taps(gb_ref, db, extb, ub)

    ush = pl.BlockSpec((None, tm, fs), lambda j, i: (j, i, 0))
    uprev = pl.BlockSpec((None, FFN_HALO, fs), lambda j, i: (j, jnp.maximum(i * (tm // FFN_HALO) - 1, 0), 0))
    shard = lambda off, r: pl.BlockSpec((None, r, fs), lambda j, i: (j + off, 0, 0))
    gsp = pl.BlockSpec((None, 8, fs), lambda j, i: (j, 0, 0))
    big = jax.ShapeDtypeStruct((NJ, T, fs), F32)
    return pl.pallas_call(
        body, name=name, grid=(NJ, T // tm),
        in_specs=[pl.BlockSpec((tm, D), lambda j, i: (i, 0)), ush, uprev, ush, uprev, shard(0, 3), shard(NJ, 3),
                  shard(0, 1), shard(NJ, 1), pl.BlockSpec((None, fs, D), lambda j, i: (j, 0, 0))],
        out_specs=[ush, ush, ush, gsp, gsp],
        out_shape=[jax.ShapeDtypeStruct((NJ, T, fs), BF16), big, big, jax.ShapeDtypeStruct((NJ, 8, fs), F32),
                   jax.ShapeDtypeStruct((NJ, 8, fs), F32)],
        scratch_shapes=[pltpu.VMEM((tm + FFN_HALO, fs), F32), pltpu.VMEM((tm + FFN_HALO, fs), F32)],
        compiler_params=_cp("parallel", "arbitrary"))(dy, ua, ua, ub, ub, dww, dww, dwb, dwb, wdn)


def ffn_bwd_b(duc, dww, wup, dh_in, off, name):
    NJ, T, fs = duc.shape
    D = wup.shape[1]
    tm = _rt(T, 512)
    NT = T // tm

    def body(d_ref, dn_ref, dw_ref, w_ref, dhin_ref, du_ref, dh_ref, acc, ext):
        i, j = pl.program_id(0), pl.program_id(1)

        @pl.when(j == 0)
        def _():
            acc[...] = dhin_ref[...]
        d = d_ref[...]
        ext[0:tm, :] = d
        ext[tm:, :] = jnp.where(i == NT - 1, 0.0, dn_ref[...])
        du = _bf(dw_ref[pl.ds(2, 1), :] * d + dw_ref[pl.ds(1, 1), :] * ext[pl.ds(1, tm), :]
                 + dw_ref[pl.ds(0, 1), :] * ext[pl.ds(2, tm), :])
        du_ref[...] = du
        acc[...] += _nt(du, w_ref[...])

        @pl.when(j == NJ - 1)
        def _():
            dh_ref[...] = acc[...]

    ush = pl.BlockSpec((None, tm, fs), lambda i, j: (j, i, 0))
    unext = pl.BlockSpec((None, FFN_HALO, fs),
                         lambda i, j: (j, jnp.minimum((i + 1) * (tm // FFN_HALO), T // FFN_HALO - 1), 0))
    tile = pl.BlockSpec((tm, D), lambda i, j: (i, 0))
    return pl.pallas_call(
        body, name=name, grid=(NT, NJ),
        in_specs=[ush, unext, pl.BlockSpec((None, 3, fs), lambda i, j: (j + off, 0, 0)),
                  pl.BlockSpec((None, D, fs), lambda i, j: (j + off, 0, 0)), tile],
        out_specs=[ush, tile],
        out_shape=[jax.ShapeDtypeStruct((NJ, T, fs), BF16), jax.ShapeDtypeStruct((T, D), F32)],
        scratch_shapes=[pltpu.VMEM((tm, D), F32), pltpu.VMEM((tm + FFN_HALO, fs), F32)],
        compiler_params=_cp("parallel", "arbitrary"))(duc, duc, dww, wup, dh_in)


def ada_fwd(c_all, w, bias, name):
    L, D, n = w.shape

    def body(c_ref, w_ref, b_ref, o_ref):
        cv = c_ref[...]
        o_ref[...] = _nn(_bf(cv * _sig(cv)), _bf(w_ref[...])) + b_ref[...]

    return pl.pallas_call(
        body, name=name, grid=(L,),
        in_specs=[pl.BlockSpec((N_DEV, D), lambda l: (0, 0)), pl.BlockSpec((None, D, n), lambda l: (l, 0, 0)),
                  pl.BlockSpec((None, 1, n), lambda l: (l, 0, 0))],
        out_specs=pl.BlockSpec((None, N_DEV, n), lambda l: (l, 0, 0)),
        out_shape=jax.ShapeDtypeStruct((L, N_DEV, n), F32), compiler_params=_cp("parallel"))(c_all, w, bias)


def ada_bwd(c_all, dm, name):
    L, _, n = dm.shape
    D = c_all.shape[1]

    def body(c_ref, d_ref, o_ref):
        cv = c_ref[...]
        o_ref[...] = _tn(_bf(cv * _sig(cv)), _bf(d_ref[...]))

    return pl.pallas_call(
        body, name=name, grid=(L,),
        in_specs=[pl.BlockSpec((N_DEV, D), lambda l: (0, 0)), pl.BlockSpec((None, N_DEV, n), lambda l: (l, 0, 0))],
        out_specs=pl.BlockSpec((None, D, n), lambda l: (l, 0, 0)),
        out_shape=jax.ShapeDtypeStruct((L, D, n), F32), compiler_params=_cp("parallel"))(c_all, dm)


def sum_slots(g, name):
    S, R, C = g.shape

    def body(g_ref, o_ref):
        acc = g_ref[0]
        for s in range(1, S):
            acc = acc + g_ref[s]
        o_ref[...] = acc

    return pl.pallas_call(
        body, name=name, grid=(1,), in_specs=[pl.BlockSpec((S, R, C), lambda i: (0, 0, 0))],
        out_specs=pl.BlockSpec((R, C), lambda i: (0, 0)), out_shape=jax.ShapeDtypeStruct((R, C), F32),
        compiler_params=_cp("arbitrary"))(g)


def lb_softmax(logits, name):
    def body(l_ref, s_ref):
        l = l_ref[...]
        e = jnp.exp(l - jnp.max(l, axis=0, keepdims=True))
        s_ref[...] = e / jnp.sum(e, axis=0, keepdims=True)
    return pl.pallas_call(body, name=name, out_shape=jax.ShapeDtypeStruct(logits.shape, F32))(logits)


def adamw(w, gs, m, v, name):
    S, R, C = gs.shape
    tr = R
    budget = (4 * 1024 * 1024) // (4 * (S + 7) * C)
    if R > budget:
        tr = max(t for t in range(8, budget + 1, 8) if R % t == 0)

    def body(w_ref, g_ref, m_ref, v_ref, go_ref, d_ref, mo_ref, vo_ref):
        g = g_ref[0]
        for s in range(1, S):
            g = g + g_ref[s]
        go_ref[...] = g
        mn = ADAM_B1 * m_ref[...] + (1.0 - ADAM_B1) * g
        vn = ADAM_B2 * v_ref[...] + (1.0 - ADAM_B2) * (g * g)
        mo_ref[...] = mn
        vo_ref[...] = vn
        m_hat = mn / (1.0 - ADAM_B1 ** ADAM_STEP)
        v_hat = vn / (1.0 - ADAM_B2 ** ADAM_STEP)
        d_ref[...] = -ADAM_LR * (m_hat / (jnp.sqrt(v_hat) + ADAM_EPS) + ADAM_WD * w_ref[...])

    tile = pl.BlockSpec((tr, C), lambda i: (i, 0))
    out = jax.ShapeDtypeStruct((R, C), F32)
    return pl.pallas_call(
        body, name=name, grid=(R // tr,), in_specs=[tile, pl.BlockSpec((S, tr, C), lambda i: (0, i, 0)), tile, tile],
        out_specs=[tile] * 4, out_shape=[out] * 4, compiler_params=_cp("parallel"))(w, gs, m, v)


def _place():
    return lax.axis_index("x"), lax.axis_index("y"), lax.axis_index("c")


def _slot(p):
    return 4 * p[0] + 2 * p[1] + p[2]


def all_gather(xs, name):
    n = len(xs)

    def body(*refs):
        ins, outs = refs[:n], refs[n:2 * n]
        send_sems, recv_sems, local_sems = refs[2 * n:]
        x, y, c = _place()
        me, sib = (x, y, c), (x, y, 1 - c)
        chips = [(1 - x, y), (x, 1 - y), (1 - x, 1 - y)]

        def copy(a, k, block, to, src=None):
            dst = outs[a].at[_slot(block)]
            return pltpu.make_async_remote_copy(
                src_ref=dst if src is None else src, dst_ref=dst, send_sem=send_sems.at[a, k],
                recv_sem=recv_sems.at[a, k], device_id=to, device_id_type=MESH)

        mine = [pltpu.make_async_copy(ins[a], outs[a].at[_slot(me)], local_sems.at[a]) for a in range(n)]
        for cp in mine:
            cp.start()
        first = []
        for a in range(n):
            first.append(copy(a, 0, me, sib, src=ins[a]))
            first += [copy(a, 1 + j, me, (*chip, c), src=ins[a]) for j, chip in enumerate(chips)]
        for cp in first:
            cp.start()
        passed = []
        for j, chip in enumerate(chips):
            for a in range(n):
                copy(a, 1 + j, (*chip, c), me).wait_recv()
                fwd = copy(a, 4 + j, (*chip, c), sib)
                fwd.start()
                passed.append(fwd)
        for a in range(n):
            copy(a, 0, sib, me).wait_recv()
            for j, chip in enumerate(chips):
                copy(a, 4 + j, (*chip, 1 - c), me).wait_recv()
        for cp in first + passed:
            cp.wait_send()
        for cp in mine:
            cp.wait()

    return pl.pallas_call(
        body, name=name, in_specs=[ANY] * n, out_specs=[ANY] * n,
        out_shape=[jax.ShapeDtypeStruct((N_DEV,) + a.shape, a.dtype) for a in xs],
        scratch_shapes=[pltpu.SemaphoreType.DMA((n, 7)), pltpu.SemaphoreType.DMA((n, 7)),
                        pltpu.SemaphoreType.DMA((n,))])(*xs)


def all_to_all(gs, name):
    n = len(gs)

    def body(*refs):
        ins, outs = refs[:n], refs[n:2 * n]
        send_sems, recv_sems, local_sems = refs[2 * n:]
        x, y, c = _place()
        me = (x, y, c)

        def peer(r):
            return (1 - x if r & 4 else x, 1 - y if r & 2 else y, 1 - c if r & 1 else c)

        def copy(a, r):
            p = peer(r)
            return pltpu.make_async_remote_copy(
                src_ref=ins[a].at[_slot(p)], dst_ref=outs[a].at[_slot(me)], send_sem=send_sems.at[a, r - 1],
                recv_sem=recv_sems.at[a, r - 1], device_id=p, device_id_type=MESH)

        def landed(a, r):
            p = peer(r)
            return pltpu.make_async_remote_copy(
                src_ref=ins[a].at[_slot(p)], dst_ref=outs[a].at[_slot(p)], send_sem=send_sems.at[a, r - 1],
                recv_sem=recv_sems.at[a, r - 1], device_id=p, device_id_type=MESH)

        mine = [pltpu.make_async_copy(ins[a].at[_slot(me)], outs[a].at[_slot(me)], local_sems.at[a])
                for a in range(n)]
        for cp in mine:
            cp.start()
        sends = [copy(a, r) for r in range(1, N_DEV) for a in range(n)]
        for cp in sends:
            cp.start()
        for r in range(1, N_DEV):
            for a in range(n):
                landed(a, r).wait_recv()
        for cp in sends:
            cp.wait_send()
        for cp in mine:
            cp.wait()

    return pl.pallas_call(
        body, name=name, in_specs=[ANY] * n, out_specs=[ANY] * n,
        out_shape=[jax.ShapeDtypeStruct(a.shape, a.dtype) for a in gs],
        scratch_shapes=[pltpu.SemaphoreType.DMA((n, 7)), pltpu.SemaphoreType.DMA((n, 7)),
                        pltpu.SemaphoreType.DMA((n,))])(*gs)


def _mods(mod, l, D):
    return [mod[l:l + 1, k * D:(k + 1) * D] for k in range(6)]


def _ffn_backward(l, dxo, hb, ua, ub, yf, xin, g2, sc2, gains, W):
    fs = ua.shape[2]
    T, D = xin.shape
    dy, dgate, dpg, _ = postnorm_bwd(dxo, yf, g2, gains["post_ffn_g"][l:l + 1], f"ffn{l}_postnorm_bwd")
    z, duca, ducb, ga, gb = ffn_bwd_a(dy, ua, ub, W["fdw_w"][l], W["fdw_b"][l], W["wdn"][l], f"ffn{l}_bwd_a")
    dua, dh = ffn_bwd_b(duca, W["fdw_w"][l], W["wup"][l], jnp.zeros((T, D), F32), 0, f"ffn{l}_bwd_b0")
    dub, dh = ffn_bwd_b(ducb, W["fdw_w"][l], W["wup"][l], dh, ua.shape[0], f"ffn{l}_bwd_b1")
    d_wdn = mm_tn(z, dy, f"ffn{l}_dwdn", False, True, fs, D)
    d_wup = jnp.concatenate([mm_tn(hb, dua, f"ffn{l}_dwup_a", True, False, D, fs),
                             mm_tn(hb, dub, f"ffn{l}_dwup_b", True, False, D, fs)], axis=0)
    dx, da, dsh = prenorm_bwd(dh, xin, dxo, gains["pre_ffn_g"][l:l + 1], sc2, f"ffn{l}_prenorm_bwd")
    small = dict(dgate=dgate, dpost=dpg, da=da, dsh=dsh,
                 dwb=jnp.concatenate([ga[:, 3], gb[:, 3]], axis=0),
                 dww=jnp.concatenate([ga[:, 0:3], gb[:, 0:3]], axis=0))
    return dx, d_wup, d_wdn, small


def _local_step(xt, tgt, mod, gains, W):
    T, D = xt.shape
    zero_d = jnp.zeros((1, D), F32)
    sh1, sc1, g1, sh2, sc2, g2 = m0 = _mods(mod, 0, D)
    h0 = prenorm_mod(xt, gains["pre_mix_g"][0:1], sc1, sh1, "l0_prenorm")
    proj = mm_nn(h0, W["whin"], jnp.zeros((1, 4 * D), F32), "hgrn_proj")
    o, og, st = hgrn_fwd(proj, W["lb"], W["gg"], "hgrn_fwd")
    y0, x1 = mm_post(og, W["whout"], zero_d, xt, g1, gains["post_mix_g"][0:1], "hgrn_out")
    h0f = prenorm_mod(x1, gains["pre_ffn_g"][0:1], sc2, sh2, "f0_prenorm")
    ua0, ub0, yf0, x2 = ffn_fwd(h0f, W["wup"][0], W["fdw_w"][0], W["fdw_b"][0], W["wdn"][0], x1, g2,
                                gains["post_ffn_g"][0:1], "ffn0_fwd")
    t1, c1, e1, t2, c2, e2 = m1 = _mods(mod, 1, D)
    h1 = prenorm_mod(x2, gains["pre_mix_g"][1:2], c1, t1, "l1_prenorm")
    uc = mm_nn(h1, W["wcin"], W["cb_in"], "conv_in")
    cv, va = conv_mid_fwd(uc, W["cw"], W["cdw_b"], W["ln_g"], W["ln_b"], "conv_mid_fwd")
    y1, x3 = mm_post(va, W["wcout"], W["cb_out"], x2, e1, gains["post_mix_g"][1:2], "conv_out")
    h1f = prenorm_mod(x3, gains["pre_ffn_g"][1:2], c2, t2, "f1_prenorm")
    ua1, ub1, yf1, x4 = ffn_fwd(h1f, W["wup"][1], W["fdw_w"][1], W["fdw_b"][1], W["wdn"][1], x3, e2,
                                gains["post_ffn_g"][1:2], "ffn1_fwd")
    dx4, loss_part = loss_grad(x4, tgt, "loss_grad")
    dx3, d_wup1, d_wdn1, sf1 = _ffn_backward(1, dx4, h1f, ua1, ub1, yf1, x3, e2, c2, gains, W)
    dy1, dgate_c, dpost_c, dys_c = postnorm_bwd(dx3, y1, e1, gains["post_mix_g"][1:2], "conv_postnorm_bwd")
    dva = mm_nt(dy1, W["wcout"][None], "conv_dva", True)
    d_wcout = mm_tn(va, dy1, "conv_dwout", True, True, D, D)
    dcv, dlg, dlbias, taps = conv_bwd_a(dva, cv, uc, W["ln_g"], W["ln_b"], "conv_bwd_a")
    duc, dbin = conv_bwd_b(dcv, uc, W["cw"], "conv_bwd_b")
    dh1 = mm_nt(duc, W["wcin"], "conv_dh", True)
    d_wcin = mm_tn(h1, duc, "conv_dwin", True, True, D, W["wcin"].shape[2])
    dx2, da_c, dsh_c = prenorm_bwd(dh1, x2, dx3, gains["pre_mix_g"][1:2], c1, "l1_prenorm_bwd")
    dx1, d_wup0, d_wdn0, sf0 = _ffn_backward(0, dx2, h0f, ua0, ub0, yf0, x1, g2, sc2, gains, W)
    dy0, dgate_h, dpost_h, _ = postnorm_bwd(dx1, y0, g1, gains["post_mix_g"][0:1], "hgrn_postnorm_bwd")
    dog = mm_nt(dy0, W["whout"][None], "hgrn_dog", True)
    d_whout = mm_tn(og, dy0, "hgrn_dwout", True, True, D, D)
    dproj, dlb, dgg = hgrn_bwd(proj, o, dog, st, W["lb"], W["gg"], "hgrn_bwd")
    dh0 = mm_nt(dproj, W["whin"], "hgrn_dh", True)
    d_whin = mm_tn(h0, dproj, "hgrn_dwin", True, True, D, W["whin"].shape[2])
    dx0, da_h, dsh_h = prenorm_bwd(dh0, xt, dx1, gains["pre_mix_g"][0:1], sc1, "l0_prenorm_bwd")

    big = dict(whin=d_whin, whout=d_whout, wcin=d_wcin, wcout=d_wcout, wup=[d_wup0, d_wup1], wdn=[d_wdn0, d_wdn1])
    pm, pf = gains["pre_mix_g"], gains["pre_ffn_g"]
    dmod = jnp.concatenate([
        dsh_h, da_h * pm[0:1], dgate_h, sf0["dsh"], sf0["da"] * pf[0:1], sf0["dgate"],
        dsh_c, da_c * pm[1:2], dgate_c, sf1["dsh"], sf1["da"] * pf[1:2], sf1["dgate"]], axis=1)
    NCH = D // LANES
    tap = taps.reshape(NCH, 32, 8, LANES).sum(axis=2)
    small = dict(
        dmod=dmod,
        pre_mix_g=jnp.concatenate([da_h * (1.0 + sc1), da_c * (1.0 + c1)], axis=0),
        post_mix_g=jnp.concatenate([dpost_h, dpost_c], axis=0),
        pre_ffn_g=jnp.concatenate([sf0["da"] * (1.0 + sc2), sf1["da"] * (1.0 + c2)], axis=0),
        post_ffn_g=jnp.concatenate([sf0["dpost"], sf1["dpost"]], axis=0),
        lb=dlb, gg=dgg,
        ffn_dw_b=jnp.stack([sf0["dwb"], sf1["dwb"]]),
        conv_b_in=dbin,
        conv_dw_w=jnp.transpose(tap[:, :31], (1, 0, 2)).reshape(31, D),
        conv_dw_b=tap[:, 31].reshape(1, D),
        conv_ln_g=dlg, conv_ln_b=dlbias, conv_b_out=dys_c,
        ffn_dw_w=jnp.stack([sf0["dww"], sf1["dww"]]))
    return loss_part, dx0, big, small


SMALL_ORDER = ["dmod", "pre_mix_g", "post_mix_g", "pre_ffn_g", "post_ffn_g", "lb", "gg", "ffn_dw_b", "conv_b_in",
               "conv_dw_w", "conv_dw_b", "conv_ln_g", "conv_ln_b", "conv_b_out", "ffn_dw_w"]


def _pack(parts):
    flat = jnp.concatenate([p.reshape(-1) for p in parts])
    pad = (-flat.shape[0]) % LANES
    if pad:
        flat = jnp.concatenate([flat, jnp.zeros((pad,), F32)])
    return flat.reshape(-1, LANES)


def _unpack(flat, shapes):
    out, off = [], 0
    for s in shapes:
        n = 1
        for d in s:
            n *= d
        out.append(flat[off:off + n].reshape(s))
        off += n
    return out


def _apply_adamw(name, w, g_slots, m, v):
    shp = w.shape
    C = shp[-1]
    R = w.size // C
    g, d, mn, vn = adamw(w.reshape(R, C), g_slots.reshape(g_slots.shape[0], R, C), m.reshape(R, C), v.reshape(R, C),
                         "adamw_" + name)
    return g.reshape(shp), d.reshape(shp), mn.reshape(shp), vn.reshape(shp)


WEIGHTS = ['ada_w', 'ada_b', 'pre_mix_g', 'post_mix_g', 'pre_ffn_g', 'post_ffn_g', 'hgrn_w_in', 'hgrn_lb_logits',
           'hgrn_gnorm_g', 'hgrn_w_out', 'conv_w_in', 'conv_b_in', 'conv_dw_w', 'conv_dw_b', 'conv_ln_g', 'conv_ln_b',
           'conv_w_out', 'conv_b_out', 'ffn_w_up', 'ffn_dw_w', 'ffn_dw_b', 'ffn_w_down']


def kernel(x, c, ada_w, ada_b, pre_mix_g, post_mix_g, pre_ffn_g, post_ffn_g, hgrn_w_in, hgrn_lb_logits, hgrn_gnorm_g, hgrn_w_out, conv_w_in, conv_b_in, conv_dw_w, conv_dw_b, conv_ln_g, conv_ln_b, conv_w_out, conv_b_out, ffn_w_up, ffn_dw_w, ffn_dw_b, ffn_w_down, loss_target, m_ada_w, m_ada_b, m_pre_mix_g, m_post_mix_g, m_pre_ffn_g, m_post_ffn_g, m_hgrn_w_in, m_hgrn_lb_logits, m_hgrn_gnorm_g, m_hgrn_w_out, m_conv_w_in, m_conv_b_in, m_conv_dw_w, m_conv_dw_b, m_conv_ln_g, m_conv_ln_b, m_conv_w_out, m_conv_b_out, m_ffn_w_up, m_ffn_dw_w, m_ffn_dw_b, m_ffn_w_down, v_ada_w, v_ada_b, v_pre_mix_g, v_post_mix_g, v_pre_ffn_g, v_post_ffn_g, v_hgrn_w_in, v_hgrn_lb_logits, v_hgrn_gnorm_g, v_hgrn_w_out, v_conv_w_in, v_conv_b_in, v_conv_dw_w, v_conv_dw_b, v_conv_ln_g, v_conv_ln_b, v_conv_w_out, v_conv_b_out, v_ffn_w_up, v_ffn_dw_w, v_ffn_dw_b, v_ffn_w_down):
    P = dict(locals())
    _, T, D = x.shape
    me = _slot(_place())
    L = ada_w.shape[0]
    na = ada_w.shape[2]
    fs = ffn_w_up.shape[2]
    nci = conv_w_in.shape[2]
    nd = conv_dw_b.shape[1]
    NCH = D // LANES

    small_in = [c, conv_b_in, conv_dw_w, conv_dw_b, conv_ln_g, conv_ln_b, conv_b_out, ffn_dw_w]
    gathered = all_gather(
        [_bf(hgrn_w_in[0]), _bf(hgrn_w_out[0]), _bf(conv_w_in[0]), _bf(conv_w_out[0]), _bf(ffn_w_up[0]),
         _bf(ffn_w_up[1]), _bf(ffn_w_down[0]), _bf(ffn_w_down[1]), _pack(small_in)], "gather_weights")
    whin, whout, wcin, wcout, wup0, wup1, wdn0, wdn1, packed = gathered
    sm = packed.reshape(N_DEV, -1)
    offs = [0]
    for a in small_in:
        offs.append(offs[-1] + a.size)
    piece = lambda k, shp: sm[:, offs[k]:offs[k + 1]].reshape((N_DEV,) + shp)
    c_all = piece(0, (D,))
    dw_nat = jnp.transpose(piece(2, (31, nd)), (1, 0, 2)).reshape(31, D)
    cw = jnp.pad(jnp.transpose(dw_nat.reshape(31, NCH, LANES), (1, 0, 2)), ((0, 0), (0, 1), (0, 0)))
    fdw = piece(7, (L, 3, fs))
    lb_soft = lb_softmax(hgrn_lb_logits, "lb_softmax")
    W = dict(
        whin=whin, whout=whout.reshape(D, D), wcin=wcin, wcout=wcout.reshape(D, D), wup=[wup0, wup1],
        wdn=[wdn0.reshape(N_DEV // 2, 2 * wdn0.shape[1], D), wdn1.reshape(N_DEV // 2, 2 * wdn1.shape[1], D)],
        lb=lb_soft[0:1], gg=hgrn_gnorm_g, cb_in=piece(1, (nci,)).reshape(1, N_DEV * nci), cw=cw,
        cdw_b=piece(3, (nd,)).reshape(1, D), ln_g=piece(4, (nd,)).reshape(1, D), ln_b=piece(5, (nd,)).reshape(1, D),
        cb_out=piece(6, (nd,)).reshape(1, D), fdw_w=[fdw[:, l] for l in range(L)],
        fdw_b=[ffn_dw_b[l].reshape(N_DEV, 1, fs) for l in range(L)])

    bias_mine = lax.dynamic_slice(ada_b, (0, me * na), (L, na)).reshape(L, 1, na)
    mod_cols = ada_fwd(c_all, ada_w, bias_mine, "ada_fwd")
    (mod_all,) = all_gather([mod_cols], "gather_mod")
    mod = jnp.transpose(lax.dynamic_index_in_dim(mod_all, me, axis=2, keepdims=False), (1, 0, 2)).reshape(L, N_DEV * na)

    gains = dict(pre_mix_g=pre_mix_g, post_mix_g=post_mix_g, pre_ffn_g=pre_ffn_g, post_ffn_g=post_ffn_g)
    loss_part, dx0, big, small = _local_step(x[0], loss_target[0], mod, gains, W)
    loss = lax.psum((0.5 / D) * jnp.sum(loss_part), ("x", "y", "c"))

    parts = [small[k] for k in SMALL_ORDER]
    (sm_all,) = all_gather([_pack(parts)], "gather_small_grads")
    tot = _unpack(sum_slots(sm_all, "sum_small_grads").reshape(-1), [p.shape for p in parts])
    tot = dict(zip(SMALL_ORDER, tot))
    dmod_all = sm_all.reshape(N_DEV, -1)[:, :L * 6 * D].reshape(N_DEV, L, 6 * D)
    dm_mine = jnp.transpose(lax.dynamic_slice(dmod_all, (0, 0, me * na), (N_DEV, L, na)), (1, 0, 2))
    s0 = lb_soft[0:1]
    onehot = (lax.broadcasted_iota(jnp.int32, lb_soft.shape, 0) == 0).astype(F32)
    col = lambda a, n: lax.dynamic_slice_in_dim(a, me * n, n, axis=a.ndim - 1)
    G = {
        'ada_w': ada_bwd(c_all, dm_mine, "ada_bwd")[None],
        'ada_b': tot["dmod"].reshape(1, L, 6 * D),
        'pre_mix_g': tot["pre_mix_g"][None], 'post_mix_g': tot["post_mix_g"][None],
        'pre_ffn_g': tot["pre_ffn_g"][None], 'post_ffn_g': tot["post_ffn_g"][None],
        'hgrn_lb_logits': (tot["lb"] * s0 * (onehot - lb_soft))[None],
        'hgrn_gnorm_g': tot["gg"][None],
        'ffn_dw_b': tot["ffn_dw_b"].reshape(1, L, N_DEV * fs),
        'conv_b_in': col(tot["conv_b_in"], nci)[None],
        'conv_dw_w': col(tot["conv_dw_w"], nd)[None, None],
        'conv_dw_b': col(tot["conv_dw_b"], nd)[None], 'conv_ln_g': col(tot["conv_ln_g"], nd)[None],
        'conv_ln_b': col(tot["conv_ln_b"], nd)[None], 'conv_b_out': col(tot["conv_b_out"], nd)[None],
        'ffn_dw_w': lax.dynamic_index_in_dim(tot["ffn_dw_w"], me, axis=1, keepdims=False)[None],
    }

    rows = hgrn_w_out.shape[1]
    frows = ffn_w_down.shape[1]
    sent = [big["whin"], big["whout"].reshape(N_DEV, rows, D), big["wcin"], big["wcout"].reshape(N_DEV, rows, D),
            big["wup"][0], big["wup"][1], big["wdn"][0].reshape(N_DEV, frows, D), big["wdn"][1].reshape(N_DEV, frows, D)]
    r_whin, r_whout, r_wcin, r_wcout, r_wup0, r_wup1, r_wdn0, r_wdn1 = all_to_all(sent, "exchange_grads")
    G['hgrn_w_in'] = r_whin[:, None]
    G['hgrn_w_out'] = r_whout[:, None]
    G['conv_w_in'] = r_wcin[:, None]
    G['conv_w_out'] = r_wcout[:, None]
    G['ffn_w_up'] = jnp.stack([r_wup0, r_wup1], axis=1)
    G['ffn_w_down'] = jnp.stack([r_wdn0, r_wdn1], axis=1)

    res = {n: _apply_adamw(n, P[n], G[n], P["m_" + n], P["v_" + n]) for n in WEIGHTS}
    return (loss, dx0[None], *[res[n][0] for n in WEIGHTS], *[res[n][1] for n in WEIGHTS],
            *[res[n][2] for n in WEIGHTS], *[res[n][3] for n in WEIGHTS])
```

```python
import functools

import jax
import jax.numpy as jnp
from jax import lax
from jax.experimental import pallas as pl
from jax.experimental.pallas import tpu as pltpu

F32 = jnp.float32
BF16 = jnp.bfloat16
EPS = 1e-6
LANES = 128
N_DEV = 8
VMEM_LIMIT = 48 * 1024 * 1024
HG_CHUNK = 128
HG_SUB = 16
EXP_CLAMP = 80.0
CONV_HALO = 32
FFN_HALO = 8
CONV_ROWS = 32
ADAM_LR, ADAM_B1, ADAM_B2, ADAM_EPS, ADAM_WD, ADAM_STEP = 0.001, 0.9, 0.999, 1e-08, 0.01, 10
MESH = pl.DeviceIdType.MESH
ANY = pl.BlockSpec(memory_space=pl.ANY)


def _cp(*sem):
    return pltpu.CompilerParams(dimension_semantics=sem, vmem_limit_bytes=VMEM_LIMIT)


def _rt(n, t):
    t = min(n, t)
    assert n % t == 0, (n, t)
    return t


def _sig(x):
    return jax.nn.sigmoid(x)


def _nt(a, b):
    return lax.dot_general(a, b, (((1,), (1,)), ((), ())), preferred_element_type=F32)


def _tn(a, b):
    return lax.dot_general(a, b, (((0,), (0,)), ((), ())), preferred_element_type=F32)


def _nn(a, b):
    return jnp.dot(a, b, preferred_element_type=F32)


def _bf(x):
    return x.astype(BF16)


_HI = lax.Precision.HIGHEST


def _nn_hi(a, b):
    return jnp.dot(a, b, preferred_element_type=F32, precision=_HI)


def _nt_hi(a, b):
    return lax.dot_general(a, b, (((1,), (1,)), ((), ())), preferred_element_type=F32, precision=_HI)


def _tn_hi(a, b):
    return lax.dot_general(a, b, (((0,), (0,)), ((), ())), preferred_element_type=F32, precision=_HI)


def _row(d):
    return pl.BlockSpec((1, d), lambda *_: (0, 0))


def prenorm_mod(x, g, sc, sh, name):
    T, D = x.shape
    tm = _rt(T, 512)

    def body(x_ref, g_ref, sc_ref, sh_ref, h_ref):
        xv = x_ref[...]
        r = lax.rsqrt(jnp.mean(xv * xv, axis=-1, keepdims=True) + EPS)
        h_ref[...] = _bf((xv * r) * g_ref[...] * (1.0 + sc_ref[...]) + sh_ref[...])

    tile = pl.BlockSpec((tm, D), lambda i: (i, 0))
    return pl.pallas_call(
        body, name=name, grid=(T // tm,), in_specs=[tile, _row(D), _row(D), _row(D)], out_specs=tile,
        out_shape=jax.ShapeDtypeStruct((T, D), BF16), compiler_params=_cp("parallel"))(x, g, sc, sh)


def prenorm_bwd(dh, x, dxres, g, sc, name):
    T, D = x.shape
    tm = _rt(T, 512)

    def body(dh_ref, x_ref, dr_ref, g_ref, sc_ref, dx_ref, da_ref, dsh_ref):
        @pl.when(pl.program_id(0) == 0)
        def _():
            da_ref[...] = jnp.zeros_like(da_ref)
            dsh_ref[...] = jnp.zeros_like(dsh_ref)
        xv = x_ref[...]
        dh = dh_ref[...]
        r = lax.rsqrt(jnp.mean(xv * xv, axis=-1, keepdims=True) + EPS)
        n = xv * r
        da_ref[...] += jnp.sum(dh * n, axis=0, keepdims=True)
        dsh_ref[...] += jnp.sum(dh, axis=0, keepdims=True)
        dn = dh * (g_ref[...] * (1.0 + sc_ref[...]))
        dx_ref[...] = dr_ref[...] + r * (dn - n * jnp.mean(dn * n, axis=-1, keepdims=True))

    tile = pl.BlockSpec((tm, D), lambda i: (i, 0))
    vec = jax.ShapeDtypeStruct((1, D), F32)
    return pl.pallas_call(
        body, name=name, grid=(T // tm,), in_specs=[tile, tile, tile, _row(D), _row(D)],
        out_specs=[tile, _row(D), _row(D)], out_shape=[jax.ShapeDtypeStruct((T, D), F32), vec, vec],
        compiler_params=_cp("arbitrary"))(dh, x, dxres, g, sc)


def postnorm_bwd(dxo, y, gate, pg, name):
    T, D = y.shape
    tm = _rt(T, 512)

    def body(d_ref, y_ref, gt_ref, pg_ref, dy_ref, dgate_ref, dpg_ref, dys_ref):
        @pl.when(pl.program_id(0) == 0)
        def _():
            dgate_ref[...] = jnp.zeros_like(dgate_ref)
            dpg_ref[...] = jnp.zeros_like(dpg_ref)
            dys_ref[...] = jnp.zeros_like(dys_ref)
        yv = y_ref[...]
        d = d_ref[...]
        r = lax.rsqrt(jnp.mean(yv * yv, axis=-1, keepdims=True) + EPS)
        n = yv * r
        dgate_ref[...] += jnp.sum(d * n, axis=0, keepdims=True) * pg_ref[...]
        dpg_ref[...] += jnp.sum(d * n, axis=0, keepdims=True) * gt_ref[...]
        dn = d * (gt_ref[...] * pg_ref[...])
        dy = r * (dn - n * jnp.mean(dn * n, axis=-1, keepdims=True))
        dy_ref[...] = dy
        dys_ref[...] += jnp.sum(dy, axis=0, keepdims=True)

    tile = pl.BlockSpec((tm, D), lambda i: (i, 0))
    vec = jax.ShapeDtypeStruct((1, D), F32)
    return pl.pallas_call(
        body, name=name, grid=(T // tm,), in_specs=[tile, tile, _row(D), _row(D)],
        out_specs=[tile, _row(D), _row(D), _row(D)], out_shape=[jax.ShapeDtypeStruct((T, D), F32), vec, vec, vec],
        compiler_params=_cp("arbitrary"))(dxo, y, gate, pg)


def loss_grad(xo, tgt, name):
    T, D = xo.shape
    tm = _rt(T, 512)

    def body(x_ref, t_ref, dx_ref, part_ref):
        @pl.when(pl.program_id(0) == 0)
        def _():
            part_ref[...] = jnp.zeros_like(part_ref)
        e = x_ref[...] - t_ref[...]
        dx_ref[...] = e * (1.0 / D)
        e2 = (e * e).reshape(tm // 8, 8, D).sum(axis=0)
        acc = e2[:, 0:LANES]
        for j in range(1, D // LANES):
            acc = acc + e2[:, j * LANES:(j + 1) * LANES]
        part_ref[...] += acc

    tile = pl.BlockSpec((tm, D), lambda i: (i, 0))
    return pl.pallas_call(
        body, name=name, grid=(T // tm,), in_specs=[tile, tile],
        out_specs=[tile, pl.BlockSpec((8, LANES), lambda i: (0, 0))],
        out_shape=[jax.ShapeDtypeStruct((T, D), F32), jax.ShapeDtypeStruct((8, LANES), F32)],
        compiler_params=_cp("arbitrary"))(xo, tgt)


def _call(body, name, grid, in_specs, out_specs, out_shape, scratch, sems, operands, ex=None):
    if ex is not None:
        def first():
            return functools.reduce(lambda p, q: p & q, [pl.program_id(k) == 0 for k in range(len(grid))])

        def last():
            return functools.reduce(lambda p, q: p & q, [pl.program_id(k) == grid[k] - 1 for k in range(len(grid))])
        body = _ride(body, len(in_specs), len(out_specs), ex, first, last)
        in_specs = in_specs + [ANY] * len(ex.ins)
        out_specs = out_specs + [ANY] * len(ex.out_shapes)
        out_shape = out_shape + ex.out_shapes
        scratch = scratch + ex.scratch
        operands = list(operands) + ex.ins
        sems = ("arbitrary",) * len(grid)
    res = pl.pallas_call(body, name=name, grid=grid, in_specs=in_specs, out_specs=out_specs, out_shape=out_shape,
                         scratch_shapes=scratch, compiler_params=_cp(*sems))(*operands)
    n_own = len(res) - (len(ex.out_shapes) if ex is not None else 0)
    return res[:n_own], res[n_own:]


def mm_nn(a, bg, bias, name, ex=None):
    M, K = a.shape
    G, _, n = bg.shape
    tm = _rt(M, 512)

    def body(a_ref, b_ref, bias_ref, o_ref):
        o_ref[...] = _nn(a_ref[...], b_ref[...]) + bias_ref[...]

    (out,), rider = _call(
        body, name, (M // tm, G),
        [pl.BlockSpec((tm, K), lambda i, j: (i, 0)), pl.BlockSpec((None, K, n), lambda i, j: (j, 0, 0)),
         pl.BlockSpec((1, n), lambda i, j: (0, j))],
        [pl.BlockSpec((tm, n), lambda i, j: (i, j))], [jax.ShapeDtypeStruct((M, G * n), F32)], [],
        ("parallel", "arbitrary"), (a, bg, bias), ex)
    return (out, rider) if ex is not None else out


def mm_post(a, b, bias, x, gate, pg, name):
    T, K = a.shape
    D = b.shape[1]
    tm = _rt(T, 512)

    def body(a_ref, b_ref, bias_ref, x_ref, gt_ref, pg_ref, y_ref, xo_ref):
        y = _nn(a_ref[...], b_ref[...]) + bias_ref[...]
        y_ref[...] = y
        r = lax.rsqrt(jnp.mean(y * y, axis=-1, keepdims=True) + EPS)
        xo_ref[...] = x_ref[...] + gt_ref[...] * ((y * r) * pg_ref[...])

    tile = pl.BlockSpec((tm, D), lambda i: (i, 0))
    out = jax.ShapeDtypeStruct((T, D), F32)
    return pl.pallas_call(
        body, name=name, grid=(T // tm,),
        in_specs=[pl.BlockSpec((tm, K), lambda i: (i, 0)), pl.BlockSpec((K, D), lambda i: (0, 0)), _row(D), tile,
                  _row(D), _row(D)],
        out_specs=[tile, tile], out_shape=[out, out], compiler_params=_cp("parallel"))(a, b, bias, x, gate, pg)


def mm_nt(dy, wg, name, natural):
    G, K, n = wg.shape
    T = dy.shape[0] if natural else dy.shape[1]
    tm = _rt(T, 512)

    def body(dy_ref, w_ref, o_ref, acc):
        j = pl.program_id(1)

        @pl.when(j == 0)
        def _():
            acc[...] = jnp.zeros_like(acc)
        acc[...] += _nt(_bf(dy_ref[...]), w_ref[...])

        @pl.when(j == G - 1)
        def _():
            o_ref[...] = acc[...]

    dspec = (pl.BlockSpec((tm, n), lambda i, j: (i, j)) if natural
             else pl.BlockSpec((None, tm, n), lambda i, j: (j, i, 0)))
    return pl.pallas_call(
        body, name=name, grid=(T // tm, G),
        in_specs=[dspec, pl.BlockSpec((None, K, n), lambda i, j: (j, 0, 0))],
        out_specs=pl.BlockSpec((tm, K), lambda i, j: (i, 0)), out_shape=jax.ShapeDtypeStruct((T, K), F32),
        scratch_shapes=[pltpu.VMEM((tm, K), F32)], compiler_params=_cp("parallel", "arbitrary"))(dy, wg)


def mm_tn(xm, gm, name, x_natural, g_natural, kx, n):
    if x_natural:
        T, Gx = xm.shape[0], xm.shape[1] // kx
    else:
        Gx, T = xm.shape[0], xm.shape[1]
    tt = _rt(T, 512)
    if x_natural:
        xspec = pl.BlockSpec((tt, kx), lambda g, k, t: (t, k))
    else:
        xspec = pl.BlockSpec((None, tt, kx), lambda g, k, t: (k, t, 0))
    if g_natural:
        Gg = gm.shape[1] // n
        gspec = pl.BlockSpec((tt, n), lambda g, k, t: (t, g))
    else:
        Gg = gm.shape[0]
        gspec = pl.BlockSpec((None, tt, n), lambda g, k, t: (g, t, 0))

    NT = T // tt

    def body(x_ref, g_ref, o_ref, acc):
        t = pl.program_id(2)

        @pl.when(t == 0)
        def _():
            acc[...] = jnp.zeros_like(acc)
        acc[...] += _tn(_bf(x_ref[...]), _bf(g_ref[...]))

        @pl.when(t == NT - 1)
        def _():
            o_ref[...] = acc[...].astype(o_ref.dtype)

    return pl.pallas_call(
        body, name=name, grid=(Gg, Gx, NT), in_specs=[xspec, gspec],
        out_specs=pl.BlockSpec((None, kx, n), lambda g, k, t: (g, k, 0)),
        out_shape=jax.ShapeDtypeStruct((Gg, Gx * kx, n), BF16), scratch_shapes=[pltpu.VMEM((kx, n), F32)],
        compiler_params=_cp("parallel", "parallel", "arbitrary"))(xm, gm)


def _split3(x):
    h = _bf(x)
    r = x - h.astype(F32)
    m = _bf(r)
    lo = _bf(r - m.astype(F32))
    return h, m, lo


def _tri_mm(tri, x):
    h, m, lo = _split3(x)
    return _nn(tri, lo) + _nn(tri, m) + _nn(tri, h)


def _hgrn_gates(qr, fz, lb):
    sq = _sig(qr)
    q = qr * sq
    sig = _sig(fz)
    f = lb + (1.0 - lb) * sig
    k = 1.0 - f
    return sq, q, sig, f, k, jnp.log(f)


def hgrn_fwd(proj, lb, gg, name, ex=None):
    T, D4 = proj.shape
    D = D4 // 4
    H = D // LANES
    C = _rt(T, HG_CHUNK)
    SB = min(HG_SUB, C)
    NC = T // C

    def body(q_ref, f_ref, v_ref, g_ref, lb_ref, gg_ref, o_ref, og_ref, st_ref, ST, bex):
        @pl.when(pl.program_id(0) == 0)
        def _():
            ST[...] = jnp.zeros_like(ST)
        r_i = lax.broadcasted_iota(jnp.int32, (C, C), 0)
        c_i = lax.broadcasted_iota(jnp.int32, (C, C), 1)
        low = _bf((r_i >= c_i).astype(F32))
        rs = lax.broadcasted_iota(jnp.int32, (SB, C), 0)
        cs = lax.broadcasted_iota(jnp.int32, (SB, C), 1)
        for hd in range(H):
            sl = slice(hd * LANES, (hd + 1) * LANES)
            _, q, _, _, k, lf = _hgrn_gates(q_ref[:, sl], f_ref[:, sl], lb_ref[:, sl])
            v = v_ref[:, sl]
            b = _tri_mm(low, lf)
            bex[hd] = b - lf
            bend = jnp.sum(lf, axis=0, keepdims=True)
            blocks = []
            for I in range(C // SB):
                p = bex[hd, pl.ds(SB * I, 1), :]
                rows = slice(SB * I, SB * (I + 1))
                qI = _bf(q[rows] * jnp.exp(b[rows] - p))
                kI = _bf(k * jnp.exp(jnp.minimum(p - b, EXP_CLAMP)))
                blocks.append(jnp.where(rs + SB * I >= cs, _nt(qI, kI), 0.0))
            A = jnp.concatenate(blocks, axis=0)
            st0 = ST[hd]
            st_ref[0, hd] = st0
            o = _nn(_bf(A), _bf(v)) + _nt(_bf(q * jnp.exp(b)), _bf(st0))
            ST[hd] = st0 * jnp.exp(bend) + _tn_hi(v, k * jnp.exp(bend - b))
            o_ref[:, sl] = o
            r = lax.rsqrt(jnp.mean(o * o, axis=-1, keepdims=True) + EPS)
            gr = g_ref[:, sl]
            og_ref[:, sl] = _bf((o * r) * gg_ref[...] * (gr * _sig(gr)))

    col = lambda j: pl.BlockSpec((C, D), lambda c, j=j: (c, j))
    tile = pl.BlockSpec((C, D), lambda c: (c, 0))
    own, rider = _call(
        body, name, (NC,), [col(0), col(1), col(2), col(3), _row(D), _row(LANES)],
        [tile, tile, pl.BlockSpec((1, H, LANES, LANES), lambda c: (c, 0, 0, 0))],
        [jax.ShapeDtypeStruct((T, D), F32), jax.ShapeDtypeStruct((T, D), BF16),
         jax.ShapeDtypeStruct((NC, H, LANES, LANES), F32)],
        [pltpu.VMEM((H, LANES, LANES), F32), pltpu.VMEM((H, C, LANES), F32)], ("arbitrary",),
        (proj, proj, proj, proj, lb, gg), ex)
    return (*own, rider) if ex is not None else own


def hgrn_bwd(proj, o, dog, st, lb, gg, name, ex=None):
    T, D4 = proj.shape
    D = D4 // 4
    H = D // LANES
    C = _rt(T, HG_CHUNK)
    SB = min(HG_SUB, C)
    NC = T // C

    def body(q_ref, f_ref, v_ref, g_ref, o_ref, dog_ref, st_ref, lb_ref, gg_ref, dp_ref, dlb_ref, dgg_ref,
             DST, RS, bex):
        @pl.when(pl.program_id(0) == 0)
        def _():
            DST[...] = jnp.zeros_like(DST)
            RS[...] = jnp.zeros_like(RS)
            dlb_ref[...] = jnp.zeros_like(dlb_ref)
            dgg_ref[...] = jnp.zeros_like(dgg_ref)
        r_i = lax.broadcasted_iota(jnp.int32, (C, C), 0)
        c_i = lax.broadcasted_iota(jnp.int32, (C, C), 1)
        causal = r_i >= c_i
        low = _bf(causal.astype(F32))
        upp = _bf((r_i <= c_i).astype(F32))
        rs = lax.broadcasted_iota(jnp.int32, (SB, C), 0)
        cs = lax.broadcasted_iota(jnp.int32, (SB, C), 1)
        ggv = gg_ref[...]
        for hd in range(H):
            sl = slice(hd * LANES, (hd + 1) * LANES)
            qr = q_ref[:, sl]
            lbv = lb_ref[:, sl]
            sq, q, sig, f, k, lf = _hgrn_gates(qr, f_ref[:, sl], lbv)
            v = v_ref[:, sl]
            oh = o_ref[:, sl]
            r = lax.rsqrt(jnp.mean(oh * oh, axis=-1, keepdims=True) + EPS)
            n = oh * r
            gr = g_ref[:, sl]
            sg = _sig(gr)
            dgo = dog_ref[:, sl]
            dgr = dgo * (n * ggv) * (sg * (1.0 + gr * (1.0 - sg)))
            don = dgo * (gr * sg)
            dgg_ref[...] += jnp.sum(don * n, axis=0, keepdims=True)
            dn = don * ggv
            do = r * (dn - n * jnp.mean(dn * n, axis=-1, keepdims=True))
            b = _tri_mm(low, lf)
            bex[hd] = b - lf
            bend = jnp.sum(lf, axis=0, keepdims=True)
            dob = _bf(do)
            dA = jnp.where(causal, _nt_hi(do, v), 0.0)
            blocks, dqs = [], []
            dk = jnp.zeros((C, LANES), F32)
            for I in range(C // SB):
                p = bex[hd, pl.ds(SB * I, 1), :]
                rows = slice(SB * I, SB * (I + 1))
                eq = jnp.exp(b[rows] - p)
                qI = q[rows] * eq
                ek = jnp.exp(jnp.minimum(p - b, EXP_CLAMP))
                kI = k * ek
                blocks.append(jnp.where(rs + SB * I >= cs, _nt(_bf(qI), _bf(kI)), 0.0))
                dqs.append(_nn_hi(dA[rows], kI) * eq)
                dk = dk + _tn_hi(dA[rows], qI) * ek
            A = jnp.concatenate(blocks, axis=0)
            dst = DST[hd]
            eb = jnp.exp(b)
            ee = jnp.exp(bend - b)
            dv = _tn(_bf(A), dob) + _nt(_bf(k * ee), _bf(dst))
            dq = jnp.concatenate(dqs, axis=0) + eb * _nn_hi(do, st_ref[0, hd])
            dk = dk + ee * _nn_hi(v, dst)
            DST[hd] = dst * jnp.exp(bend) + _tn_hi(do, q * eb)
            gsum = q * dq - k * dk
            dlf = _tri_mm(upp, gsum) + RS[:, sl]
            RS[:, sl] += jnp.sum(gsum, axis=0, keepdims=True)
            df = dlf / f - dk
            dlb_ref[:, sl] += jnp.sum(df * (1.0 - sig), axis=0, keepdims=True)
            dp_ref[:, sl] = _bf(dq * (sq * (1.0 + qr * (1.0 - sq))))
            dp_ref[:, D + hd * LANES:D + (hd + 1) * LANES] = _bf(df * (1.0 - lbv) * (sig * (1.0 - sig)))
            dp_ref[:, 2 * D + hd * LANES:2 * D + (hd + 1) * LANES] = _bf(dv)
            dp_ref[:, 3 * D + hd * LANES:3 * D + (hd + 1) * LANES] = _bf(dgr)

    col = lambda j: pl.BlockSpec((C, D), lambda c, j=j: (NC - 1 - c, j))
    tile = pl.BlockSpec((C, D), lambda c: (NC - 1 - c, 0))
    own, rider = _call(
        body, name, (NC,),
        [col(0), col(1), col(2), col(3), tile, tile,
         pl.BlockSpec((1, H, LANES, LANES), lambda c: (NC - 1 - c, 0, 0, 0)), _row(D), _row(LANES)],
        [pl.BlockSpec((C, D4), lambda c: (NC - 1 - c, 0)), _row(D), _row(LANES)],
        [jax.ShapeDtypeStruct((T, D4), BF16), jax.ShapeDtypeStruct((1, D), F32), jax.ShapeDtypeStruct((1, LANES), F32)],
        [pltpu.VMEM((H, LANES, LANES), F32), pltpu.VMEM((1, D), F32), pltpu.VMEM((H, C, LANES), F32)],
        ("arbitrary",), (proj, proj, proj, proj, o, dog, st, lb, gg), ex)
    return (*own, rider) if ex is not None else own


def _prev_blk(tm, hb):
    return lambda i: (jnp.maximum(i * (tm // hb) - 1, 0), 0)


def _next_blk(tm, hb, T):
    return lambda i: (jnp.minimum((i + 1) * (tm // hb), T // hb - 1), 0)


def _glu_to_ext(uc_ref, ucp_ref, ext, D, first):
    def glu(u):
        return u[:, :D] * _sig(u[:, D:])
    gh = jnp.where(first, 0.0, glu(ucp_ref[...]))
    gm = glu(uc_ref[...])
    for c in range(D // LANES):
        ext[c, 0:CONV_HALO, :] = gh[:, c * LANES:(c + 1) * LANES]
        ext[c, CONV_HALO:, :] = gm[:, c * LANES:(c + 1) * LANES]


def conv_mid_fwd(uc, cw, cb, lg, lbias, name):
    T, D2 = uc.shape
    D = D2 // 2
    NCH = D // LANES
    W = 31
    tm = _rt(T, 256)
    RS = min(CONV_ROWS, tm)

    def body(uc_ref, ucp_ref, cw_ref, cb_ref, lg_ref, lb_ref, cv_ref, va_ref, ext, outs):
        _glu_to_ext(uc_ref, ucp_ref, ext, D, pl.program_id(0) == 0)

        def chunk(c, carry):
            for rb in range(tm // RS):
                acc = jnp.zeros((RS, LANES), F32)
                for kk in range(W):
                    acc = acc + cw_ref[c, pl.ds(kk, 1), :] * ext[c, pl.ds(rb * RS + CONV_HALO - (W - 1) + kk, RS), :]
                outs[c, pl.ds(rb * RS, RS), :] = acc
            return carry
        lax.fori_loop(0, NCH, chunk, 0)
        cv = jnp.concatenate([outs[c] for c in range(NCH)], axis=1) + cb_ref[...]
        cv_ref[...] = cv
        mu = jnp.mean(cv, axis=-1, keepdims=True)
        xc = cv - mu
        rstd = lax.rsqrt(jnp.mean(xc * xc, axis=-1, keepdims=True) + EPS)
        l = xc * rstd * lg_ref[...] + lb_ref[...]
        va_ref[...] = _bf(l * _sig(l))

    tile = pl.BlockSpec((tm, D), lambda i: (i, 0))
    return pl.pallas_call(
        body, name=name, grid=(T // tm,),
        in_specs=[pl.BlockSpec((tm, D2), lambda i: (i, 0)), pl.BlockSpec((CONV_HALO, D2), _prev_blk(tm, CONV_HALO)),
                  pl.BlockSpec((NCH, 32, LANES), lambda i: (0, 0, 0)), _row(D), _row(D), _row(D)],
        out_specs=[tile, tile],
        out_shape=[jax.ShapeDtypeStruct((T, D), F32), jax.ShapeDtypeStruct((T, D), BF16)],
        scratch_shapes=[pltpu.VMEM((NCH, tm + CONV_HALO, LANES), F32), pltpu.VMEM((NCH, tm, LANES), F32)],
        compiler_params=_cp("parallel"))(uc, uc, cw, cb, lg, lbias)


def conv_bwd_a(dva, cv, uc, lg, lbias, name):
    T, D2 = uc.shape
    D = D2 // 2
    NCH = D // LANES
    W = 31
    tm = _rt(T, 256)
    RS = min(CONV_ROWS, tm)

    def body(dva_ref, cv_ref, uc_ref, ucp_ref, lg_ref, lb_ref, dcv_ref, dlg_ref, dlb_ref, taps_ref, ext, dcs):
        @pl.when(pl.program_id(0) == 0)
        def _():
            dlg_ref[...] = jnp.zeros_like(dlg_ref)
            dlb_ref[...] = jnp.zeros_like(dlb_ref)
            taps_ref[...] = jnp.zeros_like(taps_ref)
        _glu_to_ext(uc_ref, ucp_ref, ext, D, pl.program_id(0) == 0)
        cv = cv_ref[...]
        mu = jnp.mean(cv, axis=-1, keepdims=True)
        xc = cv - mu
        rstd = lax.rsqrt(jnp.mean(xc * xc, axis=-1, keepdims=True) + EPS)
        yn = xc * rstd
        l = yn * lg_ref[...] + lb_ref[...]
        s = _sig(l)
        dl = dva_ref[...] * (s * (1.0 + l * (1.0 - s)))
        dlg_ref[...] += jnp.sum(dl * yn, axis=0, keepdims=True)
        dlb_ref[...] += jnp.sum(dl, axis=0, keepdims=True)
        dyn = dl * lg_ref[...]
        dcv = rstd * (dyn - jnp.mean(dyn, axis=-1, keepdims=True) - yn * jnp.mean(dyn * yn, axis=-1, keepdims=True))
        dcv_ref[...] = dcv
        for c in range(NCH):
            dcs[c] = dcv[:, c * LANES:(c + 1) * LANES]

        def fold(p):
            return p.reshape(RS // 8, 8, LANES).sum(axis=0)

        def chunk(c, carry):
            for kk in range(W):
                tot = jnp.zeros((8, LANES), F32)
                for rb in range(tm // RS):
                    tot = tot + fold(dcs[c, pl.ds(rb * RS, RS), :]
                                     * ext[c, pl.ds(rb * RS + CONV_HALO - (W - 1) + kk, RS), :])
                taps_ref[c, pl.ds(8 * kk, 8), :] += tot
            tot = jnp.zeros((8, LANES), F32)
            for rb in range(tm // RS):
                tot = tot + fold(dcs[c, pl.ds(rb * RS, RS), :])
            taps_ref[c, pl.ds(8 * W, 8), :] += tot
            return carry
        lax.fori_loop(0, NCH, chunk, 0)

    tile = pl.BlockSpec((tm, D), lambda i: (i, 0))
    vec = jax.ShapeDtypeStruct((1, D), F32)
    return pl.pallas_call(
        body, name=name, grid=(T // tm,),
        in_specs=[tile, tile, pl.BlockSpec((tm, D2), lambda i: (i, 0)),
                  pl.BlockSpec((CONV_HALO, D2), _prev_blk(tm, CONV_HALO)), _row(D), _row(D)],
        out_specs=[tile, _row(D), _row(D), pl.BlockSpec((NCH, 256, LANES), lambda i: (0, 0, 0))],
        out_shape=[jax.ShapeDtypeStruct((T, D), F32), vec, vec, jax.ShapeDtypeStruct((NCH, 256, LANES), F32)],
        scratch_shapes=[pltpu.VMEM((NCH, tm + CONV_HALO, LANES), F32), pltpu.VMEM((NCH, tm, LANES), F32)],
        compiler_params=_cp("arbitrary"))(dva, cv, uc, uc, lg, lbias)


def conv_bwd_b(dcv, uc, cw, name):
    T, D2 = uc.shape
    D = D2 // 2
    NCH = D // LANES
    W = 31
    tm = _rt(T, 256)
    RS = min(CONV_ROWS, tm)
    NT = T // tm

    def body(dcv_ref, dcn_ref, uc_ref, cw_ref, du_ref, db_ref, ext, outs):
        i = pl.program_id(0)

        @pl.when(i == 0)
        def _():
            db_ref[...] = jnp.zeros_like(db_ref)
        dm = dcv_ref[...]
        dn = jnp.where(i == NT - 1, 0.0, dcn_ref[...])
        for c in range(NCH):
            ext[c, 0:tm, :] = dm[:, c * LANES:(c + 1) * LANES]
            ext[c, tm:, :] = dn[:, c * LANES:(c + 1) * LANES]

        def chunk(c, carry):
            for rb in range(tm // RS):
                acc = jnp.zeros((RS, LANES), F32)
                for kk in range(W):
                    acc = acc + cw_ref[c, pl.ds(kk, 1), :] * ext[c, pl.ds(rb * RS + (W - 1) - kk, RS), :]
                outs[c, pl.ds(rb * RS, RS), :] = acc
            return carry
        lax.fori_loop(0, NCH, chunk, 0)
        dglu = jnp.concatenate([outs[c] for c in range(NCH)], axis=1)
        u = uc_ref[...]
        a = u[:, :D]
        sg = _sig(u[:, D:])
        da = dglu * sg
        dg = dglu * a * (sg * (1.0 - sg))
        du_ref[:, :D] = _bf(da)
        du_ref[:, D:] = _bf(dg)
        db_ref[:, :D] += jnp.sum(da, axis=0, keepdims=True)
        db_ref[:, D:] += jnp.sum(dg, axis=0, keepdims=True)

    tile = pl.BlockSpec((tm, D), lambda i: (i, 0))
    wide = pl.BlockSpec((tm, D2), lambda i: (i, 0))
    return pl.pallas_call(
        body, name=name, grid=(NT,),
        in_specs=[tile, pl.BlockSpec((CONV_HALO, D), _next_blk(tm, CONV_HALO, T)), wide,
                  pl.BlockSpec((NCH, 32, LANES), lambda i: (0, 0, 0))],
        out_specs=[wide, _row(D2)],
        out_shape=[jax.ShapeDtypeStruct((T, D2), BF16), jax.ShapeDtypeStruct((1, D2), F32)],
        scratch_shapes=[pltpu.VMEM((NCH, tm + CONV_HALO, LANES), F32), pltpu.VMEM((NCH, tm, LANES), F32)],
        compiler_params=_cp("arbitrary"))(dcv, dcv, uc, cw)


def _conv3(ext, u, uh, dw_ref, bias_ref, tm):
    ext[0:FFN_HALO, :] = uh
    ext[FFN_HALO:, :] = u
    return (dw_ref[pl.ds(0, 1), :] * ext[pl.ds(FFN_HALO - 2, tm), :]
            + dw_ref[pl.ds(1, 1), :] * ext[pl.ds(FFN_HALO - 1, tm), :]
            + dw_ref[pl.ds(2, 1), :] * u + bias_ref[...])


def ffn_fwd(h, wup, dww, dwb, wdn, x, gate, pg, name):
    T, D = h.shape
    fs = wup.shape[2]
    NJ = wup.shape[0] // 2
    tm = _rt(T, 512)

    def body(h_ref, hp_ref, wa_ref, wb_ref, dwa_ref, dwb_ref, ba_ref, bb_ref, wd_ref, x_ref, gt_ref, pg_ref,
             u_ref, uc_ref, y_ref, xo_ref, acc, exta, extb):
        i, j = pl.program_id(0), pl.program_id(1)

        @pl.when(j == 0)
        def _():
            acc[...] = jnp.zeros_like(acc)
        hb = h_ref[...]
        hp = hp_ref[...]

        def half(s, w_ref, dw_ref, b_ref, ext):
            u = _nn(hb, w_ref[...])
            uh = jnp.where(i == 0, 0.0, _nn(hp, w_ref[...]))
            u_ref[s] = u
            uc = _conv3(ext, u, uh, dw_ref, b_ref, tm)
            uc_ref[s] = uc
            return uc
        a = half(0, wa_ref, dwa_ref, ba_ref, exta)
        b = half(1, wb_ref, dwb_ref, bb_ref, extb)
        acc[...] += _nn(_bf(a * _sig(a) * b), wd_ref[...])

        @pl.when(j == NJ - 1)
        def _():
            y = acc[...]
            y_ref[...] = y
            r = lax.rsqrt(jnp.mean(y * y, axis=-1, keepdims=True) + EPS)
            xo_ref[...] = x_ref[...] + gt_ref[...] * ((y * r) * pg_ref[...])

    tile = pl.BlockSpec((tm, D), lambda i, j: (i, 0))
    shard = lambda off, r: pl.BlockSpec((None, r, fs), lambda i, j: (j + off, 0, 0))
    ush = pl.BlockSpec((2, None, tm, fs), lambda i, j: (0, j, i, 0))
    vec = pl.BlockSpec((1, D), lambda i, j: (0, 0))
    return pl.pallas_call(
        body, name=name, grid=(T // tm, NJ),
        in_specs=[tile, pl.BlockSpec((FFN_HALO, D), lambda i, j: (jnp.maximum(i * (tm // FFN_HALO) - 1, 0), 0)),
                  shard(0, D), shard(NJ, D), shard(0, 3), shard(NJ, 3), shard(0, 1), shard(NJ, 1),
                  pl.BlockSpec((None, fs, D), lambda i, j: (j, 0, 0)), tile, vec, vec],
        out_specs=[ush, ush, tile, tile],
        out_shape=[jax.ShapeDtypeStruct((2, NJ, T, fs), F32), jax.ShapeDtypeStruct((2, NJ, T, fs), F32),
                   jax.ShapeDtypeStruct((T, D), F32), jax.ShapeDtypeStruct((T, D), F32)],
        scratch_shapes=[pltpu.VMEM((tm, D), F32), pltpu.VMEM((tm + FFN_HALO, fs), F32),
                        pltpu.VMEM((tm + FFN_HALO, fs), F32)],
        compiler_params=_cp("parallel", "arbitrary"))(h, h, wup, wup, dww, dww, dwb, dwb, wdn, x, gate, pg)


def ffn_bwd_a(dy, uc, wdn, name):
    _, NJ, T, fs = uc.shape
    D = dy.shape[1]
    tm = _rt(T, 512)

    def body(dy_ref, uc_ref, wd_ref, z_ref, duc_ref):
        dz = _nt(_bf(dy_ref[...]), wd_ref[...])
        a = uc_ref[0]
        b = uc_ref[1]
        sa = _sig(a)
        asa = a * sa
        z_ref[...] = _bf(asa * b)
        duc_ref[0] = dz * b * (sa + asa * (1.0 - sa))
        duc_ref[1] = dz * asa

    ush = pl.BlockSpec((2, None, tm, fs), lambda j, i: (0, j, i, 0))
    return pl.pallas_call(
        body, name=name, grid=(NJ, T // tm),
        in_specs=[pl.BlockSpec((tm, D), lambda j, i: (i, 0)), ush, pl.BlockSpec((None, fs, D), lambda j, i: (j, 0, 0))],
        out_specs=[pl.BlockSpec((None, tm, fs), lambda j, i: (j, i, 0)), ush],
        out_shape=[jax.ShapeDtypeStruct((NJ, T, fs), BF16), jax.ShapeDtypeStruct((2, NJ, T, fs), F32)],
        compiler_params=_cp("parallel", "parallel"))(dy, uc, wdn)


def ffn_bwd_b(duc, u, dww, wup, name):
    NS, T, fs = duc.shape
    D = wup.shape[1]
    tm = _rt(T, 512)
    NT = T // tm

    def body(d_ref, dn_ref, u_ref, dw_ref, w_ref, du_ref, dh_ref, g_ref, acc, ext):
        i, s = pl.program_id(0), pl.program_id(1)

        @pl.when((i == 0) & (s == 0))
        def _():
            g_ref[...] = jnp.zeros_like(g_ref)

        @pl.when(s == 0)
        def _():
            acc[...] = jnp.zeros_like(acc)
        d = d_ref[...]
        ext[0:tm, :] = d
        ext[tm:, :] = jnp.where(i == NT - 1, 0.0, dn_ref[...])
        d1 = ext[pl.ds(1, tm), :]
        d2 = ext[pl.ds(2, tm), :]
        du = _bf(dw_ref[pl.ds(2, 1), :] * d + dw_ref[pl.ds(1, 1), :] * d1 + dw_ref[pl.ds(0, 1), :] * d2)
        du_ref[...] = du
        acc[...] += _nt(du, w_ref[...])
        u = u_ref[...]
        g_ref[s, pl.ds(0, 1), :] += jnp.sum(d2 * u, axis=0, keepdims=True)
        g_ref[s, pl.ds(1, 1), :] += jnp.sum(d1 * u, axis=0, keepdims=True)
        g_ref[s, pl.ds(2, 1), :] += jnp.sum(d * u, axis=0, keepdims=True)
        g_ref[s, pl.ds(3, 1), :] += jnp.sum(d, axis=0, keepdims=True)

        @pl.when(s == NS - 1)
        def _():
            dh_ref[...] = acc[...]

    ush = pl.BlockSpec((None, tm, fs), lambda i, s: (s, i, 0))
    unext = pl.BlockSpec((None, FFN_HALO, fs),
                         lambda i, s: (s, jnp.minimum((i + 1) * (tm // FFN_HALO), T // FFN_HALO - 1), 0))
    tile = pl.BlockSpec((tm, D), lambda i, s: (i, 0))
    return pl.pallas_call(
        body, name=name, grid=(NT, NS),
        in_specs=[ush, unext, ush, pl.BlockSpec((None, 3, fs), lambda i, s: (s, 0, 0)),
                  pl.BlockSpec((None, D, fs), lambda i, s: (s, 0, 0))],
        out_specs=[ush, tile, pl.BlockSpec((NS, 8, fs), lambda i, s: (0, 0, 0))],
        out_shape=[jax.ShapeDtypeStruct((NS, T, fs), BF16), jax.ShapeDtypeStruct((T, D), F32),
                   jax.ShapeDtypeStruct((NS, 8, fs), F32)],
        scratch_shapes=[pltpu.VMEM((tm, D), F32), pltpu.VMEM((tm + FFN_HALO, fs), F32)],
        compiler_params=_cp("arbitrary", "arbitrary"))(duc, duc, u, dww, wup)


def ada_fwd(c_all, w, bias, name):
    L, D, n = w.shape

    def body(c_ref, w_ref, b_ref, o_ref):
        cv = c_ref[...]
        o_ref[...] = _nn(_bf(cv * _sig(cv)), _bf(w_ref[...])) + b_ref[...]

    return pl.pallas_call(
        body, name=name, grid=(L,),
        in_specs=[pl.BlockSpec((N_DEV, D), lambda l: (0, 0)), pl.BlockSpec((None, D, n), lambda l: (l, 0, 0)),
                  pl.BlockSpec((None, 1, n), lambda l: (l, 0, 0))],
        out_specs=pl.BlockSpec((None, N_DEV, n), lambda l: (l, 0, 0)),
        out_shape=jax.ShapeDtypeStruct((L, N_DEV, n), F32), compiler_params=_cp("parallel"))(c_all, w, bias)


def ada_bwd(c_all, dm, name):
    L, _, n = dm.shape
    D = c_all.shape[1]

    def body(c_ref, d_ref, o_ref):
        cv = c_ref[...]
        o_ref[...] = _tn(_bf(cv * _sig(cv)), _bf(d_ref[...]))

    return pl.pallas_call(
        body, name=name, grid=(L,),
        in_specs=[pl.BlockSpec((N_DEV, D), lambda l: (0, 0)), pl.BlockSpec((None, N_DEV, n), lambda l: (l, 0, 0))],
        out_specs=pl.BlockSpec((None, D, n), lambda l: (l, 0, 0)),
        out_shape=jax.ShapeDtypeStruct((L, D, n), F32), compiler_params=_cp("parallel"))(c_all, dm)


def sum_slots(g, name):
    S, R, C = g.shape

    def body(g_ref, o_ref):
        acc = g_ref[0]
        for s in range(1, S):
            acc = acc + g_ref[s]
        o_ref[...] = acc

    return pl.pallas_call(
        body, name=name, grid=(1,), in_specs=[pl.BlockSpec((S, R, C), lambda i: (0, 0, 0))],
        out_specs=pl.BlockSpec((R, C), lambda i: (0, 0)), out_shape=jax.ShapeDtypeStruct((R, C), F32),
        compiler_params=_cp("arbitrary"))(g)


def lb_softmax(logits, name):
    def body(l_ref, s_ref):
        l = l_ref[...]
        e = jnp.exp(l - jnp.max(l, axis=0, keepdims=True))
        s_ref[...] = e / jnp.sum(e, axis=0, keepdims=True)
    return pl.pallas_call(body, name=name, out_shape=jax.ShapeDtypeStruct(logits.shape, F32))(logits)


def adamw(w, gs, m, v, name):
    S, R, C = gs.shape
    tr = R
    budget = (4 * 1024 * 1024) // (4 * (S + 7) * C)
    if R > budget:
        tr = max(t for t in range(16, budget + 1, 16) if R % t == 0)

    def body(w_ref, g_ref, m_ref, v_ref, go_ref, d_ref, mo_ref, vo_ref):
        g = g_ref[0].astype(F32)
        for s in range(1, S):
            g = g + g_ref[s].astype(F32)
        go_ref[...] = g
        mn = ADAM_B1 * m_ref[...] + (1.0 - ADAM_B1) * g
        vn = ADAM_B2 * v_ref[...] + (1.0 - ADAM_B2) * (g * g)
        mo_ref[...] = mn
        vo_ref[...] = vn
        m_hat = mn / (1.0 - ADAM_B1 ** ADAM_STEP)
        v_hat = vn / (1.0 - ADAM_B2 ** ADAM_STEP)
        d_ref[...] = -ADAM_LR * (m_hat / (jnp.sqrt(v_hat) + ADAM_EPS) + ADAM_WD * w_ref[...])

    tile = pl.BlockSpec((tr, C), lambda i: (i, 0))
    out = jax.ShapeDtypeStruct((R, C), F32)
    return pl.pallas_call(
        body, name=name, grid=(R // tr,), in_specs=[tile, pl.BlockSpec((S, tr, C), lambda i: (0, i, 0)), tile, tile],
        out_specs=[tile] * 4, out_shape=[out] * 4, compiler_params=_cp("parallel"))(w, gs, m, v)


def _place():
    return lax.axis_index("x"), lax.axis_index("y"), lax.axis_index("c")


def _slot(p):
    return 4 * p[0] + 2 * p[1] + p[2]


class _Gather:
    def __init__(self, xs):
        self.ins = list(xs)
        n = self.n = len(xs)
        self.out_shapes = [jax.ShapeDtypeStruct((N_DEV,) + a.shape, a.dtype) for a in xs]
        self.scratch = [pltpu.SemaphoreType.DMA((n, 7)), pltpu.SemaphoreType.DMA((n, 7)), pltpu.SemaphoreType.DMA((n,))]

    def _parts(self, ins, outs, sems):
        send_sems, recv_sems, local_sems = sems
        x, y, c = _place()
        me, sib = (x, y, c), (x, y, 1 - c)
        chips = [(1 - x, y), (x, 1 - y), (1 - x, 1 - y)]

        def copy(a, k, block, to, src=None):
            dst = outs[a].at[_slot(block)]
            return pltpu.make_async_remote_copy(
                src_ref=dst if src is None else src, dst_ref=dst, send_sem=send_sems.at[a, k],
                recv_sem=recv_sems.at[a, k], device_id=to, device_id_type=MESH)

        mine = [pltpu.make_async_copy(ins[a], outs[a].at[_slot(me)], local_sems.at[a]) for a in range(self.n)]
        first = []
        for a in range(self.n):
            first.append(copy(a, 0, me, sib, src=ins[a]))
            first += [copy(a, 1 + j, me, (*chip, c), src=ins[a]) for j, chip in enumerate(chips)]
        return copy, mine, first, me, sib, chips, c

    def start(self, ins, outs, sems):
        _, mine, first, *_ = self._parts(ins, outs, sems)
        for cp in mine + first:
            cp.start()

    def finish(self, ins, outs, sems):
        copy, mine, first, me, sib, chips, c = self._parts(ins, outs, sems)
        passed = []
        for j, chip in enumerate(chips):
            for a in range(self.n):
                copy(a, 1 + j, (*chip, c), me).wait_recv()
                fwd = copy(a, 4 + j, (*chip, c), sib)
                fwd.start()
                passed.append(fwd)
        for a in range(self.n):
            copy(a, 0, sib, me).wait_recv()
            for j, chip in enumerate(chips):
                copy(a, 4 + j, (*chip, 1 - c), me).wait_recv()
        for cp in first + passed:
            cp.wait_send()
        for cp in mine:
            cp.wait()


def _run_alone(ex, name):
    def body(*refs):
        ni, no = len(ex.ins), len(ex.out_shapes)
        parts = refs[:ni], refs[ni:ni + no], refs[ni + no:]
        ex.start(*parts)
        ex.finish(*parts)
    return pl.pallas_call(body, name=name, in_specs=[ANY] * len(ex.ins), out_specs=[ANY] * len(ex.out_shapes),
                          out_shape=ex.out_shapes, scratch_shapes=ex.scratch)(*ex.ins)


def _ride(body, n_in, n_out, ex, first, last):
    ni, no = len(ex.ins), len(ex.out_shapes)

    def wrapped(*refs):
        a, r_in = refs[:n_in], refs[n_in:n_in + ni]
        b, r_out = refs[n_in + ni:n_in + ni + n_out], refs[n_in + ni + n_out:n_in + ni + n_out + no]
        rest = refs[n_in + ni + n_out + no:]
        s, r_s = rest[:len(rest) - len(ex.scratch)], rest[len(rest) - len(ex.scratch):]

        @pl.when(first())
        def _():
            ex.start(r_in, r_out, r_s)
        body(*a, *b, *s)

        @pl.when(last())
        def _():
            ex.finish(r_in, r_out, r_s)
    return wrapped


def all_gather(xs, name):
    return _run_alone(_Gather(xs), name)


class _Exchange:
    def __init__(self, groups):
        self.ins = [a for grp in groups for a in grp]
        self.where = [(g, i) for g, grp in enumerate(groups) for i in range(len(grp))]
        n = self.n = len(self.ins)
        self.out_shapes = [jax.ShapeDtypeStruct((N_DEV, len(grp)) + grp[0].shape[1:], grp[0].dtype) for grp in groups]
        self.scratch = [pltpu.SemaphoreType.DMA((n, 7)), pltpu.SemaphoreType.DMA((n, 7)), pltpu.SemaphoreType.DMA((n,))]

    def _parts(self, ins, outs, sems):
        send_sems, recv_sems, local_sems = sems
        x, y, c = _place()
        me = (x, y, c)

        def peer(r):
            return (1 - x if r & 4 else x, 1 - y if r & 2 else y, 1 - c if r & 1 else c)

        def land(a, src):
            g, i = self.where[a]
            return outs[g].at[_slot(src), i]

        def copy(a, r, src_dev):
            p = peer(r)
            return pltpu.make_async_remote_copy(
                src_ref=ins[a].at[_slot(p)], dst_ref=land(a, src_dev), send_sem=send_sems.at[a, r - 1],
                recv_sem=recv_sems.at[a, r - 1], device_id=p, device_id_type=MESH)

        mine = [pltpu.make_async_copy(ins[a].at[_slot(me)], land(a, me), local_sems.at[a]) for a in range(self.n)]
        sends = [copy(a, r, me) for r in range(1, N_DEV) for a in range(self.n)]
        arrivals = [copy(a, r, peer(r)) for r in range(1, N_DEV) for a in range(self.n)]
        return mine, sends, arrivals

    def start(self, ins, outs, sems):
        mine, sends, _ = self._parts(ins, outs, sems)
        for cp in mine + sends:
            cp.start()

    def finish(self, ins, outs, sems):
        mine, sends, arrivals = self._parts(ins, outs, sems)
        for cp in arrivals:
            cp.wait_recv()
        for cp in sends:
            cp.wait_send()
        for cp in mine:
            cp.wait()


def all_to_all(groups, name):
    return _run_alone(_Exchange(groups), name)


def _mods(mod, l, D):
    return [mod[l:l + 1, k * D:(k + 1) * D] for k in range(6)]


def _ffn_backward(l, dxo, hb, u, uc, yf, xin, g2, sc2, gains, W):
    _, NJ, T, fs = u.shape
    D = xin.shape[1]
    dy, dgate, dpg, _ = postnorm_bwd(dxo, yf, g2, gains["post_ffn_g"][l:l + 1], f"ffn{l}_postnorm_bwd")
    z, duc = ffn_bwd_a(dy, uc, W["wdn"][l], f"ffn{l}_bwd_a")
    du, dh, taps = ffn_bwd_b(duc.reshape(2 * NJ, T, fs), u.reshape(2 * NJ, T, fs), W["fdw_w"][l], W["wup"][l],
                             f"ffn{l}_bwd_b")
    d_wdn = mm_tn(z, dy, f"ffn{l}_dwdn", False, True, fs, D)
    d_wup = mm_tn(hb, du, f"ffn{l}_dwup", True, False, D, fs)
    dx, da, dsh = prenorm_bwd(dh, xin, dxo, gains["pre_ffn_g"][l:l + 1], sc2, f"ffn{l}_prenorm_bwd")
    small = dict(dgate=dgate, dpost=dpg, da=da, dsh=dsh, dwb=taps[:, 3], dww=taps[:, 0:3])
    return dx, d_wup, d_wdn, small


def _local_step(xt, tgt, mod, gains, W, mine):
    T, D = xt.shape
    zero_d = jnp.zeros((1, D), F32)
    half = lambda g: g.reshape(N_DEV // 2, 2 * g.shape[1], D)
    sh1, sc1, g1, sh2, sc2, g2 = m0 = _mods(mod, 0, D)
    h0 = prenorm_mod(xt, gains["pre_mix_g"][0:1], sc1, sh1, "l0_prenorm")
    proj, (whout, wup0, wdn0) = mm_nn(h0, W["whin"], jnp.zeros((1, 4 * D), F32), "hgrn_proj",
                                      _Gather([mine["whout"], mine["wup"][0], mine["wdn"][0]]))
    o, og, st, (wcin, wcout, wup1, wdn1) = hgrn_fwd(
        proj, W["lb"], W["gg"], "hgrn_fwd", _Gather([mine["wcin"], mine["wcout"], mine["wup"][1], mine["wdn"][1]]))
    W = dict(W, whout=whout.reshape(D, D), wcin=wcin, wcout=wcout.reshape(D, D), wup=[wup0, wup1],
             wdn=[half(wdn0), half(wdn1)])
    y0, x1 = mm_post(og, W["whout"], zero_d, xt, g1, gains["post_mix_g"][0:1], "hgrn_out")
    h0f = prenorm_mod(x1, gains["pre_ffn_g"][0:1], sc2, sh2, "f0_prenorm")
    u0, uc0, yf0, x2 = ffn_fwd(h0f, W["wup"][0], W["fdw_w"][0], W["fdw_b"][0], W["wdn"][0], x1, g2,
                                gains["post_ffn_g"][0:1], "ffn0_fwd")
    t1, c1, e1, t2, c2, e2 = m1 = _mods(mod, 1, D)
    h1 = prenorm_mod(x2, gains["pre_mix_g"][1:2], c1, t1, "l1_prenorm")
    uc = mm_nn(h1, W["wcin"], W["cb_in"], "conv_in")
    cv, va = conv_mid_fwd(uc, W["cw"], W["cdw_b"], W["ln_g"], W["ln_b"], "conv_mid_fwd")
    y1, x3 = mm_post(va, W["wcout"], W["cb_out"], x2, e1, gains["post_mix_g"][1:2], "conv_out")
    h1f = prenorm_mod(x3, gains["pre_ffn_g"][1:2], c2, t2, "f1_prenorm")
    u1, uc1, yf1, x4 = ffn_fwd(h1f, W["wup"][1], W["fdw_w"][1], W["fdw_b"][1], W["wdn"][1], x3, e2,
                                gains["post_ffn_g"][1:2], "ffn1_fwd")
    dx4, loss_part = loss_grad(x4, tgt, "loss_grad")
    dx3, d_wup1, d_wdn1, sf1 = _ffn_backward(1, dx4, h1f, u1, uc1, yf1, x3, e2, c2, gains, W)
    dy1, dgate_c, dpost_c, dys_c = postnorm_bwd(dx3, y1, e1, gains["post_mix_g"][1:2], "conv_postnorm_bwd")
    dva = mm_nt(dy1, W["wcout"][None], "conv_dva", True)
    d_wcout = mm_tn(va, dy1, "conv_dwout", True, True, D, D)
    dcv, dlg, dlbias, taps = conv_bwd_a(dva, cv, uc, W["ln_g"], W["ln_b"], "conv_bwd_a")
    duc, dbin = conv_bwd_b(dcv, uc, W["cw"], "conv_bwd_b")
    dh1 = mm_nt(duc, W["wcin"], "conv_dh", True)
    d_wcin = mm_tn(h1, duc, "conv_dwin", True, True, D, W["wcin"].shape[2])
    dx2, da_c, dsh_c = prenorm_bwd(dh1, x2, dx3, gains["pre_mix_g"][1:2], c1, "l1_prenorm_bwd")
    dx1, d_wup0, d_wdn0, sf0 = _ffn_backward(0, dx2, h0f, u0, uc0, yf0, x1, g2, sc2, gains, W)
    dy0, dgate_h, dpost_h, _ = postnorm_bwd(dx1, y0, g1, gains["post_mix_g"][0:1], "hgrn_postnorm_bwd")
    dog = mm_nt(dy0, W["whout"][None], "hgrn_dog", True)
    d_whout = mm_tn(og, dy0, "hgrn_dwout", True, True, D, D)
    rows = lambda g: g.reshape(N_DEV, g.shape[1] // N_DEV, D)
    dproj, dlb, dgg, (r_wcin, r_wcout, r_wup, r_wdn) = hgrn_bwd(
        proj, o, dog, st, W["lb"], W["gg"], "hgrn_bwd",
        _Exchange([[d_wcin], [rows(d_wcout)], [d_wup0, d_wup1], [rows(d_wdn0), rows(d_wdn1)]]))
    dh0 = mm_nt(dproj, W["whin"], "hgrn_dh", True)
    d_whin = mm_tn(h0, dproj, "hgrn_dwin", True, True, D, W["whin"].shape[2])
    dx0, da_h, dsh_h = prenorm_bwd(dh0, xt, dx1, gains["pre_mix_g"][0:1], sc1, "l0_prenorm_bwd")
    r_whin, r_whout = all_to_all([[d_whin], [rows(d_whout)]], "exchange_hgrn_grads")

    big = dict(hgrn_w_in=r_whin, hgrn_w_out=r_whout, conv_w_in=r_wcin, conv_w_out=r_wcout, ffn_w_up=r_wup,
               ffn_w_down=r_wdn)
    pm, pf = gains["pre_mix_g"], gains["pre_ffn_g"]
    dmod = jnp.concatenate([
        dsh_h, da_h * pm[0:1], dgate_h, sf0["dsh"], sf0["da"] * pf[0:1], sf0["dgate"],
        dsh_c, da_c * pm[1:2], dgate_c, sf1["dsh"], sf1["da"] * pf[1:2], sf1["dgate"]], axis=1)
    NCH = D // LANES
    tap = taps.reshape(NCH, 32, 8, LANES).sum(axis=2)
    small = dict(
        dmod=dmod,
        pre_mix_g=jnp.concatenate([da_h * (1.0 + sc1), da_c * (1.0 + c1)], axis=0),
        post_mix_g=jnp.concatenate([dpost_h, dpost_c], axis=0),
        pre_ffn_g=jnp.concatenate([sf0["da"] * (1.0 + sc2), sf1["da"] * (1.0 + c2)], axis=0),
        post_ffn_g=jnp.concatenate([sf0["dpost"], sf1["dpost"]], axis=0),
        lb=dlb, gg=dgg,
        ffn_dw_b=jnp.stack([sf0["dwb"], sf1["dwb"]]),
        conv_b_in=dbin,
        conv_dw_w=jnp.transpose(tap[:, :31], (1, 0, 2)).reshape(31, D),
        conv_dw_b=tap[:, 31].reshape(1, D),
        conv_ln_g=dlg, conv_ln_b=dlbias, conv_b_out=dys_c,
        ffn_dw_w=jnp.stack([sf0["dww"], sf1["dww"]]))
    return loss_part, dx0, big, small


SMALL_ORDER = ["dmod", "pre_mix_g", "post_mix_g", "pre_ffn_g", "post_ffn_g", "lb", "gg", "ffn_dw_b", "conv_b_in",
               "conv_dw_w", "conv_dw_b", "conv_ln_g", "conv_ln_b", "conv_b_out", "ffn_dw_w"]


def _pack(parts):
    flat = jnp.concatenate([p.reshape(-1) for p in parts])
    pad = (-flat.shape[0]) % LANES
    if pad:
        flat = jnp.concatenate([flat, jnp.zeros((pad,), F32)])
    return flat.reshape(-1, LANES)


def _unpack(flat, shapes):
    out, off = [], 0
    for s in shapes:
        n = 1
        for d in s:
            n *= d
        out.append(flat[off:off + n].reshape(s))
        off += n
    return out


def _apply_adamw(name, w, g_slots, m, v):
    shp = w.shape
    C = shp[-1]
    R = w.size // C
    g, d, mn, vn = adamw(w.reshape(R, C), g_slots.reshape(g_slots.shape[0], R, C), m.reshape(R, C), v.reshape(R, C),
                         "adamw_" + name)
    return g.reshape(shp), d.reshape(shp), mn.reshape(shp), vn.reshape(shp)


WEIGHTS = ['ada_w', 'ada_b', 'pre_mix_g', 'post_mix_g', 'pre_ffn_g', 'post_ffn_g', 'hgrn_w_in', 'hgrn_lb_logits',
           'hgrn_gnorm_g', 'hgrn_w_out', 'conv_w_in', 'conv_b_in', 'conv_dw_w', 'conv_dw_b', 'conv_ln_g', 'conv_ln_b',
           'conv_w_out', 'conv_b_out', 'ffn_w_up', 'ffn_dw_w', 'ffn_dw_b', 'ffn_w_down']


def kernel(x, c, ada_w, ada_b, pre_mix_g, post_mix_g, pre_ffn_g, post_ffn_g, hgrn_w_in, hgrn_lb_logits, hgrn_gnorm_g, hgrn_w_out, conv_w_in, conv_b_in, conv_dw_w, conv_dw_b, conv_ln_g, conv_ln_b, conv_w_out, conv_b_out, ffn_w_up, ffn_dw_w, ffn_dw_b, ffn_w_down, loss_target, m_ada_w, m_ada_b, m_pre_mix_g, m_post_mix_g, m_pre_ffn_g, m_post_ffn_g, m_hgrn_w_in, m_hgrn_lb_logits, m_hgrn_gnorm_g, m_hgrn_w_out, m_conv_w_in, m_conv_b_in, m_conv_dw_w, m_conv_dw_b, m_conv_ln_g, m_conv_ln_b, m_conv_w_out, m_conv_b_out, m_ffn_w_up, m_ffn_dw_w, m_ffn_dw_b, m_ffn_w_down, v_ada_w, v_ada_b, v_pre_mix_g, v_post_mix_g, v_pre_ffn_g, v_post_ffn_g, v_hgrn_w_in, v_hgrn_lb_logits, v_hgrn_gnorm_g, v_hgrn_w_out, v_conv_w_in, v_conv_b_in, v_conv_dw_w, v_conv_dw_b, v_conv_ln_g, v_conv_ln_b, v_conv_w_out, v_conv_b_out, v_ffn_w_up, v_ffn_dw_w, v_ffn_dw_b, v_ffn_w_down):
    P = dict(locals())
    _, T, D = x.shape
    me = _slot(_place())
    L = ada_w.shape[0]
    na = ada_w.shape[2]
    fs = ffn_w_up.shape[2]
    nci = conv_w_in.shape[2]
    nd = conv_dw_b.shape[1]
    NCH = D // LANES

    small_in = [c, conv_b_in, conv_dw_w, conv_dw_b, conv_ln_g, conv_ln_b, conv_b_out, ffn_dw_w]
    whin, packed = all_gather([_bf(hgrn_w_in[0]), _pack(small_in)], "gather_first")
    mine = dict(whout=_bf(hgrn_w_out[0]), wcin=_bf(conv_w_in[0]), wcout=_bf(conv_w_out[0]),
                wup=[_bf(ffn_w_up[l]) for l in range(L)], wdn=[_bf(ffn_w_down[l]) for l in range(L)])
    sm = packed.reshape(N_DEV, -1)
    offs = [0]
    for a in small_in:
        offs.append(offs[-1] + a.size)
    piece = lambda k, shp: sm[:, offs[k]:offs[k + 1]].reshape((N_DEV,) + shp)
    c_all = piece(0, (D,))
    dw_nat = jnp.transpose(piece(2, (31, nd)), (1, 0, 2)).reshape(31, D)
    cw = jnp.pad(jnp.transpose(dw_nat.reshape(31, NCH, LANES), (1, 0, 2)), ((0, 0), (0, 1), (0, 0)))
    fdw = piece(7, (L, 3, fs))
    lb_soft = lb_softmax(hgrn_lb_logits, "lb_softmax")
    W = dict(
        whin=whin, lb=lb_soft[0:1], gg=hgrn_gnorm_g, cb_in=piece(1, (nci,)).reshape(1, N_DEV * nci), cw=cw,
        cdw_b=piece(3, (nd,)).reshape(1, D), ln_g=piece(4, (nd,)).reshape(1, D), ln_b=piece(5, (nd,)).reshape(1, D),
        cb_out=piece(6, (nd,)).reshape(1, D), fdw_w=[fdw[:, l] for l in range(L)],
        fdw_b=[ffn_dw_b[l].reshape(N_DEV, 1, fs) for l in range(L)])

    bias_mine = lax.dynamic_slice(ada_b, (0, me * na), (L, na)).reshape(L, 1, na)
    mod_cols = ada_fwd(c_all, ada_w, bias_mine, "ada_fwd")
    (mod_all,) = all_gather([mod_cols], "gather_mod")
    mod = jnp.transpose(lax.dynamic_index_in_dim(mod_all, me, axis=2, keepdims=False), (1, 0, 2)).reshape(L, N_DEV * na)

    gains = dict(pre_mix_g=pre_mix_g, post_mix_g=post_mix_g, pre_ffn_g=pre_ffn_g, post_ffn_g=post_ffn_g)
    loss_part, dx0, big, small = _local_step(x[0], loss_target[0], mod, gains, W, mine)
    loss = lax.psum((0.5 / D) * jnp.sum(loss_part), ("x", "y", "c"))

    parts = [small[k] for k in SMALL_ORDER]
    (sm_all,) = all_gather([_pack(parts)], "gather_small_grads")
    tot = _unpack(sum_slots(sm_all, "sum_small_grads").reshape(-1), [p.shape for p in parts])
    tot = dict(zip(SMALL_ORDER, tot))
    dmod_all = sm_all.reshape(N_DEV, -1)[:, :L * 6 * D].reshape(N_DEV, L, 6 * D)
    dm_mine = jnp.transpose(lax.dynamic_slice(dmod_all, (0, 0, me * na), (N_DEV, L, na)), (1, 0, 2))
    s0 = lb_soft[0:1]
    onehot = (lax.broadcasted_iota(jnp.int32, lb_soft.shape, 0) == 0).astype(F32)
    col = lambda a, n: lax.dynamic_slice_in_dim(a, me * n, n, axis=a.ndim - 1)
    G = {
        'ada_w': ada_bwd(c_all, dm_mine, "ada_bwd")[None],
        'ada_b': tot["dmod"].reshape(1, L, 6 * D),
        'pre_mix_g': tot["pre_mix_g"][None], 'post_mix_g': tot["post_mix_g"][None],
        'pre_ffn_g': tot["pre_ffn_g"][None], 'post_ffn_g': tot["post_ffn_g"][None],
        'hgrn_lb_logits': (tot["lb"] * s0 * (onehot - lb_soft))[None],
        'hgrn_gnorm_g': tot["gg"][None],
        'ffn_dw_b': tot["ffn_dw_b"].reshape(1, L, N_DEV * fs),
        'conv_b_in': col(tot["conv_b_in"], nci)[None],
        'conv_dw_w': col(tot["conv_dw_w"], nd)[None, None],
        'conv_dw_b': col(tot["conv_dw_b"], nd)[None], 'conv_ln_g': col(tot["conv_ln_g"], nd)[None],
        'conv_ln_b': col(tot["conv_ln_b"], nd)[None], 'conv_b_out': col(tot["conv_b_out"], nd)[None],
        'ffn_dw_w': lax.dynamic_index_in_dim(tot["ffn_dw_w"], me, axis=1, keepdims=False)[None],
    }

    G.update(big)
    res = {n: _apply_adamw(n, P[n], G[n], P["m_" + n], P["v_" + n]) for n in WEIGHTS}
    return (loss, dx0[None], *[res[n][0] for n in WEIGHTS], *[res[n][1] for n in WEIGHTS],
            *[res[n][2] for n in WEIGHTS], *[res[n][3] for n in WEIGHTS])
```

```python
import functools

import jax
import jax.numpy as jnp
from jax import lax
from jax.experimental import pallas as pl
from jax.experimental.pallas import tpu as pltpu

F32 = jnp.float32
BF16 = jnp.bfloat16
EPS = 1e-6
LANES = 128
N_DEV = 8
VMEM_LIMIT = 48 * 1024 * 1024
FFN_BWD_VMEM = 58 * 1024 * 1024
MM_ROWS = 1024
HG_CHUNK = 128
HG_SUB = 16
EXP_CLAMP = 80.0
CONV_HALO = 32
FFN_HALO = 8
CONV_ROWS = 32
ADAM_LR, ADAM_B1, ADAM_B2, ADAM_EPS, ADAM_WD, ADAM_STEP = 0.001, 0.9, 0.999, 1e-08, 0.01, 10
MESH = pl.DeviceIdType.MESH
ANY = pl.BlockSpec(memory_space=pl.ANY)


def _cp(*sem):
    return pltpu.CompilerParams(dimension_semantics=sem, vmem_limit_bytes=VMEM_LIMIT)


def _rt(n, t):
    t = min(n, t)
    assert n % t == 0, (n, t)
    return t


def _sig(x):
    return jax.nn.sigmoid(x)


def _nt(a, b):
    return lax.dot_general(a, b, (((1,), (1,)), ((), ())), preferred_element_type=F32)


def _tn(a, b):
    return lax.dot_general(a, b, (((0,), (0,)), ((), ())), preferred_element_type=F32)


def _nn(a, b):
    return jnp.dot(a, b, preferred_element_type=F32)


def _bf(x):
    return x.astype(BF16)


def _split2(x):
    h = _bf(x)
    return h, _bf(x - h.astype(F32))


def _x3(dot, a, b):
    ah, al = _split2(a)
    bh, bl = _split2(b)
    return dot(ah, bh) + (dot(ah, bl) + dot(al, bh))


def _nn_hi(a, b):
    return _x3(_nn, a, b)


def _nt_hi(a, b):
    return _x3(_nt, a, b)


def _tn_hi(a, b):
    return _x3(_tn, a, b)


def _nn_st(a, b):
    return _nn(_bf(a), _bf(b))


def _tn_st(a, b):
    return _tn(_bf(a), _bf(b))


def _row(d):
    return pl.BlockSpec((1, d), lambda *_: (0, 0))


def prenorm_mod(x, g, sc, sh, name):
    T, D = x.shape
    tm = _rt(T, 512)

    def body(x_ref, g_ref, sc_ref, sh_ref, h_ref):
        xv = x_ref[...]
        r = lax.rsqrt(jnp.mean(xv * xv, axis=-1, keepdims=True) + EPS)
        h_ref[...] = _bf((xv * r) * g_ref[...] * (1.0 + sc_ref[...]) + sh_ref[...])

    tile = pl.BlockSpec((tm, D), lambda i: (i, 0))
    return pl.pallas_call(
        body, name=name, grid=(T // tm,), in_specs=[tile, _row(D), _row(D), _row(D)], out_specs=tile,
        out_shape=jax.ShapeDtypeStruct((T, D), BF16), compiler_params=_cp("parallel"))(x, g, sc, sh)


def prenorm_bwd(dh, x, dxres, g, sc, name):
    T, D = x.shape
    tm = _rt(T, 512)

    def body(dh_ref, x_ref, dr_ref, g_ref, sc_ref, dx_ref, da_ref, dsh_ref):
        @pl.when(pl.program_id(0) == 0)
        def _():
            da_ref[...] = jnp.zeros_like(da_ref)
            dsh_ref[...] = jnp.zeros_like(dsh_ref)
        xv = x_ref[...]
        dh = dh_ref[...]
        r = lax.rsqrt(jnp.mean(xv * xv, axis=-1, keepdims=True) + EPS)
        n = xv * r
        da_ref[...] += jnp.sum(dh * n, axis=0, keepdims=True)
        dsh_ref[...] += jnp.sum(dh, axis=0, keepdims=True)
        dn = dh * (g_ref[...] * (1.0 + sc_ref[...]))
        dx_ref[...] = dr_ref[...] + r * (dn - n * jnp.mean(dn * n, axis=-1, keepdims=True))

    tile = pl.BlockSpec((tm, D), lambda i: (i, 0))
    vec = jax.ShapeDtypeStruct((1, D), F32)
    return pl.pallas_call(
        body, name=name, grid=(T // tm,), in_specs=[tile, tile, tile, _row(D), _row(D)],
        out_specs=[tile, _row(D), _row(D)], out_shape=[jax.ShapeDtypeStruct((T, D), F32), vec, vec],
        compiler_params=_cp("arbitrary"))(dh, x, dxres, g, sc)


def postnorm_bwd(dxo, y, gate, pg, name):
    T, D = y.shape
    tm = _rt(T, 512)

    def body(d_ref, y_ref, gt_ref, pg_ref, dy_ref, dgate_ref, dpg_ref, dys_ref):
        @pl.when(pl.program_id(0) == 0)
        def _():
            dgate_ref[...] = jnp.zeros_like(dgate_ref)
            dpg_ref[...] = jnp.zeros_like(dpg_ref)
            dys_ref[...] = jnp.zeros_like(dys_ref)
        yv = y_ref[...]
        d = d_ref[...]
        r = lax.rsqrt(jnp.mean(yv * yv, axis=-1, keepdims=True) + EPS)
        n = yv * r
        dgate_ref[...] += jnp.sum(d * n, axis=0, keepdims=True) * pg_ref[...]
        dpg_ref[...] += jnp.sum(d * n, axis=0, keepdims=True) * gt_ref[...]
        dn = d * (gt_ref[...] * pg_ref[...])
        dy = r * (dn - n * jnp.mean(dn * n, axis=-1, keepdims=True))
        dy_ref[...] = dy
        dys_ref[...] += jnp.sum(dy, axis=0, keepdims=True)

    tile = pl.BlockSpec((tm, D), lambda i: (i, 0))
    vec = jax.ShapeDtypeStruct((1, D), F32)
    return pl.pallas_call(
        body, name=name, grid=(T // tm,), in_specs=[tile, tile, _row(D), _row(D)],
        out_specs=[tile, _row(D), _row(D), _row(D)], out_shape=[jax.ShapeDtypeStruct((T, D), F32), vec, vec, vec],
        compiler_params=_cp("arbitrary"))(dxo, y, gate, pg)


def loss_grad(xo, tgt, name):
    T, D = xo.shape
    tm = _rt(T, 512)

    def body(x_ref, t_ref, dx_ref, part_ref):
        @pl.when(pl.program_id(0) == 0)
        def _():
            part_ref[...] = jnp.zeros_like(part_ref)
        e = x_ref[...] - t_ref[...]
        dx_ref[...] = e * (1.0 / D)
        e2 = (e * e).reshape(tm // 8, 8, D).sum(axis=0)
        acc = e2[:, 0:LANES]
        for j in range(1, D // LANES):
            acc = acc + e2[:, j * LANES:(j + 1) * LANES]
        part_ref[...] += acc

    tile = pl.BlockSpec((tm, D), lambda i: (i, 0))
    return pl.pallas_call(
        body, name=name, grid=(T // tm,), in_specs=[tile, tile],
        out_specs=[tile, pl.BlockSpec((8, LANES), lambda i: (0, 0))],
        out_shape=[jax.ShapeDtypeStruct((T, D), F32), jax.ShapeDtypeStruct((8, LANES), F32)],
        compiler_params=_cp("arbitrary"))(xo, tgt)


def _call(body, name, grid, in_specs, out_specs, out_shape, scratch, sems, operands, ex=None):
    if ex is not None:
        def first():
            return functools.reduce(lambda p, q: p & q, [pl.program_id(k) == 0 for k in range(len(grid))])

        def last():
            return functools.reduce(lambda p, q: p & q, [pl.program_id(k) == grid[k] - 1 for k in range(len(grid))])
        body = _ride(body, len(in_specs), len(out_specs), ex, first, last)
        in_specs = in_specs + [ANY] * len(ex.ins)
        out_specs = out_specs + [ANY] * len(ex.out_shapes)
        out_shape = out_shape + ex.out_shapes
        scratch = scratch + ex.scratch
        operands = list(operands) + ex.ins
        sems = ("arbitrary",) * len(grid)
    res = pl.pallas_call(body, name=name, grid=grid, in_specs=in_specs, out_specs=out_specs, out_shape=out_shape,
                         scratch_shapes=scratch, compiler_params=_cp(*sems))(*operands)
    n_own = len(res) - (len(ex.out_shapes) if ex is not None else 0)
    return res[:n_own], res[n_own:]


def mm_nn(a, bg, bias, name, ex=None):
    M, K = a.shape
    G, _, n = bg.shape
    tm = _rt(M, MM_ROWS)

    def body(a_ref, b_ref, bias_ref, o_ref):
        o_ref[...] = _nn(a_ref[...], b_ref[...]) + bias_ref[...]

    (out,), rider = _call(
        body, name, (M // tm, G),
        [pl.BlockSpec((tm, K), lambda i, j: (i, 0)), pl.BlockSpec((None, K, n), lambda i, j: (j, 0, 0)),
         pl.BlockSpec((1, n), lambda i, j: (0, j))],
        [pl.BlockSpec((tm, n), lambda i, j: (i, j))], [jax.ShapeDtypeStruct((M, G * n), F32)], [],
        ("parallel", "arbitrary"), (a, bg, bias), ex)
    return (out, rider) if ex is not None else out


def mm_post(a, b, bias, x, gate, pg, name):
    T, K = a.shape
    D = b.shape[1]
    tm = _rt(T, 512)

    def body(a_ref, b_ref, bias_ref, x_ref, gt_ref, pg_ref, y_ref, xo_ref):
        y = _nn(a_ref[...], b_ref[...]) + bias_ref[...]
        y_ref[...] = y
        r = lax.rsqrt(jnp.mean(y * y, axis=-1, keepdims=True) + EPS)
        xo_ref[...] = x_ref[...] + gt_ref[...] * ((y * r) * pg_ref[...])

    tile = pl.BlockSpec((tm, D), lambda i: (i, 0))
    out = jax.ShapeDtypeStruct((T, D), F32)
    return pl.pallas_call(
        body, name=name, grid=(T // tm,),
        in_specs=[pl.BlockSpec((tm, K), lambda i: (i, 0)), pl.BlockSpec((K, D), lambda i: (0, 0)), _row(D), tile,
                  _row(D), _row(D)],
        out_specs=[tile, tile], out_shape=[out, out], compiler_params=_cp("parallel"))(a, b, bias, x, gate, pg)


def mm_nt(dy, wg, name, natural):
    G, K, n = wg.shape
    T = dy.shape[0] if natural else dy.shape[1]
    tm = _rt(T, MM_ROWS)

    def body(dy_ref, w_ref, o_ref, acc):
        j = pl.program_id(1)

        @pl.when(j == 0)
        def _():
            acc[...] = jnp.zeros_like(acc)
        acc[...] += _nt(_bf(dy_ref[...]), w_ref[...])

        @pl.when(j == G - 1)
        def _():
            o_ref[...] = acc[...]

    dspec = (pl.BlockSpec((tm, n), lambda i, j: (i, j)) if natural
             else pl.BlockSpec((None, tm, n), lambda i, j: (j, i, 0)))
    return pl.pallas_call(
        body, name=name, grid=(T // tm, G),
        in_specs=[dspec, pl.BlockSpec((None, K, n), lambda i, j: (j, 0, 0))],
        out_specs=pl.BlockSpec((tm, K), lambda i, j: (i, 0)), out_shape=jax.ShapeDtypeStruct((T, K), F32),
        scratch_shapes=[pltpu.VMEM((tm, K), F32)], compiler_params=_cp("parallel", "arbitrary"))(dy, wg)


def mm_tn(xm, gm, name, x_natural, g_natural, kx, n):
    if x_natural:
        T, Gx = xm.shape[0], xm.shape[1] // kx
    else:
        Gx, T = xm.shape[0], xm.shape[1]
    tt = _rt(T, MM_ROWS)
    if x_natural:
        xspec = pl.BlockSpec((tt, kx), lambda g, k, t: (t, k))
    else:
        xspec = pl.BlockSpec((None, tt, kx), lambda g, k, t: (k, t, 0))
    if g_natural:
        Gg = gm.shape[1] // n
        gspec = pl.BlockSpec((tt, n), lambda g, k, t: (t, g))
    else:
        Gg = gm.shape[0]
        gspec = pl.BlockSpec((None, tt, n), lambda g, k, t: (g, t, 0))

    NT = T // tt

    def body(x_ref, g_ref, o_ref, acc):
        t = pl.program_id(2)

        @pl.when(t == 0)
        def _():
            acc[...] = jnp.zeros_like(acc)
        acc[...] += _tn(_bf(x_ref[...]), _bf(g_ref[...]))

        @pl.when(t == NT - 1)
        def _():
            o_ref[...] = acc[...].astype(o_ref.dtype)

    return pl.pallas_call(
        body, name=name, grid=(Gg, Gx, NT), in_specs=[xspec, gspec],
        out_specs=pl.BlockSpec((None, kx, n), lambda g, k, t: (g, k, 0)),
        out_shape=jax.ShapeDtypeStruct((Gg, Gx * kx, n), BF16), scratch_shapes=[pltpu.VMEM((kx, n), F32)],
        compiler_params=_cp("parallel", "parallel", "arbitrary"))(xm, gm)


def _split3(x):
    h = _bf(x)
    r = x - h.astype(F32)
    m = _bf(r)
    lo = _bf(r - m.astype(F32))
    return h, m, lo


def _tri_mm(tri, x):
    h, m, lo = _split3(x)
    return _nn(tri, lo) + _nn(tri, m) + _nn(tri, h)


def _hgrn_gates(qr, fz, lb):
    sq = _sig(qr)
    q = qr * sq
    sig = _sig(fz)
    f = lb + (1.0 - lb) * sig
    k = 1.0 - f
    return sq, q, sig, f, k, jnp.log(f)


def hgrn_fwd(proj, lb, gg, name, ex=None):
    T, D4 = proj.shape
    D = D4 // 4
    H = D // LANES
    C = _rt(T, HG_CHUNK)
    SB = min(HG_SUB, C)
    NC = T // C

    def body(q_ref, f_ref, v_ref, g_ref, lb_ref, gg_ref, o_ref, og_ref, st_ref, ST, bex):
        @pl.when(pl.program_id(0) == 0)
        def _():
            ST[...] = jnp.zeros_like(ST)
        r_i = lax.broadcasted_iota(jnp.int32, (C, C), 0)
        c_i = lax.broadcasted_iota(jnp.int32, (C, C), 1)
        low = _bf((r_i >= c_i).astype(F32))
        rs = lax.broadcasted_iota(jnp.int32, (SB, C), 0)
        cs = lax.broadcasted_iota(jnp.int32, (SB, C), 1)
        for hd in range(H):
            sl = slice(hd * LANES, (hd + 1) * LANES)
            _, q, _, _, k, lf = _hgrn_gates(q_ref[:, sl], f_ref[:, sl], lb_ref[:, sl])
            v = v_ref[:, sl]
            b = _tri_mm(low, lf)
            bex[hd] = b - lf
            bend = jnp.sum(lf, axis=0, keepdims=True)
            blocks = []
            for I in range(C // SB):
                p = bex[hd, pl.ds(SB * I, 1), :]
                rows = slice(SB * I, SB * (I + 1))
                qI = _bf(q[rows] * jnp.exp(b[rows] - p))
                kI = _bf(k * jnp.exp(jnp.minimum(p - b, EXP_CLAMP)))
                blocks.append(jnp.where(rs + SB * I >= cs, _nt(qI, kI), 0.0))
            A = jnp.concatenate(blocks, axis=0)
            st0 = ST[hd]
            st_ref[0, hd] = st0
            o = _nn(_bf(A), _bf(v)) + _nt(_bf(q * jnp.exp(b)), _bf(st0))
            ST[hd] = st0 * jnp.exp(bend) + _tn_st(v, k * jnp.exp(bend - b))
            o_ref[:, sl] = o
            r = lax.rsqrt(jnp.mean(o * o, axis=-1, keepdims=True) + EPS)
            gr = g_ref[:, sl]
            og_ref[:, sl] = _bf((o * r) * gg_ref[...] * (gr * _sig(gr)))

    col = lambda j: pl.BlockSpec((C, D), lambda c, j=j: (c, j))
    tile = pl.BlockSpec((C, D), lambda c: (c, 0))
    own, rider = _call(
        body, name, (NC,), [col(0), col(1), col(2), col(3), _row(D), _row(LANES)],
        [tile, tile, pl.BlockSpec((1, H, LANES, LANES), lambda c: (c, 0, 0, 0))],
        [jax.ShapeDtypeStruct((T, D), F32), jax.ShapeDtypeStruct((T, D), BF16),
         jax.ShapeDtypeStruct((NC, H, LANES, LANES), F32)],
        [pltpu.VMEM((H, LANES, LANES), F32), pltpu.VMEM((H, C, LANES), F32)], ("arbitrary",),
        (proj, proj, proj, proj, lb, gg), ex)
    return (*own, rider) if ex is not None else own


def hgrn_bwd(proj, o, dog, st, lb, gg, name, ex=None):
    T, D4 = proj.shape
    D = D4 // 4
    H = D // LANES
    C = _rt(T, HG_CHUNK)
    SB = min(HG_SUB, C)
    NC = T // C

    def body(q_ref, f_ref, v_ref, g_ref, o_ref, dog_ref, st_ref, lb_ref, gg_ref, dp_ref, dlb_ref, dgg_ref,
             DST, bex):
        @pl.when(pl.program_id(0) == 0)
        def _():
            DST[...] = jnp.zeros_like(DST)
            dlb_ref[...] = jnp.zeros_like(dlb_ref)
            dgg_ref[...] = jnp.zeros_like(dgg_ref)
        r_i = lax.broadcasted_iota(jnp.int32, (C, C), 0)
        c_i = lax.broadcasted_iota(jnp.int32, (C, C), 1)
        causal = r_i >= c_i
        low = _bf(causal.astype(F32))
        upp = _bf((r_i <= c_i).astype(F32))
        rs = lax.broadcasted_iota(jnp.int32, (SB, C), 0)
        cs = lax.broadcasted_iota(jnp.int32, (SB, C), 1)
        ggv = gg_ref[...]
        for hd in range(H):
            sl = slice(hd * LANES, (hd + 1) * LANES)
            qr = q_ref[:, sl]
            lbv = lb_ref[:, sl]
            sq, q, sig, f, k, lf = _hgrn_gates(qr, f_ref[:, sl], lbv)
            v = v_ref[:, sl]
            oh = o_ref[:, sl]
            r = lax.rsqrt(jnp.mean(oh * oh, axis=-1, keepdims=True) + EPS)
            n = oh * r
            gr = g_ref[:, sl]
            sg = _sig(gr)
            dgo = dog_ref[:, sl]
            dgr = dgo * (n * ggv) * (sg * (1.0 + gr * (1.0 - sg)))
            don = dgo * (gr * sg)
            dgg_ref[...] += jnp.sum(don * n, axis=0, keepdims=True)
            dn = don * ggv
            do = r * (dn - n * jnp.mean(dn * n, axis=-1, keepdims=True))
            b = _tri_mm(low, lf)
            bex[hd] = b - lf
            bend = jnp.sum(lf, axis=0, keepdims=True)
            dob = _bf(do)
            dA = jnp.where(causal, _nt_hi(do, v), 0.0)
            blocks, dqs = [], []
            dk = jnp.zeros((C, LANES), F32)
            for I in range(C // SB):
                p = bex[hd, pl.ds(SB * I, 1), :]
                rows = slice(SB * I, SB * (I + 1))
                eq = jnp.exp(b[rows] - p)
                qI = q[rows] * eq
                ek = jnp.exp(jnp.minimum(p - b, EXP_CLAMP))
                kI = k * ek
                blocks.append(jnp.where(rs + SB * I >= cs, _nt(_bf(qI), _bf(kI)), 0.0))
                dqs.append(_nn_hi(dA[rows], kI) * eq)
                dk = dk + _tn_hi(dA[rows], qI) * ek
            A = jnp.concatenate(blocks, axis=0)
            dst = DST[hd]
            st0 = st_ref[0, hd]
            eb = jnp.exp(b)
            ee = jnp.exp(bend - b)
            ebend = jnp.exp(bend)
            dv = _tn(_bf(A), dob) + _nt(_bf(k * ee), _bf(dst))
            dk_state = ee * _nn_st(v, dst)
            dq = jnp.concatenate(dqs, axis=0) + eb * _nn_st(do, st0)
            dk = dk + dk_state
            DST[hd] = dst * ebend + _tn_st(do, q * eb)
            later = jnp.sum(k * dk_state, axis=0, keepdims=True) + ebend * jnp.sum(st0 * dst, axis=0, keepdims=True)
            dlf = _tri_mm(upp, q * dq - k * dk) + later
            df = dlf / f - dk
            dlb_ref[:, sl] += jnp.sum(df * (1.0 - sig), axis=0, keepdims=True)
            dp_ref[:, sl] = _bf(dq * (sq * (1.0 + qr * (1.0 - sq))))
            dp_ref[:, D + hd * LANES:D + (hd + 1) * LANES] = _bf(df * (1.0 - lbv) * (sig * (1.0 - sig)))
            dp_ref[:, 2 * D + hd * LANES:2 * D + (hd + 1) * LANES] = _bf(dv)
            dp_ref[:, 3 * D + hd * LANES:3 * D + (hd + 1) * LANES] = _bf(dgr)

    col = lambda j: pl.BlockSpec((C, D), lambda c, j=j: (NC - 1 - c, j))
    tile = pl.BlockSpec((C, D), lambda c: (NC - 1 - c, 0))
    own, rider = _call(
        body, name, (NC,),
        [col(0), col(1), col(2), col(3), tile, tile,
         pl.BlockSpec((1, H, LANES, LANES), lambda c: (NC - 1 - c, 0, 0, 0)), _row(D), _row(LANES)],
        [pl.BlockSpec((C, D4), lambda c: (NC - 1 - c, 0)), _row(D), _row(LANES)],
        [jax.ShapeDtypeStruct((T, D4), BF16), jax.ShapeDtypeStruct((1, D), F32), jax.ShapeDtypeStruct((1, LANES), F32)],
        [pltpu.VMEM((H, LANES, LANES), F32), pltpu.VMEM((H, C, LANES), F32)],
        ("arbitrary",), (proj, proj, proj, proj, o, dog, st, lb, gg), ex)
    return (*own, rider) if ex is not None else own


def _prev_blk(tm, hb):
    return lambda i: (jnp.maximum(i * (tm // hb) - 1, 0), 0)


def _next_blk(tm, hb, T):
    return lambda i: (jnp.minimum((i + 1) * (tm // hb), T // hb - 1), 0)


def _glu_to_ext(uc_ref, ucp_ref, ext, D, first):
    def glu(u):
        return u[:, :D] * _sig(u[:, D:])
    gh = jnp.where(first, 0.0, glu(ucp_ref[...]))
    gm = glu(uc_ref[...])
    for c in range(D // LANES):
        ext[c, 0:CONV_HALO, :] = gh[:, c * LANES:(c + 1) * LANES]
        ext[c, CONV_HALO:, :] = gm[:, c * LANES:(c + 1) * LANES]


def conv_mid_fwd(uc, cw, cb, lg, lbias, name):
    T, D2 = uc.shape
    D = D2 // 2
    NCH = D // LANES
    W = 31
    tm = _rt(T, 256)
    RS = min(CONV_ROWS, tm)

    def body(uc_ref, ucp_ref, cw_ref, cb_ref, lg_ref, lb_ref, cv_ref, va_ref, ext, outs):
        _glu_to_ext(uc_ref, ucp_ref, ext, D, pl.program_id(0) == 0)

        def chunk(c, carry):
            for rb in range(tm // RS):
                acc = jnp.zeros((RS, LANES), F32)
                for kk in range(W):
                    acc = acc + cw_ref[c, pl.ds(kk, 1), :] * ext[c, pl.ds(rb * RS + CONV_HALO - (W - 1) + kk, RS), :]
                outs[c, pl.ds(rb * RS, RS), :] = acc
            return carry
        lax.fori_loop(0, NCH, chunk, 0)
        cv = jnp.concatenate([outs[c] for c in range(NCH)], axis=1) + cb_ref[...]
        cv_ref[...] = cv
        mu = jnp.mean(cv, axis=-1, keepdims=True)
        xc = cv - mu
        rstd = lax.rsqrt(jnp.mean(xc * xc, axis=-1, keepdims=True) + EPS)
        l = xc * rstd * lg_ref[...] + lb_ref[...]
        va_ref[...] = _bf(l * _sig(l))

    tile = pl.BlockSpec((tm, D), lambda i: (i, 0))
    return pl.pallas_call(
        body, name=name, grid=(T // tm,),
        in_specs=[pl.BlockSpec((tm, D2), lambda i: (i, 0)), pl.BlockSpec((CONV_HALO, D2), _prev_blk(tm, CONV_HALO)),
                  pl.BlockSpec((NCH, 32, LANES), lambda i: (0, 0, 0)), _row(D), _row(D), _row(D)],
        out_specs=[tile, tile],
        out_shape=[jax.ShapeDtypeStruct((T, D), F32), jax.ShapeDtypeStruct((T, D), BF16)],
        scratch_shapes=[pltpu.VMEM((NCH, tm + CONV_HALO, LANES), F32), pltpu.VMEM((NCH, tm, LANES), F32)],
        compiler_params=_cp("parallel"))(uc, uc, cw, cb, lg, lbias)


def conv_bwd_a(dva, cv, uc, lg, lbias, name):
    T, D2 = uc.shape
    D = D2 // 2
    NCH = D // LANES
    W = 31
    tm = _rt(T, 256)
    RS = min(CONV_ROWS, tm)

    def body(dva_ref, cv_ref, uc_ref, ucp_ref, lg_ref, lb_ref, dcv_ref, dlg_ref, dlb_ref, taps_ref, ext, dcs):
        @pl.when(pl.program_id(0) == 0)
        def _():
            dlg_ref[...] = jnp.zeros_like(dlg_ref)
            dlb_ref[...] = jnp.zeros_like(dlb_ref)
            taps_ref[...] = jnp.zeros_like(taps_ref)
        _glu_to_ext(uc_ref, ucp_ref, ext, D, pl.program_id(0) == 0)
        cv = cv_ref[...]
        mu = jnp.mean(cv, axis=-1, keepdims=True)
        xc = cv - mu
        rstd = lax.rsqrt(jnp.mean(xc * xc, axis=-1, keepdims=True) + EPS)
        yn = xc * rstd
        l = yn * lg_ref[...] + lb_ref[...]
        s = _sig(l)
        dl = dva_ref[...] * (s * (1.0 + l * (1.0 - s)))
        dlg_ref[...] += jnp.sum(dl * yn, axis=0, keepdims=True)
        dlb_ref[...] += jnp.sum(dl, axis=0, keepdims=True)
        dyn = dl * lg_ref[...]
        dcv = rstd * (dyn - jnp.mean(dyn, axis=-1, keepdims=True) - yn * jnp.mean(dyn * yn, axis=-1, keepdims=True))
        dcv_ref[...] = dcv
        for c in range(NCH):
            dcs[c] = dcv[:, c * LANES:(c + 1) * LANES]

        def fold(p):
            return p.reshape(RS // 8, 8, LANES).sum(axis=0)

        def chunk(c, carry):
            for kk in range(W):
                tot = jnp.zeros((8, LANES), F32)
                for rb in range(tm // RS):
                    tot = tot + fold(dcs[c, pl.ds(rb * RS, RS), :]
                                     * ext[c, pl.ds(rb * RS + CONV_HALO - (W - 1) + kk, RS), :])
                taps_ref[c, pl.ds(8 * kk, 8), :] += tot
            tot = jnp.zeros((8, LANES), F32)
            for rb in range(tm // RS):
                tot = tot + fold(dcs[c, pl.ds(rb * RS, RS), :])
            taps_ref[c, pl.ds(8 * W, 8), :] += tot
            return carry
        lax.fori_loop(0, NCH, chunk, 0)

    tile = pl.BlockSpec((tm, D), lambda i: (i, 0))
    vec = jax.ShapeDtypeStruct((1, D), F32)
    return pl.pallas_call(
        body, name=name, grid=(T // tm,),
        in_specs=[tile, tile, pl.BlockSpec((tm, D2), lambda i: (i, 0)),
                  pl.BlockSpec((CONV_HALO, D2), _prev_blk(tm, CONV_HALO)), _row(D), _row(D)],
        out_specs=[tile, _row(D), _row(D), pl.BlockSpec((NCH, 256, LANES), lambda i: (0, 0, 0))],
        out_shape=[jax.ShapeDtypeStruct((T, D), F32), vec, vec, jax.ShapeDtypeStruct((NCH, 256, LANES), F32)],
        scratch_shapes=[pltpu.VMEM((NCH, tm + CONV_HALO, LANES), F32), pltpu.VMEM((NCH, tm, LANES), F32)],
        compiler_params=_cp("arbitrary"))(dva, cv, uc, uc, lg, lbias)


def conv_bwd_b(dcv, uc, cw, name):
    T, D2 = uc.shape
    D = D2 // 2
    NCH = D // LANES
    W = 31
    tm = _rt(T, 256)
    RS = min(CONV_ROWS, tm)
    NT = T // tm

    def body(dcv_ref, dcn_ref, uc_ref, cw_ref, du_ref, db_ref, ext, outs):
        i = pl.program_id(0)

        @pl.when(i == 0)
        def _():
            db_ref[...] = jnp.zeros_like(db_ref)
        dm = dcv_ref[...]
        dn = jnp.where(i == NT - 1, 0.0, dcn_ref[...])
        for c in range(NCH):
            ext[c, 0:tm, :] = dm[:, c * LANES:(c + 1) * LANES]
            ext[c, tm:, :] = dn[:, c * LANES:(c + 1) * LANES]

        def chunk(c, carry):
            for rb in range(tm // RS):
                acc = jnp.zeros((RS, LANES), F32)
                for kk in range(W):
                    acc = acc + cw_ref[c, pl.ds(kk, 1), :] * ext[c, pl.ds(rb * RS + (W - 1) - kk, RS), :]
                outs[c, pl.ds(rb * RS, RS), :] = acc
            return carry
        lax.fori_loop(0, NCH, chunk, 0)
        dglu = jnp.concatenate([outs[c] for c in range(NCH)], axis=1)
        u = uc_ref[...]
        a = u[:, :D]
        sg = _sig(u[:, D:])
        da = dglu * sg
        dg = dglu * a * (sg * (1.0 - sg))
        du_ref[:, :D] = _bf(da)
        du_ref[:, D:] = _bf(dg)
        db_ref[:, :D] += jnp.sum(da, axis=0, keepdims=True)
        db_ref[:, D:] += jnp.sum(dg, axis=0, keepdims=True)

    tile = pl.BlockSpec((tm, D), lambda i: (i, 0))
    wide = pl.BlockSpec((tm, D2), lambda i: (i, 0))
    return pl.pallas_call(
        body, name=name, grid=(NT,),
        in_specs=[tile, pl.BlockSpec((CONV_HALO, D), _next_blk(tm, CONV_HALO, T)), wide,
                  pl.BlockSpec((NCH, 32, LANES), lambda i: (0, 0, 0))],
        out_specs=[wide, _row(D2)],
        out_shape=[jax.ShapeDtypeStruct((T, D2), BF16), jax.ShapeDtypeStruct((1, D2), F32)],
        scratch_shapes=[pltpu.VMEM((NCH, tm + CONV_HALO, LANES), F32), pltpu.VMEM((NCH, tm, LANES), F32)],
        compiler_params=_cp("arbitrary"))(dcv, dcv, uc, cw)


def _conv3(ext, u, uh, dw_ref, bias_ref, tm):
    ext[0:FFN_HALO, :] = uh
    ext[FFN_HALO:, :] = u
    return (dw_ref[pl.ds(0, 1), :] * ext[pl.ds(FFN_HALO - 2, tm), :]
            + dw_ref[pl.ds(1, 1), :] * ext[pl.ds(FFN_HALO - 1, tm), :]
            + dw_ref[pl.ds(2, 1), :] * u + bias_ref[...])


def ffn_fwd(h, wup, dww, dwb, wdn, x, gate, pg, name):
    T, D = h.shape
    fs = wup.shape[2]
    NJ = wup.shape[0] // 2
    tm = _rt(T, 512)

    def body(h_ref, hp_ref, wa_ref, wb_ref, dwa_ref, dwb_ref, ba_ref, bb_ref, wd_ref, x_ref, gt_ref, pg_ref,
             u_ref, uc_ref, y_ref, xo_ref, acc, exta, extb):
        i, j = pl.program_id(0), pl.program_id(1)

        @pl.when(j == 0)
        def _():
            acc[...] = jnp.zeros_like(acc)
        hb = h_ref[...]
        hp = hp_ref[...]

        def half(s, w_ref, dw_ref, b_ref, ext):
            u = _nn(hb, w_ref[...])
            uh = jnp.where(i == 0, 0.0, _nn(hp, w_ref[...]))
            u_ref[s] = u
            uc = _conv3(ext, u, uh, dw_ref, b_ref, tm)
            uc_ref[s] = uc
            return uc
        a = half(0, wa_ref, dwa_ref, ba_ref, exta)
        b = half(1, wb_ref, dwb_ref, bb_ref, extb)
        acc[...] += _nn(_bf(a * _sig(a) * b), wd_ref[...])

        @pl.when(j == NJ - 1)
        def _():
            y = acc[...]
            y_ref[...] = y
            r = lax.rsqrt(jnp.mean(y * y, axis=-1, keepdims=True) + EPS)
            xo_ref[...] = x_ref[...] + gt_ref[...] * ((y * r) * pg_ref[...])

    tile = pl.BlockSpec((tm, D), lambda i, j: (i, 0))
    shard = lambda off, r: pl.BlockSpec((None, r, fs), lambda i, j: (j + off, 0, 0))
    ush = pl.BlockSpec((2, None, tm, fs), lambda i, j: (0, j, i, 0))
    vec = pl.BlockSpec((1, D), lambda i, j: (0, 0))
    return pl.pallas_call(
        body, name=name, grid=(T // tm, NJ),
        in_specs=[tile, pl.BlockSpec((FFN_HALO, D), lambda i, j: (jnp.maximum(i * (tm // FFN_HALO) - 1, 0), 0)),
                  shard(0, D), shard(NJ, D), shard(0, 3), shard(NJ, 3), shard(0, 1), shard(NJ, 1),
                  pl.BlockSpec((None, fs, D), lambda i, j: (j, 0, 0)), tile, vec, vec],
        out_specs=[ush, ush, tile, tile],
        out_shape=[jax.ShapeDtypeStruct((2, NJ, T, fs), F32), jax.ShapeDtypeStruct((2, NJ, T, fs), F32),
                   jax.ShapeDtypeStruct((T, D), F32), jax.ShapeDtypeStruct((T, D), F32)],
        scratch_shapes=[pltpu.VMEM((tm, D), F32), pltpu.VMEM((tm + FFN_HALO, fs), F32),
                        pltpu.VMEM((tm + FFN_HALO, fs), F32)],
        compiler_params=_cp("parallel", "arbitrary"))(h, h, wup, wup, dww, dww, dwb, dwb, wdn, x, gate, pg)


def ffn_bwd_a(dy, uc, wdn, name):
    _, NJ, T, fs = uc.shape
    D = dy.shape[1]
    tm = _rt(T, 512)

    NT = T // tm

    def body(dy_ref, uc_ref, wd_ref, duc_ref, dwd_ref, acc):
        i = pl.program_id(1)

        @pl.when(i == 0)
        def _():
            acc[...] = jnp.zeros_like(acc)
        dyb = _bf(dy_ref[...])
        dz = _nt(dyb, wd_ref[...])
        a = uc_ref[0]
        b = uc_ref[1]
        sa = _sig(a)
        asa = a * sa
        acc[...] += _tn(_bf(asa * b), dyb)
        duc_ref[0] = dz * b * (sa + asa * (1.0 - sa))
        duc_ref[1] = dz * asa

        @pl.when(i == NT - 1)
        def _():
            dwd_ref[...] = _bf(acc[...])

    ush = pl.BlockSpec((2, None, tm, fs), lambda j, i: (0, j, i, 0))
    return pl.pallas_call(
        body, name=name, grid=(NJ, NT),
        in_specs=[pl.BlockSpec((tm, D), lambda j, i: (i, 0)), ush, pl.BlockSpec((None, fs, D), lambda j, i: (j, 0, 0))],
        out_specs=[ush, pl.BlockSpec((fs, D), lambda j, i: (j, 0))],
        out_shape=[jax.ShapeDtypeStruct((2, NJ, T, fs), F32), jax.ShapeDtypeStruct((NJ * fs, D), BF16)],
        scratch_shapes=[pltpu.VMEM((fs, D), F32)],
        compiler_params=_cp("parallel", "arbitrary"))(dy, uc, wdn)


def ffn_bwd_b(duc, u, h, dww, wup, name):
    NS, T, fs = duc.shape
    D = wup.shape[1]
    tm = _rt(T, 512)
    NT = T // tm

    def body(d_ref, dn_ref, u_ref, h_ref, dw_ref, w_ref, dh_ref, g_ref, dwup_ref, acc, ext, accw, stage):
        i, s = pl.program_id(0), pl.program_id(1)

        @pl.when((i == 0) & (s == 0))
        def _():
            g_ref[...] = jnp.zeros_like(g_ref)

        @pl.when(s == 0)
        def _():
            acc[...] = jnp.zeros_like(acc)
        d = d_ref[...]
        ext[0:tm, :] = d
        ext[tm:, :] = jnp.where(i == NT - 1, 0.0, dn_ref[...])
        d1 = ext[pl.ds(1, tm), :]
        d2 = ext[pl.ds(2, tm), :]
        du = _bf(dw_ref[pl.ds(2, 1), :] * d + dw_ref[pl.ds(1, 1), :] * d1 + dw_ref[pl.ds(0, 1), :] * d2)
        acc[...] += _nt(du, w_ref[...])
        dw_part = _tn(h_ref[...], du)

        @pl.when(i == 0)
        def _():
            accw[s] = dw_part

        @pl.when(i > 0)
        def _():
            accw[s] += dw_part

        @pl.when(i == NT - 1)
        def _():
            stage[...] = _bf(accw[s])
            pltpu.sync_copy(stage, dwup_ref.at[s])
        u = u_ref[...]
        g_ref[s, pl.ds(0, 1), :] += jnp.sum(d2 * u, axis=0, keepdims=True)
        g_ref[s, pl.ds(1, 1), :] += jnp.sum(d1 * u, axis=0, keepdims=True)
        g_ref[s, pl.ds(2, 1), :] += jnp.sum(d * u, axis=0, keepdims=True)
        g_ref[s, pl.ds(3, 1), :] += jnp.sum(d, axis=0, keepdims=True)

        @pl.when(s == NS - 1)
        def _():
            dh_ref[...] = acc[...]

    ush = pl.BlockSpec((None, tm, fs), lambda i, s: (s, i, 0))
    unext = pl.BlockSpec((None, FFN_HALO, fs),
                         lambda i, s: (s, jnp.minimum((i + 1) * (tm // FFN_HALO), T // FFN_HALO - 1), 0))
    tile = pl.BlockSpec((tm, D), lambda i, s: (i, 0))
    return pl.pallas_call(
        body, name=name, grid=(NT, NS),
        in_specs=[ush, unext, ush, tile, pl.BlockSpec((None, 3, fs), lambda i, s: (s, 0, 0)),
                  pl.BlockSpec((None, D, fs), lambda i, s: (s, 0, 0))],
        out_specs=[tile, pl.BlockSpec((NS, 8, fs), lambda i, s: (0, 0, 0)), ANY],
        out_shape=[jax.ShapeDtypeStruct((T, D), F32), jax.ShapeDtypeStruct((NS, 8, fs), F32),
                   jax.ShapeDtypeStruct((NS, D, fs), BF16)],
        scratch_shapes=[pltpu.VMEM((tm, D), F32), pltpu.VMEM((tm + FFN_HALO, fs), F32),
                        pltpu.VMEM((NS, D, fs), F32), pltpu.VMEM((D, fs), BF16)],
        compiler_params=pltpu.CompilerParams(dimension_semantics=("arbitrary", "arbitrary"),
                                             vmem_limit_bytes=FFN_BWD_VMEM))(duc, duc, u, h, dww, wup)


def ada_fwd(c_all, w, bias, name):
    L, D, n = w.shape

    def body(c_ref, w_ref, b_ref, o_ref):
        cv = c_ref[...]
        o_ref[...] = _nn(_bf(cv * _sig(cv)), _bf(w_ref[...])) + b_ref[...]

    return pl.pallas_call(
        body, name=name, grid=(L,),
        in_specs=[pl.BlockSpec((N_DEV, D), lambda l: (0, 0)), pl.BlockSpec((None, D, n), lambda l: (l, 0, 0)),
                  pl.BlockSpec((None, 1, n), lambda l: (l, 0, 0))],
        out_specs=pl.BlockSpec((None, N_DEV, n), lambda l: (l, 0, 0)),
        out_shape=jax.ShapeDtypeStruct((L, N_DEV, n), F32), compiler_params=_cp("parallel"))(c_all, w, bias)


def ada_bwd(c_all, dm, name):
    L, _, n = dm.shape
    D = c_all.shape[1]

    def body(c_ref, d_ref, o_ref):
        cv = c_ref[...]
        o_ref[...] = _tn(_bf(cv * _sig(cv)), _bf(d_ref[...]))

    return pl.pallas_call(
        body, name=name, grid=(L,),
        in_specs=[pl.BlockSpec((N_DEV, D), lambda l: (0, 0)), pl.BlockSpec((None, N_DEV, n), lambda l: (l, 0, 0))],
        out_specs=pl.BlockSpec((None, D, n), lambda l: (l, 0, 0)),
        out_shape=jax.ShapeDtypeStruct((L, D, n), F32), compiler_params=_cp("parallel"))(c_all, dm)


def sum_slots(g, name):
    S, R, C = g.shape

    def body(g_ref, o_ref):
        acc = g_ref[0]
        for s in range(1, S):
            acc = acc + g_ref[s]
        o_ref[...] = acc

    return pl.pallas_call(
        body, name=name, grid=(1,), in_specs=[pl.BlockSpec((S, R, C), lambda i: (0, 0, 0))],
        out_specs=pl.BlockSpec((R, C), lambda i: (0, 0)), out_shape=jax.ShapeDtypeStruct((R, C), F32),
        compiler_params=_cp("arbitrary"))(g)


def lb_softmax(logits, name):
    def body(l_ref, s_ref):
        l = l_ref[...]
        e = jnp.exp(l - jnp.max(l, axis=0, keepdims=True))
        s_ref[...] = e / jnp.sum(e, axis=0, keepdims=True)
    return pl.pallas_call(body, name=name, out_shape=jax.ShapeDtypeStruct(logits.shape, F32))(logits)


def adamw(w, gs, m, v, name):
    S, R, C = gs.shape
    tr = R
    budget = (4 * 1024 * 1024) // (4 * (S + 7) * C)
    if R > budget:
        tr = max(t for t in range(16, budget + 1, 16) if R % t == 0)

    def body(w_ref, g_ref, m_ref, v_ref, go_ref, d_ref, mo_ref, vo_ref):
        g = g_ref[0].astype(F32)
        for s in range(1, S):
            g = g + g_ref[s].astype(F32)
        go_ref[...] = g
        mn = ADAM_B1 * m_ref[...] + (1.0 - ADAM_B1) * g
        vn = ADAM_B2 * v_ref[...] + (1.0 - ADAM_B2) * (g * g)
        mo_ref[...] = mn
        vo_ref[...] = vn
        m_hat = mn / (1.0 - ADAM_B1 ** ADAM_STEP)
        v_hat = vn / (1.0 - ADAM_B2 ** ADAM_STEP)
        d_ref[...] = -ADAM_LR * (m_hat / (jnp.sqrt(v_hat) + ADAM_EPS) + ADAM_WD * w_ref[...])

    tile = pl.BlockSpec((tr, C), lambda i: (i, 0))
    out = jax.ShapeDtypeStruct((R, C), F32)
    return pl.pallas_call(
        body, name=name, grid=(R // tr,), in_specs=[tile, pl.BlockSpec((S, tr, C), lambda i: (0, i, 0)), tile, tile],
        out_specs=[tile] * 4, out_shape=[out] * 4, compiler_params=_cp("parallel"))(w, gs, m, v)


def _place():
    return lax.axis_index("x"), lax.axis_index("y"), lax.axis_index("c")


def _slot(p):
    return 4 * p[0] + 2 * p[1] + p[2]


class _Gather:
    def __init__(self, xs):
        self.ins = list(xs)
        n = self.n = len(xs)
        self.out_shapes = [jax.ShapeDtypeStruct((N_DEV,) + a.shape, a.dtype) for a in xs]
        self.scratch = [pltpu.SemaphoreType.DMA((n, 7)), pltpu.SemaphoreType.DMA((n, 7)), pltpu.SemaphoreType.DMA((n,))]

    def _parts(self, ins, outs, sems):
        send_sems, recv_sems, local_sems = sems
        x, y, c = _place()
        me, sib = (x, y, c), (x, y, 1 - c)
        chips = [(1 - x, y), (x, 1 - y), (1 - x, 1 - y)]

        def copy(a, k, block, to, src=None):
            dst = outs[a].at[_slot(block)]
            return pltpu.make_async_remote_copy(
                src_ref=dst if src is None else src, dst_ref=dst, send_sem=send_sems.at[a, k],
                recv_sem=recv_sems.at[a, k], device_id=to, device_id_type=MESH)

        mine = [pltpu.make_async_copy(ins[a], outs[a].at[_slot(me)], local_sems.at[a]) for a in range(self.n)]
        first = []
        for a in range(self.n):
            first.append(copy(a, 0, me, sib, src=ins[a]))
            first += [copy(a, 1 + j, me, (*chip, c), src=ins[a]) for j, chip in enumerate(chips)]
        return copy, mine, first, me, sib, chips, c

    def start(self, ins, outs, sems):
        _, mine, first, *_ = self._parts(ins, outs, sems)
        for cp in mine + first:
            cp.start()

    def finish(self, ins, outs, sems):
        copy, mine, first, me, sib, chips, c = self._parts(ins, outs, sems)
        passed = []
        for j, chip in enumerate(chips):
            for a in range(self.n):
                copy(a, 1 + j, (*chip, c), me).wait_recv()
                fwd = copy(a, 4 + j, (*chip, c), sib)
                fwd.start()
                passed.append(fwd)
        for a in range(self.n):
            copy(a, 0, sib, me).wait_recv()
            for j, chip in enumerate(chips):
                copy(a, 4 + j, (*chip, 1 - c), me).wait_recv()
        for cp in first + passed:
            cp.wait_send()
        for cp in mine:
            cp.wait()


def _run_alone(ex, name):
    def body(*refs):
        ni, no = len(ex.ins), len(ex.out_shapes)
        parts = refs[:ni], refs[ni:ni + no], refs[ni + no:]
        ex.start(*parts)
        ex.finish(*parts)
    return pl.pallas_call(body, name=name, in_specs=[ANY] * len(ex.ins), out_specs=[ANY] * len(ex.out_shapes),
                          out_shape=ex.out_shapes, scratch_shapes=ex.scratch)(*ex.ins)


def _ride(body, n_in, n_out, ex, first, last):
    ni, no = len(ex.ins), len(ex.out_shapes)

    def wrapped(*refs):
        a, r_in = refs[:n_in], refs[n_in:n_in + ni]
        b, r_out = refs[n_in + ni:n_in + ni + n_out], refs[n_in + ni + n_out:n_in + ni + n_out + no]
        rest = refs[n_in + ni + n_out + no:]
        s, r_s = rest[:len(rest) - len(ex.scratch)], rest[len(rest) - len(ex.scratch):]

        @pl.when(first())
        def _():
            ex.start(r_in, r_out, r_s)
        body(*a, *b, *s)

        @pl.when(last())
        def _():
            ex.finish(r_in, r_out, r_s)
    return wrapped


def all_gather(xs, name):
    return _run_alone(_Gather(xs), name)


class _Exchange:
    def __init__(self, groups):
        self.ins = [a for grp in groups for a in grp]
        self.where = [(g, i) for g, grp in enumerate(groups) for i in range(len(grp))]
        n = self.n = len(self.ins)
        self.out_shapes = [jax.ShapeDtypeStruct((N_DEV, len(grp)) + grp[0].shape[1:], grp[0].dtype) for grp in groups]
        self.scratch = [pltpu.SemaphoreType.DMA((n, 7)), pltpu.SemaphoreType.DMA((n, 7)), pltpu.SemaphoreType.DMA((n,))]

    def _parts(self, ins, outs, sems):
        send_sems, recv_sems, local_sems = sems
        x, y, c = _place()
        me = (x, y, c)

        def peer(r):
            return (1 - x if r & 4 else x, 1 - y if r & 2 else y, 1 - c if r & 1 else c)

        def land(a, src):
            g, i = self.where[a]
            return outs[g].at[_slot(src), i]

        def copy(a, r, src_dev):
            p = peer(r)
            return pltpu.make_async_remote_copy(
                src_ref=ins[a].at[_slot(p)], dst_ref=land(a, src_dev), send_sem=send_sems.at[a, r - 1],
                recv_sem=recv_sems.at[a, r - 1], device_id=p, device_id_type=MESH)

        mine = [pltpu.make_async_copy(ins[a].at[_slot(me)], land(a, me), local_sems.at[a]) for a in range(self.n)]
        sends = [copy(a, r, me) for r in range(1, N_DEV) for a in range(self.n)]
        arrivals = [copy(a, r, peer(r)) for r in range(1, N_DEV) for a in range(self.n)]
        return mine, sends, arrivals

    def start(self, ins, outs, sems):
        mine, sends, _ = self._parts(ins, outs, sems)
        for cp in mine + sends:
            cp.start()

    def finish(self, ins, outs, sems):
        mine, sends, arrivals = self._parts(ins, outs, sems)
        for cp in arrivals:
            cp.wait_recv()
        for cp in sends:
            cp.wait_send()
        for cp in mine:
            cp.wait()


def all_to_all(groups, name):
    return _run_alone(_Exchange(groups), name)


def _mods(mod, l, D):
    return [mod[l:l + 1, k * D:(k + 1) * D] for k in range(6)]


def _ffn_backward(l, dxo, hb, u, uc, yf, xin, g2, sc2, gains, W):
    _, NJ, T, fs = u.shape
    D = xin.shape[1]
    dy, dgate, dpg, _ = postnorm_bwd(dxo, yf, g2, gains["post_ffn_g"][l:l + 1], f"ffn{l}_postnorm_bwd")
    duc, d_wdn = ffn_bwd_a(dy, uc, W["wdn"][l], f"ffn{l}_bwd_a")
    dh, taps, d_wup = ffn_bwd_b(duc.reshape(2 * NJ, T, fs), u.reshape(2 * NJ, T, fs), hb, W["fdw_w"][l],
                                W["wup"][l], f"ffn{l}_bwd_b")
    dx, da, dsh = prenorm_bwd(dh, xin, dxo, gains["pre_ffn_g"][l:l + 1], sc2, f"ffn{l}_prenorm_bwd")
    small = dict(dgate=dgate, dpost=dpg, da=da, dsh=dsh, dwb=taps[:, 3], dww=taps[:, 0:3])
    return dx, d_wup, d_wdn, small


def _local_step(xt, tgt, mod, gains, W, mine):
    T, D = xt.shape
    zero_d = jnp.zeros((1, D), F32)
    half = lambda g: g.reshape(N_DEV // 2, 2 * g.shape[1], D)
    sh1, sc1, g1, sh2, sc2, g2 = m0 = _mods(mod, 0, D)
    h0 = prenorm_mod(xt, gains["pre_mix_g"][0:1], sc1, sh1, "l0_prenorm")
    proj, (whout, wup0, wdn0) = mm_nn(h0, W["whin"], jnp.zeros((1, 4 * D), F32), "hgrn_proj",
                                      _Gather([mine["whout"], mine["wup"][0], mine["wdn"][0]]))
    o, og, st, (wcin, wcout, wup1, wdn1) = hgrn_fwd(
        proj, W["lb"], W["gg"], "hgrn_fwd", _Gather([mine["wcin"], mine["wcout"], mine["wup"][1], mine["wdn"][1]]))
    W = dict(W, whout=whout.reshape(D, D), wcin=wcin, wcout=wcout.reshape(D, D), wup=[wup0, wup1],
             wdn=[half(wdn0), half(wdn1)])
    y0, x1 = mm_post(og, W["whout"], zero_d, xt, g1, gains["post_mix_g"][0:1], "hgrn_out")
    h0f = prenorm_mod(x1, gains["pre_ffn_g"][0:1], sc2, sh2, "f0_prenorm")
    u0, uc0, yf0, x2 = ffn_fwd(h0f, W["wup"][0], W["fdw_w"][0], W["fdw_b"][0], W["wdn"][0], x1, g2,
                                gains["post_ffn_g"][0:1], "ffn0_fwd")
    t1, c1, e1, t2, c2, e2 = m1 = _mods(mod, 1, D)
    h1 = prenorm_mod(x2, gains["pre_mix_g"][1:2], c1, t1, "l1_prenorm")
    uc = mm_nn(h1, W["wcin"], W["cb_in"], "conv_in")
    cv, va = conv_mid_fwd(uc, W["cw"], W["cdw_b"], W["ln_g"], W["ln_b"], "conv_mid_fwd")
    y1, x3 = mm_post(va, W["wcout"], W["cb_out"], x2, e1, gains["post_mix_g"][1:2], "conv_out")
    h1f = prenorm_mod(x3, gains["pre_ffn_g"][1:2], c2, t2, "f1_prenorm")
    u1, uc1, yf1, x4 = ffn_fwd(h1f, W["wup"][1], W["fdw_w"][1], W["fdw_b"][1], W["wdn"][1], x3, e2,
                                gains["post_ffn_g"][1:2], "ffn1_fwd")
    dx4, loss_part = loss_grad(x4, tgt, "loss_grad")
    dx3, d_wup1, d_wdn1, sf1 = _ffn_backward(1, dx4, h1f, u1, uc1, yf1, x3, e2, c2, gains, W)
    dy1, dgate_c, dpost_c, dys_c = postnorm_bwd(dx3, y1, e1, gains["post_mix_g"][1:2], "conv_postnorm_bwd")
    dva = mm_nt(dy1, W["wcout"][None], "conv_dva", True)
    d_wcout = mm_tn(va, dy1, "conv_dwout", True, True, D, D)
    dcv, dlg, dlbias, taps = conv_bwd_a(dva, cv, uc, W["ln_g"], W["ln_b"], "conv_bwd_a")
    duc, dbin = conv_bwd_b(dcv, uc, W["cw"], "conv_bwd_b")
    dh1 = mm_nt(duc, W["wcin"], "conv_dh", True)
    d_wcin = mm_tn(h1, duc, "conv_dwin", True, True, D, W["wcin"].shape[2])
    dx2, da_c, dsh_c = prenorm_bwd(dh1, x2, dx3, gains["pre_mix_g"][1:2], c1, "l1_prenorm_bwd")
    dx1, d_wup0, d_wdn0, sf0 = _ffn_backward(0, dx2, h0f, u0, uc0, yf0, x1, g2, sc2, gains, W)
    dy0, dgate_h, dpost_h, _ = postnorm_bwd(dx1, y0, g1, gains["post_mix_g"][0:1], "hgrn_postnorm_bwd")
    dog = mm_nt(dy0, W["whout"][None], "hgrn_dog", True)
    d_whout = mm_tn(og, dy0, "hgrn_dwout", True, True, D, D)
    rows = lambda g: g.reshape(N_DEV, -1, D)
    dproj, dlb, dgg, (r_wcin, r_wcout, r_wup, r_wdn) = hgrn_bwd(
        proj, o, dog, st, W["lb"], W["gg"], "hgrn_bwd",
        _Exchange([[d_wcin], [rows(d_wcout)], [d_wup0, d_wup1], [rows(d_wdn0), rows(d_wdn1)]]))
    dh0 = mm_nt(dproj, W["whin"], "hgrn_dh", True)
    d_whin = mm_tn(h0, dproj, "hgrn_dwin", True, True, D, W["whin"].shape[2])
    dx0, da_h, dsh_h = prenorm_bwd(dh0, xt, dx1, gains["pre_mix_g"][0:1], sc1, "l0_prenorm_bwd")
    r_whin, r_whout = all_to_all([[d_whin], [rows(d_whout)]], "exchange_hgrn_grads")

    big = dict(hgrn_w_in=r_whin, hgrn_w_out=r_whout, conv_w_in=r_wcin, conv_w_out=r_wcout, ffn_w_up=r_wup,
               ffn_w_down=r_wdn)
    pm, pf = gains["pre_mix_g"], gains["pre_ffn_g"]
    dmod = jnp.concatenate([
        dsh_h, da_h * pm[0:1], dgate_h, sf0["dsh"], sf0["da"] * pf[0:1], sf0["dgate"],
        dsh_c, da_c * pm[1:2], dgate_c, sf1["dsh"], sf1["da"] * pf[1:2], sf1["dgate"]], axis=1)
    NCH = D // LANES
    tap = taps.reshape(NCH, 32, 8, LANES).sum(axis=2)
    small = dict(
        dmod=dmod,
        pre_mix_g=jnp.concatenate([da_h * (1.0 + sc1), da_c * (1.0 + c1)], axis=0),
        post_mix_g=jnp.concatenate([dpost_h, dpost_c], axis=0),
        pre_ffn_g=jnp.concatenate([sf0["da"] * (1.0 + sc2), sf1["da"] * (1.0 + c2)], axis=0),
        post_ffn_g=jnp.concatenate([sf0["dpost"], sf1["dpost"]], axis=0),
        lb=dlb, gg=dgg,
        ffn_dw_b=jnp.stack([sf0["dwb"], sf1["dwb"]]),
        conv_b_in=dbin,
        conv_dw_w=jnp.transpose(tap[:, :31], (1, 0, 2)).reshape(31, D),
        conv_dw_b=tap[:, 31].reshape(1, D),
        conv_ln_g=dlg, conv_ln_b=dlbias, conv_b_out=dys_c,
        ffn_dw_w=jnp.stack([sf0["dww"], sf1["dww"]]))
    return loss_part, dx0, big, small


SMALL_ORDER = ["dmod", "pre_mix_g", "post_mix_g", "pre_ffn_g", "post_ffn_g", "lb", "gg", "ffn_dw_b", "conv_b_in",
               "conv_dw_w", "conv_dw_b", "conv_ln_g", "conv_ln_b", "conv_b_out", "ffn_dw_w"]


def _pack(parts):
    flat = jnp.concatenate([p.reshape(-1) for p in parts])
    pad = (-flat.shape[0]) % LANES
    if pad:
        flat = jnp.concatenate([flat, jnp.zeros((pad,), F32)])
    return flat.reshape(-1, LANES)


def _unpack(flat, shapes):
    out, off = [], 0
    for s in shapes:
        n = 1
        for d in s:
            n *= d
        out.append(flat[off:off + n].reshape(s))
        off += n
    return out


def _apply_adamw(name, w, g_slots, m, v):
    shp = w.shape
    C = shp[-1]
    R = w.size // C
    g, d, mn, vn = adamw(w.reshape(R, C), g_slots.reshape(g_slots.shape[0], R, C), m.reshape(R, C), v.reshape(R, C),
                         "adamw_" + name)
    return g.reshape(shp), d.reshape(shp), mn.reshape(shp), vn.reshape(shp)


WEIGHTS = ['ada_w', 'ada_b', 'pre_mix_g', 'post_mix_g', 'pre_ffn_g', 'post_ffn_g', 'hgrn_w_in', 'hgrn_lb_logits',
           'hgrn_gnorm_g', 'hgrn_w_out', 'conv_w_in', 'conv_b_in', 'conv_dw_w', 'conv_dw_b', 'conv_ln_g', 'conv_ln_b',
           'conv_w_out', 'conv_b_out', 'ffn_w_up', 'ffn_dw_w', 'ffn_dw_b', 'ffn_w_down']


def kernel(x, c, ada_w, ada_b, pre_mix_g, post_mix_g, pre_ffn_g, post_ffn_g, hgrn_w_in, hgrn_lb_logits, hgrn_gnorm_g, hgrn_w_out, conv_w_in, conv_b_in, conv_dw_w, conv_dw_b, conv_ln_g, conv_ln_b, conv_w_out, conv_b_out, ffn_w_up, ffn_dw_w, ffn_dw_b, ffn_w_down, loss_target, m_ada_w, m_ada_b, m_pre_mix_g, m_post_mix_g, m_pre_ffn_g, m_post_ffn_g, m_hgrn_w_in, m_hgrn_lb_logits, m_hgrn_gnorm_g, m_hgrn_w_out, m_conv_w_in, m_conv_b_in, m_conv_dw_w, m_conv_dw_b, m_conv_ln_g, m_conv_ln_b, m_conv_w_out, m_conv_b_out, m_ffn_w_up, m_ffn_dw_w, m_ffn_dw_b, m_ffn_w_down, v_ada_w, v_ada_b, v_pre_mix_g, v_post_mix_g, v_pre_ffn_g, v_post_ffn_g, v_hgrn_w_in, v_hgrn_lb_logits, v_hgrn_gnorm_g, v_hgrn_w_out, v_conv_w_in, v_conv_b_in, v_conv_dw_w, v_conv_dw_b, v_conv_ln_g, v_conv_ln_b, v_conv_w_out, v_conv_b_out, v_ffn_w_up, v_ffn_dw_w, v_ffn_dw_b, v_ffn_w_down):
    P = dict(locals())
    _, T, D = x.shape
    me = _slot(_place())
    L = ada_w.shape[0]
    na = ada_w.shape[2]
    fs = ffn_w_up.shape[2]
    nci = conv_w_in.shape[2]
    nd = conv_dw_b.shape[1]
    NCH = D // LANES

    small_in = [c, conv_b_in, conv_dw_w, conv_dw_b, conv_ln_g, conv_ln_b, conv_b_out, ffn_dw_w]
    whin, packed = all_gather([_bf(hgrn_w_in[0]), _pack(small_in)], "gather_first")
    mine = dict(whout=_bf(hgrn_w_out[0]), wcin=_bf(conv_w_in[0]), wcout=_bf(conv_w_out[0]),
                wup=[_bf(ffn_w_up[l]) for l in range(L)], wdn=[_bf(ffn_w_down[l]) for l in range(L)])
    sm = packed.reshape(N_DEV, -1)
    offs = [0]
    for a in small_in:
        offs.append(offs[-1] + a.size)
    piece = lambda k, shp: sm[:, offs[k]:offs[k + 1]].reshape((N_DEV,) + shp)
    c_all = piece(0, (D,))
    dw_nat = jnp.transpose(piece(2, (31, nd)), (1, 0, 2)).reshape(31, D)
    cw = jnp.pad(jnp.transpose(dw_nat.reshape(31, NCH, LANES), (1, 0, 2)), ((0, 0), (0, 1), (0, 0)))
    fdw = piece(7, (L, 3, fs))
    lb_soft = lb_softmax(hgrn_lb_logits, "lb_softmax")
    W = dict(
        whin=whin, lb=lb_soft[0:1], gg=hgrn_gnorm_g, cb_in=piece(1, (nci,)).reshape(1, N_DEV * nci), cw=cw,
        cdw_b=piece(3, (nd,)).reshape(1, D), ln_g=piece(4, (nd,)).reshape(1, D), ln_b=piece(5, (nd,)).reshape(1, D),
        cb_out=piece(6, (nd,)).reshape(1, D), fdw_w=[fdw[:, l] for l in range(L)],
        fdw_b=[ffn_dw_b[l].reshape(N_DEV, 1, fs) for l in range(L)])

    bias_mine = lax.dynamic_slice(ada_b, (0, me * na), (L, na)).reshape(L, 1, na)
    mod_cols = ada_fwd(c_all, ada_w, bias_mine, "ada_fwd")
    (mod_all,) = all_gather([mod_cols], "gather_mod")
    mod = jnp.transpose(lax.dynamic_index_in_dim(mod_all, me, axis=2, keepdims=False), (1, 0, 2)).reshape(L, N_DEV * na)

    gains = dict(pre_mix_g=pre_mix_g, post_mix_g=post_mix_g, pre_ffn_g=pre_ffn_g, post_ffn_g=post_ffn_g)
    loss_part, dx0, big, small = _local_step(x[0], loss_target[0], mod, gains, W, mine)
    loss = lax.psum((0.5 / D) * jnp.sum(loss_part), ("x", "y", "c"))

    parts = [small[k] for k in SMALL_ORDER]
    (sm_all,) = all_gather([_pack(parts)], "gather_small_grads")
    tot = _unpack(sum_slots(sm_all, "sum_small_grads").reshape(-1), [p.shape for p in parts])
    tot = dict(zip(SMALL_ORDER, tot))
    dmod_all = sm_all.reshape(N_DEV, -1)[:, :L * 6 * D].reshape(N_DEV, L, 6 * D)
    dm_mine = jnp.transpose(lax.dynamic_slice(dmod_all, (0, 0, me * na), (N_DEV, L, na)), (1, 0, 2))
    s0 = lb_soft[0:1]
    onehot = (lax.broadcasted_iota(jnp.int32, lb_soft.shape, 0) == 0).astype(F32)
    col = lambda a, n: lax.dynamic_slice_in_dim(a, me * n, n, axis=a.ndim - 1)
    G = {
        'ada_w': ada_bwd(c_all, dm_mine, "ada_bwd")[None],
        'ada_b': tot["dmod"].reshape(1, L, 6 * D),
        'pre_mix_g': tot["pre_mix_g"][None], 'post_mix_g': tot["post_mix_g"][None],
        'pre_ffn_g': tot["pre_ffn_g"][None], 'post_ffn_g': tot["post_ffn_g"][None],
        'hgrn_lb_logits': (tot["lb"] * s0 * (onehot - lb_soft))[None],
        'hgrn_gnorm_g': tot["gg"][None],
        'ffn_dw_b': tot["ffn_dw_b"].reshape(1, L, N_DEV * fs),
        'conv_b_in': col(tot["conv_b_in"], nci)[None],
        'conv_dw_w': col(tot["conv_dw_w"], nd)[None, None],
        'conv_dw_b': col(tot["conv_dw_b"], nd)[None], 'conv_ln_g': col(tot["conv_ln_g"], nd)[None],
        'conv_ln_b': col(tot["conv_ln_b"], nd)[None], 'conv_b_out': col(tot["conv_b_out"], nd)[None],
        'ffn_dw_w': lax.dynamic_index_in_dim(tot["ffn_dw_w"], me, axis=1, keepdims=False)[None],
    }

    G.update(big)
    res = {n: _apply_adamw(n, P[n], G[n], P["m_" + n], P["v_" + n]) for n in WEIGHTS}
    return (loss, dx0[None], *[res[n][0] for n in WEIGHTS], *[res[n][1] for n in WEIGHTS],
            *[res[n][2] for n in WEIGHTS], *[res[n][3] for n in WEIGHTS])
```

```python
import functools

import jax
import jax.numpy as jnp
from jax import lax
from jax.experimental import pallas as pl
from jax.experimental.pallas import tpu as pltpu

F32 = jnp.float32
BF16 = jnp.bfloat16
EPS = 1e-6
LANES = 128
N_DEV = 8
VMEM_LIMIT = 48 * 1024 * 1024
FFN_BWD_VMEM = 58 * 1024 * 1024
MM_ROWS = 1024
HG_CHUNK = 128
HG_SUB = 16
EXP_CLAMP = 80.0
CONV_HALO = 32
FFN_HALO = 8
CONV_ROWS = 32
ADAM_LR, ADAM_B1, ADAM_B2, ADAM_EPS, ADAM_WD, ADAM_STEP = 0.001, 0.9, 0.999, 1e-08, 0.01, 10
MESH = pl.DeviceIdType.MESH
ANY = pl.BlockSpec(memory_space=pl.ANY)


def _cp(*sem):
    return pltpu.CompilerParams(dimension_semantics=sem, vmem_limit_bytes=VMEM_LIMIT)


def _rt(n, t):
    t = min(n, t)
    assert n % t == 0, (n, t)
    return t


def _sig(x):
    return jax.nn.sigmoid(x)


def _nt(a, b):
    return lax.dot_general(a, b, (((1,), (1,)), ((), ())), preferred_element_type=F32)


def _tn(a, b):
    return lax.dot_general(a, b, (((0,), (0,)), ((), ())), preferred_element_type=F32)


def _nn(a, b):
    return jnp.dot(a, b, preferred_element_type=F32)


def _bf(x):
    return x.astype(BF16)


def _split2(x):
    h = _bf(x)
    return h, _bf(x - h.astype(F32))


def _x3(dot, a, b):
    ah, al = _split2(a)
    bh, bl = _split2(b)
    return dot(ah, bh) + (dot(ah, bl) + dot(al, bh))


def _nn_hi(a, b):
    return _x3(_nn, a, b)


def _nt_hi(a, b):
    return _x3(_nt, a, b)


def _tn_hi(a, b):
    return _x3(_tn, a, b)


def _nn_st(a, b):
    return _nn(_bf(a), _bf(b))


def _tn_st(a, b):
    return _tn(_bf(a), _bf(b))


def _row(d):
    return pl.BlockSpec((1, d), lambda *_: (0, 0))


def prenorm_mod(x, g, sc, sh, name):
    T, D = x.shape
    tm = _rt(T, 512)

    def body(x_ref, g_ref, sc_ref, sh_ref, h_ref):
        xv = x_ref[...]
        r = lax.rsqrt(jnp.mean(xv * xv, axis=-1, keepdims=True) + EPS)
        h_ref[...] = _bf((xv * r) * g_ref[...] * (1.0 + sc_ref[...]) + sh_ref[...])

    tile = pl.BlockSpec((tm, D), lambda i: (i, 0))
    return pl.pallas_call(
        body, name=name, grid=(T // tm,), in_specs=[tile, _row(D), _row(D), _row(D)], out_specs=tile,
        out_shape=jax.ShapeDtypeStruct((T, D), BF16), compiler_params=_cp("parallel"))(x, g, sc, sh)


def prenorm_bwd(dh, x, dxres, g, sc, name):
    T, D = x.shape
    tm = _rt(T, 512)

    def body(dh_ref, x_ref, dr_ref, g_ref, sc_ref, dx_ref, da_ref, dsh_ref):
        @pl.when(pl.program_id(0) == 0)
        def _():
            da_ref[...] = jnp.zeros_like(da_ref)
            dsh_ref[...] = jnp.zeros_like(dsh_ref)
        xv = x_ref[...]
        dh = dh_ref[...]
        r = lax.rsqrt(jnp.mean(xv * xv, axis=-1, keepdims=True) + EPS)
        n = xv * r
        da_ref[...] += jnp.sum(dh * n, axis=0, keepdims=True)
        dsh_ref[...] += jnp.sum(dh, axis=0, keepdims=True)
        dn = dh * (g_ref[...] * (1.0 + sc_ref[...]))
        dx_ref[...] = dr_ref[...] + r * (dn - n * jnp.mean(dn * n, axis=-1, keepdims=True))

    tile = pl.BlockSpec((tm, D), lambda i: (i, 0))
    vec = jax.ShapeDtypeStruct((1, D), F32)
    return pl.pallas_call(
        body, name=name, grid=(T // tm,), in_specs=[tile, tile, tile, _row(D), _row(D)],
        out_specs=[tile, _row(D), _row(D)], out_shape=[jax.ShapeDtypeStruct((T, D), F32), vec, vec],
        compiler_params=_cp("arbitrary"))(dh, x, dxres, g, sc)


def postnorm_bwd(dxo, y, gate, pg, name):
    T, D = y.shape
    tm = _rt(T, 512)

    def body(d_ref, y_ref, gt_ref, pg_ref, dy_ref, dgate_ref, dpg_ref, dys_ref):
        @pl.when(pl.program_id(0) == 0)
        def _():
            dgate_ref[...] = jnp.zeros_like(dgate_ref)
            dpg_ref[...] = jnp.zeros_like(dpg_ref)
            dys_ref[...] = jnp.zeros_like(dys_ref)
        yv = y_ref[...]
        d = d_ref[...]
        r = lax.rsqrt(jnp.mean(yv * yv, axis=-1, keepdims=True) + EPS)
        n = yv * r
        dgate_ref[...] += jnp.sum(d * n, axis=0, keepdims=True) * pg_ref[...]
        dpg_ref[...] += jnp.sum(d * n, axis=0, keepdims=True) * gt_ref[...]
        dn = d * (gt_ref[...] * pg_ref[...])
        dy = r * (dn - n * jnp.mean(dn * n, axis=-1, keepdims=True))
        dy_ref[...] = _bf(dy)
        dys_ref[...] += jnp.sum(dy, axis=0, keepdims=True)

    tile = pl.BlockSpec((tm, D), lambda i: (i, 0))
    vec = jax.ShapeDtypeStruct((1, D), F32)
    return pl.pallas_call(
        body, name=name, grid=(T // tm,), in_specs=[tile, tile, _row(D), _row(D)],
        out_specs=[tile, _row(D), _row(D), _row(D)], out_shape=[jax.ShapeDtypeStruct((T, D), BF16), vec, vec, vec],
        compiler_params=_cp("arbitrary"))(dxo, y, gate, pg)


def loss_grad(xo, tgt, name):
    T, D = xo.shape
    tm = _rt(T, 512)

    def body(x_ref, t_ref, dx_ref, part_ref):
        @pl.when(pl.program_id(0) == 0)
        def _():
            part_ref[...] = jnp.zeros_like(part_ref)
        e = x_ref[...] - t_ref[...]
        dx_ref[...] = e * (1.0 / D)
        e2 = (e * e).reshape(tm // 8, 8, D).sum(axis=0)
        acc = e2[:, 0:LANES]
        for j in range(1, D // LANES):
            acc = acc + e2[:, j * LANES:(j + 1) * LANES]
        part_ref[...] += acc

    tile = pl.BlockSpec((tm, D), lambda i: (i, 0))
    return pl.pallas_call(
        body, name=name, grid=(T // tm,), in_specs=[tile, tile],
        out_specs=[tile, pl.BlockSpec((8, LANES), lambda i: (0, 0))],
        out_shape=[jax.ShapeDtypeStruct((T, D), F32), jax.ShapeDtypeStruct((8, LANES), F32)],
        compiler_params=_cp("arbitrary"))(xo, tgt)


def _call(body, name, grid, in_specs, out_specs, out_shape, scratch, sems, operands, ex=None):
    if ex is not None:
        def first():
            return functools.reduce(lambda p, q: p & q, [pl.program_id(k) == 0 for k in range(len(grid))])

        def last():
            return functools.reduce(lambda p, q: p & q, [pl.program_id(k) == grid[k] - 1 for k in range(len(grid))])
        body = _ride(body, len(in_specs), len(out_specs), ex, first, last)
        in_specs = in_specs + [ANY] * len(ex.ins)
        out_specs = out_specs + [ANY] * len(ex.out_shapes)
        out_shape = out_shape + ex.out_shapes
        scratch = scratch + ex.scratch
        operands = list(operands) + ex.ins
        sems = ("arbitrary",) * len(grid)
    res = pl.pallas_call(body, name=name, grid=grid, in_specs=in_specs, out_specs=out_specs, out_shape=out_shape,
                         scratch_shapes=scratch, compiler_params=_cp(*sems))(*operands)
    n_own = len(res) - (len(ex.out_shapes) if ex is not None else 0)
    return res[:n_own], res[n_own:]


def mm_nn(a, bg, bias, name, ex=None):
    M, K = a.shape
    G, _, n = bg.shape
    tm = _rt(M, MM_ROWS)

    def body(a_ref, b_ref, bias_ref, o_ref):
        o_ref[...] = _nn(a_ref[...], b_ref[...]) + bias_ref[...]

    (out,), rider = _call(
        body, name, (M // tm, G),
        [pl.BlockSpec((tm, K), lambda i, j: (i, 0)), pl.BlockSpec((None, K, n), lambda i, j: (j, 0, 0)),
         pl.BlockSpec((1, n), lambda i, j: (0, j))],
        [pl.BlockSpec((tm, n), lambda i, j: (i, j))], [jax.ShapeDtypeStruct((M, G * n), F32)], [],
        ("parallel", "arbitrary"), (a, bg, bias), ex)
    return (out, rider) if ex is not None else out


def mm_post(a, b, bias, x, gate, pg, name):
    T, K = a.shape
    D = b.shape[1]
    tm = _rt(T, 512)

    def body(a_ref, b_ref, bias_ref, x_ref, gt_ref, pg_ref, y_ref, xo_ref):
        y = _nn(a_ref[...], b_ref[...]) + bias_ref[...]
        y_ref[...] = y
        r = lax.rsqrt(jnp.mean(y * y, axis=-1, keepdims=True) + EPS)
        xo_ref[...] = x_ref[...] + gt_ref[...] * ((y * r) * pg_ref[...])

    tile = pl.BlockSpec((tm, D), lambda i: (i, 0))
    out = jax.ShapeDtypeStruct((T, D), F32)
    return pl.pallas_call(
        body, name=name, grid=(T // tm,),
        in_specs=[pl.BlockSpec((tm, K), lambda i: (i, 0)), pl.BlockSpec((K, D), lambda i: (0, 0)), _row(D), tile,
                  _row(D), _row(D)],
        out_specs=[tile, tile], out_shape=[out, out], compiler_params=_cp("parallel"))(a, b, bias, x, gate, pg)


def mm_nt(dy, wg, name, natural, ex=None):
    G, K, n = wg.shape
    T = dy.shape[0] if natural else dy.shape[1]
    tm = _rt(T, MM_ROWS)

    def body(dy_ref, w_ref, o_ref, acc):
        j = pl.program_id(1)

        @pl.when(j == 0)
        def _():
            acc[...] = jnp.zeros_like(acc)
        acc[...] += _nt(_bf(dy_ref[...]), w_ref[...])

        @pl.when(j == G - 1)
        def _():
            o_ref[...] = acc[...]

    dspec = (pl.BlockSpec((tm, n), lambda i, j: (i, j)) if natural
             else pl.BlockSpec((None, tm, n), lambda i, j: (j, i, 0)))
    (out,), rider = _call(
        body, name, (T // tm, G), [dspec, pl.BlockSpec((None, K, n), lambda i, j: (j, 0, 0))],
        [pl.BlockSpec((tm, K), lambda i, j: (i, 0))], [jax.ShapeDtypeStruct((T, K), F32)],
        [pltpu.VMEM((tm, K), F32)], ("parallel", "arbitrary"), (dy, wg), ex)
    return (out, rider) if ex is not None else out


def mm_tn(xm, gm, name, x_natural, g_natural, kx, n):
    if x_natural:
        T, Gx = xm.shape[0], xm.shape[1] // kx
    else:
        Gx, T = xm.shape[0], xm.shape[1]
    tt = _rt(T, MM_ROWS)
    if x_natural:
        xspec = pl.BlockSpec((tt, kx), lambda g, k, t: (t, k))
    else:
        xspec = pl.BlockSpec((None, tt, kx), lambda g, k, t: (k, t, 0))
    if g_natural:
        Gg = gm.shape[1] // n
        gspec = pl.BlockSpec((tt, n), lambda g, k, t: (t, g))
    else:
        Gg = gm.shape[0]
        gspec = pl.BlockSpec((None, tt, n), lambda g, k, t: (g, t, 0))

    NT = T // tt

    def body(x_ref, g_ref, o_ref, acc):
        t = pl.program_id(2)

        @pl.when(t == 0)
        def _():
            acc[...] = jnp.zeros_like(acc)
        acc[...] += _tn(_bf(x_ref[...]), _bf(g_ref[...]))

        @pl.when(t == NT - 1)
        def _():
            o_ref[...] = acc[...].astype(o_ref.dtype)

    return pl.pallas_call(
        body, name=name, grid=(Gg, Gx, NT), in_specs=[xspec, gspec],
        out_specs=pl.BlockSpec((None, kx, n), lambda g, k, t: (g, k, 0)),
        out_shape=jax.ShapeDtypeStruct((Gg, Gx * kx, n), BF16), scratch_shapes=[pltpu.VMEM((kx, n), F32)],
        compiler_params=_cp("parallel", "parallel", "arbitrary"))(xm, gm)


def _split3(x):
    h = _bf(x)
    r = x - h.astype(F32)
    m = _bf(r)
    lo = _bf(r - m.astype(F32))
    return h, m, lo


def _tri_mm(tri, x):
    h, m, lo = _split3(x)
    return _nn(tri, lo) + _nn(tri, m) + _nn(tri, h)


def _hgrn_gates(qr, fz, lb):
    sq = _sig(qr)
    q = qr * sq
    sig = _sig(fz)
    f = lb + (1.0 - lb) * sig
    k = 1.0 - f
    return sq, q, sig, f, k, jnp.log(f)


def hgrn_fwd(proj, lb, gg, name, ex=None):
    T, D4 = proj.shape
    D = D4 // 4
    H = D // LANES
    C = _rt(T, HG_CHUNK)
    SB = min(HG_SUB, C)
    NC = T // C

    def body(q_ref, f_ref, v_ref, g_ref, lb_ref, gg_ref, o_ref, og_ref, st_ref, ST, bex):
        @pl.when(pl.program_id(0) == 0)
        def _():
            ST[...] = jnp.zeros_like(ST)
        r_i = lax.broadcasted_iota(jnp.int32, (C, C), 0)
        c_i = lax.broadcasted_iota(jnp.int32, (C, C), 1)
        low = _bf((r_i >= c_i).astype(F32))
        rs = lax.broadcasted_iota(jnp.int32, (SB, C), 0)
        cs = lax.broadcasted_iota(jnp.int32, (SB, C), 1)
        for hd in range(H):
            sl = slice(hd * LANES, (hd + 1) * LANES)
            _, q, _, _, k, lf = _hgrn_gates(q_ref[:, sl], f_ref[:, sl], lb_ref[:, sl])
            v = v_ref[:, sl]
            b = _tri_mm(low, lf)
            bex[hd] = b - lf
            bend = jnp.sum(lf, axis=0, keepdims=True)
            blocks = []
            for I in range(C // SB):
                p = bex[hd, pl.ds(SB * I, 1), :]
                rows = slice(SB * I, SB * (I + 1))
                qI = _bf(q[rows] * jnp.exp(b[rows] - p))
                kI = _bf(k * jnp.exp(jnp.minimum(p - b, EXP_CLAMP)))
                blocks.append(jnp.where(rs + SB * I >= cs, _nt(qI, kI), 0.0))
            A = jnp.concatenate(blocks, axis=0)
            st0 = ST[hd]
            st_ref[0, hd] = st0
            o = _nn(_bf(A), _bf(v)) + _nt(_bf(q * jnp.exp(b)), _bf(st0))
            ST[hd] = st0 * jnp.exp(bend) + _tn_st(v, k * jnp.exp(bend - b))
            o_ref[:, sl] = o
            r = lax.rsqrt(jnp.mean(o * o, axis=-1, keepdims=True) + EPS)
            gr = g_ref[:, sl]
            og_ref[:, sl] = _bf((o * r) * gg_ref[...] * (gr * _sig(gr)))

    col = lambda j: pl.BlockSpec((C, D), lambda c, j=j: (c, j))
    tile = pl.BlockSpec((C, D), lambda c: (c, 0))
    own, rider = _call(
        body, name, (NC,), [col(0), col(1), col(2), col(3), _row(D), _row(LANES)],
        [tile, tile, pl.BlockSpec((1, H, LANES, LANES), lambda c: (c, 0, 0, 0))],
        [jax.ShapeDtypeStruct((T, D), F32), jax.ShapeDtypeStruct((T, D), BF16),
         jax.ShapeDtypeStruct((NC, H, LANES, LANES), F32)],
        [pltpu.VMEM((H, LANES, LANES), F32), pltpu.VMEM((H, C, LANES), F32)], ("arbitrary",),
        (proj, proj, proj, proj, lb, gg), ex)
    return (*own, rider) if ex is not None else own


def hgrn_bwd(proj, o, dog, st, lb, gg, name, ex=None):
    T, D4 = proj.shape
    D = D4 // 4
    H = D // LANES
    C = _rt(T, HG_CHUNK)
    SB = min(HG_SUB, C)
    NC = T // C

    def body(q_ref, f_ref, v_ref, g_ref, o_ref, dog_ref, st_ref, lb_ref, gg_ref, dp_ref, dlb_ref, dgg_ref,
             DST, bex):
        @pl.when(pl.program_id(0) == 0)
        def _():
            DST[...] = jnp.zeros_like(DST)
            dlb_ref[...] = jnp.zeros_like(dlb_ref)
            dgg_ref[...] = jnp.zeros_like(dgg_ref)
        r_i = lax.broadcasted_iota(jnp.int32, (C, C), 0)
        c_i = lax.broadcasted_iota(jnp.int32, (C, C), 1)
        causal = r_i >= c_i
        low = _bf(causal.astype(F32))
        upp = _bf((r_i <= c_i).astype(F32))
        rs = lax.broadcasted_iota(jnp.int32, (SB, C), 0)
        cs = lax.broadcasted_iota(jnp.int32, (SB, C), 1)
        ggv = gg_ref[...]
        for hd in range(H):
            sl = slice(hd * LANES, (hd + 1) * LANES)
            qr = q_ref[:, sl]
            lbv = lb_ref[:, sl]
            sq, q, sig, f, k, lf = _hgrn_gates(qr, f_ref[:, sl], lbv)
            v = v_ref[:, sl]
            oh = o_ref[:, sl]
            r = lax.rsqrt(jnp.mean(oh * oh, axis=-1, keepdims=True) + EPS)
            n = oh * r
            gr = g_ref[:, sl]
            sg = _sig(gr)
            dgo = dog_ref[:, sl]
            dgr = dgo * (n * ggv) * (sg * (1.0 + gr * (1.0 - sg)))
            don = dgo * (gr * sg)
            dgg_ref[...] += jnp.sum(don * n, axis=0, keepdims=True)
            dn = don * ggv
            do = r * (dn - n * jnp.mean(dn * n, axis=-1, keepdims=True))
            b = _tri_mm(low, lf)
            bex[hd] = b - lf
            bend = jnp.sum(lf, axis=0, keepdims=True)
            dob = _bf(do)
            dA = jnp.where(causal, _nt_hi(do, v), 0.0)
            blocks, dqs = [], []
            dk = jnp.zeros((C, LANES), F32)
            for I in range(C // SB):
                p = bex[hd, pl.ds(SB * I, 1), :]
                rows = slice(SB * I, SB * (I + 1))
                eq = jnp.exp(b[rows] - p)
                qI = q[rows] * eq
                ek = jnp.exp(jnp.minimum(p - b, EXP_CLAMP))
                kI = k * ek
                blocks.append(jnp.where(rs + SB * I >= cs, _nt(_bf(qI), _bf(kI)), 0.0))
                dqs.append(_nn_hi(dA[rows], kI) * eq)
                dk = dk + _tn_hi(dA[rows], qI) * ek
            A = jnp.concatenate(blocks, axis=0)
            dst = DST[hd]
            st0 = st_ref[0, hd]
            eb = jnp.exp(b)
            ee = jnp.exp(bend - b)
            ebend = jnp.exp(bend)
            dv = _tn(_bf(A), dob) + _nt(_bf(k * ee), _bf(dst))
            dk_state = ee * _nn_st(v, dst)
            dq = jnp.concatenate(dqs, axis=0) + eb * _nn_st(do, st0)
            dk = dk + dk_state
            DST[hd] = dst * ebend + _tn_st(do, q * eb)
            later = jnp.sum(k * dk_state, axis=0, keepdims=True) + ebend * jnp.sum(st0 * dst, axis=0, keepdims=True)
            dlf = _tri_mm(upp, q * dq - k * dk) + later
            df = dlf / f - dk
            dlb_ref[:, sl] += jnp.sum(df * (1.0 - sig), axis=0, keepdims=True)
            dp_ref[:, sl] = _bf(dq * (sq * (1.0 + qr * (1.0 - sq))))
            dp_ref[:, D + hd * LANES:D + (hd + 1) * LANES] = _bf(df * (1.0 - lbv) * (sig * (1.0 - sig)))
            dp_ref[:, 2 * D + hd * LANES:2 * D + (hd + 1) * LANES] = _bf(dv)
            dp_ref[:, 3 * D + hd * LANES:3 * D + (hd + 1) * LANES] = _bf(dgr)

    col = lambda j: pl.BlockSpec((C, D), lambda c, j=j: (NC - 1 - c, j))
    tile = pl.BlockSpec((C, D), lambda c: (NC - 1 - c, 0))
    own, rider = _call(
        body, name, (NC,),
        [col(0), col(1), col(2), col(3), tile, tile,
         pl.BlockSpec((1, H, LANES, LANES), lambda c: (NC - 1 - c, 0, 0, 0)), _row(D), _row(LANES)],
        [pl.BlockSpec((C, D4), lambda c: (NC - 1 - c, 0)), _row(D), _row(LANES)],
        [jax.ShapeDtypeStruct((T, D4), BF16), jax.ShapeDtypeStruct((1, D), F32), jax.ShapeDtypeStruct((1, LANES), F32)],
        [pltpu.VMEM((H, LANES, LANES), F32), pltpu.VMEM((H, C, LANES), F32)],
        ("arbitrary",), (proj, proj, proj, proj, o, dog, st, lb, gg), ex)
    return (*own, rider) if ex is not None else own


def _prev_blk(tm, hb):
    return lambda i: (jnp.maximum(i * (tm // hb) - 1, 0), 0)


def _next_blk(tm, hb, T):
    return lambda i: (jnp.minimum((i + 1) * (tm // hb), T // hb - 1), 0)


def _glu_to_ext(uc_ref, ucp_ref, ext, D, first):
    def glu(u):
        return u[:, :D] * _sig(u[:, D:])
    gh = jnp.where(first, 0.0, glu(ucp_ref[...]))
    gm = glu(uc_ref[...])
    for c in range(D // LANES):
        ext[c, 0:CONV_HALO, :] = gh[:, c * LANES:(c + 1) * LANES]
        ext[c, CONV_HALO:, :] = gm[:, c * LANES:(c + 1) * LANES]


def conv_mid_fwd(uc, cw, cb, lg, lbias, name):
    T, D2 = uc.shape
    D = D2 // 2
    NCH = D // LANES
    W = 31
    tm = _rt(T, 256)
    RS = min(CONV_ROWS, tm)

    def body(uc_ref, ucp_ref, cw_ref, cb_ref, lg_ref, lb_ref, cv_ref, va_ref, ext, outs):
        _glu_to_ext(uc_ref, ucp_ref, ext, D, pl.program_id(0) == 0)

        def chunk(c, carry):
            for rb in range(tm // RS):
                acc = jnp.zeros((RS, LANES), F32)
                for kk in range(W):
                    acc = acc + cw_ref[c, pl.ds(kk, 1), :] * ext[c, pl.ds(rb * RS + CONV_HALO - (W - 1) + kk, RS), :]
                outs[c, pl.ds(rb * RS, RS), :] = acc
            return carry
        lax.fori_loop(0, NCH, chunk, 0)
        cv = jnp.concatenate([outs[c] for c in range(NCH)], axis=1) + cb_ref[...]
        cv_ref[...] = cv
        mu = jnp.mean(cv, axis=-1, keepdims=True)
        xc = cv - mu
        rstd = lax.rsqrt(jnp.mean(xc * xc, axis=-1, keepdims=True) + EPS)
        l = xc * rstd * lg_ref[...] + lb_ref[...]
        va_ref[...] = _bf(l * _sig(l))

    tile = pl.BlockSpec((tm, D), lambda i: (i, 0))
    return pl.pallas_call(
        body, name=name, grid=(T // tm,),
        in_specs=[pl.BlockSpec((tm, D2), lambda i: (i, 0)), pl.BlockSpec((CONV_HALO, D2), _prev_blk(tm, CONV_HALO)),
                  pl.BlockSpec((NCH, 32, LANES), lambda i: (0, 0, 0)), _row(D), _row(D), _row(D)],
        out_specs=[tile, tile],
        out_shape=[jax.ShapeDtypeStruct((T, D), F32), jax.ShapeDtypeStruct((T, D), BF16)],
        scratch_shapes=[pltpu.VMEM((NCH, tm + CONV_HALO, LANES), F32), pltpu.VMEM((NCH, tm, LANES), F32)],
        compiler_params=_cp("parallel"))(uc, uc, cw, cb, lg, lbias)


def conv_bwd_a(dva, cv, uc, lg, lbias, name):
    T, D2 = uc.shape
    D = D2 // 2
    NCH = D // LANES
    W = 31
    tm = _rt(T, 256)
    RS = min(CONV_ROWS, tm)

    def body(dva_ref, cv_ref, uc_ref, ucp_ref, lg_ref, lb_ref, dcv_ref, dlg_ref, dlb_ref, taps_ref, ext, dcs):
        @pl.when(pl.program_id(0) == 0)
        def _():
            dlg_ref[...] = jnp.zeros_like(dlg_ref)
            dlb_ref[...] = jnp.zeros_like(dlb_ref)
            taps_ref[...] = jnp.zeros_like(taps_ref)
        _glu_to_ext(uc_ref, ucp_ref, ext, D, pl.program_id(0) == 0)
        cv = cv_ref[...]
        mu = jnp.mean(cv, axis=-1, keepdims=True)
        xc = cv - mu
        rstd = lax.rsqrt(jnp.mean(xc * xc, axis=-1, keepdims=True) + EPS)
        yn = xc * rstd
        l = yn * lg_ref[...] + lb_ref[...]
        s = _sig(l)
        dl = dva_ref[...] * (s * (1.0 + l * (1.0 - s)))
        dlg_ref[...] += jnp.sum(dl * yn, axis=0, keepdims=True)
        dlb_ref[...] += jnp.sum(dl, axis=0, keepdims=True)
        dyn = dl * lg_ref[...]
        dcv = rstd * (dyn - jnp.mean(dyn, axis=-1, keepdims=True) - yn * jnp.mean(dyn * yn, axis=-1, keepdims=True))
        dcv_ref[...] = dcv
        for c in range(NCH):
            dcs[c] = dcv[:, c * LANES:(c + 1) * LANES]

        def fold(p):
            return p.reshape(RS // 8, 8, LANES).sum(axis=0)

        def chunk(c, carry):
            for kk in range(W):
                tot = jnp.zeros((8, LANES), F32)
                for rb in range(tm // RS):
                    tot = tot + fold(dcs[c, pl.ds(rb * RS, RS), :]
                                     * ext[c, pl.ds(rb * RS + CONV_HALO - (W - 1) + kk, RS), :])
                taps_ref[c, pl.ds(8 * kk, 8), :] += tot
            tot = jnp.zeros((8, LANES), F32)
            for rb in range(tm // RS):
                tot = tot + fold(dcs[c, pl.ds(rb * RS, RS), :])
            taps_ref[c, pl.ds(8 * W, 8), :] += tot
            return carry
        lax.fori_loop(0, NCH, chunk, 0)

    tile = pl.BlockSpec((tm, D), lambda i: (i, 0))
    vec = jax.ShapeDtypeStruct((1, D), F32)
    return pl.pallas_call(
        body, name=name, grid=(T // tm,),
        in_specs=[tile, tile, pl.BlockSpec((tm, D2), lambda i: (i, 0)),
                  pl.BlockSpec((CONV_HALO, D2), _prev_blk(tm, CONV_HALO)), _row(D), _row(D)],
        out_specs=[tile, _row(D), _row(D), pl.BlockSpec((NCH, 256, LANES), lambda i: (0, 0, 0))],
        out_shape=[jax.ShapeDtypeStruct((T, D), F32), vec, vec, jax.ShapeDtypeStruct((NCH, 256, LANES), F32)],
        scratch_shapes=[pltpu.VMEM((NCH, tm + CONV_HALO, LANES), F32), pltpu.VMEM((NCH, tm, LANES), F32)],
        compiler_params=_cp("arbitrary"))(dva, cv, uc, uc, lg, lbias)


def conv_bwd_b(dcv, uc, cw, name):
    T, D2 = uc.shape
    D = D2 // 2
    NCH = D // LANES
    W = 31
    tm = _rt(T, 256)
    RS = min(CONV_ROWS, tm)
    NT = T // tm

    def body(dcv_ref, dcn_ref, uc_ref, cw_ref, du_ref, db_ref, ext, outs):
        i = pl.program_id(0)

        @pl.when(i == 0)
        def _():
            db_ref[...] = jnp.zeros_like(db_ref)
        dm = dcv_ref[...]
        dn = jnp.where(i == NT - 1, 0.0, dcn_ref[...])
        for c in range(NCH):
            ext[c, 0:tm, :] = dm[:, c * LANES:(c + 1) * LANES]
            ext[c, tm:, :] = dn[:, c * LANES:(c + 1) * LANES]

        def chunk(c, carry):
            for rb in range(tm // RS):
                acc = jnp.zeros((RS, LANES), F32)
                for kk in range(W):
                    acc = acc + cw_ref[c, pl.ds(kk, 1), :] * ext[c, pl.ds(rb * RS + (W - 1) - kk, RS), :]
                outs[c, pl.ds(rb * RS, RS), :] = acc
            return carry
        lax.fori_loop(0, NCH, chunk, 0)
        dglu = jnp.concatenate([outs[c] for c in range(NCH)], axis=1)
        u = uc_ref[...]
        a = u[:, :D]
        sg = _sig(u[:, D:])
        da = dglu * sg
        dg = dglu * a * (sg * (1.0 - sg))
        du_ref[:, :D] = _bf(da)
        du_ref[:, D:] = _bf(dg)
        db_ref[:, :D] += jnp.sum(da, axis=0, keepdims=True)
        db_ref[:, D:] += jnp.sum(dg, axis=0, keepdims=True)

    tile = pl.BlockSpec((tm, D), lambda i: (i, 0))
    wide = pl.BlockSpec((tm, D2), lambda i: (i, 0))
    return pl.pallas_call(
        body, name=name, grid=(NT,),
        in_specs=[tile, pl.BlockSpec((CONV_HALO, D), _next_blk(tm, CONV_HALO, T)), wide,
                  pl.BlockSpec((NCH, 32, LANES), lambda i: (0, 0, 0))],
        out_specs=[wide, _row(D2)],
        out_shape=[jax.ShapeDtypeStruct((T, D2), BF16), jax.ShapeDtypeStruct((1, D2), F32)],
        scratch_shapes=[pltpu.VMEM((NCH, tm + CONV_HALO, LANES), F32), pltpu.VMEM((NCH, tm, LANES), F32)],
        compiler_params=_cp("arbitrary"))(dcv, dcv, uc, cw)


def _shift_rows(x, halo, k, back):
    rows, n = x.shape
    x3, h3 = x.reshape(rows // 8, 8, n), halo.reshape(1, 8, n)
    sub = lax.broadcasted_iota(jnp.int32, (1, 8, n), 1)
    if back:
        r = pltpu.roll(jnp.concatenate([h3, x3], axis=0), k, 1)
        y = jnp.where(sub < k, r[:-1], r[1:])
    else:
        r = pltpu.roll(jnp.concatenate([x3, h3], axis=0), 8 - k, 1)
        y = jnp.where(sub < 8 - k, r[:-1], r[1:])
    return y.reshape(rows, n)


def _conv3(u, uh, dw_ref, bias_ref):
    return (dw_ref[pl.ds(0, 1), :] * _shift_rows(u, uh, 2, True) + dw_ref[pl.ds(1, 1), :] * _shift_rows(u, uh, 1, True)
            + dw_ref[pl.ds(2, 1), :] * u + bias_ref[...])


def ffn_fwd(h, wup, dww, dwb, wdn, x, gate, pg, name):
    T, D = h.shape
    fs = wup.shape[2]
    NJ = wup.shape[0] // 2
    tm = _rt(T, 512)

    def body(h_ref, hp_ref, wa_ref, wb_ref, dwa_ref, dwb_ref, ba_ref, bb_ref, wd_ref, x_ref, gt_ref, pg_ref,
             u_ref, uc_ref, y_ref, xo_ref, acc):
        i, j = pl.program_id(0), pl.program_id(1)

        @pl.when(j == 0)
        def _():
            acc[...] = jnp.zeros_like(acc)
        hb = h_ref[...]
        hp = hp_ref[...]

        def half(s, w_ref, dw_ref, b_ref):
            u = _nn(hb, w_ref[...])
            uh = jnp.where(i == 0, 0.0, _nn(hp, w_ref[...]))
            u_ref[s] = u
            uc = _conv3(u, uh, dw_ref, b_ref)
            uc_ref[s] = uc
            return uc
        a = half(0, wa_ref, dwa_ref, ba_ref)
        b = half(1, wb_ref, dwb_ref, bb_ref)
        acc[...] += _nn(_bf(a * _sig(a) * b), wd_ref[...])

        @pl.when(j == NJ - 1)
        def _():
            y = acc[...]
            y_ref[...] = y
            r = lax.rsqrt(jnp.mean(y * y, axis=-1, keepdims=True) + EPS)
            xo_ref[...] = x_ref[...] + gt_ref[...] * ((y * r) * pg_ref[...])

    tile = pl.BlockSpec((tm, D), lambda i, j: (i, 0))
    shard = lambda off, r: pl.BlockSpec((None, r, fs), lambda i, j: (j + off, 0, 0))
    ush = pl.BlockSpec((2, None, tm, fs), lambda i, j: (0, j, i, 0))
    vec = pl.BlockSpec((1, D), lambda i, j: (0, 0))
    return pl.pallas_call(
        body, name=name, grid=(T // tm, NJ),
        in_specs=[tile, pl.BlockSpec((FFN_HALO, D), lambda i, j: (jnp.maximum(i * (tm // FFN_HALO) - 1, 0), 0)),
                  shard(0, D), shard(NJ, D), shard(0, 3), shard(NJ, 3), shard(0, 1), shard(NJ, 1),
                  pl.BlockSpec((None, fs, D), lambda i, j: (j, 0, 0)), tile, vec, vec],
        out_specs=[ush, ush, tile, tile],
        out_shape=[jax.ShapeDtypeStruct((2, NJ, T, fs), F32), jax.ShapeDtypeStruct((2, NJ, T, fs), F32),
                   jax.ShapeDtypeStruct((T, D), F32), jax.ShapeDtypeStruct((T, D), F32)],
        scratch_shapes=[pltpu.VMEM((tm, D), F32)],
        compiler_params=_cp("parallel", "arbitrary"))(h, h, wup, wup, dww, dww, dwb, dwb, wdn, x, gate, pg)


def ffn_bwd_a(dy, uc, wdn, name):
    _, NJ, T, fs = uc.shape
    D = dy.shape[1]
    tm = _rt(T, 512)

    NT = T // tm

    def body(dy_ref, uc_ref, wd_ref, duc_ref, dwd_ref, acc):
        i = pl.program_id(1)

        @pl.when(i == 0)
        def _():
            acc[...] = jnp.zeros_like(acc)
        dyb = _bf(dy_ref[...])
        dz = _nt(dyb, wd_ref[...])
        a = uc_ref[0]
        b = uc_ref[1]
        sa = _sig(a)
        asa = a * sa
        acc[...] += _tn(_bf(asa * b), dyb)
        duc_ref[0] = dz * b * (sa + asa * (1.0 - sa))
        duc_ref[1] = dz * asa

        @pl.when(i == NT - 1)
        def _():
            dwd_ref[...] = _bf(acc[...])

    ush = pl.BlockSpec((2, None, tm, fs), lambda j, i: (0, j, i, 0))
    return pl.pallas_call(
        body, name=name, grid=(NJ, NT),
        in_specs=[pl.BlockSpec((tm, D), lambda j, i: (i, 0)), ush, pl.BlockSpec((None, fs, D), lambda j, i: (j, 0, 0))],
        out_specs=[ush, pl.BlockSpec((fs, D), lambda j, i: (j, 0))],
        out_shape=[jax.ShapeDtypeStruct((2, NJ, T, fs), F32), jax.ShapeDtypeStruct((NJ * fs, D), BF16)],
        scratch_shapes=[pltpu.VMEM((fs, D), F32)],
        compiler_params=_cp("parallel", "arbitrary"))(dy, uc, wdn)


def ffn_bwd_b(duc, u, h, dww, wup, name):
    NS, T, fs = duc.shape
    D = wup.shape[1]
    tm = _rt(T, 512)
    NT = T // tm

    def body(d_ref, dn_ref, u_ref, h_ref, dw_ref, w_ref, dh_ref, g_ref, dwup_ref, acc, accw, stage):
        i, s = pl.program_id(0), pl.program_id(1)

        @pl.when((i == 0) & (s == 0))
        def _():
            g_ref[...] = jnp.zeros_like(g_ref)

        @pl.when(s == 0)
        def _():
            acc[...] = jnp.zeros_like(acc)
        d = d_ref[...]
        dn = jnp.where(i == NT - 1, 0.0, dn_ref[...])
        d1 = _shift_rows(d, dn, 1, False)
        d2 = _shift_rows(d, dn, 2, False)
        du = _bf(dw_ref[pl.ds(2, 1), :] * d + dw_ref[pl.ds(1, 1), :] * d1 + dw_ref[pl.ds(0, 1), :] * d2)
        acc[...] += _nt(du, w_ref[...])
        dw_part = _tn(h_ref[...], du)

        @pl.when(i == 0)
        def _():
            accw[s] = dw_part

        @pl.when(i > 0)
        def _():
            accw[s] += dw_part

        @pl.when(i == NT - 1)
        def _():
            stage[...] = _bf(accw[s])
            pltpu.sync_copy(stage, dwup_ref.at[s])
        u = u_ref[...]
        g_ref[s, pl.ds(0, 1), :] += jnp.sum(d2 * u, axis=0, keepdims=True)
        g_ref[s, pl.ds(1, 1), :] += jnp.sum(d1 * u, axis=0, keepdims=True)
        g_ref[s, pl.ds(2, 1), :] += jnp.sum(d * u, axis=0, keepdims=True)
        g_ref[s, pl.ds(3, 1), :] += jnp.sum(d, axis=0, keepdims=True)

        @pl.when(s == NS - 1)
        def _():
            dh_ref[...] = acc[...]

    ush = pl.BlockSpec((None, tm, fs), lambda i, s: (s, i, 0))
    unext = pl.BlockSpec((None, FFN_HALO, fs),
                         lambda i, s: (s, jnp.minimum((i + 1) * (tm // FFN_HALO), T // FFN_HALO - 1), 0))
    tile = pl.BlockSpec((tm, D), lambda i, s: (i, 0))
    return pl.pallas_call(
        body, name=name, grid=(NT, NS),
        in_specs=[ush, unext, ush, tile, pl.BlockSpec((None, 3, fs), lambda i, s: (s, 0, 0)),
                  pl.BlockSpec((None, D, fs), lambda i, s: (s, 0, 0))],
        out_specs=[tile, pl.BlockSpec((NS, 8, fs), lambda i, s: (0, 0, 0)), ANY],
        out_shape=[jax.ShapeDtypeStruct((T, D), F32), jax.ShapeDtypeStruct((NS, 8, fs), F32),
                   jax.ShapeDtypeStruct((NS, D, fs), BF16)],
        scratch_shapes=[pltpu.VMEM((tm, D), F32), pltpu.VMEM((NS, D, fs), F32), pltpu.VMEM((D, fs), BF16)],
        compiler_params=pltpu.CompilerParams(dimension_semantics=("arbitrary", "arbitrary"),
                                             vmem_limit_bytes=FFN_BWD_VMEM))(duc, duc, u, h, dww, wup)


def ada_fwd(c_all, w, bias, name):
    L, D, n = w.shape

    def body(c_ref, w_ref, b_ref, o_ref):
        cv = c_ref[...]
        o_ref[...] = _nn(_bf(cv * _sig(cv)), _bf(w_ref[...])) + b_ref[...]

    return pl.pallas_call(
        body, name=name, grid=(L,),
        in_specs=[pl.BlockSpec((N_DEV, D), lambda l: (0, 0)), pl.BlockSpec((None, D, n), lambda l: (l, 0, 0)),
                  pl.BlockSpec((None, 1, n), lambda l: (l, 0, 0))],
        out_specs=pl.BlockSpec((None, N_DEV, n), lambda l: (l, 0, 0)),
        out_shape=jax.ShapeDtypeStruct((L, N_DEV, n), F32), compiler_params=_cp("parallel"))(c_all, w, bias)


def ada_bwd(c_all, dm, name):
    L, _, n = dm.shape
    D = c_all.shape[1]

    def body(c_ref, d_ref, o_ref):
        cv = c_ref[...]
        o_ref[...] = _tn(_bf(cv * _sig(cv)), _bf(d_ref[...]))

    return pl.pallas_call(
        body, name=name, grid=(L,),
        in_specs=[pl.BlockSpec((N_DEV, D), lambda l: (0, 0)), pl.BlockSpec((None, N_DEV, n), lambda l: (l, 0, 0))],
        out_specs=pl.BlockSpec((None, D, n), lambda l: (l, 0, 0)),
        out_shape=jax.ShapeDtypeStruct((L, D, n), F32), compiler_params=_cp("parallel"))(c_all, dm)


def sum_slots(g, name):
    S, R, C = g.shape

    def body(g_ref, o_ref):
        acc = g_ref[0]
        for s in range(1, S):
            acc = acc + g_ref[s]
        o_ref[...] = acc

    return pl.pallas_call(
        body, name=name, grid=(1,), in_specs=[pl.BlockSpec((S, R, C), lambda i: (0, 0, 0))],
        out_specs=pl.BlockSpec((R, C), lambda i: (0, 0)), out_shape=jax.ShapeDtypeStruct((R, C), F32),
        compiler_params=_cp("arbitrary"))(g)


def lb_softmax(logits, name):
    def body(l_ref, s_ref):
        l = l_ref[...]
        e = jnp.exp(l - jnp.max(l, axis=0, keepdims=True))
        s_ref[...] = e / jnp.sum(e, axis=0, keepdims=True)
    return pl.pallas_call(body, name=name, out_shape=jax.ShapeDtypeStruct(logits.shape, F32))(logits)


def adamw(w, gs, m, v, name):
    S, R, C = gs.shape
    tr = R
    budget = (4 * 1024 * 1024) // (4 * (S + 7) * C)
    if R > budget:
        tr = max(t for t in range(16, budget + 1, 16) if R % t == 0)

    def body(w_ref, g_ref, m_ref, v_ref, go_ref, d_ref, mo_ref, vo_ref):
        g = g_ref[0].astype(F32)
        for s in range(1, S):
            g = g + g_ref[s].astype(F32)
        go_ref[...] = g
        mn = ADAM_B1 * m_ref[...] + (1.0 - ADAM_B1) * g
        vn = ADAM_B2 * v_ref[...] + (1.0 - ADAM_B2) * (g * g)
        mo_ref[...] = mn
        vo_ref[...] = vn
        m_hat = mn / (1.0 - ADAM_B1 ** ADAM_STEP)
        v_hat = vn / (1.0 - ADAM_B2 ** ADAM_STEP)
        d_ref[...] = -ADAM_LR * (m_hat / (jnp.sqrt(v_hat) + ADAM_EPS) + ADAM_WD * w_ref[...])

    tile = pl.BlockSpec((tr, C), lambda i: (i, 0))
    out = jax.ShapeDtypeStruct((R, C), F32)
    return pl.pallas_call(
        body, name=name, grid=(R // tr,), in_specs=[tile, pl.BlockSpec((S, tr, C), lambda i: (0, i, 0)), tile, tile],
        out_specs=[tile] * 4, out_shape=[out] * 4, compiler_params=_cp("parallel"))(w, gs, m, v)


def _place():
    return lax.axis_index("x"), lax.axis_index("y"), lax.axis_index("c")


def _slot(p):
    return 4 * p[0] + 2 * p[1] + p[2]


class _Gather:
    def __init__(self, xs):
        self.ins = list(xs)
        n = self.n = len(xs)
        self.out_shapes = [jax.ShapeDtypeStruct((N_DEV,) + a.shape, a.dtype) for a in xs]
        self.scratch = [pltpu.SemaphoreType.DMA((n, 7)), pltpu.SemaphoreType.DMA((n, 7)), pltpu.SemaphoreType.DMA((n,))]

    def _parts(self, ins, outs, sems):
        send_sems, recv_sems, local_sems = sems
        x, y, c = _place()
        me, sib = (x, y, c), (x, y, 1 - c)
        chips = [(1 - x, y), (x, 1 - y), (1 - x, 1 - y)]

        def copy(a, k, block, to, src=None):
            dst = outs[a].at[_slot(block)]
            return pltpu.make_async_remote_copy(
                src_ref=dst if src is None else src, dst_ref=dst, send_sem=send_sems.at[a, k],
                recv_sem=recv_sems.at[a, k], device_id=to, device_id_type=MESH)

        mine = [pltpu.make_async_copy(ins[a], outs[a].at[_slot(me)], local_sems.at[a]) for a in range(self.n)]
        first = []
        for a in range(self.n):
            first.append(copy(a, 0, me, sib, src=ins[a]))
            first += [copy(a, 1 + j, me, (*chip, c), src=ins[a]) for j, chip in enumerate(chips)]
        return copy, mine, first, me, sib, chips, c

    def start(self, ins, outs, sems):
        _, mine, first, *_ = self._parts(ins, outs, sems)
        for cp in mine + first:
            cp.start()

    def finish(self, ins, outs, sems):
        copy, mine, first, me, sib, chips, c = self._parts(ins, outs, sems)
        passed = []
        for j, chip in enumerate(chips):
            for a in range(self.n):
                copy(a, 1 + j, (*chip, c), me).wait_recv()
                fwd = copy(a, 4 + j, (*chip, c), sib)
                fwd.start()
                passed.append(fwd)
        for a in range(self.n):
            copy(a, 0, sib, me).wait_recv()
            for j, chip in enumerate(chips):
                copy(a, 4 + j, (*chip, 1 - c), me).wait_recv()
        for cp in first + passed:
            cp.wait_send()
        for cp in mine:
            cp.wait()


def _run_alone(ex, name):
    def body(*refs):
        ni, no = len(ex.ins), len(ex.out_shapes)
        parts = refs[:ni], refs[ni:ni + no], refs[ni + no:]
        ex.start(*parts)
        ex.finish(*parts)
    return pl.pallas_call(body, name=name, in_specs=[ANY] * len(ex.ins), out_specs=[ANY] * len(ex.out_shapes),
                          out_shape=ex.out_shapes, scratch_shapes=ex.scratch)(*ex.ins)


def _ride(body, n_in, n_out, ex, first, last):
    ni, no = len(ex.ins), len(ex.out_shapes)

    def wrapped(*refs):
        a, r_in = refs[:n_in], refs[n_in:n_in + ni]
        b, r_out = refs[n_in + ni:n_in + ni + n_out], refs[n_in + ni + n_out:n_in + ni + n_out + no]
        rest = refs[n_in + ni + n_out + no:]
        s, r_s = rest[:len(rest) - len(ex.scratch)], rest[len(rest) - len(ex.scratch):]

        @pl.when(first())
        def _():
            ex.start(r_in, r_out, r_s)
        body(*a, *b, *s)

        @pl.when(last())
        def _():
            ex.finish(r_in, r_out, r_s)
    return wrapped


def all_gather(xs, name):
    return _run_alone(_Gather(xs), name)


class _Exchange:
    def __init__(self, groups):
        self.ins = [a for grp in groups for a in grp]
        self.where = [(g, i) for g, grp in enumerate(groups) for i in range(len(grp))]
        n = self.n = len(self.ins)
        self.out_shapes = [jax.ShapeDtypeStruct((N_DEV, len(grp)) + grp[0].shape[1:], grp[0].dtype) for grp in groups]
        self.scratch = [pltpu.SemaphoreType.DMA((n, 7)), pltpu.SemaphoreType.DMA((n, 7)), pltpu.SemaphoreType.DMA((n,))]

    def _parts(self, ins, outs, sems):
        send_sems, recv_sems, local_sems = sems
        x, y, c = _place()
        me = (x, y, c)

        def peer(r):
            return (1 - x if r & 4 else x, 1 - y if r & 2 else y, 1 - c if r & 1 else c)

        def land(a, src):
            g, i = self.where[a]
            return outs[g].at[_slot(src), i]

        def copy(a, r, src_dev):
            p = peer(r)
            return pltpu.make_async_remote_copy(
                src_ref=ins[a].at[_slot(p)], dst_ref=land(a, src_dev), send_sem=send_sems.at[a, r - 1],
                recv_sem=recv_sems.at[a, r - 1], device_id=p, device_id_type=MESH)

        mine = [pltpu.make_async_copy(ins[a].at[_slot(me)], land(a, me), local_sems.at[a]) for a in range(self.n)]
        sends = [copy(a, r, me) for r in range(1, N_DEV) for a in range(self.n)]
        arrivals = [copy(a, r, peer(r)) for r in range(1, N_DEV) for a in range(self.n)]
        return mine, sends, arrivals

    def start(self, ins, outs, sems):
        mine, sends, _ = self._parts(ins, outs, sems)
        for cp in mine + sends:
            cp.start()

    def finish(self, ins, outs, sems):
        mine, sends, arrivals = self._parts(ins, outs, sems)
        for cp in arrivals:
            cp.wait_recv()
        for cp in sends:
            cp.wait_send()
        for cp in mine:
            cp.wait()


def all_to_all(groups, name):
    return _run_alone(_Exchange(groups), name)


def _mods(mod, l, D):
    return [mod[l:l + 1, k * D:(k + 1) * D] for k in range(6)]


def _ffn_backward(l, dxo, hb, u, uc, yf, xin, g2, sc2, gains, W):
    _, NJ, T, fs = u.shape
    D = xin.shape[1]
    dy, dgate, dpg, _ = postnorm_bwd(dxo, yf, g2, gains["post_ffn_g"][l:l + 1], f"ffn{l}_postnorm_bwd")
    duc, d_wdn = ffn_bwd_a(dy, uc, W["wdn"][l], f"ffn{l}_bwd_a")
    dh, taps, d_wup = ffn_bwd_b(duc.reshape(2 * NJ, T, fs), u.reshape(2 * NJ, T, fs), hb, W["fdw_w"][l],
                                W["wup"][l], f"ffn{l}_bwd_b")
    dx, da, dsh = prenorm_bwd(dh, xin, dxo, gains["pre_ffn_g"][l:l + 1], sc2, f"ffn{l}_prenorm_bwd")
    small = dict(dgate=dgate, dpost=dpg, da=da, dsh=dsh, dwb=taps[:, 3], dww=taps[:, 0:3])
    return dx, d_wup, d_wdn, small


def _local_step(xt, tgt, mod, gains, W, mine):
    T, D = xt.shape
    zero_d = jnp.zeros((1, D), F32)
    half = lambda g: g.reshape(N_DEV // 2, 2 * g.shape[1], D)
    sh1, sc1, g1, sh2, sc2, g2 = m0 = _mods(mod, 0, D)
    h0 = prenorm_mod(xt, gains["pre_mix_g"][0:1], sc1, sh1, "l0_prenorm")
    proj, (whout, wup0, wdn0) = mm_nn(h0, W["whin"], jnp.zeros((1, 4 * D), F32), "hgrn_proj",
                                      _Gather([mine["whout"], mine["wup"][0], mine["wdn"][0]]))
    o, og, st, (wcin, wcout, wup1, wdn1) = hgrn_fwd(
        proj, W["lb"], W["gg"], "hgrn_fwd", _Gather([mine["wcin"], mine["wcout"], mine["wup"][1], mine["wdn"][1]]))
    W = dict(W, whout=whout.reshape(D, D), wcin=wcin, wcout=wcout.reshape(D, D), wup=[wup0, wup1],
             wdn=[half(wdn0), half(wdn1)])
    y0, x1 = mm_post(og, W["whout"], zero_d, xt, g1, gains["post_mix_g"][0:1], "hgrn_out")
    h0f = prenorm_mod(x1, gains["pre_ffn_g"][0:1], sc2, sh2, "f0_prenorm")
    u0, uc0, yf0, x2 = ffn_fwd(h0f, W["wup"][0], W["fdw_w"][0], W["fdw_b"][0], W["wdn"][0], x1, g2,
                                gains["post_ffn_g"][0:1], "ffn0_fwd")
    t1, c1, e1, t2, c2, e2 = m1 = _mods(mod, 1, D)
    h1 = prenorm_mod(x2, gains["pre_mix_g"][1:2], c1, t1, "l1_prenorm")
    uc = mm_nn(h1, W["wcin"], W["cb_in"], "conv_in")
    cv, va = conv_mid_fwd(uc, W["cw"], W["cdw_b"], W["ln_g"], W["ln_b"], "conv_mid_fwd")
    y1, x3 = mm_post(va, W["wcout"], W["cb_out"], x2, e1, gains["post_mix_g"][1:2], "conv_out")
    h1f = prenorm_mod(x3, gains["pre_ffn_g"][1:2], c2, t2, "f1_prenorm")
    u1, uc1, yf1, x4 = ffn_fwd(h1f, W["wup"][1], W["fdw_w"][1], W["fdw_b"][1], W["wdn"][1], x3, e2,
                                gains["post_ffn_g"][1:2], "ffn1_fwd")
    dx4, loss_part = loss_grad(x4, tgt, "loss_grad")
    dx3, d_wup1, d_wdn1, sf1 = _ffn_backward(1, dx4, h1f, u1, uc1, yf1, x3, e2, c2, gains, W)
    dy1, dgate_c, dpost_c, dys_c = postnorm_bwd(dx3, y1, e1, gains["post_mix_g"][1:2], "conv_postnorm_bwd")
    dva = mm_nt(dy1, W["wcout"][None], "conv_dva", True)
    d_wcout = mm_tn(va, dy1, "conv_dwout", True, True, D, D)
    dcv, dlg, dlbias, taps = conv_bwd_a(dva, cv, uc, W["ln_g"], W["ln_b"], "conv_bwd_a")
    duc, dbin = conv_bwd_b(dcv, uc, W["cw"], "conv_bwd_b")
    dh1 = mm_nt(duc, W["wcin"], "conv_dh", True)
    d_wcin = mm_tn(h1, duc, "conv_dwin", True, True, D, W["wcin"].shape[2])
    dx2, da_c, dsh_c = prenorm_bwd(dh1, x2, dx3, gains["pre_mix_g"][1:2], c1, "l1_prenorm_bwd")
    dx1, d_wup0, d_wdn0, sf0 = _ffn_backward(0, dx2, h0f, u0, uc0, yf0, x1, g2, sc2, gains, W)
    dy0, dgate_h, dpost_h, _ = postnorm_bwd(dx1, y0, g1, gains["post_mix_g"][0:1], "hgrn_postnorm_bwd")
    dog = mm_nt(dy0, W["whout"][None], "hgrn_dog", True)
    d_whout = mm_tn(og, dy0, "hgrn_dwout", True, True, D, D)
    rows = lambda g: g.reshape(N_DEV, -1, D)
    dproj, dlb, dgg, (r_wcin, r_wcout, r_wup, r_wdn, r_whout) = hgrn_bwd(
        proj, o, dog, st, W["lb"], W["gg"], "hgrn_bwd",
        _Exchange([[d_wcin], [rows(d_wcout)], [d_wup0, d_wup1], [rows(d_wdn0), rows(d_wdn1)], [rows(d_whout)]]))
    d_whin = mm_tn(h0, dproj, "hgrn_dwin", True, True, D, W["whin"].shape[2])
    dh0, (r_whin,) = mm_nt(dproj, W["whin"], "hgrn_dh", True, _Exchange([[d_whin]]))
    dx0, da_h, dsh_h = prenorm_bwd(dh0, xt, dx1, gains["pre_mix_g"][0:1], sc1, "l0_prenorm_bwd")

    big = dict(hgrn_w_in=r_whin, hgrn_w_out=r_whout, conv_w_in=r_wcin, conv_w_out=r_wcout, ffn_w_up=r_wup,
               ffn_w_down=r_wdn)
    pm, pf = gains["pre_mix_g"], gains["pre_ffn_g"]
    dmod = jnp.concatenate([
        dsh_h, da_h * pm[0:1], dgate_h, sf0["dsh"], sf0["da"] * pf[0:1], sf0["dgate"],
        dsh_c, da_c * pm[1:2], dgate_c, sf1["dsh"], sf1["da"] * pf[1:2], sf1["dgate"]], axis=1)
    NCH = D // LANES
    tap = taps.reshape(NCH, 32, 8, LANES).sum(axis=2)
    small = dict(
        dmod=dmod,
        pre_mix_g=jnp.concatenate([da_h * (1.0 + sc1), da_c * (1.0 + c1)], axis=0),
        post_mix_g=jnp.concatenate([dpost_h, dpost_c], axis=0),
        pre_ffn_g=jnp.concatenate([sf0["da"] * (1.0 + sc2), sf1["da"] * (1.0 + c2)], axis=0),
        post_ffn_g=jnp.concatenate([sf0["dpost"], sf1["dpost"]], axis=0),
        lb=dlb, gg=dgg,
        ffn_dw_b=jnp.stack([sf0["dwb"], sf1["dwb"]]),
        conv_b_in=dbin,
        conv_dw_w=jnp.transpose(tap[:, :31], (1, 0, 2)).reshape(31, D),
        conv_dw_b=tap[:, 31].reshape(1, D),
        conv_ln_g=dlg, conv_ln_b=dlbias, conv_b_out=dys_c,
        ffn_dw_w=jnp.stack([sf0["dww"], sf1["dww"]]))
    return loss_part, dx0, big, small


SMALL_ORDER = ["dmod", "pre_mix_g", "post_mix_g", "pre_ffn_g", "post_ffn_g", "lb", "gg", "ffn_dw_b", "conv_b_in",
               "conv_dw_w", "conv_dw_b", "conv_ln_g", "conv_ln_b", "conv_b_out", "ffn_dw_w"]


def _pack(parts):
    flat = jnp.concatenate([p.reshape(-1) for p in parts])
    pad = (-flat.shape[0]) % LANES
    if pad:
        flat = jnp.concatenate([flat, jnp.zeros((pad,), F32)])
    return flat.reshape(-1, LANES)


def _unpack(flat, shapes):
    out, off = [], 0
    for s in shapes:
        n = 1
        for d in s:
            n *= d
        out.append(flat[off:off + n].reshape(s))
        off += n
    return out


def _apply_adamw(name, w, g_slots, m, v):
    shp = w.shape
    C = shp[-1]
    R = w.size // C
    g, d, mn, vn = adamw(w.reshape(R, C), g_slots.reshape(g_slots.shape[0], R, C), m.reshape(R, C), v.reshape(R, C),
                         "adamw_" + name)
    return g.reshape(shp), d.reshape(shp), mn.reshape(shp), vn.reshape(shp)


WEIGHTS = ['ada_w', 'ada_b', 'pre_mix_g', 'post_mix_g', 'pre_ffn_g', 'post_ffn_g', 'hgrn_w_in', 'hgrn_lb_logits',
           'hgrn_gnorm_g', 'hgrn_w_out', 'conv_w_in', 'conv_b_in', 'conv_dw_w', 'conv_dw_b', 'conv_ln_g', 'conv_ln_b',
           'conv_w_out', 'conv_b_out', 'ffn_w_up', 'ffn_dw_w', 'ffn_dw_b', 'ffn_w_down']


def kernel(x, c, ada_w, ada_b, pre_mix_g, post_mix_g, pre_ffn_g, post_ffn_g, hgrn_w_in, hgrn_lb_logits, hgrn_gnorm_g, hgrn_w_out, conv_w_in, conv_b_in, conv_dw_w, conv_dw_b, conv_ln_g, conv_ln_b, conv_w_out, conv_b_out, ffn_w_up, ffn_dw_w, ffn_dw_b, ffn_w_down, loss_target, m_ada_w, m_ada_b, m_pre_mix_g, m_post_mix_g, m_pre_ffn_g, m_post_ffn_g, m_hgrn_w_in, m_hgrn_lb_logits, m_hgrn_gnorm_g, m_hgrn_w_out, m_conv_w_in, m_conv_b_in, m_conv_dw_w, m_conv_dw_b, m_conv_ln_g, m_conv_ln_b, m_conv_w_out, m_conv_b_out, m_ffn_w_up, m_ffn_dw_w, m_ffn_dw_b, m_ffn_w_down, v_ada_w, v_ada_b, v_pre_mix_g, v_post_mix_g, v_pre_ffn_g, v_post_ffn_g, v_hgrn_w_in, v_hgrn_lb_logits, v_hgrn_gnorm_g, v_hgrn_w_out, v_conv_w_in, v_conv_b_in, v_conv_dw_w, v_conv_dw_b, v_conv_ln_g, v_conv_ln_b, v_conv_w_out, v_conv_b_out, v_ffn_w_up, v_ffn_dw_w, v_ffn_dw_b, v_ffn_w_down):
    P = dict(locals())
    _, T, D = x.shape
    me = _slot(_place())
    L = ada_w.shape[0]
    na = ada_w.shape[2]
    fs = ffn_w_up.shape[2]
    nci = conv_w_in.shape[2]
    nd = conv_dw_b.shape[1]
    NCH = D // LANES

    small_in = [c, conv_b_in, conv_dw_w, conv_dw_b, conv_ln_g, conv_ln_b, conv_b_out, ffn_dw_w]
    whin, packed = all_gather([_bf(hgrn_w_in[0]), _pack(small_in)], "gather_first")
    mine = dict(whout=_bf(hgrn_w_out[0]), wcin=_bf(conv_w_in[0]), wcout=_bf(conv_w_out[0]),
                wup=[_bf(ffn_w_up[l]) for l in range(L)], wdn=[_bf(ffn_w_down[l]) for l in range(L)])
    sm = packed.reshape(N_DEV, -1)
    offs = [0]
    for a in small_in:
        offs.append(offs[-1] + a.size)
    piece = lambda k, shp: sm[:, offs[k]:offs[k + 1]].reshape((N_DEV,) + shp)
    c_all = piece(0, (D,))
    dw_nat = jnp.transpose(piece(2, (31, nd)), (1, 0, 2)).reshape(31, D)
    cw = jnp.pad(jnp.transpose(dw_nat.reshape(31, NCH, LANES), (1, 0, 2)), ((0, 0), (0, 1), (0, 0)))
    fdw = piece(7, (L, 3, fs))
    lb_soft = lb_softmax(hgrn_lb_logits, "lb_softmax")
    W = dict(
        whin=whin, lb=lb_soft[0:1], gg=hgrn_gnorm_g, cb_in=piece(1, (nci,)).reshape(1, N_DEV * nci), cw=cw,
        cdw_b=piece(3, (nd,)).reshape(1, D), ln_g=piece(4, (nd,)).reshape(1, D), ln_b=piece(5, (nd,)).reshape(1, D),
        cb_out=piece(6, (nd,)).reshape(1, D), fdw_w=[fdw[:, l] for l in range(L)],
        fdw_b=[ffn_dw_b[l].reshape(N_DEV, 1, fs) for l in range(L)])

    bias_mine = lax.dynamic_slice(ada_b, (0, me * na), (L, na)).reshape(L, 1, na)
    mod_cols = ada_fwd(c_all, ada_w, bias_mine, "ada_fwd")
    (mod_all,) = all_gather([mod_cols], "gather_mod")
    mod = jnp.transpose(lax.dynamic_index_in_dim(mod_all, me, axis=2, keepdims=False), (1, 0, 2)).reshape(L, N_DEV * na)

    gains = dict(pre_mix_g=pre_mix_g, post_mix_g=post_mix_g, pre_ffn_g=pre_ffn_g, post_ffn_g=post_ffn_g)
    loss_part, dx0, big, small = _local_step(x[0], loss_target[0], mod, gains, W, mine)
    loss = lax.psum((0.5 / D) * jnp.sum(loss_part), ("x", "y", "c"))

    parts = [small[k] for k in SMALL_ORDER]
    (sm_all,) = all_gather([_pack(parts)], "gather_small_grads")
    tot = _unpack(sum_slots(sm_all, "sum_small_grads").reshape(-1), [p.shape for p in parts])
    tot = dict(zip(SMALL_ORDER, tot))
    dmod_all = sm_all.reshape(N_DEV, -1)[:, :L * 6 * D].reshape(N_DEV, L, 6 * D)
    dm_mine = jnp.transpose(lax.dynamic_slice(dmod_all, (0, 0, me * na), (N_DEV, L, na)), (1, 0, 2))
    s0 = lb_soft[0:1]
    onehot = (lax.broadcasted_iota(jnp.int32, lb_soft.shape, 0) == 0).astype(F32)
    col = lambda a, n: lax.dynamic_slice_in_dim(a, me * n, n, axis=a.ndim - 1)
    G = {
        'ada_w': ada_bwd(c_all, dm_mine, "ada_bwd")[None],
        'ada_b': tot["dmod"].reshape(1, L, 6 * D),
        'pre_mix_g': tot["pre_mix_g"][None], 'post_mix_g': tot["post_mix_g"][None],
        'pre_ffn_g': tot["pre_ffn_g"][None], 'post_ffn_g': tot["post_ffn_g"][None],
        'hgrn_lb_logits': (tot["lb"] * s0 * (onehot - lb_soft))[None],
        'hgrn_gnorm_g': tot["gg"][None],
        'ffn_dw_b': tot["ffn_dw_b"].reshape(1, L, N_DEV * fs),
        'conv_b_in': col(tot["conv_b_in"], nci)[None],
        'conv_dw_w': col(tot["conv_dw_w"], nd)[None, None],
        'conv_dw_b': col(tot["conv_dw_b"], nd)[None], 'conv_ln_g': col(tot["conv_ln_g"], nd)[None],
        'conv_ln_b': col(tot["conv_ln_b"], nd)[None], 'conv_b_out': col(tot["conv_b_out"], nd)[None],
        'ffn_dw_w': lax.dynamic_index_in_dim(tot["ffn_dw_w"], me, axis=1, keepdims=False)[None],
    }

    G.update(big)
    res = {n: _apply_adamw(n, P[n], G[n], P["m_" + n], P["v_" + n]) for n in WEIGHTS}
    return (loss, dx0[None], *[res[n][0] for n in WEIGHTS], *[res[n][1] for n in WEIGHTS],
            *[res[n][2] for n in WEIGHTS], *[res[n][3] for n in WEIGHTS])
```

```python
import functools

import jax
import jax.numpy as jnp
from jax import lax
from jax.experimental import pallas as pl
from jax.experimental.pallas import tpu as pltpu

F32 = jnp.float32
BF16 = jnp.bfloat16
EPS = 1e-6
LANES = 128
N_DEV = 8
VMEM_LIMIT = 48 * 1024 * 1024
FFN_BWD_VMEM = 58 * 1024 * 1024
MM_ROWS = 1024
HG_CHUNK = 128
HG_SUB = 16
EXP_CLAMP = 80.0
CONV_HALO = 32
FFN_HALO = 8
CONV_ROWS = 32
ADAM_LR, ADAM_B1, ADAM_B2, ADAM_EPS, ADAM_WD, ADAM_STEP = 0.001, 0.9, 0.999, 1e-08, 0.01, 10
MESH = pl.DeviceIdType.MESH
ANY = pl.BlockSpec(memory_space=pl.ANY)


def _cp(*sem):
    return pltpu.CompilerParams(dimension_semantics=sem, vmem_limit_bytes=VMEM_LIMIT)


def _rt(n, t):
    t = min(n, t)
    assert n % t == 0, (n, t)
    return t


def _sig(x):
    return jax.nn.sigmoid(x)


def _nt(a, b):
    return lax.dot_general(a, b, (((1,), (1,)), ((), ())), preferred_element_type=F32)


def _tn(a, b):
    return lax.dot_general(a, b, (((0,), (0,)), ((), ())), preferred_element_type=F32)


def _nn(a, b):
    return jnp.dot(a, b, preferred_element_type=F32)


def _bf(x):
    return x.astype(BF16)


def _nn_st(a, b):
    return _nn(_bf(a), _bf(b))


def _tn_st(a, b):
    return _tn(_bf(a), _bf(b))


def _row(d):
    return pl.BlockSpec((1, d), lambda *_: (0, 0))


def prenorm_mod(x, g, sc, sh, name):
    T, D = x.shape
    tm = _rt(T, 512)

    def body(x_ref, g_ref, sc_ref, sh_ref, h_ref):
        xv = x_ref[...]
        r = lax.rsqrt(jnp.mean(xv * xv, axis=-1, keepdims=True) + EPS)
        h_ref[...] = _bf((xv * r) * g_ref[...] * (1.0 + sc_ref[...]) + sh_ref[...])

    tile = pl.BlockSpec((tm, D), lambda i: (i, 0))
    return pl.pallas_call(
        body, name=name, grid=(T // tm,), in_specs=[tile, _row(D), _row(D), _row(D)], out_specs=tile,
        out_shape=jax.ShapeDtypeStruct((T, D), BF16), compiler_params=_cp("parallel"))(x, g, sc, sh)


def resid_bwd(name, pre=None, loss=None, post=None):
    T, D = (pre[1] if pre is not None else loss[0]).shape
    tm = _rt(T, 512)
    n_src = 5 if pre is not None else 2
    n_in = n_src + (3 if post is not None else 0)

    def body(*refs):
        ins, outs = refs[:n_in], refs[n_in:]
        first = pl.program_id(0) == 0
        if pre is not None:
            dh_ref, x_ref, dr_ref, g_ref, sc_ref = ins[:5]
            dx_ref, da_ref, dsh_ref = outs[:3]

            @pl.when(first)
            def _():
                da_ref[...] = jnp.zeros_like(da_ref)
                dsh_ref[...] = jnp.zeros_like(dsh_ref)
            xv = x_ref[...]
            dh = dh_ref[...]
            r = lax.rsqrt(jnp.mean(xv * xv, axis=-1, keepdims=True) + EPS)
            n = xv * r
            da_ref[...] += jnp.sum(dh * n, axis=0, keepdims=True)
            dsh_ref[...] += jnp.sum(dh, axis=0, keepdims=True)
            dn = dh * (g_ref[...] * (1.0 + sc_ref[...]))
            dx = dr_ref[...] + r * (dn - n * jnp.mean(dn * n, axis=-1, keepdims=True))
            dx_ref[...] = dx
            rest = outs[3:]
        else:
            x_ref, t_ref = ins[:2]
            dx_ref, part_ref = outs[:2]

            @pl.when(first)
            def _():
                part_ref[...] = jnp.zeros_like(part_ref)
            e = x_ref[...] - t_ref[...]
            dx = e * (1.0 / D)
            dx_ref[...] = dx
            e2 = (e * e).reshape(tm // 8, 8, D).sum(axis=0)
            acc = e2[:, 0:LANES]
            for j in range(1, D // LANES):
                acc = acc + e2[:, j * LANES:(j + 1) * LANES]
            part_ref[...] += acc
            rest = outs[2:]
        if post is not None:
            y_ref, gt_ref, pg_ref = ins[n_src:]
            dy_ref, dgate_ref, dpg_ref, dys_ref = rest

            @pl.when(first)
            def _():
                dgate_ref[...] = jnp.zeros_like(dgate_ref)
                dpg_ref[...] = jnp.zeros_like(dpg_ref)
                dys_ref[...] = jnp.zeros_like(dys_ref)
            yv = y_ref[...]
            ry = lax.rsqrt(jnp.mean(yv * yv, axis=-1, keepdims=True) + EPS)
            ny = yv * ry
            s = jnp.sum(dx * ny, axis=0, keepdims=True)
            dgate_ref[...] += s * pg_ref[...]
            dpg_ref[...] += s * gt_ref[...]
            dny = dx * (gt_ref[...] * pg_ref[...])
            dy = ry * (dny - ny * jnp.mean(dny * ny, axis=-1, keepdims=True))
            dy_ref[...] = _bf(dy)
            dys_ref[...] += jnp.sum(dy, axis=0, keepdims=True)

    tile = pl.BlockSpec((tm, D), lambda i: (i, 0))
    vec = jax.ShapeDtypeStruct((1, D), F32)
    big = jax.ShapeDtypeStruct((T, D), F32)
    if pre is not None:
        operands, in_specs = list(pre), [tile, tile, tile, _row(D), _row(D)]
        out_specs, out_shape = [tile, _row(D), _row(D)], [big, vec, vec]
    else:
        operands, in_specs = list(loss), [tile, tile]
        out_specs, out_shape = [tile, pl.BlockSpec((8, LANES), lambda i: (0, 0))], [big, jax.ShapeDtypeStruct((8, LANES), F32)]
    if post is not None:
        operands, in_specs = operands + list(post), in_specs + [tile, _row(D), _row(D)]
        out_specs = out_specs + [tile, _row(D), _row(D), _row(D)]
        out_shape = out_shape + [jax.ShapeDtypeStruct((T, D), BF16), vec, vec, vec]
    return pl.pallas_call(body, name=name, grid=(T // tm,), in_specs=in_specs, out_specs=out_specs, out_shape=out_shape,
                          compiler_params=_cp("arbitrary"))(*operands)


def _call(body, name, grid, in_specs, out_specs, out_shape, scratch, sems, operands, ex=None):
    if ex is not None:
        def first():
            return functools.reduce(lambda p, q: p & q, [pl.program_id(k) == 0 for k in range(len(grid))])

        def last():
            return functools.reduce(lambda p, q: p & q, [pl.program_id(k) == grid[k] - 1 for k in range(len(grid))])
        body = _ride(body, len(in_specs), len(out_specs), ex, first, last)
        in_specs = in_specs + [ANY] * len(ex.ins)
        out_specs = out_specs + [ANY] * len(ex.out_shapes)
        out_shape = out_shape + ex.out_shapes
        scratch = scratch + ex.scratch
        operands = list(operands) + ex.ins
        sems = ("arbitrary",) * len(grid)
    res = pl.pallas_call(body, name=name, grid=grid, in_specs=in_specs, out_specs=out_specs, out_shape=out_shape,
                         scratch_shapes=scratch, compiler_params=_cp(*sems))(*operands)
    n_own = len(res) - (len(ex.out_shapes) if ex is not None else 0)
    return res[:n_own], res[n_own:]


def mm_nn(a, bg, bias, name, ex=None):
    M, K = a.shape
    G, _, n = bg.shape
    tm = _rt(M, MM_ROWS)

    def body(a_ref, b_ref, bias_ref, o_ref):
        o_ref[...] = _nn(a_ref[...], b_ref[...]) + bias_ref[...]

    (out,), rider = _call(
        body, name, (M // tm, G),
        [pl.BlockSpec((tm, K), lambda i, j: (i, 0)), pl.BlockSpec((None, K, n), lambda i, j: (j, 0, 0)),
         pl.BlockSpec((1, n), lambda i, j: (0, j))],
        [pl.BlockSpec((tm, n), lambda i, j: (i, j))], [jax.ShapeDtypeStruct((M, G * n), F32)], [],
        ("parallel", "arbitrary"), (a, bg, bias), ex)
    return (out, rider) if ex is not None else out


def mm_post(a, b, bias, x, gate, pg, name):
    T, K = a.shape
    D = b.shape[1]
    tm = _rt(T, 512)

    def body(a_ref, b_ref, bias_ref, x_ref, gt_ref, pg_ref, y_ref, xo_ref):
        y = _nn(a_ref[...], b_ref[...]) + bias_ref[...]
        y_ref[...] = y
        r = lax.rsqrt(jnp.mean(y * y, axis=-1, keepdims=True) + EPS)
        xo_ref[...] = x_ref[...] + gt_ref[...] * ((y * r) * pg_ref[...])

    tile = pl.BlockSpec((tm, D), lambda i: (i, 0))
    out = jax.ShapeDtypeStruct((T, D), F32)
    return pl.pallas_call(
        body, name=name, grid=(T // tm,),
        in_specs=[pl.BlockSpec((tm, K), lambda i: (i, 0)), pl.BlockSpec((K, D), lambda i: (0, 0)), _row(D), tile,
                  _row(D), _row(D)],
        out_specs=[tile, tile], out_shape=[out, out], compiler_params=_cp("parallel"))(a, b, bias, x, gate, pg)


def mm_nt(dy, wg, name, natural, ex=None):
    G, K, n = wg.shape
    T = dy.shape[0] if natural else dy.shape[1]
    tm = _rt(T, MM_ROWS)

    def body(dy_ref, w_ref, o_ref, acc):
        j = pl.program_id(1)

        @pl.when(j == 0)
        def _():
            acc[...] = jnp.zeros_like(acc)
        acc[...] += _nt(_bf(dy_ref[...]), w_ref[...])

        @pl.when(j == G - 1)
        def _():
            o_ref[...] = acc[...]

    dspec = (pl.BlockSpec((tm, n), lambda i, j: (i, j)) if natural
             else pl.BlockSpec((None, tm, n), lambda i, j: (j, i, 0)))
    (out,), rider = _call(
        body, name, (T // tm, G), [dspec, pl.BlockSpec((None, K, n), lambda i, j: (j, 0, 0))],
        [pl.BlockSpec((tm, K), lambda i, j: (i, 0))], [jax.ShapeDtypeStruct((T, K), F32)],
        [pltpu.VMEM((tm, K), F32)], ("parallel", "arbitrary"), (dy, wg), ex)
    return (out, rider) if ex is not None else out


def mm_tn(xm, gm, name, x_natural, g_natural, kx, n):
    if x_natural:
        T, Gx = xm.shape[0], xm.shape[1] // kx
    else:
        Gx, T = xm.shape[0], xm.shape[1]
    tt = _rt(T, MM_ROWS)
    if x_natural:
        xspec = pl.BlockSpec((tt, kx), lambda g, k, t: (t, k))
    else:
        xspec = pl.BlockSpec((None, tt, kx), lambda g, k, t: (k, t, 0))
    if g_natural:
        Gg = gm.shape[1] // n
        gspec = pl.BlockSpec((tt, n), lambda g, k, t: (t, g))
    else:
        Gg = gm.shape[0]
        gspec = pl.BlockSpec((None, tt, n), lambda g, k, t: (g, t, 0))

    NT = T // tt

    def body(x_ref, g_ref, o_ref, acc):
        t = pl.program_id(2)

        @pl.when(t == 0)
        def _():
            acc[...] = jnp.zeros_like(acc)
        acc[...] += _tn(_bf(x_ref[...]), _bf(g_ref[...]))

        @pl.when(t == NT - 1)
        def _():
            o_ref[...] = acc[...].astype(o_ref.dtype)

    return pl.pallas_call(
        body, name=name, grid=(Gg, Gx, NT), in_specs=[xspec, gspec],
        out_specs=pl.BlockSpec((None, kx, n), lambda g, k, t: (g, k, 0)),
        out_shape=jax.ShapeDtypeStruct((Gg, Gx * kx, n), BF16), scratch_shapes=[pltpu.VMEM((kx, n), F32)],
        compiler_params=_cp("parallel", "parallel", "arbitrary"))(xm, gm)


def _split3(x):
    h = _bf(x)
    r = x - h.astype(F32)
    m = _bf(r)
    lo = _bf(r - m.astype(F32))
    return h, m, lo


def _tri_mm(tri, x):
    h, m, lo = _split3(x)
    return _nn(tri, lo) + _nn(tri, m) + _nn(tri, h)


def _hgrn_gates(qr, fz, lb):
    sq = _sig(qr)
    q = qr * sq
    sig = _sig(fz)
    f = lb + (1.0 - lb) * sig
    k = 1.0 - f
    return sq, q, sig, f, k, jnp.log(f)


def hgrn_fwd(proj, lb, gg, name, ex=None):
    T, D4 = proj.shape
    D = D4 // 4
    H = D // LANES
    C = _rt(T, HG_CHUNK)
    SB = min(HG_SUB, C)
    NC = T // C

    def body(q_ref, f_ref, v_ref, g_ref, lb_ref, gg_ref, o_ref, og_ref, st_ref, ST, bex):
        @pl.when(pl.program_id(0) == 0)
        def _():
            ST[...] = jnp.zeros_like(ST)
        r_i = lax.broadcasted_iota(jnp.int32, (C, C), 0)
        c_i = lax.broadcasted_iota(jnp.int32, (C, C), 1)
        low = _bf((r_i >= c_i).astype(F32))
        rs = lax.broadcasted_iota(jnp.int32, (SB, C), 0)
        cs = lax.broadcasted_iota(jnp.int32, (SB, C), 1)
        for hd in range(H):
            sl = slice(hd * LANES, (hd + 1) * LANES)
            _, q, _, _, k, lf = _hgrn_gates(q_ref[:, sl], f_ref[:, sl], lb_ref[:, sl])
            v = v_ref[:, sl]
            b = _tri_mm(low, lf)
            bex[hd] = b - lf
            bend = jnp.sum(lf, axis=0, keepdims=True)
            blocks = []
            for I in range(C // SB):
                p = bex[hd, pl.ds(SB * I, 1), :]
                rows = slice(SB * I, SB * (I + 1))
                qI = _bf(q[rows] * jnp.exp(b[rows] - p))
                kI = _bf(k * jnp.exp(jnp.minimum(p - b, EXP_CLAMP)))
                blocks.append(jnp.where(rs + SB * I >= cs, _nt(qI, kI), 0.0))
            A = jnp.concatenate(blocks, axis=0)
            st0 = ST[hd]
            st_ref[0, hd] = st0
            o = _nn(_bf(A), _bf(v)) + _nt(_bf(q * jnp.exp(b)), _bf(st0))
            ST[hd] = st0 * jnp.exp(bend) + _tn_st(v, k * jnp.exp(bend - b))
            o_ref[:, sl] = o
            r = lax.rsqrt(jnp.mean(o * o, axis=-1, keepdims=True) + EPS)
            gr = g_ref[:, sl]
            og_ref[:, sl] = _bf((o * r) * gg_ref[...] * (gr * _sig(gr)))

    col = lambda j: pl.BlockSpec((C, D), lambda c, j=j: (c, j))
    tile = pl.BlockSpec((C, D), lambda c: (c, 0))
    own, rider = _call(
        body, name, (NC,), [col(0), col(1), col(2), col(3), _row(D), _row(LANES)],
        [tile, tile, pl.BlockSpec((1, H, LANES, LANES), lambda c: (c, 0, 0, 0))],
        [jax.ShapeDtypeStruct((T, D), F32), jax.ShapeDtypeStruct((T, D), BF16),
         jax.ShapeDtypeStruct((NC, H, LANES, LANES), F32)],
        [pltpu.VMEM((H, LANES, LANES), F32), pltpu.VMEM((H, C, LANES), F32)], ("arbitrary",),
        (proj, proj, proj, proj, lb, gg), ex)
    return (*own, rider) if ex is not None else own


def hgrn_bwd(proj, o, dog, st, lb, gg, name, ex=None):
    T, D4 = proj.shape
    D = D4 // 4
    H = D // LANES
    C = _rt(T, HG_CHUNK)
    SB = min(HG_SUB, C)
    NC = T // C

    def body(q_ref, f_ref, v_ref, g_ref, o_ref, dog_ref, st_ref, lb_ref, gg_ref, dp_ref, dlb_ref, dgg_ref,
             DST, bex):
        @pl.when(pl.program_id(0) == 0)
        def _():
            DST[...] = jnp.zeros_like(DST)
            dlb_ref[...] = jnp.zeros_like(dlb_ref)
            dgg_ref[...] = jnp.zeros_like(dgg_ref)
        r_i = lax.broadcasted_iota(jnp.int32, (C, C), 0)
        c_i = lax.broadcasted_iota(jnp.int32, (C, C), 1)
        causal = r_i >= c_i
        low = _bf(causal.astype(F32))
        upp = _bf((r_i <= c_i).astype(F32))
        rs = lax.broadcasted_iota(jnp.int32, (SB, C), 0)
        cs = lax.broadcasted_iota(jnp.int32, (SB, C), 1)
        ggv = gg_ref[...]
        for hd in range(H):
            sl = slice(hd * LANES, (hd + 1) * LANES)
            qr = q_ref[:, sl]
            lbv = lb_ref[:, sl]
            sq, q, sig, f, k, lf = _hgrn_gates(qr, f_ref[:, sl], lbv)
            v = v_ref[:, sl]
            oh = o_ref[:, sl]
            r = lax.rsqrt(jnp.mean(oh * oh, axis=-1, keepdims=True) + EPS)
            n = oh * r
            gr = g_ref[:, sl]
            sg = _sig(gr)
            dgo = dog_ref[:, sl]
            dgr = dgo * (n * ggv) * (sg * (1.0 + gr * (1.0 - sg)))
            don = dgo * (gr * sg)
            dgg_ref[...] += jnp.sum(don * n, axis=0, keepdims=True)
            dn = don * ggv
            do = r * (dn - n * jnp.mean(dn * n, axis=-1, keepdims=True))
            b = _tri_mm(low, lf)
            bex[hd] = b - lf
            bend = jnp.sum(lf, axis=0, keepdims=True)
            dob = _bf(do)
            dA = _bf(jnp.where(causal, _nt(dob, _bf(v)), 0.0))
            blocks, dqs, gqs = [], [], []
            dk = jnp.zeros((C, LANES), F32)
            gk = jnp.zeros((C, LANES), F32)
            for I in range(C // SB):
                p = bex[hd, pl.ds(SB * I, 1), :]
                rows = slice(SB * I, SB * (I + 1))
                eq = jnp.exp(b[rows] - p)
                qI = _bf(q[rows] * eq)
                ek = jnp.exp(jnp.minimum(p - b, EXP_CLAMP))
                kI = _bf(k * ek)
                blocks.append(jnp.where(rs + SB * I >= cs, _nt(qI, kI), 0.0))
                dqI = _nn(dA[rows], kI)
                dkI = _tn(dA[rows], qI)
                dqs.append(dqI * eq)
                gqs.append(qI.astype(F32) * dqI)
                dk = dk + dkI * ek
                gk = gk + kI.astype(F32) * dkI
            A = jnp.concatenate(blocks, axis=0)
            dst = DST[hd]
            st0 = st_ref[0, hd]
            eb = jnp.exp(b)
            ee = jnp.exp(bend - b)
            ebend = jnp.exp(bend)
            dv = _tn(_bf(A), dob) + _nt(_bf(k * ee), _bf(dst))
            dk_state = ee * _nn_st(v, dst)
            dq_state = eb * _nn_st(do, st0)
            dq = jnp.concatenate(dqs, axis=0) + dq_state
            dk = dk + dk_state
            DST[hd] = dst * ebend + _tn_st(do, q * eb)
            later = jnp.sum(k * dk_state, axis=0, keepdims=True) + ebend * jnp.sum(st0 * dst, axis=0, keepdims=True)
            pairs = jnp.concatenate(gqs, axis=0) - gk + (q * dq_state - k * dk_state)
            dlf = _tri_mm(upp, pairs) + later
            df = dlf / f - dk
            dlb_ref[:, sl] += jnp.sum(df * (1.0 - sig), axis=0, keepdims=True)
            dp_ref[:, sl] = _bf(dq * (sq * (1.0 + qr * (1.0 - sq))))
            dp_ref[:, D + hd * LANES:D + (hd + 1) * LANES] = _bf(df * (1.0 - lbv) * (sig * (1.0 - sig)))
            dp_ref[:, 2 * D + hd * LANES:2 * D + (hd + 1) * LANES] = _bf(dv)
            dp_ref[:, 3 * D + hd * LANES:3 * D + (hd + 1) * LANES] = _bf(dgr)

    col = lambda j: pl.BlockSpec((C, D), lambda c, j=j: (NC - 1 - c, j))
    tile = pl.BlockSpec((C, D), lambda c: (NC - 1 - c, 0))
    own, rider = _call(
        body, name, (NC,),
        [col(0), col(1), col(2), col(3), tile, tile,
         pl.BlockSpec((1, H, LANES, LANES), lambda c: (NC - 1 - c, 0, 0, 0)), _row(D), _row(LANES)],
        [pl.BlockSpec((C, D4), lambda c: (NC - 1 - c, 0)), _row(D), _row(LANES)],
        [jax.ShapeDtypeStruct((T, D4), BF16), jax.ShapeDtypeStruct((1, D), F32), jax.ShapeDtypeStruct((1, LANES), F32)],
        [pltpu.VMEM((H, LANES, LANES), F32), pltpu.VMEM((H, C, LANES), F32)],
        ("arbitrary",), (proj, proj, proj, proj, o, dog, st, lb, gg), ex)
    return (*own, rider) if ex is not None else own


def _prev_blk(tm, hb):
    return lambda i: (jnp.maximum(i * (tm // hb) - 1, 0), 0)


def _next_blk(tm, hb, T):
    return lambda i: (jnp.minimum((i + 1) * (tm // hb), T // hb - 1), 0)


def _glu_to_ext(uc_ref, ucp_ref, ext, D, first):
    def glu(u):
        return u[:, :D] * _sig(u[:, D:])
    gh = jnp.where(first, 0.0, glu(ucp_ref[...]))
    gm = glu(uc_ref[...])
    for c in range(D // LANES):
        ext[c, 0:CONV_HALO, :] = gh[:, c * LANES:(c + 1) * LANES]
        ext[c, CONV_HALO:, :] = gm[:, c * LANES:(c + 1) * LANES]


def conv_mid_fwd(uc, cw, cb, lg, lbias, name):
    T, D2 = uc.shape
    D = D2 // 2
    NCH = D // LANES
    W = 31
    tm = _rt(T, 256)
    RS = min(CONV_ROWS, tm)

    def body(uc_ref, ucp_ref, cw_ref, cb_ref, lg_ref, lb_ref, cv_ref, va_ref, ext, outs):
        _glu_to_ext(uc_ref, ucp_ref, ext, D, pl.program_id(0) == 0)

        def chunk(c, carry):
            for rb in range(tm // RS):
                acc = jnp.zeros((RS, LANES), F32)
                for kk in range(W):
                    acc = acc + cw_ref[c, pl.ds(kk, 1), :] * ext[c, pl.ds(rb * RS + CONV_HALO - (W - 1) + kk, RS), :]
                outs[c, pl.ds(rb * RS, RS), :] = acc
            return carry
        lax.fori_loop(0, NCH, chunk, 0)
        cv = jnp.concatenate([outs[c] for c in range(NCH)], axis=1) + cb_ref[...]
        cv_ref[...] = cv
        mu = jnp.mean(cv, axis=-1, keepdims=True)
        xc = cv - mu
        rstd = lax.rsqrt(jnp.mean(xc * xc, axis=-1, keepdims=True) + EPS)
        l = xc * rstd * lg_ref[...] + lb_ref[...]
        va_ref[...] = _bf(l * _sig(l))

    tile = pl.BlockSpec((tm, D), lambda i: (i, 0))
    return pl.pallas_call(
        body, name=name, grid=(T // tm,),
        in_specs=[pl.BlockSpec((tm, D2), lambda i: (i, 0)), pl.BlockSpec((CONV_HALO, D2), _prev_blk(tm, CONV_HALO)),
                  pl.BlockSpec((NCH, 32, LANES), lambda i: (0, 0, 0)), _row(D), _row(D), _row(D)],
        out_specs=[tile, tile],
        out_shape=[jax.ShapeDtypeStruct((T, D), F32), jax.ShapeDtypeStruct((T, D), BF16)],
        scratch_shapes=[pltpu.VMEM((NCH, tm + CONV_HALO, LANES), F32), pltpu.VMEM((NCH, tm, LANES), F32)],
        compiler_params=_cp("parallel"))(uc, uc, cw, cb, lg, lbias)


def conv_bwd_a(dva, cv, uc, lg, lbias, name):
    T, D2 = uc.shape
    D = D2 // 2
    NCH = D // LANES
    W = 31
    tm = _rt(T, 256)
    RS = min(CONV_ROWS, tm)

    def body(dva_ref, cv_ref, uc_ref, ucp_ref, lg_ref, lb_ref, dcv_ref, dlg_ref, dlb_ref, taps_ref, ext, dcs):
        @pl.when(pl.program_id(0) == 0)
        def _():
            dlg_ref[...] = jnp.zeros_like(dlg_ref)
            dlb_ref[...] = jnp.zeros_like(dlb_ref)
            taps_ref[...] = jnp.zeros_like(taps_ref)
        _glu_to_ext(uc_ref, ucp_ref, ext, D, pl.program_id(0) == 0)
        cv = cv_ref[...]
        mu = jnp.mean(cv, axis=-1, keepdims=True)
        xc = cv - mu
        rstd = lax.rsqrt(jnp.mean(xc * xc, axis=-1, keepdims=True) + EPS)
        yn = xc * rstd
        l = yn * lg_ref[...] + lb_ref[...]
        s = _sig(l)
        dl = dva_ref[...] * (s * (1.0 + l * (1.0 - s)))
        dlg_ref[...] += jnp.sum(dl * yn, axis=0, keepdims=True)
        dlb_ref[...] += jnp.sum(dl, axis=0, keepdims=True)
        dyn = dl * lg_ref[...]
        dcv = rstd * (dyn - jnp.mean(dyn, axis=-1, keepdims=True) - yn * jnp.mean(dyn * yn, axis=-1, keepdims=True))
        dcv_ref[...] = dcv
        for c in range(NCH):
            dcs[c] = dcv[:, c * LANES:(c + 1) * LANES]

        def fold(p):
            return p.reshape(RS // 8, 8, LANES).sum(axis=0)

        def chunk(c, carry):
            for kk in range(W):
                tot = jnp.zeros((8, LANES), F32)
                for rb in range(tm // RS):
                    tot = tot + fold(dcs[c, pl.ds(rb * RS, RS), :]
                                     * ext[c, pl.ds(rb * RS + CONV_HALO - (W - 1) + kk, RS), :])
                taps_ref[c, pl.ds(8 * kk, 8), :] += tot
            tot = jnp.zeros((8, LANES), F32)
            for rb in range(tm // RS):
                tot = tot + fold(dcs[c, pl.ds(rb * RS, RS), :])
            taps_ref[c, pl.ds(8 * W, 8), :] += tot
            return carry
        lax.fori_loop(0, NCH, chunk, 0)

    tile = pl.BlockSpec((tm, D), lambda i: (i, 0))
    vec = jax.ShapeDtypeStruct((1, D), F32)
    return pl.pallas_call(
        body, name=name, grid=(T // tm,),
        in_specs=[tile, tile, pl.BlockSpec((tm, D2), lambda i: (i, 0)),
                  pl.BlockSpec((CONV_HALO, D2), _prev_blk(tm, CONV_HALO)), _row(D), _row(D)],
        out_specs=[tile, _row(D), _row(D), pl.BlockSpec((NCH, 256, LANES), lambda i: (0, 0, 0))],
        out_shape=[jax.ShapeDtypeStruct((T, D), F32), vec, vec, jax.ShapeDtypeStruct((NCH, 256, LANES), F32)],
        scratch_shapes=[pltpu.VMEM((NCH, tm + CONV_HALO, LANES), F32), pltpu.VMEM((NCH, tm, LANES), F32)],
        compiler_params=_cp("arbitrary"))(dva, cv, uc, uc, lg, lbias)


def conv_bwd_b(dcv, uc, cw, name):
    T, D2 = uc.shape
    D = D2 // 2
    NCH = D // LANES
    W = 31
    tm = _rt(T, 256)
    RS = min(CONV_ROWS, tm)
    NT = T // tm

    def body(dcv_ref, dcn_ref, uc_ref, cw_ref, du_ref, db_ref, ext, outs):
        i = pl.program_id(0)

        @pl.when(i == 0)
        def _():
            db_ref[...] = jnp.zeros_like(db_ref)
        dm = dcv_ref[...]
        dn = jnp.where(i == NT - 1, 0.0, dcn_ref[...])
        for c in range(NCH):
            ext[c, 0:tm, :] = dm[:, c * LANES:(c + 1) * LANES]
            ext[c, tm:, :] = dn[:, c * LANES:(c + 1) * LANES]

        def chunk(c, carry):
            for rb in range(tm // RS):
                acc = jnp.zeros((RS, LANES), F32)
                for kk in range(W):
                    acc = acc + cw_ref[c, pl.ds(kk, 1), :] * ext[c, pl.ds(rb * RS + (W - 1) - kk, RS), :]
                outs[c, pl.ds(rb * RS, RS), :] = acc
            return carry
        lax.fori_loop(0, NCH, chunk, 0)
        dglu = jnp.concatenate([outs[c] for c in range(NCH)], axis=1)
        u = uc_ref[...]
        a = u[:, :D]
        sg = _sig(u[:, D:])
        da = dglu * sg
        dg = dglu * a * (sg * (1.0 - sg))
        du_ref[:, :D] = _bf(da)
        du_ref[:, D:] = _bf(dg)
        db_ref[:, :D] += jnp.sum(da, axis=0, keepdims=True)
        db_ref[:, D:] += jnp.sum(dg, axis=0, keepdims=True)

    tile = pl.BlockSpec((tm, D), lambda i: (i, 0))
    wide = pl.BlockSpec((tm, D2), lambda i: (i, 0))
    return pl.pallas_call(
        body, name=name, grid=(NT,),
        in_specs=[tile, pl.BlockSpec((CONV_HALO, D), _next_blk(tm, CONV_HALO, T)), wide,
                  pl.BlockSpec((NCH, 32, LANES), lambda i: (0, 0, 0))],
        out_specs=[wide, _row(D2)],
        out_shape=[jax.ShapeDtypeStruct((T, D2), BF16), jax.ShapeDtypeStruct((1, D2), F32)],
        scratch_shapes=[pltpu.VMEM((NCH, tm + CONV_HALO, LANES), F32), pltpu.VMEM((NCH, tm, LANES), F32)],
        compiler_params=_cp("arbitrary"))(dcv, dcv, uc, cw)


def _shift_rows(x, halo, k, back):
    rows, n = x.shape
    x3, h3 = x.reshape(rows // 8, 8, n), halo.reshape(1, 8, n)
    sub = lax.broadcasted_iota(jnp.int32, (1, 8, n), 1)
    if back:
        r = pltpu.roll(jnp.concatenate([h3, x3], axis=0), k, 1)
        y = jnp.where(sub < k, r[:-1], r[1:])
    else:
        r = pltpu.roll(jnp.concatenate([x3, h3], axis=0), 8 - k, 1)
        y = jnp.where(sub < 8 - k, r[:-1], r[1:])
    return y.reshape(rows, n)


def _conv3(u, uh, dw_ref, bias_ref):
    return (dw_ref[pl.ds(0, 1), :] * _shift_rows(u, uh, 2, True) + dw_ref[pl.ds(1, 1), :] * _shift_rows(u, uh, 1, True)
            + dw_ref[pl.ds(2, 1), :] * u + bias_ref[...])


def ffn_fwd(h, wup, dww, dwb, wdn, x, gate, pg, name):
    T, D = h.shape
    fs = wup.shape[2]
    NJ = wup.shape[0] // 2
    tm = _rt(T, 512)

    def body(h_ref, hp_ref, wa_ref, wb_ref, dwa_ref, dwb_ref, ba_ref, bb_ref, wd_ref, x_ref, gt_ref, pg_ref,
             u_ref, uc_ref, y_ref, xo_ref, acc):
        i, j = pl.program_id(0), pl.program_id(1)

        @pl.when(j == 0)
        def _():
            acc[...] = jnp.zeros_like(acc)
        hb = h_ref[...]
        hp = hp_ref[...]

        def half(s, w_ref, dw_ref, b_ref):
            u = _nn(hb, w_ref[...])
            uh = jnp.where(i == 0, 0.0, _nn(hp, w_ref[...]))
            u_ref[s] = u
            uc = _conv3(u, uh, dw_ref, b_ref)
            uc_ref[s] = uc
            return uc
        a = half(0, wa_ref, dwa_ref, ba_ref)
        b = half(1, wb_ref, dwb_ref, bb_ref)
        acc[...] += _nn(_bf(a * _sig(a) * b), wd_ref[...])

        @pl.when(j == NJ - 1)
        def _():
            y = acc[...]
            y_ref[...] = y
            r = lax.rsqrt(jnp.mean(y * y, axis=-1, keepdims=True) + EPS)
            xo_ref[...] = x_ref[...] + gt_ref[...] * ((y * r) * pg_ref[...])

    tile = pl.BlockSpec((tm, D), lambda i, j: (i, 0))
    shard = lambda off, r: pl.BlockSpec((None, r, fs), lambda i, j: (j + off, 0, 0))
    ush = pl.BlockSpec((2, None, tm, fs), lambda i, j: (0, j, i, 0))
    vec = pl.BlockSpec((1, D), lambda i, j: (0, 0))
    return pl.pallas_call(
        body, name=name, grid=(T // tm, NJ),
        in_specs=[tile, pl.BlockSpec((FFN_HALO, D), lambda i, j: (jnp.maximum(i * (tm // FFN_HALO) - 1, 0), 0)),
                  shard(0, D), shard(NJ, D), shard(0, 3), shard(NJ, 3), shard(0, 1), shard(NJ, 1),
                  pl.BlockSpec((None, fs, D), lambda i, j: (j, 0, 0)), tile, vec, vec],
        out_specs=[ush, ush, tile, tile],
        out_shape=[jax.ShapeDtypeStruct((2, NJ, T, fs), F32), jax.ShapeDtypeStruct((2, NJ, T, fs), F32),
                   jax.ShapeDtypeStruct((T, D), F32), jax.ShapeDtypeStruct((T, D), F32)],
        scratch_shapes=[pltpu.VMEM((tm, D), F32)],
        compiler_params=_cp("parallel", "arbitrary"))(h, h, wup, wup, dww, dww, dwb, dwb, wdn, x, gate, pg)


def ffn_bwd_a(dy, uc, wdn, name):
    _, NJ, T, fs = uc.shape
    D = dy.shape[1]
    tm = _rt(T, 512)

    NT = T // tm

    def body(dy_ref, uc_ref, wd_ref, duc_ref, dwd_ref, acc):
        i = pl.program_id(1)

        @pl.when(i == 0)
        def _():
            acc[...] = jnp.zeros_like(acc)
        dyb = _bf(dy_ref[...])
        dz = _nt(dyb, wd_ref[...])
        a = uc_ref[0]
        b = uc_ref[1]
        sa = _sig(a)
        asa = a * sa
        acc[...] += _tn(_bf(asa * b), dyb)
        duc_ref[0] = dz * b * (sa + asa * (1.0 - sa))
        duc_ref[1] = dz * asa

        @pl.when(i == NT - 1)
        def _():
            dwd_ref[...] = _bf(acc[...])

    ush = pl.BlockSpec((2, None, tm, fs), lambda j, i: (0, j, i, 0))
    return pl.pallas_call(
        body, name=name, grid=(NJ, NT),
        in_specs=[pl.BlockSpec((tm, D), lambda j, i: (i, 0)), ush, pl.BlockSpec((None, fs, D), lambda j, i: (j, 0, 0))],
        out_specs=[ush, pl.BlockSpec((fs, D), lambda j, i: (j, 0))],
        out_shape=[jax.ShapeDtypeStruct((2, NJ, T, fs), F32), jax.ShapeDtypeStruct((NJ * fs, D), BF16)],
        scratch_shapes=[pltpu.VMEM((fs, D), F32)],
        compiler_params=_cp("parallel", "arbitrary"))(dy, uc, wdn)


def ffn_bwd_b(duc, u, h, dww, wup, name):
    NS, T, fs = duc.shape
    D = wup.shape[1]
    tm = _rt(T, 512)
    NT = T // tm

    def body(d_ref, dn_ref, u_ref, h_ref, dw_ref, w_ref, dh_ref, g_ref, dwup_ref, acc, accw, stage):
        i, s = pl.program_id(0), pl.program_id(1)

        @pl.when((i == 0) & (s == 0))
        def _():
            g_ref[...] = jnp.zeros_like(g_ref)

        @pl.when(s == 0)
        def _():
            acc[...] = jnp.zeros_like(acc)
        d = d_ref[...]
        dn = jnp.where(i == NT - 1, 0.0, dn_ref[...])
        d1 = _shift_rows(d, dn, 1, False)
        d2 = _shift_rows(d, dn, 2, False)
        du = _bf(dw_ref[pl.ds(2, 1), :] * d + dw_ref[pl.ds(1, 1), :] * d1 + dw_ref[pl.ds(0, 1), :] * d2)
        acc[...] += _nt(du, w_ref[...])
        dw_part = _tn(h_ref[...], du)

        @pl.when(i == 0)
        def _():
            accw[s] = dw_part

        @pl.when(i > 0)
        def _():
            accw[s] += dw_part

        @pl.when(i == NT - 1)
        def _():
            stage[...] = _bf(accw[s])
            pltpu.sync_copy(stage, dwup_ref.at[s])
        u = u_ref[...]
        g_ref[s, pl.ds(0, 1), :] += jnp.sum(d2 * u, axis=0, keepdims=True)
        g_ref[s, pl.ds(1, 1), :] += jnp.sum(d1 * u, axis=0, keepdims=True)
        g_ref[s, pl.ds(2, 1), :] += jnp.sum(d * u, axis=0, keepdims=True)
        g_ref[s, pl.ds(3, 1), :] += jnp.sum(d, axis=0, keepdims=True)

        @pl.when(s == NS - 1)
        def _():
            dh_ref[...] = acc[...]

    ush = pl.BlockSpec((None, tm, fs), lambda i, s: (s, i, 0))
    unext = pl.BlockSpec((None, FFN_HALO, fs),
                         lambda i, s: (s, jnp.minimum((i + 1) * (tm // FFN_HALO), T // FFN_HALO - 1), 0))
    tile = pl.BlockSpec((tm, D), lambda i, s: (i, 0))
    return pl.pallas_call(
        body, name=name, grid=(NT, NS),
        in_specs=[ush, unext, ush, tile, pl.BlockSpec((None, 3, fs), lambda i, s: (s, 0, 0)),
                  pl.BlockSpec((None, D, fs), lambda i, s: (s, 0, 0))],
        out_specs=[tile, pl.BlockSpec((NS, 8, fs), lambda i, s: (0, 0, 0)), ANY],
        out_shape=[jax.ShapeDtypeStruct((T, D), F32), jax.ShapeDtypeStruct((NS, 8, fs), F32),
                   jax.ShapeDtypeStruct((NS, D, fs), BF16)],
        scratch_shapes=[pltpu.VMEM((tm, D), F32), pltpu.VMEM((NS, D, fs), F32), pltpu.VMEM((D, fs), BF16)],
        compiler_params=pltpu.CompilerParams(dimension_semantics=("arbitrary", "arbitrary"),
                                             vmem_limit_bytes=FFN_BWD_VMEM))(duc, duc, u, h, dww, wup)


def ada_fwd(c_all, w, bias, name):
    L, D, n = w.shape

    def body(c_ref, w_ref, b_ref, o_ref):
        cv = c_ref[...]
        o_ref[...] = _nn(_bf(cv * _sig(cv)), _bf(w_ref[...])) + b_ref[...]

    return pl.pallas_call(
        body, name=name, grid=(L,),
        in_specs=[pl.BlockSpec((N_DEV, D), lambda l: (0, 0)), pl.BlockSpec((None, D, n), lambda l: (l, 0, 0)),
                  pl.BlockSpec((None, 1, n), lambda l: (l, 0, 0))],
        out_specs=pl.BlockSpec((None, N_DEV, n), lambda l: (l, 0, 0)),
        out_shape=jax.ShapeDtypeStruct((L, N_DEV, n), F32), compiler_params=_cp("parallel"))(c_all, w, bias)


def ada_bwd(c_all, dm, name):
    L, _, n = dm.shape
    D = c_all.shape[1]

    def body(c_ref, d_ref, o_ref):
        cv = c_ref[...]
        o_ref[...] = _tn(_bf(cv * _sig(cv)), _bf(d_ref[...]))

    return pl.pallas_call(
        body, name=name, grid=(L,),
        in_specs=[pl.BlockSpec((N_DEV, D), lambda l: (0, 0)), pl.BlockSpec((None, N_DEV, n), lambda l: (l, 0, 0))],
        out_specs=pl.BlockSpec((None, D, n), lambda l: (l, 0, 0)),
        out_shape=jax.ShapeDtypeStruct((L, D, n), F32), compiler_params=_cp("parallel"))(c_all, dm)


def sum_slots(g, name):
    S, R, C = g.shape

    def body(g_ref, o_ref):
        acc = g_ref[0]
        for s in range(1, S):
            acc = acc + g_ref[s]
        o_ref[...] = acc

    return pl.pallas_call(
        body, name=name, grid=(1,), in_specs=[pl.BlockSpec((S, R, C), lambda i: (0, 0, 0))],
        out_specs=pl.BlockSpec((R, C), lambda i: (0, 0)), out_shape=jax.ShapeDtypeStruct((R, C), F32),
        compiler_params=_cp("arbitrary"))(g)


def lb_softmax(logits, name):
    def body(l_ref, s_ref):
        l = l_ref[...]
        e = jnp.exp(l - jnp.max(l, axis=0, keepdims=True))
        s_ref[...] = e / jnp.sum(e, axis=0, keepdims=True)
    return pl.pallas_call(body, name=name, out_shape=jax.ShapeDtypeStruct(logits.shape, F32))(logits)


def adamw(w, gs, m, v, name):
    S, R, C = gs.shape
    tr = R
    budget = (4 * 1024 * 1024) // (4 * (S + 7) * C)
    if R > budget:
        tr = max(t for t in range(16, budget + 1, 16) if R % t == 0)

    def body(w_ref, g_ref, m_ref, v_ref, go_ref, d_ref, mo_ref, vo_ref):
        g = g_ref[0].astype(F32)
        for s in range(1, S):
            g = g + g_ref[s].astype(F32)
        go_ref[...] = g
        mn = ADAM_B1 * m_ref[...] + (1.0 - ADAM_B1) * g
        vn = ADAM_B2 * v_ref[...] + (1.0 - ADAM_B2) * (g * g)
        mo_ref[...] = mn
        vo_ref[...] = vn
        m_hat = mn / (1.0 - ADAM_B1 ** ADAM_STEP)
        v_hat = vn / (1.0 - ADAM_B2 ** ADAM_STEP)
        d_ref[...] = -ADAM_LR * (m_hat / (jnp.sqrt(v_hat) + ADAM_EPS) + ADAM_WD * w_ref[...])

    tile = pl.BlockSpec((tr, C), lambda i: (i, 0))
    out = jax.ShapeDtypeStruct((R, C), F32)
    return pl.pallas_call(
        body, name=name, grid=(R // tr,), in_specs=[tile, pl.BlockSpec((S, tr, C), lambda i: (0, i, 0)), tile, tile],
        out_specs=[tile] * 4, out_shape=[out] * 4, compiler_params=_cp("parallel"))(w, gs, m, v)


def _place():
    return lax.axis_index("x"), lax.axis_index("y"), lax.axis_index("c")


def _slot(p):
    return 4 * p[0] + 2 * p[1] + p[2]


class _Gather:
    def __init__(self, xs):
        self.ins = list(xs)
        n = self.n = len(xs)
        self.out_shapes = [jax.ShapeDtypeStruct((N_DEV,) + a.shape, a.dtype) for a in xs]
        self.scratch = [pltpu.SemaphoreType.DMA((n, 7)), pltpu.SemaphoreType.DMA((n, 7)), pltpu.SemaphoreType.DMA((n,))]

    def _parts(self, ins, outs, sems):
        send_sems, recv_sems, local_sems = sems
        x, y, c = _place()
        me, sib = (x, y, c), (x, y, 1 - c)
        chips = [(1 - x, y), (x, 1 - y), (1 - x, 1 - y)]

        def copy(a, k, block, to, src=None):
            dst = outs[a].at[_slot(block)]
            return pltpu.make_async_remote_copy(
                src_ref=dst if src is None else src, dst_ref=dst, send_sem=send_sems.at[a, k],
                recv_sem=recv_sems.at[a, k], device_id=to, device_id_type=MESH)

        mine = [pltpu.make_async_copy(ins[a], outs[a].at[_slot(me)], local_sems.at[a]) for a in range(self.n)]
        first = []
        for a in range(self.n):
            first.append(copy(a, 0, me, sib, src=ins[a]))
            first += [copy(a, 1 + j, me, (*chip, c), src=ins[a]) for j, chip in enumerate(chips)]
        return copy, mine, first, me, sib, chips, c

    def start(self, ins, outs, sems):
        _, mine, first, *_ = self._parts(ins, outs, sems)
        for cp in mine + first:
            cp.start()

    def finish(self, ins, outs, sems):
        copy, mine, first, me, sib, chips, c = self._parts(ins, outs, sems)
        passed = []
        for j, chip in enumerate(chips):
            for a in range(self.n):
                copy(a, 1 + j, (*chip, c), me).wait_recv()
                fwd = copy(a, 4 + j, (*chip, c), sib)
                fwd.start()
                passed.append(fwd)
        for a in range(self.n):
            copy(a, 0, sib, me).wait_recv()
            for j, chip in enumerate(chips):
                copy(a, 4 + j, (*chip, 1 - c), me).wait_recv()
        for cp in first + passed:
            cp.wait_send()
        for cp in mine:
            cp.wait()


def _run_alone(ex, name):
    def body(*refs):
        ni, no = len(ex.ins), len(ex.out_shapes)
        parts = refs[:ni], refs[ni:ni + no], refs[ni + no:]
        ex.start(*parts)
        ex.finish(*parts)
    return pl.pallas_call(body, name=name, in_specs=[ANY] * len(ex.ins), out_specs=[ANY] * len(ex.out_shapes),
                          out_shape=ex.out_shapes, scratch_shapes=ex.scratch)(*ex.ins)


def _ride(body, n_in, n_out, ex, first, last):
    ni, no = len(ex.ins), len(ex.out_shapes)

    def wrapped(*refs):
        a, r_in = refs[:n_in], refs[n_in:n_in + ni]
        b, r_out = refs[n_in + ni:n_in + ni + n_out], refs[n_in + ni + n_out:n_in + ni + n_out + no]
        rest = refs[n_in + ni + n_out + no:]
        s, r_s = rest[:len(rest) - len(ex.scratch)], rest[len(rest) - len(ex.scratch):]

        @pl.when(first())
        def _():
            ex.start(r_in, r_out, r_s)
        body(*a, *b, *s)

        @pl.when(last())
        def _():
            ex.finish(r_in, r_out, r_s)
    return wrapped


def all_gather(xs, name):
    return _run_alone(_Gather(xs), name)


class _Exchange:
    def __init__(self, groups):
        self.ins = [a for grp in groups for a in grp]
        self.where = [(g, i) for g, grp in enumerate(groups) for i in range(len(grp))]
        n = self.n = len(self.ins)
        self.out_shapes = [jax.ShapeDtypeStruct((N_DEV, len(grp)) + grp[0].shape[1:], grp[0].dtype) for grp in groups]
        self.scratch = [pltpu.SemaphoreType.DMA((n, 7)), pltpu.SemaphoreType.DMA((n, 7)), pltpu.SemaphoreType.DMA((n,))]

    def _parts(self, ins, outs, sems):
        send_sems, recv_sems, local_sems = sems
        x, y, c = _place()
        me = (x, y, c)

        def peer(r):
            return (1 - x if r & 4 else x, 1 - y if r & 2 else y, 1 - c if r & 1 else c)

        def land(a, src):
            g, i = self.where[a]
            return outs[g].at[_slot(src), i]

        def copy(a, r, src_dev):
            p = peer(r)
            return pltpu.make_async_remote_copy(
                src_ref=ins[a].at[_slot(p)], dst_ref=land(a, src_dev), send_sem=send_sems.at[a, r - 1],
                recv_sem=recv_sems.at[a, r - 1], device_id=p, device_id_type=MESH)

        mine = [pltpu.make_async_copy(ins[a].at[_slot(me)], land(a, me), local_sems.at[a]) for a in range(self.n)]
        sends = [copy(a, r, me) for r in range(1, N_DEV) for a in range(self.n)]
        arrivals = [copy(a, r, peer(r)) for r in range(1, N_DEV) for a in range(self.n)]
        return mine, sends, arrivals

    def start(self, ins, outs, sems):
        mine, sends, _ = self._parts(ins, outs, sems)
        for cp in mine + sends:
            cp.start()

    def finish(self, ins, outs, sems):
        mine, sends, arrivals = self._parts(ins, outs, sems)
        for cp in arrivals:
            cp.wait_recv()
        for cp in sends:
            cp.wait_send()
        for cp in mine:
            cp.wait()


def all_to_all(groups, name):
    return _run_alone(_Exchange(groups), name)


def _mods(mod, l, D):
    return [mod[l:l + 1, k * D:(k + 1) * D] for k in range(6)]


def _ffn_backward(l, dxo, dy, hb, u, uc, xin, sc2, gains, W, post):
    _, NJ, T, fs = u.shape
    duc, d_wdn = ffn_bwd_a(dy, uc, W["wdn"][l], f"ffn{l}_bwd_a")
    dh, taps, d_wup = ffn_bwd_b(duc.reshape(2 * NJ, T, fs), u.reshape(2 * NJ, T, fs), hb, W["fdw_w"][l],
                                W["wup"][l], f"ffn{l}_bwd_b")
    dx, da, dsh, *below = resid_bwd(f"ffn{l}_resid_bwd", pre=(dh, xin, dxo, gains["pre_ffn_g"][l:l + 1], sc2), post=post)
    small = dict(da=da, dsh=dsh, dwb=taps[:, 3], dww=taps[:, 0:3])
    return dx, d_wup, d_wdn, small, below


def _local_step(xt, tgt, mod, gains, W, mine):
    T, D = xt.shape
    zero_d = jnp.zeros((1, D), F32)
    half = lambda g: g.reshape(N_DEV // 2, 2 * g.shape[1], D)
    sh1, sc1, g1, sh2, sc2, g2 = m0 = _mods(mod, 0, D)
    h0 = prenorm_mod(xt, gains["pre_mix_g"][0:1], sc1, sh1, "l0_prenorm")
    proj, (whout, wup0, wdn0) = mm_nn(h0, W["whin"], jnp.zeros((1, 4 * D), F32), "hgrn_proj",
                                      _Gather([mine["whout"], mine["wup"][0], mine["wdn"][0]]))
    o, og, st, (wcin, wcout, wup1, wdn1) = hgrn_fwd(
        proj, W["lb"], W["gg"], "hgrn_fwd", _Gather([mine["wcin"], mine["wcout"], mine["wup"][1], mine["wdn"][1]]))
    W = dict(W, whout=whout.reshape(D, D), wcin=wcin, wcout=wcout.reshape(D, D), wup=[wup0, wup1],
             wdn=[half(wdn0), half(wdn1)])
    y0, x1 = mm_post(og, W["whout"], zero_d, xt, g1, gains["post_mix_g"][0:1], "hgrn_out")
    h0f = prenorm_mod(x1, gains["pre_ffn_g"][0:1], sc2, sh2, "f0_prenorm")
    u0, uc0, yf0, x2 = ffn_fwd(h0f, W["wup"][0], W["fdw_w"][0], W["fdw_b"][0], W["wdn"][0], x1, g2,
                                gains["post_ffn_g"][0:1], "ffn0_fwd")
    t1, c1, e1, t2, c2, e2 = m1 = _mods(mod, 1, D)
    h1 = prenorm_mod(x2, gains["pre_mix_g"][1:2], c1, t1, "l1_prenorm")
    uc = mm_nn(h1, W["wcin"], W["cb_in"], "conv_in")
    cv, va = conv_mid_fwd(uc, W["cw"], W["cdw_b"], W["ln_g"], W["ln_b"], "conv_mid_fwd")
    y1, x3 = mm_post(va, W["wcout"], W["cb_out"], x2, e1, gains["post_mix_g"][1:2], "conv_out")
    h1f = prenorm_mod(x3, gains["pre_ffn_g"][1:2], c2, t2, "f1_prenorm")
    u1, uc1, yf1, x4 = ffn_fwd(h1f, W["wup"][1], W["fdw_w"][1], W["fdw_b"][1], W["wdn"][1], x3, e2,
                                gains["post_ffn_g"][1:2], "ffn1_fwd")
    dx4, loss_part, dyf1, dgate_f1, dpost_f1, _ = resid_bwd(
        "loss_resid_bwd", loss=(x4, tgt), post=(yf1, e2, gains["post_ffn_g"][1:2]))
    dx3, d_wup1, d_wdn1, sf1, (dy1, dgate_c, dpost_c, dys_c) = _ffn_backward(
        1, dx4, dyf1, h1f, u1, uc1, x3, c2, gains, W, (y1, e1, gains["post_mix_g"][1:2]))
    dva = mm_nt(dy1, W["wcout"][None], "conv_dva", True)
    d_wcout = mm_tn(va, dy1, "conv_dwout", True, True, D, D)
    dcv, dlg, dlbias, taps = conv_bwd_a(dva, cv, uc, W["ln_g"], W["ln_b"], "conv_bwd_a")
    duc, dbin = conv_bwd_b(dcv, uc, W["cw"], "conv_bwd_b")
    dh1 = mm_nt(duc, W["wcin"], "conv_dh", True)
    d_wcin = mm_tn(h1, duc, "conv_dwin", True, True, D, W["wcin"].shape[2])
    dx2, da_c, dsh_c, dyf0, dgate_f0, dpost_f0, _ = resid_bwd(
        "l1_resid_bwd", pre=(dh1, x2, dx3, gains["pre_mix_g"][1:2], c1), post=(yf0, g2, gains["post_ffn_g"][0:1]))
    dx1, d_wup0, d_wdn0, sf0, (dy0, dgate_h, dpost_h, _) = _ffn_backward(
        0, dx2, dyf0, h0f, u0, uc0, x1, sc2, gains, W, (y0, g1, gains["post_mix_g"][0:1]))
    dog = mm_nt(dy0, W["whout"][None], "hgrn_dog", True)
    d_whout = mm_tn(og, dy0, "hgrn_dwout", True, True, D, D)
    rows = lambda g: g.reshape(N_DEV, -1, D)
    dproj, dlb, dgg, (r_wcin, r_wcout, r_wup, r_wdn, r_whout) = hgrn_bwd(
        proj, o, dog, st, W["lb"], W["gg"], "hgrn_bwd",
        _Exchange([[d_wcin], [rows(d_wcout)], [d_wup0, d_wup1], [rows(d_wdn0), rows(d_wdn1)], [rows(d_whout)]]))
    d_whin = mm_tn(h0, dproj, "hgrn_dwin", True, True, D, W["whin"].shape[2])
    dh0, (r_whin,) = mm_nt(dproj, W["whin"], "hgrn_dh", True, _Exchange([[d_whin]]))
    dx0, da_h, dsh_h = resid_bwd("l0_resid_bwd", pre=(dh0, xt, dx1, gains["pre_mix_g"][0:1], sc1))
    sf0.update(dgate=dgate_f0, dpost=dpost_f0)
    sf1.update(dgate=dgate_f1, dpost=dpost_f1)

    big = dict(hgrn_w_in=r_whin, hgrn_w_out=r_whout, conv_w_in=r_wcin, conv_w_out=r_wcout, ffn_w_up=r_wup,
               ffn_w_down=r_wdn)
    pm, pf = gains["pre_mix_g"], gains["pre_ffn_g"]
    dmod = jnp.concatenate([
        dsh_h, da_h * pm[0:1], dgate_h, sf0["dsh"], sf0["da"] * pf[0:1], sf0["dgate"],
        dsh_c, da_c * pm[1:2], dgate_c, sf1["dsh"], sf1["da"] * pf[1:2], sf1["dgate"]], axis=1)
    NCH = D // LANES
    tap = taps.reshape(NCH, 32, 8, LANES).sum(axis=2)
    small = dict(
        dmod=dmod,
        pre_mix_g=jnp.concatenate([da_h * (1.0 + sc1), da_c * (1.0 + c1)], axis=0),
        post_mix_g=jnp.concatenate([dpost_h, dpost_c], axis=0),
        pre_ffn_g=jnp.concatenate([sf0["da"] * (1.0 + sc2), sf1["da"] * (1.0 + c2)], axis=0),
        post_ffn_g=jnp.concatenate([sf0["dpost"], sf1["dpost"]], axis=0),
        lb=dlb, gg=dgg,
        ffn_dw_b=jnp.stack([sf0["dwb"], sf1["dwb"]]),
        conv_b_in=dbin,
        conv_dw_w=jnp.transpose(tap[:, :31], (1, 0, 2)).reshape(31, D),
        conv_dw_b=tap[:, 31].reshape(1, D),
        conv_ln_g=dlg, conv_ln_b=dlbias, conv_b_out=dys_c,
        ffn_dw_w=jnp.stack([sf0["dww"], sf1["dww"]]))
    return loss_part, dx0, big, small


SMALL_ORDER = ["dmod", "pre_mix_g", "post_mix_g", "pre_ffn_g", "post_ffn_g", "lb", "gg", "ffn_dw_b", "conv_b_in",
               "conv_dw_w", "conv_dw_b", "conv_ln_g", "conv_ln_b", "conv_b_out", "ffn_dw_w"]


def _pack(parts):
    flat = jnp.concatenate([p.reshape(-1) for p in parts])
    pad = (-flat.shape[0]) % LANES
    if pad:
        flat = jnp.concatenate([flat, jnp.zeros((pad,), F32)])
    return flat.reshape(-1, LANES)


def _unpack(flat, shapes):
    out, off = [], 0
    for s in shapes:
        n = 1
        for d in s:
            n *= d
        out.append(flat[off:off + n].reshape(s))
        off += n
    return out


def _apply_adamw(name, w, g_slots, m, v):
    shp = w.shape
    C = shp[-1]
    R = w.size // C
    g, d, mn, vn = adamw(w.reshape(R, C), g_slots.reshape(g_slots.shape[0], R, C), m.reshape(R, C), v.reshape(R, C),
                         "adamw_" + name)
    return g.reshape(shp), d.reshape(shp), mn.reshape(shp), vn.reshape(shp)


WEIGHTS = ['ada_w', 'ada_b', 'pre_mix_g', 'post_mix_g', 'pre_ffn_g', 'post_ffn_g', 'hgrn_w_in', 'hgrn_lb_logits',
           'hgrn_gnorm_g', 'hgrn_w_out', 'conv_w_in', 'conv_b_in', 'conv_dw_w', 'conv_dw_b', 'conv_ln_g', 'conv_ln_b',
           'conv_w_out', 'conv_b_out', 'ffn_w_up', 'ffn_dw_w', 'ffn_dw_b', 'ffn_w_down']


def kernel(x, c, ada_w, ada_b, pre_mix_g, post_mix_g, pre_ffn_g, post_ffn_g, hgrn_w_in, hgrn_lb_logits, hgrn_gnorm_g, hgrn_w_out, conv_w_in, conv_b_in, conv_dw_w, conv_dw_b, conv_ln_g, conv_ln_b, conv_w_out, conv_b_out, ffn_w_up, ffn_dw_w, ffn_dw_b, ffn_w_down, loss_target, m_ada_w, m_ada_b, m_pre_mix_g, m_post_mix_g, m_pre_ffn_g, m_post_ffn_g, m_hgrn_w_in, m_hgrn_lb_logits, m_hgrn_gnorm_g, m_hgrn_w_out, m_conv_w_in, m_conv_b_in, m_conv_dw_w, m_conv_dw_b, m_conv_ln_g, m_conv_ln_b, m_conv_w_out, m_conv_b_out, m_ffn_w_up, m_ffn_dw_w, m_ffn_dw_b, m_ffn_w_down, v_ada_w, v_ada_b, v_pre_mix_g, v_post_mix_g, v_pre_ffn_g, v_post_ffn_g, v_hgrn_w_in, v_hgrn_lb_logits, v_hgrn_gnorm_g, v_hgrn_w_out, v_conv_w_in, v_conv_b_in, v_conv_dw_w, v_conv_dw_b, v_conv_ln_g, v_conv_ln_b, v_conv_w_out, v_conv_b_out, v_ffn_w_up, v_ffn_dw_w, v_ffn_dw_b, v_ffn_w_down):
    P = dict(locals())
    _, T, D = x.shape
    me = _slot(_place())
    L = ada_w.shape[0]
    na = ada_w.shape[2]
    fs = ffn_w_up.shape[2]
    nci = conv_w_in.shape[2]
    nd = conv_dw_b.shape[1]
    NCH = D // LANES

    small_in = [c, conv_b_in, conv_dw_w, conv_dw_b, conv_ln_g, conv_ln_b, conv_b_out, ffn_dw_w]
    whin, packed = all_gather([_bf(hgrn_w_in[0]), _pack(small_in)], "gather_first")
    mine = dict(whout=_bf(hgrn_w_out[0]), wcin=_bf(conv_w_in[0]), wcout=_bf(conv_w_out[0]),
                wup=[_bf(ffn_w_up[l]) for l in range(L)], wdn=[_bf(ffn_w_down[l]) for l in range(L)])
    sm = packed.reshape(N_DEV, -1)
    offs = [0]
    for a in small_in:
        offs.append(offs[-1] + a.size)
    piece = lambda k, shp: sm[:, offs[k]:offs[k + 1]].reshape((N_DEV,) + shp)
    c_all = piece(0, (D,))
    dw_nat = jnp.transpose(piece(2, (31, nd)), (1, 0, 2)).reshape(31, D)
    cw = jnp.pad(jnp.transpose(dw_nat.reshape(31, NCH, LANES), (1, 0, 2)), ((0, 0), (0, 1), (0, 0)))
    fdw = piece(7, (L, 3, fs))
    lb_soft = lb_softmax(hgrn_lb_logits, "lb_softmax")
    W = dict(
        whin=whin, lb=lb_soft[0:1], gg=hgrn_gnorm_g, cb_in=piece(1, (nci,)).reshape(1, N_DEV * nci), cw=cw,
        cdw_b=piece(3, (nd,)).reshape(1, D), ln_g=piece(4, (nd,)).reshape(1, D), ln_b=piece(5, (nd,)).reshape(1, D),
        cb_out=piece(6, (nd,)).reshape(1, D), fdw_w=[fdw[:, l] for l in range(L)],
        fdw_b=[ffn_dw_b[l].reshape(N_DEV, 1, fs) for l in range(L)])

    bias_mine = lax.dynamic_slice(ada_b, (0, me * na), (L, na)).reshape(L, 1, na)
    mod_cols = ada_fwd(c_all, ada_w, bias_mine, "ada_fwd")
    (mod_all,) = all_gather([mod_cols], "gather_mod")
    mod = jnp.transpose(lax.dynamic_index_in_dim(mod_all, me, axis=2, keepdims=False), (1, 0, 2)).reshape(L, N_DEV * na)

    gains = dict(pre_mix_g=pre_mix_g, post_mix_g=post_mix_g, pre_ffn_g=pre_ffn_g, post_ffn_g=post_ffn_g)
    loss_part, dx0, big, small = _local_step(x[0], loss_target[0], mod, gains, W, mine)
    loss = lax.psum((0.5 / D) * jnp.sum(loss_part), ("x", "y", "c"))

    parts = [small[k] for k in SMALL_ORDER]
    (sm_all,) = all_gather([_pack(parts)], "gather_small_grads")
    tot = _unpack(sum_slots(sm_all, "sum_small_grads").reshape(-1), [p.shape for p in parts])
    tot = dict(zip(SMALL_ORDER, tot))
    dmod_all = sm_all.reshape(N_DEV, -1)[:, :L * 6 * D].reshape(N_DEV, L, 6 * D)
    dm_mine = jnp.transpose(lax.dynamic_slice(dmod_all, (0, 0, me * na), (N_DEV, L, na)), (1, 0, 2))
    s0 = lb_soft[0:1]
    onehot = (lax.broadcasted_iota(jnp.int32, lb_soft.shape, 0) == 0).astype(F32)
    col = lambda a, n: lax.dynamic_slice_in_dim(a, me * n, n, axis=a.ndim - 1)
    G = {
        'ada_w': ada_bwd(c_all, dm_mine, "ada_bwd")[None],
        'ada_b': tot["dmod"].reshape(1, L, 6 * D),
        'pre_mix_g': tot["pre_mix_g"][None], 'post_mix_g': tot["post_mix_g"][None],
        'pre_ffn_g': tot["pre_ffn_g"][None], 'post_ffn_g': tot["post_ffn_g"][None],
        'hgrn_lb_logits': (tot["lb"] * s0 * (onehot - lb_soft))[None],
        'hgrn_gnorm_g': tot["gg"][None],
        'ffn_dw_b': tot["ffn_dw_b"].reshape(1, L, N_DEV * fs),
        'conv_b_in': col(tot["conv_b_in"], nci)[None],
        'conv_dw_w': col(tot["conv_dw_w"], nd)[None, None],
        'conv_dw_b': col(tot["conv_dw_b"], nd)[None], 'conv_ln_g': col(tot["conv_ln_g"], nd)[None],
        'conv_ln_b': col(tot["conv_ln_b"], nd)[None], 'conv_b_out': col(tot["conv_b_out"], nd)[None],
        'ffn_dw_w': lax.dynamic_index_in_dim(tot["ffn_dw_w"], me, axis=1, keepdims=False)[None],
    }

    G.update(big)
    res = {n: _apply_adamw(n, P[n], G[n], P["m_" + n], P["v_" + n]) for n in WEIGHTS}
    return (loss, dx0[None], *[res[n][0] for n in WEIGHTS], *[res[n][1] for n in WEIGHTS],
            *[res[n][2] for n in WEIGHTS], *[res[n][3] for n in WEIGHTS])
```

```python
import functools

import jax
import jax.numpy as jnp
from jax import lax
from jax.experimental import pallas as pl
from jax.experimental.pallas import tpu as pltpu

F32 = jnp.float32
BF16 = jnp.bfloat16
EPS = 1e-6
LANES = 128
N_DEV = 8
VMEM_LIMIT = 48 * 1024 * 1024
FFN_BWD_VMEM = 58 * 1024 * 1024
MM_ROWS = 1024
HG_CHUNK = 128
HG_SUB = 32
EXP_CLAMP = 80.0
CONV_HALO = 32
FFN_HALO = 8
CONV_ROWS = 32
ADAM_LR, ADAM_B1, ADAM_B2, ADAM_EPS, ADAM_WD, ADAM_STEP = 0.001, 0.9, 0.999, 1e-08, 0.01, 10
MESH = pl.DeviceIdType.MESH
ANY = pl.BlockSpec(memory_space=pl.ANY)


def _cp(*sem):
    return pltpu.CompilerParams(dimension_semantics=sem, vmem_limit_bytes=VMEM_LIMIT)


def _rt(n, t):
    t = min(n, t)
    assert n % t == 0, (n, t)
    return t


def _sig(x):
    return jax.nn.sigmoid(x)


def _nt(a, b):
    return lax.dot_general(a, b, (((1,), (1,)), ((), ())), preferred_element_type=F32)


def _tn(a, b):
    return lax.dot_general(a, b, (((0,), (0,)), ((), ())), preferred_element_type=F32)


def _nn(a, b):
    return jnp.dot(a, b, preferred_element_type=F32)


def _bf(x):
    return x.astype(BF16)


def _nn_st(a, b):
    return _nn(_bf(a), _bf(b))


def _tn_st(a, b):
    return _tn(_bf(a), _bf(b))


def _row(d):
    return pl.BlockSpec((1, d), lambda *_: (0, 0))


def prenorm_mod(x, g, sc, sh, name):
    T, D = x.shape
    tm = _rt(T, 512)

    def body(x_ref, g_ref, sc_ref, sh_ref, h_ref):
        xv = x_ref[...]
        r = lax.rsqrt(jnp.mean(xv * xv, axis=-1, keepdims=True) + EPS)
        h_ref[...] = _bf((xv * r) * g_ref[...] * (1.0 + sc_ref[...]) + sh_ref[...])

    tile = pl.BlockSpec((tm, D), lambda i: (i, 0))
    return pl.pallas_call(
        body, name=name, grid=(T // tm,), in_specs=[tile, _row(D), _row(D), _row(D)], out_specs=tile,
        out_shape=jax.ShapeDtypeStruct((T, D), BF16), compiler_params=_cp("parallel"))(x, g, sc, sh)


def resid_bwd(name, pre=None, loss=None, post=None):
    T, D = (pre[1] if pre is not None else loss[0]).shape
    tm = _rt(T, 512)
    n_src = 5 if pre is not None else 2
    n_in = n_src + (3 if post is not None else 0)

    def body(*refs):
        ins, outs = refs[:n_in], refs[n_in:]
        first = pl.program_id(0) == 0
        if pre is not None:
            dh_ref, x_ref, dr_ref, g_ref, sc_ref = ins[:5]
            dx_ref, da_ref, dsh_ref = outs[:3]

            @pl.when(first)
            def _():
                da_ref[...] = jnp.zeros_like(da_ref)
                dsh_ref[...] = jnp.zeros_like(dsh_ref)
            xv = x_ref[...]
            dh = dh_ref[...]
            r = lax.rsqrt(jnp.mean(xv * xv, axis=-1, keepdims=True) + EPS)
            n = xv * r
            da_ref[...] += jnp.sum(dh * n, axis=0, keepdims=True)
            dsh_ref[...] += jnp.sum(dh, axis=0, keepdims=True)
            dn = dh * (g_ref[...] * (1.0 + sc_ref[...]))
            dx = dr_ref[...] + r * (dn - n * jnp.mean(dn * n, axis=-1, keepdims=True))
            dx_ref[...] = dx
            rest = outs[3:]
        else:
            x_ref, t_ref = ins[:2]
            dx_ref, part_ref = outs[:2]

            @pl.when(first)
            def _():
                part_ref[...] = jnp.zeros_like(part_ref)
            e = x_ref[...] - t_ref[...]
            dx = e * (1.0 / D)
            dx_ref[...] = dx
            e2 = (e * e).reshape(tm // 8, 8, D).sum(axis=0)
            acc = e2[:, 0:LANES]
            for j in range(1, D // LANES):
                acc = acc + e2[:, j * LANES:(j + 1) * LANES]
            part_ref[...] += acc
            rest = outs[2:]
        if post is not None:
            y_ref, gt_ref, pg_ref = ins[n_src:]
            dy_ref, dgate_ref, dpg_ref, dys_ref = rest

            @pl.when(first)
            def _():
                dgate_ref[...] = jnp.zeros_like(dgate_ref)
                dpg_ref[...] = jnp.zeros_like(dpg_ref)
                dys_ref[...] = jnp.zeros_like(dys_ref)
            yv = y_ref[...]
            ry = lax.rsqrt(jnp.mean(yv * yv, axis=-1, keepdims=True) + EPS)
            ny = yv * ry
            s = jnp.sum(dx * ny, axis=0, keepdims=True)
            dgate_ref[...] += s * pg_ref[...]
            dpg_ref[...] += s * gt_ref[...]
            dny = dx * (gt_ref[...] * pg_ref[...])
            dy = ry * (dny - ny * jnp.mean(dny * ny, axis=-1, keepdims=True))
            dy_ref[...] = _bf(dy)
            dys_ref[...] += jnp.sum(dy, axis=0, keepdims=True)

    tile = pl.BlockSpec((tm, D), lambda i: (i, 0))
    vec = jax.ShapeDtypeStruct((1, D), F32)
    big = jax.ShapeDtypeStruct((T, D), F32)
    if pre is not None:
        operands, in_specs = list(pre), [tile, tile, tile, _row(D), _row(D)]
        out_specs, out_shape = [tile, _row(D), _row(D)], [big, vec, vec]
    else:
        operands, in_specs = list(loss), [tile, tile]
        out_specs, out_shape = [tile, pl.BlockSpec((8, LANES), lambda i: (0, 0))], [big, jax.ShapeDtypeStruct((8, LANES), F32)]
    if post is not None:
        operands, in_specs = operands + list(post), in_specs + [tile, _row(D), _row(D)]
        out_specs = out_specs + [tile, _row(D), _row(D), _row(D)]
        out_shape = out_shape + [jax.ShapeDtypeStruct((T, D), BF16), vec, vec, vec]
    return pl.pallas_call(body, name=name, grid=(T // tm,), in_specs=in_specs, out_specs=out_specs, out_shape=out_shape,
                          compiler_params=_cp("arbitrary"))(*operands)


def _call(body, name, grid, in_specs, out_specs, out_shape, scratch, sems, operands, ex=None):
    if ex is not None:
        def first():
            return functools.reduce(lambda p, q: p & q, [pl.program_id(k) == 0 for k in range(len(grid))])

        def last():
            return functools.reduce(lambda p, q: p & q, [pl.program_id(k) == grid[k] - 1 for k in range(len(grid))])
        body = _ride(body, len(in_specs), len(out_specs), ex, first, last)
        in_specs = in_specs + [ANY] * len(ex.ins)
        out_specs = out_specs + [ANY] * len(ex.out_shapes)
        out_shape = out_shape + ex.out_shapes
        scratch = scratch + ex.scratch
        operands = list(operands) + ex.ins
        sems = ("arbitrary",) * len(grid)
    res = pl.pallas_call(body, name=name, grid=grid, in_specs=in_specs, out_specs=out_specs, out_shape=out_shape,
                         scratch_shapes=scratch, compiler_params=_cp(*sems))(*operands)
    n_own = len(res) - (len(ex.out_shapes) if ex is not None else 0)
    return res[:n_own], res[n_own:]


def mm_nn(a, bg, bias, name, ex=None):
    M, K = a.shape
    G, _, n = bg.shape
    tm = _rt(M, MM_ROWS)

    def body(a_ref, b_ref, bias_ref, o_ref):
        o_ref[...] = _nn(a_ref[...], b_ref[...]) + bias_ref[...]

    (out,), rider = _call(
        body, name, (M // tm, G),
        [pl.BlockSpec((tm, K), lambda i, j: (i, 0)), pl.BlockSpec((None, K, n), lambda i, j: (j, 0, 0)),
         pl.BlockSpec((1, n), lambda i, j: (0, j))],
        [pl.BlockSpec((tm, n), lambda i, j: (i, j))], [jax.ShapeDtypeStruct((M, G * n), F32)], [],
        ("parallel", "arbitrary"), (a, bg, bias), ex)
    return (out, rider) if ex is not None else out


def _next_input(xo, g_ref, sc_ref, sh_ref):
    r = lax.rsqrt(jnp.mean(xo * xo, axis=-1, keepdims=True) + EPS)
    return _bf((xo * r) * g_ref[...] * (1.0 + sc_ref[...]) + sh_ref[...])


def mm_post(a, b, bias, x, gate, pg, nxt, name):
    T, K = a.shape
    D = b.shape[1]
    tm = _rt(T, 512)

    def body(a_ref, b_ref, bias_ref, x_ref, gt_ref, pg_ref, g_ref, sc_ref, sh_ref, y_ref, xo_ref, h_ref):
        y = _nn(a_ref[...], b_ref[...]) + bias_ref[...]
        y_ref[...] = y
        r = lax.rsqrt(jnp.mean(y * y, axis=-1, keepdims=True) + EPS)
        xo = x_ref[...] + gt_ref[...] * ((y * r) * pg_ref[...])
        xo_ref[...] = xo
        h_ref[...] = _next_input(xo, g_ref, sc_ref, sh_ref)

    tile = pl.BlockSpec((tm, D), lambda i: (i, 0))
    out = jax.ShapeDtypeStruct((T, D), F32)
    return pl.pallas_call(
        body, name=name, grid=(T // tm,),
        in_specs=[pl.BlockSpec((tm, K), lambda i: (i, 0)), pl.BlockSpec((K, D), lambda i: (0, 0)), _row(D), tile]
        + [_row(D)] * 5,
        out_specs=[tile, tile, tile], out_shape=[out, out, jax.ShapeDtypeStruct((T, D), BF16)],
        compiler_params=_cp("parallel"))(a, b, bias, x, gate, pg, *nxt)


def mm_nt(dy, wg, name, natural, ex=None):
    G, K, n = wg.shape
    T = dy.shape[0] if natural else dy.shape[1]
    tm = _rt(T, MM_ROWS)

    def body(dy_ref, w_ref, o_ref, acc):
        j = pl.program_id(1)

        @pl.when(j == 0)
        def _():
            acc[...] = jnp.zeros_like(acc)
        acc[...] += _nt(_bf(dy_ref[...]), w_ref[...])

        @pl.when(j == G - 1)
        def _():
            o_ref[...] = acc[...]

    dspec = (pl.BlockSpec((tm, n), lambda i, j: (i, j)) if natural
             else pl.BlockSpec((None, tm, n), lambda i, j: (j, i, 0)))
    (out,), rider = _call(
        body, name, (T // tm, G), [dspec, pl.BlockSpec((None, K, n), lambda i, j: (j, 0, 0))],
        [pl.BlockSpec((tm, K), lambda i, j: (i, 0))], [jax.ShapeDtypeStruct((T, K), F32)],
        [pltpu.VMEM((tm, K), F32)], ("parallel", "arbitrary"), (dy, wg), ex)
    return (out, rider) if ex is not None else out


def mm_tn(xm, gm, name, x_natural, g_natural, kx, n):
    if x_natural:
        T, Gx = xm.shape[0], xm.shape[1] // kx
    else:
        Gx, T = xm.shape[0], xm.shape[1]
    tt = _rt(T, MM_ROWS)
    if x_natural:
        xspec = pl.BlockSpec((tt, kx), lambda g, k, t: (t, k))
    else:
        xspec = pl.BlockSpec((None, tt, kx), lambda g, k, t: (k, t, 0))
    if g_natural:
        Gg = gm.shape[1] // n
        gspec = pl.BlockSpec((tt, n), lambda g, k, t: (t, g))
    else:
        Gg = gm.shape[0]
        gspec = pl.BlockSpec((None, tt, n), lambda g, k, t: (g, t, 0))

    NT = T // tt

    def body(x_ref, g_ref, o_ref, acc):
        t = pl.program_id(2)

        @pl.when(t == 0)
        def _():
            acc[...] = jnp.zeros_like(acc)
        acc[...] += _tn(_bf(x_ref[...]), _bf(g_ref[...]))

        @pl.when(t == NT - 1)
        def _():
            o_ref[...] = acc[...].astype(o_ref.dtype)

    return pl.pallas_call(
        body, name=name, grid=(Gg, Gx, NT), in_specs=[xspec, gspec],
        out_specs=pl.BlockSpec((None, kx, n), lambda g, k, t: (g, k, 0)),
        out_shape=jax.ShapeDtypeStruct((Gg, Gx * kx, n), BF16), scratch_shapes=[pltpu.VMEM((kx, n), F32)],
        compiler_params=_cp("parallel", "parallel", "arbitrary"))(xm, gm)


def _split3(x):
    h = _bf(x)
    r = x - h.astype(F32)
    m = _bf(r)
    lo = _bf(r - m.astype(F32))
    return h, m, lo


def _tri_mm(tri, x):
    h, m, lo = _split3(x)
    return _nn(tri, lo) + _nn(tri, m) + _nn(tri, h)


def _hgrn_gates(qr, fz, lb):
    sq = _sig(qr)
    q = qr * sq
    sig = _sig(fz)
    f = lb + (1.0 - lb) * sig
    k = 1.0 - f
    return sq, q, sig, f, k, jnp.log(f)


def hgrn_fwd(proj, lb, gg, name, ex=None):
    T, D4 = proj.shape
    D = D4 // 4
    H = D // LANES
    C = _rt(T, HG_CHUNK)
    SB = min(HG_SUB, C)
    NC = T // C

    def body(q_ref, f_ref, v_ref, g_ref, lb_ref, gg_ref, o_ref, og_ref, st_ref, ST, bex):
        @pl.when(pl.program_id(0) == 0)
        def _():
            ST[...] = jnp.zeros_like(ST)
        r_i = lax.broadcasted_iota(jnp.int32, (C, C), 0)
        c_i = lax.broadcasted_iota(jnp.int32, (C, C), 1)
        low = _bf((r_i >= c_i).astype(F32))
        rs = lax.broadcasted_iota(jnp.int32, (SB, C), 0)
        cs = lax.broadcasted_iota(jnp.int32, (SB, C), 1)
        for hd in range(H):
            sl = slice(hd * LANES, (hd + 1) * LANES)
            _, q, _, _, k, lf = _hgrn_gates(q_ref[:, sl], f_ref[:, sl], lb_ref[:, sl])
            v = v_ref[:, sl]
            b = _tri_mm(low, lf)
            bex[hd] = b - lf
            bend = jnp.sum(lf, axis=0, keepdims=True)
            blocks = []
            for I in range(C // SB):
                p = bex[hd, pl.ds(SB * I + SB // 2, 1), :]
                rows = slice(SB * I, SB * (I + 1))
                qI = _bf(q[rows] * jnp.exp(jnp.minimum(b[rows] - p, EXP_CLAMP)))
                kI = _bf(k * jnp.exp(jnp.minimum(p - b, EXP_CLAMP)))
                blocks.append(jnp.where(rs + SB * I >= cs, _nt(qI, kI), 0.0))
            A = jnp.concatenate(blocks, axis=0)
            st0 = ST[hd]
            st_ref[0, hd] = st0
            o = _nn(_bf(A), _bf(v)) + _nt(_bf(q * jnp.exp(b)), _bf(st0))
            ST[hd] = st0 * jnp.exp(bend) + _tn_st(v, k * jnp.exp(bend - b))
            o_ref[:, sl] = o
            r = lax.rsqrt(jnp.mean(o * o, axis=-1, keepdims=True) + EPS)
            gr = g_ref[:, sl]
            og_ref[:, sl] = _bf((o * r) * gg_ref[...] * (gr * _sig(gr)))

    col = lambda j: pl.BlockSpec((C, D), lambda c, j=j: (c, j))
    tile = pl.BlockSpec((C, D), lambda c: (c, 0))
    own, rider = _call(
        body, name, (NC,), [col(0), col(1), col(2), col(3), _row(D), _row(LANES)],
        [tile, tile, pl.BlockSpec((1, H, LANES, LANES), lambda c: (c, 0, 0, 0))],
        [jax.ShapeDtypeStruct((T, D), F32), jax.ShapeDtypeStruct((T, D), BF16),
         jax.ShapeDtypeStruct((NC, H, LANES, LANES), F32)],
        [pltpu.VMEM((H, LANES, LANES), F32), pltpu.VMEM((H, C, LANES), F32)], ("arbitrary",),
        (proj, proj, proj, proj, lb, gg), ex)
    return (*own, rider) if ex is not None else own


def hgrn_bwd(proj, o, dog, st, lb, gg, name, ex=None):
    T, D4 = proj.shape
    D = D4 // 4
    H = D // LANES
    C = _rt(T, HG_CHUNK)
    SB = min(HG_SUB, C)
    NC = T // C

    def body(q_ref, f_ref, v_ref, g_ref, o_ref, dog_ref, st_ref, lb_ref, gg_ref, dp_ref, dlb_ref, dgg_ref,
             DST, bex):
        @pl.when(pl.program_id(0) == 0)
        def _():
            DST[...] = jnp.zeros_like(DST)
            dlb_ref[...] = jnp.zeros_like(dlb_ref)
            dgg_ref[...] = jnp.zeros_like(dgg_ref)
        r_i = lax.broadcasted_iota(jnp.int32, (C, C), 0)
        c_i = lax.broadcasted_iota(jnp.int32, (C, C), 1)
        causal = r_i >= c_i
        low = _bf(causal.astype(F32))
        upp = _bf((r_i <= c_i).astype(F32))
        rs = lax.broadcasted_iota(jnp.int32, (SB, C), 0)
        cs = lax.broadcasted_iota(jnp.int32, (SB, C), 1)
        ggv = gg_ref[...]
        for hd in range(H):
            sl = slice(hd * LANES, (hd + 1) * LANES)
            qr = q_ref[:, sl]
            lbv = lb_ref[:, sl]
            sq, q, sig, f, k, lf = _hgrn_gates(qr, f_ref[:, sl], lbv)
            v = v_ref[:, sl]
            oh = o_ref[:, sl]
            r = lax.rsqrt(jnp.mean(oh * oh, axis=-1, keepdims=True) + EPS)
            n = oh * r
            gr = g_ref[:, sl]
            sg = _sig(gr)
            dgo = dog_ref[:, sl]
            dgr = dgo * (n * ggv) * (sg * (1.0 + gr * (1.0 - sg)))
            don = dgo * (gr * sg)
            dgg_ref[...] += jnp.sum(don * n, axis=0, keepdims=True)
            dn = don * ggv
            do = r * (dn - n * jnp.mean(dn * n, axis=-1, keepdims=True))
            b = _tri_mm(low, lf)
            bex[hd] = b - lf
            bend = jnp.sum(lf, axis=0, keepdims=True)
            dob = _bf(do)
            dA = _bf(jnp.where(causal, _nt(dob, _bf(v)), 0.0))
            blocks, dqs, gqs = [], [], []
            dk = jnp.zeros((C, LANES), F32)
            gk = jnp.zeros((C, LANES), F32)
            for I in range(C // SB):
                p = bex[hd, pl.ds(SB * I + SB // 2, 1), :]
                rows = slice(SB * I, SB * (I + 1))
                eq = jnp.exp(jnp.minimum(b[rows] - p, EXP_CLAMP))
                qI = _bf(q[rows] * eq)
                ek = jnp.exp(jnp.minimum(p - b, EXP_CLAMP))
                kI = _bf(k * ek)
                blocks.append(jnp.where(rs + SB * I >= cs, _nt(qI, kI), 0.0))
                dqI = _nn(dA[rows], kI)
                dkI = _tn(dA[rows], qI)
                dqs.append(dqI * eq)
                gqs.append(qI.astype(F32) * dqI)
                dk = dk + dkI * ek
                gk = gk + kI.astype(F32) * dkI
            A = jnp.concatenate(blocks, axis=0)
            dst = DST[hd]
            st0 = st_ref[0, hd]
            eb = jnp.exp(b)
            ee = jnp.exp(bend - b)
            ebend = jnp.exp(bend)
            dv = _tn(_bf(A), dob) + _nt(_bf(k * ee), _bf(dst))
            dk_state = ee * _nn_st(v, dst)
            dq_state = eb * _nn_st(do, st0)
            dq = jnp.concatenate(dqs, axis=0) + dq_state
            dk = dk + dk_state
            DST[hd] = dst * ebend + _tn_st(do, q * eb)
            later = jnp.sum(k * dk_state, axis=0, keepdims=True) + ebend * jnp.sum(st0 * dst, axis=0, keepdims=True)
            pairs = jnp.concatenate(gqs, axis=0) - gk + (q * dq_state - k * dk_state)
            dlf = _tri_mm(upp, pairs) + later
            df = dlf / f - dk
            dlb_ref[:, sl] += jnp.sum(df * (1.0 - sig), axis=0, keepdims=True)
            dp_ref[:, sl] = _bf(dq * (sq * (1.0 + qr * (1.0 - sq))))
            dp_ref[:, D + hd * LANES:D + (hd + 1) * LANES] = _bf(df * (1.0 - lbv) * (sig * (1.0 - sig)))
            dp_ref[:, 2 * D + hd * LANES:2 * D + (hd + 1) * LANES] = _bf(dv)
            dp_ref[:, 3 * D + hd * LANES:3 * D + (hd + 1) * LANES] = _bf(dgr)

    col = lambda j: pl.BlockSpec((C, D), lambda c, j=j: (NC - 1 - c, j))
    tile = pl.BlockSpec((C, D), lambda c: (NC - 1 - c, 0))
    own, rider = _call(
        body, name, (NC,),
        [col(0), col(1), col(2), col(3), tile, tile,
         pl.BlockSpec((1, H, LANES, LANES), lambda c: (NC - 1 - c, 0, 0, 0)), _row(D), _row(LANES)],
        [pl.BlockSpec((C, D4), lambda c: (NC - 1 - c, 0)), _row(D), _row(LANES)],
        [jax.ShapeDtypeStruct((T, D4), BF16), jax.ShapeDtypeStruct((1, D), F32), jax.ShapeDtypeStruct((1, LANES), F32)],
        [pltpu.VMEM((H, LANES, LANES), F32), pltpu.VMEM((H, C, LANES), F32)],
        ("arbitrary",), (proj, proj, proj, proj, o, dog, st, lb, gg), ex)
    return (*own, rider) if ex is not None else own


def _prev_blk(tm, hb):
    return lambda i: (jnp.maximum(i * (tm // hb) - 1, 0), 0)


def _next_blk(tm, hb, T):
    return lambda i: (jnp.minimum((i + 1) * (tm // hb), T // hb - 1), 0)


def _glu_to_ext(uc_ref, ucp_ref, ext, D, first):
    def glu(u):
        return u[:, :D] * _sig(u[:, D:])
    gh = jnp.where(first, 0.0, glu(ucp_ref[...]))
    gm = glu(uc_ref[...])
    for c in range(D // LANES):
        ext[c, 0:CONV_HALO, :] = gh[:, c * LANES:(c + 1) * LANES]
        ext[c, CONV_HALO:, :] = gm[:, c * LANES:(c + 1) * LANES]


def conv_mid_fwd(uc, cw, cb, lg, lbias, name):
    T, D2 = uc.shape
    D = D2 // 2
    NCH = D // LANES
    W = 31
    tm = _rt(T, 256)
    RS = min(CONV_ROWS, tm)

    def body(uc_ref, ucp_ref, cw_ref, cb_ref, lg_ref, lb_ref, cv_ref, va_ref, ext, outs):
        _glu_to_ext(uc_ref, ucp_ref, ext, D, pl.program_id(0) == 0)

        def chunk(c, carry):
            for rb in range(tm // RS):
                acc = jnp.zeros((RS, LANES), F32)
                for kk in range(W):
                    acc = acc + cw_ref[c, pl.ds(kk, 1), :] * ext[c, pl.ds(rb * RS + CONV_HALO - (W - 1) + kk, RS), :]
                outs[c, pl.ds(rb * RS, RS), :] = acc
            return carry
        lax.fori_loop(0, NCH, chunk, 0)
        cv = jnp.concatenate([outs[c] for c in range(NCH)], axis=1) + cb_ref[...]
        cv_ref[...] = cv
        mu = jnp.mean(cv, axis=-1, keepdims=True)
        xc = cv - mu
        rstd = lax.rsqrt(jnp.mean(xc * xc, axis=-1, keepdims=True) + EPS)
        l = xc * rstd * lg_ref[...] + lb_ref[...]
        va_ref[...] = _bf(l * _sig(l))

    tile = pl.BlockSpec((tm, D), lambda i: (i, 0))
    return pl.pallas_call(
        body, name=name, grid=(T // tm,),
        in_specs=[pl.BlockSpec((tm, D2), lambda i: (i, 0)), pl.BlockSpec((CONV_HALO, D2), _prev_blk(tm, CONV_HALO)),
                  pl.BlockSpec((NCH, 32, LANES), lambda i: (0, 0, 0)), _row(D), _row(D), _row(D)],
        out_specs=[tile, tile],
        out_shape=[jax.ShapeDtypeStruct((T, D), F32), jax.ShapeDtypeStruct((T, D), BF16)],
        scratch_shapes=[pltpu.VMEM((NCH, tm + CONV_HALO, LANES), F32), pltpu.VMEM((NCH, tm, LANES), F32)],
        compiler_params=_cp("parallel"))(uc, uc, cw, cb, lg, lbias)


def conv_bwd_a(dva, cv, uc, lg, lbias, name):
    T, D2 = uc.shape
    D = D2 // 2
    NCH = D // LANES
    W = 31
    tm = _rt(T, 256)
    RS = min(CONV_ROWS, tm)

    def body(dva_ref, cv_ref, uc_ref, ucp_ref, lg_ref, lb_ref, dcv_ref, dlg_ref, dlb_ref, taps_ref, ext, dcs):
        @pl.when(pl.program_id(0) == 0)
        def _():
            dlg_ref[...] = jnp.zeros_like(dlg_ref)
            dlb_ref[...] = jnp.zeros_like(dlb_ref)
            taps_ref[...] = jnp.zeros_like(taps_ref)
        _glu_to_ext(uc_ref, ucp_ref, ext, D, pl.program_id(0) == 0)
        cv = cv_ref[...]
        mu = jnp.mean(cv, axis=-1, keepdims=True)
        xc = cv - mu
        rstd = lax.rsqrt(jnp.mean(xc * xc, axis=-1, keepdims=True) + EPS)
        yn = xc * rstd
        l = yn * lg_ref[...] + lb_ref[...]
        s = _sig(l)
        dl = dva_ref[...] * (s * (1.0 + l * (1.0 - s)))
        dlg_ref[...] += jnp.sum(dl * yn, axis=0, keepdims=True)
        dlb_ref[...] += jnp.sum(dl, axis=0, keepdims=True)
        dyn = dl * lg_ref[...]
        dcv = rstd * (dyn - jnp.mean(dyn, axis=-1, keepdims=True) - yn * jnp.mean(dyn * yn, axis=-1, keepdims=True))
        dcv_ref[...] = dcv
        for c in range(NCH):
            dcs[c] = dcv[:, c * LANES:(c + 1) * LANES]

        def fold(p):
            return p.reshape(RS // 8, 8, LANES).sum(axis=0)

        def chunk(c, carry):
            for kk in range(W):
                tot = jnp.zeros((8, LANES), F32)
                for rb in range(tm // RS):
                    tot = tot + fold(dcs[c, pl.ds(rb * RS, RS), :]
                                     * ext[c, pl.ds(rb * RS + CONV_HALO - (W - 1) + kk, RS), :])
                taps_ref[c, pl.ds(8 * kk, 8), :] += tot
            tot = jnp.zeros((8, LANES), F32)
            for rb in range(tm // RS):
                tot = tot + fold(dcs[c, pl.ds(rb * RS, RS), :])
            taps_ref[c, pl.ds(8 * W, 8), :] += tot
            return carry
        lax.fori_loop(0, NCH, chunk, 0)

    tile = pl.BlockSpec((tm, D), lambda i: (i, 0))
    vec = jax.ShapeDtypeStruct((1, D), F32)
    return pl.pallas_call(
        body, name=name, grid=(T // tm,),
        in_specs=[tile, tile, pl.BlockSpec((tm, D2), lambda i: (i, 0)),
                  pl.BlockSpec((CONV_HALO, D2), _prev_blk(tm, CONV_HALO)), _row(D), _row(D)],
        out_specs=[tile, _row(D), _row(D), pl.BlockSpec((NCH, 256, LANES), lambda i: (0, 0, 0))],
        out_shape=[jax.ShapeDtypeStruct((T, D), F32), vec, vec, jax.ShapeDtypeStruct((NCH, 256, LANES), F32)],
        scratch_shapes=[pltpu.VMEM((NCH, tm + CONV_HALO, LANES), F32), pltpu.VMEM((NCH, tm, LANES), F32)],
        compiler_params=_cp("arbitrary"))(dva, cv, uc, uc, lg, lbias)


def conv_bwd_b(dcv, uc, cw, name):
    T, D2 = uc.shape
    D = D2 // 2
    NCH = D // LANES
    W = 31
    tm = _rt(T, 256)
    RS = min(CONV_ROWS, tm)
    NT = T // tm

    def body(dcv_ref, dcn_ref, uc_ref, cw_ref, du_ref, db_ref, ext, outs):
        i = pl.program_id(0)

        @pl.when(i == 0)
        def _():
            db_ref[...] = jnp.zeros_like(db_ref)
        dm = dcv_ref[...]
        dn = jnp.where(i == NT - 1, 0.0, dcn_ref[...])
        for c in range(NCH):
            ext[c, 0:tm, :] = dm[:, c * LANES:(c + 1) * LANES]
            ext[c, tm:, :] = dn[:, c * LANES:(c + 1) * LANES]

        def chunk(c, carry):
            for rb in range(tm // RS):
                acc = jnp.zeros((RS, LANES), F32)
                for kk in range(W):
                    acc = acc + cw_ref[c, pl.ds(kk, 1), :] * ext[c, pl.ds(rb * RS + (W - 1) - kk, RS), :]
                outs[c, pl.ds(rb * RS, RS), :] = acc
            return carry
        lax.fori_loop(0, NCH, chunk, 0)
        dglu = jnp.concatenate([outs[c] for c in range(NCH)], axis=1)
        u = uc_ref[...]
        a = u[:, :D]
        sg = _sig(u[:, D:])
        da = dglu * sg
        dg = dglu * a * (sg * (1.0 - sg))
        du_ref[:, :D] = _bf(da)
        du_ref[:, D:] = _bf(dg)
        db_ref[:, :D] += jnp.sum(da, axis=0, keepdims=True)
        db_ref[:, D:] += jnp.sum(dg, axis=0, keepdims=True)

    tile = pl.BlockSpec((tm, D), lambda i: (i, 0))
    wide = pl.BlockSpec((tm, D2), lambda i: (i, 0))
    return pl.pallas_call(
        body, name=name, grid=(NT,),
        in_specs=[tile, pl.BlockSpec((CONV_HALO, D), _next_blk(tm, CONV_HALO, T)), wide,
                  pl.BlockSpec((NCH, 32, LANES), lambda i: (0, 0, 0))],
        out_specs=[wide, _row(D2)],
        out_shape=[jax.ShapeDtypeStruct((T, D2), BF16), jax.ShapeDtypeStruct((1, D2), F32)],
        scratch_shapes=[pltpu.VMEM((NCH, tm + CONV_HALO, LANES), F32), pltpu.VMEM((NCH, tm, LANES), F32)],
        compiler_params=_cp("arbitrary"))(dcv, dcv, uc, cw)


def _shift_rows(x, halo, k, back):
    rows, n = x.shape
    x3, h3 = x.reshape(rows // 8, 8, n), halo.reshape(1, 8, n)
    sub = lax.broadcasted_iota(jnp.int32, (1, 8, n), 1)
    if back:
        r = pltpu.roll(jnp.concatenate([h3, x3], axis=0), k, 1)
        y = jnp.where(sub < k, r[:-1], r[1:])
    else:
        r = pltpu.roll(jnp.concatenate([x3, h3], axis=0), 8 - k, 1)
        y = jnp.where(sub < 8 - k, r[:-1], r[1:])
    return y.reshape(rows, n)


def _conv3(u, uh, dw_ref, bias_ref):
    return (dw_ref[pl.ds(0, 1), :] * _shift_rows(u, uh, 2, True) + dw_ref[pl.ds(1, 1), :] * _shift_rows(u, uh, 1, True)
            + dw_ref[pl.ds(2, 1), :] * u + bias_ref[...])


def ffn_fwd(h, wup, dww, dwb, wdn, x, gate, pg, nxt, name):
    T, D = h.shape
    fs = wup.shape[2]
    NJ = wup.shape[0] // 2
    tm = _rt(T, 512)
    n_nxt = 0 if nxt is None else 3

    def body(h_ref, hp_ref, wa_ref, wb_ref, dwa_ref, dwb_ref, ba_ref, bb_ref, wd_ref, x_ref, gt_ref, pg_ref, *rest):
        nx, (u_ref, uc_ref, y_ref, xo_ref), more = rest[:n_nxt], rest[n_nxt:n_nxt + 4], rest[n_nxt + 4:]
        acc = more[-1]
        i, j = pl.program_id(0), pl.program_id(1)

        @pl.when(j == 0)
        def _():
            acc[...] = jnp.zeros_like(acc)
        hb = h_ref[...]
        hp = hp_ref[...]

        def half(s, w_ref, dw_ref, b_ref):
            u = _nn(hb, w_ref[...])
            uh = jnp.where(i == 0, 0.0, _nn(hp, w_ref[...]))
            u_ref[s] = u
            uc = _conv3(u, uh, dw_ref, b_ref)
            uc_ref[s] = uc
            return uc
        a = half(0, wa_ref, dwa_ref, ba_ref)
        b = half(1, wb_ref, dwb_ref, bb_ref)
        acc[...] += _nn(_bf(a * _sig(a) * b), wd_ref[...])

        @pl.when(j == NJ - 1)
        def _():
            y = acc[...]
            y_ref[...] = y
            r = lax.rsqrt(jnp.mean(y * y, axis=-1, keepdims=True) + EPS)
            xo = x_ref[...] + gt_ref[...] * ((y * r) * pg_ref[...])
            xo_ref[...] = xo
            if nxt is not None:
                more[0][...] = _next_input(xo, *nx)

    tile = pl.BlockSpec((tm, D), lambda i, j: (i, 0))
    shard = lambda off, r: pl.BlockSpec((None, r, fs), lambda i, j: (j + off, 0, 0))
    ush = pl.BlockSpec((2, None, tm, fs), lambda i, j: (0, j, i, 0))
    vec = pl.BlockSpec((1, D), lambda i, j: (0, 0))
    return pl.pallas_call(
        body, name=name, grid=(T // tm, NJ),
        in_specs=[tile, pl.BlockSpec((FFN_HALO, D), lambda i, j: (jnp.maximum(i * (tm // FFN_HALO) - 1, 0), 0)),
                  shard(0, D), shard(NJ, D), shard(0, 3), shard(NJ, 3), shard(0, 1), shard(NJ, 1),
                  pl.BlockSpec((None, fs, D), lambda i, j: (j, 0, 0)), tile, vec, vec] + [vec] * n_nxt,
        out_specs=[ush, ush, tile, tile] + [tile] * (n_nxt > 0),
        out_shape=[jax.ShapeDtypeStruct((2, NJ, T, fs), F32), jax.ShapeDtypeStruct((2, NJ, T, fs), F32),
                   jax.ShapeDtypeStruct((T, D), F32), jax.ShapeDtypeStruct((T, D), F32)]
        + [jax.ShapeDtypeStruct((T, D), BF16)] * (n_nxt > 0),
        scratch_shapes=[pltpu.VMEM((tm, D), F32)],
        compiler_params=_cp("parallel", "arbitrary"))(h, h, wup, wup, dww, dww, dwb, dwb, wdn, x, gate, pg,
                                                      *(nxt or ()))


def ffn_bwd_a(dy, uc, wdn, name):
    _, NJ, T, fs = uc.shape
    D = dy.shape[1]
    tm = _rt(T, 512)

    NT = T // tm

    def body(dy_ref, uc_ref, wd_ref, duc_ref, dwd_ref, acc):
        i = pl.program_id(1)

        @pl.when(i == 0)
        def _():
            acc[...] = jnp.zeros_like(acc)
        dyb = _bf(dy_ref[...])
        dz = _nt(dyb, wd_ref[...])
        a = uc_ref[0]
        b = uc_ref[1]
        sa = _sig(a)
        asa = a * sa
        acc[...] += _tn(_bf(asa * b), dyb)
        duc_ref[0] = dz * b * (sa + asa * (1.0 - sa))
        duc_ref[1] = dz * asa

        @pl.when(i == NT - 1)
        def _():
            dwd_ref[...] = _bf(acc[...])

    ush = pl.BlockSpec((2, None, tm, fs), lambda j, i: (0, j, i, 0))
    return pl.pallas_call(
        body, name=name, grid=(NJ, NT),
        in_specs=[pl.BlockSpec((tm, D), lambda j, i: (i, 0)), ush, pl.BlockSpec((None, fs, D), lambda j, i: (j, 0, 0))],
        out_specs=[ush, pl.BlockSpec((fs, D), lambda j, i: (j, 0))],
        out_shape=[jax.ShapeDtypeStruct((2, NJ, T, fs), F32), jax.ShapeDtypeStruct((NJ * fs, D), BF16)],
        scratch_shapes=[pltpu.VMEM((fs, D), F32)],
        compiler_params=_cp("parallel", "arbitrary"))(dy, uc, wdn)


def ffn_bwd_b(duc, u, h, dww, wup, name):
    NS, T, fs = duc.shape
    D = wup.shape[1]
    tm = _rt(T, 512)
    NT = T // tm

    def body(d_ref, dn_ref, u_ref, h_ref, dw_ref, w_ref, dh_ref, g_ref, dwup_ref, acc, accw, stage):
        i, s = pl.program_id(0), pl.program_id(1)

        @pl.when((i == 0) & (s == 0))
        def _():
            g_ref[...] = jnp.zeros_like(g_ref)

        @pl.when(s == 0)
        def _():
            acc[...] = jnp.zeros_like(acc)
        d = d_ref[...]
        dn = jnp.where(i == NT - 1, 0.0, dn_ref[...])
        d1 = _shift_rows(d, dn, 1, False)
        d2 = _shift_rows(d, dn, 2, False)
        du = _bf(dw_ref[pl.ds(2, 1), :] * d + dw_ref[pl.ds(1, 1), :] * d1 + dw_ref[pl.ds(0, 1), :] * d2)
        acc[...] += _nt(du, w_ref[...])
        dw_part = _tn(h_ref[...], du)

        @pl.when(i == 0)
        def _():
            accw[s] = dw_part

        @pl.when(i > 0)
        def _():
            accw[s] += dw_part

        @pl.when(i == NT - 1)
        def _():
            stage[...] = _bf(accw[s])
            pltpu.sync_copy(stage, dwup_ref.at[s])
        u = u_ref[...]
        g_ref[s, pl.ds(0, 1), :] += jnp.sum(d2 * u, axis=0, keepdims=True)
        g_ref[s, pl.ds(1, 1), :] += jnp.sum(d1 * u, axis=0, keepdims=True)
        g_ref[s, pl.ds(2, 1), :] += jnp.sum(d * u, axis=0, keepdims=True)
        g_ref[s, pl.ds(3, 1), :] += jnp.sum(d, axis=0, keepdims=True)

        @pl.when(s == NS - 1)
        def _():
            dh_ref[...] = acc[...]

    ush = pl.BlockSpec((None, tm, fs), lambda i, s: (s, i, 0))
    unext = pl.BlockSpec((None, FFN_HALO, fs),
                         lambda i, s: (s, jnp.minimum((i + 1) * (tm // FFN_HALO), T // FFN_HALO - 1), 0))
    tile = pl.BlockSpec((tm, D), lambda i, s: (i, 0))
    return pl.pallas_call(
        body, name=name, grid=(NT, NS),
        in_specs=[ush, unext, ush, tile, pl.BlockSpec((None, 3, fs), lambda i, s: (s, 0, 0)),
                  pl.BlockSpec((None, D, fs), lambda i, s: (s, 0, 0))],
        out_specs=[tile, pl.BlockSpec((NS, 8, fs), lambda i, s: (0, 0, 0)), ANY],
        out_shape=[jax.ShapeDtypeStruct((T, D), F32), jax.ShapeDtypeStruct((NS, 8, fs), F32),
                   jax.ShapeDtypeStruct((NS, D, fs), BF16)],
        scratch_shapes=[pltpu.VMEM((tm, D), F32), pltpu.VMEM((NS, D, fs), F32), pltpu.VMEM((D, fs), BF16)],
        compiler_params=pltpu.CompilerParams(dimension_semantics=("arbitrary", "arbitrary"),
                                             vmem_limit_bytes=FFN_BWD_VMEM))(duc, duc, u, h, dww, wup)


def ada_fwd(c_all, w, bias, name):
    L, D, n = w.shape

    def body(c_ref, w_ref, b_ref, o_ref):
        cv = c_ref[...]
        o_ref[...] = _nn(_bf(cv * _sig(cv)), _bf(w_ref[...])) + b_ref[...]

    return pl.pallas_call(
        body, name=name, grid=(L,),
        in_specs=[pl.BlockSpec((N_DEV, D), lambda l: (0, 0)), pl.BlockSpec((None, D, n), lambda l: (l, 0, 0)),
                  pl.BlockSpec((None, 1, n), lambda l: (l, 0, 0))],
        out_specs=pl.BlockSpec((None, N_DEV, n), lambda l: (l, 0, 0)),
        out_shape=jax.ShapeDtypeStruct((L, N_DEV, n), F32), compiler_params=_cp("parallel"))(c_all, w, bias)


def ada_bwd(c_all, dm, name):
    L, _, n = dm.shape
    D = c_all.shape[1]

    def body(c_ref, d_ref, o_ref):
        cv = c_ref[...]
        o_ref[...] = _tn(_bf(cv * _sig(cv)), _bf(d_ref[...]))

    return pl.pallas_call(
        body, name=name, grid=(L,),
        in_specs=[pl.BlockSpec((N_DEV, D), lambda l: (0, 0)), pl.BlockSpec((None, N_DEV, n), lambda l: (l, 0, 0))],
        out_specs=pl.BlockSpec((None, D, n), lambda l: (l, 0, 0)),
        out_shape=jax.ShapeDtypeStruct((L, D, n), F32), compiler_params=_cp("parallel"))(c_all, dm)


def sum_slots(g, name):
    S, R, C = g.shape

    def body(g_ref, o_ref):
        acc = g_ref[0]
        for s in range(1, S):
            acc = acc + g_ref[s]
        o_ref[...] = acc

    return pl.pallas_call(
        body, name=name, grid=(1,), in_specs=[pl.BlockSpec((S, R, C), lambda i: (0, 0, 0))],
        out_specs=pl.BlockSpec((R, C), lambda i: (0, 0)), out_shape=jax.ShapeDtypeStruct((R, C), F32),
        compiler_params=_cp("arbitrary"))(g)


def lb_softmax(logits, name):
    def body(l_ref, s_ref):
        l = l_ref[...]
        e = jnp.exp(l - jnp.max(l, axis=0, keepdims=True))
        s_ref[...] = e / jnp.sum(e, axis=0, keepdims=True)
    return pl.pallas_call(body, name=name, out_shape=jax.ShapeDtypeStruct(logits.shape, F32))(logits)


def adamw(w, gs, m, v, name):
    S, R, C = gs.shape
    tr = R
    budget = (4 * 1024 * 1024) // (4 * (S + 7) * C)
    if R > budget:
        tr = max(t for t in range(16, budget + 1, 16) if R % t == 0)

    def body(w_ref, g_ref, m_ref, v_ref, go_ref, d_ref, mo_ref, vo_ref):
        g = g_ref[0].astype(F32)
        for s in range(1, S):
            g = g + g_ref[s].astype(F32)
        go_ref[...] = g
        mn = ADAM_B1 * m_ref[...] + (1.0 - ADAM_B1) * g
        vn = ADAM_B2 * v_ref[...] + (1.0 - ADAM_B2) * (g * g)
        mo_ref[...] = mn
        vo_ref[...] = vn
        m_hat = mn / (1.0 - ADAM_B1 ** ADAM_STEP)
        v_hat = vn / (1.0 - ADAM_B2 ** ADAM_STEP)
        d_ref[...] = -ADAM_LR * (m_hat / (jnp.sqrt(v_hat) + ADAM_EPS) + ADAM_WD * w_ref[...])

    tile = pl.BlockSpec((tr, C), lambda i: (i, 0))
    out = jax.ShapeDtypeStruct((R, C), F32)
    return pl.pallas_call(
        body, name=name, grid=(R // tr,), in_specs=[tile, pl.BlockSpec((S, tr, C), lambda i: (0, i, 0)), tile, tile],
        out_specs=[tile] * 4, out_shape=[out] * 4, compiler_params=_cp("parallel"))(w, gs, m, v)


def _place():
    return lax.axis_index("x"), lax.axis_index("y"), lax.axis_index("c")


def _slot(p):
    return 4 * p[0] + 2 * p[1] + p[2]


class _Gather:
    def __init__(self, xs):
        self.ins = list(xs)
        n = self.n = len(xs)
        self.out_shapes = [jax.ShapeDtypeStruct((N_DEV,) + a.shape, a.dtype) for a in xs]
        self.scratch = [pltpu.SemaphoreType.DMA((n, 7)), pltpu.SemaphoreType.DMA((n, 7)), pltpu.SemaphoreType.DMA((n,))]

    def _parts(self, ins, outs, sems):
        send_sems, recv_sems, local_sems = sems
        x, y, c = _place()
        me, sib = (x, y, c), (x, y, 1 - c)
        chips = [(1 - x, y), (x, 1 - y), (1 - x, 1 - y)]

        def copy(a, k, block, to, src=None):
            dst = outs[a].at[_slot(block)]
            return pltpu.make_async_remote_copy(
                src_ref=dst if src is None else src, dst_ref=dst, send_sem=send_sems.at[a, k],
                recv_sem=recv_sems.at[a, k], device_id=to, device_id_type=MESH)

        mine = [pltpu.make_async_copy(ins[a], outs[a].at[_slot(me)], local_sems.at[a]) for a in range(self.n)]
        first = []
        for a in range(self.n):
            first.append(copy(a, 0, me, sib, src=ins[a]))
            first += [copy(a, 1 + j, me, (*chip, c), src=ins[a]) for j, chip in enumerate(chips)]
        return copy, mine, first, me, sib, chips, c

    def start(self, ins, outs, sems):
        _, mine, first, *_ = self._parts(ins, outs, sems)
        for cp in mine + first:
            cp.start()

    def finish(self, ins, outs, sems):
        copy, mine, first, me, sib, chips, c = self._parts(ins, outs, sems)
        passed = []
        for j, chip in enumerate(chips):
            for a in range(self.n):
                copy(a, 1 + j, (*chip, c), me).wait_recv()
                fwd = copy(a, 4 + j, (*chip, c), sib)
                fwd.start()
                passed.append(fwd)
        for a in range(self.n):
            copy(a, 0, sib, me).wait_recv()
            for j, chip in enumerate(chips):
                copy(a, 4 + j, (*chip, 1 - c), me).wait_recv()
        for cp in first + passed:
            cp.wait_send()
        for cp in mine:
            cp.wait()


def _run_alone(ex, name):
    def body(*refs):
        ni, no = len(ex.ins), len(ex.out_shapes)
        parts = refs[:ni], refs[ni:ni + no], refs[ni + no:]
        ex.start(*parts)
        ex.finish(*parts)
    return pl.pallas_call(body, name=name, in_specs=[ANY] * len(ex.ins), out_specs=[ANY] * len(ex.out_shapes),
                          out_shape=ex.out_shapes, scratch_shapes=ex.scratch)(*ex.ins)


def _ride(body, n_in, n_out, ex, first, last):
    ni, no = len(ex.ins), len(ex.out_shapes)

    def wrapped(*refs):
        a, r_in = refs[:n_in], refs[n_in:n_in + ni]
        b, r_out = refs[n_in + ni:n_in + ni + n_out], refs[n_in + ni + n_out:n_in + ni + n_out + no]
        rest = refs[n_in + ni + n_out + no:]
        s, r_s = rest[:len(rest) - len(ex.scratch)], rest[len(rest) - len(ex.scratch):]

        @pl.when(first())
        def _():
            ex.start(r_in, r_out, r_s)
        body(*a, *b, *s)

        @pl.when(last())
        def _():
            ex.finish(r_in, r_out, r_s)
    return wrapped


def all_gather(xs, name):
    return _run_alone(_Gather(xs), name)


class _Exchange:
    def __init__(self, groups):
        self.ins = [a for grp in groups for a in grp]
        self.where = [(g, i) for g, grp in enumerate(groups) for i in range(len(grp))]
        n = self.n = len(self.ins)
        self.out_shapes = [jax.ShapeDtypeStruct((N_DEV, len(grp)) + grp[0].shape[1:], grp[0].dtype) for grp in groups]
        self.scratch = [pltpu.SemaphoreType.DMA((n, 7)), pltpu.SemaphoreType.DMA((n, 7)), pltpu.SemaphoreType.DMA((n,))]

    def _parts(self, ins, outs, sems):
        send_sems, recv_sems, local_sems = sems
        x, y, c = _place()
        me = (x, y, c)

        def peer(r):
            return (1 - x if r & 4 else x, 1 - y if r & 2 else y, 1 - c if r & 1 else c)

        def land(a, src):
            g, i = self.where[a]
            return outs[g].at[_slot(src), i]

        def copy(a, r, src_dev):
            p = peer(r)
            return pltpu.make_async_remote_copy(
                src_ref=ins[a].at[_slot(p)], dst_ref=land(a, src_dev), send_sem=send_sems.at[a, r - 1],
                recv_sem=recv_sems.at[a, r - 1], device_id=p, device_id_type=MESH)

        mine = [pltpu.make_async_copy(ins[a].at[_slot(me)], land(a, me), local_sems.at[a]) for a in range(self.n)]
        sends = [copy(a, r, me) for r in range(1, N_DEV) for a in range(self.n)]
        arrivals = [copy(a, r, peer(r)) for r in range(1, N_DEV) for a in range(self.n)]
        return mine, sends, arrivals

    def start(self, ins, outs, sems):
        mine, sends, _ = self._parts(ins, outs, sems)
        for cp in mine + sends:
            cp.start()

    def finish(self, ins, outs, sems):
        mine, sends, arrivals = self._parts(ins, outs, sems)
        for cp in arrivals:
            cp.wait_recv()
        for cp in sends:
            cp.wait_send()
        for cp in mine:
            cp.wait()


def all_to_all(groups, name):
    return _run_alone(_Exchange(groups), name)


def _mods(mod, l, D):
    return [mod[l:l + 1, k * D:(k + 1) * D] for k in range(6)]


def _ffn_backward(l, dxo, dy, hb, u, uc, xin, sc2, gains, W, post):
    _, NJ, T, fs = u.shape
    duc, d_wdn = ffn_bwd_a(dy, uc, W["wdn"][l], f"ffn{l}_bwd_a")
    dh, taps, d_wup = ffn_bwd_b(duc.reshape(2 * NJ, T, fs), u.reshape(2 * NJ, T, fs), hb, W["fdw_w"][l],
                                W["wup"][l], f"ffn{l}_bwd_b")
    dx, da, dsh, *below = resid_bwd(f"ffn{l}_resid_bwd", pre=(dh, xin, dxo, gains["pre_ffn_g"][l:l + 1], sc2), post=post)
    small = dict(da=da, dsh=dsh, dwb=taps[:, 3], dww=taps[:, 0:3])
    return dx, d_wup, d_wdn, small, below


def _local_step(xt, tgt, mod, gains, W, mine):
    T, D = xt.shape
    zero_d = jnp.zeros((1, D), F32)
    half = lambda g: g.reshape(N_DEV // 2, 2 * g.shape[1], D)
    sh1, sc1, g1, sh2, sc2, g2 = m0 = _mods(mod, 0, D)
    h0 = prenorm_mod(xt, gains["pre_mix_g"][0:1], sc1, sh1, "l0_prenorm")
    proj, (whout, wup0, wdn0) = mm_nn(h0, W["whin"], jnp.zeros((1, 4 * D), F32), "hgrn_proj",
                                      _Gather([mine["whout"], mine["wup"][0], mine["wdn"][0]]))
    o, og, st, (wcin, wcout, wup1, wdn1) = hgrn_fwd(
        proj, W["lb"], W["gg"], "hgrn_fwd", _Gather([mine["wcin"], mine["wcout"], mine["wup"][1], mine["wdn"][1]]))
    W = dict(W, whout=whout.reshape(D, D), wcin=wcin, wcout=wcout.reshape(D, D), wup=[wup0, wup1],
             wdn=[half(wdn0), half(wdn1)])
    t1, c1, e1, t2, c2, e2 = m1 = _mods(mod, 1, D)
    y0, x1, h0f = mm_post(og, W["whout"], zero_d, xt, g1, gains["post_mix_g"][0:1],
                          (gains["pre_ffn_g"][0:1], sc2, sh2), "hgrn_out")
    u0, uc0, yf0, x2, h1 = ffn_fwd(h0f, W["wup"][0], W["fdw_w"][0], W["fdw_b"][0], W["wdn"][0], x1, g2,
                                   gains["post_ffn_g"][0:1], (gains["pre_mix_g"][1:2], c1, t1), "ffn0_fwd")
    uc = mm_nn(h1, W["wcin"], W["cb_in"], "conv_in")
    cv, va = conv_mid_fwd(uc, W["cw"], W["cdw_b"], W["ln_g"], W["ln_b"], "conv_mid_fwd")
    y1, x3, h1f = mm_post(va, W["wcout"], W["cb_out"], x2, e1, gains["post_mix_g"][1:2],
                          (gains["pre_ffn_g"][1:2], c2, t2), "conv_out")
    u1, uc1, yf1, x4 = ffn_fwd(h1f, W["wup"][1], W["fdw_w"][1], W["fdw_b"][1], W["wdn"][1], x3, e2,
                               gains["post_ffn_g"][1:2], None, "ffn1_fwd")
    dx4, loss_part, dyf1, dgate_f1, dpost_f1, _ = resid_bwd(
        "loss_resid_bwd", loss=(x4, tgt), post=(yf1, e2, gains["post_ffn_g"][1:2]))
    dx3, d_wup1, d_wdn1, sf1, (dy1, dgate_c, dpost_c, dys_c) = _ffn_backward(
        1, dx4, dyf1, h1f, u1, uc1, x3, c2, gains, W, (y1, e1, gains["post_mix_g"][1:2]))
    dva = mm_nt(dy1, W["wcout"][None], "conv_dva", True)
    d_wcout = mm_tn(va, dy1, "conv_dwout", True, True, D, D)
    dcv, dlg, dlbias, taps = conv_bwd_a(dva, cv, uc, W["ln_g"], W["ln_b"], "conv_bwd_a")
    duc, dbin = conv_bwd_b(dcv, uc, W["cw"], "conv_bwd_b")
    dh1 = mm_nt(duc, W["wcin"], "conv_dh", True)
    d_wcin = mm_tn(h1, duc, "conv_dwin", True, True, D, W["wcin"].shape[2])
    dx2, da_c, dsh_c, dyf0, dgate_f0, dpost_f0, _ = resid_bwd(
        "l1_resid_bwd", pre=(dh1, x2, dx3, gains["pre_mix_g"][1:2], c1), post=(yf0, g2, gains["post_ffn_g"][0:1]))
    dx1, d_wup0, d_wdn0, sf0, (dy0, dgate_h, dpost_h, _) = _ffn_backward(
        0, dx2, dyf0, h0f, u0, uc0, x1, sc2, gains, W, (y0, g1, gains["post_mix_g"][0:1]))
    dog = mm_nt(dy0, W["whout"][None], "hgrn_dog", True)
    d_whout = mm_tn(og, dy0, "hgrn_dwout", True, True, D, D)
    rows = lambda g: g.reshape(N_DEV, -1, D)
    dproj, dlb, dgg, (r_wcin, r_wcout, r_wup, r_wdn, r_whout) = hgrn_bwd(
        proj, o, dog, st, W["lb"], W["gg"], "hgrn_bwd",
        _Exchange([[d_wcin], [rows(d_wcout)], [d_wup0, d_wup1], [rows(d_wdn0), rows(d_wdn1)], [rows(d_whout)]]))
    d_whin = mm_tn(h0, dproj, "hgrn_dwin", True, True, D, W["whin"].shape[2])
    dh0, (r_whin,) = mm_nt(dproj, W["whin"], "hgrn_dh", True, _Exchange([[d_whin]]))
    dx0, da_h, dsh_h = resid_bwd("l0_resid_bwd", pre=(dh0, xt, dx1, gains["pre_mix_g"][0:1], sc1))
    sf0.update(dgate=dgate_f0, dpost=dpost_f0)
    sf1.update(dgate=dgate_f1, dpost=dpost_f1)

    big = dict(hgrn_w_in=r_whin, hgrn_w_out=r_whout, conv_w_in=r_wcin, conv_w_out=r_wcout, ffn_w_up=r_wup,
               ffn_w_down=r_wdn)
    pm, pf = gains["pre_mix_g"], gains["pre_ffn_g"]
    dmod = jnp.concatenate([
        dsh_h, da_h * pm[0:1], dgate_h, sf0["dsh"], sf0["da"] * pf[0:1], sf0["dgate"],
        dsh_c, da_c * pm[1:2], dgate_c, sf1["dsh"], sf1["da"] * pf[1:2], sf1["dgate"]], axis=1)
    NCH = D // LANES
    tap = taps.reshape(NCH, 32, 8, LANES).sum(axis=2)
    small = dict(
        dmod=dmod,
        pre_mix_g=jnp.concatenate([da_h * (1.0 + sc1), da_c * (1.0 + c1)], axis=0),
        post_mix_g=jnp.concatenate([dpost_h, dpost_c], axis=0),
        pre_ffn_g=jnp.concatenate([sf0["da"] * (1.0 + sc2), sf1["da"] * (1.0 + c2)], axis=0),
        post_ffn_g=jnp.concatenate([sf0["dpost"], sf1["dpost"]], axis=0),
        lb=dlb, gg=dgg,
        ffn_dw_b=jnp.stack([sf0["dwb"], sf1["dwb"]]),
        conv_b_in=dbin,
        conv_dw_w=jnp.transpose(tap[:, :31], (1, 0, 2)).reshape(31, D),
        conv_dw_b=tap[:, 31].reshape(1, D),
        conv_ln_g=dlg, conv_ln_b=dlbias, conv_b_out=dys_c,
        ffn_dw_w=jnp.stack([sf0["dww"], sf1["dww"]]))
    return loss_part, dx0, big, small


SMALL_ORDER = ["dmod", "pre_mix_g", "post_mix_g", "pre_ffn_g", "post_ffn_g", "lb", "gg", "ffn_dw_b", "conv_b_in",
               "conv_dw_w", "conv_dw_b", "conv_ln_g", "conv_ln_b", "conv_b_out", "ffn_dw_w"]


def _pack(parts):
    flat = jnp.concatenate([p.reshape(-1) for p in parts])
    pad = (-flat.shape[0]) % LANES
    if pad:
        flat = jnp.concatenate([flat, jnp.zeros((pad,), F32)])
    return flat.reshape(-1, LANES)


def _unpack(flat, shapes):
    out, off = [], 0
    for s in shapes:
        n = 1
        for d in s:
            n *= d
        out.append(flat[off:off + n].reshape(s))
        off += n
    return out


def _apply_adamw(name, w, g_slots, m, v):
    shp = w.shape
    C = shp[-1]
    R = w.size // C
    g, d, mn, vn = adamw(w.reshape(R, C), g_slots.reshape(g_slots.shape[0], R, C), m.reshape(R, C), v.reshape(R, C),
                         "adamw_" + name)
    return g.reshape(shp), d.reshape(shp), mn.reshape(shp), vn.reshape(shp)


WEIGHTS = ['ada_w', 'ada_b', 'pre_mix_g', 'post_mix_g', 'pre_ffn_g', 'post_ffn_g', 'hgrn_w_in', 'hgrn_lb_logits',
           'hgrn_gnorm_g', 'hgrn_w_out', 'conv_w_in', 'conv_b_in', 'conv_dw_w', 'conv_dw_b', 'conv_ln_g', 'conv_ln_b',
           'conv_w_out', 'conv_b_out', 'ffn_w_up', 'ffn_dw_w', 'ffn_dw_b', 'ffn_w_down']


def kernel(x, c, ada_w, ada_b, pre_mix_g, post_mix_g, pre_ffn_g, post_ffn_g, hgrn_w_in, hgrn_lb_logits, hgrn_gnorm_g, hgrn_w_out, conv_w_in, conv_b_in, conv_dw_w, conv_dw_b, conv_ln_g, conv_ln_b, conv_w_out, conv_b_out, ffn_w_up, ffn_dw_w, ffn_dw_b, ffn_w_down, loss_target, m_ada_w, m_ada_b, m_pre_mix_g, m_post_mix_g, m_pre_ffn_g, m_post_ffn_g, m_hgrn_w_in, m_hgrn_lb_logits, m_hgrn_gnorm_g, m_hgrn_w_out, m_conv_w_in, m_conv_b_in, m_conv_dw_w, m_conv_dw_b, m_conv_ln_g, m_conv_ln_b, m_conv_w_out, m_conv_b_out, m_ffn_w_up, m_ffn_dw_w, m_ffn_dw_b, m_ffn_w_down, v_ada_w, v_ada_b, v_pre_mix_g, v_post_mix_g, v_pre_ffn_g, v_post_ffn_g, v_hgrn_w_in, v_hgrn_lb_logits, v_hgrn_gnorm_g, v_hgrn_w_out, v_conv_w_in, v_conv_b_in, v_conv_dw_w, v_conv_dw_b, v_conv_ln_g, v_conv_ln_b, v_conv_w_out, v_conv_b_out, v_ffn_w_up, v_ffn_dw_w, v_ffn_dw_b, v_ffn_w_down):
    P = dict(locals())
    _, T, D = x.shape
    me = _slot(_place())
    L = ada_w.shape[0]
    na = ada_w.shape[2]
    fs = ffn_w_up.shape[2]
    nci = conv_w_in.shape[2]
    nd = conv_dw_b.shape[1]
    NCH = D // LANES

    small_in = [c, conv_b_in, conv_dw_w, conv_dw_b, conv_ln_g, conv_ln_b, conv_b_out, ffn_dw_w]
    whin, packed = all_gather([_bf(hgrn_w_in[0]), _pack(small_in)], "gather_first")
    mine = dict(whout=_bf(hgrn_w_out[0]), wcin=_bf(conv_w_in[0]), wcout=_bf(conv_w_out[0]),
                wup=[_bf(ffn_w_up[l]) for l in range(L)], wdn=[_bf(ffn_w_down[l]) for l in range(L)])
    sm = packed.reshape(N_DEV, -1)
    offs = [0]
    for a in small_in:
        offs.append(offs[-1] + a.size)
    piece = lambda k, shp: sm[:, offs[k]:offs[k + 1]].reshape((N_DEV,) + shp)
    c_all = piece(0, (D,))
    dw_nat = jnp.transpose(piece(2, (31, nd)), (1, 0, 2)).reshape(31, D)
    cw = jnp.pad(jnp.transpose(dw_nat.reshape(31, NCH, LANES), (1, 0, 2)), ((0, 0), (0, 1), (0, 0)))
    fdw = piece(7, (L, 3, fs))
    lb_soft = lb_softmax(hgrn_lb_logits, "lb_softmax")
    W = dict(
        whin=whin, lb=lb_soft[0:1], gg=hgrn_gnorm_g, cb_in=piece(1, (nci,)).reshape(1, N_DEV * nci), cw=cw,
        cdw_b=piece(3, (nd,)).reshape(1, D), ln_g=piece(4, (nd,)).reshape(1, D), ln_b=piece(5, (nd,)).reshape(1, D),
        cb_out=piece(6, (nd,)).reshape(1, D), fdw_w=[fdw[:, l] for l in range(L)],
        fdw_b=[ffn_dw_b[l].reshape(N_DEV, 1, fs) for l in range(L)])

    bias_mine = lax.dynamic_slice(ada_b, (0, me * na), (L, na)).reshape(L, 1, na)
    mod_cols = ada_fwd(c_all, ada_w, bias_mine, "ada_fwd")
    (mod_all,) = all_gather([mod_cols], "gather_mod")
    mod = jnp.transpose(lax.dynamic_index_in_dim(mod_all, me, axis=2, keepdims=False), (1, 0, 2)).reshape(L, N_DEV * na)

    gains = dict(pre_mix_g=pre_mix_g, post_mix_g=post_mix_g, pre_ffn_g=pre_ffn_g, post_ffn_g=post_ffn_g)
    loss_part, dx0, big, small = _local_step(x[0], loss_target[0], mod, gains, W, mine)
    loss = lax.psum((0.5 / D) * jnp.sum(loss_part), ("x", "y", "c"))

    parts = [small[k] for k in SMALL_ORDER]
    (sm_all,) = all_gather([_pack(parts)], "gather_small_grads")
    tot = _unpack(sum_slots(sm_all, "sum_small_grads").reshape(-1), [p.shape for p in parts])
    tot = dict(zip(SMALL_ORDER, tot))
    dmod_all = sm_all.reshape(N_DEV, -1)[:, :L * 6 * D].reshape(N_DEV, L, 6 * D)
    dm_mine = jnp.transpose(lax.dynamic_slice(dmod_all, (0, 0, me * na), (N_DEV, L, na)), (1, 0, 2))
    s0 = lb_soft[0:1]
    onehot = (lax.broadcasted_iota(jnp.int32, lb_soft.shape, 0) == 0).astype(F32)
    col = lambda a, n: lax.dynamic_slice_in_dim(a, me * n, n, axis=a.ndim - 1)
    G = {
        'ada_w': ada_bwd(c_all, dm_mine, "ada_bwd")[None],
        'ada_b': tot["dmod"].reshape(1, L, 6 * D),
        'pre_mix_g': tot["pre_mix_g"][None], 'post_mix_g': tot["post_mix_g"][None],
        'pre_ffn_g': tot["pre_ffn_g"][None], 'post_ffn_g': tot["post_ffn_g"][None],
        'hgrn_lb_logits': (tot["lb"] * s0 * (onehot - lb_soft))[None],
        'hgrn_gnorm_g': tot["gg"][None],
        'ffn_dw_b': tot["ffn_dw_b"].reshape(1, L, N_DEV * fs),
        'conv_b_in': col(tot["conv_b_in"], nci)[None],
        'conv_dw_w': col(tot["conv_dw_w"], nd)[None, None],
        'conv_dw_b': col(tot["conv_dw_b"], nd)[None], 'conv_ln_g': col(tot["conv_ln_g"], nd)[None],
        'conv_ln_b': col(tot["conv_ln_b"], nd)[None], 'conv_b_out': col(tot["conv_b_out"], nd)[None],
        'ffn_dw_w': lax.dynamic_index_in_dim(tot["ffn_dw_w"], me, axis=1, keepdims=False)[None],
    }

    G.update(big)
    res = {n: _apply_adamw(n, P[n], G[n], P["m_" + n], P["v_" + n]) for n in WEIGHTS}
    return (loss, dx0[None], *[res[n][0] for n in WEIGHTS], *[res[n][1] for n in WEIGHTS],
            *[res[n][2] for n in WEIGHTS], *[res[n][3] for n in WEIGHTS])
```

```python
import functools

import jax
import jax.numpy as jnp
from jax import lax
from jax.experimental import pallas as pl
from jax.experimental.pallas import tpu as pltpu

F32 = jnp.float32
BF16 = jnp.bfloat16
EPS = 1e-6
LANES = 128
N_DEV = 8
VMEM_LIMIT = 48 * 1024 * 1024
FFN_BWD_VMEM = 58 * 1024 * 1024
MM_ROWS = 1024
HG_CHUNK = 128
HG_SUB = 32
EXP_CLAMP = 80.0
CONV_HALO = 32
FFN_HALO = 8
CONV_ROWS = 32
ADAM_LR, ADAM_B1, ADAM_B2, ADAM_EPS, ADAM_WD, ADAM_STEP = 0.001, 0.9, 0.999, 1e-08, 0.01, 10
MESH = pl.DeviceIdType.MESH
ANY = pl.BlockSpec(memory_space=pl.ANY)


def _cp(*sem):
    return pltpu.CompilerParams(dimension_semantics=sem, vmem_limit_bytes=VMEM_LIMIT)


def _rt(n, t):
    t = min(n, t)
    assert n % t == 0, (n, t)
    return t


def _sig(x):
    return jax.nn.sigmoid(x)


def _nt(a, b):
    return lax.dot_general(a, b, (((1,), (1,)), ((), ())), preferred_element_type=F32)


def _tn(a, b):
    return lax.dot_general(a, b, (((0,), (0,)), ((), ())), preferred_element_type=F32)


def _nn(a, b):
    return jnp.dot(a, b, preferred_element_type=F32)


def _bf(x):
    return x.astype(BF16)


def _nn_st(a, b):
    return _nn(_bf(a), _bf(b))


def _tn_st(a, b):
    return _tn(_bf(a), _bf(b))


def _row(d):
    return pl.BlockSpec((1, d), lambda *_: (0, 0))


def prenorm_mod(x, g, sc, sh, name):
    T, D = x.shape
    tm = _rt(T, 512)

    def body(x_ref, g_ref, sc_ref, sh_ref, h_ref):
        xv = x_ref[...]
        r = lax.rsqrt(jnp.mean(xv * xv, axis=-1, keepdims=True) + EPS)
        h_ref[...] = _bf((xv * r) * g_ref[...] * (1.0 + sc_ref[...]) + sh_ref[...])

    tile = pl.BlockSpec((tm, D), lambda i: (i, 0))
    return pl.pallas_call(
        body, name=name, grid=(T // tm,), in_specs=[tile, _row(D), _row(D), _row(D)], out_specs=tile,
        out_shape=jax.ShapeDtypeStruct((T, D), BF16), compiler_params=_cp("parallel"))(x, g, sc, sh)


def resid_bwd(name, pre=None, loss=None, post=None):
    T, D = (pre[1] if pre is not None else loss[0]).shape
    tm = _rt(T, 512)
    n_src = 5 if pre is not None else 2
    n_in = n_src + (3 if post is not None else 0)

    def body(*refs):
        ins, outs = refs[:n_in], refs[n_in:]
        first = pl.program_id(0) == 0
        if pre is not None:
            dh_ref, x_ref, dr_ref, g_ref, sc_ref = ins[:5]
            dx_ref, da_ref, dsh_ref = outs[:3]

            @pl.when(first)
            def _():
                da_ref[...] = jnp.zeros_like(da_ref)
                dsh_ref[...] = jnp.zeros_like(dsh_ref)
            xv = x_ref[...]
            dh = dh_ref[...]
            r = lax.rsqrt(jnp.mean(xv * xv, axis=-1, keepdims=True) + EPS)
            n = xv * r
            da_ref[...] += jnp.sum(dh * n, axis=0, keepdims=True)
            dsh_ref[...] += jnp.sum(dh, axis=0, keepdims=True)
            dn = dh * (g_ref[...] * (1.0 + sc_ref[...]))
            dx = dr_ref[...] + r * (dn - n * jnp.mean(dn * n, axis=-1, keepdims=True))
            dx_ref[...] = dx
            rest = outs[3:]
        else:
            x_ref, t_ref = ins[:2]
            dx_ref, part_ref = outs[:2]

            @pl.when(first)
            def _():
                part_ref[...] = jnp.zeros_like(part_ref)
            e = x_ref[...] - t_ref[...]
            dx = e * (1.0 / D)
            dx_ref[...] = dx
            e2 = (e * e).reshape(tm // 8, 8, D).sum(axis=0)
            acc = e2[:, 0:LANES]
            for j in range(1, D // LANES):
                acc = acc + e2[:, j * LANES:(j + 1) * LANES]
            part_ref[...] += acc
            rest = outs[2:]
        if post is not None:
            y_ref, gt_ref, pg_ref = ins[n_src:]
            dy_ref, dgate_ref, dpg_ref, dys_ref = rest

            @pl.when(first)
            def _():
                dgate_ref[...] = jnp.zeros_like(dgate_ref)
                dpg_ref[...] = jnp.zeros_like(dpg_ref)
                dys_ref[...] = jnp.zeros_like(dys_ref)
            yv = y_ref[...]
            ry = lax.rsqrt(jnp.mean(yv * yv, axis=-1, keepdims=True) + EPS)
            ny = yv * ry
            s = jnp.sum(dx * ny, axis=0, keepdims=True)
            dgate_ref[...] += s * pg_ref[...]
            dpg_ref[...] += s * gt_ref[...]
            dny = dx * (gt_ref[...] * pg_ref[...])
            dy = ry * (dny - ny * jnp.mean(dny * ny, axis=-1, keepdims=True))
            dy_ref[...] = _bf(dy)
            dys_ref[...] += jnp.sum(dy, axis=0, keepdims=True)

    tile = pl.BlockSpec((tm, D), lambda i: (i, 0))
    vec = jax.ShapeDtypeStruct((1, D), F32)
    big = jax.ShapeDtypeStruct((T, D), F32)
    if pre is not None:
        operands, in_specs = list(pre), [tile, tile, tile, _row(D), _row(D)]
        out_specs, out_shape = [tile, _row(D), _row(D)], [big, vec, vec]
    else:
        operands, in_specs = list(loss), [tile, tile]
        out_specs, out_shape = [tile, pl.BlockSpec((8, LANES), lambda i: (0, 0))], [big, jax.ShapeDtypeStruct((8, LANES), F32)]
    if post is not None:
        operands, in_specs = operands + list(post), in_specs + [tile, _row(D), _row(D)]
        out_specs = out_specs + [tile, _row(D), _row(D), _row(D)]
        out_shape = out_shape + [jax.ShapeDtypeStruct((T, D), BF16), vec, vec, vec]
    return pl.pallas_call(body, name=name, grid=(T // tm,), in_specs=in_specs, out_specs=out_specs, out_shape=out_shape,
                          compiler_params=_cp("arbitrary"))(*operands)


def _call(body, name, grid, in_specs, out_specs, out_shape, scratch, sems, operands, ex=None, vmem=VMEM_LIMIT):
    if ex is not None:
        def first():
            return functools.reduce(lambda p, q: p & q, [pl.program_id(k) == 0 for k in range(len(grid))])

        def last():
            return functools.reduce(lambda p, q: p & q, [pl.program_id(k) == grid[k] - 1 for k in range(len(grid))])
        body = _ride(body, len(in_specs), len(out_specs), ex, first, last)
        in_specs = in_specs + [ANY] * len(ex.ins)
        out_specs = out_specs + [ANY] * len(ex.out_shapes)
        out_shape = out_shape + ex.out_shapes
        scratch = scratch + ex.scratch
        operands = list(operands) + ex.ins
        sems = ("arbitrary",) * len(grid)
    res = pl.pallas_call(body, name=name, grid=grid, in_specs=in_specs, out_specs=out_specs, out_shape=out_shape,
                         scratch_shapes=scratch,
                         compiler_params=pltpu.CompilerParams(dimension_semantics=sems, vmem_limit_bytes=vmem))(*operands)
    n_own = len(res) - (len(ex.out_shapes) if ex is not None else 0)
    return res[:n_own], res[n_own:]


def mm_nn(a, bg, bias, name, ex=None):
    M, K = a.shape
    G, _, n = bg.shape
    tm = _rt(M, MM_ROWS)

    def body(a_ref, b_ref, bias_ref, o_ref):
        o_ref[...] = _nn(a_ref[...], b_ref[...]) + bias_ref[...]

    (out,), rider = _call(
        body, name, (M // tm, G),
        [pl.BlockSpec((tm, K), lambda i, j: (i, 0)), pl.BlockSpec((None, K, n), lambda i, j: (j, 0, 0)),
         pl.BlockSpec((1, n), lambda i, j: (0, j))],
        [pl.BlockSpec((tm, n), lambda i, j: (i, j))], [jax.ShapeDtypeStruct((M, G * n), F32)], [],
        ("parallel", "arbitrary"), (a, bg, bias), ex)
    return (out, rider) if ex is not None else out


def _next_input(xo, g_ref, sc_ref, sh_ref):
    r = lax.rsqrt(jnp.mean(xo * xo, axis=-1, keepdims=True) + EPS)
    return _bf((xo * r) * g_ref[...] * (1.0 + sc_ref[...]) + sh_ref[...])


def mm_post(a, b, bias, x, gate, pg, nxt, name):
    T, K = a.shape
    D = b.shape[1]
    tm = _rt(T, 512)

    def body(a_ref, b_ref, bias_ref, x_ref, gt_ref, pg_ref, g_ref, sc_ref, sh_ref, y_ref, xo_ref, h_ref):
        y = _nn(a_ref[...], b_ref[...]) + bias_ref[...]
        y_ref[...] = y
        r = lax.rsqrt(jnp.mean(y * y, axis=-1, keepdims=True) + EPS)
        xo = x_ref[...] + gt_ref[...] * ((y * r) * pg_ref[...])
        xo_ref[...] = xo
        h_ref[...] = _next_input(xo, g_ref, sc_ref, sh_ref)

    tile = pl.BlockSpec((tm, D), lambda i: (i, 0))
    out = jax.ShapeDtypeStruct((T, D), F32)
    return pl.pallas_call(
        body, name=name, grid=(T // tm,),
        in_specs=[pl.BlockSpec((tm, K), lambda i: (i, 0)), pl.BlockSpec((K, D), lambda i: (0, 0)), _row(D), tile]
        + [_row(D)] * 5,
        out_specs=[tile, tile, tile], out_shape=[out, out, jax.ShapeDtypeStruct((T, D), BF16)],
        compiler_params=_cp("parallel"))(a, b, bias, x, gate, pg, *nxt)


def mm_nt(dy, wg, name, natural, ex=None):
    G, K, n = wg.shape
    T = dy.shape[0] if natural else dy.shape[1]
    tm = _rt(T, MM_ROWS)

    def body(dy_ref, w_ref, o_ref, acc):
        j = pl.program_id(1)

        @pl.when(j == 0)
        def _():
            acc[...] = jnp.zeros_like(acc)
        acc[...] += _nt(_bf(dy_ref[...]), w_ref[...])

        @pl.when(j == G - 1)
        def _():
            o_ref[...] = acc[...]

    dspec = (pl.BlockSpec((tm, n), lambda i, j: (i, j)) if natural
             else pl.BlockSpec((None, tm, n), lambda i, j: (j, i, 0)))
    (out,), rider = _call(
        body, name, (T // tm, G), [dspec, pl.BlockSpec((None, K, n), lambda i, j: (j, 0, 0))],
        [pl.BlockSpec((tm, K), lambda i, j: (i, 0))], [jax.ShapeDtypeStruct((T, K), F32)],
        [pltpu.VMEM((tm, K), F32)], ("parallel", "arbitrary"), (dy, wg), ex)
    return (out, rider) if ex is not None else out


def mm_tn(xm, gm, name, x_natural, g_natural, kx, n):
    if x_natural:
        T, Gx = xm.shape[0], xm.shape[1] // kx
    else:
        Gx, T = xm.shape[0], xm.shape[1]
    tt = _rt(T, MM_ROWS)
    if x_natural:
        xspec = pl.BlockSpec((tt, kx), lambda g, k, t: (t, k))
    else:
        xspec = pl.BlockSpec((None, tt, kx), lambda g, k, t: (k, t, 0))
    if g_natural:
        Gg = gm.shape[1] // n
        gspec = pl.BlockSpec((tt, n), lambda g, k, t: (t, g))
    else:
        Gg = gm.shape[0]
        gspec = pl.BlockSpec((None, tt, n), lambda g, k, t: (g, t, 0))

    NT = T // tt

    def body(x_ref, g_ref, o_ref, acc):
        t = pl.program_id(2)

        @pl.when(t == 0)
        def _():
            acc[...] = jnp.zeros_like(acc)
        acc[...] += _tn(_bf(x_ref[...]), _bf(g_ref[...]))

        @pl.when(t == NT - 1)
        def _():
            o_ref[...] = acc[...].astype(o_ref.dtype)

    return pl.pallas_call(
        body, name=name, grid=(Gg, Gx, NT), in_specs=[xspec, gspec],
        out_specs=pl.BlockSpec((None, kx, n), lambda g, k, t: (g, k, 0)),
        out_shape=jax.ShapeDtypeStruct((Gg, Gx * kx, n), BF16), scratch_shapes=[pltpu.VMEM((kx, n), F32)],
        compiler_params=_cp("parallel", "parallel", "arbitrary"))(xm, gm)


def _split3(x):
    h = _bf(x)
    r = x - h.astype(F32)
    m = _bf(r)
    lo = _bf(r - m.astype(F32))
    return h, m, lo


def _tri_mm(tri, x):
    h, m, lo = _split3(x)
    return _nn(tri, lo) + _nn(tri, m) + _nn(tri, h)


def _hgrn_gates(qr, fz, lb):
    sq = _sig(qr)
    q = qr * sq
    sig = _sig(fz)
    f = lb + (1.0 - lb) * sig
    k = 1.0 - f
    return sq, q, sig, f, k, jnp.log(f)


def hgrn_fwd(proj, lb, gg, name, ex=None):
    T, D4 = proj.shape
    D = D4 // 4
    H = D // LANES
    C = _rt(T, HG_CHUNK)
    SB = min(HG_SUB, C)
    NC = T // C

    def body(q_ref, f_ref, v_ref, g_ref, lb_ref, gg_ref, o_ref, og_ref, st_ref, ST, bex):
        @pl.when(pl.program_id(0) == 0)
        def _():
            ST[...] = jnp.zeros_like(ST)
        r_i = lax.broadcasted_iota(jnp.int32, (C, C), 0)
        c_i = lax.broadcasted_iota(jnp.int32, (C, C), 1)
        low = _bf((r_i >= c_i).astype(F32))
        rs = lax.broadcasted_iota(jnp.int32, (SB, C), 0)
        cs = lax.broadcasted_iota(jnp.int32, (SB, C), 1)
        for hd in range(H):
            sl = slice(hd * LANES, (hd + 1) * LANES)
            _, q, _, _, k, lf = _hgrn_gates(q_ref[:, sl], f_ref[:, sl], lb_ref[:, sl])
            v = v_ref[:, sl]
            b = _tri_mm(low, lf)
            bex[hd] = b - lf
            bend = jnp.sum(lf, axis=0, keepdims=True)
            blocks = []
            for I in range(C // SB):
                p = bex[hd, pl.ds(SB * I + SB // 2, 1), :]
                rows = slice(SB * I, SB * (I + 1))
                qI = _bf(q[rows] * jnp.exp(jnp.minimum(b[rows] - p, EXP_CLAMP)))
                kI = _bf(k * jnp.exp(jnp.minimum(p - b, EXP_CLAMP)))
                blocks.append(jnp.where(rs + SB * I >= cs, _nt(qI, kI), 0.0))
            A = jnp.concatenate(blocks, axis=0)
            st0 = ST[hd]
            st_ref[0, hd] = st0
            o = _nn(_bf(A), _bf(v)) + _nt(_bf(q * jnp.exp(b)), _bf(st0))
            ST[hd] = st0 * jnp.exp(bend) + _tn_st(v, k * jnp.exp(bend - b))
            o_ref[:, sl] = o
            r = lax.rsqrt(jnp.mean(o * o, axis=-1, keepdims=True) + EPS)
            gr = g_ref[:, sl]
            og_ref[:, sl] = _bf((o * r) * gg_ref[...] * (gr * _sig(gr)))

    col = lambda j: pl.BlockSpec((C, D), lambda c, j=j: (c, j))
    tile = pl.BlockSpec((C, D), lambda c: (c, 0))
    own, rider = _call(
        body, name, (NC,), [col(0), col(1), col(2), col(3), _row(D), _row(LANES)],
        [tile, tile, pl.BlockSpec((1, H, LANES, LANES), lambda c: (c, 0, 0, 0))],
        [jax.ShapeDtypeStruct((T, D), F32), jax.ShapeDtypeStruct((T, D), BF16),
         jax.ShapeDtypeStruct((NC, H, LANES, LANES), F32)],
        [pltpu.VMEM((H, LANES, LANES), F32), pltpu.VMEM((H, C, LANES), F32)], ("arbitrary",),
        (proj, proj, proj, proj, lb, gg), ex)
    return (*own, rider) if ex is not None else own


def hgrn_bwd(proj, o, dog, st, lb, gg, name, ex=None):
    T, D4 = proj.shape
    D = D4 // 4
    H = D // LANES
    C = _rt(T, HG_CHUNK)
    SB = min(HG_SUB, C)
    NC = T // C

    def body(q_ref, f_ref, v_ref, g_ref, o_ref, dog_ref, st_ref, lb_ref, gg_ref, dp_ref, dlb_ref, dgg_ref,
             DST, bex):
        @pl.when(pl.program_id(0) == 0)
        def _():
            DST[...] = jnp.zeros_like(DST)
            dlb_ref[...] = jnp.zeros_like(dlb_ref)
            dgg_ref[...] = jnp.zeros_like(dgg_ref)
        r_i = lax.broadcasted_iota(jnp.int32, (C, C), 0)
        c_i = lax.broadcasted_iota(jnp.int32, (C, C), 1)
        causal = r_i >= c_i
        low = _bf(causal.astype(F32))
        upp = _bf((r_i <= c_i).astype(F32))
        rs = lax.broadcasted_iota(jnp.int32, (SB, C), 0)
        cs = lax.broadcasted_iota(jnp.int32, (SB, C), 1)
        ggv = gg_ref[...]
        for hd in range(H):
            sl = slice(hd * LANES, (hd + 1) * LANES)
            qr = q_ref[:, sl]
            lbv = lb_ref[:, sl]
            sq, q, sig, f, k, lf = _hgrn_gates(qr, f_ref[:, sl], lbv)
            v = v_ref[:, sl]
            oh = o_ref[:, sl]
            r = lax.rsqrt(jnp.mean(oh * oh, axis=-1, keepdims=True) + EPS)
            n = oh * r
            gr = g_ref[:, sl]
            sg = _sig(gr)
            dgo = dog_ref[:, sl]
            dgr = dgo * (n * ggv) * (sg * (1.0 + gr * (1.0 - sg)))
            don = dgo * (gr * sg)
            dgg_ref[...] += jnp.sum(don * n, axis=0, keepdims=True)
            dn = don * ggv
            do = r * (dn - n * jnp.mean(dn * n, axis=-1, keepdims=True))
            b = _tri_mm(low, lf)
            bex[hd] = b - lf
            bend = jnp.sum(lf, axis=0, keepdims=True)
            dob = _bf(do)
            dA = _bf(jnp.where(causal, _nt(dob, _bf(v)), 0.0))
            blocks, dqs, gqs = [], [], []
            dk = jnp.zeros((C, LANES), F32)
            gk = jnp.zeros((C, LANES), F32)
            for I in range(C // SB):
                p = bex[hd, pl.ds(SB * I + SB // 2, 1), :]
                rows = slice(SB * I, SB * (I + 1))
                eq = jnp.exp(jnp.minimum(b[rows] - p, EXP_CLAMP))
                qI = _bf(q[rows] * eq)
                ek = jnp.exp(jnp.minimum(p - b, EXP_CLAMP))
                kI = _bf(k * ek)
                blocks.append(jnp.where(rs + SB * I >= cs, _nt(qI, kI), 0.0))
                dqI = _nn(dA[rows], kI)
                dkI = _tn(dA[rows], qI)
                dqs.append(dqI * eq)
                gqs.append(qI.astype(F32) * dqI)
                dk = dk + dkI * ek
                gk = gk + kI.astype(F32) * dkI
            A = jnp.concatenate(blocks, axis=0)
            dst = DST[hd]
            st0 = st_ref[0, hd]
            eb = jnp.exp(b)
            ee = jnp.exp(bend - b)
            ebend = jnp.exp(bend)
            dv = _tn(_bf(A), dob) + _nt(_bf(k * ee), _bf(dst))
            dk_state = ee * _nn_st(v, dst)
            dq_state = eb * _nn_st(do, st0)
            dq = jnp.concatenate(dqs, axis=0) + dq_state
            dk = dk + dk_state
            DST[hd] = dst * ebend + _tn_st(do, q * eb)
            later = jnp.sum(k * dk_state, axis=0, keepdims=True) + ebend * jnp.sum(st0 * dst, axis=0, keepdims=True)
            pairs = jnp.concatenate(gqs, axis=0) - gk + (q * dq_state - k * dk_state)
            dlf = _tri_mm(upp, pairs) + later
            df = dlf / f - dk
            dlb_ref[:, sl] += jnp.sum(df * (1.0 - sig), axis=0, keepdims=True)
            dp_ref[:, sl] = _bf(dq * (sq * (1.0 + qr * (1.0 - sq))))
            dp_ref[:, D + hd * LANES:D + (hd + 1) * LANES] = _bf(df * (1.0 - lbv) * (sig * (1.0 - sig)))
            dp_ref[:, 2 * D + hd * LANES:2 * D + (hd + 1) * LANES] = _bf(dv)
            dp_ref[:, 3 * D + hd * LANES:3 * D + (hd + 1) * LANES] = _bf(dgr)

    col = lambda j: pl.BlockSpec((C, D), lambda c, j=j: (NC - 1 - c, j))
    tile = pl.BlockSpec((C, D), lambda c: (NC - 1 - c, 0))
    own, rider = _call(
        body, name, (NC,),
        [col(0), col(1), col(2), col(3), tile, tile,
         pl.BlockSpec((1, H, LANES, LANES), lambda c: (NC - 1 - c, 0, 0, 0)), _row(D), _row(LANES)],
        [pl.BlockSpec((C, D4), lambda c: (NC - 1 - c, 0)), _row(D), _row(LANES)],
        [jax.ShapeDtypeStruct((T, D4), BF16), jax.ShapeDtypeStruct((1, D), F32), jax.ShapeDtypeStruct((1, LANES), F32)],
        [pltpu.VMEM((H, LANES, LANES), F32), pltpu.VMEM((H, C, LANES), F32)],
        ("arbitrary",), (proj, proj, proj, proj, o, dog, st, lb, gg), ex)
    return (*own, rider) if ex is not None else own


def _prev_blk(tm, hb):
    return lambda i: (jnp.maximum(i * (tm // hb) - 1, 0), 0)


def _next_blk(tm, hb, T):
    return lambda i: (jnp.minimum((i + 1) * (tm // hb), T // hb - 1), 0)


def _glu_to_ext(uc_ref, ucp_ref, ext, D, first):
    def glu(u):
        return u[:, :D] * _sig(u[:, D:])
    gh = jnp.where(first, 0.0, glu(ucp_ref[...]))
    gm = glu(uc_ref[...])
    for c in range(D // LANES):
        ext[c, 0:CONV_HALO, :] = gh[:, c * LANES:(c + 1) * LANES]
        ext[c, CONV_HALO:, :] = gm[:, c * LANES:(c + 1) * LANES]


def conv_mid_fwd(uc, cw, cb, lg, lbias, name):
    T, D2 = uc.shape
    D = D2 // 2
    NCH = D // LANES
    W = 31
    tm = _rt(T, 256)
    RS = min(CONV_ROWS, tm)

    def body(uc_ref, ucp_ref, cw_ref, cb_ref, lg_ref, lb_ref, cv_ref, va_ref, ext, outs):
        _glu_to_ext(uc_ref, ucp_ref, ext, D, pl.program_id(0) == 0)

        def chunk(c, carry):
            for rb in range(tm // RS):
                acc = jnp.zeros((RS, LANES), F32)
                for kk in range(W):
                    acc = acc + cw_ref[c, pl.ds(kk, 1), :] * ext[c, pl.ds(rb * RS + CONV_HALO - (W - 1) + kk, RS), :]
                outs[c, pl.ds(rb * RS, RS), :] = acc
            return carry
        lax.fori_loop(0, NCH, chunk, 0)
        cv = jnp.concatenate([outs[c] for c in range(NCH)], axis=1) + cb_ref[...]
        cv_ref[...] = cv
        mu = jnp.mean(cv, axis=-1, keepdims=True)
        xc = cv - mu
        rstd = lax.rsqrt(jnp.mean(xc * xc, axis=-1, keepdims=True) + EPS)
        l = xc * rstd * lg_ref[...] + lb_ref[...]
        va_ref[...] = _bf(l * _sig(l))

    tile = pl.BlockSpec((tm, D), lambda i: (i, 0))
    return pl.pallas_call(
        body, name=name, grid=(T // tm,),
        in_specs=[pl.BlockSpec((tm, D2), lambda i: (i, 0)), pl.BlockSpec((CONV_HALO, D2), _prev_blk(tm, CONV_HALO)),
                  pl.BlockSpec((NCH, 32, LANES), lambda i: (0, 0, 0)), _row(D), _row(D), _row(D)],
        out_specs=[tile, tile],
        out_shape=[jax.ShapeDtypeStruct((T, D), F32), jax.ShapeDtypeStruct((T, D), BF16)],
        scratch_shapes=[pltpu.VMEM((NCH, tm + CONV_HALO, LANES), F32), pltpu.VMEM((NCH, tm, LANES), F32)],
        compiler_params=_cp("parallel"))(uc, uc, cw, cb, lg, lbias)


def conv_bwd_a(dva, cv, uc, lg, lbias, name):
    T, D2 = uc.shape
    D = D2 // 2
    NCH = D // LANES
    W = 31
    tm = _rt(T, 256)
    RS = min(CONV_ROWS, tm)

    def body(dva_ref, cv_ref, uc_ref, ucp_ref, lg_ref, lb_ref, dcv_ref, dlg_ref, dlb_ref, taps_ref, ext, dcs):
        @pl.when(pl.program_id(0) == 0)
        def _():
            dlg_ref[...] = jnp.zeros_like(dlg_ref)
            dlb_ref[...] = jnp.zeros_like(dlb_ref)
            taps_ref[...] = jnp.zeros_like(taps_ref)
        _glu_to_ext(uc_ref, ucp_ref, ext, D, pl.program_id(0) == 0)
        cv = cv_ref[...]
        mu = jnp.mean(cv, axis=-1, keepdims=True)
        xc = cv - mu
        rstd = lax.rsqrt(jnp.mean(xc * xc, axis=-1, keepdims=True) + EPS)
        yn = xc * rstd
        l = yn * lg_ref[...] + lb_ref[...]
        s = _sig(l)
        dl = dva_ref[...] * (s * (1.0 + l * (1.0 - s)))
        dlg_ref[...] += jnp.sum(dl * yn, axis=0, keepdims=True)
        dlb_ref[...] += jnp.sum(dl, axis=0, keepdims=True)
        dyn = dl * lg_ref[...]
        dcv = rstd * (dyn - jnp.mean(dyn, axis=-1, keepdims=True) - yn * jnp.mean(dyn * yn, axis=-1, keepdims=True))
        dcv_ref[...] = dcv
        for c in range(NCH):
            dcs[c] = dcv[:, c * LANES:(c + 1) * LANES]

        def fold(p):
            return p.reshape(RS // 8, 8, LANES).sum(axis=0)

        def chunk(c, carry):
            for kk in range(W):
                tot = jnp.zeros((8, LANES), F32)
                for rb in range(tm // RS):
                    tot = tot + fold(dcs[c, pl.ds(rb * RS, RS), :]
                                     * ext[c, pl.ds(rb * RS + CONV_HALO - (W - 1) + kk, RS), :])
                taps_ref[c, pl.ds(8 * kk, 8), :] += tot
            tot = jnp.zeros((8, LANES), F32)
            for rb in range(tm // RS):
                tot = tot + fold(dcs[c, pl.ds(rb * RS, RS), :])
            taps_ref[c, pl.ds(8 * W, 8), :] += tot
            return carry
        lax.fori_loop(0, NCH, chunk, 0)

    tile = pl.BlockSpec((tm, D), lambda i: (i, 0))
    vec = jax.ShapeDtypeStruct((1, D), F32)
    return pl.pallas_call(
        body, name=name, grid=(T // tm,),
        in_specs=[tile, tile, pl.BlockSpec((tm, D2), lambda i: (i, 0)),
                  pl.BlockSpec((CONV_HALO, D2), _prev_blk(tm, CONV_HALO)), _row(D), _row(D)],
        out_specs=[tile, _row(D), _row(D), pl.BlockSpec((NCH, 256, LANES), lambda i: (0, 0, 0))],
        out_shape=[jax.ShapeDtypeStruct((T, D), F32), vec, vec, jax.ShapeDtypeStruct((NCH, 256, LANES), F32)],
        scratch_shapes=[pltpu.VMEM((NCH, tm + CONV_HALO, LANES), F32), pltpu.VMEM((NCH, tm, LANES), F32)],
        compiler_params=_cp("arbitrary"))(dva, cv, uc, uc, lg, lbias)


def conv_bwd_b(dcv, uc, cw, name):
    T, D2 = uc.shape
    D = D2 // 2
    NCH = D // LANES
    W = 31
    tm = _rt(T, 256)
    RS = min(CONV_ROWS, tm)
    NT = T // tm

    def body(dcv_ref, dcn_ref, uc_ref, cw_ref, du_ref, db_ref, ext, outs):
        i = pl.program_id(0)

        @pl.when(i == 0)
        def _():
            db_ref[...] = jnp.zeros_like(db_ref)
        dm = dcv_ref[...]
        dn = jnp.where(i == NT - 1, 0.0, dcn_ref[...])
        for c in range(NCH):
            ext[c, 0:tm, :] = dm[:, c * LANES:(c + 1) * LANES]
            ext[c, tm:, :] = dn[:, c * LANES:(c + 1) * LANES]

        def chunk(c, carry):
            for rb in range(tm // RS):
                acc = jnp.zeros((RS, LANES), F32)
                for kk in range(W):
                    acc = acc + cw_ref[c, pl.ds(kk, 1), :] * ext[c, pl.ds(rb * RS + (W - 1) - kk, RS), :]
                outs[c, pl.ds(rb * RS, RS), :] = acc
            return carry
        lax.fori_loop(0, NCH, chunk, 0)
        dglu = jnp.concatenate([outs[c] for c in range(NCH)], axis=1)
        u = uc_ref[...]
        a = u[:, :D]
        sg = _sig(u[:, D:])
        da = dglu * sg
        dg = dglu * a * (sg * (1.0 - sg))
        du_ref[:, :D] = _bf(da)
        du_ref[:, D:] = _bf(dg)
        db_ref[:, :D] += jnp.sum(da, axis=0, keepdims=True)
        db_ref[:, D:] += jnp.sum(dg, axis=0, keepdims=True)

    tile = pl.BlockSpec((tm, D), lambda i: (i, 0))
    wide = pl.BlockSpec((tm, D2), lambda i: (i, 0))
    return pl.pallas_call(
        body, name=name, grid=(NT,),
        in_specs=[tile, pl.BlockSpec((CONV_HALO, D), _next_blk(tm, CONV_HALO, T)), wide,
                  pl.BlockSpec((NCH, 32, LANES), lambda i: (0, 0, 0))],
        out_specs=[wide, _row(D2)],
        out_shape=[jax.ShapeDtypeStruct((T, D2), BF16), jax.ShapeDtypeStruct((1, D2), F32)],
        scratch_shapes=[pltpu.VMEM((NCH, tm + CONV_HALO, LANES), F32), pltpu.VMEM((NCH, tm, LANES), F32)],
        compiler_params=_cp("arbitrary"))(dcv, dcv, uc, cw)


def _shift_rows(x, halo, k, back):
    rows, n = x.shape
    x3, h3 = x.reshape(rows // 8, 8, n), halo.reshape(1, 8, n)
    sub = lax.broadcasted_iota(jnp.int32, (1, 8, n), 1)
    if back:
        r = pltpu.roll(jnp.concatenate([h3, x3], axis=0), k, 1)
        y = jnp.where(sub < k, r[:-1], r[1:])
    else:
        r = pltpu.roll(jnp.concatenate([x3, h3], axis=0), 8 - k, 1)
        y = jnp.where(sub < 8 - k, r[:-1], r[1:])
    return y.reshape(rows, n)


def _conv3(u, uh, dw_ref, bias_ref):
    return (dw_ref[pl.ds(0, 1), :] * _shift_rows(u, uh, 2, True) + dw_ref[pl.ds(1, 1), :] * _shift_rows(u, uh, 1, True)
            + dw_ref[pl.ds(2, 1), :] * u + bias_ref[...])


def ffn_fwd(h, wup, dww, dwb, wdn, x, gate, pg, nxt, name):
    T, D = h.shape
    fs = wup.shape[2]
    NJ = wup.shape[0] // 2
    tm = _rt(T, 512)
    n_nxt = 0 if nxt is None else 3

    def body(h_ref, hp_ref, wa_ref, wb_ref, dwa_ref, dwb_ref, ba_ref, bb_ref, wd_ref, x_ref, gt_ref, pg_ref, *rest):
        nx, (u_ref, uc_ref, y_ref, xo_ref), more = rest[:n_nxt], rest[n_nxt:n_nxt + 4], rest[n_nxt + 4:]
        acc = more[-1]
        i, j = pl.program_id(0), pl.program_id(1)

        @pl.when(j == 0)
        def _():
            acc[...] = jnp.zeros_like(acc)
        hb = h_ref[...]
        hp = hp_ref[...]

        def half(s, w_ref, dw_ref, b_ref):
            u = _nn(hb, w_ref[...])
            uh = jnp.where(i == 0, 0.0, _nn(hp, w_ref[...]))
            u_ref[s] = u
            uc = _conv3(u, uh, dw_ref, b_ref)
            uc_ref[s] = uc
            return uc
        a = half(0, wa_ref, dwa_ref, ba_ref)
        b = half(1, wb_ref, dwb_ref, bb_ref)
        acc[...] += _nn(_bf(a * _sig(a) * b), wd_ref[...])

        @pl.when(j == NJ - 1)
        def _():
            y = acc[...]
            y_ref[...] = y
            r = lax.rsqrt(jnp.mean(y * y, axis=-1, keepdims=True) + EPS)
            xo = x_ref[...] + gt_ref[...] * ((y * r) * pg_ref[...])
            xo_ref[...] = xo
            if nxt is not None:
                more[0][...] = _next_input(xo, *nx)

    tile = pl.BlockSpec((tm, D), lambda i, j: (i, 0))
    shard = lambda off, r: pl.BlockSpec((None, r, fs), lambda i, j: (j + off, 0, 0))
    ush = pl.BlockSpec((2, None, tm, fs), lambda i, j: (0, j, i, 0))
    vec = pl.BlockSpec((1, D), lambda i, j: (0, 0))
    return pl.pallas_call(
        body, name=name, grid=(T // tm, NJ),
        in_specs=[tile, pl.BlockSpec((FFN_HALO, D), lambda i, j: (jnp.maximum(i * (tm // FFN_HALO) - 1, 0), 0)),
                  shard(0, D), shard(NJ, D), shard(0, 3), shard(NJ, 3), shard(0, 1), shard(NJ, 1),
                  pl.BlockSpec((None, fs, D), lambda i, j: (j, 0, 0)), tile, vec, vec] + [vec] * n_nxt,
        out_specs=[ush, ush, tile, tile] + [tile] * (n_nxt > 0),
        out_shape=[jax.ShapeDtypeStruct((2, NJ, T, fs), F32), jax.ShapeDtypeStruct((2, NJ, T, fs), F32),
                   jax.ShapeDtypeStruct((T, D), F32), jax.ShapeDtypeStruct((T, D), F32)]
        + [jax.ShapeDtypeStruct((T, D), BF16)] * (n_nxt > 0),
        scratch_shapes=[pltpu.VMEM((tm, D), F32)],
        compiler_params=_cp("parallel", "arbitrary"))(h, h, wup, wup, dww, dww, dwb, dwb, wdn, x, gate, pg,
                                                      *(nxt or ()))


def ffn_bwd_a(dy, uc, wdn, name):
    _, NJ, T, fs = uc.shape
    D = dy.shape[1]
    tm = _rt(T, 512)

    NT = T // tm

    def body(dy_ref, uc_ref, wd_ref, duc_ref, dwd_ref, acc):
        i = pl.program_id(1)

        @pl.when(i == 0)
        def _():
            acc[...] = jnp.zeros_like(acc)
        dyb = _bf(dy_ref[...])
        dz = _nt(dyb, wd_ref[...])
        a = uc_ref[0]
        b = uc_ref[1]
        sa = _sig(a)
        asa = a * sa
        acc[...] += _tn(_bf(asa * b), dyb)
        duc_ref[0] = dz * b * (sa + asa * (1.0 - sa))
        duc_ref[1] = dz * asa

        @pl.when(i == NT - 1)
        def _():
            dwd_ref[...] = _bf(acc[...])

    ush = pl.BlockSpec((2, None, tm, fs), lambda j, i: (0, j, i, 0))
    return pl.pallas_call(
        body, name=name, grid=(NJ, NT),
        in_specs=[pl.BlockSpec((tm, D), lambda j, i: (i, 0)), ush, pl.BlockSpec((None, fs, D), lambda j, i: (j, 0, 0))],
        out_specs=[ush, pl.BlockSpec((fs, D), lambda j, i: (j, 0))],
        out_shape=[jax.ShapeDtypeStruct((2, NJ, T, fs), F32), jax.ShapeDtypeStruct((NJ * fs, D), BF16)],
        scratch_shapes=[pltpu.VMEM((fs, D), F32)],
        compiler_params=_cp("parallel", "arbitrary"))(dy, uc, wdn)


def ffn_bwd_b(duc, u, h, dww, wup, name, ex=None):
    NS, T, fs = duc.shape
    D = wup.shape[1]
    tm = _rt(T, 512)
    NT = T // tm

    def body(d_ref, dn_ref, u_ref, h_ref, dw_ref, w_ref, dh_ref, g_ref, dwup_ref, acc, accw, stage):
        i, s = pl.program_id(0), pl.program_id(1)

        @pl.when((i == 0) & (s == 0))
        def _():
            g_ref[...] = jnp.zeros_like(g_ref)

        @pl.when(s == 0)
        def _():
            acc[...] = jnp.zeros_like(acc)
        d = d_ref[...]
        dn = jnp.where(i == NT - 1, 0.0, dn_ref[...])
        d1 = _shift_rows(d, dn, 1, False)
        d2 = _shift_rows(d, dn, 2, False)
        du = _bf(dw_ref[pl.ds(2, 1), :] * d + dw_ref[pl.ds(1, 1), :] * d1 + dw_ref[pl.ds(0, 1), :] * d2)
        acc[...] += _nt(du, w_ref[...])
        dw_part = _tn(h_ref[...], du)

        @pl.when(i == 0)
        def _():
            accw[s] = dw_part

        @pl.when(i > 0)
        def _():
            accw[s] += dw_part

        @pl.when(i == NT - 1)
        def _():
            stage[...] = _bf(accw[s])
            pltpu.sync_copy(stage, dwup_ref.at[s])
        u = u_ref[...]
        g_ref[s, pl.ds(0, 1), :] += jnp.sum(d2 * u, axis=0, keepdims=True)
        g_ref[s, pl.ds(1, 1), :] += jnp.sum(d1 * u, axis=0, keepdims=True)
        g_ref[s, pl.ds(2, 1), :] += jnp.sum(d * u, axis=0, keepdims=True)
        g_ref[s, pl.ds(3, 1), :] += jnp.sum(d, axis=0, keepdims=True)

        @pl.when(s == NS - 1)
        def _():
            dh_ref[...] = acc[...]

    ush = pl.BlockSpec((None, tm, fs), lambda i, s: (s, i, 0))
    unext = pl.BlockSpec((None, FFN_HALO, fs),
                         lambda i, s: (s, jnp.minimum((i + 1) * (tm // FFN_HALO), T // FFN_HALO - 1), 0))
    tile = pl.BlockSpec((tm, D), lambda i, s: (i, 0))
    own, rider = _call(
        body, name, (NT, NS),
        [ush, unext, ush, tile, pl.BlockSpec((None, 3, fs), lambda i, s: (s, 0, 0)),
         pl.BlockSpec((None, D, fs), lambda i, s: (s, 0, 0))],
        [tile, pl.BlockSpec((NS, 8, fs), lambda i, s: (0, 0, 0)), ANY],
        [jax.ShapeDtypeStruct((T, D), F32), jax.ShapeDtypeStruct((NS, 8, fs), F32),
         jax.ShapeDtypeStruct((NS, D, fs), BF16)],
        [pltpu.VMEM((tm, D), F32), pltpu.VMEM((NS, D, fs), F32), pltpu.VMEM((D, fs), BF16)],
        ("arbitrary", "arbitrary"), (duc, duc, u, h, dww, wup), ex, FFN_BWD_VMEM)
    return (*own, rider) if ex is not None else own


def ada_fwd(c_all, w, bias, name):
    L, D, n = w.shape

    def body(c_ref, w_ref, b_ref, o_ref):
        cv = c_ref[...]
        o_ref[...] = _nn(_bf(cv * _sig(cv)), _bf(w_ref[...])) + b_ref[...]

    return pl.pallas_call(
        body, name=name, grid=(L,),
        in_specs=[pl.BlockSpec((N_DEV, D), lambda l: (0, 0)), pl.BlockSpec((None, D, n), lambda l: (l, 0, 0)),
                  pl.BlockSpec((None, 1, n), lambda l: (l, 0, 0))],
        out_specs=pl.BlockSpec((None, N_DEV, n), lambda l: (l, 0, 0)),
        out_shape=jax.ShapeDtypeStruct((L, N_DEV, n), F32), compiler_params=_cp("parallel"))(c_all, w, bias)


def ada_bwd(c_all, dm, name):
    L, _, n = dm.shape
    D = c_all.shape[1]

    def body(c_ref, d_ref, o_ref):
        cv = c_ref[...]
        o_ref[...] = _tn(_bf(cv * _sig(cv)), _bf(d_ref[...]))

    return pl.pallas_call(
        body, name=name, grid=(L,),
        in_specs=[pl.BlockSpec((N_DEV, D), lambda l: (0, 0)), pl.BlockSpec((None, N_DEV, n), lambda l: (l, 0, 0))],
        out_specs=pl.BlockSpec((None, D, n), lambda l: (l, 0, 0)),
        out_shape=jax.ShapeDtypeStruct((L, D, n), F32), compiler_params=_cp("parallel"))(c_all, dm)


def sum_slots(g, name):
    S, R, C = g.shape

    def body(g_ref, o_ref):
        acc = g_ref[0]
        for s in range(1, S):
            acc = acc + g_ref[s]
        o_ref[...] = acc

    return pl.pallas_call(
        body, name=name, grid=(1,), in_specs=[pl.BlockSpec((S, R, C), lambda i: (0, 0, 0))],
        out_specs=pl.BlockSpec((R, C), lambda i: (0, 0)), out_shape=jax.ShapeDtypeStruct((R, C), F32),
        compiler_params=_cp("arbitrary"))(g)


def lb_softmax(logits, name):
    def body(l_ref, s_ref):
        l = l_ref[...]
        e = jnp.exp(l - jnp.max(l, axis=0, keepdims=True))
        s_ref[...] = e / jnp.sum(e, axis=0, keepdims=True)
    return pl.pallas_call(body, name=name, out_shape=jax.ShapeDtypeStruct(logits.shape, F32))(logits)


def adamw(w, gparts, m, v, name):
    P = len(gparts)
    S, Rp, C = gparts[0].shape
    R = P * Rp
    tr = Rp
    budget = (4 * 1024 * 1024) // (4 * (P * S + 7) * C)
    if Rp > budget:
        tr = max(t for t in range(16, budget + 1, 16) if Rp % t == 0)
    per = Rp // tr

    def body(w_ref, *rest):
        g_refs, (m_ref, v_ref, go_ref, d_ref, mo_ref, vo_ref) = rest[:P], rest[P:]
        part = pl.program_id(0) // per
        g = None
        for p, g_ref in enumerate(g_refs):
            gp = g_ref[0].astype(F32)
            for s in range(1, S):
                gp = gp + g_ref[s].astype(F32)
            g = gp if g is None else jnp.where(part == p, gp, g)
        go_ref[...] = g
        mn = ADAM_B1 * m_ref[...] + (1.0 - ADAM_B1) * g
        vn = ADAM_B2 * v_ref[...] + (1.0 - ADAM_B2) * (g * g)
        mo_ref[...] = mn
        vo_ref[...] = vn
        m_hat = mn / (1.0 - ADAM_B1 ** ADAM_STEP)
        v_hat = vn / (1.0 - ADAM_B2 ** ADAM_STEP)
        d_ref[...] = -ADAM_LR * (m_hat / (jnp.sqrt(v_hat) + ADAM_EPS) + ADAM_WD * w_ref[...])

    tile = pl.BlockSpec((tr, C), lambda i: (i, 0))
    slots = [pl.BlockSpec((S, tr, C), lambda i, p=p: (0, jnp.clip(i - p * per, 0, per - 1), 0)) for p in range(P)]
    out = jax.ShapeDtypeStruct((R, C), F32)
    return pl.pallas_call(
        body, name=name, grid=(R // tr,), in_specs=[tile] + slots + [tile, tile],
        out_specs=[tile] * 4, out_shape=[out] * 4, compiler_params=_cp("parallel"))(w, *gparts, m, v)


def _place():
    return lax.axis_index("x"), lax.axis_index("y"), lax.axis_index("c")


def _slot(p):
    return 4 * p[0] + 2 * p[1] + p[2]


class _Gather:
    def __init__(self, xs):
        self.ins = list(xs)
        n = self.n = len(xs)
        self.out_shapes = [jax.ShapeDtypeStruct((N_DEV,) + a.shape, a.dtype) for a in xs]
        self.scratch = [pltpu.SemaphoreType.DMA((n, 7)), pltpu.SemaphoreType.DMA((n, 7)), pltpu.SemaphoreType.DMA((n,))]

    def _parts(self, ins, outs, sems):
        send_sems, recv_sems, local_sems = sems
        x, y, c = _place()
        me, sib = (x, y, c), (x, y, 1 - c)
        chips = [(1 - x, y), (x, 1 - y), (1 - x, 1 - y)]

        def copy(a, k, block, to, src=None):
            dst = outs[a].at[_slot(block)]
            return pltpu.make_async_remote_copy(
                src_ref=dst if src is None else src, dst_ref=dst, send_sem=send_sems.at[a, k],
                recv_sem=recv_sems.at[a, k], device_id=to, device_id_type=MESH)

        mine = [pltpu.make_async_copy(ins[a], outs[a].at[_slot(me)], local_sems.at[a]) for a in range(self.n)]
        first = []
        for a in range(self.n):
            first.append(copy(a, 0, me, sib, src=ins[a]))
            first += [copy(a, 1 + j, me, (*chip, c), src=ins[a]) for j, chip in enumerate(chips)]
        return copy, mine, first, me, sib, chips, c

    def start(self, ins, outs, sems):
        _, mine, first, *_ = self._parts(ins, outs, sems)
        for cp in mine + first:
            cp.start()

    def finish(self, ins, outs, sems):
        copy, mine, first, me, sib, chips, c = self._parts(ins, outs, sems)
        passed = []
        for j, chip in enumerate(chips):
            for a in range(self.n):
                copy(a, 1 + j, (*chip, c), me).wait_recv()
                fwd = copy(a, 4 + j, (*chip, c), sib)
                fwd.start()
                passed.append(fwd)
        for a in range(self.n):
            copy(a, 0, sib, me).wait_recv()
            for j, chip in enumerate(chips):
                copy(a, 4 + j, (*chip, 1 - c), me).wait_recv()
        for cp in first + passed:
            cp.wait_send()
        for cp in mine:
            cp.wait()


def _run_alone(ex, name):
    def body(*refs):
        ni, no = len(ex.ins), len(ex.out_shapes)
        parts = refs[:ni], refs[ni:ni + no], refs[ni + no:]
        ex.start(*parts)
        ex.finish(*parts)
    return pl.pallas_call(body, name=name, in_specs=[ANY] * len(ex.ins), out_specs=[ANY] * len(ex.out_shapes),
                          out_shape=ex.out_shapes, scratch_shapes=ex.scratch)(*ex.ins)


def _ride(body, n_in, n_out, ex, first, last):
    ni, no = len(ex.ins), len(ex.out_shapes)

    def wrapped(*refs):
        a, r_in = refs[:n_in], refs[n_in:n_in + ni]
        b, r_out = refs[n_in + ni:n_in + ni + n_out], refs[n_in + ni + n_out:n_in + ni + n_out + no]
        rest = refs[n_in + ni + n_out + no:]
        s, r_s = rest[:len(rest) - len(ex.scratch)], rest[len(rest) - len(ex.scratch):]

        @pl.when(first())
        def _():
            ex.start(r_in, r_out, r_s)
        body(*a, *b, *s)

        @pl.when(last())
        def _():
            ex.finish(r_in, r_out, r_s)
    return wrapped


def all_gather(xs, name):
    return _run_alone(_Gather(xs), name)


class _Exchange:
    def __init__(self, groups):
        self.ins = [a for grp in groups for a in grp]
        self.where = [(g, i) for g, grp in enumerate(groups) for i in range(len(grp))]
        n = self.n = len(self.ins)
        self.out_shapes = [jax.ShapeDtypeStruct((N_DEV, len(grp)) + grp[0].shape[1:], grp[0].dtype) for grp in groups]
        self.scratch = [pltpu.SemaphoreType.DMA((n, 7)), pltpu.SemaphoreType.DMA((n, 7)), pltpu.SemaphoreType.DMA((n,))]

    def _parts(self, ins, outs, sems):
        send_sems, recv_sems, local_sems = sems
        x, y, c = _place()
        me = (x, y, c)

        def peer(r):
            return (1 - x if r & 4 else x, 1 - y if r & 2 else y, 1 - c if r & 1 else c)

        def land(a, src):
            g, i = self.where[a]
            return outs[g].at[_slot(src), i]

        def copy(a, r, src_dev):
            p = peer(r)
            return pltpu.make_async_remote_copy(
                src_ref=ins[a].at[_slot(p)], dst_ref=land(a, src_dev), send_sem=send_sems.at[a, r - 1],
                recv_sem=recv_sems.at[a, r - 1], device_id=p, device_id_type=MESH)

        mine = [pltpu.make_async_copy(ins[a].at[_slot(me)], land(a, me), local_sems.at[a]) for a in range(self.n)]
        sends = [copy(a, r, me) for r in range(1, N_DEV) for a in range(self.n)]
        arrivals = [copy(a, r, peer(r)) for r in range(1, N_DEV) for a in range(self.n)]
        return mine, sends, arrivals

    def start(self, ins, outs, sems):
        mine, sends, _ = self._parts(ins, outs, sems)
        for cp in mine + sends:
            cp.start()

    def finish(self, ins, outs, sems):
        mine, sends, arrivals = self._parts(ins, outs, sems)
        for cp in arrivals:
            cp.wait_recv()
        for cp in sends:
            cp.wait_send()
        for cp in mine:
            cp.wait()


def all_to_all(groups, name):
    return _run_alone(_Exchange(groups), name)


def _mods(mod, l, D):
    return [mod[l:l + 1, k * D:(k + 1) * D] for k in range(6)]


def _ffn_backward(l, dxo, dy, hb, u, uc, xin, sc2, gains, W, post, ex=None):
    _, NJ, T, fs = u.shape
    duc, d_wdn = ffn_bwd_a(dy, uc, W["wdn"][l], f"ffn{l}_bwd_a")
    dh, taps, d_wup, *rider = ffn_bwd_b(duc.reshape(2 * NJ, T, fs), u.reshape(2 * NJ, T, fs), hb, W["fdw_w"][l],
                                        W["wup"][l], f"ffn{l}_bwd_b", ex)
    dx, da, dsh, *below = resid_bwd(f"ffn{l}_resid_bwd", pre=(dh, xin, dxo, gains["pre_ffn_g"][l:l + 1], sc2), post=post)
    small = dict(da=da, dsh=dsh, dwb=taps[:, 3], dww=taps[:, 0:3])
    return dx, d_wup, d_wdn, small, below, (rider[0] if rider else None)


def _local_step(xt, tgt, mod, gains, W, mine):
    T, D = xt.shape
    zero_d = jnp.zeros((1, D), F32)
    half = lambda g: g.reshape(N_DEV // 2, 2 * g.shape[1], D)
    sh1, sc1, g1, sh2, sc2, g2 = m0 = _mods(mod, 0, D)
    h0 = prenorm_mod(xt, gains["pre_mix_g"][0:1], sc1, sh1, "l0_prenorm")
    proj, (whout, wup0, wdn0) = mm_nn(h0, W["whin"], jnp.zeros((1, 4 * D), F32), "hgrn_proj",
                                      _Gather([mine["whout"], mine["wup"][0], mine["wdn"][0]]))
    o, og, st, (wcin, wcout, wup1, wdn1) = hgrn_fwd(
        proj, W["lb"], W["gg"], "hgrn_fwd", _Gather([mine["wcin"], mine["wcout"], mine["wup"][1], mine["wdn"][1]]))
    W = dict(W, whout=whout.reshape(D, D), wcin=wcin, wcout=wcout.reshape(D, D), wup=[wup0, wup1],
             wdn=[half(wdn0), half(wdn1)])
    t1, c1, e1, t2, c2, e2 = m1 = _mods(mod, 1, D)
    y0, x1, h0f = mm_post(og, W["whout"], zero_d, xt, g1, gains["post_mix_g"][0:1],
                          (gains["pre_ffn_g"][0:1], sc2, sh2), "hgrn_out")
    u0, uc0, yf0, x2, h1 = ffn_fwd(h0f, W["wup"][0], W["fdw_w"][0], W["fdw_b"][0], W["wdn"][0], x1, g2,
                                   gains["post_ffn_g"][0:1], (gains["pre_mix_g"][1:2], c1, t1), "ffn0_fwd")
    uc = mm_nn(h1, W["wcin"], W["cb_in"], "conv_in")
    cv, va = conv_mid_fwd(uc, W["cw"], W["cdw_b"], W["ln_g"], W["ln_b"], "conv_mid_fwd")
    y1, x3, h1f = mm_post(va, W["wcout"], W["cb_out"], x2, e1, gains["post_mix_g"][1:2],
                          (gains["pre_ffn_g"][1:2], c2, t2), "conv_out")
    u1, uc1, yf1, x4 = ffn_fwd(h1f, W["wup"][1], W["fdw_w"][1], W["fdw_b"][1], W["wdn"][1], x3, e2,
                               gains["post_ffn_g"][1:2], None, "ffn1_fwd")
    dx4, loss_part, dyf1, dgate_f1, dpost_f1, _ = resid_bwd(
        "loss_resid_bwd", loss=(x4, tgt), post=(yf1, e2, gains["post_ffn_g"][1:2]))
    dx3, d_wup1, d_wdn1, sf1, (dy1, dgate_c, dpost_c, dys_c), _ = _ffn_backward(
        1, dx4, dyf1, h1f, u1, uc1, x3, c2, gains, W, (y1, e1, gains["post_mix_g"][1:2]))
    dva = mm_nt(dy1, W["wcout"][None], "conv_dva", True)
    d_wcout = mm_tn(va, dy1, "conv_dwout", True, True, D, D)
    dcv, dlg, dlbias, taps = conv_bwd_a(dva, cv, uc, W["ln_g"], W["ln_b"], "conv_bwd_a")
    duc, dbin = conv_bwd_b(dcv, uc, W["cw"], "conv_bwd_b")
    dh1 = mm_nt(duc, W["wcin"], "conv_dh", True)
    d_wcin = mm_tn(h1, duc, "conv_dwin", True, True, D, W["wcin"].shape[2])
    dx2, da_c, dsh_c, dyf0, dgate_f0, dpost_f0, _ = resid_bwd(
        "l1_resid_bwd", pre=(dh1, x2, dx3, gains["pre_mix_g"][1:2], c1), post=(yf0, g2, gains["post_ffn_g"][0:1]))
    rows = lambda g: g.reshape(N_DEV, -1, D)
    dx1, d_wup0, d_wdn0, sf0, (dy0, dgate_h, dpost_h, _), (r_wcin, r_wcout, r_wup1, r_wdn1) = _ffn_backward(
        0, dx2, dyf0, h0f, u0, uc0, x1, sc2, gains, W, (y0, g1, gains["post_mix_g"][0:1]),
        _Exchange([[d_wcin], [rows(d_wcout)], [d_wup1], [rows(d_wdn1)]]))
    dog = mm_nt(dy0, W["whout"][None], "hgrn_dog", True)
    d_whout = mm_tn(og, dy0, "hgrn_dwout", True, True, D, D)
    dproj, dlb, dgg, (r_wup0, r_wdn0, r_whout) = hgrn_bwd(
        proj, o, dog, st, W["lb"], W["gg"], "hgrn_bwd", _Exchange([[d_wup0], [rows(d_wdn0)], [rows(d_whout)]]))
    d_whin = mm_tn(h0, dproj, "hgrn_dwin", True, True, D, W["whin"].shape[2])
    dh0, (r_whin,) = mm_nt(dproj, W["whin"], "hgrn_dh", True, _Exchange([[d_whin]]))
    dx0, da_h, dsh_h = resid_bwd("l0_resid_bwd", pre=(dh0, xt, dx1, gains["pre_mix_g"][0:1], sc1))
    sf0.update(dgate=dgate_f0, dpost=dpost_f0)
    sf1.update(dgate=dgate_f1, dpost=dpost_f1)

    big = dict(hgrn_w_in=r_whin, hgrn_w_out=r_whout, conv_w_in=r_wcin, conv_w_out=r_wcout,
               ffn_w_up=[r_wup0, r_wup1], ffn_w_down=[r_wdn0, r_wdn1])
    pm, pf = gains["pre_mix_g"], gains["pre_ffn_g"]
    dmod = jnp.concatenate([
        dsh_h, da_h * pm[0:1], dgate_h, sf0["dsh"], sf0["da"] * pf[0:1], sf0["dgate"],
        dsh_c, da_c * pm[1:2], dgate_c, sf1["dsh"], sf1["da"] * pf[1:2], sf1["dgate"]], axis=1)
    NCH = D // LANES
    tap = taps.reshape(NCH, 32, 8, LANES).sum(axis=2)
    small = dict(
        dmod=dmod,
        pre_mix_g=jnp.concatenate([da_h * (1.0 + sc1), da_c * (1.0 + c1)], axis=0),
        post_mix_g=jnp.concatenate([dpost_h, dpost_c], axis=0),
        pre_ffn_g=jnp.concatenate([sf0["da"] * (1.0 + sc2), sf1["da"] * (1.0 + c2)], axis=0),
        post_ffn_g=jnp.concatenate([sf0["dpost"], sf1["dpost"]], axis=0),
        lb=dlb, gg=dgg,
        ffn_dw_b=jnp.stack([sf0["dwb"], sf1["dwb"]]),
        conv_b_in=dbin,
        conv_dw_w=jnp.transpose(tap[:, :31], (1, 0, 2)).reshape(31, D),
        conv_dw_b=tap[:, 31].reshape(1, D),
        conv_ln_g=dlg, conv_ln_b=dlbias, conv_b_out=dys_c,
        ffn_dw_w=jnp.stack([sf0["dww"], sf1["dww"]]))
    return loss_part, dx0, big, small


SMALL_ORDER = ["dmod", "pre_mix_g", "post_mix_g", "pre_ffn_g", "post_ffn_g", "lb", "gg", "ffn_dw_b", "conv_b_in",
               "conv_dw_w", "conv_dw_b", "conv_ln_g", "conv_ln_b", "conv_b_out", "ffn_dw_w"]


def _pack(parts):
    flat = jnp.concatenate([p.reshape(-1) for p in parts])
    pad = (-flat.shape[0]) % LANES
    if pad:
        flat = jnp.concatenate([flat, jnp.zeros((pad,), F32)])
    return flat.reshape(-1, LANES)


def _unpack(flat, shapes):
    out, off = [], 0
    for s in shapes:
        n = 1
        for d in s:
            n *= d
        out.append(flat[off:off + n].reshape(s))
        off += n
    return out


def _apply_adamw(name, w, g_slots, m, v):
    shp = w.shape
    C = shp[-1]
    R = w.size // C
    parts = g_slots if isinstance(g_slots, list) else [g_slots]
    parts = [p.reshape(p.shape[0], R // len(parts), C) for p in parts]
    g, d, mn, vn = adamw(w.reshape(R, C), parts, m.reshape(R, C), v.reshape(R, C), "adamw_" + name)
    return g.reshape(shp), d.reshape(shp), mn.reshape(shp), vn.reshape(shp)


WEIGHTS = ['ada_w', 'ada_b', 'pre_mix_g', 'post_mix_g', 'pre_ffn_g', 'post_ffn_g', 'hgrn_w_in', 'hgrn_lb_logits',
           'hgrn_gnorm_g', 'hgrn_w_out', 'conv_w_in', 'conv_b_in', 'conv_dw_w', 'conv_dw_b', 'conv_ln_g', 'conv_ln_b',
           'conv_w_out', 'conv_b_out', 'ffn_w_up', 'ffn_dw_w', 'ffn_dw_b', 'ffn_w_down']


def kernel(x, c, ada_w, ada_b, pre_mix_g, post_mix_g, pre_ffn_g, post_ffn_g, hgrn_w_in, hgrn_lb_logits, hgrn_gnorm_g, hgrn_w_out, conv_w_in, conv_b_in, conv_dw_w, conv_dw_b, conv_ln_g, conv_ln_b, conv_w_out, conv_b_out, ffn_w_up, ffn_dw_w, ffn_dw_b, ffn_w_down, loss_target, m_ada_w, m_ada_b, m_pre_mix_g, m_post_mix_g, m_pre_ffn_g, m_post_ffn_g, m_hgrn_w_in, m_hgrn_lb_logits, m_hgrn_gnorm_g, m_hgrn_w_out, m_conv_w_in, m_conv_b_in, m_conv_dw_w, m_conv_dw_b, m_conv_ln_g, m_conv_ln_b, m_conv_w_out, m_conv_b_out, m_ffn_w_up, m_ffn_dw_w, m_ffn_dw_b, m_ffn_w_down, v_ada_w, v_ada_b, v_pre_mix_g, v_post_mix_g, v_pre_ffn_g, v_post_ffn_g, v_hgrn_w_in, v_hgrn_lb_logits, v_hgrn_gnorm_g, v_hgrn_w_out, v_conv_w_in, v_conv_b_in, v_conv_dw_w, v_conv_dw_b, v_conv_ln_g, v_conv_ln_b, v_conv_w_out, v_conv_b_out, v_ffn_w_up, v_ffn_dw_w, v_ffn_dw_b, v_ffn_w_down):
    P = dict(locals())
    _, T, D = x.shape
    me = _slot(_place())
    L = ada_w.shape[0]
    na = ada_w.shape[2]
    fs = ffn_w_up.shape[2]
    nci = conv_w_in.shape[2]
    nd = conv_dw_b.shape[1]
    NCH = D // LANES

    small_in = [c, conv_b_in, conv_dw_w, conv_dw_b, conv_ln_g, conv_ln_b, conv_b_out, ffn_dw_w]
    whin, packed = all_gather([_bf(hgrn_w_in[0]), _pack(small_in)], "gather_first")
    mine = dict(whout=_bf(hgrn_w_out[0]), wcin=_bf(conv_w_in[0]), wcout=_bf(conv_w_out[0]),
                wup=[_bf(ffn_w_up[l]) for l in range(L)], wdn=[_bf(ffn_w_down[l]) for l in range(L)])
    sm = packed.reshape(N_DEV, -1)
    offs = [0]
    for a in small_in:
        offs.append(offs[-1] + a.size)
    piece = lambda k, shp: sm[:, offs[k]:offs[k + 1]].reshape((N_DEV,) + shp)
    c_all = piece(0, (D,))
    dw_nat = jnp.transpose(piece(2, (31, nd)), (1, 0, 2)).reshape(31, D)
    cw = jnp.pad(jnp.transpose(dw_nat.reshape(31, NCH, LANES), (1, 0, 2)), ((0, 0), (0, 1), (0, 0)))
    fdw = piece(7, (L, 3, fs))
    lb_soft = lb_softmax(hgrn_lb_logits, "lb_softmax")
    W = dict(
        whin=whin, lb=lb_soft[0:1], gg=hgrn_gnorm_g, cb_in=piece(1, (nci,)).reshape(1, N_DEV * nci), cw=cw,
        cdw_b=piece(3, (nd,)).reshape(1, D), ln_g=piece(4, (nd,)).reshape(1, D), ln_b=piece(5, (nd,)).reshape(1, D),
        cb_out=piece(6, (nd,)).reshape(1, D), fdw_w=[fdw[:, l] for l in range(L)],
        fdw_b=[ffn_dw_b[l].reshape(N_DEV, 1, fs) for l in range(L)])

    bias_mine = lax.dynamic_slice(ada_b, (0, me * na), (L, na)).reshape(L, 1, na)
    mod_cols = ada_fwd(c_all, ada_w, bias_mine, "ada_fwd")
    (mod_all,) = all_gather([mod_cols], "gather_mod")
    mod = jnp.transpose(lax.dynamic_index_in_dim(mod_all, me, axis=2, keepdims=False), (1, 0, 2)).reshape(L, N_DEV * na)

    gains = dict(pre_mix_g=pre_mix_g, post_mix_g=post_mix_g, pre_ffn_g=pre_ffn_g, post_ffn_g=post_ffn_g)
    loss_part, dx0, big, small = _local_step(x[0], loss_target[0], mod, gains, W, mine)
    loss = lax.psum((0.5 / D) * jnp.sum(loss_part), ("x", "y", "c"))

    parts = [small[k] for k in SMALL_ORDER]
    (sm_all,) = all_gather([_pack(parts)], "gather_small_grads")
    tot = _unpack(sum_slots(sm_all, "sum_small_grads").reshape(-1), [p.shape for p in parts])
    tot = dict(zip(SMALL_ORDER, tot))
    dmod_all = sm_all.reshape(N_DEV, -1)[:, :L * 6 * D].reshape(N_DEV, L, 6 * D)
    dm_mine = jnp.transpose(lax.dynamic_slice(dmod_all, (0, 0, me * na), (N_DEV, L, na)), (1, 0, 2))
    s0 = lb_soft[0:1]
    onehot = (lax.broadcasted_iota(jnp.int32, lb_soft.shape, 0) == 0).astype(F32)
    col = lambda a, n: lax.dynamic_slice_in_dim(a, me * n, n, axis=a.ndim - 1)
    G = {
        'ada_w': ada_bwd(c_all, dm_mine, "ada_bwd")[None],
        'ada_b': tot["dmod"].reshape(1, L, 6 * D),
        'pre_mix_g': tot["pre_mix_g"][None], 'post_mix_g': tot["post_mix_g"][None],
        'pre_ffn_g': tot["pre_ffn_g"][None], 'post_ffn_g': tot["post_ffn_g"][None],
        'hgrn_lb_logits': (tot["lb"] * s0 * (onehot - lb_soft))[None],
        'hgrn_gnorm_g': tot["gg"][None],
        'ffn_dw_b': tot["ffn_dw_b"].reshape(1, L, N_DEV * fs),
        'conv_b_in': col(tot["conv_b_in"], nci)[None],
        'conv_dw_w': col(tot["conv_dw_w"], nd)[None, None],
        'conv_dw_b': col(tot["conv_dw_b"], nd)[None], 'conv_ln_g': col(tot["conv_ln_g"], nd)[None],
        'conv_ln_b': col(tot["conv_ln_b"], nd)[None], 'conv_b_out': col(tot["conv_b_out"], nd)[None],
        'ffn_dw_w': lax.dynamic_index_in_dim(tot["ffn_dw_w"], me, axis=1, keepdims=False)[None],
    }

    G.update(big)
    res = {n: _apply_adamw(n, P[n], G[n], P["m_" + n], P["v_" + n]) for n in WEIGHTS}
    return (loss, dx0[None], *[res[n][0] for n in WEIGHTS], *[res[n][1] for n in WEIGHTS],
            *[res[n][2] for n in WEIGHTS], *[res[n][3] for n in WEIGHTS])
```

```python
import functools

import jax
import jax.numpy as jnp
from jax import lax
from jax.experimental import pallas as pl
from jax.experimental.pallas import tpu as pltpu

F32 = jnp.float32
BF16 = jnp.bfloat16
EPS = 1e-6
LANES = 128
N_DEV = 8
VMEM_LIMIT = 48 * 1024 * 1024
FFN_BWD_VMEM = 58 * 1024 * 1024
MM_ROWS = 1024
HG_CHUNK = 128
HG_SUB = 32
EXP_CLAMP = 80.0
CONV_HALO = 32
FFN_HALO = 8
CONV_ROWS = 32
ADAM_LR, ADAM_B1, ADAM_B2, ADAM_EPS, ADAM_WD, ADAM_STEP = 0.001, 0.9, 0.999, 1e-08, 0.01, 10
MESH = pl.DeviceIdType.MESH
ANY = pl.BlockSpec(memory_space=pl.ANY)


def _cp(*sem):
    return pltpu.CompilerParams(dimension_semantics=sem, vmem_limit_bytes=VMEM_LIMIT)


def _rt(n, t):
    t = min(n, t)
    assert n % t == 0, (n, t)
    return t


def _sig(x):
    return jax.nn.sigmoid(x)


def _nt(a, b):
    return lax.dot_general(a, b, (((1,), (1,)), ((), ())), preferred_element_type=F32)


def _tn(a, b):
    return lax.dot_general(a, b, (((0,), (0,)), ((), ())), preferred_element_type=F32)


def _nn(a, b):
    return jnp.dot(a, b, preferred_element_type=F32)


def _bf(x):
    return x.astype(BF16)


def _nn_st(a, b):
    return _nn(_bf(a), _bf(b))


def _tn_st(a, b):
    return _tn(_bf(a), _bf(b))


def _row(d):
    return pl.BlockSpec((1, d), lambda *_: (0, 0))


def prenorm_mod(x, g, sc, sh, name, ex=None):
    T, D = x.shape
    tm = _rt(T, 512)

    def body(x_ref, g_ref, sc_ref, sh_ref, h_ref):
        h_ref[...] = _next_input(x_ref[...], g_ref, sc_ref, sh_ref)

    tile = pl.BlockSpec((tm, D), lambda i: (i, 0))
    (h,), rider = _call(body, name, (T // tm,), [tile, _row(D), _row(D), _row(D)], [tile],
                        [jax.ShapeDtypeStruct((T, D), BF16)], [], ("parallel",), (x, g, sc, sh), ex)
    return (h, rider) if ex is not None else h


def resid_bwd(name, pre=None, loss=None, post=None):
    T, D = (pre[1] if pre is not None else loss[0]).shape
    tm = _rt(T, 512)
    n_src = 5 if pre is not None else 2
    n_in = n_src + (3 if post is not None else 0)

    def body(*refs):
        ins, outs = refs[:n_in], refs[n_in:]
        first = pl.program_id(0) == 0
        if pre is not None:
            dh_ref, x_ref, dr_ref, g_ref, sc_ref = ins[:5]
            dx_ref, da_ref, dsh_ref = outs[:3]

            @pl.when(first)
            def _():
                da_ref[...] = jnp.zeros_like(da_ref)
                dsh_ref[...] = jnp.zeros_like(dsh_ref)
            xv = x_ref[...]
            dh = dh_ref[...]
            r = lax.rsqrt(jnp.mean(xv * xv, axis=-1, keepdims=True) + EPS)
            n = xv * r
            da_ref[...] += jnp.sum(dh * n, axis=0, keepdims=True)
            dsh_ref[...] += jnp.sum(dh, axis=0, keepdims=True)
            dn = dh * (g_ref[...] * (1.0 + sc_ref[...]))
            dx = dr_ref[...] + r * (dn - n * jnp.mean(dn * n, axis=-1, keepdims=True))
            dx_ref[...] = dx
            rest = outs[3:]
        else:
            x_ref, t_ref = ins[:2]
            dx_ref, part_ref = outs[:2]

            @pl.when(first)
            def _():
                part_ref[...] = jnp.zeros_like(part_ref)
            e = x_ref[...] - t_ref[...]
            dx = e * (1.0 / D)
            dx_ref[...] = dx
            e2 = (e * e).reshape(tm // 8, 8, D).sum(axis=0)
            acc = e2[:, 0:LANES]
            for j in range(1, D // LANES):
                acc = acc + e2[:, j * LANES:(j + 1) * LANES]
            part_ref[...] += acc
            rest = outs[2:]
        if post is not None:
            y_ref, gt_ref, pg_ref = ins[n_src:]
            dy_ref, dgate_ref, dpg_ref, dys_ref = rest

            @pl.when(first)
            def _():
                dgate_ref[...] = jnp.zeros_like(dgate_ref)
                dpg_ref[...] = jnp.zeros_like(dpg_ref)
                dys_ref[...] = jnp.zeros_like(dys_ref)
            yv = y_ref[...]
            ry = lax.rsqrt(jnp.mean(yv * yv, axis=-1, keepdims=True) + EPS)
            ny = yv * ry
            s = jnp.sum(dx * ny, axis=0, keepdims=True)
            dgate_ref[...] += s * pg_ref[...]
            dpg_ref[...] += s * gt_ref[...]
            dny = dx * (gt_ref[...] * pg_ref[...])
            dy = ry * (dny - ny * jnp.mean(dny * ny, axis=-1, keepdims=True))
            dy_ref[...] = _bf(dy)
            dys_ref[...] += jnp.sum(dy, axis=0, keepdims=True)

    tile = pl.BlockSpec((tm, D), lambda i: (i, 0))
    vec = jax.ShapeDtypeStruct((1, D), F32)
    big = jax.ShapeDtypeStruct((T, D), F32)
    if pre is not None:
        operands, in_specs = list(pre), [tile, tile, tile, _row(D), _row(D)]
        out_specs, out_shape = [tile, _row(D), _row(D)], [big, vec, vec]
    else:
        operands, in_specs = list(loss), [tile, tile]
        out_specs, out_shape = [tile, pl.BlockSpec((8, LANES), lambda i: (0, 0))], [big, jax.ShapeDtypeStruct((8, LANES), F32)]
    if post is not None:
        operands, in_specs = operands + list(post), in_specs + [tile, _row(D), _row(D)]
        out_specs = out_specs + [tile, _row(D), _row(D), _row(D)]
        out_shape = out_shape + [jax.ShapeDtypeStruct((T, D), BF16), vec, vec, vec]
    return pl.pallas_call(body, name=name, grid=(T // tm,), in_specs=in_specs, out_specs=out_specs, out_shape=out_shape,
                          compiler_params=_cp("arbitrary"))(*operands)


def _call(body, name, grid, in_specs, out_specs, out_shape, scratch, sems, operands, ex=None, vmem=VMEM_LIMIT):
    if ex is not None:
        def first():
            return functools.reduce(lambda p, q: p & q, [pl.program_id(k) == 0 for k in range(len(grid))])

        def last():
            return functools.reduce(lambda p, q: p & q, [pl.program_id(k) == grid[k] - 1 for k in range(len(grid))])
        body = _ride(body, len(in_specs), len(out_specs), ex, first, last)
        in_specs = in_specs + [ANY] * len(ex.ins)
        out_specs = out_specs + [ANY] * len(ex.out_shapes)
        out_shape = out_shape + ex.out_shapes
        scratch = scratch + ex.scratch
        operands = list(operands) + ex.ins
        sems = ("arbitrary",) * len(grid)
    res = pl.pallas_call(body, name=name, grid=grid, in_specs=in_specs, out_specs=out_specs, out_shape=out_shape,
                         scratch_shapes=scratch,
                         compiler_params=pltpu.CompilerParams(dimension_semantics=sems, vmem_limit_bytes=vmem))(*operands)
    n_own = len(res) - (len(ex.out_shapes) if ex is not None else 0)
    return res[:n_own], res[n_own:]


def mm_nn(a, bg, bias, name, ex=None):
    M, K = a.shape
    G, _, n = bg.shape
    tm = _rt(M, MM_ROWS)

    def body(a_ref, b_ref, bias_ref, o_ref):
        o_ref[...] = _nn(a_ref[...], b_ref[...]) + bias_ref[...]

    (out,), rider = _call(
        body, name, (M // tm, G),
        [pl.BlockSpec((tm, K), lambda i, j: (i, 0)), pl.BlockSpec((None, K, n), lambda i, j: (j, 0, 0)),
         pl.BlockSpec((1, n), lambda i, j: (0, j))],
        [pl.BlockSpec((tm, n), lambda i, j: (i, j))], [jax.ShapeDtypeStruct((M, G * n), F32)], [],
        ("parallel", "arbitrary"), (a, bg, bias), ex)
    return (out, rider) if ex is not None else out


def _next_input(xo, g_ref, sc_ref, sh_ref):
    r = lax.rsqrt(jnp.mean(xo * xo, axis=-1, keepdims=True) + EPS)
    return _bf((xo * r) * g_ref[...] * (1.0 + sc_ref[...]) + sh_ref[...])


def mm_post(a, b, bias, x, gate, pg, nxt, name):
    T, K = a.shape
    D = b.shape[1]
    tm = _rt(T, 512)

    def body(a_ref, b_ref, bias_ref, x_ref, gt_ref, pg_ref, g_ref, sc_ref, sh_ref, y_ref, xo_ref, h_ref):
        y = _nn(a_ref[...], b_ref[...]) + bias_ref[...]
        y_ref[...] = y
        r = lax.rsqrt(jnp.mean(y * y, axis=-1, keepdims=True) + EPS)
        xo = x_ref[...] + gt_ref[...] * ((y * r) * pg_ref[...])
        xo_ref[...] = xo
        h_ref[...] = _next_input(xo, g_ref, sc_ref, sh_ref)

    tile = pl.BlockSpec((tm, D), lambda i: (i, 0))
    out = jax.ShapeDtypeStruct((T, D), F32)
    return pl.pallas_call(
        body, name=name, grid=(T // tm,),
        in_specs=[pl.BlockSpec((tm, K), lambda i: (i, 0)), pl.BlockSpec((K, D), lambda i: (0, 0)), _row(D), tile]
        + [_row(D)] * 5,
        out_specs=[tile, tile, tile], out_shape=[out, out, jax.ShapeDtypeStruct((T, D), BF16)],
        compiler_params=_cp("parallel"))(a, b, bias, x, gate, pg, *nxt)


def mm_nt(dy, wg, name, natural, ex=None):
    G, K, n = wg.shape
    T = dy.shape[0] if natural else dy.shape[1]
    tm = _rt(T, MM_ROWS)

    def body(dy_ref, w_ref, o_ref, acc):
        j = pl.program_id(1)

        @pl.when(j == 0)
        def _():
            acc[...] = jnp.zeros_like(acc)
        acc[...] += _nt(_bf(dy_ref[...]), w_ref[...])

        @pl.when(j == G - 1)
        def _():
            o_ref[...] = acc[...]

    dspec = (pl.BlockSpec((tm, n), lambda i, j: (i, j)) if natural
             else pl.BlockSpec((None, tm, n), lambda i, j: (j, i, 0)))
    (out,), rider = _call(
        body, name, (T // tm, G), [dspec, pl.BlockSpec((None, K, n), lambda i, j: (j, 0, 0))],
        [pl.BlockSpec((tm, K), lambda i, j: (i, 0))], [jax.ShapeDtypeStruct((T, K), F32)],
        [pltpu.VMEM((tm, K), F32)], ("parallel", "arbitrary"), (dy, wg), ex)
    return (out, rider) if ex is not None else out


def mm_tn(xm, gm, name, x_natural, g_natural, kx, n):
    if x_natural:
        T, Gx = xm.shape[0], xm.shape[1] // kx
    else:
        Gx, T = xm.shape[0], xm.shape[1]
    tt = _rt(T, 2 * MM_ROWS)
    if x_natural:
        xspec = pl.BlockSpec((tt, kx), lambda g, k, t: (t, k))
    else:
        xspec = pl.BlockSpec((None, tt, kx), lambda g, k, t: (k, t, 0))
    if g_natural:
        Gg = gm.shape[1] // n
        gspec = pl.BlockSpec((tt, n), lambda g, k, t: (t, g))
    else:
        Gg = gm.shape[0]
        gspec = pl.BlockSpec((None, tt, n), lambda g, k, t: (g, t, 0))

    NT = T // tt

    def body(x_ref, g_ref, o_ref, acc):
        t = pl.program_id(2)

        @pl.when(t == 0)
        def _():
            acc[...] = jnp.zeros_like(acc)
        acc[...] += _tn(_bf(x_ref[...]), _bf(g_ref[...]))

        @pl.when(t == NT - 1)
        def _():
            o_ref[...] = acc[...].astype(o_ref.dtype)

    return pl.pallas_call(
        body, name=name, grid=(Gg, Gx, NT), in_specs=[xspec, gspec],
        out_specs=pl.BlockSpec((None, kx, n), lambda g, k, t: (g, k, 0)),
        out_shape=jax.ShapeDtypeStruct((Gg, Gx * kx, n), BF16), scratch_shapes=[pltpu.VMEM((kx, n), F32)],
        compiler_params=_cp("parallel", "parallel", "arbitrary"))(xm, gm)


def _split3(x):
    h = _bf(x)
    r = x - h.astype(F32)
    m = _bf(r)
    lo = _bf(r - m.astype(F32))
    return h, m, lo


def _tri_mm(tri, x):
    h, m, lo = _split3(x)
    return _nn(tri, lo) + _nn(tri, m) + _nn(tri, h)


def _hgrn_gates(qr, fz, lb):
    sq = _sig(qr)
    q = qr * sq
    sig = _sig(fz)
    f = lb + (1.0 - lb) * sig
    k = 1.0 - f
    return sq, q, sig, f, k, jnp.log(f)


def hgrn_fwd(proj, lb, gg, name, ex=None):
    T, D4 = proj.shape
    D = D4 // 4
    H = D // LANES
    C = _rt(T, HG_CHUNK)
    SB = min(HG_SUB, C)
    NC = T // C

    def body(q_ref, f_ref, v_ref, g_ref, lb_ref, gg_ref, o_ref, og_ref, st_ref, ST, bex):
        @pl.when(pl.program_id(0) == 0)
        def _():
            ST[...] = jnp.zeros_like(ST)
        r_i = lax.broadcasted_iota(jnp.int32, (C, C), 0)
        c_i = lax.broadcasted_iota(jnp.int32, (C, C), 1)
        low = _bf((r_i >= c_i).astype(F32))
        rs = lax.broadcasted_iota(jnp.int32, (SB, C), 0)
        cs = lax.broadcasted_iota(jnp.int32, (SB, C), 1)
        for hd in range(H):
            sl = slice(hd * LANES, (hd + 1) * LANES)
            _, q, _, _, k, lf = _hgrn_gates(q_ref[:, sl], f_ref[:, sl], lb_ref[:, sl])
            v = v_ref[:, sl]
            b = _tri_mm(low, lf)
            bex[hd] = b - lf
            bend = jnp.sum(lf, axis=0, keepdims=True)
            blocks = []
            for I in range(C // SB):
                p = bex[hd, pl.ds(SB * I + SB // 2, 1), :]
                rows = slice(SB * I, SB * (I + 1))
                qI = _bf(q[rows] * jnp.exp(jnp.minimum(b[rows] - p, EXP_CLAMP)))
                kI = _bf(k * jnp.exp(jnp.minimum(p - b, EXP_CLAMP)))
                blocks.append(jnp.where(rs + SB * I >= cs, _nt(qI, kI), 0.0))
            A = jnp.concatenate(blocks, axis=0)
            st0 = ST[hd]
            st_ref[0, hd] = st0
            o = _nn(_bf(A), _bf(v)) + _nt(_bf(q * jnp.exp(b)), _bf(st0))
            ST[hd] = st0 * jnp.exp(bend) + _tn_st(v, k * jnp.exp(bend - b))
            o_ref[:, sl] = o
            r = lax.rsqrt(jnp.mean(o * o, axis=-1, keepdims=True) + EPS)
            gr = g_ref[:, sl]
            og_ref[:, sl] = _bf((o * r) * gg_ref[...] * (gr * _sig(gr)))

    col = lambda j: pl.BlockSpec((C, D), lambda c, j=j: (c, j))
    tile = pl.BlockSpec((C, D), lambda c: (c, 0))
    own, rider = _call(
        body, name, (NC,), [col(0), col(1), col(2), col(3), _row(D), _row(LANES)],
        [tile, tile, pl.BlockSpec((1, H, LANES, LANES), lambda c: (c, 0, 0, 0))],
        [jax.ShapeDtypeStruct((T, D), F32), jax.ShapeDtypeStruct((T, D), BF16),
         jax.ShapeDtypeStruct((NC, H, LANES, LANES), F32)],
        [pltpu.VMEM((H, LANES, LANES), F32), pltpu.VMEM((H, C, LANES), F32)], ("arbitrary",),
        (proj, proj, proj, proj, lb, gg), ex)
    return (*own, rider) if ex is not None else own


def hgrn_bwd(proj, o, dog, st, lb, gg, name, ex=None):
    T, D4 = proj.shape
    D = D4 // 4
    H = D // LANES
    C = _rt(T, HG_CHUNK)
    SB = min(HG_SUB, C)
    NC = T // C

    def body(q_ref, f_ref, v_ref, g_ref, o_ref, dog_ref, st_ref, lb_ref, gg_ref, dp_ref, dlb_ref, dgg_ref,
             DST, bex):
        @pl.when(pl.program_id(0) == 0)
        def _():
            DST[...] = jnp.zeros_like(DST)
            dlb_ref[...] = jnp.zeros_like(dlb_ref)
            dgg_ref[...] = jnp.zeros_like(dgg_ref)
        r_i = lax.broadcasted_iota(jnp.int32, (C, C), 0)
        c_i = lax.broadcasted_iota(jnp.int32, (C, C), 1)
        causal = r_i >= c_i
        low = _bf(causal.astype(F32))
        upp = _bf((r_i <= c_i).astype(F32))
        rs = lax.broadcasted_iota(jnp.int32, (SB, C), 0)
        cs = lax.broadcasted_iota(jnp.int32, (SB, C), 1)
        ggv = gg_ref[...]
        for hd in range(H):
            sl = slice(hd * LANES, (hd + 1) * LANES)
            qr = q_ref[:, sl]
            lbv = lb_ref[:, sl]
            sq, q, sig, f, k, lf = _hgrn_gates(qr, f_ref[:, sl], lbv)
            v = v_ref[:, sl]
            oh = o_ref[:, sl]
            r = lax.rsqrt(jnp.mean(oh * oh, axis=-1, keepdims=True) + EPS)
            n = oh * r
            gr = g_ref[:, sl]
            sg = _sig(gr)
            dgo = dog_ref[:, sl]
            dgr = dgo * (n * ggv) * (sg * (1.0 + gr * (1.0 - sg)))
            don = dgo * (gr * sg)
            dgg_ref[...] += jnp.sum(don * n, axis=0, keepdims=True)
            dn = don * ggv
            do = r * (dn - n * jnp.mean(dn * n, axis=-1, keepdims=True))
            b = _tri_mm(low, lf)
            bex[hd] = b - lf
            bend = jnp.sum(lf, axis=0, keepdims=True)
            dob = _bf(do)
            dA = _bf(jnp.where(causal, _nt(dob, _bf(v)), 0.0))
            blocks, dqs, gqs = [], [], []
            dk = jnp.zeros((C, LANES), F32)
            gk = jnp.zeros((C, LANES), F32)
            for I in range(C // SB):
                p = bex[hd, pl.ds(SB * I + SB // 2, 1), :]
                rows = slice(SB * I, SB * (I + 1))
                eq = jnp.exp(jnp.minimum(b[rows] - p, EXP_CLAMP))
                qI = _bf(q[rows] * eq)
                ek = jnp.exp(jnp.minimum(p - b, EXP_CLAMP))
                kI = _bf(k * ek)
                blocks.append(jnp.where(rs + SB * I >= cs, _nt(qI, kI), 0.0))
                dqI = _nn(dA[rows], kI)
                dkI = _tn(dA[rows], qI)
                dqs.append(dqI * eq)
                gqs.append(qI.astype(F32) * dqI)
                dk = dk + dkI * ek
                gk = gk + kI.astype(F32) * dkI
            A = jnp.concatenate(blocks, axis=0)
            dst = DST[hd]
            st0 = st_ref[0, hd]
            eb = jnp.exp(b)
            ee = jnp.exp(bend - b)
            ebend = jnp.exp(bend)
            dv = _tn(_bf(A), dob) + _nt(_bf(k * ee), _bf(dst))
            dk_state = ee * _nn_st(v, dst)
            dq_state = eb * _nn_st(do, st0)
            dq = jnp.concatenate(dqs, axis=0) + dq_state
            dk = dk + dk_state
            DST[hd] = dst * ebend + _tn_st(do, q * eb)
            later = jnp.sum(k * dk_state, axis=0, keepdims=True) + ebend * jnp.sum(st0 * dst, axis=0, keepdims=True)
            pairs = jnp.concatenate(gqs, axis=0) - gk + (q * dq_state - k * dk_state)
            dlf = _tri_mm(upp, pairs) + later
            df = dlf / f - dk
            dlb_ref[:, sl] += jnp.sum(df * (1.0 - sig), axis=0, keepdims=True)
            dp_ref[:, sl] = _bf(dq * (sq * (1.0 + qr * (1.0 - sq))))
            dp_ref[:, D + hd * LANES:D + (hd + 1) * LANES] = _bf(df * (1.0 - lbv) * (sig * (1.0 - sig)))
            dp_ref[:, 2 * D + hd * LANES:2 * D + (hd + 1) * LANES] = _bf(dv)
            dp_ref[:, 3 * D + hd * LANES:3 * D + (hd + 1) * LANES] = _bf(dgr)

    col = lambda j: pl.BlockSpec((C, D), lambda c, j=j: (NC - 1 - c, j))
    tile = pl.BlockSpec((C, D), lambda c: (NC - 1 - c, 0))
    own, rider = _call(
        body, name, (NC,),
        [col(0), col(1), col(2), col(3), tile, tile,
         pl.BlockSpec((1, H, LANES, LANES), lambda c: (NC - 1 - c, 0, 0, 0)), _row(D), _row(LANES)],
        [pl.BlockSpec((C, D4), lambda c: (NC - 1 - c, 0)), _row(D), _row(LANES)],
        [jax.ShapeDtypeStruct((T, D4), BF16), jax.ShapeDtypeStruct((1, D), F32), jax.ShapeDtypeStruct((1, LANES), F32)],
        [pltpu.VMEM((H, LANES, LANES), F32), pltpu.VMEM((H, C, LANES), F32)],
        ("arbitrary",), (proj, proj, proj, proj, o, dog, st, lb, gg), ex)
    return (*own, rider) if ex is not None else own


def _prev_blk(tm, hb):
    return lambda i: (jnp.maximum(i * (tm // hb) - 1, 0), 0)


def _next_blk(tm, hb, T):
    return lambda i: (jnp.minimum((i + 1) * (tm // hb), T // hb - 1), 0)


def _glu_to_ext(uc_ref, ucp_ref, ext, D, first):
    def glu(u):
        return u[:, :D] * _sig(u[:, D:])
    gh = jnp.where(first, 0.0, glu(ucp_ref[...]))
    gm = glu(uc_ref[...])
    for c in range(D // LANES):
        ext[c, 0:CONV_HALO, :] = gh[:, c * LANES:(c + 1) * LANES]
        ext[c, CONV_HALO:, :] = gm[:, c * LANES:(c + 1) * LANES]


def conv_mid_fwd(uc, cw, cb, lg, lbias, name):
    T, D2 = uc.shape
    D = D2 // 2
    NCH = D // LANES
    W = 31
    tm = _rt(T, 256)
    RS = min(CONV_ROWS, tm)

    def body(uc_ref, ucp_ref, cw_ref, cb_ref, lg_ref, lb_ref, cv_ref, va_ref, ext, outs):
        _glu_to_ext(uc_ref, ucp_ref, ext, D, pl.program_id(0) == 0)

        def chunk(c, carry):
            for rb in range(tm // RS):
                acc = jnp.zeros((RS, LANES), F32)
                for kk in range(W):
                    acc = acc + cw_ref[c, pl.ds(kk, 1), :] * ext[c, pl.ds(rb * RS + CONV_HALO - (W - 1) + kk, RS), :]
                outs[c, pl.ds(rb * RS, RS), :] = acc
            return carry
        lax.fori_loop(0, NCH, chunk, 0)
        cv = jnp.concatenate([outs[c] for c in range(NCH)], axis=1) + cb_ref[...]
        cv_ref[...] = cv
        mu = jnp.mean(cv, axis=-1, keepdims=True)
        xc = cv - mu
        rstd = lax.rsqrt(jnp.mean(xc * xc, axis=-1, keepdims=True) + EPS)
        l = xc * rstd * lg_ref[...] + lb_ref[...]
        va_ref[...] = _bf(l * _sig(l))

    tile = pl.BlockSpec((tm, D), lambda i: (i, 0))
    return pl.pallas_call(
        body, name=name, grid=(T // tm,),
        in_specs=[pl.BlockSpec((tm, D2), lambda i: (i, 0)), pl.BlockSpec((CONV_HALO, D2), _prev_blk(tm, CONV_HALO)),
                  pl.BlockSpec((NCH, 32, LANES), lambda i: (0, 0, 0)), _row(D), _row(D), _row(D)],
        out_specs=[tile, tile],
        out_shape=[jax.ShapeDtypeStruct((T, D), F32), jax.ShapeDtypeStruct((T, D), BF16)],
        scratch_shapes=[pltpu.VMEM((NCH, tm + CONV_HALO, LANES), F32), pltpu.VMEM((NCH, tm, LANES), F32)],
        compiler_params=_cp("parallel"))(uc, uc, cw, cb, lg, lbias)


def conv_bwd_a(dva, cv, uc, lg, lbias, name):
    T, D2 = uc.shape
    D = D2 // 2
    NCH = D // LANES
    W = 31
    tm = _rt(T, 256)
    RS = min(CONV_ROWS, tm)

    def body(dva_ref, cv_ref, uc_ref, ucp_ref, lg_ref, lb_ref, dcv_ref, dlg_ref, dlb_ref, taps_ref, ext, dcs):
        @pl.when(pl.program_id(0) == 0)
        def _():
            dlg_ref[...] = jnp.zeros_like(dlg_ref)
            dlb_ref[...] = jnp.zeros_like(dlb_ref)
            taps_ref[...] = jnp.zeros_like(taps_ref)
        _glu_to_ext(uc_ref, ucp_ref, ext, D, pl.program_id(0) == 0)
        cv = cv_ref[...]
        mu = jnp.mean(cv, axis=-1, keepdims=True)
        xc = cv - mu
        rstd = lax.rsqrt(jnp.mean(xc * xc, axis=-1, keepdims=True) + EPS)
        yn = xc * rstd
        l = yn * lg_ref[...] + lb_ref[...]
        s = _sig(l)
        dl = dva_ref[...] * (s * (1.0 + l * (1.0 - s)))
        dlg_ref[...] += jnp.sum(dl * yn, axis=0, keepdims=True)
        dlb_ref[...] += jnp.sum(dl, axis=0, keepdims=True)
        dyn = dl * lg_ref[...]
        dcv = rstd * (dyn - jnp.mean(dyn, axis=-1, keepdims=True) - yn * jnp.mean(dyn * yn, axis=-1, keepdims=True))
        dcv_ref[...] = dcv
        for c in range(NCH):
            dcs[c] = dcv[:, c * LANES:(c + 1) * LANES]

        def fold(p):
            return p.reshape(RS // 8, 8, LANES).sum(axis=0)

        def chunk(c, carry):
            for kk in range(W):
                tot = jnp.zeros((8, LANES), F32)
                for rb in range(tm // RS):
                    tot = tot + fold(dcs[c, pl.ds(rb * RS, RS), :]
                                     * ext[c, pl.ds(rb * RS + CONV_HALO - (W - 1) + kk, RS), :])
                taps_ref[c, pl.ds(8 * kk, 8), :] += tot
            tot = jnp.zeros((8, LANES), F32)
            for rb in range(tm // RS):
                tot = tot + fold(dcs[c, pl.ds(rb * RS, RS), :])
            taps_ref[c, pl.ds(8 * W, 8), :] += tot
            return carry
        lax.fori_loop(0, NCH, chunk, 0)

    tile = pl.BlockSpec((tm, D), lambda i: (i, 0))
    vec = jax.ShapeDtypeStruct((1, D), F32)
    return pl.pallas_call(
        body, name=name, grid=(T // tm,),
        in_specs=[tile, tile, pl.BlockSpec((tm, D2), lambda i: (i, 0)),
                  pl.BlockSpec((CONV_HALO, D2), _prev_blk(tm, CONV_HALO)), _row(D), _row(D)],
        out_specs=[tile, _row(D), _row(D), pl.BlockSpec((NCH, 256, LANES), lambda i: (0, 0, 0))],
        out_shape=[jax.ShapeDtypeStruct((T, D), F32), vec, vec, jax.ShapeDtypeStruct((NCH, 256, LANES), F32)],
        scratch_shapes=[pltpu.VMEM((NCH, tm + CONV_HALO, LANES), F32), pltpu.VMEM((NCH, tm, LANES), F32)],
        compiler_params=_cp("arbitrary"))(dva, cv, uc, uc, lg, lbias)


def conv_bwd_b(dcv, uc, cw, name):
    T, D2 = uc.shape
    D = D2 // 2
    NCH = D // LANES
    W = 31
    tm = _rt(T, 256)
    RS = min(CONV_ROWS, tm)
    NT = T // tm

    def body(dcv_ref, dcn_ref, uc_ref, cw_ref, du_ref, db_ref, ext, outs):
        i = pl.program_id(0)

        @pl.when(i == 0)
        def _():
            db_ref[...] = jnp.zeros_like(db_ref)
        dm = dcv_ref[...]
        dn = jnp.where(i == NT - 1, 0.0, dcn_ref[...])
        for c in range(NCH):
            ext[c, 0:tm, :] = dm[:, c * LANES:(c + 1) * LANES]
            ext[c, tm:, :] = dn[:, c * LANES:(c + 1) * LANES]

        def chunk(c, carry):
            for rb in range(tm // RS):
                acc = jnp.zeros((RS, LANES), F32)
                for kk in range(W):
                    acc = acc + cw_ref[c, pl.ds(kk, 1), :] * ext[c, pl.ds(rb * RS + (W - 1) - kk, RS), :]
                outs[c, pl.ds(rb * RS, RS), :] = acc
            return carry
        lax.fori_loop(0, NCH, chunk, 0)
        dglu = jnp.concatenate([outs[c] for c in range(NCH)], axis=1)
        u = uc_ref[...]
        a = u[:, :D]
        sg = _sig(u[:, D:])
        da = dglu * sg
        dg = dglu * a * (sg * (1.0 - sg))
        du_ref[:, :D] = _bf(da)
        du_ref[:, D:] = _bf(dg)
        db_ref[:, :D] += jnp.sum(da, axis=0, keepdims=True)
        db_ref[:, D:] += jnp.sum(dg, axis=0, keepdims=True)

    tile = pl.BlockSpec((tm, D), lambda i: (i, 0))
    wide = pl.BlockSpec((tm, D2), lambda i: (i, 0))
    return pl.pallas_call(
        body, name=name, grid=(NT,),
        in_specs=[tile, pl.BlockSpec((CONV_HALO, D), _next_blk(tm, CONV_HALO, T)), wide,
                  pl.BlockSpec((NCH, 32, LANES), lambda i: (0, 0, 0))],
        out_specs=[wide, _row(D2)],
        out_shape=[jax.ShapeDtypeStruct((T, D2), BF16), jax.ShapeDtypeStruct((1, D2), F32)],
        scratch_shapes=[pltpu.VMEM((NCH, tm + CONV_HALO, LANES), F32), pltpu.VMEM((NCH, tm, LANES), F32)],
        compiler_params=_cp("arbitrary"))(dcv, dcv, uc, cw)


def _shift_rows(x, halo, k, back):
    rows, n = x.shape
    x3, h3 = x.reshape(rows // 8, 8, n), halo.reshape(1, 8, n)
    sub = lax.broadcasted_iota(jnp.int32, (1, 8, n), 1)
    if back:
        r = pltpu.roll(jnp.concatenate([h3, x3], axis=0), k, 1)
        y = jnp.where(sub < k, r[:-1], r[1:])
    else:
        r = pltpu.roll(jnp.concatenate([x3, h3], axis=0), 8 - k, 1)
        y = jnp.where(sub < 8 - k, r[:-1], r[1:])
    return y.reshape(rows, n)


def _conv3(u, uh, dw_ref, bias_ref):
    return (dw_ref[pl.ds(0, 1), :] * _shift_rows(u, uh, 2, True) + dw_ref[pl.ds(1, 1), :] * _shift_rows(u, uh, 1, True)
            + dw_ref[pl.ds(2, 1), :] * u + bias_ref[...])


def ffn_fwd(h, wup, dww, dwb, wdn, x, gate, pg, nxt, name):
    T, D = h.shape
    fs = wup.shape[2]
    NJ = wup.shape[0] // 2
    tm = _rt(T, 512)
    n_nxt = 0 if nxt is None else 3

    def body(h_ref, hp_ref, wa_ref, wb_ref, dwa_ref, dwb_ref, ba_ref, bb_ref, wd_ref, x_ref, gt_ref, pg_ref, *rest):
        nx, (u_ref, uc_ref, y_ref, xo_ref), more = rest[:n_nxt], rest[n_nxt:n_nxt + 4], rest[n_nxt + 4:]
        acc = more[-1]
        i, j = pl.program_id(0), pl.program_id(1)

        @pl.when(j == 0)
        def _():
            acc[...] = jnp.zeros_like(acc)
        hb = h_ref[...]
        hp = hp_ref[...]

        def half(s, w_ref, dw_ref, b_ref):
            u = _nn(hb, w_ref[...])
            uh = jnp.where(i == 0, 0.0, _nn(hp, w_ref[...]))
            u_ref[s] = u
            uc = _conv3(u, uh, dw_ref, b_ref)
            uc_ref[s] = uc
            return uc
        a = half(0, wa_ref, dwa_ref, ba_ref)
        b = half(1, wb_ref, dwb_ref, bb_ref)
        acc[...] += _nn(_bf(a * _sig(a) * b), wd_ref[...])

        @pl.when(j == NJ - 1)
        def _():
            y = acc[...]
            y_ref[...] = y
            r = lax.rsqrt(jnp.mean(y * y, axis=-1, keepdims=True) + EPS)
            xo = x_ref[...] + gt_ref[...] * ((y * r) * pg_ref[...])
            xo_ref[...] = xo
            if nxt is not None:
                more[0][...] = _next_input(xo, *nx)

    tile = pl.BlockSpec((tm, D), lambda i, j: (i, 0))
    shard = lambda off, r: pl.BlockSpec((None, r, fs), lambda i, j: (j + off, 0, 0))
    ush = pl.BlockSpec((2, None, tm, fs), lambda i, j: (0, j, i, 0))
    vec = pl.BlockSpec((1, D), lambda i, j: (0, 0))
    return pl.pallas_call(
        body, name=name, grid=(T // tm, NJ),
        in_specs=[tile, pl.BlockSpec((FFN_HALO, D), lambda i, j: (jnp.maximum(i * (tm // FFN_HALO) - 1, 0), 0)),
                  shard(0, D), shard(NJ, D), shard(0, 3), shard(NJ, 3), shard(0, 1), shard(NJ, 1),
                  pl.BlockSpec((None, fs, D), lambda i, j: (j, 0, 0)), tile, vec, vec] + [vec] * n_nxt,
        out_specs=[ush, ush, tile, tile] + [tile] * (n_nxt > 0),
        out_shape=[jax.ShapeDtypeStruct((2, NJ, T, fs), F32), jax.ShapeDtypeStruct((2, NJ, T, fs), F32),
                   jax.ShapeDtypeStruct((T, D), F32), jax.ShapeDtypeStruct((T, D), F32)]
        + [jax.ShapeDtypeStruct((T, D), BF16)] * (n_nxt > 0),
        scratch_shapes=[pltpu.VMEM((tm, D), F32)],
        compiler_params=_cp("parallel", "arbitrary"))(h, h, wup, wup, dww, dww, dwb, dwb, wdn, x, gate, pg,
                                                      *(nxt or ()))


def ffn_bwd_a(dy, uc, wdn, name):
    _, NJ, T, fs = uc.shape
    D = dy.shape[1]
    tm = _rt(T, 512)

    NT = T // tm

    def body(dy_ref, uc_ref, wd_ref, duc_ref, dwd_ref, acc):
        i = pl.program_id(1)

        @pl.when(i == 0)
        def _():
            acc[...] = jnp.zeros_like(acc)
        dyb = _bf(dy_ref[...])
        dz = _nt(dyb, wd_ref[...])
        a = uc_ref[0]
        b = uc_ref[1]
        sa = _sig(a)
        asa = a * sa
        acc[...] += _tn(_bf(asa * b), dyb)
        duc_ref[0] = dz * b * (sa + asa * (1.0 - sa))
        duc_ref[1] = dz * asa

        @pl.when(i == NT - 1)
        def _():
            dwd_ref[...] = _bf(acc[...])

    ush = pl.BlockSpec((2, None, tm, fs), lambda j, i: (0, j, i, 0))
    return pl.pallas_call(
        body, name=name, grid=(NJ, NT),
        in_specs=[pl.BlockSpec((tm, D), lambda j, i: (i, 0)), ush, pl.BlockSpec((None, fs, D), lambda j, i: (j, 0, 0))],
        out_specs=[ush, pl.BlockSpec((fs, D), lambda j, i: (j, 0))],
        out_shape=[jax.ShapeDtypeStruct((2, NJ, T, fs), F32), jax.ShapeDtypeStruct((NJ * fs, D), BF16)],
        scratch_shapes=[pltpu.VMEM((fs, D), F32)],
        compiler_params=_cp("parallel", "arbitrary"))(dy, uc, wdn)


def ffn_bwd_b(duc, u, h, dww, wup, name, ex=None):
    NS, T, fs = duc.shape
    D = wup.shape[1]
    tm = _rt(T, 512)
    NT = T // tm

    def body(d_ref, dn_ref, u_ref, h_ref, dw_ref, w_ref, dh_ref, g_ref, dwup_ref, acc, accw, stage):
        i, s = pl.program_id(0), pl.program_id(1)

        @pl.when((i == 0) & (s == 0))
        def _():
            g_ref[...] = jnp.zeros_like(g_ref)

        @pl.when(s == 0)
        def _():
            acc[...] = jnp.zeros_like(acc)
        d = d_ref[...]
        dn = jnp.where(i == NT - 1, 0.0, dn_ref[...])
        d1 = _shift_rows(d, dn, 1, False)
        d2 = _shift_rows(d, dn, 2, False)
        du = _bf(dw_ref[pl.ds(2, 1), :] * d + dw_ref[pl.ds(1, 1), :] * d1 + dw_ref[pl.ds(0, 1), :] * d2)
        acc[...] += _nt(du, w_ref[...])
        dw_part = _tn(h_ref[...], du)

        @pl.when(i == 0)
        def _():
            accw[s] = dw_part

        @pl.when(i > 0)
        def _():
            accw[s] += dw_part

        @pl.when(i == NT - 1)
        def _():
            stage[...] = _bf(accw[s])
            pltpu.sync_copy(stage, dwup_ref.at[s])
        u = u_ref[...]
        g_ref[s, pl.ds(0, 1), :] += jnp.sum(d2 * u, axis=0, keepdims=True)
        g_ref[s, pl.ds(1, 1), :] += jnp.sum(d1 * u, axis=0, keepdims=True)
        g_ref[s, pl.ds(2, 1), :] += jnp.sum(d * u, axis=0, keepdims=True)
        g_ref[s, pl.ds(3, 1), :] += jnp.sum(d, axis=0, keepdims=True)

        @pl.when(s == NS - 1)
        def _():
            dh_ref[...] = acc[...]

    ush = pl.BlockSpec((None, tm, fs), lambda i, s: (s, i, 0))
    unext = pl.BlockSpec((None, FFN_HALO, fs),
                         lambda i, s: (s, jnp.minimum((i + 1) * (tm // FFN_HALO), T // FFN_HALO - 1), 0))
    tile = pl.BlockSpec((tm, D), lambda i, s: (i, 0))
    own, rider = _call(
        body, name, (NT, NS),
        [ush, unext, ush, tile, pl.BlockSpec((None, 3, fs), lambda i, s: (s, 0, 0)),
         pl.BlockSpec((None, D, fs), lambda i, s: (s, 0, 0))],
        [tile, pl.BlockSpec((NS, 8, fs), lambda i, s: (0, 0, 0)), ANY],
        [jax.ShapeDtypeStruct((T, D), F32), jax.ShapeDtypeStruct((NS, 8, fs), F32),
         jax.ShapeDtypeStruct((NS, D, fs), BF16)],
        [pltpu.VMEM((tm, D), F32), pltpu.VMEM((NS, D, fs), F32), pltpu.VMEM((D, fs), BF16)],
        ("arbitrary", "arbitrary"), (duc, duc, u, h, dww, wup), ex, FFN_BWD_VMEM)
    return (*own, rider) if ex is not None else own


def ada_fwd(c_all, w, bias, name):
    L, D, n = w.shape

    def body(c_ref, w_ref, b_ref, o_ref):
        cv = c_ref[...]
        o_ref[...] = _nn(_bf(cv * _sig(cv)), _bf(w_ref[...])) + b_ref[...]

    return pl.pallas_call(
        body, name=name, grid=(L,),
        in_specs=[pl.BlockSpec((N_DEV, D), lambda l: (0, 0)), pl.BlockSpec((None, D, n), lambda l: (l, 0, 0)),
                  pl.BlockSpec((None, 1, n), lambda l: (l, 0, 0))],
        out_specs=pl.BlockSpec((None, N_DEV, n), lambda l: (l, 0, 0)),
        out_shape=jax.ShapeDtypeStruct((L, N_DEV, n), F32), compiler_params=_cp("parallel"))(c_all, w, bias)


def ada_bwd(c_all, dm, name):
    L, _, n = dm.shape
    D = c_all.shape[1]

    def body(c_ref, d_ref, o_ref):
        cv = c_ref[...]
        o_ref[...] = _tn(_bf(cv * _sig(cv)), _bf(d_ref[...]))

    return pl.pallas_call(
        body, name=name, grid=(L,),
        in_specs=[pl.BlockSpec((N_DEV, D), lambda l: (0, 0)), pl.BlockSpec((None, N_DEV, n), lambda l: (l, 0, 0))],
        out_specs=pl.BlockSpec((None, D, n), lambda l: (l, 0, 0)),
        out_shape=jax.ShapeDtypeStruct((L, D, n), F32), compiler_params=_cp("parallel"))(c_all, dm)


def sum_slots(g, name):
    S, R, C = g.shape

    def body(g_ref, o_ref):
        acc = g_ref[0]
        for s in range(1, S):
            acc = acc + g_ref[s]
        o_ref[...] = acc

    return pl.pallas_call(
        body, name=name, grid=(1,), in_specs=[pl.BlockSpec((S, R, C), lambda i: (0, 0, 0))],
        out_specs=pl.BlockSpec((R, C), lambda i: (0, 0)), out_shape=jax.ShapeDtypeStruct((R, C), F32),
        compiler_params=_cp("arbitrary"))(g)


def lb_softmax(logits, name):
    def body(l_ref, s_ref):
        l = l_ref[...]
        e = jnp.exp(l - jnp.max(l, axis=0, keepdims=True))
        s_ref[...] = e / jnp.sum(e, axis=0, keepdims=True)
    return pl.pallas_call(body, name=name, out_shape=jax.ShapeDtypeStruct(logits.shape, F32))(logits)


def adamw(w, gparts, m, v, name):
    P = len(gparts)
    S, Rp, C = gparts[0].shape
    R = P * Rp
    tr = Rp
    budget = (4 * 1024 * 1024) // (4 * (P * S + 7) * C)
    if Rp > budget:
        tr = max(t for t in range(16, budget + 1, 16) if Rp % t == 0)
    per = Rp // tr

    def body(w_ref, *rest):
        g_refs, (m_ref, v_ref, go_ref, d_ref, mo_ref, vo_ref) = rest[:P], rest[P:]
        part = pl.program_id(0) // per
        g = None
        for p, g_ref in enumerate(g_refs):
            gp = g_ref[0].astype(F32)
            for s in range(1, S):
                gp = gp + g_ref[s].astype(F32)
            g = gp if g is None else jnp.where(part == p, gp, g)
        go_ref[...] = g
        mn = ADAM_B1 * m_ref[...] + (1.0 - ADAM_B1) * g
        vn = ADAM_B2 * v_ref[...] + (1.0 - ADAM_B2) * (g * g)
        mo_ref[...] = mn
        vo_ref[...] = vn
        m_hat = mn / (1.0 - ADAM_B1 ** ADAM_STEP)
        v_hat = vn / (1.0 - ADAM_B2 ** ADAM_STEP)
        d_ref[...] = -ADAM_LR * (m_hat / (jnp.sqrt(v_hat) + ADAM_EPS) + ADAM_WD * w_ref[...])

    tile = pl.BlockSpec((tr, C), lambda i: (i, 0))
    slots = [pl.BlockSpec((S, tr, C), lambda i, p=p: (0, jnp.clip(i - p * per, 0, per - 1), 0)) for p in range(P)]
    out = jax.ShapeDtypeStruct((R, C), F32)
    return pl.pallas_call(
        body, name=name, grid=(R // tr,), in_specs=[tile] + slots + [tile, tile],
        out_specs=[tile] * 4, out_shape=[out] * 4, compiler_params=_cp("parallel"))(w, *gparts, m, v)


def _place():
    return lax.axis_index("x"), lax.axis_index("y"), lax.axis_index("c")


def _slot(p):
    return 4 * p[0] + 2 * p[1] + p[2]


class _Gather:
    def __init__(self, xs):
        self.ins = list(xs)
        n = self.n = len(xs)
        self.out_shapes = [jax.ShapeDtypeStruct((N_DEV,) + a.shape, a.dtype) for a in xs]
        self.scratch = [pltpu.SemaphoreType.DMA((n, 7)), pltpu.SemaphoreType.DMA((n, 7)), pltpu.SemaphoreType.DMA((n,))]

    def _parts(self, ins, outs, sems):
        send_sems, recv_sems, local_sems = sems
        x, y, c = _place()
        me, sib = (x, y, c), (x, y, 1 - c)
        chips = [(1 - x, y), (x, 1 - y), (1 - x, 1 - y)]

        def copy(a, k, block, to, src=None):
            dst = outs[a].at[_slot(block)]
            return pltpu.make_async_remote_copy(
                src_ref=dst if src is None else src, dst_ref=dst, send_sem=send_sems.at[a, k],
                recv_sem=recv_sems.at[a, k], device_id=to, device_id_type=MESH)

        mine = [pltpu.make_async_copy(ins[a], outs[a].at[_slot(me)], local_sems.at[a]) for a in range(self.n)]
        first = []
        for a in range(self.n):
            first.append(copy(a, 0, me, sib, src=ins[a]))
            first += [copy(a, 1 + j, me, (*chip, c), src=ins[a]) for j, chip in enumerate(chips)]
        return copy, mine, first, me, sib, chips, c

    def start(self, ins, outs, sems):
        _, mine, first, *_ = self._parts(ins, outs, sems)
        for cp in mine + first:
            cp.start()

    def finish(self, ins, outs, sems):
        copy, mine, first, me, sib, chips, c = self._parts(ins, outs, sems)
        passed = []
        for j, chip in enumerate(chips):
            for a in range(self.n):
                copy(a, 1 + j, (*chip, c), me).wait_recv()
                fwd = copy(a, 4 + j, (*chip, c), sib)
                fwd.start()
                passed.append(fwd)
        for a in range(self.n):
            copy(a, 0, sib, me).wait_recv()
            for j, chip in enumerate(chips):
                copy(a, 4 + j, (*chip, 1 - c), me).wait_recv()
        for cp in first + passed:
            cp.wait_send()
        for cp in mine:
            cp.wait()


def _run_alone(ex, name):
    def body(*refs):
        ni, no = len(ex.ins), len(ex.out_shapes)
        parts = refs[:ni], refs[ni:ni + no], refs[ni + no:]
        ex.start(*parts)
        ex.finish(*parts)
    return pl.pallas_call(body, name=name, in_specs=[ANY] * len(ex.ins), out_specs=[ANY] * len(ex.out_shapes),
                          out_shape=ex.out_shapes, scratch_shapes=ex.scratch)(*ex.ins)


def _ride(body, n_in, n_out, ex, first, last):
    ni, no = len(ex.ins), len(ex.out_shapes)

    def wrapped(*refs):
        a, r_in = refs[:n_in], refs[n_in:n_in + ni]
        b, r_out = refs[n_in + ni:n_in + ni + n_out], refs[n_in + ni + n_out:n_in + ni + n_out + no]
        rest = refs[n_in + ni + n_out + no:]
        s, r_s = rest[:len(rest) - len(ex.scratch)], rest[len(rest) - len(ex.scratch):]

        @pl.when(first())
        def _():
            ex.start(r_in, r_out, r_s)
        body(*a, *b, *s)

        @pl.when(last())
        def _():
            ex.finish(r_in, r_out, r_s)
    return wrapped


def all_gather(xs, name):
    return _run_alone(_Gather(xs), name)


class _Exchange:
    def __init__(self, groups):
        self.ins = [a for grp in groups for a in grp]
        self.where = [(g, i) for g, grp in enumerate(groups) for i in range(len(grp))]
        n = self.n = len(self.ins)
        self.out_shapes = [jax.ShapeDtypeStruct((N_DEV, len(grp)) + grp[0].shape[1:], grp[0].dtype) for grp in groups]
        self.scratch = [pltpu.SemaphoreType.DMA((n, 7)), pltpu.SemaphoreType.DMA((n, 7)), pltpu.SemaphoreType.DMA((n,))]

    def _parts(self, ins, outs, sems):
        send_sems, recv_sems, local_sems = sems
        x, y, c = _place()
        me = (x, y, c)

        def peer(r):
            return (1 - x if r & 4 else x, 1 - y if r & 2 else y, 1 - c if r & 1 else c)

        def land(a, src):
            g, i = self.where[a]
            return outs[g].at[_slot(src), i]

        def copy(a, r, src_dev):
            p = peer(r)
            return pltpu.make_async_remote_copy(
                src_ref=ins[a].at[_slot(p)], dst_ref=land(a, src_dev), send_sem=send_sems.at[a, r - 1],
                recv_sem=recv_sems.at[a, r - 1], device_id=p, device_id_type=MESH)

        mine = [pltpu.make_async_copy(ins[a].at[_slot(me)], land(a, me), local_sems.at[a]) for a in range(self.n)]
        sends = [copy(a, r, me) for r in range(1, N_DEV) for a in range(self.n)]
        arrivals = [copy(a, r, peer(r)) for r in range(1, N_DEV) for a in range(self.n)]
        return mine, sends, arrivals

    def start(self, ins, outs, sems):
        mine, sends, _ = self._parts(ins, outs, sems)
        for cp in mine + sends:
            cp.start()

    def finish(self, ins, outs, sems):
        mine, sends, arrivals = self._parts(ins, outs, sems)
        for cp in arrivals:
            cp.wait_recv()
        for cp in sends:
            cp.wait_send()
        for cp in mine:
            cp.wait()


def all_to_all(groups, name):
    return _run_alone(_Exchange(groups), name)


def _mods(mod, l, D):
    return [mod[l:l + 1, k * D:(k + 1) * D] for k in range(6)]


def _ffn_backward(l, dxo, dy, hb, u, uc, xin, sc2, gains, W, post, ex=None):
    _, NJ, T, fs = u.shape
    duc, d_wdn = ffn_bwd_a(dy, uc, W["wdn"][l], f"ffn{l}_bwd_a")
    dh, taps, d_wup, *rider = ffn_bwd_b(duc.reshape(2 * NJ, T, fs), u.reshape(2 * NJ, T, fs), hb, W["fdw_w"][l],
                                        W["wup"][l], f"ffn{l}_bwd_b", ex)
    dx, da, dsh, *below = resid_bwd(f"ffn{l}_resid_bwd", pre=(dh, xin, dxo, gains["pre_ffn_g"][l:l + 1], sc2), post=post)
    small = dict(da=da, dsh=dsh, dwb=taps[:, 3], dww=taps[:, 0:3])
    return dx, d_wup, d_wdn, small, below, (rider[0] if rider else None)


def _local_step(xt, tgt, mod, gains, W, mine):
    T, D = xt.shape
    zero_d = jnp.zeros((1, D), F32)
    half = lambda g: g.reshape(N_DEV // 2, 2 * g.shape[1], D)
    sh1, sc1, g1, sh2, sc2, g2 = m0 = _mods(mod, 0, D)
    h0, (whin,) = prenorm_mod(xt, gains["pre_mix_g"][0:1], sc1, sh1, "l0_prenorm", _Gather([mine["whin"]]))
    W = dict(W, whin=whin)
    proj, (whout, wup0, wdn0) = mm_nn(h0, W["whin"], jnp.zeros((1, 4 * D), F32), "hgrn_proj",
                                      _Gather([mine["whout"], mine["wup"][0], mine["wdn"][0]]))
    o, og, st, (wcin, wcout, wup1, wdn1) = hgrn_fwd(
        proj, W["lb"], W["gg"], "hgrn_fwd", _Gather([mine["wcin"], mine["wcout"], mine["wup"][1], mine["wdn"][1]]))
    W = dict(W, whout=whout.reshape(D, D), wcin=wcin, wcout=wcout.reshape(D, D), wup=[wup0, wup1],
             wdn=[half(wdn0), half(wdn1)])
    t1, c1, e1, t2, c2, e2 = m1 = _mods(mod, 1, D)
    y0, x1, h0f = mm_post(og, W["whout"], zero_d, xt, g1, gains["post_mix_g"][0:1],
                          (gains["pre_ffn_g"][0:1], sc2, sh2), "hgrn_out")
    u0, uc0, yf0, x2, h1 = ffn_fwd(h0f, W["wup"][0], W["fdw_w"][0], W["fdw_b"][0], W["wdn"][0], x1, g2,
                                   gains["post_ffn_g"][0:1], (gains["pre_mix_g"][1:2], c1, t1), "ffn0_fwd")
    uc = mm_nn(h1, W["wcin"], W["cb_in"], "conv_in")
    cv, va = conv_mid_fwd(uc, W["cw"], W["cdw_b"], W["ln_g"], W["ln_b"], "conv_mid_fwd")
    y1, x3, h1f = mm_post(va, W["wcout"], W["cb_out"], x2, e1, gains["post_mix_g"][1:2],
                          (gains["pre_ffn_g"][1:2], c2, t2), "conv_out")
    u1, uc1, yf1, x4 = ffn_fwd(h1f, W["wup"][1], W["fdw_w"][1], W["fdw_b"][1], W["wdn"][1], x3, e2,
                               gains["post_ffn_g"][1:2], None, "ffn1_fwd")
    dx4, loss_part, dyf1, dgate_f1, dpost_f1, _ = resid_bwd(
        "loss_resid_bwd", loss=(x4, tgt), post=(yf1, e2, gains["post_ffn_g"][1:2]))
    dx3, d_wup1, d_wdn1, sf1, (dy1, dgate_c, dpost_c, dys_c), _ = _ffn_backward(
        1, dx4, dyf1, h1f, u1, uc1, x3, c2, gains, W, (y1, e1, gains["post_mix_g"][1:2]))
    dva = mm_nt(dy1, W["wcout"][None], "conv_dva", True)
    d_wcout = mm_tn(va, dy1, "conv_dwout", True, True, D, D)
    dcv, dlg, dlbias, taps = conv_bwd_a(dva, cv, uc, W["ln_g"], W["ln_b"], "conv_bwd_a")
    duc, dbin = conv_bwd_b(dcv, uc, W["cw"], "conv_bwd_b")
    dh1 = mm_nt(duc, W["wcin"], "conv_dh", True)
    d_wcin = mm_tn(h1, duc, "conv_dwin", True, True, D, W["wcin"].shape[2])
    dx2, da_c, dsh_c, dyf0, dgate_f0, dpost_f0, _ = resid_bwd(
        "l1_resid_bwd", pre=(dh1, x2, dx3, gains["pre_mix_g"][1:2], c1), post=(yf0, g2, gains["post_ffn_g"][0:1]))
    rows = lambda g: g.reshape(N_DEV, -1, D)
    dx1, d_wup0, d_wdn0, sf0, (dy0, dgate_h, dpost_h, _), (r_wcin, r_wcout, r_wup1, r_wdn1) = _ffn_backward(
        0, dx2, dyf0, h0f, u0, uc0, x1, sc2, gains, W, (y0, g1, gains["post_mix_g"][0:1]),
        _Exchange([[d_wcin], [rows(d_wcout)], [d_wup1], [rows(d_wdn1)]]))
    dog = mm_nt(dy0, W["whout"][None], "hgrn_dog", True)
    d_whout = mm_tn(og, dy0, "hgrn_dwout", True, True, D, D)
    dproj, dlb, dgg, (r_wup0, r_wdn0, r_whout) = hgrn_bwd(
        proj, o, dog, st, W["lb"], W["gg"], "hgrn_bwd", _Exchange([[d_wup0], [rows(d_wdn0)], [rows(d_whout)]]))
    d_whin = mm_tn(h0, dproj, "hgrn_dwin", True, True, D, W["whin"].shape[2])
    dh0, (r_whin,) = mm_nt(dproj, W["whin"], "hgrn_dh", True, _Exchange([[d_whin]]))
    dx0, da_h, dsh_h = resid_bwd("l0_resid_bwd", pre=(dh0, xt, dx1, gains["pre_mix_g"][0:1], sc1))
    sf0.update(dgate=dgate_f0, dpost=dpost_f0)
    sf1.update(dgate=dgate_f1, dpost=dpost_f1)

    big = dict(hgrn_w_in=r_whin, hgrn_w_out=r_whout, conv_w_in=r_wcin, conv_w_out=r_wcout,
               ffn_w_up=[r_wup0, r_wup1], ffn_w_down=[r_wdn0, r_wdn1])
    pm, pf = gains["pre_mix_g"], gains["pre_ffn_g"]
    dmod = jnp.concatenate([
        dsh_h, da_h * pm[0:1], dgate_h, sf0["dsh"], sf0["da"] * pf[0:1], sf0["dgate"],
        dsh_c, da_c * pm[1:2], dgate_c, sf1["dsh"], sf1["da"] * pf[1:2], sf1["dgate"]], axis=1)
    NCH = D // LANES
    tap = taps.reshape(NCH, 32, 8, LANES).sum(axis=2)
    small = dict(
        dmod=dmod,
        pre_mix_g=jnp.concatenate([da_h * (1.0 + sc1), da_c * (1.0 + c1)], axis=0),
        post_mix_g=jnp.concatenate([dpost_h, dpost_c], axis=0),
        pre_ffn_g=jnp.concatenate([sf0["da"] * (1.0 + sc2), sf1["da"] * (1.0 + c2)], axis=0),
        post_ffn_g=jnp.concatenate([sf0["dpost"], sf1["dpost"]], axis=0),
        lb=dlb, gg=dgg,
        ffn_dw_b=jnp.stack([sf0["dwb"], sf1["dwb"]]),
        conv_b_in=dbin,
        conv_dw_w=jnp.transpose(tap[:, :31], (1, 0, 2)).reshape(31, D),
        conv_dw_b=tap[:, 31].reshape(1, D),
        conv_ln_g=dlg, conv_ln_b=dlbias, conv_b_out=dys_c,
        ffn_dw_w=jnp.stack([sf0["dww"], sf1["dww"]]))
    return loss_part, dx0, big, small


SMALL_ORDER = ["dmod", "pre_mix_g", "post_mix_g", "pre_ffn_g", "post_ffn_g", "lb", "gg", "ffn_dw_b", "conv_b_in",
               "conv_dw_w", "conv_dw_b", "conv_ln_g", "conv_ln_b", "conv_b_out", "ffn_dw_w"]


def _pack(parts):
    flat = jnp.concatenate([p.reshape(-1) for p in parts])
    pad = (-flat.shape[0]) % LANES
    if pad:
        flat = jnp.concatenate([flat, jnp.zeros((pad,), F32)])
    return flat.reshape(-1, LANES)


def _unpack(flat, shapes):
    out, off = [], 0
    for s in shapes:
        n = 1
        for d in s:
            n *= d
        out.append(flat[off:off + n].reshape(s))
        off += n
    return out


def _apply_adamw(name, w, g_slots, m, v):
    shp = w.shape
    C = shp[-1]
    R = w.size // C
    parts = g_slots if isinstance(g_slots, list) else [g_slots]
    parts = [p.reshape(p.shape[0], R // len(parts), C) for p in parts]
    g, d, mn, vn = adamw(w.reshape(R, C), parts, m.reshape(R, C), v.reshape(R, C), "adamw_" + name)
    return g.reshape(shp), d.reshape(shp), mn.reshape(shp), vn.reshape(shp)


WEIGHTS = ['ada_w', 'ada_b', 'pre_mix_g', 'post_mix_g', 'pre_ffn_g', 'post_ffn_g', 'hgrn_w_in', 'hgrn_lb_logits',
           'hgrn_gnorm_g', 'hgrn_w_out', 'conv_w_in', 'conv_b_in', 'conv_dw_w', 'conv_dw_b', 'conv_ln_g', 'conv_ln_b',
           'conv_w_out', 'conv_b_out', 'ffn_w_up', 'ffn_dw_w', 'ffn_dw_b', 'ffn_w_down']


def kernel(x, c, ada_w, ada_b, pre_mix_g, post_mix_g, pre_ffn_g, post_ffn_g, hgrn_w_in, hgrn_lb_logits, hgrn_gnorm_g, hgrn_w_out, conv_w_in, conv_b_in, conv_dw_w, conv_dw_b, conv_ln_g, conv_ln_b, conv_w_out, conv_b_out, ffn_w_up, ffn_dw_w, ffn_dw_b, ffn_w_down, loss_target, m_ada_w, m_ada_b, m_pre_mix_g, m_post_mix_g, m_pre_ffn_g, m_post_ffn_g, m_hgrn_w_in, m_hgrn_lb_logits, m_hgrn_gnorm_g, m_hgrn_w_out, m_conv_w_in, m_conv_b_in, m_conv_dw_w, m_conv_dw_b, m_conv_ln_g, m_conv_ln_b, m_conv_w_out, m_conv_b_out, m_ffn_w_up, m_ffn_dw_w, m_ffn_dw_b, m_ffn_w_down, v_ada_w, v_ada_b, v_pre_mix_g, v_post_mix_g, v_pre_ffn_g, v_post_ffn_g, v_hgrn_w_in, v_hgrn_lb_logits, v_hgrn_gnorm_g, v_hgrn_w_out, v_conv_w_in, v_conv_b_in, v_conv_dw_w, v_conv_dw_b, v_conv_ln_g, v_conv_ln_b, v_conv_w_out, v_conv_b_out, v_ffn_w_up, v_ffn_dw_w, v_ffn_dw_b, v_ffn_w_down):
    P = dict(locals())
    _, T, D = x.shape
    me = _slot(_place())
    L = ada_w.shape[0]
    na = ada_w.shape[2]
    fs = ffn_w_up.shape[2]
    nci = conv_w_in.shape[2]
    nd = conv_dw_b.shape[1]
    NCH = D // LANES

    small_in = [c, conv_b_in, conv_dw_w, conv_dw_b, conv_ln_g, conv_ln_b, conv_b_out, ffn_dw_w]
    (packed,) = all_gather([_pack(small_in)], "gather_first")
    mine = dict(whin=_bf(hgrn_w_in[0]), whout=_bf(hgrn_w_out[0]), wcin=_bf(conv_w_in[0]), wcout=_bf(conv_w_out[0]),
                wup=[_bf(ffn_w_up[l]) for l in range(L)], wdn=[_bf(ffn_w_down[l]) for l in range(L)])
    sm = packed.reshape(N_DEV, -1)
    offs = [0]
    for a in small_in:
        offs.append(offs[-1] + a.size)
    piece = lambda k, shp: sm[:, offs[k]:offs[k + 1]].reshape((N_DEV,) + shp)
    c_all = piece(0, (D,))
    dw_nat = jnp.transpose(piece(2, (31, nd)), (1, 0, 2)).reshape(31, D)
    cw = jnp.pad(jnp.transpose(dw_nat.reshape(31, NCH, LANES), (1, 0, 2)), ((0, 0), (0, 1), (0, 0)))
    fdw = piece(7, (L, 3, fs))
    lb_soft = lb_softmax(hgrn_lb_logits, "lb_softmax")
    W = dict(
        lb=lb_soft[0:1], gg=hgrn_gnorm_g, cb_in=piece(1, (nci,)).reshape(1, N_DEV * nci), cw=cw,
        cdw_b=piece(3, (nd,)).reshape(1, D), ln_g=piece(4, (nd,)).reshape(1, D), ln_b=piece(5, (nd,)).reshape(1, D),
        cb_out=piece(6, (nd,)).reshape(1, D), fdw_w=[fdw[:, l] for l in range(L)],
        fdw_b=[ffn_dw_b[l].reshape(N_DEV, 1, fs) for l in range(L)])

    bias_mine = lax.dynamic_slice(ada_b, (0, me * na), (L, na)).reshape(L, 1, na)
    mod_cols = ada_fwd(c_all, ada_w, bias_mine, "ada_fwd")
    (mod_all,) = all_gather([mod_cols], "gather_mod")
    mod = jnp.transpose(lax.dynamic_index_in_dim(mod_all, me, axis=2, keepdims=False), (1, 0, 2)).reshape(L, N_DEV * na)

    gains = dict(pre_mix_g=pre_mix_g, post_mix_g=post_mix_g, pre_ffn_g=pre_ffn_g, post_ffn_g=post_ffn_g)
    loss_part, dx0, big, small = _local_step(x[0], loss_target[0], mod, gains, W, mine)
    loss = lax.psum((0.5 / D) * jnp.sum(loss_part), ("x", "y", "c"))

    parts = [small[k] for k in SMALL_ORDER]
    (sm_all,) = all_gather([_pack(parts)], "gather_small_grads")
    tot = _unpack(sum_slots(sm_all, "sum_small_grads").reshape(-1), [p.shape for p in parts])
    tot = dict(zip(SMALL_ORDER, tot))
    dmod_all = sm_all.reshape(N_DEV, -1)[:, :L * 6 * D].reshape(N_DEV, L, 6 * D)
    dm_mine = jnp.transpose(lax.dynamic_slice(dmod_all, (0, 0, me * na), (N_DEV, L, na)), (1, 0, 2))
    s0 = lb_soft[0:1]
    onehot = (lax.broadcasted_iota(jnp.int32, lb_soft.shape, 0) == 0).astype(F32)
    col = lambda a, n: lax.dynamic_slice_in_dim(a, me * n, n, axis=a.ndim - 1)
    G = {
        'ada_w': ada_bwd(c_all, dm_mine, "ada_bwd")[None],
        'ada_b': tot["dmod"].reshape(1, L, 6 * D),
        'pre_mix_g': tot["pre_mix_g"][None], 'post_mix_g': tot["post_mix_g"][None],
        'pre_ffn_g': tot["pre_ffn_g"][None], 'post_ffn_g': tot["post_ffn_g"][None],
        'hgrn_lb_logits': (tot["lb"] * s0 * (onehot - lb_soft))[None],
        'hgrn_gnorm_g': tot["gg"][None],
        'ffn_dw_b': tot["ffn_dw_b"].reshape(1, L, N_DEV * fs),
        'conv_b_in': col(tot["conv_b_in"], nci)[None],
        'conv_dw_w': col(tot["conv_dw_w"], nd)[None, None],
        'conv_dw_b': col(tot["conv_dw_b"], nd)[None], 'conv_ln_g': col(tot["conv_ln_g"], nd)[None],
        'conv_ln_b': col(tot["conv_ln_b"], nd)[None], 'conv_b_out': col(tot["conv_b_out"], nd)[None],
        'ffn_dw_w': lax.dynamic_index_in_dim(tot["ffn_dw_w"], me, axis=1, keepdims=False)[None],
    }

    G.update(big)
    res = {n: _apply_adamw(n, P[n], G[n], P["m_" + n], P["v_" + n]) for n in WEIGHTS}
    return (loss, dx0[None], *[res[n][0] for n in WEIGHTS], *[res[n][1] for n in WEIGHTS],
            *[res[n][2] for n in WEIGHTS], *[res[n][3] for n in WEIGHTS])
```

```python
import functools

import jax
import jax.numpy as jnp
from jax import lax
from jax.experimental import pallas as pl
from jax.experimental.pallas import tpu as pltpu

F32 = jnp.float32
BF16 = jnp.bfloat16
EPS = 1e-6
LANES = 128
N_DEV = 8
VMEM_LIMIT = 48 * 1024 * 1024
FFN_BWD_VMEM = 58 * 1024 * 1024
MM_ROWS = 1024
HG_CHUNK = 128
HG_SUB = 32
EXP_CLAMP = 80.0
CONV_HALO = 32
FFN_HALO = 8
CONV_ROWS = 32
ADAM_LR, ADAM_B1, ADAM_B2, ADAM_EPS, ADAM_WD, ADAM_STEP = 0.001, 0.9, 0.999, 1e-08, 0.01, 10
MESH = pl.DeviceIdType.MESH
ANY = pl.BlockSpec(memory_space=pl.ANY)


def _cp(*sem):
    return pltpu.CompilerParams(dimension_semantics=sem, vmem_limit_bytes=VMEM_LIMIT)


def _rt(n, t):
    t = min(n, t)
    assert n % t == 0, (n, t)
    return t


def _sig(x):
    return jax.nn.sigmoid(x)


def _nt(a, b):
    return lax.dot_general(a, b, (((1,), (1,)), ((), ())), preferred_element_type=F32)


def _tn(a, b):
    return lax.dot_general(a, b, (((0,), (0,)), ((), ())), preferred_element_type=F32)


def _nn(a, b):
    return jnp.dot(a, b, preferred_element_type=F32)


def _bf(x):
    return x.astype(BF16)


def _nn_st(a, b):
    return _nn(_bf(a), _bf(b))


def _tn_st(a, b):
    return _tn(_bf(a), _bf(b))


def _row(d):
    return pl.BlockSpec((1, d), lambda *_: (0, 0))


def prenorm_mod(x, g, sc, sh, name, ex=None):
    T, D = x.shape
    tm = _rt(T, 512)

    def body(x_ref, g_ref, sc_ref, sh_ref, h_ref):
        h_ref[...] = _next_input(x_ref[...], g_ref, sc_ref, sh_ref)

    tile = pl.BlockSpec((tm, D), lambda i: (i, 0))
    (h,), rider = _call(body, name, (T // tm,), [tile, _row(D), _row(D), _row(D)], [tile],
                        [jax.ShapeDtypeStruct((T, D), BF16)], [], ("parallel",), (x, g, sc, sh), ex)
    return (h, rider) if ex is not None else h


def resid_bwd(name, pre=None, loss=None, post=None):
    T, D = (pre[1] if pre is not None else loss[0]).shape
    tm = _rt(T, 512)
    n_src = 5 if pre is not None else 2
    n_in = n_src + (3 if post is not None else 0)

    def body(*refs):
        ins, outs = refs[:n_in], refs[n_in:]
        first = pl.program_id(0) == 0
        if pre is not None:
            dh_ref, x_ref, dr_ref, g_ref, sc_ref = ins[:5]
            dx_ref, da_ref, dsh_ref = outs[:3]

            @pl.when(first)
            def _():
                da_ref[...] = jnp.zeros_like(da_ref)
                dsh_ref[...] = jnp.zeros_like(dsh_ref)
            xv = x_ref[...]
            dh = dh_ref[...]
            r = lax.rsqrt(jnp.mean(xv * xv, axis=-1, keepdims=True) + EPS)
            n = xv * r
            da_ref[...] += jnp.sum(dh * n, axis=0, keepdims=True)
            dsh_ref[...] += jnp.sum(dh, axis=0, keepdims=True)
            dn = dh * (g_ref[...] * (1.0 + sc_ref[...]))
            dx = dr_ref[...] + r * (dn - n * jnp.mean(dn * n, axis=-1, keepdims=True))
            dx_ref[...] = dx
            rest = outs[3:]
        else:
            x_ref, t_ref = ins[:2]
            dx_ref, part_ref = outs[:2]

            @pl.when(first)
            def _():
                part_ref[...] = jnp.zeros_like(part_ref)
            e = x_ref[...] - t_ref[...]
            dx = e * (1.0 / D)
            dx_ref[...] = dx
            e2 = (e * e).reshape(tm // 8, 8, D).sum(axis=0)
            acc = e2[:, 0:LANES]
            for j in range(1, D // LANES):
                acc = acc + e2[:, j * LANES:(j + 1) * LANES]
            part_ref[...] += acc
            rest = outs[2:]
        if post is not None:
            y_ref, gt_ref, pg_ref = ins[n_src:]
            dy_ref, dgate_ref, dpg_ref, dys_ref = rest

            @pl.when(first)
            def _():
                dgate_ref[...] = jnp.zeros_like(dgate_ref)
                dpg_ref[...] = jnp.zeros_like(dpg_ref)
                dys_ref[...] = jnp.zeros_like(dys_ref)
            yv = y_ref[...]
            ry = lax.rsqrt(jnp.mean(yv * yv, axis=-1, keepdims=True) + EPS)
            ny = yv * ry
            s = jnp.sum(dx * ny, axis=0, keepdims=True)
            dgate_ref[...] += s * pg_ref[...]
            dpg_ref[...] += s * gt_ref[...]
            dny = dx * (gt_ref[...] * pg_ref[...])
            dy = ry * (dny - ny * jnp.mean(dny * ny, axis=-1, keepdims=True))
            dy_ref[...] = _bf(dy)
            dys_ref[...] += jnp.sum(dy, axis=0, keepdims=True)

    tile = pl.BlockSpec((tm, D), lambda i: (i, 0))
    vec = jax.ShapeDtypeStruct((1, D), F32)
    big = jax.ShapeDtypeStruct((T, D), F32)
    if pre is not None:
        operands, in_specs = list(pre), [tile, tile, tile, _row(D), _row(D)]
        out_specs, out_shape = [tile, _row(D), _row(D)], [big, vec, vec]
    else:
        operands, in_specs = list(loss), [tile, tile]
        out_specs, out_shape = [tile, pl.BlockSpec((8, LANES), lambda i: (0, 0))], [big, jax.ShapeDtypeStruct((8, LANES), F32)]
    if post is not None:
        operands, in_specs = operands + list(post), in_specs + [tile, _row(D), _row(D)]
        out_specs = out_specs + [tile, _row(D), _row(D), _row(D)]
        out_shape = out_shape + [jax.ShapeDtypeStruct((T, D), BF16), vec, vec, vec]
    return pl.pallas_call(body, name=name, grid=(T // tm,), in_specs=in_specs, out_specs=out_specs, out_shape=out_shape,
                          compiler_params=_cp("arbitrary"))(*operands)


def _call(body, name, grid, in_specs, out_specs, out_shape, scratch, sems, operands, ex=None, vmem=VMEM_LIMIT):
    if ex is not None:
        def first():
            return functools.reduce(lambda p, q: p & q, [pl.program_id(k) == 0 for k in range(len(grid))])

        def last():
            return functools.reduce(lambda p, q: p & q, [pl.program_id(k) == grid[k] - 1 for k in range(len(grid))])
        body = _ride(body, len(in_specs), len(out_specs), ex, first, last)
        in_specs = in_specs + [ANY] * len(ex.ins)
        out_specs = out_specs + [ANY] * len(ex.out_shapes)
        out_shape = out_shape + ex.out_shapes
        scratch = scratch + ex.scratch
        operands = list(operands) + ex.ins
        sems = ("arbitrary",) * len(grid)
    res = pl.pallas_call(body, name=name, grid=grid, in_specs=in_specs, out_specs=out_specs, out_shape=out_shape,
                         scratch_shapes=scratch,
                         compiler_params=pltpu.CompilerParams(dimension_semantics=sems, vmem_limit_bytes=vmem))(*operands)
    n_own = len(res) - (len(ex.out_shapes) if ex is not None else 0)
    return res[:n_own], res[n_own:]


def mm_nn(a, bg, bias, name, ex=None):
    M, K = a.shape
    G, _, n = bg.shape
    tm = _rt(M, MM_ROWS)

    def body(a_ref, b_ref, bias_ref, o_ref):
        o_ref[...] = _nn(a_ref[...], b_ref[...]) + bias_ref[...]

    (out,), rider = _call(
        body, name, (M // tm, G),
        [pl.BlockSpec((tm, K), lambda i, j: (i, 0)), pl.BlockSpec((None, K, n), lambda i, j: (j, 0, 0)),
         pl.BlockSpec((1, n), lambda i, j: (0, j))],
        [pl.BlockSpec((tm, n), lambda i, j: (i, j))], [jax.ShapeDtypeStruct((M, G * n), F32)], [],
        ("parallel", "arbitrary"), (a, bg, bias), ex)
    return (out, rider) if ex is not None else out


def _next_input(xo, g_ref, sc_ref, sh_ref):
    r = lax.rsqrt(jnp.mean(xo * xo, axis=-1, keepdims=True) + EPS)
    return _bf((xo * r) * g_ref[...] * (1.0 + sc_ref[...]) + sh_ref[...])


def mm_post(a, b, bias, x, gate, pg, nxt, name):
    T, K = a.shape
    D = b.shape[1]
    tm = _rt(T, 512)

    def body(a_ref, b_ref, bias_ref, x_ref, gt_ref, pg_ref, g_ref, sc_ref, sh_ref, y_ref, xo_ref, h_ref):
        y = _nn(a_ref[...], b_ref[...]) + bias_ref[...]
        y_ref[...] = y
        r = lax.rsqrt(jnp.mean(y * y, axis=-1, keepdims=True) + EPS)
        xo = x_ref[...] + gt_ref[...] * ((y * r) * pg_ref[...])
        xo_ref[...] = xo
        h_ref[...] = _next_input(xo, g_ref, sc_ref, sh_ref)

    tile = pl.BlockSpec((tm, D), lambda i: (i, 0))
    out = jax.ShapeDtypeStruct((T, D), F32)
    return pl.pallas_call(
        body, name=name, grid=(T // tm,),
        in_specs=[pl.BlockSpec((tm, K), lambda i: (i, 0)), pl.BlockSpec((K, D), lambda i: (0, 0)), _row(D), tile]
        + [_row(D)] * 5,
        out_specs=[tile, tile, tile], out_shape=[out, out, jax.ShapeDtypeStruct((T, D), BF16)],
        compiler_params=_cp("parallel"))(a, b, bias, x, gate, pg, *nxt)


def mm_nt(dy, wg, name, natural, ex=None):
    G, K, n = wg.shape
    T = dy.shape[0] if natural else dy.shape[1]
    tm = _rt(T, MM_ROWS)

    def body(dy_ref, w_ref, o_ref, acc):
        j = pl.program_id(1)

        @pl.when(j == 0)
        def _():
            acc[...] = jnp.zeros_like(acc)
        acc[...] += _nt(_bf(dy_ref[...]), w_ref[...])

        @pl.when(j == G - 1)
        def _():
            o_ref[...] = acc[...]

    dspec = (pl.BlockSpec((tm, n), lambda i, j: (i, j)) if natural
             else pl.BlockSpec((None, tm, n), lambda i, j: (j, i, 0)))
    (out,), rider = _call(
        body, name, (T // tm, G), [dspec, pl.BlockSpec((None, K, n), lambda i, j: (j, 0, 0))],
        [pl.BlockSpec((tm, K), lambda i, j: (i, 0))], [jax.ShapeDtypeStruct((T, K), F32)],
        [pltpu.VMEM((tm, K), F32)], ("parallel", "arbitrary"), (dy, wg), ex)
    return (out, rider) if ex is not None else out


def mm_tn(xm, gm, name, x_natural, g_natural, kx, n):
    if x_natural:
        T, Gx = xm.shape[0], xm.shape[1] // kx
    else:
        Gx, T = xm.shape[0], xm.shape[1]
    tt = _rt(T, 2 * MM_ROWS)
    if x_natural:
        xspec = pl.BlockSpec((tt, kx), lambda g, k, t: (t, k))
    else:
        xspec = pl.BlockSpec((None, tt, kx), lambda g, k, t: (k, t, 0))
    if g_natural:
        Gg = gm.shape[1] // n
        gspec = pl.BlockSpec((tt, n), lambda g, k, t: (t, g))
    else:
        Gg = gm.shape[0]
        gspec = pl.BlockSpec((None, tt, n), lambda g, k, t: (g, t, 0))

    NT = T // tt

    def body(x_ref, g_ref, o_ref, acc):
        t = pl.program_id(2)

        @pl.when(t == 0)
        def _():
            acc[...] = jnp.zeros_like(acc)
        acc[...] += _tn(_bf(x_ref[...]), _bf(g_ref[...]))

        @pl.when(t == NT - 1)
        def _():
            o_ref[...] = acc[...].astype(o_ref.dtype)

    return pl.pallas_call(
        body, name=name, grid=(Gg, Gx, NT), in_specs=[xspec, gspec],
        out_specs=pl.BlockSpec((None, kx, n), lambda g, k, t: (g, k, 0)),
        out_shape=jax.ShapeDtypeStruct((Gg, Gx * kx, n), BF16), scratch_shapes=[pltpu.VMEM((kx, n), F32)],
        compiler_params=_cp("parallel", "parallel", "arbitrary"))(xm, gm)


def _split3(x):
    h = _bf(x)
    r = x - h.astype(F32)
    m = _bf(r)
    lo = _bf(r - m.astype(F32))
    return h, m, lo


def _tri_mm(tri, x):
    h, m, lo = _split3(x)
    return _nn(tri, lo) + _nn(tri, m) + _nn(tri, h)


def _hgrn_gates(qr, fz, lb):
    sq = _sig(qr)
    q = qr * sq
    sig = _sig(fz)
    f = lb + (1.0 - lb) * sig
    k = 1.0 - f
    return sq, q, sig, f, k, jnp.log(f)


def hgrn_fwd(proj, lb, gg, name, ex=None):
    T, D4 = proj.shape
    D = D4 // 4
    H = D // LANES
    C = _rt(T, HG_CHUNK)
    SB = min(HG_SUB, C)
    NC = T // C

    def body(q_ref, f_ref, v_ref, g_ref, lb_ref, gg_ref, o_ref, og_ref, st_ref, ST, bex):
        @pl.when(pl.program_id(0) == 0)
        def _():
            ST[...] = jnp.zeros_like(ST)
        r_i = lax.broadcasted_iota(jnp.int32, (C, C), 0)
        c_i = lax.broadcasted_iota(jnp.int32, (C, C), 1)
        low = _bf((r_i >= c_i).astype(F32))
        rs = lax.broadcasted_iota(jnp.int32, (SB, C), 0)
        cs = lax.broadcasted_iota(jnp.int32, (SB, C), 1)
        for hd in range(H):
            sl = slice(hd * LANES, (hd + 1) * LANES)
            _, q, _, _, k, lf = _hgrn_gates(q_ref[:, sl], f_ref[:, sl], lb_ref[:, sl])
            v = v_ref[:, sl]
            b = _tri_mm(low, lf)
            bex[hd] = b - lf
            bend = jnp.sum(lf, axis=0, keepdims=True)
            blocks = []
            for I in range(C // SB):
                p = bex[hd, pl.ds(SB * I + SB // 2, 1), :]
                rows = slice(SB * I, SB * (I + 1))
                qI = _bf(q[rows] * jnp.exp(jnp.minimum(b[rows] - p, EXP_CLAMP)))
                kI = _bf(k * jnp.exp(jnp.minimum(p - b, EXP_CLAMP)))
                blocks.append(jnp.where(rs + SB * I >= cs, _nt(qI, kI), 0.0))
            A = jnp.concatenate(blocks, axis=0)
            st0 = ST[hd]
            st_ref[0, hd] = st0
            o = _nn(_bf(A), _bf(v)) + _nt(_bf(q * jnp.exp(b)), _bf(st0))
            ST[hd] = st0 * jnp.exp(bend) + _tn_st(v, k * jnp.exp(bend - b))
            o_ref[:, sl] = o
            r = lax.rsqrt(jnp.mean(o * o, axis=-1, keepdims=True) + EPS)
            gr = g_ref[:, sl]
            og_ref[:, sl] = _bf((o * r) * gg_ref[...] * (gr * _sig(gr)))

    col = lambda j: pl.BlockSpec((C, D), lambda c, j=j: (c, j))
    tile = pl.BlockSpec((C, D), lambda c: (c, 0))
    own, rider = _call(
        body, name, (NC,), [col(0), col(1), col(2), col(3), _row(D), _row(LANES)],
        [tile, tile, pl.BlockSpec((1, H, LANES, LANES), lambda c: (c, 0, 0, 0))],
        [jax.ShapeDtypeStruct((T, D), F32), jax.ShapeDtypeStruct((T, D), BF16),
         jax.ShapeDtypeStruct((NC, H, LANES, LANES), F32)],
        [pltpu.VMEM((H, LANES, LANES), F32), pltpu.VMEM((H, C, LANES), F32)], ("arbitrary",),
        (proj, proj, proj, proj, lb, gg), ex)
    return (*own, rider) if ex is not None else own


def hgrn_bwd(proj, o, dog, st, lb, gg, name, ex=None):
    T, D4 = proj.shape
    D = D4 // 4
    H = D // LANES
    C = _rt(T, HG_CHUNK)
    SB = min(HG_SUB, C)
    NC = T // C

    def body(q_ref, f_ref, v_ref, g_ref, o_ref, dog_ref, st_ref, lb_ref, gg_ref, dp_ref, dlb_ref, dgg_ref,
             DST, bex):
        @pl.when(pl.program_id(0) == 0)
        def _():
            DST[...] = jnp.zeros_like(DST)
            dlb_ref[...] = jnp.zeros_like(dlb_ref)
            dgg_ref[...] = jnp.zeros_like(dgg_ref)
        r_i = lax.broadcasted_iota(jnp.int32, (C, C), 0)
        c_i = lax.broadcasted_iota(jnp.int32, (C, C), 1)
        causal = r_i >= c_i
        low = _bf(causal.astype(F32))
        upp = _bf((r_i <= c_i).astype(F32))
        rs = lax.broadcasted_iota(jnp.int32, (SB, C), 0)
        cs = lax.broadcasted_iota(jnp.int32, (SB, C), 1)
        ggv = gg_ref[...]
        for hd in range(H):
            sl = slice(hd * LANES, (hd + 1) * LANES)
            qr = q_ref[:, sl]
            lbv = lb_ref[:, sl]
            sq, q, sig, f, k, lf = _hgrn_gates(qr, f_ref[:, sl], lbv)
            v = v_ref[:, sl]
            oh = o_ref[:, sl]
            r = lax.rsqrt(jnp.mean(oh * oh, axis=-1, keepdims=True) + EPS)
            n = oh * r
            gr = g_ref[:, sl]
            sg = _sig(gr)
            dgo = dog_ref[:, sl]
            dgr = dgo * (n * ggv) * (sg * (1.0 + gr * (1.0 - sg)))
            don = dgo * (gr * sg)
            dgg_ref[...] += jnp.sum(don * n, axis=0, keepdims=True)
            dn = don * ggv
            do = r * (dn - n * jnp.mean(dn * n, axis=-1, keepdims=True))
            b = _tri_mm(low, lf)
            bex[hd] = b - lf
            bend = jnp.sum(lf, axis=0, keepdims=True)
            dob = _bf(do)
            dA = _bf(jnp.where(causal, _nt(dob, _bf(v)), 0.0))
            blocks, dqs, gqs = [], [], []
            dk = jnp.zeros((C, LANES), F32)
            gk = jnp.zeros((C, LANES), F32)
            for I in range(C // SB):
                p = bex[hd, pl.ds(SB * I + SB // 2, 1), :]
                rows = slice(SB * I, SB * (I + 1))
                eq = jnp.exp(jnp.minimum(b[rows] - p, EXP_CLAMP))
                qI = _bf(q[rows] * eq)
                ek = jnp.exp(jnp.minimum(p - b, EXP_CLAMP))
                kI = _bf(k * ek)
                blocks.append(jnp.where(rs + SB * I >= cs, _nt(qI, kI), 0.0))
                dqI = _nn(dA[rows], kI)
                dkI = _tn(dA[rows], qI)
                dqs.append(dqI * eq)
                gqs.append(qI.astype(F32) * dqI)
                dk = dk + dkI * ek
                gk = gk + kI.astype(F32) * dkI
            A = jnp.concatenate(blocks, axis=0)
            dst = DST[hd]
            st0 = st_ref[0, hd]
            eb = jnp.exp(b)
            ee = jnp.exp(bend - b)
            ebend = jnp.exp(bend)
            dv = _tn(_bf(A), dob) + _nt(_bf(k * ee), _bf(dst))
            dk_state = ee * _nn_st(v, dst)
            dq_state = eb * _nn_st(do, st0)
            dq = jnp.concatenate(dqs, axis=0) + dq_state
            dk = dk + dk_state
            DST[hd] = dst * ebend + _tn_st(do, q * eb)
            later = jnp.sum(k * dk_state, axis=0, keepdims=True) + ebend * jnp.sum(st0 * dst, axis=0, keepdims=True)
            pairs = jnp.concatenate(gqs, axis=0) - gk + (q * dq_state - k * dk_state)
            dlf = _tri_mm(upp, pairs) + later
            df = dlf / f - dk
            dlb_ref[:, sl] += jnp.sum(df * (1.0 - sig), axis=0, keepdims=True)
            dp_ref[:, sl] = _bf(dq * (sq * (1.0 + qr * (1.0 - sq))))
            dp_ref[:, D + hd * LANES:D + (hd + 1) * LANES] = _bf(df * (1.0 - lbv) * (sig * (1.0 - sig)))
            dp_ref[:, 2 * D + hd * LANES:2 * D + (hd + 1) * LANES] = _bf(dv)
            dp_ref[:, 3 * D + hd * LANES:3 * D + (hd + 1) * LANES] = _bf(dgr)

    col = lambda j: pl.BlockSpec((C, D), lambda c, j=j: (NC - 1 - c, j))
    tile = pl.BlockSpec((C, D), lambda c: (NC - 1 - c, 0))
    own, rider = _call(
        body, name, (NC,),
        [col(0), col(1), col(2), col(3), tile, tile,
         pl.BlockSpec((1, H, LANES, LANES), lambda c: (NC - 1 - c, 0, 0, 0)), _row(D), _row(LANES)],
        [pl.BlockSpec((C, D4), lambda c: (NC - 1 - c, 0)), _row(D), _row(LANES)],
        [jax.ShapeDtypeStruct((T, D4), BF16), jax.ShapeDtypeStruct((1, D), F32), jax.ShapeDtypeStruct((1, LANES), F32)],
        [pltpu.VMEM((H, LANES, LANES), F32), pltpu.VMEM((H, C, LANES), F32)],
        ("arbitrary",), (proj, proj, proj, proj, o, dog, st, lb, gg), ex)
    return (*own, rider) if ex is not None else own


def _prev_blk(tm, hb):
    return lambda i: (jnp.maximum(i * (tm // hb) - 1, 0), 0)


def _next_blk(tm, hb, T):
    return lambda i: (jnp.minimum((i + 1) * (tm // hb), T // hb - 1), 0)


def _glu_to_ext(uc_ref, ucp_ref, ext, D, first):
    def glu(u):
        return u[:, :D] * _sig(u[:, D:])
    gh = jnp.where(first, 0.0, glu(ucp_ref[...]))
    gm = glu(uc_ref[...])
    for c in range(D // LANES):
        ext[c, 0:CONV_HALO, :] = gh[:, c * LANES:(c + 1) * LANES]
        ext[c, CONV_HALO:, :] = gm[:, c * LANES:(c + 1) * LANES]


def conv_mid_fwd(uc, cw, cb, lg, lbias, name):
    T, D2 = uc.shape
    D = D2 // 2
    NCH = D // LANES
    W = 31
    tm = _rt(T, 256)
    RS = min(CONV_ROWS, tm)

    def body(uc_ref, ucp_ref, cw_ref, cb_ref, lg_ref, lb_ref, cv_ref, va_ref, ext, outs):
        _glu_to_ext(uc_ref, ucp_ref, ext, D, pl.program_id(0) == 0)

        def chunk(c, carry):
            for rb in range(tm // RS):
                acc = jnp.zeros((RS, LANES), F32)
                for kk in range(W):
                    acc = acc + cw_ref[c, pl.ds(kk, 1), :] * ext[c, pl.ds(rb * RS + CONV_HALO - (W - 1) + kk, RS), :]
                outs[c, pl.ds(rb * RS, RS), :] = acc
            return carry
        lax.fori_loop(0, NCH, chunk, 0)
        cv = jnp.concatenate([outs[c] for c in range(NCH)], axis=1) + cb_ref[...]
        cv_ref[...] = cv
        mu = jnp.mean(cv, axis=-1, keepdims=True)
        xc = cv - mu
        rstd = lax.rsqrt(jnp.mean(xc * xc, axis=-1, keepdims=True) + EPS)
        l = xc * rstd * lg_ref[...] + lb_ref[...]
        va_ref[...] = _bf(l * _sig(l))

    tile = pl.BlockSpec((tm, D), lambda i: (i, 0))
    return pl.pallas_call(
        body, name=name, grid=(T // tm,),
        in_specs=[pl.BlockSpec((tm, D2), lambda i: (i, 0)), pl.BlockSpec((CONV_HALO, D2), _prev_blk(tm, CONV_HALO)),
                  pl.BlockSpec((NCH, 32, LANES), lambda i: (0, 0, 0)), _row(D), _row(D), _row(D)],
        out_specs=[tile, tile],
        out_shape=[jax.ShapeDtypeStruct((T, D), F32), jax.ShapeDtypeStruct((T, D), BF16)],
        scratch_shapes=[pltpu.VMEM((NCH, tm + CONV_HALO, LANES), F32), pltpu.VMEM((NCH, tm, LANES), F32)],
        compiler_params=_cp("parallel"))(uc, uc, cw, cb, lg, lbias)


def conv_bwd_a(dva, cv, uc, lg, lbias, name):
    T, D2 = uc.shape
    D = D2 // 2
    NCH = D // LANES
    W = 31
    tm = _rt(T, 256)
    RS = min(CONV_ROWS, tm)

    def body(dva_ref, cv_ref, uc_ref, ucp_ref, lg_ref, lb_ref, dcv_ref, dlg_ref, dlb_ref, taps_ref, ext, dcs):
        @pl.when(pl.program_id(0) == 0)
        def _():
            dlg_ref[...] = jnp.zeros_like(dlg_ref)
            dlb_ref[...] = jnp.zeros_like(dlb_ref)
            taps_ref[...] = jnp.zeros_like(taps_ref)
        _glu_to_ext(uc_ref, ucp_ref, ext, D, pl.program_id(0) == 0)
        cv = cv_ref[...]
        mu = jnp.mean(cv, axis=-1, keepdims=True)
        xc = cv - mu
        rstd = lax.rsqrt(jnp.mean(xc * xc, axis=-1, keepdims=True) + EPS)
        yn = xc * rstd
        l = yn * lg_ref[...] + lb_ref[...]
        s = _sig(l)
        dl = dva_ref[...] * (s * (1.0 + l * (1.0 - s)))
        dlg_ref[...] += jnp.sum(dl * yn, axis=0, keepdims=True)
        dlb_ref[...] += jnp.sum(dl, axis=0, keepdims=True)
        dyn = dl * lg_ref[...]
        dcv = rstd * (dyn - jnp.mean(dyn, axis=-1, keepdims=True) - yn * jnp.mean(dyn * yn, axis=-1, keepdims=True))
        dcv_ref[...] = dcv
        for c in range(NCH):
            dcs[c] = dcv[:, c * LANES:(c + 1) * LANES]

        def fold(p):
            return p.reshape(RS // 8, 8, LANES).sum(axis=0)

        def chunk(c, carry):
            for kk in range(W):
                tot = jnp.zeros((8, LANES), F32)
                for rb in range(tm // RS):
                    tot = tot + fold(dcs[c, pl.ds(rb * RS, RS), :]
                                     * ext[c, pl.ds(rb * RS + CONV_HALO - (W - 1) + kk, RS), :])
                taps_ref[c, pl.ds(8 * kk, 8), :] += tot
            tot = jnp.zeros((8, LANES), F32)
            for rb in range(tm // RS):
                tot = tot + fold(dcs[c, pl.ds(rb * RS, RS), :])
            taps_ref[c, pl.ds(8 * W, 8), :] += tot
            return carry
        lax.fori_loop(0, NCH, chunk, 0)

    tile = pl.BlockSpec((tm, D), lambda i: (i, 0))
    vec = jax.ShapeDtypeStruct((1, D), F32)
    return pl.pallas_call(
        body, name=name, grid=(T // tm,),
        in_specs=[tile, tile, pl.BlockSpec((tm, D2), lambda i: (i, 0)),
                  pl.BlockSpec((CONV_HALO, D2), _prev_blk(tm, CONV_HALO)), _row(D), _row(D)],
        out_specs=[tile, _row(D), _row(D), pl.BlockSpec((NCH, 256, LANES), lambda i: (0, 0, 0))],
        out_shape=[jax.ShapeDtypeStruct((T, D), F32), vec, vec, jax.ShapeDtypeStruct((NCH, 256, LANES), F32)],
        scratch_shapes=[pltpu.VMEM((NCH, tm + CONV_HALO, LANES), F32), pltpu.VMEM((NCH, tm, LANES), F32)],
        compiler_params=_cp("arbitrary"))(dva, cv, uc, uc, lg, lbias)


def conv_bwd_b(dcv, uc, cw, name):
    T, D2 = uc.shape
    D = D2 // 2
    NCH = D // LANES
    W = 31
    tm = _rt(T, 256)
    RS = min(CONV_ROWS, tm)
    NT = T // tm

    def body(dcv_ref, dcn_ref, uc_ref, cw_ref, du_ref, db_ref, ext, outs):
        i = pl.program_id(0)

        @pl.when(i == 0)
        def _():
            db_ref[...] = jnp.zeros_like(db_ref)
        dm = dcv_ref[...]
        dn = jnp.where(i == NT - 1, 0.0, dcn_ref[...])
        for c in range(NCH):
            ext[c, 0:tm, :] = dm[:, c * LANES:(c + 1) * LANES]
            ext[c, tm:, :] = dn[:, c * LANES:(c + 1) * LANES]

        def chunk(c, carry):
            for rb in range(tm // RS):
                acc = jnp.zeros((RS, LANES), F32)
                for kk in range(W):
                    acc = acc + cw_ref[c, pl.ds(kk, 1), :] * ext[c, pl.ds(rb * RS + (W - 1) - kk, RS), :]
                outs[c, pl.ds(rb * RS, RS), :] = acc
            return carry
        lax.fori_loop(0, NCH, chunk, 0)
        dglu = jnp.concatenate([outs[c] for c in range(NCH)], axis=1)
        u = uc_ref[...]
        a = u[:, :D]
        sg = _sig(u[:, D:])
        da = dglu * sg
        dg = dglu * a * (sg * (1.0 - sg))
        du_ref[:, :D] = _bf(da)
        du_ref[:, D:] = _bf(dg)
        db_ref[:, :D] += jnp.sum(da, axis=0, keepdims=True)
        db_ref[:, D:] += jnp.sum(dg, axis=0, keepdims=True)

    tile = pl.BlockSpec((tm, D), lambda i: (i, 0))
    wide = pl.BlockSpec((tm, D2), lambda i: (i, 0))
    return pl.pallas_call(
        body, name=name, grid=(NT,),
        in_specs=[tile, pl.BlockSpec((CONV_HALO, D), _next_blk(tm, CONV_HALO, T)), wide,
                  pl.BlockSpec((NCH, 32, LANES), lambda i: (0, 0, 0))],
        out_specs=[wide, _row(D2)],
        out_shape=[jax.ShapeDtypeStruct((T, D2), BF16), jax.ShapeDtypeStruct((1, D2), F32)],
        scratch_shapes=[pltpu.VMEM((NCH, tm + CONV_HALO, LANES), F32), pltpu.VMEM((NCH, tm, LANES), F32)],
        compiler_params=_cp("arbitrary"))(dcv, dcv, uc, cw)


def _shift_rows(x, halo, k, back):
    rows, n = x.shape
    x3, h3 = x.reshape(rows // 8, 8, n), halo.reshape(1, 8, n)
    sub = lax.broadcasted_iota(jnp.int32, (1, 8, n), 1)
    if back:
        r = pltpu.roll(jnp.concatenate([h3, x3], axis=0), k, 1)
        y = jnp.where(sub < k, r[:-1], r[1:])
    else:
        r = pltpu.roll(jnp.concatenate([x3, h3], axis=0), 8 - k, 1)
        y = jnp.where(sub < 8 - k, r[:-1], r[1:])
    return y.reshape(rows, n)


def _conv3(u, uh, dw_ref, bias_ref):
    return (dw_ref[pl.ds(0, 1), :] * _shift_rows(u, uh, 2, True) + dw_ref[pl.ds(1, 1), :] * _shift_rows(u, uh, 1, True)
            + dw_ref[pl.ds(2, 1), :] * u + bias_ref[...])


def ffn_fwd(h, wup, dww, dwb, wdn, x, gate, pg, nxt, name):
    T, D = h.shape
    fs = wup.shape[2]
    NJ = wup.shape[0] // 2
    tm = _rt(T, 512)
    n_nxt = 0 if nxt is None else 3

    def body(h_ref, hp_ref, wa_ref, wb_ref, dwa_ref, dwb_ref, ba_ref, bb_ref, wd_ref, x_ref, gt_ref, pg_ref, *rest):
        nx, (u_ref, uc_ref, y_ref, xo_ref), more = rest[:n_nxt], rest[n_nxt:n_nxt + 4], rest[n_nxt + 4:]
        acc = more[-1]
        i, j = pl.program_id(0), pl.program_id(1)

        @pl.when(j == 0)
        def _():
            acc[...] = jnp.zeros_like(acc)
        hb = h_ref[...]
        hp = hp_ref[...]

        def half(s, w_ref, dw_ref, b_ref):
            u = _nn(hb, w_ref[...])
            uh = jnp.where(i == 0, 0.0, _nn(hp, w_ref[...]))
            u_ref[s] = u
            uc = _conv3(u, uh, dw_ref, b_ref)
            uc_ref[s] = uc
            return uc
        a = half(0, wa_ref, dwa_ref, ba_ref)
        b = half(1, wb_ref, dwb_ref, bb_ref)
        acc[...] += _nn(_bf(a * _sig(a) * b), wd_ref[...])

        @pl.when(j == NJ - 1)
        def _():
            y = acc[...]
            y_ref[...] = y
            r = lax.rsqrt(jnp.mean(y * y, axis=-1, keepdims=True) + EPS)
            xo = x_ref[...] + gt_ref[...] * ((y * r) * pg_ref[...])
            xo_ref[...] = xo
            if nxt is not None:
                more[0][...] = _next_input(xo, *nx)

    tile = pl.BlockSpec((tm, D), lambda i, j: (i, 0))
    shard = lambda off, r: pl.BlockSpec((None, r, fs), lambda i, j: (j + off, 0, 0))
    ush = pl.BlockSpec((2, None, tm, fs), lambda i, j: (0, j, i, 0))
    vec = pl.BlockSpec((1, D), lambda i, j: (0, 0))
    return pl.pallas_call(
        body, name=name, grid=(T // tm, NJ),
        in_specs=[tile, pl.BlockSpec((FFN_HALO, D), lambda i, j: (jnp.maximum(i * (tm // FFN_HALO) - 1, 0), 0)),
                  shard(0, D), shard(NJ, D), shard(0, 3), shard(NJ, 3), shard(0, 1), shard(NJ, 1),
                  pl.BlockSpec((None, fs, D), lambda i, j: (j, 0, 0)), tile, vec, vec] + [vec] * n_nxt,
        out_specs=[ush, ush, tile, tile] + [tile] * (n_nxt > 0),
        out_shape=[jax.ShapeDtypeStruct((2, NJ, T, fs), F32), jax.ShapeDtypeStruct((2, NJ, T, fs), F32),
                   jax.ShapeDtypeStruct((T, D), F32), jax.ShapeDtypeStruct((T, D), F32)]
        + [jax.ShapeDtypeStruct((T, D), BF16)] * (n_nxt > 0),
        scratch_shapes=[pltpu.VMEM((tm, D), F32)],
        compiler_params=_cp("parallel", "arbitrary"))(h, h, wup, wup, dww, dww, dwb, dwb, wdn, x, gate, pg,
                                                      *(nxt or ()))


def ffn_bwd_a(dy, uc, wdn, name):
    _, NJ, T, fs = uc.shape
    D = dy.shape[1]
    tm = _rt(T, 512)

    NT = T // tm

    def body(dy_ref, uc_ref, wd_ref, duc_ref, dwd_ref, acc):
        i = pl.program_id(1)

        @pl.when(i == 0)
        def _():
            acc[...] = jnp.zeros_like(acc)
        dyb = _bf(dy_ref[...])
        dz = _nt(dyb, wd_ref[...])
        a = uc_ref[0]
        b = uc_ref[1]
        sa = _sig(a)
        asa = a * sa
        acc[...] += _tn(_bf(asa * b), dyb)
        duc_ref[0] = dz * b * (sa + asa * (1.0 - sa))
        duc_ref[1] = dz * asa

        @pl.when(i == NT - 1)
        def _():
            dwd_ref[...] = _bf(acc[...])

    ush = pl.BlockSpec((2, None, tm, fs), lambda j, i: (0, j, i, 0))
    return pl.pallas_call(
        body, name=name, grid=(NJ, NT),
        in_specs=[pl.BlockSpec((tm, D), lambda j, i: (i, 0)), ush, pl.BlockSpec((None, fs, D), lambda j, i: (j, 0, 0))],
        out_specs=[ush, pl.BlockSpec((fs, D), lambda j, i: (j, 0))],
        out_shape=[jax.ShapeDtypeStruct((2, NJ, T, fs), F32), jax.ShapeDtypeStruct((NJ * fs, D), BF16)],
        scratch_shapes=[pltpu.VMEM((fs, D), F32)],
        compiler_params=_cp("parallel", "arbitrary"))(dy, uc, wdn)


def ffn_bwd_b(duc, u, h, dww, wup, name, ex=None):
    NS, T, fs = duc.shape
    D = wup.shape[1]
    tm = _rt(T, 512)
    NT = T // tm

    def body(d_ref, dn_ref, u_ref, h_ref, dw_ref, w_ref, dh_ref, g_ref, dwup_ref, acc, accw, stage):
        i, s = pl.program_id(0), pl.program_id(1)

        @pl.when((i == 0) & (s == 0))
        def _():
            g_ref[...] = jnp.zeros_like(g_ref)

        @pl.when(s == 0)
        def _():
            acc[...] = jnp.zeros_like(acc)
        d = d_ref[...]
        dn = jnp.where(i == NT - 1, 0.0, dn_ref[...])
        d1 = _shift_rows(d, dn, 1, False)
        d2 = _shift_rows(d, dn, 2, False)
        du = _bf(dw_ref[pl.ds(2, 1), :] * d + dw_ref[pl.ds(1, 1), :] * d1 + dw_ref[pl.ds(0, 1), :] * d2)
        acc[...] += _nt(du, w_ref[...])
        dw_part = _tn(h_ref[...], du)

        @pl.when(i == 0)
        def _():
            accw[s] = dw_part

        @pl.when(i > 0)
        def _():
            accw[s] += dw_part

        @pl.when(i == NT - 1)
        def _():
            stage[...] = _bf(accw[s])
            pltpu.sync_copy(stage, dwup_ref.at[s])
        u = u_ref[...]
        g_ref[s, pl.ds(0, 1), :] += jnp.sum(d2 * u, axis=0, keepdims=True)
        g_ref[s, pl.ds(1, 1), :] += jnp.sum(d1 * u, axis=0, keepdims=True)
        g_ref[s, pl.ds(2, 1), :] += jnp.sum(d * u, axis=0, keepdims=True)
        g_ref[s, pl.ds(3, 1), :] += jnp.sum(d, axis=0, keepdims=True)

        @pl.when(s == NS - 1)
        def _():
            dh_ref[...] = acc[...]

    ush = pl.BlockSpec((None, tm, fs), lambda i, s: (s, i, 0))
    unext = pl.BlockSpec((None, FFN_HALO, fs),
                         lambda i, s: (s, jnp.minimum((i + 1) * (tm // FFN_HALO), T // FFN_HALO - 1), 0))
    tile = pl.BlockSpec((tm, D), lambda i, s: (i, 0))
    own, rider = _call(
        body, name, (NT, NS),
        [ush, unext, ush, tile, pl.BlockSpec((None, 3, fs), lambda i, s: (s, 0, 0)),
         pl.BlockSpec((None, D, fs), lambda i, s: (s, 0, 0))],
        [tile, pl.BlockSpec((NS, 8, fs), lambda i, s: (0, 0, 0)), ANY],
        [jax.ShapeDtypeStruct((T, D), F32), jax.ShapeDtypeStruct((NS, 8, fs), F32),
         jax.ShapeDtypeStruct((NS, D, fs), BF16)],
        [pltpu.VMEM((tm, D), F32), pltpu.VMEM((NS, D, fs), F32), pltpu.VMEM((D, fs), BF16)],
        ("arbitrary", "arbitrary"), (duc, duc, u, h, dww, wup), ex, FFN_BWD_VMEM)
    return (*own, rider) if ex is not None else own


def ada_fwd(c_all, w, bias, name):
    L, D, n = w.shape

    def body(c_ref, w_ref, b_ref, o_ref):
        cv = c_ref[...]
        o_ref[...] = _nn(_bf(cv * _sig(cv)), _bf(w_ref[...])) + b_ref[...]

    return pl.pallas_call(
        body, name=name, grid=(L,),
        in_specs=[pl.BlockSpec((N_DEV, D), lambda l: (0, 0)), pl.BlockSpec((None, D, n), lambda l: (l, 0, 0)),
                  pl.BlockSpec((None, 1, n), lambda l: (l, 0, 0))],
        out_specs=pl.BlockSpec((None, N_DEV, n), lambda l: (l, 0, 0)),
        out_shape=jax.ShapeDtypeStruct((L, N_DEV, n), F32), compiler_params=_cp("parallel"))(c_all, w, bias)


def ada_bwd(c_all, dm, name):
    L, _, n = dm.shape
    D = c_all.shape[1]

    def body(c_ref, d_ref, o_ref):
        cv = c_ref[...]
        o_ref[...] = _tn(_bf(cv * _sig(cv)), _bf(d_ref[...]))

    return pl.pallas_call(
        body, name=name, grid=(L,),
        in_specs=[pl.BlockSpec((N_DEV, D), lambda l: (0, 0)), pl.BlockSpec((None, N_DEV, n), lambda l: (l, 0, 0))],
        out_specs=pl.BlockSpec((None, D, n), lambda l: (l, 0, 0)),
        out_shape=jax.ShapeDtypeStruct((L, D, n), F32), compiler_params=_cp("parallel"))(c_all, dm)


def sum_slots(g, name):
    S, R, C = g.shape

    def body(g_ref, o_ref):
        acc = g_ref[0]
        for s in range(1, S):
            acc = acc + g_ref[s]
        o_ref[...] = acc

    return pl.pallas_call(
        body, name=name, grid=(1,), in_specs=[pl.BlockSpec((S, R, C), lambda i: (0, 0, 0))],
        out_specs=pl.BlockSpec((R, C), lambda i: (0, 0)), out_shape=jax.ShapeDtypeStruct((R, C), F32),
        compiler_params=_cp("arbitrary"))(g)


def lb_softmax(logits, name):
    def body(l_ref, s_ref):
        l = l_ref[...]
        e = jnp.exp(l - jnp.max(l, axis=0, keepdims=True))
        s_ref[...] = e / jnp.sum(e, axis=0, keepdims=True)
    return pl.pallas_call(body, name=name, out_shape=jax.ShapeDtypeStruct(logits.shape, F32))(logits)


def adamw(w, gparts, m, v, name):
    P = len(gparts)
    S, Rp, C = gparts[0].shape
    R = P * Rp
    tr = Rp
    budget = (4 * 1024 * 1024) // (4 * (P * S + 7) * C)
    if Rp > budget:
        tr = max(t for t in range(16, budget + 1, 16) if Rp % t == 0)
    per = Rp // tr

    def body(w_ref, *rest):
        g_refs, (m_ref, v_ref, go_ref, d_ref, mo_ref, vo_ref) = rest[:P], rest[P:]
        part = pl.program_id(0) // per
        g = None
        for p, g_ref in enumerate(g_refs):
            gp = g_ref[0].astype(F32)
            for s in range(1, S):
                gp = gp + g_ref[s].astype(F32)
            g = gp if g is None else jnp.where(part == p, gp, g)
        go_ref[...] = g
        mn = ADAM_B1 * m_ref[...] + (1.0 - ADAM_B1) * g
        vn = ADAM_B2 * v_ref[...] + (1.0 - ADAM_B2) * (g * g)
        mo_ref[...] = mn
        vo_ref[...] = vn
        m_hat = mn / (1.0 - ADAM_B1 ** ADAM_STEP)
        v_hat = vn / (1.0 - ADAM_B2 ** ADAM_STEP)
        d_ref[...] = -ADAM_LR * (m_hat / (jnp.sqrt(v_hat) + ADAM_EPS) + ADAM_WD * w_ref[...])

    tile = pl.BlockSpec((tr, C), lambda i: (i, 0))
    slots = [pl.BlockSpec((S, tr, C), lambda i, p=p: (0, jnp.clip(i - p * per, 0, per - 1), 0)) for p in range(P)]
    out = jax.ShapeDtypeStruct((R, C), F32)
    return pl.pallas_call(
        body, name=name, grid=(R // tr,), in_specs=[tile] + slots + [tile, tile],
        out_specs=[tile] * 4, out_shape=[out] * 4, compiler_params=_cp("parallel"))(w, *gparts, m, v)


def _place():
    return lax.axis_index("x"), lax.axis_index("y"), lax.axis_index("c")


def _slot(p):
    return 4 * p[0] + 2 * p[1] + p[2]


class _Gather:
    def __init__(self, xs):
        self.ins = list(xs)
        n = self.n = len(xs)
        self.out_shapes = [jax.ShapeDtypeStruct((N_DEV,) + a.shape, a.dtype) for a in xs]
        self.scratch = [pltpu.SemaphoreType.DMA((n, 7)), pltpu.SemaphoreType.DMA((n, 7)), pltpu.SemaphoreType.DMA((n,))]

    def _parts(self, ins, outs, sems):
        send_sems, recv_sems, local_sems = sems
        x, y, c = _place()
        me, sib = (x, y, c), (x, y, 1 - c)
        chips = [(1 - x, y), (x, 1 - y), (1 - x, 1 - y)]

        def copy(a, k, block, to, src=None):
            dst = outs[a].at[_slot(block)]
            return pltpu.make_async_remote_copy(
                src_ref=dst if src is None else src, dst_ref=dst, send_sem=send_sems.at[a, k],
                recv_sem=recv_sems.at[a, k], device_id=to, device_id_type=MESH)

        mine = [pltpu.make_async_copy(ins[a], outs[a].at[_slot(me)], local_sems.at[a]) for a in range(self.n)]
        first = []
        for a in range(self.n):
            first.append(copy(a, 0, me, sib, src=ins[a]))
            first += [copy(a, 1 + j, me, (*chip, c), src=ins[a]) for j, chip in enumerate(chips)]
        return copy, mine, first, me, sib, chips, c

    def start(self, ins, outs, sems):
        _, mine, first, *_ = self._parts(ins, outs, sems)
        for cp in mine + first:
            cp.start()

    def finish(self, ins, outs, sems):
        copy, mine, first, me, sib, chips, c = self._parts(ins, outs, sems)
        passed = []
        for j, chip in enumerate(chips):
            for a in range(self.n):
                copy(a, 1 + j, (*chip, c), me).wait_recv()
                fwd = copy(a, 4 + j, (*chip, c), sib)
                fwd.start()
                passed.append(fwd)
        for a in range(self.n):
            copy(a, 0, sib, me).wait_recv()
            for j, chip in enumerate(chips):
                copy(a, 4 + j, (*chip, 1 - c), me).wait_recv()
        for cp in first + passed:
            cp.wait_send()
        for cp in mine:
            cp.wait()


def _run_alone(ex, name):
    def body(*refs):
        ni, no = len(ex.ins), len(ex.out_shapes)
        parts = refs[:ni], refs[ni:ni + no], refs[ni + no:]
        ex.start(*parts)
        ex.finish(*parts)
    return pl.pallas_call(body, name=name, in_specs=[ANY] * len(ex.ins), out_specs=[ANY] * len(ex.out_shapes),
                          out_shape=ex.out_shapes, scratch_shapes=ex.scratch)(*ex.ins)


def _ride(body, n_in, n_out, ex, first, last):
    ni, no = len(ex.ins), len(ex.out_shapes)

    def wrapped(*refs):
        a, r_in = refs[:n_in], refs[n_in:n_in + ni]
        b, r_out = refs[n_in + ni:n_in + ni + n_out], refs[n_in + ni + n_out:n_in + ni + n_out + no]
        rest = refs[n_in + ni + n_out + no:]
        s, r_s = rest[:len(rest) - len(ex.scratch)], rest[len(rest) - len(ex.scratch):]

        @pl.when(first())
        def _():
            ex.start(r_in, r_out, r_s)
        body(*a, *b, *s)

        @pl.when(last())
        def _():
            ex.finish(r_in, r_out, r_s)
    return wrapped


def all_gather(xs, name):
    return _run_alone(_Gather(xs), name)


class _Exchange:
    def __init__(self, groups):
        self.ins = [a for grp in groups for a in grp]
        self.where = [(g, i) for g, grp in enumerate(groups) for i in range(len(grp))]
        n = self.n = len(self.ins)
        self.out_shapes = [jax.ShapeDtypeStruct((N_DEV, len(grp)) + grp[0].shape[1:], grp[0].dtype) for grp in groups]
        self.scratch = [pltpu.SemaphoreType.DMA((n, 7)), pltpu.SemaphoreType.DMA((n, 7)), pltpu.SemaphoreType.DMA((n,))]

    def _parts(self, ins, outs, sems):
        send_sems, recv_sems, local_sems = sems
        x, y, c = _place()
        me = (x, y, c)

        def peer(r):
            return (1 - x if r & 4 else x, 1 - y if r & 2 else y, 1 - c if r & 1 else c)

        def land(a, src):
            g, i = self.where[a]
            return outs[g].at[_slot(src), i]

        def copy(a, r, src_dev):
            p = peer(r)
            return pltpu.make_async_remote_copy(
                src_ref=ins[a].at[_slot(p)], dst_ref=land(a, src_dev), send_sem=send_sems.at[a, r - 1],
                recv_sem=recv_sems.at[a, r - 1], device_id=p, device_id_type=MESH)

        mine = [pltpu.make_async_copy(ins[a].at[_slot(me)], land(a, me), local_sems.at[a]) for a in range(self.n)]
        sends = [copy(a, r, me) for r in range(1, N_DEV) for a in range(self.n)]
        arrivals = [copy(a, r, peer(r)) for r in range(1, N_DEV) for a in range(self.n)]
        return mine, sends, arrivals

    def start(self, ins, outs, sems):
        mine, sends, _ = self._parts(ins, outs, sems)
        for cp in mine + sends:
            cp.start()

    def finish(self, ins, outs, sems):
        mine, sends, arrivals = self._parts(ins, outs, sems)
        for cp in arrivals:
            cp.wait_recv()
        for cp in sends:
            cp.wait_send()
        for cp in mine:
            cp.wait()


def all_to_all(groups, name):
    return _run_alone(_Exchange(groups), name)


def _mods(mod, l, D):
    return [mod[l:l + 1, k * D:(k + 1) * D] for k in range(6)]


def _ffn_backward(l, dxo, dy, hb, u, uc, xin, sc2, gains, W, post, ex=None):
    _, NJ, T, fs = u.shape
    duc, d_wdn = ffn_bwd_a(dy, uc, W["wdn"][l], f"ffn{l}_bwd_a")
    dh, taps, d_wup, *rider = ffn_bwd_b(duc.reshape(2 * NJ, T, fs), u.reshape(2 * NJ, T, fs), hb, W["fdw_w"][l],
                                        W["wup"][l], f"ffn{l}_bwd_b", ex)
    dx, da, dsh, *below = resid_bwd(f"ffn{l}_resid_bwd", pre=(dh, xin, dxo, gains["pre_ffn_g"][l:l + 1], sc2), post=post)
    small = dict(da=da, dsh=dsh, dwb=taps[:, 3], dww=taps[:, 0:3])
    return dx, d_wup, d_wdn, small, below, (rider[0] if rider else None)


def _local_step(xt, tgt, mod, gains, W, mine):
    T, D = xt.shape
    zero_d = jnp.zeros((1, D), F32)
    half = lambda g: g.reshape(N_DEV // 2, 2 * g.shape[1], D)
    sh1, sc1, g1, sh2, sc2, g2 = m0 = _mods(mod, 0, D)
    h0, (whin,) = prenorm_mod(xt, gains["pre_mix_g"][0:1], sc1, sh1, "l0_prenorm", _Gather([mine["whin"]]))
    W = dict(W, whin=whin)
    proj, (whout, wup0) = mm_nn(h0, W["whin"], jnp.zeros((1, 4 * D), F32), "hgrn_proj",
                                _Gather([mine["whout"], mine["wup"][0]]))
    o, og, st, (wdn0, wcin, wcout, wup1, wdn1) = hgrn_fwd(
        proj, W["lb"], W["gg"], "hgrn_fwd",
        _Gather([mine["wdn"][0], mine["wcin"], mine["wcout"], mine["wup"][1], mine["wdn"][1]]))
    W = dict(W, whout=whout.reshape(D, D), wcin=wcin, wcout=wcout.reshape(D, D), wup=[wup0, wup1],
             wdn=[half(wdn0), half(wdn1)])
    t1, c1, e1, t2, c2, e2 = m1 = _mods(mod, 1, D)
    y0, x1, h0f = mm_post(og, W["whout"], zero_d, xt, g1, gains["post_mix_g"][0:1],
                          (gains["pre_ffn_g"][0:1], sc2, sh2), "hgrn_out")
    u0, uc0, yf0, x2, h1 = ffn_fwd(h0f, W["wup"][0], W["fdw_w"][0], W["fdw_b"][0], W["wdn"][0], x1, g2,
                                   gains["post_ffn_g"][0:1], (gains["pre_mix_g"][1:2], c1, t1), "ffn0_fwd")
    uc = mm_nn(h1, W["wcin"], W["cb_in"], "conv_in")
    cv, va = conv_mid_fwd(uc, W["cw"], W["cdw_b"], W["ln_g"], W["ln_b"], "conv_mid_fwd")
    y1, x3, h1f = mm_post(va, W["wcout"], W["cb_out"], x2, e1, gains["post_mix_g"][1:2],
                          (gains["pre_ffn_g"][1:2], c2, t2), "conv_out")
    u1, uc1, yf1, x4 = ffn_fwd(h1f, W["wup"][1], W["fdw_w"][1], W["fdw_b"][1], W["wdn"][1], x3, e2,
                               gains["post_ffn_g"][1:2], None, "ffn1_fwd")
    dx4, loss_part, dyf1, dgate_f1, dpost_f1, _ = resid_bwd(
        "loss_resid_bwd", loss=(x4, tgt), post=(yf1, e2, gains["post_ffn_g"][1:2]))
    dx3, d_wup1, d_wdn1, sf1, (dy1, dgate_c, dpost_c, dys_c), _ = _ffn_backward(
        1, dx4, dyf1, h1f, u1, uc1, x3, c2, gains, W, (y1, e1, gains["post_mix_g"][1:2]))
    dva = mm_nt(dy1, W["wcout"][None], "conv_dva", True)
    d_wcout = mm_tn(va, dy1, "conv_dwout", True, True, D, D)
    dcv, dlg, dlbias, taps = conv_bwd_a(dva, cv, uc, W["ln_g"], W["ln_b"], "conv_bwd_a")
    duc, dbin = conv_bwd_b(dcv, uc, W["cw"], "conv_bwd_b")
    dh1 = mm_nt(duc, W["wcin"], "conv_dh", True)
    d_wcin = mm_tn(h1, duc, "conv_dwin", True, True, D, W["wcin"].shape[2])
    dx2, da_c, dsh_c, dyf0, dgate_f0, dpost_f0, _ = resid_bwd(
        "l1_resid_bwd", pre=(dh1, x2, dx3, gains["pre_mix_g"][1:2], c1), post=(yf0, g2, gains["post_ffn_g"][0:1]))
    rows = lambda g: g.reshape(N_DEV, -1, D)
    dx1, d_wup0, d_wdn0, sf0, (dy0, dgate_h, dpost_h, _), (r_wcin, r_wcout, r_wup1, r_wdn1) = _ffn_backward(
        0, dx2, dyf0, h0f, u0, uc0, x1, sc2, gains, W, (y0, g1, gains["post_mix_g"][0:1]),
        _Exchange([[d_wcin], [rows(d_wcout)], [d_wup1], [rows(d_wdn1)]]))
    dog = mm_nt(dy0, W["whout"][None], "hgrn_dog", True)
    d_whout = mm_tn(og, dy0, "hgrn_dwout", True, True, D, D)
    dproj, dlb, dgg, (r_wup0, r_wdn0, r_whout) = hgrn_bwd(
        proj, o, dog, st, W["lb"], W["gg"], "hgrn_bwd", _Exchange([[d_wup0], [rows(d_wdn0)], [rows(d_whout)]]))
    d_whin = mm_tn(h0, dproj, "hgrn_dwin", True, True, D, W["whin"].shape[2])
    dh0, (r_whin,) = mm_nt(dproj, W["whin"], "hgrn_dh", True, _Exchange([[d_whin]]))
    dx0, da_h, dsh_h = resid_bwd("l0_resid_bwd", pre=(dh0, xt, dx1, gains["pre_mix_g"][0:1], sc1))
    sf0.update(dgate=dgate_f0, dpost=dpost_f0)
    sf1.update(dgate=dgate_f1, dpost=dpost_f1)

    big = dict(hgrn_w_in=r_whin, hgrn_w_out=r_whout, conv_w_in=r_wcin, conv_w_out=r_wcout,
               ffn_w_up=[r_wup0, r_wup1], ffn_w_down=[r_wdn0, r_wdn1])
    pm, pf = gains["pre_mix_g"], gains["pre_ffn_g"]
    dmod = jnp.concatenate([
        dsh_h, da_h * pm[0:1], dgate_h, sf0["dsh"], sf0["da"] * pf[0:1], sf0["dgate"],
        dsh_c, da_c * pm[1:2], dgate_c, sf1["dsh"], sf1["da"] * pf[1:2], sf1["dgate"]], axis=1)
    NCH = D // LANES
    tap = taps.reshape(NCH, 32, 8, LANES).sum(axis=2)
    small = dict(
        dmod=dmod,
        pre_mix_g=jnp.concatenate([da_h * (1.0 + sc1), da_c * (1.0 + c1)], axis=0),
        post_mix_g=jnp.concatenate([dpost_h, dpost_c], axis=0),
        pre_ffn_g=jnp.concatenate([sf0["da"] * (1.0 + sc2), sf1["da"] * (1.0 + c2)], axis=0),
        post_ffn_g=jnp.concatenate([sf0["dpost"], sf1["dpost"]], axis=0),
        lb=dlb, gg=dgg,
        ffn_dw_b=jnp.stack([sf0["dwb"], sf1["dwb"]]),
        conv_b_in=dbin,
        conv_dw_w=jnp.transpose(tap[:, :31], (1, 0, 2)).reshape(31, D),
        conv_dw_b=tap[:, 31].reshape(1, D),
        conv_ln_g=dlg, conv_ln_b=dlbias, conv_b_out=dys_c,
        ffn_dw_w=jnp.stack([sf0["dww"], sf1["dww"]]))
    return loss_part, dx0, big, small


SMALL_ORDER = ["dmod", "pre_mix_g", "post_mix_g", "pre_ffn_g", "post_ffn_g", "lb", "gg", "ffn_dw_b", "conv_b_in",
               "conv_dw_w", "conv_dw_b", "conv_ln_g", "conv_ln_b", "conv_b_out", "ffn_dw_w"]


def _pack(parts):
    flat = jnp.concatenate([p.reshape(-1) for p in parts])
    pad = (-flat.shape[0]) % LANES
    if pad:
        flat = jnp.concatenate([flat, jnp.zeros((pad,), F32)])
    return flat.reshape(-1, LANES)


def _unpack(flat, shapes):
    out, off = [], 0
    for s in shapes:
        n = 1
        for d in s:
            n *= d
        out.append(flat[off:off + n].reshape(s))
        off += n
    return out


def _apply_adamw(name, w, g_slots, m, v):
    shp = w.shape
    C = shp[-1]
    R = w.size // C
    parts = g_slots if isinstance(g_slots, list) else [g_slots]
    parts = [p.reshape(p.shape[0], R // len(parts), C) for p in parts]
    g, d, mn, vn = adamw(w.reshape(R, C), parts, m.reshape(R, C), v.reshape(R, C), "adamw_" + name)
    return g.reshape(shp), d.reshape(shp), mn.reshape(shp), vn.reshape(shp)


WEIGHTS = ['ada_w', 'ada_b', 'pre_mix_g', 'post_mix_g', 'pre_ffn_g', 'post_ffn_g', 'hgrn_w_in', 'hgrn_lb_logits',
           'hgrn_gnorm_g', 'hgrn_w_out', 'conv_w_in', 'conv_b_in', 'conv_dw_w', 'conv_dw_b', 'conv_ln_g', 'conv_ln_b',
           'conv_w_out', 'conv_b_out', 'ffn_w_up', 'ffn_dw_w', 'ffn_dw_b', 'ffn_w_down']


def kernel(x, c, ada_w, ada_b, pre_mix_g, post_mix_g, pre_ffn_g, post_ffn_g, hgrn_w_in, hgrn_lb_logits, hgrn_gnorm_g, hgrn_w_out, conv_w_in, conv_b_in, conv_dw_w, conv_dw_b, conv_ln_g, conv_ln_b, conv_w_out, conv_b_out, ffn_w_up, ffn_dw_w, ffn_dw_b, ffn_w_down, loss_target, m_ada_w, m_ada_b, m_pre_mix_g, m_post_mix_g, m_pre_ffn_g, m_post_ffn_g, m_hgrn_w_in, m_hgrn_lb_logits, m_hgrn_gnorm_g, m_hgrn_w_out, m_conv_w_in, m_conv_b_in, m_conv_dw_w, m_conv_dw_b, m_conv_ln_g, m_conv_ln_b, m_conv_w_out, m_conv_b_out, m_ffn_w_up, m_ffn_dw_w, m_ffn_dw_b, m_ffn_w_down, v_ada_w, v_ada_b, v_pre_mix_g, v_post_mix_g, v_pre_ffn_g, v_post_ffn_g, v_hgrn_w_in, v_hgrn_lb_logits, v_hgrn_gnorm_g, v_hgrn_w_out, v_conv_w_in, v_conv_b_in, v_conv_dw_w, v_conv_dw_b, v_conv_ln_g, v_conv_ln_b, v_conv_w_out, v_conv_b_out, v_ffn_w_up, v_ffn_dw_w, v_ffn_dw_b, v_ffn_w_down):
    P = dict(locals())
    _, T, D = x.shape
    me = _slot(_place())
    L = ada_w.shape[0]
    na = ada_w.shape[2]
    fs = ffn_w_up.shape[2]
    nci = conv_w_in.shape[2]
    nd = conv_dw_b.shape[1]
    NCH = D // LANES

    small_in = [c, conv_b_in, conv_dw_w, conv_dw_b, conv_ln_g, conv_ln_b, conv_b_out, ffn_dw_w]
    (packed,) = all_gather([_pack(small_in)], "gather_first")
    mine = dict(whin=_bf(hgrn_w_in[0]), whout=_bf(hgrn_w_out[0]), wcin=_bf(conv_w_in[0]), wcout=_bf(conv_w_out[0]),
                wup=[_bf(ffn_w_up[l]) for l in range(L)], wdn=[_bf(ffn_w_down[l]) for l in range(L)])
    sm = packed.reshape(N_DEV, -1)
    offs = [0]
    for a in small_in:
        offs.append(offs[-1] + a.size)
    piece = lambda k, shp: sm[:, offs[k]:offs[k + 1]].reshape((N_DEV,) + shp)
    c_all = piece(0, (D,))
    dw_nat = jnp.transpose(piece(2, (31, nd)), (1, 0, 2)).reshape(31, D)
    cw = jnp.pad(jnp.transpose(dw_nat.reshape(31, NCH, LANES), (1, 0, 2)), ((0, 0), (0, 1), (0, 0)))
    fdw = piece(7, (L, 3, fs))
    lb_soft = lb_softmax(hgrn_lb_logits, "lb_softmax")
    W = dict(
        lb=lb_soft[0:1], gg=hgrn_gnorm_g, cb_in=piece(1, (nci,)).reshape(1, N_DEV * nci), cw=cw,
        cdw_b=piece(3, (nd,)).reshape(1, D), ln_g=piece(4, (nd,)).reshape(1, D), ln_b=piece(5, (nd,)).reshape(1, D),
        cb_out=piece(6, (nd,)).reshape(1, D), fdw_w=[fdw[:, l] for l in range(L)],
        fdw_b=[ffn_dw_b[l].reshape(N_DEV, 1, fs) for l in range(L)])

    bias_mine = lax.dynamic_slice(ada_b, (0, me * na), (L, na)).reshape(L, 1, na)
    mod_cols = ada_fwd(c_all, ada_w, bias_mine, "ada_fwd")
    (mod_all,) = all_gather([mod_cols], "gather_mod")
    mod = jnp.transpose(lax.dynamic_index_in_dim(mod_all, me, axis=2, keepdims=False), (1, 0, 2)).reshape(L, N_DEV * na)

    gains = dict(pre_mix_g=pre_mix_g, post_mix_g=post_mix_g, pre_ffn_g=pre_ffn_g, post_ffn_g=post_ffn_g)
    loss_part, dx0, big, small = _local_step(x[0], loss_target[0], mod, gains, W, mine)
    loss = lax.psum((0.5 / D) * jnp.sum(loss_part), ("x", "y", "c"))

    parts = [small[k] for k in SMALL_ORDER]
    (sm_all,) = all_gather([_pack(parts)], "gather_small_grads")
    tot = _unpack(sum_slots(sm_all, "sum_small_grads").reshape(-1), [p.shape for p in parts])
    tot = dict(zip(SMALL_ORDER, tot))
    dmod_all = sm_all.reshape(N_DEV, -1)[:, :L * 6 * D].reshape(N_DEV, L, 6 * D)
    dm_mine = jnp.transpose(lax.dynamic_slice(dmod_all, (0, 0, me * na), (N_DEV, L, na)), (1, 0, 2))
    s0 = lb_soft[0:1]
    onehot = (lax.broadcasted_iota(jnp.int32, lb_soft.shape, 0) == 0).astype(F32)
    col = lambda a, n: lax.dynamic_slice_in_dim(a, me * n, n, axis=a.ndim - 1)
    G = {
        'ada_w': ada_bwd(c_all, dm_mine, "ada_bwd")[None],
        'ada_b': tot["dmod"].reshape(1, L, 6 * D),
        'pre_mix_g': tot["pre_mix_g"][None], 'post_mix_g': tot["post_mix_g"][None],
        'pre_ffn_g': tot["pre_ffn_g"][None], 'post_ffn_g': tot["post_ffn_g"][None],
        'hgrn_lb_logits': (tot["lb"] * s0 * (onehot - lb_soft))[None],
        'hgrn_gnorm_g': tot["gg"][None],
        'ffn_dw_b': tot["ffn_dw_b"].reshape(1, L, N_DEV * fs),
        'conv_b_in': col(tot["conv_b_in"], nci)[None],
        'conv_dw_w': col(tot["conv_dw_w"], nd)[None, None],
        'conv_dw_b': col(tot["conv_dw_b"], nd)[None], 'conv_ln_g': col(tot["conv_ln_g"], nd)[None],
        'conv_ln_b': col(tot["conv_ln_b"], nd)[None], 'conv_b_out': col(tot["conv_b_out"], nd)[None],
        'ffn_dw_w': lax.dynamic_index_in_dim(tot["ffn_dw_w"], me, axis=1, keepdims=False)[None],
    }

    G.update(big)
    res = {n: _apply_adamw(n, P[n], G[n], P["m_" + n], P["v_" + n]) for n in WEIGHTS}
    return (loss, dx0[None], *[res[n][0] for n in WEIGHTS], *[res[n][1] for n in WEIGHTS],
            *[res[n][2] for n in WEIGHTS], *[res[n][3] for n in WEIGHTS])
```

```python
import functools

import jax
import jax.numpy as jnp
from jax import lax
from jax.experimental import pallas as pl
from jax.experimental.pallas import tpu as pltpu

F32 = jnp.float32
BF16 = jnp.bfloat16
EPS = 1e-6
LANES = 128
N_DEV = 8
VMEM_LIMIT = 48 * 1024 * 1024
FFN_BWD_VMEM = 58 * 1024 * 1024
MM_ROWS = 1024
HG_CHUNK = 128
HG_SUB = 32
EXP_CLAMP = 80.0
CONV_HALO = 32
FFN_HALO = 8
CONV_ROWS = 32
ADAM_LR, ADAM_B1, ADAM_B2, ADAM_EPS, ADAM_WD, ADAM_STEP = 0.001, 0.9, 0.999, 1e-08, 0.01, 10
MESH = pl.DeviceIdType.MESH
ANY = pl.BlockSpec(memory_space=pl.ANY)


def _cp(*sem):
    return pltpu.CompilerParams(dimension_semantics=sem, vmem_limit_bytes=VMEM_LIMIT)


def _rt(n, t):
    t = min(n, t)
    assert n % t == 0, (n, t)
    return t


def _sig(x):
    return jax.nn.sigmoid(x)


def _nt(a, b):
    return lax.dot_general(a, b, (((1,), (1,)), ((), ())), preferred_element_type=F32)


def _tn(a, b):
    return lax.dot_general(a, b, (((0,), (0,)), ((), ())), preferred_element_type=F32)


def _nn(a, b):
    return jnp.dot(a, b, preferred_element_type=F32)


def _bf(x):
    return x.astype(BF16)


def _nn_st(a, b):
    return _nn(_bf(a), _bf(b))


def _tn_st(a, b):
    return _tn(_bf(a), _bf(b))


def _row(d):
    return pl.BlockSpec((1, d), lambda *_: (0, 0))


def prenorm_mod(x, g, sc, sh, name, ex=None):
    T, D = x.shape
    tm = _rt(T, 512)

    def body(x_ref, g_ref, sc_ref, sh_ref, h_ref):
        h_ref[...] = _next_input(x_ref[...], g_ref, sc_ref, sh_ref)

    tile = pl.BlockSpec((tm, D), lambda i: (i, 0))
    (h,), rider = _call(body, name, (T // tm,), [tile, _row(D), _row(D), _row(D)], [tile],
                        [jax.ShapeDtypeStruct((T, D), BF16)], [], ("parallel",), (x, g, sc, sh), ex)
    return (h, rider) if ex is not None else h


def resid_bwd(name, pre=None, loss=None, post=None):
    T, D = (pre[1] if pre is not None else loss[0]).shape
    tm = _rt(T, 512)
    n_src = 5 if pre is not None else 2
    n_in = n_src + (3 if post is not None else 0)

    def body(*refs):
        ins, outs = refs[:n_in], refs[n_in:]
        first = pl.program_id(0) == 0
        if pre is not None:
            dh_ref, x_ref, dr_ref, g_ref, sc_ref = ins[:5]
            dx_ref, da_ref, dsh_ref = outs[:3]

            @pl.when(first)
            def _():
                da_ref[...] = jnp.zeros_like(da_ref)
                dsh_ref[...] = jnp.zeros_like(dsh_ref)
            xv = x_ref[...]
            dh = dh_ref[...]
            r = lax.rsqrt(jnp.mean(xv * xv, axis=-1, keepdims=True) + EPS)
            n = xv * r
            da_ref[...] += jnp.sum(dh * n, axis=0, keepdims=True)
            dsh_ref[...] += jnp.sum(dh, axis=0, keepdims=True)
            dn = dh * (g_ref[...] * (1.0 + sc_ref[...]))
            dx = dr_ref[...] + r * (dn - n * jnp.mean(dn * n, axis=-1, keepdims=True))
            dx_ref[...] = dx
            rest = outs[3:]
        else:
            x_ref, t_ref = ins[:2]
            dx_ref, part_ref = outs[:2]

            @pl.when(first)
            def _():
                part_ref[...] = jnp.zeros_like(part_ref)
            e = x_ref[...] - t_ref[...]
            dx = e * (1.0 / D)
            dx_ref[...] = dx
            e2 = (e * e).reshape(tm // 8, 8, D).sum(axis=0)
            acc = e2[:, 0:LANES]
            for j in range(1, D // LANES):
                acc = acc + e2[:, j * LANES:(j + 1) * LANES]
            part_ref[...] += acc
            rest = outs[2:]
        if post is not None:
            y_ref, gt_ref, pg_ref = ins[n_src:]
            dy_ref, dgate_ref, dpg_ref, dys_ref = rest

            @pl.when(first)
            def _():
                dgate_ref[...] = jnp.zeros_like(dgate_ref)
                dpg_ref[...] = jnp.zeros_like(dpg_ref)
                dys_ref[...] = jnp.zeros_like(dys_ref)
            yv = y_ref[...]
            ry = lax.rsqrt(jnp.mean(yv * yv, axis=-1, keepdims=True) + EPS)
            ny = yv * ry
            s = jnp.sum(dx * ny, axis=0, keepdims=True)
            dgate_ref[...] += s * pg_ref[...]
            dpg_ref[...] += s * gt_ref[...]
            dny = dx * (gt_ref[...] * pg_ref[...])
            dy = ry * (dny - ny * jnp.mean(dny * ny, axis=-1, keepdims=True))
            dy_ref[...] = _bf(dy)
            dys_ref[...] += jnp.sum(dy, axis=0, keepdims=True)

    tile = pl.BlockSpec((tm, D), lambda i: (i, 0))
    vec = jax.ShapeDtypeStruct((1, D), F32)
    big = jax.ShapeDtypeStruct((T, D), F32)
    if pre is not None:
        operands, in_specs = list(pre), [tile, tile, tile, _row(D), _row(D)]
        out_specs, out_shape = [tile, _row(D), _row(D)], [big, vec, vec]
    else:
        operands, in_specs = list(loss), [tile, tile]
        out_specs, out_shape = [tile, pl.BlockSpec((8, LANES), lambda i: (0, 0))], [big, jax.ShapeDtypeStruct((8, LANES), F32)]
    if post is not None:
        operands, in_specs = operands + list(post), in_specs + [tile, _row(D), _row(D)]
        out_specs = out_specs + [tile, _row(D), _row(D), _row(D)]
        out_shape = out_shape + [jax.ShapeDtypeStruct((T, D), BF16), vec, vec, vec]
    return pl.pallas_call(body, name=name, grid=(T // tm,), in_specs=in_specs, out_specs=out_specs, out_shape=out_shape,
                          compiler_params=_cp("arbitrary"))(*operands)


def _call(body, name, grid, in_specs, out_specs, out_shape, scratch, sems, operands, ex=None, vmem=VMEM_LIMIT):
    if ex is not None:
        def first():
            return functools.reduce(lambda p, q: p & q, [pl.program_id(k) == 0 for k in range(len(grid))])

        def last():
            return functools.reduce(lambda p, q: p & q, [pl.program_id(k) == grid[k] - 1 for k in range(len(grid))])
        body = _ride(body, len(in_specs), len(out_specs), ex, first, last)
        in_specs = in_specs + [ANY] * len(ex.ins)
        out_specs = out_specs + [ANY] * len(ex.out_shapes)
        out_shape = out_shape + ex.out_shapes
        scratch = scratch + ex.scratch
        operands = list(operands) + ex.ins
        sems = ("arbitrary",) * len(grid)
    res = pl.pallas_call(body, name=name, grid=grid, in_specs=in_specs, out_specs=out_specs, out_shape=out_shape,
                         scratch_shapes=scratch,
                         compiler_params=pltpu.CompilerParams(dimension_semantics=sems, vmem_limit_bytes=vmem))(*operands)
    n_own = len(res) - (len(ex.out_shapes) if ex is not None else 0)
    return res[:n_own], res[n_own:]


def mm_nn(a, bg, bias, name, ex=None):
    M, K = a.shape
    G, _, n = bg.shape
    tm = _rt(M, MM_ROWS)

    def body(a_ref, b_ref, bias_ref, o_ref):
        o_ref[...] = _nn(a_ref[...], b_ref[...]) + bias_ref[...]

    (out,), rider = _call(
        body, name, (M // tm, G),
        [pl.BlockSpec((tm, K), lambda i, j: (i, 0)), pl.BlockSpec((None, K, n), lambda i, j: (j, 0, 0)),
         pl.BlockSpec((1, n), lambda i, j: (0, j))],
        [pl.BlockSpec((tm, n), lambda i, j: (i, j))], [jax.ShapeDtypeStruct((M, G * n), F32)], [],
        ("parallel", "arbitrary"), (a, bg, bias), ex)
    return (out, rider) if ex is not None else out


def _next_input(xo, g_ref, sc_ref, sh_ref):
    r = lax.rsqrt(jnp.mean(xo * xo, axis=-1, keepdims=True) + EPS)
    return _bf((xo * r) * g_ref[...] * (1.0 + sc_ref[...]) + sh_ref[...])


def mm_post(a, b, bias, x, gate, pg, nxt, name):
    T, K = a.shape
    D = b.shape[1]
    tm = _rt(T, 512)

    def body(a_ref, b_ref, bias_ref, x_ref, gt_ref, pg_ref, g_ref, sc_ref, sh_ref, y_ref, xo_ref, h_ref):
        y = _nn(a_ref[...], b_ref[...]) + bias_ref[...]
        y_ref[...] = y
        r = lax.rsqrt(jnp.mean(y * y, axis=-1, keepdims=True) + EPS)
        xo = x_ref[...] + gt_ref[...] * ((y * r) * pg_ref[...])
        xo_ref[...] = xo
        h_ref[...] = _next_input(xo, g_ref, sc_ref, sh_ref)

    tile = pl.BlockSpec((tm, D), lambda i: (i, 0))
    out = jax.ShapeDtypeStruct((T, D), F32)
    return pl.pallas_call(
        body, name=name, grid=(T // tm,),
        in_specs=[pl.BlockSpec((tm, K), lambda i: (i, 0)), pl.BlockSpec((K, D), lambda i: (0, 0)), _row(D), tile]
        + [_row(D)] * 5,
        out_specs=[tile, tile, tile], out_shape=[out, out, jax.ShapeDtypeStruct((T, D), BF16)],
        compiler_params=_cp("parallel"))(a, b, bias, x, gate, pg, *nxt)


def mm_nt(dy, w, name, ex=None):
    T, N = dy.shape
    K = w.shape[0]
    tm = _rt(T, 512)

    def body(dy_ref, w_ref, o_ref):
        o_ref[...] = _nt(_bf(dy_ref[...]), w_ref[...])

    (out,), rider = _call(
        body, name, (T // tm,), [pl.BlockSpec((tm, N), lambda i: (i, 0)), pl.BlockSpec((K, N), lambda i: (0, 0))],
        [pl.BlockSpec((tm, K), lambda i: (i, 0))], [jax.ShapeDtypeStruct((T, K), F32)], [], ("parallel",),
        (dy, w), ex)
    return (out, rider) if ex is not None else out


def mm_tn(xm, gm, name, x_natural, g_natural, kx, n):
    if x_natural:
        T, Gx = xm.shape[0], xm.shape[1] // kx
    else:
        Gx, T = xm.shape[0], xm.shape[1]
    tt = _rt(T, 4 * MM_ROWS)
    if x_natural:
        xspec = pl.BlockSpec((tt, kx), lambda g, k, t: (t, k))
    else:
        xspec = pl.BlockSpec((None, tt, kx), lambda g, k, t: (k, t, 0))
    if g_natural:
        Gg = gm.shape[1] // n
        gspec = pl.BlockSpec((tt, n), lambda g, k, t: (t, g))
    else:
        Gg = gm.shape[0]
        gspec = pl.BlockSpec((None, tt, n), lambda g, k, t: (g, t, 0))

    NT = T // tt

    def body(x_ref, g_ref, o_ref, acc):
        t = pl.program_id(2)

        @pl.when(t == 0)
        def _():
            acc[...] = jnp.zeros_like(acc)
        acc[...] += _tn(_bf(x_ref[...]), _bf(g_ref[...]))

        @pl.when(t == NT - 1)
        def _():
            o_ref[...] = acc[...].astype(o_ref.dtype)

    return pl.pallas_call(
        body, name=name, grid=(Gg, Gx, NT), in_specs=[xspec, gspec],
        out_specs=pl.BlockSpec((None, kx, n), lambda g, k, t: (g, k, 0)),
        out_shape=jax.ShapeDtypeStruct((Gg, Gx * kx, n), BF16), scratch_shapes=[pltpu.VMEM((kx, n), F32)],
        compiler_params=_cp("parallel", "parallel", "arbitrary"))(xm, gm)


def _split3(x):
    h = _bf(x)
    r = x - h.astype(F32)
    m = _bf(r)
    lo = _bf(r - m.astype(F32))
    return h, m, lo


def _tri_mm(tri, x):
    h, m, lo = _split3(x)
    return _nn(tri, lo) + _nn(tri, m) + _nn(tri, h)


def _hgrn_gates(qr, fz, lb):
    sq = _sig(qr)
    q = qr * sq
    sig = _sig(fz)
    f = lb + (1.0 - lb) * sig
    k = 1.0 - f
    return sq, q, sig, f, k, jnp.log(f)


def hgrn_fwd(proj, lb, gg, name, ex=None):
    T, D4 = proj.shape
    D = D4 // 4
    H = D // LANES
    C = _rt(T, HG_CHUNK)
    SB = min(HG_SUB, C)
    NC = T // C

    def body(q_ref, f_ref, v_ref, g_ref, lb_ref, gg_ref, o_ref, og_ref, st_ref, ST, bex):
        @pl.when(pl.program_id(0) == 0)
        def _():
            ST[...] = jnp.zeros_like(ST)
        r_i = lax.broadcasted_iota(jnp.int32, (C, C), 0)
        c_i = lax.broadcasted_iota(jnp.int32, (C, C), 1)
        low = _bf((r_i >= c_i).astype(F32))
        rs = lax.broadcasted_iota(jnp.int32, (SB, C), 0)
        cs = lax.broadcasted_iota(jnp.int32, (SB, C), 1)
        for hd in range(H):
            sl = slice(hd * LANES, (hd + 1) * LANES)
            _, q, _, _, k, lf = _hgrn_gates(q_ref[:, sl], f_ref[:, sl], lb_ref[:, sl])
            v = v_ref[:, sl]
            b = _tri_mm(low, lf)
            bex[hd] = b - lf
            bend = jnp.sum(lf, axis=0, keepdims=True)
            blocks = []
            for I in range(C // SB):
                p = bex[hd, pl.ds(SB * I + SB // 2, 1), :]
                rows = slice(SB * I, SB * (I + 1))
                qI = _bf(q[rows] * jnp.exp(jnp.minimum(b[rows] - p, EXP_CLAMP)))
                kI = _bf(k * jnp.exp(jnp.minimum(p - b, EXP_CLAMP)))
                blocks.append(jnp.where(rs + SB * I >= cs, _nt(qI, kI), 0.0))
            A = jnp.concatenate(blocks, axis=0)
            st0 = ST[hd]
            st_ref[0, hd] = st0
            o = _nn(_bf(A), _bf(v)) + _nt(_bf(q * jnp.exp(b)), _bf(st0))
            ST[hd] = st0 * jnp.exp(bend) + _tn_st(v, k * jnp.exp(bend - b))
            o_ref[:, sl] = o
            r = lax.rsqrt(jnp.mean(o * o, axis=-1, keepdims=True) + EPS)
            gr = g_ref[:, sl]
            og_ref[:, sl] = _bf((o * r) * gg_ref[...] * (gr * _sig(gr)))

    col = lambda j: pl.BlockSpec((C, D), lambda c, j=j: (c, j))
    tile = pl.BlockSpec((C, D), lambda c: (c, 0))
    own, rider = _call(
        body, name, (NC,), [col(0), col(1), col(2), col(3), _row(D), _row(LANES)],
        [tile, tile, pl.BlockSpec((1, H, LANES, LANES), lambda c: (c, 0, 0, 0))],
        [jax.ShapeDtypeStruct((T, D), F32), jax.ShapeDtypeStruct((T, D), BF16),
         jax.ShapeDtypeStruct((NC, H, LANES, LANES), F32)],
        [pltpu.VMEM((H, LANES, LANES), F32), pltpu.VMEM((H, C, LANES), F32)], ("arbitrary",),
        (proj, proj, proj, proj, lb, gg), ex)
    return (*own, rider) if ex is not None else own


def hgrn_bwd(proj, o, dog, st, lb, gg, name, ex=None):
    T, D4 = proj.shape
    D = D4 // 4
    H = D // LANES
    C = _rt(T, HG_CHUNK)
    SB = min(HG_SUB, C)
    NC = T // C

    def body(q_ref, f_ref, v_ref, g_ref, o_ref, dog_ref, st_ref, lb_ref, gg_ref, dp_ref, dlb_ref, dgg_ref,
             DST, bex):
        @pl.when(pl.program_id(0) == 0)
        def _():
            DST[...] = jnp.zeros_like(DST)
            dlb_ref[...] = jnp.zeros_like(dlb_ref)
            dgg_ref[...] = jnp.zeros_like(dgg_ref)
        r_i = lax.broadcasted_iota(jnp.int32, (C, C), 0)
        c_i = lax.broadcasted_iota(jnp.int32, (C, C), 1)
        causal = r_i >= c_i
        low = _bf(causal.astype(F32))
        upp = _bf((r_i <= c_i).astype(F32))
        rs = lax.broadcasted_iota(jnp.int32, (SB, C), 0)
        cs = lax.broadcasted_iota(jnp.int32, (SB, C), 1)
        ggv = gg_ref[...]
        for hd in range(H):
            sl = slice(hd * LANES, (hd + 1) * LANES)
            qr = q_ref[:, sl]
            lbv = lb_ref[:, sl]
            sq, q, sig, f, k, lf = _hgrn_gates(qr, f_ref[:, sl], lbv)
            v = v_ref[:, sl]
            oh = o_ref[:, sl]
            r = lax.rsqrt(jnp.mean(oh * oh, axis=-1, keepdims=True) + EPS)
            n = oh * r
            gr = g_ref[:, sl]
            sg = _sig(gr)
            dgo = dog_ref[:, sl]
            dgr = dgo * (n * ggv) * (sg * (1.0 + gr * (1.0 - sg)))
            don = dgo * (gr * sg)
            dgg_ref[...] += jnp.sum(don * n, axis=0, keepdims=True)
            dn = don * ggv
            do = r * (dn - n * jnp.mean(dn * n, axis=-1, keepdims=True))
            b = _tri_mm(low, lf)
            bex[hd] = b - lf
            bend = jnp.sum(lf, axis=0, keepdims=True)
            dob = _bf(do)
            dA = _bf(jnp.where(causal, _nt(dob, _bf(v)), 0.0))
            blocks, dqs, gqs = [], [], []
            dk = jnp.zeros((C, LANES), F32)
            gk = jnp.zeros((C, LANES), F32)
            for I in range(C // SB):
                p = bex[hd, pl.ds(SB * I + SB // 2, 1), :]
                rows = slice(SB * I, SB * (I + 1))
                eq = jnp.exp(jnp.minimum(b[rows] - p, EXP_CLAMP))
                qI = _bf(q[rows] * eq)
                ek = jnp.exp(jnp.minimum(p - b, EXP_CLAMP))
                kI = _bf(k * ek)
                blocks.append(jnp.where(rs + SB * I >= cs, _nt(qI, kI), 0.0))
                dqI = _nn(dA[rows], kI)
                dkI = _tn(dA[rows], qI)
                dqs.append(dqI * eq)
                gqs.append(qI.astype(F32) * dqI)
                dk = dk + dkI * ek
                gk = gk + kI.astype(F32) * dkI
            A = jnp.concatenate(blocks, axis=0)
            dst = DST[hd]
            st0 = st_ref[0, hd]
            eb = jnp.exp(b)
            ee = jnp.exp(bend - b)
            ebend = jnp.exp(bend)
            dv = _tn(_bf(A), dob) + _nt(_bf(k * ee), _bf(dst))
            dk_state = ee * _nn_st(v, dst)
            dq_state = eb * _nn_st(do, st0)
            dq = jnp.concatenate(dqs, axis=0) + dq_state
            dk = dk + dk_state
            DST[hd] = dst * ebend + _tn_st(do, q * eb)
            later = jnp.sum(k * dk_state, axis=0, keepdims=True) + ebend * jnp.sum(st0 * dst, axis=0, keepdims=True)
            pairs = jnp.concatenate(gqs, axis=0) - gk + (q * dq_state - k * dk_state)
            dlf = _tri_mm(upp, pairs) + later
            df = dlf / f - dk
            dlb_ref[:, sl] += jnp.sum(df * (1.0 - sig), axis=0, keepdims=True)
            dp_ref[:, sl] = _bf(dq * (sq * (1.0 + qr * (1.0 - sq))))
            dp_ref[:, D + hd * LANES:D + (hd + 1) * LANES] = _bf(df * (1.0 - lbv) * (sig * (1.0 - sig)))
            dp_ref[:, 2 * D + hd * LANES:2 * D + (hd + 1) * LANES] = _bf(dv)
            dp_ref[:, 3 * D + hd * LANES:3 * D + (hd + 1) * LANES] = _bf(dgr)

    col = lambda j: pl.BlockSpec((C, D), lambda c, j=j: (NC - 1 - c, j))
    tile = pl.BlockSpec((C, D), lambda c: (NC - 1 - c, 0))
    own, rider = _call(
        body, name, (NC,),
        [col(0), col(1), col(2), col(3), tile, tile,
         pl.BlockSpec((1, H, LANES, LANES), lambda c: (NC - 1 - c, 0, 0, 0)), _row(D), _row(LANES)],
        [pl.BlockSpec((C, D4), lambda c: (NC - 1 - c, 0)), _row(D), _row(LANES)],
        [jax.ShapeDtypeStruct((T, D4), BF16), jax.ShapeDtypeStruct((1, D), F32), jax.ShapeDtypeStruct((1, LANES), F32)],
        [pltpu.VMEM((H, LANES, LANES), F32), pltpu.VMEM((H, C, LANES), F32)],
        ("arbitrary",), (proj, proj, proj, proj, o, dog, st, lb, gg), ex)
    return (*own, rider) if ex is not None else own


def _prev_blk(tm, hb):
    return lambda i: (jnp.maximum(i * (tm // hb) - 1, 0), 0)


def _next_blk(tm, hb, T):
    return lambda i: (jnp.minimum((i + 1) * (tm // hb), T // hb - 1), 0)


def _glu_to_ext(uc_ref, ucp_ref, ext, D, first):
    def glu(u):
        return u[:, :D] * _sig(u[:, D:])
    gh = jnp.where(first, 0.0, glu(ucp_ref[...]))
    gm = glu(uc_ref[...])
    for c in range(D // LANES):
        ext[c, 0:CONV_HALO, :] = gh[:, c * LANES:(c + 1) * LANES]
        ext[c, CONV_HALO:, :] = gm[:, c * LANES:(c + 1) * LANES]


def conv_mid_fwd(uc, cw, cb, lg, lbias, name):
    T, D2 = uc.shape
    D = D2 // 2
    NCH = D // LANES
    W = 31
    tm = _rt(T, 256)
    RS = min(CONV_ROWS, tm)

    def body(uc_ref, ucp_ref, cw_ref, cb_ref, lg_ref, lb_ref, cv_ref, va_ref, ext, outs):
        _glu_to_ext(uc_ref, ucp_ref, ext, D, pl.program_id(0) == 0)

        def chunk(c, carry):
            for rb in range(tm // RS):
                acc = jnp.zeros((RS, LANES), F32)
                for kk in range(W):
                    acc = acc + cw_ref[c, pl.ds(kk, 1), :] * ext[c, pl.ds(rb * RS + CONV_HALO - (W - 1) + kk, RS), :]
                outs[c, pl.ds(rb * RS, RS), :] = acc
            return carry
        lax.fori_loop(0, NCH, chunk, 0)
        cv = jnp.concatenate([outs[c] for c in range(NCH)], axis=1) + cb_ref[...]
        cv_ref[...] = cv
        mu = jnp.mean(cv, axis=-1, keepdims=True)
        xc = cv - mu
        rstd = lax.rsqrt(jnp.mean(xc * xc, axis=-1, keepdims=True) + EPS)
        l = xc * rstd * lg_ref[...] + lb_ref[...]
        va_ref[...] = _bf(l * _sig(l))

    tile = pl.BlockSpec((tm, D), lambda i: (i, 0))
    return pl.pallas_call(
        body, name=name, grid=(T // tm,),
        in_specs=[pl.BlockSpec((tm, D2), lambda i: (i, 0)), pl.BlockSpec((CONV_HALO, D2), _prev_blk(tm, CONV_HALO)),
                  pl.BlockSpec((NCH, 32, LANES), lambda i: (0, 0, 0)), _row(D), _row(D), _row(D)],
        out_specs=[tile, tile],
        out_shape=[jax.ShapeDtypeStruct((T, D), F32), jax.ShapeDtypeStruct((T, D), BF16)],
        scratch_shapes=[pltpu.VMEM((NCH, tm + CONV_HALO, LANES), F32), pltpu.VMEM((NCH, tm, LANES), F32)],
        compiler_params=_cp("parallel"))(uc, uc, cw, cb, lg, lbias)


def conv_bwd_a(dva, cv, uc, lg, lbias, name):
    T, D2 = uc.shape
    D = D2 // 2
    NCH = D // LANES
    W = 31
    tm = _rt(T, 256)
    RS = min(CONV_ROWS, tm)

    def body(dva_ref, cv_ref, uc_ref, ucp_ref, lg_ref, lb_ref, dcv_ref, dlg_ref, dlb_ref, taps_ref, ext, dcs):
        @pl.when(pl.program_id(0) == 0)
        def _():
            dlg_ref[...] = jnp.zeros_like(dlg_ref)
            dlb_ref[...] = jnp.zeros_like(dlb_ref)
            taps_ref[...] = jnp.zeros_like(taps_ref)
        _glu_to_ext(uc_ref, ucp_ref, ext, D, pl.program_id(0) == 0)
        cv = cv_ref[...]
        mu = jnp.mean(cv, axis=-1, keepdims=True)
        xc = cv - mu
        rstd = lax.rsqrt(jnp.mean(xc * xc, axis=-1, keepdims=True) + EPS)
        yn = xc * rstd
        l = yn * lg_ref[...] + lb_ref[...]
        s = _sig(l)
        dl = dva_ref[...] * (s * (1.0 + l * (1.0 - s)))
        dlg_ref[...] += jnp.sum(dl * yn, axis=0, keepdims=True)
        dlb_ref[...] += jnp.sum(dl, axis=0, keepdims=True)
        dyn = dl * lg_ref[...]
        dcv = rstd * (dyn - jnp.mean(dyn, axis=-1, keepdims=True) - yn * jnp.mean(dyn * yn, axis=-1, keepdims=True))
        dcv_ref[...] = dcv
        for c in range(NCH):
            dcs[c] = dcv[:, c * LANES:(c + 1) * LANES]

        def fold(p):
            return p.reshape(RS // 8, 8, LANES).sum(axis=0)

        def chunk(c, carry):
            for kk in range(W):
                tot = jnp.zeros((8, LANES), F32)
                for rb in range(tm // RS):
                    tot = tot + fold(dcs[c, pl.ds(rb * RS, RS), :]
                                     * ext[c, pl.ds(rb * RS + CONV_HALO - (W - 1) + kk, RS), :])
                taps_ref[c, pl.ds(8 * kk, 8), :] += tot
            tot = jnp.zeros((8, LANES), F32)
            for rb in range(tm // RS):
                tot = tot + fold(dcs[c, pl.ds(rb * RS, RS), :])
            taps_ref[c, pl.ds(8 * W, 8), :] += tot
            return carry
        lax.fori_loop(0, NCH, chunk, 0)

    tile = pl.BlockSpec((tm, D), lambda i: (i, 0))
    vec = jax.ShapeDtypeStruct((1, D), F32)
    return pl.pallas_call(
        body, name=name, grid=(T // tm,),
        in_specs=[tile, tile, pl.BlockSpec((tm, D2), lambda i: (i, 0)),
                  pl.BlockSpec((CONV_HALO, D2), _prev_blk(tm, CONV_HALO)), _row(D), _row(D)],
        out_specs=[tile, _row(D), _row(D), pl.BlockSpec((NCH, 256, LANES), lambda i: (0, 0, 0))],
        out_shape=[jax.ShapeDtypeStruct((T, D), F32), vec, vec, jax.ShapeDtypeStruct((NCH, 256, LANES), F32)],
        scratch_shapes=[pltpu.VMEM((NCH, tm + CONV_HALO, LANES), F32), pltpu.VMEM((NCH, tm, LANES), F32)],
        compiler_params=_cp("arbitrary"))(dva, cv, uc, uc, lg, lbias)


def conv_bwd_b(dcv, uc, cw, name):
    T, D2 = uc.shape
    D = D2 // 2
    NCH = D // LANES
    W = 31
    tm = _rt(T, 256)
    RS = min(CONV_ROWS, tm)
    NT = T // tm

    def body(dcv_ref, dcn_ref, uc_ref, cw_ref, du_ref, db_ref, ext, outs):
        i = pl.program_id(0)

        @pl.when(i == 0)
        def _():
            db_ref[...] = jnp.zeros_like(db_ref)
        dm = dcv_ref[...]
        dn = jnp.where(i == NT - 1, 0.0, dcn_ref[...])
        for c in range(NCH):
            ext[c, 0:tm, :] = dm[:, c * LANES:(c + 1) * LANES]
            ext[c, tm:, :] = dn[:, c * LANES:(c + 1) * LANES]

        def chunk(c, carry):
            for rb in range(tm // RS):
                acc = jnp.zeros((RS, LANES), F32)
                for kk in range(W):
                    acc = acc + cw_ref[c, pl.ds(kk, 1), :] * ext[c, pl.ds(rb * RS + (W - 1) - kk, RS), :]
                outs[c, pl.ds(rb * RS, RS), :] = acc
            return carry
        lax.fori_loop(0, NCH, chunk, 0)
        dglu = jnp.concatenate([outs[c] for c in range(NCH)], axis=1)
        u = uc_ref[...]
        a = u[:, :D]
        sg = _sig(u[:, D:])
        da = dglu * sg
        dg = dglu * a * (sg * (1.0 - sg))
        du_ref[:, :D] = _bf(da)
        du_ref[:, D:] = _bf(dg)
        db_ref[:, :D] += jnp.sum(da, axis=0, keepdims=True)
        db_ref[:, D:] += jnp.sum(dg, axis=0, keepdims=True)

    tile = pl.BlockSpec((tm, D), lambda i: (i, 0))
    wide = pl.BlockSpec((tm, D2), lambda i: (i, 0))
    return pl.pallas_call(
        body, name=name, grid=(NT,),
        in_specs=[tile, pl.BlockSpec((CONV_HALO, D), _next_blk(tm, CONV_HALO, T)), wide,
                  pl.BlockSpec((NCH, 32, LANES), lambda i: (0, 0, 0))],
        out_specs=[wide, _row(D2)],
        out_shape=[jax.ShapeDtypeStruct((T, D2), BF16), jax.ShapeDtypeStruct((1, D2), F32)],
        scratch_shapes=[pltpu.VMEM((NCH, tm + CONV_HALO, LANES), F32), pltpu.VMEM((NCH, tm, LANES), F32)],
        compiler_params=_cp("arbitrary"))(dcv, dcv, uc, cw)


def _shift_rows(x, halo, k, back):
    rows, n = x.shape
    x3, h3 = x.reshape(rows // 8, 8, n), halo.reshape(1, 8, n)
    sub = lax.broadcasted_iota(jnp.int32, (1, 8, n), 1)
    if back:
        r = pltpu.roll(jnp.concatenate([h3, x3], axis=0), k, 1)
        y = jnp.where(sub < k, r[:-1], r[1:])
    else:
        r = pltpu.roll(jnp.concatenate([x3, h3], axis=0), 8 - k, 1)
        y = jnp.where(sub < 8 - k, r[:-1], r[1:])
    return y.reshape(rows, n)


def _conv3(u, uh, dw_ref, bias_ref):
    return (dw_ref[pl.ds(0, 1), :] * _shift_rows(u, uh, 2, True) + dw_ref[pl.ds(1, 1), :] * _shift_rows(u, uh, 1, True)
            + dw_ref[pl.ds(2, 1), :] * u + bias_ref[...])


def ffn_fwd(h, wup, dww, dwb, wdn, x, gate, pg, nxt, name):
    T, D = h.shape
    fs = wup.shape[2]
    NJ = wup.shape[0] // 2
    tm = _rt(T, 512)
    n_nxt = 0 if nxt is None else 3

    def body(h_ref, hp_ref, wa_ref, wb_ref, dwa_ref, dwb_ref, ba_ref, bb_ref, wd_ref, x_ref, gt_ref, pg_ref, *rest):
        nx, (u_ref, uc_ref, y_ref, xo_ref), more = rest[:n_nxt], rest[n_nxt:n_nxt + 4], rest[n_nxt + 4:]
        acc = more[-1]
        i, j = pl.program_id(0), pl.program_id(1)

        @pl.when(j == 0)
        def _():
            acc[...] = jnp.zeros_like(acc)
        hb = h_ref[...]
        hp = hp_ref[...]

        def half(s, w_ref, dw_ref, b_ref):
            u = _nn(hb, w_ref[...])
            uh = jnp.where(i == 0, 0.0, _nn(hp, w_ref[...]))
            u_ref[s] = u
            uc = _conv3(u, uh, dw_ref, b_ref)
            uc_ref[s] = uc
            return uc
        a = half(0, wa_ref, dwa_ref, ba_ref)
        b = half(1, wb_ref, dwb_ref, bb_ref)
        acc[...] += _nn(_bf(a * _sig(a) * b), wd_ref[...])

        @pl.when(j == NJ - 1)
        def _():
            y = acc[...]
            y_ref[...] = y
            r = lax.rsqrt(jnp.mean(y * y, axis=-1, keepdims=True) + EPS)
            xo = x_ref[...] + gt_ref[...] * ((y * r) * pg_ref[...])
            xo_ref[...] = xo
            if nxt is not None:
                more[0][...] = _next_input(xo, *nx)

    tile = pl.BlockSpec((tm, D), lambda i, j: (i, 0))
    shard = lambda off, r: pl.BlockSpec((None, r, fs), lambda i, j: (j + off, 0, 0))
    ush = pl.BlockSpec((2, None, tm, fs), lambda i, j: (0, j, i, 0))
    vec = pl.BlockSpec((1, D), lambda i, j: (0, 0))
    return pl.pallas_call(
        body, name=name, grid=(T // tm, NJ),
        in_specs=[tile, pl.BlockSpec((FFN_HALO, D), lambda i, j: (jnp.maximum(i * (tm // FFN_HALO) - 1, 0), 0)),
                  shard(0, D), shard(NJ, D), shard(0, 3), shard(NJ, 3), shard(0, 1), shard(NJ, 1),
                  pl.BlockSpec((None, fs, D), lambda i, j: (j, 0, 0)), tile, vec, vec] + [vec] * n_nxt,
        out_specs=[ush, ush, tile, tile] + [tile] * (n_nxt > 0),
        out_shape=[jax.ShapeDtypeStruct((2, NJ, T, fs), F32), jax.ShapeDtypeStruct((2, NJ, T, fs), F32),
                   jax.ShapeDtypeStruct((T, D), F32), jax.ShapeDtypeStruct((T, D), F32)]
        + [jax.ShapeDtypeStruct((T, D), BF16)] * (n_nxt > 0),
        scratch_shapes=[pltpu.VMEM((tm, D), F32)],
        compiler_params=_cp("parallel", "arbitrary"))(h, h, wup, wup, dww, dww, dwb, dwb, wdn, x, gate, pg,
                                                      *(nxt or ()))


def ffn_bwd_a(dy, uc, wdn, name):
    _, NJ, T, fs = uc.shape
    D = dy.shape[1]
    tm = _rt(T, 512)

    NT = T // tm

    def body(dy_ref, uc_ref, wd_ref, duc_ref, dwd_ref, acc):
        i = pl.program_id(1)

        @pl.when(i == 0)
        def _():
            acc[...] = jnp.zeros_like(acc)
        dyb = _bf(dy_ref[...])
        dz = _nt(dyb, wd_ref[...])
        a = uc_ref[0]
        b = uc_ref[1]
        sa = _sig(a)
        asa = a * sa
        acc[...] += _tn(_bf(asa * b), dyb)
        duc_ref[0] = dz * b * (sa + asa * (1.0 - sa))
        duc_ref[1] = dz * asa

        @pl.when(i == NT - 1)
        def _():
            dwd_ref[...] = _bf(acc[...])

    ush = pl.BlockSpec((2, None, tm, fs), lambda j, i: (0, j, i, 0))
    return pl.pallas_call(
        body, name=name, grid=(NJ, NT),
        in_specs=[pl.BlockSpec((tm, D), lambda j, i: (i, 0)), ush, pl.BlockSpec((None, fs, D), lambda j, i: (j, 0, 0))],
        out_specs=[ush, pl.BlockSpec((fs, D), lambda j, i: (j, 0))],
        out_shape=[jax.ShapeDtypeStruct((2, NJ, T, fs), F32), jax.ShapeDtypeStruct((NJ * fs, D), BF16)],
        scratch_shapes=[pltpu.VMEM((fs, D), F32)],
        compiler_params=_cp("parallel", "arbitrary"))(dy, uc, wdn)


def ffn_bwd_b(duc, u, h, dww, wup, name, ex=None):
    NS, T, fs = duc.shape
    D = wup.shape[1]
    tm = _rt(T, 512)
    NT = T // tm

    def body(d_ref, dn_ref, u_ref, h_ref, dw_ref, w_ref, dh_ref, g_ref, dwup_ref, acc, accw, stage):
        i, s = pl.program_id(0), pl.program_id(1)

        @pl.when((i == 0) & (s == 0))
        def _():
            g_ref[...] = jnp.zeros_like(g_ref)

        @pl.when(s == 0)
        def _():
            acc[...] = jnp.zeros_like(acc)
        d = d_ref[...]
        dn = jnp.where(i == NT - 1, 0.0, dn_ref[...])
        d1 = _shift_rows(d, dn, 1, False)
        d2 = _shift_rows(d, dn, 2, False)
        du = _bf(dw_ref[pl.ds(2, 1), :] * d + dw_ref[pl.ds(1, 1), :] * d1 + dw_ref[pl.ds(0, 1), :] * d2)
        acc[...] += _nt(du, w_ref[...])
        dw_part = _tn(h_ref[...], du)

        @pl.when(i == 0)
        def _():
            accw[s] = dw_part

        @pl.when(i > 0)
        def _():
            accw[s] += dw_part

        @pl.when(i == NT - 1)
        def _():
            stage[...] = _bf(accw[s])
            pltpu.sync_copy(stage, dwup_ref.at[s])
        u = u_ref[...]
        g_ref[s, pl.ds(0, 1), :] += jnp.sum(d2 * u, axis=0, keepdims=True)
        g_ref[s, pl.ds(1, 1), :] += jnp.sum(d1 * u, axis=0, keepdims=True)
        g_ref[s, pl.ds(2, 1), :] += jnp.sum(d * u, axis=0, keepdims=True)
        g_ref[s, pl.ds(3, 1), :] += jnp.sum(d, axis=0, keepdims=True)

        @pl.when(s == NS - 1)
        def _():
            dh_ref[...] = acc[...]

    ush = pl.BlockSpec((None, tm, fs), lambda i, s: (s, i, 0))
    unext = pl.BlockSpec((None, FFN_HALO, fs),
                         lambda i, s: (s, jnp.minimum((i + 1) * (tm // FFN_HALO), T // FFN_HALO - 1), 0))
    tile = pl.BlockSpec((tm, D), lambda i, s: (i, 0))
    own, rider = _call(
        body, name, (NT, NS),
        [ush, unext, ush, tile, pl.BlockSpec((None, 3, fs), lambda i, s: (s, 0, 0)),
         pl.BlockSpec((None, D, fs), lambda i, s: (s, 0, 0))],
        [tile, pl.BlockSpec((NS, 8, fs), lambda i, s: (0, 0, 0)), ANY],
        [jax.ShapeDtypeStruct((T, D), F32), jax.ShapeDtypeStruct((NS, 8, fs), F32),
         jax.ShapeDtypeStruct((NS, D, fs), BF16)],
        [pltpu.VMEM((tm, D), F32), pltpu.VMEM((NS, D, fs), F32), pltpu.VMEM((D, fs), BF16)],
        ("arbitrary", "arbitrary"), (duc, duc, u, h, dww, wup), ex, FFN_BWD_VMEM)
    return (*own, rider) if ex is not None else own


def ada_fwd(c_all, w, bias, name):
    L, D, n = w.shape

    def body(c_ref, w_ref, b_ref, o_ref):
        cv = c_ref[...]
        o_ref[...] = _nn(_bf(cv * _sig(cv)), _bf(w_ref[...])) + b_ref[...]

    return pl.pallas_call(
        body, name=name, grid=(L,),
        in_specs=[pl.BlockSpec((N_DEV, D), lambda l: (0, 0)), pl.BlockSpec((None, D, n), lambda l: (l, 0, 0)),
                  pl.BlockSpec((None, 1, n), lambda l: (l, 0, 0))],
        out_specs=pl.BlockSpec((None, N_DEV, n), lambda l: (l, 0, 0)),
        out_shape=jax.ShapeDtypeStruct((L, N_DEV, n), F32), compiler_params=_cp("parallel"))(c_all, w, bias)


def ada_bwd(c_all, dm, name):
    L, _, n = dm.shape
    D = c_all.shape[1]

    def body(c_ref, d_ref, o_ref):
        cv = c_ref[...]
        o_ref[...] = _tn(_bf(cv * _sig(cv)), _bf(d_ref[...]))

    return pl.pallas_call(
        body, name=name, grid=(L,),
        in_specs=[pl.BlockSpec((N_DEV, D), lambda l: (0, 0)), pl.BlockSpec((None, N_DEV, n), lambda l: (l, 0, 0))],
        out_specs=pl.BlockSpec((None, D, n), lambda l: (l, 0, 0)),
        out_shape=jax.ShapeDtypeStruct((L, D, n), F32), compiler_params=_cp("parallel"))(c_all, dm)


def sum_slots(g, name):
    S, R, C = g.shape

    def body(g_ref, o_ref):
        acc = g_ref[0]
        for s in range(1, S):
            acc = acc + g_ref[s]
        o_ref[...] = acc

    return pl.pallas_call(
        body, name=name, grid=(1,), in_specs=[pl.BlockSpec((S, R, C), lambda i: (0, 0, 0))],
        out_specs=pl.BlockSpec((R, C), lambda i: (0, 0)), out_shape=jax.ShapeDtypeStruct((R, C), F32),
        compiler_params=_cp("arbitrary"))(g)


def lb_softmax(logits, name):
    def body(l_ref, s_ref):
        l = l_ref[...]
        e = jnp.exp(l - jnp.max(l, axis=0, keepdims=True))
        s_ref[...] = e / jnp.sum(e, axis=0, keepdims=True)
    return pl.pallas_call(body, name=name, out_shape=jax.ShapeDtypeStruct(logits.shape, F32))(logits)


def adamw(w, gparts, m, v, name):
    P = len(gparts)
    S, Rp, C = gparts[0].shape
    R = P * Rp
    tr = Rp
    budget = (4 * 1024 * 1024) // (4 * (P * S + 7) * C)
    if Rp > budget:
        tr = max(t for t in range(16, budget + 1, 16) if Rp % t == 0)
    per = Rp // tr

    def body(w_ref, *rest):
        g_refs, (m_ref, v_ref, go_ref, d_ref, mo_ref, vo_ref) = rest[:P], rest[P:]
        part = pl.program_id(0) // per
        g = None
        for p, g_ref in enumerate(g_refs):
            gp = g_ref[0].astype(F32)
            for s in range(1, S):
                gp = gp + g_ref[s].astype(F32)
            g = gp if g is None else jnp.where(part == p, gp, g)
        go_ref[...] = g
        mn = ADAM_B1 * m_ref[...] + (1.0 - ADAM_B1) * g
        vn = ADAM_B2 * v_ref[...] + (1.0 - ADAM_B2) * (g * g)
        mo_ref[...] = mn
        vo_ref[...] = vn
        m_hat = mn / (1.0 - ADAM_B1 ** ADAM_STEP)
        v_hat = vn / (1.0 - ADAM_B2 ** ADAM_STEP)
        d_ref[...] = -ADAM_LR * (m_hat / (jnp.sqrt(v_hat) + ADAM_EPS) + ADAM_WD * w_ref[...])

    tile = pl.BlockSpec((tr, C), lambda i: (i, 0))
    slots = [pl.BlockSpec((S, tr, C), lambda i, p=p: (0, jnp.clip(i - p * per, 0, per - 1), 0)) for p in range(P)]
    out = jax.ShapeDtypeStruct((R, C), F32)
    return pl.pallas_call(
        body, name=name, grid=(R // tr,), in_specs=[tile] + slots + [tile, tile],
        out_specs=[tile] * 4, out_shape=[out] * 4, compiler_params=_cp("parallel"))(w, *gparts, m, v)


def _place():
    return lax.axis_index("x"), lax.axis_index("y"), lax.axis_index("c")


def _slot(p):
    return 4 * p[0] + 2 * p[1] + p[2]


class _Gather:
    def __init__(self, xs):
        self.ins = list(xs)
        n = self.n = len(xs)
        self.out_shapes = [jax.ShapeDtypeStruct((N_DEV,) + a.shape, a.dtype) for a in xs]
        self.scratch = [pltpu.SemaphoreType.DMA((n, 7)), pltpu.SemaphoreType.DMA((n, 7)), pltpu.SemaphoreType.DMA((n,))]

    def _parts(self, ins, outs, sems):
        send_sems, recv_sems, local_sems = sems
        x, y, c = _place()
        me, sib = (x, y, c), (x, y, 1 - c)
        chips = [(1 - x, y), (x, 1 - y), (1 - x, 1 - y)]

        def copy(a, k, block, to, src=None):
            dst = outs[a].at[_slot(block)]
            return pltpu.make_async_remote_copy(
                src_ref=dst if src is None else src, dst_ref=dst, send_sem=send_sems.at[a, k],
                recv_sem=recv_sems.at[a, k], device_id=to, device_id_type=MESH)

        mine = [pltpu.make_async_copy(ins[a], outs[a].at[_slot(me)], local_sems.at[a]) for a in range(self.n)]
        first = []
        for a in range(self.n):
            first.append(copy(a, 0, me, sib, src=ins[a]))
            first += [copy(a, 1 + j, me, (*chip, c), src=ins[a]) for j, chip in enumerate(chips)]
        return copy, mine, first, me, sib, chips, c

    def start(self, ins, outs, sems):
        _, mine, first, *_ = self._parts(ins, outs, sems)
        for cp in mine + first:
            cp.start()

    def finish(self, ins, outs, sems):
        copy, mine, first, me, sib, chips, c = self._parts(ins, outs, sems)
        passed = []
        for j, chip in enumerate(chips):
            for a in range(self.n):
                copy(a, 1 + j, (*chip, c), me).wait_recv()
                fwd = copy(a, 4 + j, (*chip, c), sib)
                fwd.start()
                passed.append(fwd)
        for a in range(self.n):
            copy(a, 0, sib, me).wait_recv()
            for j, chip in enumerate(chips):
                copy(a, 4 + j, (*chip, 1 - c), me).wait_recv()
        for cp in first + passed:
            cp.wait_send()
        for cp in mine:
            cp.wait()


def _run_alone(ex, name):
    def body(*refs):
        ni, no = len(ex.ins), len(ex.out_shapes)
        parts = refs[:ni], refs[ni:ni + no], refs[ni + no:]
        ex.start(*parts)
        ex.finish(*parts)
    return pl.pallas_call(body, name=name, in_specs=[ANY] * len(ex.ins), out_specs=[ANY] * len(ex.out_shapes),
                          out_shape=ex.out_shapes, scratch_shapes=ex.scratch)(*ex.ins)


def _ride(body, n_in, n_out, ex, first, last):
    ni, no = len(ex.ins), len(ex.out_shapes)

    def wrapped(*refs):
        a, r_in = refs[:n_in], refs[n_in:n_in + ni]
        b, r_out = refs[n_in + ni:n_in + ni + n_out], refs[n_in + ni + n_out:n_in + ni + n_out + no]
        rest = refs[n_in + ni + n_out + no:]
        s, r_s = rest[:len(rest) - len(ex.scratch)], rest[len(rest) - len(ex.scratch):]

        @pl.when(first())
        def _():
            ex.start(r_in, r_out, r_s)
        body(*a, *b, *s)

        @pl.when(last())
        def _():
            ex.finish(r_in, r_out, r_s)
    return wrapped


def all_gather(xs, name):
    return _run_alone(_Gather(xs), name)


class _Exchange:
    def __init__(self, groups):
        self.ins = [a for grp in groups for a in grp]
        self.where = [(g, i) for g, grp in enumerate(groups) for i in range(len(grp))]
        n = self.n = len(self.ins)
        self.out_shapes = [jax.ShapeDtypeStruct((N_DEV, len(grp)) + grp[0].shape[1:], grp[0].dtype) for grp in groups]
        self.scratch = [pltpu.SemaphoreType.DMA((n, 7)), pltpu.SemaphoreType.DMA((n, 7)), pltpu.SemaphoreType.DMA((n,))]

    def _parts(self, ins, outs, sems):
        send_sems, recv_sems, local_sems = sems
        x, y, c = _place()
        me = (x, y, c)

        def peer(r):
            return (1 - x if r & 4 else x, 1 - y if r & 2 else y, 1 - c if r & 1 else c)

        def land(a, src):
            g, i = self.where[a]
            return outs[g].at[_slot(src), i]

        def copy(a, r, src_dev):
            p = peer(r)
            return pltpu.make_async_remote_copy(
                src_ref=ins[a].at[_slot(p)], dst_ref=land(a, src_dev), send_sem=send_sems.at[a, r - 1],
                recv_sem=recv_sems.at[a, r - 1], device_id=p, device_id_type=MESH)

        mine = [pltpu.make_async_copy(ins[a].at[_slot(me)], land(a, me), local_sems.at[a]) for a in range(self.n)]
        sends = [copy(a, r, me) for r in range(1, N_DEV) for a in range(self.n)]
        arrivals = [copy(a, r, peer(r)) for r in range(1, N_DEV) for a in range(self.n)]
        return mine, sends, arrivals

    def start(self, ins, outs, sems):
        mine, sends, _ = self._parts(ins, outs, sems)
        for cp in mine + sends:
            cp.start()

    def finish(self, ins, outs, sems):
        mine, sends, arrivals = self._parts(ins, outs, sems)
        for cp in arrivals:
            cp.wait_recv()
        for cp in sends:
            cp.wait_send()
        for cp in mine:
            cp.wait()


def all_to_all(groups, name):
    return _run_alone(_Exchange(groups), name)


def _mods(mod, l, D):
    return [mod[l:l + 1, k * D:(k + 1) * D] for k in range(6)]


def _ffn_backward(l, dxo, dy, hb, u, uc, xin, sc2, gains, W, post, ex=None):
    _, NJ, T, fs = u.shape
    duc, d_wdn = ffn_bwd_a(dy, uc, W["wdn"][l], f"ffn{l}_bwd_a")
    dh, taps, d_wup, *rider = ffn_bwd_b(duc.reshape(2 * NJ, T, fs), u.reshape(2 * NJ, T, fs), hb, W["fdw_w"][l],
                                        W["wup"][l], f"ffn{l}_bwd_b", ex)
    dx, da, dsh, *below = resid_bwd(f"ffn{l}_resid_bwd", pre=(dh, xin, dxo, gains["pre_ffn_g"][l:l + 1], sc2), post=post)
    small = dict(da=da, dsh=dsh, dwb=taps[:, 3], dww=taps[:, 0:3])
    return dx, d_wup, d_wdn, small, below, (rider[0] if rider else None)


def _local_step(xt, tgt, mod, gains, W, mine):
    T, D = xt.shape
    zero_d = jnp.zeros((1, D), F32)
    half = lambda g: g.reshape(N_DEV // 2, 2 * g.shape[1], D)
    whole = lambda g: jnp.transpose(g, (1, 0, 2)).reshape(g.shape[1], -1)
    sh1, sc1, g1, sh2, sc2, g2 = m0 = _mods(mod, 0, D)
    h0, (whin,) = prenorm_mod(xt, gains["pre_mix_g"][0:1], sc1, sh1, "l0_prenorm", _Gather([mine["whin"]]))
    W = dict(W, whin=whin)
    proj, (whout, wup0, wdn0) = mm_nn(h0, W["whin"], jnp.zeros((1, 4 * D), F32), "hgrn_proj",
                                      _Gather([mine["whout"], mine["wup"][0], mine["wdn"][0]]))
    o, og, st, (wcin, wcout, wup1, wdn1) = hgrn_fwd(
        proj, W["lb"], W["gg"], "hgrn_fwd", _Gather([mine["wcin"], mine["wcout"], mine["wup"][1], mine["wdn"][1]]))
    W = dict(W, whout=whout.reshape(D, D), wcin=wcin, wcout=wcout.reshape(D, D), wup=[wup0, wup1],
             wdn=[half(wdn0), half(wdn1)])
    t1, c1, e1, t2, c2, e2 = m1 = _mods(mod, 1, D)
    y0, x1, h0f = mm_post(og, W["whout"], zero_d, xt, g1, gains["post_mix_g"][0:1],
                          (gains["pre_ffn_g"][0:1], sc2, sh2), "hgrn_out")
    u0, uc0, yf0, x2, h1 = ffn_fwd(h0f, W["wup"][0], W["fdw_w"][0], W["fdw_b"][0], W["wdn"][0], x1, g2,
                                   gains["post_ffn_g"][0:1], (gains["pre_mix_g"][1:2], c1, t1), "ffn0_fwd")
    uc = mm_nn(h1, W["wcin"], W["cb_in"], "conv_in")
    cv, va = conv_mid_fwd(uc, W["cw"], W["cdw_b"], W["ln_g"], W["ln_b"], "conv_mid_fwd")
    y1, x3, h1f = mm_post(va, W["wcout"], W["cb_out"], x2, e1, gains["post_mix_g"][1:2],
                          (gains["pre_ffn_g"][1:2], c2, t2), "conv_out")
    u1, uc1, yf1, x4 = ffn_fwd(h1f, W["wup"][1], W["fdw_w"][1], W["fdw_b"][1], W["wdn"][1], x3, e2,
                               gains["post_ffn_g"][1:2], None, "ffn1_fwd")
    dx4, loss_part, dyf1, dgate_f1, dpost_f1, _ = resid_bwd(
        "loss_resid_bwd", loss=(x4, tgt), post=(yf1, e2, gains["post_ffn_g"][1:2]))
    dx3, d_wup1, d_wdn1, sf1, (dy1, dgate_c, dpost_c, dys_c), _ = _ffn_backward(
        1, dx4, dyf1, h1f, u1, uc1, x3, c2, gains, W, (y1, e1, gains["post_mix_g"][1:2]))
    dva = mm_nt(dy1, W["wcout"], "conv_dva")
    d_wcout = mm_tn(va, dy1, "conv_dwout", True, True, D, D)
    dcv, dlg, dlbias, taps = conv_bwd_a(dva, cv, uc, W["ln_g"], W["ln_b"], "conv_bwd_a")
    duc, dbin = conv_bwd_b(dcv, uc, W["cw"], "conv_bwd_b")
    dh1 = mm_nt(duc, whole(W["wcin"]), "conv_dh")
    d_wcin = mm_tn(h1, duc, "conv_dwin", True, True, D, W["wcin"].shape[2])
    dx2, da_c, dsh_c, dyf0, dgate_f0, dpost_f0, _ = resid_bwd(
        "l1_resid_bwd", pre=(dh1, x2, dx3, gains["pre_mix_g"][1:2], c1), post=(yf0, g2, gains["post_ffn_g"][0:1]))
    rows = lambda g: g.reshape(N_DEV, -1, D)
    dx1, d_wup0, d_wdn0, sf0, (dy0, dgate_h, dpost_h, _), (r_wcin, r_wcout, r_wup1, r_wdn1) = _ffn_backward(
        0, dx2, dyf0, h0f, u0, uc0, x1, sc2, gains, W, (y0, g1, gains["post_mix_g"][0:1]),
        _Exchange([[d_wcin], [rows(d_wcout)], [d_wup1], [rows(d_wdn1)]]))
    dog = mm_nt(dy0, W["whout"], "hgrn_dog")
    d_whout = mm_tn(og, dy0, "hgrn_dwout", True, True, D, D)
    dproj, dlb, dgg, (r_wup0, r_wdn0, r_whout) = hgrn_bwd(
        proj, o, dog, st, W["lb"], W["gg"], "hgrn_bwd", _Exchange([[d_wup0], [rows(d_wdn0)], [rows(d_whout)]]))
    d_whin = mm_tn(h0, dproj, "hgrn_dwin", True, True, D, W["whin"].shape[2])
    dh0, (r_whin,) = mm_nt(dproj, whole(W["whin"]), "hgrn_dh", _Exchange([[d_whin]]))
    dx0, da_h, dsh_h = resid_bwd("l0_resid_bwd", pre=(dh0, xt, dx1, gains["pre_mix_g"][0:1], sc1))
    sf0.update(dgate=dgate_f0, dpost=dpost_f0)
    sf1.update(dgate=dgate_f1, dpost=dpost_f1)

    big = dict(hgrn_w_in=r_whin, hgrn_w_out=r_whout, conv_w_in=r_wcin, conv_w_out=r_wcout,
               ffn_w_up=[r_wup0, r_wup1], ffn_w_down=[r_wdn0, r_wdn1])
    pm, pf = gains["pre_mix_g"], gains["pre_ffn_g"]
    dmod = jnp.concatenate([
        dsh_h, da_h * pm[0:1], dgate_h, sf0["dsh"], sf0["da"] * pf[0:1], sf0["dgate"],
        dsh_c, da_c * pm[1:2], dgate_c, sf1["dsh"], sf1["da"] * pf[1:2], sf1["dgate"]], axis=1)
    NCH = D // LANES
    tap = taps.reshape(NCH, 32, 8, LANES).sum(axis=2)
    small = dict(
        dmod=dmod,
        pre_mix_g=jnp.concatenate([da_h * (1.0 + sc1), da_c * (1.0 + c1)], axis=0),
        post_mix_g=jnp.concatenate([dpost_h, dpost_c], axis=0),
        pre_ffn_g=jnp.concatenate([sf0["da"] * (1.0 + sc2), sf1["da"] * (1.0 + c2)], axis=0),
        post_ffn_g=jnp.concatenate([sf0["dpost"], sf1["dpost"]], axis=0),
        lb=dlb, gg=dgg,
        ffn_dw_b=jnp.stack([sf0["dwb"], sf1["dwb"]]),
        conv_b_in=dbin,
        conv_dw_w=jnp.transpose(tap[:, :31], (1, 0, 2)).reshape(31, D),
        conv_dw_b=tap[:, 31].reshape(1, D),
        conv_ln_g=dlg, conv_ln_b=dlbias, conv_b_out=dys_c,
        ffn_dw_w=jnp.stack([sf0["dww"], sf1["dww"]]))
    return loss_part, dx0, big, small


SMALL_ORDER = ["dmod", "pre_mix_g", "post_mix_g", "pre_ffn_g", "post_ffn_g", "lb", "gg", "ffn_dw_b", "conv_b_in",
               "conv_dw_w", "conv_dw_b", "conv_ln_g", "conv_ln_b", "conv_b_out", "ffn_dw_w"]


def _pack(parts):
    flat = jnp.concatenate([p.reshape(-1) for p in parts])
    pad = (-flat.shape[0]) % LANES
    if pad:
        flat = jnp.concatenate([flat, jnp.zeros((pad,), F32)])
    return flat.reshape(-1, LANES)


def _unpack(flat, shapes):
    out, off = [], 0
    for s in shapes:
        n = 1
        for d in s:
            n *= d
        out.append(flat[off:off + n].reshape(s))
        off += n
    return out


def _apply_adamw(name, w, g_slots, m, v):
    shp = w.shape
    C = shp[-1]
    R = w.size // C
    parts = g_slots if isinstance(g_slots, list) else [g_slots]
    parts = [p.reshape(p.shape[0], R // len(parts), C) for p in parts]
    g, d, mn, vn = adamw(w.reshape(R, C), parts, m.reshape(R, C), v.reshape(R, C), "adamw_" + name)
    return g.reshape(shp), d.reshape(shp), mn.reshape(shp), vn.reshape(shp)


WEIGHTS = ['ada_w', 'ada_b', 'pre_mix_g', 'post_mix_g', 'pre_ffn_g', 'post_ffn_g', 'hgrn_w_in', 'hgrn_lb_logits',
           'hgrn_gnorm_g', 'hgrn_w_out', 'conv_w_in', 'conv_b_in', 'conv_dw_w', 'conv_dw_b', 'conv_ln_g', 'conv_ln_b',
           'conv_w_out', 'conv_b_out', 'ffn_w_up', 'ffn_dw_w', 'ffn_dw_b', 'ffn_w_down']


def kernel(x, c, ada_w, ada_b, pre_mix_g, post_mix_g, pre_ffn_g, post_ffn_g, hgrn_w_in, hgrn_lb_logits, hgrn_gnorm_g, hgrn_w_out, conv_w_in, conv_b_in, conv_dw_w, conv_dw_b, conv_ln_g, conv_ln_b, conv_w_out, conv_b_out, ffn_w_up, ffn_dw_w, ffn_dw_b, ffn_w_down, loss_target, m_ada_w, m_ada_b, m_pre_mix_g, m_post_mix_g, m_pre_ffn_g, m_post_ffn_g, m_hgrn_w_in, m_hgrn_lb_logits, m_hgrn_gnorm_g, m_hgrn_w_out, m_conv_w_in, m_conv_b_in, m_conv_dw_w, m_conv_dw_b, m_conv_ln_g, m_conv_ln_b, m_conv_w_out, m_conv_b_out, m_ffn_w_up, m_ffn_dw_w, m_ffn_dw_b, m_ffn_w_down, v_ada_w, v_ada_b, v_pre_mix_g, v_post_mix_g, v_pre_ffn_g, v_post_ffn_g, v_hgrn_w_in, v_hgrn_lb_logits, v_hgrn_gnorm_g, v_hgrn_w_out, v_conv_w_in, v_conv_b_in, v_conv_dw_w, v_conv_dw_b, v_conv_ln_g, v_conv_ln_b, v_conv_w_out, v_conv_b_out, v_ffn_w_up, v_ffn_dw_w, v_ffn_dw_b, v_ffn_w_down):
    P = dict(locals())
    _, T, D = x.shape
    me = _slot(_place())
    L = ada_w.shape[0]
    na = ada_w.shape[2]
    fs = ffn_w_up.shape[2]
    nci = conv_w_in.shape[2]
    nd = conv_dw_b.shape[1]
    NCH = D // LANES

    small_in = [c, conv_b_in, conv_dw_w, conv_dw_b, conv_ln_g, conv_ln_b, conv_b_out, ffn_dw_w]
    (packed,) = all_gather([_pack(small_in)], "gather_first")
    mine = dict(whin=_bf(hgrn_w_in[0]), whout=_bf(hgrn_w_out[0]), wcin=_bf(conv_w_in[0]), wcout=_bf(conv_w_out[0]),
                wup=[_bf(ffn_w_up[l]) for l in range(L)], wdn=[_bf(ffn_w_down[l]) for l in range(L)])
    sm = packed.reshape(N_DEV, -1)
    offs = [0]
    for a in small_in:
        offs.append(offs[-1] + a.size)
    piece = lambda k, shp: sm[:, offs[k]:offs[k + 1]].reshape((N_DEV,) + shp)
    c_all = piece(0, (D,))
    dw_nat = jnp.transpose(piece(2, (31, nd)), (1, 0, 2)).reshape(31, D)
    cw = jnp.pad(jnp.transpose(dw_nat.reshape(31, NCH, LANES), (1, 0, 2)), ((0, 0), (0, 1), (0, 0)))
    fdw = piece(7, (L, 3, fs))
    lb_soft = lb_softmax(hgrn_lb_logits, "lb_softmax")
    W = dict(
        lb=lb_soft[0:1], gg=hgrn_gnorm_g, cb_in=piece(1, (nci,)).reshape(1, N_DEV * nci), cw=cw,
        cdw_b=piece(3, (nd,)).reshape(1, D), ln_g=piece(4, (nd,)).reshape(1, D), ln_b=piece(5, (nd,)).reshape(1, D),
        cb_out=piece(6, (nd,)).reshape(1, D), fdw_w=[fdw[:, l] for l in range(L)],
        fdw_b=[ffn_dw_b[l].reshape(N_DEV, 1, fs) for l in range(L)])

    bias_mine = lax.dynamic_slice(ada_b, (0, me * na), (L, na)).reshape(L, 1, na)
    mod_cols = ada_fwd(c_all, ada_w, bias_mine, "ada_fwd")
    (mod_all,) = all_gather([mod_cols], "gather_mod")
    mod = jnp.transpose(lax.dynamic_index_in_dim(mod_all, me, axis=2, keepdims=False), (1, 0, 2)).reshape(L, N_DEV * na)

    gains = dict(pre_mix_g=pre_mix_g, post_mix_g=post_mix_g, pre_ffn_g=pre_ffn_g, post_ffn_g=post_ffn_g)
    loss_part, dx0, big, small = _local_step(x[0], loss_target[0], mod, gains, W, mine)
    loss = lax.psum((0.5 / D) * jnp.sum(loss_part), ("x", "y", "c"))

    parts = [small[k] for k in SMALL_ORDER]
    (sm_all,) = all_gather([_pack(parts)], "gather_small_grads")
    tot = _unpack(sum_slots(sm_all, "sum_small_grads").reshape(-1), [p.shape for p in parts])
    tot = dict(zip(SMALL_ORDER, tot))
    dmod_all = sm_all.reshape(N_DEV, -1)[:, :L * 6 * D].reshape(N_DEV, L, 6 * D)
    dm_mine = jnp.transpose(lax.dynamic_slice(dmod_all, (0, 0, me * na), (N_DEV, L, na)), (1, 0, 2))
    s0 = lb_soft[0:1]
    onehot = (lax.broadcasted_iota(jnp.int32, lb_soft.shape, 0) == 0).astype(F32)
    col = lambda a, n: lax.dynamic_slice_in_dim(a, me * n, n, axis=a.ndim - 1)
    G = {
        'ada_w': ada_bwd(c_all, dm_mine, "ada_bwd")[None],
        'ada_b': tot["dmod"].reshape(1, L, 6 * D),
        'pre_mix_g': tot["pre_mix_g"][None], 'post_mix_g': tot["post_mix_g"][None],
        'pre_ffn_g': tot["pre_ffn_g"][None], 'post_ffn_g': tot["post_ffn_g"][None],
        'hgrn_lb_logits': (tot["lb"] * s0 * (onehot - lb_soft))[None],
        'hgrn_gnorm_g': tot["gg"][None],
        'ffn_dw_b': tot["ffn_dw_b"].reshape(1, L, N_DEV * fs),
        'conv_b_in': col(tot["conv_b_in"], nci)[None],
        'conv_dw_w': col(tot["conv_dw_w"], nd)[None, None],
        'conv_dw_b': col(tot["conv_dw_b"], nd)[None], 'conv_ln_g': col(tot["conv_ln_g"], nd)[None],
        'conv_ln_b': col(tot["conv_ln_b"], nd)[None], 'conv_b_out': col(tot["conv_b_out"], nd)[None],
        'ffn_dw_w': lax.dynamic_index_in_dim(tot["ffn_dw_w"], me, axis=1, keepdims=False)[None],
    }

    G.update(big)
    res = {n: _apply_adamw(n, P[n], G[n], P["m_" + n], P["v_" + n]) for n in WEIGHTS}
    return (loss, dx0[None], *[res[n][0] for n in WEIGHTS], *[res[n][1] for n in WEIGHTS],
            *[res[n][2] for n in WEIGHTS], *[res[n][3] for n in WEIGHTS])
```

```python
import functools

import jax
import jax.numpy as jnp
from jax import lax
from jax.experimental import pallas as pl
from jax.experimental.pallas import tpu as pltpu

F32 = jnp.float32
BF16 = jnp.bfloat16
EPS = 1e-6
LANES = 128
N_DEV = 8
VMEM_LIMIT = 48 * 1024 * 1024
FFN_BWD_VMEM = 58 * 1024 * 1024
MM_ROWS = 1024
HG_CHUNK = 128
HG_SUB = 32
EXP_CLAMP = 80.0
CONV_HALO = 32
FFN_HALO = 8
CONV_ROWS = 32
ADAM_LR, ADAM_B1, ADAM_B2, ADAM_EPS, ADAM_WD, ADAM_STEP = 0.001, 0.9, 0.999, 1e-08, 0.01, 10
MESH = pl.DeviceIdType.MESH
ANY = pl.BlockSpec(memory_space=pl.ANY)


def _cp(*sem):
    return pltpu.CompilerParams(dimension_semantics=sem, vmem_limit_bytes=VMEM_LIMIT)


def _rt(n, t):
    t = min(n, t)
    assert n % t == 0, (n, t)
    return t


def _sig(x):
    return jax.nn.sigmoid(x)


def _nt(a, b):
    return lax.dot_general(a, b, (((1,), (1,)), ((), ())), preferred_element_type=F32)


def _tn(a, b):
    return lax.dot_general(a, b, (((0,), (0,)), ((), ())), preferred_element_type=F32)


def _nn(a, b):
    return jnp.dot(a, b, preferred_element_type=F32)


def _bf(x):
    return x.astype(BF16)


def _nn_st(a, b):
    return _nn(_bf(a), _bf(b))


def _tn_st(a, b):
    return _tn(_bf(a), _bf(b))


def _row(d):
    return pl.BlockSpec((1, d), lambda *_: (0, 0))


def prenorm_mod(x, g, sc, sh, name, ex=None):
    T, D = x.shape
    tm = _rt(T, 512)

    def body(x_ref, g_ref, sc_ref, sh_ref, h_ref):
        h_ref[...] = _next_input(x_ref[...], g_ref, sc_ref, sh_ref)

    tile = pl.BlockSpec((tm, D), lambda i: (i, 0))
    (h,), rider = _call(body, name, (T // tm,), [tile, _row(D), _row(D), _row(D)], [tile],
                        [jax.ShapeDtypeStruct((T, D), BF16)], [], ("parallel",), (x, g, sc, sh), ex)
    return (h, rider) if ex is not None else h


def resid_bwd(name, pre=None, loss=None, post=None):
    T, D = (pre[1] if pre is not None else loss[0]).shape
    tm = _rt(T, 512)
    n_src = 5 if pre is not None else 2
    n_in = n_src + (3 if post is not None else 0)

    def body(*refs):
        ins, outs = refs[:n_in], refs[n_in:]
        first = pl.program_id(0) == 0
        if pre is not None:
            dh_ref, x_ref, dr_ref, g_ref, sc_ref = ins[:5]
            dx_ref, da_ref, dsh_ref = outs[:3]

            @pl.when(first)
            def _():
                da_ref[...] = jnp.zeros_like(da_ref)
                dsh_ref[...] = jnp.zeros_like(dsh_ref)
            xv = x_ref[...]
            dh = dh_ref[...]
            r = lax.rsqrt(jnp.mean(xv * xv, axis=-1, keepdims=True) + EPS)
            n = xv * r
            da_ref[...] += jnp.sum(dh * n, axis=0, keepdims=True)
            dsh_ref[...] += jnp.sum(dh, axis=0, keepdims=True)
            dn = dh * (g_ref[...] * (1.0 + sc_ref[...]))
            dx = dr_ref[...] + r * (dn - n * jnp.mean(dn * n, axis=-1, keepdims=True))
            dx_ref[...] = dx
            rest = outs[3:]
        else:
            x_ref, t_ref = ins[:2]
            dx_ref, part_ref = outs[:2]

            @pl.when(first)
            def _():
                part_ref[...] = jnp.zeros_like(part_ref)
            e = x_ref[...] - t_ref[...]
            dx = e * (1.0 / D)
            dx_ref[...] = dx
            e2 = (e * e).reshape(tm // 8, 8, D).sum(axis=0)
            acc = e2[:, 0:LANES]
            for j in range(1, D // LANES):
                acc = acc + e2[:, j * LANES:(j + 1) * LANES]
            part_ref[...] += acc
            rest = outs[2:]
        if post is not None:
            y_ref, gt_ref, pg_ref = ins[n_src:]
            dy_ref, dgate_ref, dpg_ref, dys_ref = rest

            @pl.when(first)
            def _():
                dgate_ref[...] = jnp.zeros_like(dgate_ref)
                dpg_ref[...] = jnp.zeros_like(dpg_ref)
                dys_ref[...] = jnp.zeros_like(dys_ref)
            yv = y_ref[...]
            ry = lax.rsqrt(jnp.mean(yv * yv, axis=-1, keepdims=True) + EPS)
            ny = yv * ry
            s = jnp.sum(dx * ny, axis=0, keepdims=True)
            dgate_ref[...] += s * pg_ref[...]
            dpg_ref[...] += s * gt_ref[...]
            dny = dx * (gt_ref[...] * pg_ref[...])
            dy = ry * (dny - ny * jnp.mean(dny * ny, axis=-1, keepdims=True))
            dy_ref[...] = _bf(dy)
            dys_ref[...] += jnp.sum(dy, axis=0, keepdims=True)

    tile = pl.BlockSpec((tm, D), lambda i: (i, 0))
    vec = jax.ShapeDtypeStruct((1, D), F32)
    big = jax.ShapeDtypeStruct((T, D), F32)
    if pre is not None:
        operands, in_specs = list(pre), [tile, tile, tile, _row(D), _row(D)]
        out_specs, out_shape = [tile, _row(D), _row(D)], [big, vec, vec]
    else:
        operands, in_specs = list(loss), [tile, tile]
        out_specs, out_shape = [tile, pl.BlockSpec((8, LANES), lambda i: (0, 0))], [big, jax.ShapeDtypeStruct((8, LANES), F32)]
    if post is not None:
        operands, in_specs = operands + list(post), in_specs + [tile, _row(D), _row(D)]
        out_specs = out_specs + [tile, _row(D), _row(D), _row(D)]
        out_shape = out_shape + [jax.ShapeDtypeStruct((T, D), BF16), vec, vec, vec]
    return pl.pallas_call(body, name=name, grid=(T // tm,), in_specs=in_specs, out_specs=out_specs, out_shape=out_shape,
                          compiler_params=_cp("arbitrary"))(*operands)


def _call(body, name, grid, in_specs, out_specs, out_shape, scratch, sems, operands, ex=None, vmem=VMEM_LIMIT):
    if ex is not None:
        def first():
            return functools.reduce(lambda p, q: p & q, [pl.program_id(k) == 0 for k in range(len(grid))])

        def last():
            return functools.reduce(lambda p, q: p & q, [pl.program_id(k) == grid[k] - 1 for k in range(len(grid))])
        body = _ride(body, len(in_specs), len(out_specs), ex, first, last)
        in_specs = in_specs + [ANY] * len(ex.ins)
        out_specs = out_specs + [ANY] * len(ex.out_shapes)
        out_shape = out_shape + ex.out_shapes
        scratch = scratch + ex.scratch
        operands = list(operands) + ex.ins
        sems = ("arbitrary",) * len(grid)
    res = pl.pallas_call(body, name=name, grid=grid, in_specs=in_specs, out_specs=out_specs, out_shape=out_shape,
                         scratch_shapes=scratch,
                         compiler_params=pltpu.CompilerParams(dimension_semantics=sems, vmem_limit_bytes=vmem))(*operands)
    n_own = len(res) - (len(ex.out_shapes) if ex is not None else 0)
    return res[:n_own], res[n_own:]


def mm_nn(a, bg, bias, name, ex=None):
    M, K = a.shape
    G, _, n = bg.shape
    tm = _rt(M, MM_ROWS)

    def body(a_ref, b_ref, bias_ref, o_ref):
        o_ref[...] = _nn(a_ref[...], b_ref[...]) + bias_ref[...]

    (out,), rider = _call(
        body, name, (M // tm, G),
        [pl.BlockSpec((tm, K), lambda i, j: (i, 0)), pl.BlockSpec((None, K, n), lambda i, j: (j, 0, 0)),
         pl.BlockSpec((1, n), lambda i, j: (0, j))],
        [pl.BlockSpec((tm, n), lambda i, j: (i, j))], [jax.ShapeDtypeStruct((M, G * n), F32)], [],
        ("parallel", "arbitrary"), (a, bg, bias), ex)
    return (out, rider) if ex is not None else out


def _next_input(xo, g_ref, sc_ref, sh_ref):
    r = lax.rsqrt(jnp.mean(xo * xo, axis=-1, keepdims=True) + EPS)
    return _bf((xo * r) * g_ref[...] * (1.0 + sc_ref[...]) + sh_ref[...])


def mm_post(a, b, bias, x, gate, pg, nxt, name):
    T, K = a.shape
    D = b.shape[1]
    tm = _rt(T, 512)

    def body(a_ref, b_ref, bias_ref, x_ref, gt_ref, pg_ref, g_ref, sc_ref, sh_ref, y_ref, xo_ref, h_ref):
        y = _nn(a_ref[...], b_ref[...]) + bias_ref[...]
        y_ref[...] = y
        r = lax.rsqrt(jnp.mean(y * y, axis=-1, keepdims=True) + EPS)
        xo = x_ref[...] + gt_ref[...] * ((y * r) * pg_ref[...])
        xo_ref[...] = xo
        h_ref[...] = _next_input(xo, g_ref, sc_ref, sh_ref)

    tile = pl.BlockSpec((tm, D), lambda i: (i, 0))
    out = jax.ShapeDtypeStruct((T, D), F32)
    return pl.pallas_call(
        body, name=name, grid=(T // tm,),
        in_specs=[pl.BlockSpec((tm, K), lambda i: (i, 0)), pl.BlockSpec((K, D), lambda i: (0, 0)), _row(D), tile]
        + [_row(D)] * 5,
        out_specs=[tile, tile, tile], out_shape=[out, out, jax.ShapeDtypeStruct((T, D), BF16)],
        compiler_params=_cp("parallel"))(a, b, bias, x, gate, pg, *nxt)


def mm_nt(dy, w, name, ex=None):
    T, N = dy.shape
    K = w.shape[0]
    tm = _rt(T, 512)

    def body(dy_ref, w_ref, o_ref):
        o_ref[...] = _nt(_bf(dy_ref[...]), w_ref[...])

    (out,), rider = _call(
        body, name, (T // tm,), [pl.BlockSpec((tm, N), lambda i: (i, 0)), pl.BlockSpec((K, N), lambda i: (0, 0))],
        [pl.BlockSpec((tm, K), lambda i: (i, 0))], [jax.ShapeDtypeStruct((T, K), F32)], [], ("parallel",),
        (dy, w), ex)
    return (out, rider) if ex is not None else out


def mm_tn(xm, gm, name, x_natural, g_natural, kx, n):
    if x_natural:
        T, Gx = xm.shape[0], xm.shape[1] // kx
    else:
        Gx, T = xm.shape[0], xm.shape[1]
    tt = _rt(T, 4 * MM_ROWS)
    if x_natural:
        xspec = pl.BlockSpec((tt, kx), lambda g, k, t: (t, k))
    else:
        xspec = pl.BlockSpec((None, tt, kx), lambda g, k, t: (k, t, 0))
    if g_natural:
        Gg = gm.shape[1] // n
        gspec = pl.BlockSpec((tt, n), lambda g, k, t: (t, g))
    else:
        Gg = gm.shape[0]
        gspec = pl.BlockSpec((None, tt, n), lambda g, k, t: (g, t, 0))

    NT = T // tt

    def body(x_ref, g_ref, o_ref, acc):
        t = pl.program_id(2)

        @pl.when(t == 0)
        def _():
            acc[...] = jnp.zeros_like(acc)
        acc[...] += _tn(_bf(x_ref[...]), _bf(g_ref[...]))

        @pl.when(t == NT - 1)
        def _():
            o_ref[...] = acc[...].astype(o_ref.dtype)

    return pl.pallas_call(
        body, name=name, grid=(Gg, Gx, NT), in_specs=[xspec, gspec],
        out_specs=pl.BlockSpec((None, kx, n), lambda g, k, t: (g, k, 0)),
        out_shape=jax.ShapeDtypeStruct((Gg, Gx * kx, n), BF16), scratch_shapes=[pltpu.VMEM((kx, n), F32)],
        compiler_params=_cp("parallel", "parallel", "arbitrary"))(xm, gm)


def _split3(x):
    h = _bf(x)
    r = x - h.astype(F32)
    m = _bf(r)
    lo = _bf(r - m.astype(F32))
    return h, m, lo


def _tri_mm(tri, x):
    h, m, lo = _split3(x)
    return _nn(tri, lo) + _nn(tri, m) + _nn(tri, h)


def _hgrn_gates(qr, fz, lb):
    sq = _sig(qr)
    q = qr * sq
    sig = _sig(fz)
    f = lb + (1.0 - lb) * sig
    k = 1.0 - f
    return sq, q, sig, f, k, jnp.log(f)


def hgrn_fwd(proj, lb, gg, name, ex=None):
    T, D4 = proj.shape
    D = D4 // 4
    H = D // LANES
    C = _rt(T, HG_CHUNK)
    SB = min(HG_SUB, C)
    NC = T // C

    def body(q_ref, f_ref, v_ref, g_ref, lb_ref, gg_ref, o_ref, og_ref, st_ref, ST, bex):
        @pl.when(pl.program_id(0) == 0)
        def _():
            ST[...] = jnp.zeros_like(ST)
        r_i = lax.broadcasted_iota(jnp.int32, (C, C), 0)
        c_i = lax.broadcasted_iota(jnp.int32, (C, C), 1)
        low = _bf((r_i >= c_i).astype(F32))
        rs = lax.broadcasted_iota(jnp.int32, (SB, C), 0)
        cs = lax.broadcasted_iota(jnp.int32, (SB, C), 1)
        for hd in range(H):
            sl = slice(hd * LANES, (hd + 1) * LANES)
            _, q, _, _, k, lf = _hgrn_gates(q_ref[:, sl], f_ref[:, sl], lb_ref[:, sl])
            v = v_ref[:, sl]
            b = _tri_mm(low, lf)
            bex[hd] = b - lf
            bend = jnp.sum(lf, axis=0, keepdims=True)
            blocks = []
            for I in range(C // SB):
                p = bex[hd, pl.ds(SB * I + SB // 2, 1), :]
                rows = slice(SB * I, SB * (I + 1))
                qI = _bf(q[rows] * jnp.exp(jnp.minimum(b[rows] - p, EXP_CLAMP)))
                kI = _bf(k * jnp.exp(jnp.minimum(p - b, EXP_CLAMP)))
                blocks.append(jnp.where(rs + SB * I >= cs, _nt(qI, kI), 0.0))
            A = jnp.concatenate(blocks, axis=0)
            st0 = ST[hd]
            st_ref[0, hd] = st0
            o = _nn(_bf(A), _bf(v)) + _nt(_bf(q * jnp.exp(b)), _bf(st0))
            ST[hd] = st0 * jnp.exp(bend) + _tn_st(v, k * jnp.exp(bend - b))
            o_ref[:, sl] = o
            r = lax.rsqrt(jnp.mean(o * o, axis=-1, keepdims=True) + EPS)
            gr = g_ref[:, sl]
            og_ref[:, sl] = _bf((o * r) * gg_ref[...] * (gr * _sig(gr)))

    col = lambda j: pl.BlockSpec((C, D), lambda c, j=j: (c, j))
    tile = pl.BlockSpec((C, D), lambda c: (c, 0))
    own, rider = _call(
        body, name, (NC,), [col(0), col(1), col(2), col(3), _row(D), _row(LANES)],
        [tile, tile, pl.BlockSpec((1, H, LANES, LANES), lambda c: (c, 0, 0, 0))],
        [jax.ShapeDtypeStruct((T, D), F32), jax.ShapeDtypeStruct((T, D), BF16),
         jax.ShapeDtypeStruct((NC, H, LANES, LANES), F32)],
        [pltpu.VMEM((H, LANES, LANES), F32), pltpu.VMEM((H, C, LANES), F32)], ("arbitrary",),
        (proj, proj, proj, proj, lb, gg), ex)
    return (*own, rider) if ex is not None else own


def hgrn_bwd(proj, o, dog, st, lb, gg, name, ex=None):
    T, D4 = proj.shape
    D = D4 // 4
    H = D // LANES
    C = _rt(T, HG_CHUNK)
    SB = min(HG_SUB, C)
    NC = T // C

    def body(q_ref, f_ref, v_ref, g_ref, o_ref, dog_ref, st_ref, lb_ref, gg_ref, dp_ref, dlb_ref, dgg_ref,
             DST, bex):
        @pl.when(pl.program_id(0) == 0)
        def _():
            DST[...] = jnp.zeros_like(DST)
            dlb_ref[...] = jnp.zeros_like(dlb_ref)
            dgg_ref[...] = jnp.zeros_like(dgg_ref)
        r_i = lax.broadcasted_iota(jnp.int32, (C, C), 0)
        c_i = lax.broadcasted_iota(jnp.int32, (C, C), 1)
        causal = r_i >= c_i
        low = _bf(causal.astype(F32))
        upp = _bf((r_i <= c_i).astype(F32))
        rs = lax.broadcasted_iota(jnp.int32, (SB, C), 0)
        cs = lax.broadcasted_iota(jnp.int32, (SB, C), 1)
        ggv = gg_ref[...]
        for hd in range(H):
            sl = slice(hd * LANES, (hd + 1) * LANES)
            qr = q_ref[:, sl]
            lbv = lb_ref[:, sl]
            sq, q, sig, f, k, lf = _hgrn_gates(qr, f_ref[:, sl], lbv)
            v = v_ref[:, sl]
            oh = o_ref[:, sl]
            r = lax.rsqrt(jnp.mean(oh * oh, axis=-1, keepdims=True) + EPS)
            n = oh * r
            gr = g_ref[:, sl]
            sg = _sig(gr)
            dgo = dog_ref[:, sl]
            dgr = dgo * (n * ggv) * (sg * (1.0 + gr * (1.0 - sg)))
            don = dgo * (gr * sg)
            dgg_ref[...] += jnp.sum(don * n, axis=0, keepdims=True)
            dn = don * ggv
            do = r * (dn - n * jnp.mean(dn * n, axis=-1, keepdims=True))
            b = _tri_mm(low, lf)
            bex[hd] = b - lf
            bend = jnp.sum(lf, axis=0, keepdims=True)
            dob = _bf(do)
            dA = _bf(jnp.where(causal, _nt(dob, _bf(v)), 0.0))
            blocks, dqs, gqs = [], [], []
            dk = jnp.zeros((C, LANES), F32)
            gk = jnp.zeros((C, LANES), F32)
            for I in range(C // SB):
                p = bex[hd, pl.ds(SB * I + SB // 2, 1), :]
                rows = slice(SB * I, SB * (I + 1))
                eq = jnp.exp(jnp.minimum(b[rows] - p, EXP_CLAMP))
                qI = _bf(q[rows] * eq)
                ek = jnp.exp(jnp.minimum(p - b, EXP_CLAMP))
                kI = _bf(k * ek)
                blocks.append(jnp.where(rs + SB * I >= cs, _nt(qI, kI), 0.0))
                dqI = _nn(dA[rows], kI)
                dkI = _tn(dA[rows], qI)
                dqs.append(dqI * eq)
                gqs.append(qI.astype(F32) * dqI)
                dk = dk + dkI * ek
                gk = gk + kI.astype(F32) * dkI
            A = jnp.concatenate(blocks, axis=0)
            dst = DST[hd]
            st0 = st_ref[0, hd]
            eb = jnp.exp(b)
            ee = jnp.exp(bend - b)
            ebend = jnp.exp(bend)
            dv = _tn(_bf(A), dob) + _nt(_bf(k * ee), _bf(dst))
            dk_state = ee * _nn_st(v, dst)
            dq_state = eb * _nn_st(do, st0)
            dq = jnp.concatenate(dqs, axis=0) + dq_state
            dk = dk + dk_state
            DST[hd] = dst * ebend + _tn_st(do, q * eb)
            later = jnp.sum(k * dk_state, axis=0, keepdims=True) + ebend * jnp.sum(st0 * dst, axis=0, keepdims=True)
            pairs = jnp.concatenate(gqs, axis=0) - gk + (q * dq_state - k * dk_state)
            dlf = _tri_mm(upp, pairs) + later
            df = dlf / f - dk
            dlb_ref[:, sl] += jnp.sum(df * (1.0 - sig), axis=0, keepdims=True)
            dp_ref[:, sl] = _bf(dq * (sq * (1.0 + qr * (1.0 - sq))))
            dp_ref[:, D + hd * LANES:D + (hd + 1) * LANES] = _bf(df * (1.0 - lbv) * (sig * (1.0 - sig)))
            dp_ref[:, 2 * D + hd * LANES:2 * D + (hd + 1) * LANES] = _bf(dv)
            dp_ref[:, 3 * D + hd * LANES:3 * D + (hd + 1) * LANES] = _bf(dgr)

    col = lambda j: pl.BlockSpec((C, D), lambda c, j=j: (NC - 1 - c, j))
    tile = pl.BlockSpec((C, D), lambda c: (NC - 1 - c, 0))
    own, rider = _call(
        body, name, (NC,),
        [col(0), col(1), col(2), col(3), tile, tile,
         pl.BlockSpec((1, H, LANES, LANES), lambda c: (NC - 1 - c, 0, 0, 0)), _row(D), _row(LANES)],
        [pl.BlockSpec((C, D4), lambda c: (NC - 1 - c, 0)), _row(D), _row(LANES)],
        [jax.ShapeDtypeStruct((T, D4), BF16), jax.ShapeDtypeStruct((1, D), F32), jax.ShapeDtypeStruct((1, LANES), F32)],
        [pltpu.VMEM((H, LANES, LANES), F32), pltpu.VMEM((H, C, LANES), F32)],
        ("arbitrary",), (proj, proj, proj, proj, o, dog, st, lb, gg), ex)
    return (*own, rider) if ex is not None else own


def _prev_blk(tm, hb):
    return lambda i: (jnp.maximum(i * (tm // hb) - 1, 0), 0)


def _next_blk(tm, hb, T):
    return lambda i: (jnp.minimum((i + 1) * (tm // hb), T // hb - 1), 0)


def _glu_to_ext(uc_ref, ucp_ref, ext, D, first):
    def glu(u):
        return u[:, :D] * _sig(u[:, D:])
    gh = jnp.where(first, 0.0, glu(ucp_ref[...]))
    gm = glu(uc_ref[...])
    for c in range(D // LANES):
        ext[c, 0:CONV_HALO, :] = gh[:, c * LANES:(c + 1) * LANES]
        ext[c, CONV_HALO:, :] = gm[:, c * LANES:(c + 1) * LANES]


def conv_mid_fwd(uc, cw, cb, lg, lbias, name):
    T, D2 = uc.shape
    D = D2 // 2
    NCH = D // LANES
    W = 31
    tm = _rt(T, 256)
    RS = min(CONV_ROWS, tm)

    def body(uc_ref, ucp_ref, cw_ref, cb_ref, lg_ref, lb_ref, cv_ref, va_ref, ext, outs):
        _glu_to_ext(uc_ref, ucp_ref, ext, D, pl.program_id(0) == 0)

        def chunk(c, carry):
            for rb in range(tm // RS):
                acc = jnp.zeros((RS, LANES), F32)
                for kk in range(W):
                    acc = acc + cw_ref[c, pl.ds(kk, 1), :] * ext[c, pl.ds(rb * RS + CONV_HALO - (W - 1) + kk, RS), :]
                outs[c, pl.ds(rb * RS, RS), :] = acc
            return carry
        lax.fori_loop(0, NCH, chunk, 0)
        cv = jnp.concatenate([outs[c] for c in range(NCH)], axis=1) + cb_ref[...]
        cv_ref[...] = cv
        mu = jnp.mean(cv, axis=-1, keepdims=True)
        xc = cv - mu
        rstd = lax.rsqrt(jnp.mean(xc * xc, axis=-1, keepdims=True) + EPS)
        l = xc * rstd * lg_ref[...] + lb_ref[...]
        va_ref[...] = _bf(l * _sig(l))

    tile = pl.BlockSpec((tm, D), lambda i: (i, 0))
    return pl.pallas_call(
        body, name=name, grid=(T // tm,),
        in_specs=[pl.BlockSpec((tm, D2), lambda i: (i, 0)), pl.BlockSpec((CONV_HALO, D2), _prev_blk(tm, CONV_HALO)),
                  pl.BlockSpec((NCH, 32, LANES), lambda i: (0, 0, 0)), _row(D), _row(D), _row(D)],
        out_specs=[tile, tile],
        out_shape=[jax.ShapeDtypeStruct((T, D), F32), jax.ShapeDtypeStruct((T, D), BF16)],
        scratch_shapes=[pltpu.VMEM((NCH, tm + CONV_HALO, LANES), F32), pltpu.VMEM((NCH, tm, LANES), F32)],
        compiler_params=_cp("parallel"))(uc, uc, cw, cb, lg, lbias)


def conv_bwd_a(dva, cv, uc, lg, lbias, name):
    T, D2 = uc.shape
    D = D2 // 2
    NCH = D // LANES
    W = 31
    tm = _rt(T, 256)
    RS = min(CONV_ROWS, tm)

    def body(dva_ref, cv_ref, uc_ref, ucp_ref, lg_ref, lb_ref, dcv_ref, dlg_ref, dlb_ref, taps_ref, ext, dcs):
        @pl.when(pl.program_id(0) == 0)
        def _():
            dlg_ref[...] = jnp.zeros_like(dlg_ref)
            dlb_ref[...] = jnp.zeros_like(dlb_ref)
            taps_ref[...] = jnp.zeros_like(taps_ref)
        _glu_to_ext(uc_ref, ucp_ref, ext, D, pl.program_id(0) == 0)
        cv = cv_ref[...]
        mu = jnp.mean(cv, axis=-1, keepdims=True)
        xc = cv - mu
        rstd = lax.rsqrt(jnp.mean(xc * xc, axis=-1, keepdims=True) + EPS)
        yn = xc * rstd
        l = yn * lg_ref[...] + lb_ref[...]
        s = _sig(l)
        dl = dva_ref[...] * (s * (1.0 + l * (1.0 - s)))
        dlg_ref[...] += jnp.sum(dl * yn, axis=0, keepdims=True)
        dlb_ref[...] += jnp.sum(dl, axis=0, keepdims=True)
        dyn = dl * lg_ref[...]
        dcv = rstd * (dyn - jnp.mean(dyn, axis=-1, keepdims=True) - yn * jnp.mean(dyn * yn, axis=-1, keepdims=True))
        dcv_ref[...] = dcv
        for c in range(NCH):
            dcs[c] = dcv[:, c * LANES:(c + 1) * LANES]

        def fold(p):
            return p.reshape(RS // 8, 8, LANES).sum(axis=0)

        def chunk(c, carry):
            for kk in range(W):
                tot = jnp.zeros((8, LANES), F32)
                for rb in range(tm // RS):
                    tot = tot + fold(dcs[c, pl.ds(rb * RS, RS), :]
                                     * ext[c, pl.ds(rb * RS + CONV_HALO - (W - 1) + kk, RS), :])
                taps_ref[c, pl.ds(8 * kk, 8), :] += tot
            tot = jnp.zeros((8, LANES), F32)
            for rb in range(tm // RS):
                tot = tot + fold(dcs[c, pl.ds(rb * RS, RS), :])
            taps_ref[c, pl.ds(8 * W, 8), :] += tot
            return carry
        lax.fori_loop(0, NCH, chunk, 0)

    tile = pl.BlockSpec((tm, D), lambda i: (i, 0))
    vec = jax.ShapeDtypeStruct((1, D), F32)
    return pl.pallas_call(
        body, name=name, grid=(T // tm,),
        in_specs=[tile, tile, pl.BlockSpec((tm, D2), lambda i: (i, 0)),
                  pl.BlockSpec((CONV_HALO, D2), _prev_blk(tm, CONV_HALO)), _row(D), _row(D)],
        out_specs=[tile, _row(D), _row(D), pl.BlockSpec((NCH, 256, LANES), lambda i: (0, 0, 0))],
        out_shape=[jax.ShapeDtypeStruct((T, D), F32), vec, vec, jax.ShapeDtypeStruct((NCH, 256, LANES), F32)],
        scratch_shapes=[pltpu.VMEM((NCH, tm + CONV_HALO, LANES), F32), pltpu.VMEM((NCH, tm, LANES), F32)],
        compiler_params=_cp("arbitrary"))(dva, cv, uc, uc, lg, lbias)


def conv_bwd_b(dcv, uc, cw, name):
    T, D2 = uc.shape
    D = D2 // 2
    NCH = D // LANES
    W = 31
    tm = _rt(T, 256)
    RS = min(CONV_ROWS, tm)
    NT = T // tm

    def body(dcv_ref, dcn_ref, uc_ref, cw_ref, du_ref, db_ref, ext, outs):
        i = pl.program_id(0)

        @pl.when(i == 0)
        def _():
            db_ref[...] = jnp.zeros_like(db_ref)
        dm = dcv_ref[...]
        dn = jnp.where(i == NT - 1, 0.0, dcn_ref[...])
        for c in range(NCH):
            ext[c, 0:tm, :] = dm[:, c * LANES:(c + 1) * LANES]
            ext[c, tm:, :] = dn[:, c * LANES:(c + 1) * LANES]

        def chunk(c, carry):
            for rb in range(tm // RS):
                acc = jnp.zeros((RS, LANES), F32)
                for kk in range(W):
                    acc = acc + cw_ref[c, pl.ds(kk, 1), :] * ext[c, pl.ds(rb * RS + (W - 1) - kk, RS), :]
                outs[c, pl.ds(rb * RS, RS), :] = acc
            return carry
        lax.fori_loop(0, NCH, chunk, 0)
        dglu = jnp.concatenate([outs[c] for c in range(NCH)], axis=1)
        u = uc_ref[...]
        a = u[:, :D]
        sg = _sig(u[:, D:])
        da = dglu * sg
        dg = dglu * a * (sg * (1.0 - sg))
        du_ref[:, :D] = _bf(da)
        du_ref[:, D:] = _bf(dg)
        db_ref[:, :D] += jnp.sum(da, axis=0, keepdims=True)
        db_ref[:, D:] += jnp.sum(dg, axis=0, keepdims=True)

    tile = pl.BlockSpec((tm, D), lambda i: (i, 0))
    wide = pl.BlockSpec((tm, D2), lambda i: (i, 0))
    return pl.pallas_call(
        body, name=name, grid=(NT,),
        in_specs=[tile, pl.BlockSpec((CONV_HALO, D), _next_blk(tm, CONV_HALO, T)), wide,
                  pl.BlockSpec((NCH, 32, LANES), lambda i: (0, 0, 0))],
        out_specs=[wide, _row(D2)],
        out_shape=[jax.ShapeDtypeStruct((T, D2), BF16), jax.ShapeDtypeStruct((1, D2), F32)],
        scratch_shapes=[pltpu.VMEM((NCH, tm + CONV_HALO, LANES), F32), pltpu.VMEM((NCH, tm, LANES), F32)],
        compiler_params=_cp("arbitrary"))(dcv, dcv, uc, cw)


def _shift_rows(x, halo, k, back):
    rows, n = x.shape
    x3, h3 = x.reshape(rows // 8, 8, n), halo.reshape(1, 8, n)
    sub = lax.broadcasted_iota(jnp.int32, (1, 8, n), 1)
    if back:
        r = pltpu.roll(jnp.concatenate([h3, x3], axis=0), k, 1)
        y = jnp.where(sub < k, r[:-1], r[1:])
    else:
        r = pltpu.roll(jnp.concatenate([x3, h3], axis=0), 8 - k, 1)
        y = jnp.where(sub < 8 - k, r[:-1], r[1:])
    return y.reshape(rows, n)


def _conv3(u, uh, dw_ref, bias_ref):
    return (dw_ref[pl.ds(0, 1), :] * _shift_rows(u, uh, 2, True) + dw_ref[pl.ds(1, 1), :] * _shift_rows(u, uh, 1, True)
            + dw_ref[pl.ds(2, 1), :] * u + bias_ref[...])


def ffn_fwd(h, wup, dww, dwb, wdn, x, gate, pg, nxt, name):
    T, D = h.shape
    fs = wup.shape[2]
    NJ = wup.shape[0] // 2
    tm = _rt(T, 512)
    n_nxt = 0 if nxt is None else 3

    fsp = wdn.shape[0] // NJ

    def body(h_ref, hp_ref, wa_ref, wb_ref, dwa_ref, dwb_ref, ba_ref, bb_ref, wd_ref, x_ref, gt_ref, pg_ref, *rest):
        nx, (u_ref, uc_ref, y_ref, xo_ref), more = rest[:n_nxt], rest[n_nxt:n_nxt + 4], rest[n_nxt + 4:]
        zb = more[-1]
        i, j = pl.program_id(0), pl.program_id(1)

        @pl.when((i == 0) & (j == 0))
        def _():
            zb[...] = jnp.zeros_like(zb)
        hb = h_ref[...]
        hp = hp_ref[...]

        def half(s, w_ref, dw_ref, b_ref):
            u = _nn(hb, w_ref[...])
            uh = jnp.where(i == 0, 0.0, _nn(hp, w_ref[...]))
            u_ref[s] = u
            uc = _conv3(u, uh, dw_ref, b_ref)
            uc_ref[s] = uc
            return uc
        a = half(0, wa_ref, dwa_ref, ba_ref)
        b = half(1, wb_ref, dwb_ref, bb_ref)
        zb[j, :, 0:fs] = _bf(a * _sig(a) * b)

        @pl.when(j == NJ - 1)
        def _():
            y = _nn(jnp.concatenate([zb[jj] for jj in range(NJ)], axis=1), wd_ref[...])
            y_ref[...] = y
            r = lax.rsqrt(jnp.mean(y * y, axis=-1, keepdims=True) + EPS)
            xo = x_ref[...] + gt_ref[...] * ((y * r) * pg_ref[...])
            xo_ref[...] = xo
            if nxt is not None:
                more[0][...] = _next_input(xo, *nx)

    tile = pl.BlockSpec((tm, D), lambda i, j: (i, 0))
    shard = lambda off, r: pl.BlockSpec((None, r, fs), lambda i, j: (j + off, 0, 0))
    ush = pl.BlockSpec((2, None, tm, fs), lambda i, j: (0, j, i, 0))
    vec = pl.BlockSpec((1, D), lambda i, j: (0, 0))
    return pl.pallas_call(
        body, name=name, grid=(T // tm, NJ),
        in_specs=[tile, pl.BlockSpec((FFN_HALO, D), lambda i, j: (jnp.maximum(i * (tm // FFN_HALO) - 1, 0), 0)),
                  shard(0, D), shard(NJ, D), shard(0, 3), shard(NJ, 3), shard(0, 1), shard(NJ, 1),
                  pl.BlockSpec((NJ * fsp, D), lambda i, j: (0, 0)), tile, vec, vec] + [vec] * n_nxt,
        out_specs=[ush, ush, tile, tile] + [tile] * (n_nxt > 0),
        out_shape=[jax.ShapeDtypeStruct((2, NJ, T, fs), F32), jax.ShapeDtypeStruct((2, NJ, T, fs), F32),
                   jax.ShapeDtypeStruct((T, D), F32), jax.ShapeDtypeStruct((T, D), F32)]
        + [jax.ShapeDtypeStruct((T, D), BF16)] * (n_nxt > 0),
        scratch_shapes=[pltpu.VMEM((NJ, tm, fsp), BF16)],
        compiler_params=_cp("arbitrary", "arbitrary"))(h, h, wup, wup, dww, dww, dwb, dwb, wdn, x, gate, pg,
                                                       *(nxt or ()))


def ffn_bwd_a(dy, uc, wdn, name):
    _, NJ, T, fs = uc.shape
    D = dy.shape[1]
    tm = _rt(T, 512)

    NT = T // tm

    def body(dy_ref, uc_ref, wd_ref, duc_ref, dwd_ref, acc):
        i = pl.program_id(1)

        @pl.when(i == 0)
        def _():
            acc[...] = jnp.zeros_like(acc)
        dyb = _bf(dy_ref[...])
        dz = _nt(dyb, wd_ref[...])
        a = uc_ref[0]
        b = uc_ref[1]
        sa = _sig(a)
        asa = a * sa
        acc[...] += _tn(_bf(asa * b), dyb)
        duc_ref[0] = dz * b * (sa + asa * (1.0 - sa))
        duc_ref[1] = dz * asa

        @pl.when(i == NT - 1)
        def _():
            dwd_ref[...] = _bf(acc[...])

    ush = pl.BlockSpec((2, None, tm, fs), lambda j, i: (0, j, i, 0))
    return pl.pallas_call(
        body, name=name, grid=(NJ, NT),
        in_specs=[pl.BlockSpec((tm, D), lambda j, i: (i, 0)), ush, pl.BlockSpec((None, fs, D), lambda j, i: (j, 0, 0))],
        out_specs=[ush, pl.BlockSpec((fs, D), lambda j, i: (j, 0))],
        out_shape=[jax.ShapeDtypeStruct((2, NJ, T, fs), F32), jax.ShapeDtypeStruct((NJ * fs, D), BF16)],
        scratch_shapes=[pltpu.VMEM((fs, D), F32)],
        compiler_params=_cp("parallel", "arbitrary"))(dy, uc, wdn)


def ffn_bwd_b(duc, u, h, dww, wup, name, ex=None):
    NS, T, fs = duc.shape
    D = wup.shape[1]
    tm = _rt(T, 512)
    NT = T // tm

    def body(d_ref, dn_ref, u_ref, h_ref, dw_ref, w_ref, dh_ref, g_ref, dwup_ref, acc, accw, stage):
        i, s = pl.program_id(0), pl.program_id(1)

        @pl.when((i == 0) & (s == 0))
        def _():
            g_ref[...] = jnp.zeros_like(g_ref)

        @pl.when(s == 0)
        def _():
            acc[...] = jnp.zeros_like(acc)
        d = d_ref[...]
        dn = jnp.where(i == NT - 1, 0.0, dn_ref[...])
        d1 = _shift_rows(d, dn, 1, False)
        d2 = _shift_rows(d, dn, 2, False)
        du = _bf(dw_ref[pl.ds(2, 1), :] * d + dw_ref[pl.ds(1, 1), :] * d1 + dw_ref[pl.ds(0, 1), :] * d2)
        acc[...] += _nt(du, w_ref[...])
        dw_part = _tn(h_ref[...], du)

        @pl.when(i == 0)
        def _():
            accw[s] = dw_part

        @pl.when(i > 0)
        def _():
            accw[s] += dw_part

        @pl.when(i == NT - 1)
        def _():
            stage[...] = _bf(accw[s])
            pltpu.sync_copy(stage, dwup_ref.at[s])
        u = u_ref[...]
        g_ref[s, pl.ds(0, 1), :] += jnp.sum(d2 * u, axis=0, keepdims=True)
        g_ref[s, pl.ds(1, 1), :] += jnp.sum(d1 * u, axis=0, keepdims=True)
        g_ref[s, pl.ds(2, 1), :] += jnp.sum(d * u, axis=0, keepdims=True)
        g_ref[s, pl.ds(3, 1), :] += jnp.sum(d, axis=0, keepdims=True)

        @pl.when(s == NS - 1)
        def _():
            dh_ref[...] = acc[...]

    ush = pl.BlockSpec((None, tm, fs), lambda i, s: (s, i, 0))
    unext = pl.BlockSpec((None, FFN_HALO, fs),
                         lambda i, s: (s, jnp.minimum((i + 1) * (tm // FFN_HALO), T // FFN_HALO - 1), 0))
    tile = pl.BlockSpec((tm, D), lambda i, s: (i, 0))
    own, rider = _call(
        body, name, (NT, NS),
        [ush, unext, ush, tile, pl.BlockSpec((None, 3, fs), lambda i, s: (s, 0, 0)),
         pl.BlockSpec((None, D, fs), lambda i, s: (s, 0, 0))],
        [tile, pl.BlockSpec((NS, 8, fs), lambda i, s: (0, 0, 0)), ANY],
        [jax.ShapeDtypeStruct((T, D), F32), jax.ShapeDtypeStruct((NS, 8, fs), F32),
         jax.ShapeDtypeStruct((NS, D, fs), BF16)],
        [pltpu.VMEM((tm, D), F32), pltpu.VMEM((NS, D, fs), F32), pltpu.VMEM((D, fs), BF16)],
        ("arbitrary", "arbitrary"), (duc, duc, u, h, dww, wup), ex, FFN_BWD_VMEM)
    return (*own, rider) if ex is not None else own


def ada_fwd(c_all, w, bias, name):
    L, D, n = w.shape

    def body(c_ref, w_ref, b_ref, o_ref):
        cv = c_ref[...]
        o_ref[...] = _nn(_bf(cv * _sig(cv)), _bf(w_ref[...])) + b_ref[...]

    return pl.pallas_call(
        body, name=name, grid=(L,),
        in_specs=[pl.BlockSpec((N_DEV, D), lambda l: (0, 0)), pl.BlockSpec((None, D, n), lambda l: (l, 0, 0)),
                  pl.BlockSpec((None, 1, n), lambda l: (l, 0, 0))],
        out_specs=pl.BlockSpec((None, N_DEV, n), lambda l: (l, 0, 0)),
        out_shape=jax.ShapeDtypeStruct((L, N_DEV, n), F32), compiler_params=_cp("parallel"))(c_all, w, bias)


def ada_bwd(c_all, dm, name):
    L, _, n = dm.shape
    D = c_all.shape[1]

    def body(c_ref, d_ref, o_ref):
        cv = c_ref[...]
        o_ref[...] = _tn(_bf(cv * _sig(cv)), _bf(d_ref[...]))

    return pl.pallas_call(
        body, name=name, grid=(L,),
        in_specs=[pl.BlockSpec((N_DEV, D), lambda l: (0, 0)), pl.BlockSpec((None, N_DEV, n), lambda l: (l, 0, 0))],
        out_specs=pl.BlockSpec((None, D, n), lambda l: (l, 0, 0)),
        out_shape=jax.ShapeDtypeStruct((L, D, n), F32), compiler_params=_cp("parallel"))(c_all, dm)


def sum_slots(g, name):
    S, R, C = g.shape

    def body(g_ref, o_ref):
        acc = g_ref[0]
        for s in range(1, S):
            acc = acc + g_ref[s]
        o_ref[...] = acc

    return pl.pallas_call(
        body, name=name, grid=(1,), in_specs=[pl.BlockSpec((S, R, C), lambda i: (0, 0, 0))],
        out_specs=pl.BlockSpec((R, C), lambda i: (0, 0)), out_shape=jax.ShapeDtypeStruct((R, C), F32),
        compiler_params=_cp("arbitrary"))(g)


def lb_softmax(logits, name):
    def body(l_ref, s_ref):
        l = l_ref[...]
        e = jnp.exp(l - jnp.max(l, axis=0, keepdims=True))
        s_ref[...] = e / jnp.sum(e, axis=0, keepdims=True)
    return pl.pallas_call(body, name=name, out_shape=jax.ShapeDtypeStruct(logits.shape, F32))(logits)


def adamw(w, gparts, m, v, name):
    P = len(gparts)
    S, Rp, C = gparts[0].shape
    R = P * Rp
    tr = Rp
    budget = (4 * 1024 * 1024) // (4 * (P * S + 7) * C)
    if Rp > budget:
        tr = max(t for t in range(16, budget + 1, 16) if Rp % t == 0)
    per = Rp // tr

    def body(w_ref, *rest):
        g_refs, (m_ref, v_ref, go_ref, d_ref, mo_ref, vo_ref) = rest[:P], rest[P:]
        part = pl.program_id(0) // per
        g = None
        for p, g_ref in enumerate(g_refs):
            gp = g_ref[0].astype(F32)
            for s in range(1, S):
                gp = gp + g_ref[s].astype(F32)
            g = gp if g is None else jnp.where(part == p, gp, g)
        go_ref[...] = g
        mn = ADAM_B1 * m_ref[...] + (1.0 - ADAM_B1) * g
        vn = ADAM_B2 * v_ref[...] + (1.0 - ADAM_B2) * (g * g)
        mo_ref[...] = mn
        vo_ref[...] = vn
        m_hat = mn / (1.0 - ADAM_B1 ** ADAM_STEP)
        v_hat = vn / (1.0 - ADAM_B2 ** ADAM_STEP)
        d_ref[...] = -ADAM_LR * (m_hat / (jnp.sqrt(v_hat) + ADAM_EPS) + ADAM_WD * w_ref[...])

    tile = pl.BlockSpec((tr, C), lambda i: (i, 0))
    slots = [pl.BlockSpec((S, tr, C), lambda i, p=p: (0, jnp.clip(i - p * per, 0, per - 1), 0)) for p in range(P)]
    out = jax.ShapeDtypeStruct((R, C), F32)
    return pl.pallas_call(
        body, name=name, grid=(R // tr,), in_specs=[tile] + slots + [tile, tile],
        out_specs=[tile] * 4, out_shape=[out] * 4, compiler_params=_cp("parallel"))(w, *gparts, m, v)


def _place():
    return lax.axis_index("x"), lax.axis_index("y"), lax.axis_index("c")


def _slot(p):
    return 4 * p[0] + 2 * p[1] + p[2]


class _Gather:
    def __init__(self, xs):
        self.ins = list(xs)
        n = self.n = len(xs)
        self.out_shapes = [jax.ShapeDtypeStruct((N_DEV,) + a.shape, a.dtype) for a in xs]
        self.scratch = [pltpu.SemaphoreType.DMA((n, 7)), pltpu.SemaphoreType.DMA((n, 7)), pltpu.SemaphoreType.DMA((n,))]

    def _parts(self, ins, outs, sems):
        send_sems, recv_sems, local_sems = sems
        x, y, c = _place()
        me, sib = (x, y, c), (x, y, 1 - c)
        chips = [(1 - x, y), (x, 1 - y), (1 - x, 1 - y)]

        def copy(a, k, block, to, src=None):
            dst = outs[a].at[_slot(block)]
            return pltpu.make_async_remote_copy(
                src_ref=dst if src is None else src, dst_ref=dst, send_sem=send_sems.at[a, k],
                recv_sem=recv_sems.at[a, k], device_id=to, device_id_type=MESH)

        mine = [pltpu.make_async_copy(ins[a], outs[a].at[_slot(me)], local_sems.at[a]) for a in range(self.n)]
        first = []
        for a in range(self.n):
            first.append(copy(a, 0, me, sib, src=ins[a]))
            first += [copy(a, 1 + j, me, (*chip, c), src=ins[a]) for j, chip in enumerate(chips)]
        return copy, mine, first, me, sib, chips, c

    def start(self, ins, outs, sems):
        _, mine, first, *_ = self._parts(ins, outs, sems)
        for cp in mine + first:
            cp.start()

    def finish(self, ins, outs, sems):
        copy, mine, first, me, sib, chips, c = self._parts(ins, outs, sems)
        passed = []
        for j, chip in enumerate(chips):
            for a in range(self.n):
                copy(a, 1 + j, (*chip, c), me).wait_recv()
                fwd = copy(a, 4 + j, (*chip, c), sib)
                fwd.start()
                passed.append(fwd)
        for a in range(self.n):
            copy(a, 0, sib, me).wait_recv()
            for j, chip in enumerate(chips):
                copy(a, 4 + j, (*chip, 1 - c), me).wait_recv()
        for cp in first + passed:
            cp.wait_send()
        for cp in mine:
            cp.wait()


def _run_alone(ex, name):
    def body(*refs):
        ni, no = len(ex.ins), len(ex.out_shapes)
        parts = refs[:ni], refs[ni:ni + no], refs[ni + no:]
        ex.start(*parts)
        ex.finish(*parts)
    return pl.pallas_call(body, name=name, in_specs=[ANY] * len(ex.ins), out_specs=[ANY] * len(ex.out_shapes),
                          out_shape=ex.out_shapes, scratch_shapes=ex.scratch)(*ex.ins)


def _ride(body, n_in, n_out, ex, first, last):
    ni, no = len(ex.ins), len(ex.out_shapes)

    def wrapped(*refs):
        a, r_in = refs[:n_in], refs[n_in:n_in + ni]
        b, r_out = refs[n_in + ni:n_in + ni + n_out], refs[n_in + ni + n_out:n_in + ni + n_out + no]
        rest = refs[n_in + ni + n_out + no:]
        s, r_s = rest[:len(rest) - len(ex.scratch)], rest[len(rest) - len(ex.scratch):]

        @pl.when(first())
        def _():
            ex.start(r_in, r_out, r_s)
        body(*a, *b, *s)

        @pl.when(last())
        def _():
            ex.finish(r_in, r_out, r_s)
    return wrapped


def all_gather(xs, name):
    return _run_alone(_Gather(xs), name)


class _Exchange:
    def __init__(self, groups):
        self.ins = [a for grp in groups for a in grp]
        self.where = [(g, i) for g, grp in enumerate(groups) for i in range(len(grp))]
        n = self.n = len(self.ins)
        self.out_shapes = [jax.ShapeDtypeStruct((N_DEV, len(grp)) + grp[0].shape[1:], grp[0].dtype) for grp in groups]
        self.scratch = [pltpu.SemaphoreType.DMA((n, 7)), pltpu.SemaphoreType.DMA((n, 7)), pltpu.SemaphoreType.DMA((n,))]

    def _parts(self, ins, outs, sems):
        send_sems, recv_sems, local_sems = sems
        x, y, c = _place()
        me = (x, y, c)

        def peer(r):
            return (1 - x if r & 4 else x, 1 - y if r & 2 else y, 1 - c if r & 1 else c)

        def land(a, src):
            g, i = self.where[a]
            return outs[g].at[_slot(src), i]

        def copy(a, r, src_dev):
            p = peer(r)
            return pltpu.make_async_remote_copy(
                src_ref=ins[a].at[_slot(p)], dst_ref=land(a, src_dev), send_sem=send_sems.at[a, r - 1],
                recv_sem=recv_sems.at[a, r - 1], device_id=p, device_id_type=MESH)

        mine = [pltpu.make_async_copy(ins[a].at[_slot(me)], land(a, me), local_sems.at[a]) for a in range(self.n)]
        sends = [copy(a, r, me) for r in range(1, N_DEV) for a in range(self.n)]
        arrivals = [copy(a, r, peer(r)) for r in range(1, N_DEV) for a in range(self.n)]
        return mine, sends, arrivals

    def start(self, ins, outs, sems):
        mine, sends, _ = self._parts(ins, outs, sems)
        for cp in mine + sends:
            cp.start()

    def finish(self, ins, outs, sems):
        mine, sends, arrivals = self._parts(ins, outs, sems)
        for cp in arrivals:
            cp.wait_recv()
        for cp in sends:
            cp.wait_send()
        for cp in mine:
            cp.wait()


def all_to_all(groups, name):
    return _run_alone(_Exchange(groups), name)


def _mods(mod, l, D):
    return [mod[l:l + 1, k * D:(k + 1) * D] for k in range(6)]


def _ffn_backward(l, dxo, dy, hb, u, uc, xin, sc2, gains, W, post, ex=None):
    _, NJ, T, fs = u.shape
    duc, d_wdn = ffn_bwd_a(dy, uc, W["wdn"][l], f"ffn{l}_bwd_a")
    dh, taps, d_wup, *rider = ffn_bwd_b(duc.reshape(2 * NJ, T, fs), u.reshape(2 * NJ, T, fs), hb, W["fdw_w"][l],
                                        W["wup"][l], f"ffn{l}_bwd_b", ex)
    dx, da, dsh, *below = resid_bwd(f"ffn{l}_resid_bwd", pre=(dh, xin, dxo, gains["pre_ffn_g"][l:l + 1], sc2), post=post)
    small = dict(da=da, dsh=dsh, dwb=taps[:, 3], dww=taps[:, 0:3])
    return dx, d_wup, d_wdn, small, below, (rider[0] if rider else None)


def _local_step(xt, tgt, mod, gains, W, mine):
    T, D = xt.shape
    zero_d = jnp.zeros((1, D), F32)
    half = lambda g: g.reshape(N_DEV // 2, 2 * g.shape[1], D)
    whole = lambda g: jnp.transpose(g, (1, 0, 2)).reshape(g.shape[1], -1)
    sh1, sc1, g1, sh2, sc2, g2 = m0 = _mods(mod, 0, D)
    h0, (whin,) = prenorm_mod(xt, gains["pre_mix_g"][0:1], sc1, sh1, "l0_prenorm", _Gather([mine["whin"]]))
    W = dict(W, whin=whin)
    proj, (whout, wup0, wdn0) = mm_nn(h0, W["whin"], jnp.zeros((1, 4 * D), F32), "hgrn_proj",
                                      _Gather([mine["whout"], mine["wup"][0], mine["wdn"][0]]))
    o, og, st, (wcin, wcout, wup1, wdn1) = hgrn_fwd(
        proj, W["lb"], W["gg"], "hgrn_fwd", _Gather([mine["wcin"], mine["wcout"], mine["wup"][1], mine["wdn"][1]]))
    W = dict(W, whout=whout.reshape(D, D), wcin=wcin, wcout=wcout.reshape(D, D), wup=[wup0, wup1],
             wdn=[half(wdn0), half(wdn1)])
    lanes = lambda g: jnp.pad(g, ((0, 0), (0, (-g.shape[1]) % LANES), (0, 0))).reshape(-1, D)
    W["wdn_fwd"] = [lanes(g) for g in W["wdn"]]
    t1, c1, e1, t2, c2, e2 = m1 = _mods(mod, 1, D)
    y0, x1, h0f = mm_post(og, W["whout"], zero_d, xt, g1, gains["post_mix_g"][0:1],
                          (gains["pre_ffn_g"][0:1], sc2, sh2), "hgrn_out")
    u0, uc0, yf0, x2, h1 = ffn_fwd(h0f, W["wup"][0], W["fdw_w"][0], W["fdw_b"][0], W["wdn_fwd"][0], x1, g2,
                                   gains["post_ffn_g"][0:1], (gains["pre_mix_g"][1:2], c1, t1), "ffn0_fwd")
    uc = mm_nn(h1, W["wcin"], W["cb_in"], "conv_in")
    cv, va = conv_mid_fwd(uc, W["cw"], W["cdw_b"], W["ln_g"], W["ln_b"], "conv_mid_fwd")
    y1, x3, h1f = mm_post(va, W["wcout"], W["cb_out"], x2, e1, gains["post_mix_g"][1:2],
                          (gains["pre_ffn_g"][1:2], c2, t2), "conv_out")
    u1, uc1, yf1, x4 = ffn_fwd(h1f, W["wup"][1], W["fdw_w"][1], W["fdw_b"][1], W["wdn_fwd"][1], x3, e2,
                               gains["post_ffn_g"][1:2], None, "ffn1_fwd")
    dx4, loss_part, dyf1, dgate_f1, dpost_f1, _ = resid_bwd(
        "loss_resid_bwd", loss=(x4, tgt), post=(yf1, e2, gains["post_ffn_g"][1:2]))
    dx3, d_wup1, d_wdn1, sf1, (dy1, dgate_c, dpost_c, dys_c), _ = _ffn_backward(
        1, dx4, dyf1, h1f, u1, uc1, x3, c2, gains, W, (y1, e1, gains["post_mix_g"][1:2]))
    dva = mm_nt(dy1, W["wcout"], "conv_dva")
    d_wcout = mm_tn(va, dy1, "conv_dwout", True, True, D, D)
    dcv, dlg, dlbias, taps = conv_bwd_a(dva, cv, uc, W["ln_g"], W["ln_b"], "conv_bwd_a")
    duc, dbin = conv_bwd_b(dcv, uc, W["cw"], "conv_bwd_b")
    dh1 = mm_nt(duc, whole(W["wcin"]), "conv_dh")
    d_wcin = mm_tn(h1, duc, "conv_dwin", True, True, D, W["wcin"].shape[2])
    dx2, da_c, dsh_c, dyf0, dgate_f0, dpost_f0, _ = resid_bwd(
        "l1_resid_bwd", pre=(dh1, x2, dx3, gains["pre_mix_g"][1:2], c1), post=(yf0, g2, gains["post_ffn_g"][0:1]))
    rows = lambda g: g.reshape(N_DEV, -1, D)
    dx1, d_wup0, d_wdn0, sf0, (dy0, dgate_h, dpost_h, _), (r_wcin, r_wcout, r_wup1, r_wdn1) = _ffn_backward(
        0, dx2, dyf0, h0f, u0, uc0, x1, sc2, gains, W, (y0, g1, gains["post_mix_g"][0:1]),
        _Exchange([[d_wcin], [rows(d_wcout)], [d_wup1], [rows(d_wdn1)]]))
    dog = mm_nt(dy0, W["whout"], "hgrn_dog")
    d_whout = mm_tn(og, dy0, "hgrn_dwout", True, True, D, D)
    dproj, dlb, dgg, (r_wup0, r_wdn0, r_whout) = hgrn_bwd(
        proj, o, dog, st, W["lb"], W["gg"], "hgrn_bwd", _Exchange([[d_wup0], [rows(d_wdn0)], [rows(d_whout)]]))
    d_whin = mm_tn(h0, dproj, "hgrn_dwin", True, True, D, W["whin"].shape[2])
    dh0, (r_whin,) = mm_nt(dproj, whole(W["whin"]), "hgrn_dh", _Exchange([[d_whin]]))
    dx0, da_h, dsh_h = resid_bwd("l0_resid_bwd", pre=(dh0, xt, dx1, gains["pre_mix_g"][0:1], sc1))
    sf0.update(dgate=dgate_f0, dpost=dpost_f0)
    sf1.update(dgate=dgate_f1, dpost=dpost_f1)

    big = dict(hgrn_w_in=r_whin, hgrn_w_out=r_whout, conv_w_in=r_wcin, conv_w_out=r_wcout,
               ffn_w_up=[r_wup0, r_wup1], ffn_w_down=[r_wdn0, r_wdn1])
    pm, pf = gains["pre_mix_g"], gains["pre_ffn_g"]
    dmod = jnp.concatenate([
        dsh_h, da_h * pm[0:1], dgate_h, sf0["dsh"], sf0["da"] * pf[0:1], sf0["dgate"],
        dsh_c, da_c * pm[1:2], dgate_c, sf1["dsh"], sf1["da"] * pf[1:2], sf1["dgate"]], axis=1)
    NCH = D // LANES
    tap = taps.reshape(NCH, 32, 8, LANES).sum(axis=2)
    small = dict(
        dmod=dmod,
        pre_mix_g=jnp.concatenate([da_h * (1.0 + sc1), da_c * (1.0 + c1)], axis=0),
        post_mix_g=jnp.concatenate([dpost_h, dpost_c], axis=0),
        pre_ffn_g=jnp.concatenate([sf0["da"] * (1.0 + sc2), sf1["da"] * (1.0 + c2)], axis=0),
        post_ffn_g=jnp.concatenate([sf0["dpost"], sf1["dpost"]], axis=0),
        lb=dlb, gg=dgg,
        ffn_dw_b=jnp.stack([sf0["dwb"], sf1["dwb"]]),
        conv_b_in=dbin,
        conv_dw_w=jnp.transpose(tap[:, :31], (1, 0, 2)).reshape(31, D),
        conv_dw_b=tap[:, 31].reshape(1, D),
        conv_ln_g=dlg, conv_ln_b=dlbias, conv_b_out=dys_c,
        ffn_dw_w=jnp.stack([sf0["dww"], sf1["dww"]]))
    return loss_part, dx0, big, small


SMALL_ORDER = ["dmod", "pre_mix_g", "post_mix_g", "pre_ffn_g", "post_ffn_g", "lb", "gg", "ffn_dw_b", "conv_b_in",
               "conv_dw_w", "conv_dw_b", "conv_ln_g", "conv_ln_b", "conv_b_out", "ffn_dw_w"]


def _pack(parts):
    flat = jnp.concatenate([p.reshape(-1) for p in parts])
    pad = (-flat.shape[0]) % LANES
    if pad:
        flat = jnp.concatenate([flat, jnp.zeros((pad,), F32)])
    return flat.reshape(-1, LANES)


def _unpack(flat, shapes):
    out, off = [], 0
    for s in shapes:
        n = 1
        for d in s:
            n *= d
        out.append(flat[off:off + n].reshape(s))
        off += n
    return out


def _apply_adamw(name, w, g_slots, m, v):
    shp = w.shape
    C = shp[-1]
    R = w.size // C
    parts = g_slots if isinstance(g_slots, list) else [g_slots]
    parts = [p.reshape(p.shape[0], R // len(parts), C) for p in parts]
    g, d, mn, vn = adamw(w.reshape(R, C), parts, m.reshape(R, C), v.reshape(R, C), "adamw_" + name)
    return g.reshape(shp), d.reshape(shp), mn.reshape(shp), vn.reshape(shp)


WEIGHTS = ['ada_w', 'ada_b', 'pre_mix_g', 'post_mix_g', 'pre_ffn_g', 'post_ffn_g', 'hgrn_w_in', 'hgrn_lb_logits',
           'hgrn_gnorm_g', 'hgrn_w_out', 'conv_w_in', 'conv_b_in', 'conv_dw_w', 'conv_dw_b', 'conv_ln_g', 'conv_ln_b',
           'conv_w_out', 'conv_b_out', 'ffn_w_up', 'ffn_dw_w', 'ffn_dw_b', 'ffn_w_down']


def kernel(x, c, ada_w, ada_b, pre_mix_g, post_mix_g, pre_ffn_g, post_ffn_g, hgrn_w_in, hgrn_lb_logits, hgrn_gnorm_g, hgrn_w_out, conv_w_in, conv_b_in, conv_dw_w, conv_dw_b, conv_ln_g, conv_ln_b, conv_w_out, conv_b_out, ffn_w_up, ffn_dw_w, ffn_dw_b, ffn_w_down, loss_target, m_ada_w, m_ada_b, m_pre_mix_g, m_post_mix_g, m_pre_ffn_g, m_post_ffn_g, m_hgrn_w_in, m_hgrn_lb_logits, m_hgrn_gnorm_g, m_hgrn_w_out, m_conv_w_in, m_conv_b_in, m_conv_dw_w, m_conv_dw_b, m_conv_ln_g, m_conv_ln_b, m_conv_w_out, m_conv_b_out, m_ffn_w_up, m_ffn_dw_w, m_ffn_dw_b, m_ffn_w_down, v_ada_w, v_ada_b, v_pre_mix_g, v_post_mix_g, v_pre_ffn_g, v_post_ffn_g, v_hgrn_w_in, v_hgrn_lb_logits, v_hgrn_gnorm_g, v_hgrn_w_out, v_conv_w_in, v_conv_b_in, v_conv_dw_w, v_conv_dw_b, v_conv_ln_g, v_conv_ln_b, v_conv_w_out, v_conv_b_out, v_ffn_w_up, v_ffn_dw_w, v_ffn_dw_b, v_ffn_w_down):
    P = dict(locals())
    _, T, D = x.shape
    me = _slot(_place())
    L = ada_w.shape[0]
    na = ada_w.shape[2]
    fs = ffn_w_up.shape[2]
    nci = conv_w_in.shape[2]
    nd = conv_dw_b.shape[1]
    NCH = D // LANES

    small_in = [c, conv_b_in, conv_dw_w, conv_dw_b, conv_ln_g, conv_ln_b, conv_b_out, ffn_dw_w]
    (packed,) = all_gather([_pack(small_in)], "gather_first")
    mine = dict(whin=_bf(hgrn_w_in[0]), whout=_bf(hgrn_w_out[0]), wcin=_bf(conv_w_in[0]), wcout=_bf(conv_w_out[0]),
                wup=[_bf(ffn_w_up[l]) for l in range(L)], wdn=[_bf(ffn_w_down[l]) for l in range(L)])
    sm = packed.reshape(N_DEV, -1)
    offs = [0]
    for a in small_in:
        offs.append(offs[-1] + a.size)
    piece = lambda k, shp: sm[:, offs[k]:offs[k + 1]].reshape((N_DEV,) + shp)
    c_all = piece(0, (D,))
    dw_nat = jnp.transpose(piece(2, (31, nd)), (1, 0, 2)).reshape(31, D)
    cw = jnp.pad(jnp.transpose(dw_nat.reshape(31, NCH, LANES), (1, 0, 2)), ((0, 0), (0, 1), (0, 0)))
    fdw = piece(7, (L, 3, fs))
    lb_soft = lb_softmax(hgrn_lb_logits, "lb_softmax")
    W = dict(
        lb=lb_soft[0:1], gg=hgrn_gnorm_g, cb_in=piece(1, (nci,)).reshape(1, N_DEV * nci), cw=cw,
        cdw_b=piece(3, (nd,)).reshape(1, D), ln_g=piece(4, (nd,)).reshape(1, D), ln_b=piece(5, (nd,)).reshape(1, D),
        cb_out=piece(6, (nd,)).reshape(1, D), fdw_w=[fdw[:, l] for l in range(L)],
        fdw_b=[ffn_dw_b[l].reshape(N_DEV, 1, fs) for l in range(L)])

    bias_mine = lax.dynamic_slice(ada_b, (0, me * na), (L, na)).reshape(L, 1, na)
    mod_cols = ada_fwd(c_all, ada_w, bias_mine, "ada_fwd")
    (mod_all,) = all_gather([mod_cols], "gather_mod")
    mod = jnp.transpose(lax.dynamic_index_in_dim(mod_all, me, axis=2, keepdims=False), (1, 0, 2)).reshape(L, N_DEV * na)

    gains = dict(pre_mix_g=pre_mix_g, post_mix_g=post_mix_g, pre_ffn_g=pre_ffn_g, post_ffn_g=post_ffn_g)
    loss_part, dx0, big, small = _local_step(x[0], loss_target[0], mod, gains, W, mine)
    loss = lax.psum((0.5 / D) * jnp.sum(loss_part), ("x", "y", "c"))

    parts = [small[k] for k in SMALL_ORDER]
    (sm_all,) = all_gather([_pack(parts)], "gather_small_grads")
    tot = _unpack(sum_slots(sm_all, "sum_small_grads").reshape(-1), [p.shape for p in parts])
    tot = dict(zip(SMALL_ORDER, tot))
    dmod_all = sm_all.reshape(N_DEV, -1)[:, :L * 6 * D].reshape(N_DEV, L, 6 * D)
    dm_mine = jnp.transpose(lax.dynamic_slice(dmod_all, (0, 0, me * na), (N_DEV, L, na)), (1, 0, 2))
    s0 = lb_soft[0:1]
    onehot = (lax.broadcasted_iota(jnp.int32, lb_soft.shape, 0) == 0).astype(F32)
    col = lambda a, n: lax.dynamic_slice_in_dim(a, me * n, n, axis=a.ndim - 1)
    G = {
        'ada_w': ada_bwd(c_all, dm_mine, "ada_bwd")[None],
        'ada_b': tot["dmod"].reshape(1, L, 6 * D),
        'pre_mix_g': tot["pre_mix_g"][None], 'post_mix_g': tot["post_mix_g"][None],
        'pre_ffn_g': tot["pre_ffn_g"][None], 'post_ffn_g': tot["post_ffn_g"][None],
        'hgrn_lb_logits': (tot["lb"] * s0 * (onehot - lb_soft))[None],
        'hgrn_gnorm_g': tot["gg"][None],
        'ffn_dw_b': tot["ffn_dw_b"].reshape(1, L, N_DEV * fs),
        'conv_b_in': col(tot["conv_b_in"], nci)[None],
        'conv_dw_w': col(tot["conv_dw_w"], nd)[None, None],
        'conv_dw_b': col(tot["conv_dw_b"], nd)[None], 'conv_ln_g': col(tot["conv_ln_g"], nd)[None],
        'conv_ln_b': col(tot["conv_ln_b"], nd)[None], 'conv_b_out': col(tot["conv_b_out"], nd)[None],
        'ffn_dw_w': lax.dynamic_index_in_dim(tot["ffn_dw_w"], me, axis=1, keepdims=False)[None],
    }

    G.update(big)
    res = {n: _apply_adamw(n, P[n], G[n], P["m_" + n], P["v_" + n]) for n in WEIGHTS}
    return (loss, dx0[None], *[res[n][0] for n in WEIGHTS], *[res[n][1] for n in WEIGHTS],
            *[res[n][2] for n in WEIGHTS], *[res[n][3] for n in WEIGHTS])
```

```python
import functools

import jax
import jax.numpy as jnp
from jax import lax
from jax.experimental import pallas as pl
from jax.experimental.pallas import tpu as pltpu

F32 = jnp.float32
BF16 = jnp.bfloat16
EPS = 1e-6
LANES = 128
N_DEV = 8
VMEM_LIMIT = 48 * 1024 * 1024
FFN_BWD_VMEM = 58 * 1024 * 1024
MM_ROWS = 1024
HG_CHUNK = 128
HG_SUB = 32
EXP_CLAMP = 80.0
CONV_HALO = 32
FFN_HALO = 8
CONV_ROWS = 32
ADAM_LR, ADAM_B1, ADAM_B2, ADAM_EPS, ADAM_WD, ADAM_STEP = 0.001, 0.9, 0.999, 1e-08, 0.01, 10
MESH = pl.DeviceIdType.MESH
ANY = pl.BlockSpec(memory_space=pl.ANY)


def _cp(*sem):
    return pltpu.CompilerParams(dimension_semantics=sem, vmem_limit_bytes=VMEM_LIMIT)


def _rt(n, t):
    t = min(n, t)
    assert n % t == 0, (n, t)
    return t


def _sig(x):
    return jax.nn.sigmoid(x)


def _nt(a, b):
    return lax.dot_general(a, b, (((1,), (1,)), ((), ())), preferred_element_type=F32)


def _tn(a, b):
    return lax.dot_general(a, b, (((0,), (0,)), ((), ())), preferred_element_type=F32)


def _nn(a, b):
    return jnp.dot(a, b, preferred_element_type=F32)


def _bf(x):
    return x.astype(BF16)


def _nn_st(a, b):
    return _nn(_bf(a), _bf(b))


def _tn_st(a, b):
    return _tn(_bf(a), _bf(b))


def _row(d):
    return pl.BlockSpec((1, d), lambda *_: (0, 0))


def prenorm_mod(x, g, sc, sh, name, ex=None):
    T, D = x.shape
    tm = _rt(T, 512)

    def body(x_ref, g_ref, sc_ref, sh_ref, h_ref):
        h_ref[...] = _next_input(x_ref[...], g_ref, sc_ref, sh_ref)

    tile = pl.BlockSpec((tm, D), lambda i: (i, 0))
    (h,), rider = _call(body, name, (T // tm,), [tile, _row(D), _row(D), _row(D)], [tile],
                        [jax.ShapeDtypeStruct((T, D), BF16)], [], ("parallel",), (x, g, sc, sh), ex)
    return (h, rider) if ex is not None else h


def resid_bwd(name, pre=None, loss=None, post=None):
    T, D = (pre[1] if pre is not None else loss[0]).shape
    tm = _rt(T, 512)
    n_src = 5 if pre is not None else 2
    n_in = n_src + (3 if post is not None else 0)

    def body(*refs):
        ins, outs = refs[:n_in], refs[n_in:]
        first = pl.program_id(0) == 0
        if pre is not None:
            dh_ref, x_ref, dr_ref, g_ref, sc_ref = ins[:5]
            dx_ref, da_ref, dsh_ref = outs[:3]

            @pl.when(first)
            def _():
                da_ref[...] = jnp.zeros_like(da_ref)
                dsh_ref[...] = jnp.zeros_like(dsh_ref)
            xv = x_ref[...]
            dh = dh_ref[...]
            r = lax.rsqrt(jnp.mean(xv * xv, axis=-1, keepdims=True) + EPS)
            n = xv * r
            da_ref[...] += jnp.sum(dh * n, axis=0, keepdims=True)
            dsh_ref[...] += jnp.sum(dh, axis=0, keepdims=True)
            dn = dh * (g_ref[...] * (1.0 + sc_ref[...]))
            dx = dr_ref[...] + r * (dn - n * jnp.mean(dn * n, axis=-1, keepdims=True))
            dx_ref[...] = dx
            rest = outs[3:]
        else:
            x_ref, t_ref = ins[:2]
            dx_ref, part_ref = outs[:2]

            @pl.when(first)
            def _():
                part_ref[...] = jnp.zeros_like(part_ref)
            e = x_ref[...] - t_ref[...]
            dx = e * (1.0 / D)
            dx_ref[...] = dx
            e2 = (e * e).reshape(tm // 8, 8, D).sum(axis=0)
            acc = e2[:, 0:LANES]
            for j in range(1, D // LANES):
                acc = acc + e2[:, j * LANES:(j + 1) * LANES]
            part_ref[...] += acc
            rest = outs[2:]
        if post is not None:
            y_ref, gt_ref, pg_ref = ins[n_src:]
            dy_ref, dgate_ref, dpg_ref, dys_ref = rest

            @pl.when(first)
            def _():
                dgate_ref[...] = jnp.zeros_like(dgate_ref)
                dpg_ref[...] = jnp.zeros_like(dpg_ref)
                dys_ref[...] = jnp.zeros_like(dys_ref)
            yv = y_ref[...]
            ry = lax.rsqrt(jnp.mean(yv * yv, axis=-1, keepdims=True) + EPS)
            ny = yv * ry
            s = jnp.sum(dx * ny, axis=0, keepdims=True)
            dgate_ref[...] += s * pg_ref[...]
            dpg_ref[...] += s * gt_ref[...]
            dny = dx * (gt_ref[...] * pg_ref[...])
            dy = ry * (dny - ny * jnp.mean(dny * ny, axis=-1, keepdims=True))
            dy_ref[...] = _bf(dy)
            dys_ref[...] += jnp.sum(dy, axis=0, keepdims=True)

    tile = pl.BlockSpec((tm, D), lambda i: (i, 0))
    vec = jax.ShapeDtypeStruct((1, D), F32)
    big = jax.ShapeDtypeStruct((T, D), F32)
    if pre is not None:
        operands, in_specs = list(pre), [tile, tile, tile, _row(D), _row(D)]
        out_specs, out_shape = [tile, _row(D), _row(D)], [big, vec, vec]
    else:
        operands, in_specs = list(loss), [tile, tile]
        out_specs, out_shape = [tile, pl.BlockSpec((8, LANES), lambda i: (0, 0))], [big, jax.ShapeDtypeStruct((8, LANES), F32)]
    if post is not None:
        operands, in_specs = operands + list(post), in_specs + [tile, _row(D), _row(D)]
        out_specs = out_specs + [tile, _row(D), _row(D), _row(D)]
        out_shape = out_shape + [jax.ShapeDtypeStruct((T, D), BF16), vec, vec, vec]
    return pl.pallas_call(body, name=name, grid=(T // tm,), in_specs=in_specs, out_specs=out_specs, out_shape=out_shape,
                          compiler_params=_cp("arbitrary"))(*operands)


def _call(body, name, grid, in_specs, out_specs, out_shape, scratch, sems, operands, ex=None, vmem=VMEM_LIMIT):
    if ex is not None:
        def first():
            return functools.reduce(lambda p, q: p & q, [pl.program_id(k) == 0 for k in range(len(grid))])

        def last():
            return functools.reduce(lambda p, q: p & q, [pl.program_id(k) == grid[k] - 1 for k in range(len(grid))])
        body = _ride(body, len(in_specs), len(out_specs), ex, first, last)
        in_specs = in_specs + [ANY] * len(ex.ins)
        out_specs = out_specs + [ANY] * len(ex.out_shapes)
        out_shape = out_shape + ex.out_shapes
        scratch = scratch + ex.scratch
        operands = list(operands) + ex.ins
        sems = ("arbitrary",) * len(grid)
    res = pl.pallas_call(body, name=name, grid=grid, in_specs=in_specs, out_specs=out_specs, out_shape=out_shape,
                         scratch_shapes=scratch,
                         compiler_params=pltpu.CompilerParams(dimension_semantics=sems, vmem_limit_bytes=vmem))(*operands)
    n_own = len(res) - (len(ex.out_shapes) if ex is not None else 0)
    return res[:n_own], res[n_own:]


def mm_nn(a, bg, bias, name, ex=None):
    M, K = a.shape
    G, _, n = bg.shape
    tm = _rt(M, MM_ROWS)

    def body(a_ref, b_ref, bias_ref, o_ref):
        o_ref[...] = _nn(a_ref[...], b_ref[...]) + bias_ref[...]

    (out,), rider = _call(
        body, name, (M // tm, G),
        [pl.BlockSpec((tm, K), lambda i, j: (i, 0)), pl.BlockSpec((None, K, n), lambda i, j: (j, 0, 0)),
         pl.BlockSpec((1, n), lambda i, j: (0, j))],
        [pl.BlockSpec((tm, n), lambda i, j: (i, j))], [jax.ShapeDtypeStruct((M, G * n), F32)], [],
        ("parallel", "arbitrary"), (a, bg, bias), ex)
    return (out, rider) if ex is not None else out


def _next_input(xo, g_ref, sc_ref, sh_ref):
    r = lax.rsqrt(jnp.mean(xo * xo, axis=-1, keepdims=True) + EPS)
    return _bf((xo * r) * g_ref[...] * (1.0 + sc_ref[...]) + sh_ref[...])


def mm_post(a, b, bias, x, gate, pg, nxt, name):
    T, K = a.shape
    D = b.shape[1]
    tm = _rt(T, 512)

    def body(a_ref, b_ref, bias_ref, x_ref, gt_ref, pg_ref, g_ref, sc_ref, sh_ref, y_ref, xo_ref, h_ref):
        y = _nn(a_ref[...], b_ref[...]) + bias_ref[...]
        y_ref[...] = y
        r = lax.rsqrt(jnp.mean(y * y, axis=-1, keepdims=True) + EPS)
        xo = x_ref[...] + gt_ref[...] * ((y * r) * pg_ref[...])
        xo_ref[...] = xo
        h_ref[...] = _next_input(xo, g_ref, sc_ref, sh_ref)

    tile = pl.BlockSpec((tm, D), lambda i: (i, 0))
    out = jax.ShapeDtypeStruct((T, D), F32)
    return pl.pallas_call(
        body, name=name, grid=(T // tm,),
        in_specs=[pl.BlockSpec((tm, K), lambda i: (i, 0)), pl.BlockSpec((K, D), lambda i: (0, 0)), _row(D), tile]
        + [_row(D)] * 5,
        out_specs=[tile, tile, tile], out_shape=[out, out, jax.ShapeDtypeStruct((T, D), BF16)],
        compiler_params=_cp("parallel"))(a, b, bias, x, gate, pg, *nxt)


def mm_nt(dy, w, name, ex=None):
    T, N = dy.shape
    K = w.shape[0]
    tm = _rt(T, 512)

    def body(dy_ref, w_ref, o_ref):
        o_ref[...] = _nt(_bf(dy_ref[...]), w_ref[...])

    (out,), rider = _call(
        body, name, (T // tm,), [pl.BlockSpec((tm, N), lambda i: (i, 0)), pl.BlockSpec((K, N), lambda i: (0, 0))],
        [pl.BlockSpec((tm, K), lambda i: (i, 0))], [jax.ShapeDtypeStruct((T, K), F32)], [], ("parallel",),
        (dy, w), ex)
    return (out, rider) if ex is not None else out


def mm_tn(xm, gm, name, x_natural, g_natural, kx, n):
    if x_natural:
        T, Gx = xm.shape[0], xm.shape[1] // kx
    else:
        Gx, T = xm.shape[0], xm.shape[1]
    tt = _rt(T, 4 * MM_ROWS)
    if x_natural:
        xspec = pl.BlockSpec((tt, kx), lambda g, k, t: (t, k))
    else:
        xspec = pl.BlockSpec((None, tt, kx), lambda g, k, t: (k, t, 0))
    if g_natural:
        Gg = gm.shape[1] // n
        gspec = pl.BlockSpec((tt, n), lambda g, k, t: (t, g))
    else:
        Gg = gm.shape[0]
        gspec = pl.BlockSpec((None, tt, n), lambda g, k, t: (g, t, 0))

    NT = T // tt

    def body(x_ref, g_ref, o_ref, acc):
        t = pl.program_id(2)

        @pl.when(t == 0)
        def _():
            acc[...] = jnp.zeros_like(acc)
        acc[...] += _tn(_bf(x_ref[...]), _bf(g_ref[...]))

        @pl.when(t == NT - 1)
        def _():
            o_ref[...] = acc[...].astype(o_ref.dtype)

    return pl.pallas_call(
        body, name=name, grid=(Gg, Gx, NT), in_specs=[xspec, gspec],
        out_specs=pl.BlockSpec((None, kx, n), lambda g, k, t: (g, k, 0)),
        out_shape=jax.ShapeDtypeStruct((Gg, Gx * kx, n), BF16), scratch_shapes=[pltpu.VMEM((kx, n), F32)],
        compiler_params=_cp("parallel", "parallel", "arbitrary"))(xm, gm)


def _split3(x):
    h = _bf(x)
    r = x - h.astype(F32)
    m = _bf(r)
    lo = _bf(r - m.astype(F32))
    return h, m, lo


def _tri_mm(tri, x):
    h, m, lo = _split3(x)
    return _nn(tri, lo) + _nn(tri, m) + _nn(tri, h)


def _hgrn_gates(qr, fz, lb):
    sq = _sig(qr)
    q = qr * sq
    sig = _sig(fz)
    f = lb + (1.0 - lb) * sig
    k = 1.0 - f
    return sq, q, sig, f, k, jnp.log(f)


def hgrn_fwd(proj, lb, gg, name, ex=None):
    T, D4 = proj.shape
    D = D4 // 4
    H = D // LANES
    C = _rt(T, HG_CHUNK)
    SB = min(HG_SUB, C)
    NC = T // C

    def body(q_ref, f_ref, v_ref, g_ref, lb_ref, gg_ref, o_ref, og_ref, st_ref, ST, bex):
        @pl.when(pl.program_id(0) == 0)
        def _():
            ST[...] = jnp.zeros_like(ST)
        r_i = lax.broadcasted_iota(jnp.int32, (C, C), 0)
        c_i = lax.broadcasted_iota(jnp.int32, (C, C), 1)
        low = _bf((r_i >= c_i).astype(F32))
        rs = lax.broadcasted_iota(jnp.int32, (SB, C), 0)
        cs = lax.broadcasted_iota(jnp.int32, (SB, C), 1)
        for hd in range(H):
            sl = slice(hd * LANES, (hd + 1) * LANES)
            _, q, _, _, k, lf = _hgrn_gates(q_ref[:, sl], f_ref[:, sl], lb_ref[:, sl])
            v = v_ref[:, sl]
            b = _tri_mm(low, lf)
            bex[hd] = b - lf
            bend = jnp.sum(lf, axis=0, keepdims=True)
            blocks = []
            for I in range(C // SB):
                p = bex[hd, pl.ds(SB * I + SB // 2, 1), :]
                rows = slice(SB * I, SB * (I + 1))
                qI = _bf(q[rows] * jnp.exp(jnp.minimum(b[rows] - p, EXP_CLAMP)))
                kI = _bf(k * jnp.exp(jnp.minimum(p - b, EXP_CLAMP)))
                blocks.append(jnp.where(rs + SB * I >= cs, _nt(qI, kI), 0.0))
            A = jnp.concatenate(blocks, axis=0)
            st0 = ST[hd]
            st_ref[0, hd] = st0
            o = _nn(_bf(A), _bf(v)) + _nt(_bf(q * jnp.exp(b)), _bf(st0))
            ST[hd] = st0 * jnp.exp(bend) + _tn_st(v, k * jnp.exp(bend - b))
            o_ref[:, sl] = o
            r = lax.rsqrt(jnp.mean(o * o, axis=-1, keepdims=True) + EPS)
            gr = g_ref[:, sl]
            og_ref[:, sl] = _bf((o * r) * gg_ref[...] * (gr * _sig(gr)))

    col = lambda j: pl.BlockSpec((C, D), lambda c, j=j: (c, j))
    tile = pl.BlockSpec((C, D), lambda c: (c, 0))
    own, rider = _call(
        body, name, (NC,), [col(0), col(1), col(2), col(3), _row(D), _row(LANES)],
        [tile, tile, pl.BlockSpec((1, H, LANES, LANES), lambda c: (c, 0, 0, 0))],
        [jax.ShapeDtypeStruct((T, D), F32), jax.ShapeDtypeStruct((T, D), BF16),
         jax.ShapeDtypeStruct((NC, H, LANES, LANES), F32)],
        [pltpu.VMEM((H, LANES, LANES), F32), pltpu.VMEM((H, C, LANES), F32)], ("arbitrary",),
        (proj, proj, proj, proj, lb, gg), ex)
    return (*own, rider) if ex is not None else own


def hgrn_bwd(proj, o, dog, st, lb, gg, name, ex=None):
    T, D4 = proj.shape
    D = D4 // 4
    H = D // LANES
    C = _rt(T, HG_CHUNK)
    SB = min(HG_SUB, C)
    NC = T // C

    def body(q_ref, f_ref, v_ref, g_ref, o_ref, dog_ref, st_ref, lb_ref, gg_ref, dp_ref, dlb_ref, dgg_ref,
             DST, bex):
        @pl.when(pl.program_id(0) == 0)
        def _():
            DST[...] = jnp.zeros_like(DST)
            dlb_ref[...] = jnp.zeros_like(dlb_ref)
            dgg_ref[...] = jnp.zeros_like(dgg_ref)
        r_i = lax.broadcasted_iota(jnp.int32, (C, C), 0)
        c_i = lax.broadcasted_iota(jnp.int32, (C, C), 1)
        causal = r_i >= c_i
        low = _bf(causal.astype(F32))
        upp = _bf((r_i <= c_i).astype(F32))
        rs = lax.broadcasted_iota(jnp.int32, (SB, C), 0)
        cs = lax.broadcasted_iota(jnp.int32, (SB, C), 1)
        ggv = gg_ref[...]
        for hd in range(H):
            sl = slice(hd * LANES, (hd + 1) * LANES)
            qr = q_ref[:, sl]
            lbv = lb_ref[:, sl]
            sq, q, sig, f, k, lf = _hgrn_gates(qr, f_ref[:, sl], lbv)
            v = v_ref[:, sl]
            oh = o_ref[:, sl]
            r = lax.rsqrt(jnp.mean(oh * oh, axis=-1, keepdims=True) + EPS)
            n = oh * r
            gr = g_ref[:, sl]
            sg = _sig(gr)
            dgo = dog_ref[:, sl]
            dgr = dgo * (n * ggv) * (sg * (1.0 + gr * (1.0 - sg)))
            don = dgo * (gr * sg)
            dgg_ref[...] += jnp.sum(don * n, axis=0, keepdims=True)
            dn = don * ggv
            do = r * (dn - n * jnp.mean(dn * n, axis=-1, keepdims=True))
            b = _tri_mm(low, lf)
            bex[hd] = b - lf
            bend = jnp.sum(lf, axis=0, keepdims=True)
            dob = _bf(do)
            dA = _bf(jnp.where(causal, _nt(dob, _bf(v)), 0.0))
            blocks, dqs, gqs = [], [], []
            dk = jnp.zeros((C, LANES), F32)
            gk = jnp.zeros((C, LANES), F32)
            for I in range(C // SB):
                p = bex[hd, pl.ds(SB * I + SB // 2, 1), :]
                rows = slice(SB * I, SB * (I + 1))
                eq = jnp.exp(jnp.minimum(b[rows] - p, EXP_CLAMP))
                qI = _bf(q[rows] * eq)
                ek = jnp.exp(jnp.minimum(p - b, EXP_CLAMP))
                kI = _bf(k * ek)
                blocks.append(jnp.where(rs + SB * I >= cs, _nt(qI, kI), 0.0))
                dqI = _nn(dA[rows], kI)
                dkI = _tn(dA[rows], qI)
                dqs.append(dqI * eq)
                gqs.append(qI.astype(F32) * dqI)
                dk = dk + dkI * ek
                gk = gk + kI.astype(F32) * dkI
            A = jnp.concatenate(blocks, axis=0)
            dst = DST[hd]
            st0 = st_ref[0, hd]
            eb = jnp.exp(b)
            ee = jnp.exp(bend - b)
            ebend = jnp.exp(bend)
            dv = _tn(_bf(A), dob) + _nt(_bf(k * ee), _bf(dst))
            dk_state = ee * _nn_st(v, dst)
            dq_state = eb * _nn_st(do, st0)
            dq = jnp.concatenate(dqs, axis=0) + dq_state
            dk = dk + dk_state
            DST[hd] = dst * ebend + _tn_st(do, q * eb)
            later = jnp.sum(k * dk_state, axis=0, keepdims=True) + ebend * jnp.sum(st0 * dst, axis=0, keepdims=True)
            pairs = jnp.concatenate(gqs, axis=0) - gk + (q * dq_state - k * dk_state)
            dlf = _tri_mm(upp, pairs) + later
            df = dlf / f - dk
            dlb_ref[:, sl] += jnp.sum(df * (1.0 - sig), axis=0, keepdims=True)
            dp_ref[:, sl] = _bf(dq * (sq * (1.0 + qr * (1.0 - sq))))
            dp_ref[:, D + hd * LANES:D + (hd + 1) * LANES] = _bf(df * (1.0 - lbv) * (sig * (1.0 - sig)))
            dp_ref[:, 2 * D + hd * LANES:2 * D + (hd + 1) * LANES] = _bf(dv)
            dp_ref[:, 3 * D + hd * LANES:3 * D + (hd + 1) * LANES] = _bf(dgr)

    col = lambda j: pl.BlockSpec((C, D), lambda c, j=j: (NC - 1 - c, j))
    tile = pl.BlockSpec((C, D), lambda c: (NC - 1 - c, 0))
    own, rider = _call(
        body, name, (NC,),
        [col(0), col(1), col(2), col(3), tile, tile,
         pl.BlockSpec((1, H, LANES, LANES), lambda c: (NC - 1 - c, 0, 0, 0)), _row(D), _row(LANES)],
        [pl.BlockSpec((C, D4), lambda c: (NC - 1 - c, 0)), _row(D), _row(LANES)],
        [jax.ShapeDtypeStruct((T, D4), BF16), jax.ShapeDtypeStruct((1, D), F32), jax.ShapeDtypeStruct((1, LANES), F32)],
        [pltpu.VMEM((H, LANES, LANES), F32), pltpu.VMEM((H, C, LANES), F32)],
        ("arbitrary",), (proj, proj, proj, proj, o, dog, st, lb, gg), ex)
    return (*own, rider) if ex is not None else own


def _prev_blk(tm, hb):
    return lambda i: (jnp.maximum(i * (tm // hb) - 1, 0), 0)


def _next_blk(tm, hb, T):
    return lambda i: (jnp.minimum((i + 1) * (tm // hb), T // hb - 1), 0)


def _glu_to_ext(uc_ref, ucp_ref, ext, D, first):
    def glu(u):
        return u[:, :D] * _sig(u[:, D:])
    gh = jnp.where(first, 0.0, glu(ucp_ref[...]))
    gm = glu(uc_ref[...])
    for c in range(D // LANES):
        ext[c, 0:CONV_HALO, :] = gh[:, c * LANES:(c + 1) * LANES]
        ext[c, CONV_HALO:, :] = gm[:, c * LANES:(c + 1) * LANES]


def conv_mid_fwd(uc, cw, cb, lg, lbias, name):
    T, D2 = uc.shape
    D = D2 // 2
    NCH = D // LANES
    W = 31
    tm = _rt(T, 256)
    RS = min(CONV_ROWS, tm)

    def body(uc_ref, ucp_ref, cw_ref, cb_ref, lg_ref, lb_ref, cv_ref, va_ref, ext, outs):
        _glu_to_ext(uc_ref, ucp_ref, ext, D, pl.program_id(0) == 0)

        def chunk(c, carry):
            for rb in range(tm // RS):
                acc = jnp.zeros((RS, LANES), F32)
                for kk in range(W):
                    acc = acc + cw_ref[c, pl.ds(kk, 1), :] * ext[c, pl.ds(rb * RS + CONV_HALO - (W - 1) + kk, RS), :]
                outs[c, pl.ds(rb * RS, RS), :] = acc
            return carry
        lax.fori_loop(0, NCH, chunk, 0)
        cv = jnp.concatenate([outs[c] for c in range(NCH)], axis=1) + cb_ref[...]
        cv_ref[...] = cv
        mu = jnp.mean(cv, axis=-1, keepdims=True)
        xc = cv - mu
        rstd = lax.rsqrt(jnp.mean(xc * xc, axis=-1, keepdims=True) + EPS)
        l = xc * rstd * lg_ref[...] + lb_ref[...]
        va_ref[...] = _bf(l * _sig(l))

    tile = pl.BlockSpec((tm, D), lambda i: (i, 0))
    return pl.pallas_call(
        body, name=name, grid=(T // tm,),
        in_specs=[pl.BlockSpec((tm, D2), lambda i: (i, 0)), pl.BlockSpec((CONV_HALO, D2), _prev_blk(tm, CONV_HALO)),
                  pl.BlockSpec((NCH, 32, LANES), lambda i: (0, 0, 0)), _row(D), _row(D), _row(D)],
        out_specs=[tile, tile],
        out_shape=[jax.ShapeDtypeStruct((T, D), F32), jax.ShapeDtypeStruct((T, D), BF16)],
        scratch_shapes=[pltpu.VMEM((NCH, tm + CONV_HALO, LANES), F32), pltpu.VMEM((NCH, tm, LANES), F32)],
        compiler_params=_cp("parallel"))(uc, uc, cw, cb, lg, lbias)


def conv_bwd_a(dva, cv, uc, lg, lbias, name):
    T, D2 = uc.shape
    D = D2 // 2
    NCH = D // LANES
    W = 31
    tm = _rt(T, 256)
    RS = min(CONV_ROWS, tm)

    def body(dva_ref, cv_ref, uc_ref, ucp_ref, lg_ref, lb_ref, dcv_ref, dlg_ref, dlb_ref, taps_ref, ext, dcs):
        @pl.when(pl.program_id(0) == 0)
        def _():
            dlg_ref[...] = jnp.zeros_like(dlg_ref)
            dlb_ref[...] = jnp.zeros_like(dlb_ref)
            taps_ref[...] = jnp.zeros_like(taps_ref)
        _glu_to_ext(uc_ref, ucp_ref, ext, D, pl.program_id(0) == 0)
        cv = cv_ref[...]
        mu = jnp.mean(cv, axis=-1, keepdims=True)
        xc = cv - mu
        rstd = lax.rsqrt(jnp.mean(xc * xc, axis=-1, keepdims=True) + EPS)
        yn = xc * rstd
        l = yn * lg_ref[...] + lb_ref[...]
        s = _sig(l)
        dl = dva_ref[...] * (s * (1.0 + l * (1.0 - s)))
        dlg_ref[...] += jnp.sum(dl * yn, axis=0, keepdims=True)
        dlb_ref[...] += jnp.sum(dl, axis=0, keepdims=True)
        dyn = dl * lg_ref[...]
        dcv = rstd * (dyn - jnp.mean(dyn, axis=-1, keepdims=True) - yn * jnp.mean(dyn * yn, axis=-1, keepdims=True))
        dcv_ref[...] = dcv
        for c in range(NCH):
            dcs[c] = dcv[:, c * LANES:(c + 1) * LANES]

        def fold(p):
            return p.reshape(RS // 8, 8, LANES).sum(axis=0)

        def chunk(c, carry):
            for kk in range(W):
                tot = jnp.zeros((8, LANES), F32)
                for rb in range(tm // RS):
                    tot = tot + fold(dcs[c, pl.ds(rb * RS, RS), :]
                                     * ext[c, pl.ds(rb * RS + CONV_HALO - (W - 1) + kk, RS), :])
                taps_ref[c, pl.ds(8 * kk, 8), :] += tot
            tot = jnp.zeros((8, LANES), F32)
            for rb in range(tm // RS):
                tot = tot + fold(dcs[c, pl.ds(rb * RS, RS), :])
            taps_ref[c, pl.ds(8 * W, 8), :] += tot
            return carry
        lax.fori_loop(0, NCH, chunk, 0)

    tile = pl.BlockSpec((tm, D), lambda i: (i, 0))
    vec = jax.ShapeDtypeStruct((1, D), F32)
    return pl.pallas_call(
        body, name=name, grid=(T // tm,),
        in_specs=[tile, tile, pl.BlockSpec((tm, D2), lambda i: (i, 0)),
                  pl.BlockSpec((CONV_HALO, D2), _prev_blk(tm, CONV_HALO)), _row(D), _row(D)],
        out_specs=[tile, _row(D), _row(D), pl.BlockSpec((NCH, 256, LANES), lambda i: (0, 0, 0))],
        out_shape=[jax.ShapeDtypeStruct((T, D), F32), vec, vec, jax.ShapeDtypeStruct((NCH, 256, LANES), F32)],
        scratch_shapes=[pltpu.VMEM((NCH, tm + CONV_HALO, LANES), F32), pltpu.VMEM((NCH, tm, LANES), F32)],
        compiler_params=_cp("arbitrary"))(dva, cv, uc, uc, lg, lbias)


def conv_bwd_b(dcv, uc, cw, name):
    T, D2 = uc.shape
    D = D2 // 2
    NCH = D // LANES
    W = 31
    tm = _rt(T, 256)
    RS = min(CONV_ROWS, tm)
    NT = T // tm

    def body(dcv_ref, dcn_ref, uc_ref, cw_ref, du_ref, db_ref, ext, outs):
        i = pl.program_id(0)

        @pl.when(i == 0)
        def _():
            db_ref[...] = jnp.zeros_like(db_ref)
        dm = dcv_ref[...]
        dn = jnp.where(i == NT - 1, 0.0, dcn_ref[...])
        for c in range(NCH):
            ext[c, 0:tm, :] = dm[:, c * LANES:(c + 1) * LANES]
            ext[c, tm:, :] = dn[:, c * LANES:(c + 1) * LANES]

        def chunk(c, carry):
            for rb in range(tm // RS):
                acc = jnp.zeros((RS, LANES), F32)
                for kk in range(W):
                    acc = acc + cw_ref[c, pl.ds(kk, 1), :] * ext[c, pl.ds(rb * RS + (W - 1) - kk, RS), :]
                outs[c, pl.ds(rb * RS, RS), :] = acc
            return carry
        lax.fori_loop(0, NCH, chunk, 0)
        dglu = jnp.concatenate([outs[c] for c in range(NCH)], axis=1)
        u = uc_ref[...]
        a = u[:, :D]
        sg = _sig(u[:, D:])
        da = dglu * sg
        dg = dglu * a * (sg * (1.0 - sg))
        du_ref[:, :D] = _bf(da)
        du_ref[:, D:] = _bf(dg)
        db_ref[:, :D] += jnp.sum(da, axis=0, keepdims=True)
        db_ref[:, D:] += jnp.sum(dg, axis=0, keepdims=True)

    tile = pl.BlockSpec((tm, D), lambda i: (i, 0))
    wide = pl.BlockSpec((tm, D2), lambda i: (i, 0))
    return pl.pallas_call(
        body, name=name, grid=(NT,),
        in_specs=[tile, pl.BlockSpec((CONV_HALO, D), _next_blk(tm, CONV_HALO, T)), wide,
                  pl.BlockSpec((NCH, 32, LANES), lambda i: (0, 0, 0))],
        out_specs=[wide, _row(D2)],
        out_shape=[jax.ShapeDtypeStruct((T, D2), BF16), jax.ShapeDtypeStruct((1, D2), F32)],
        scratch_shapes=[pltpu.VMEM((NCH, tm + CONV_HALO, LANES), F32), pltpu.VMEM((NCH, tm, LANES), F32)],
        compiler_params=_cp("arbitrary"))(dcv, dcv, uc, cw)


def _shift_rows(x, halo, k, back):
    rows, n = x.shape
    x3, h3 = x.reshape(rows // 8, 8, n), halo.reshape(1, 8, n)
    sub = lax.broadcasted_iota(jnp.int32, (1, 8, n), 1)
    if back:
        r = pltpu.roll(jnp.concatenate([h3, x3], axis=0), k, 1)
        y = jnp.where(sub < k, r[:-1], r[1:])
    else:
        r = pltpu.roll(jnp.concatenate([x3, h3], axis=0), 8 - k, 1)
        y = jnp.where(sub < 8 - k, r[:-1], r[1:])
    return y.reshape(rows, n)


def _conv3(u, uh, dw_ref, bias_ref):
    return (dw_ref[pl.ds(0, 1), :] * _shift_rows(u, uh, 2, True) + dw_ref[pl.ds(1, 1), :] * _shift_rows(u, uh, 1, True)
            + dw_ref[pl.ds(2, 1), :] * u + bias_ref[...])


def ffn_fwd(h, wup, dww, dwb, wdn, x, gate, pg, nxt, name):
    T, D = h.shape
    fs = wup.shape[2]
    NJ = wup.shape[0] // 2
    tm = _rt(T, 512)
    n_nxt = 0 if nxt is None else 3

    def body(h_ref, hp_ref, wa_ref, wb_ref, dwa_ref, dwb_ref, ba_ref, bb_ref, wd_ref, x_ref, gt_ref, pg_ref, *rest):
        nx, (u_ref, uc_ref, y_ref, xo_ref), more = rest[:n_nxt], rest[n_nxt:n_nxt + 4], rest[n_nxt + 4:]
        acc = more[-1]
        i, j = pl.program_id(0), pl.program_id(1)

        @pl.when(j == 0)
        def _():
            acc[...] = jnp.zeros_like(acc)
        hb = h_ref[...]
        hp = hp_ref[...]

        def half(s, w_ref, dw_ref, b_ref):
            u = _nn(hb, w_ref[...])
            uh = jnp.where(i == 0, 0.0, _nn(hp, w_ref[...]))
            u_ref[s] = u
            uc = _conv3(u, uh, dw_ref, b_ref)
            uc_ref[s] = uc
            return uc
        a = half(0, wa_ref, dwa_ref, ba_ref)
        b = half(1, wb_ref, dwb_ref, bb_ref)
        acc[...] += _nn(_bf(a * _sig(a) * b), wd_ref[...])

        @pl.when(j == NJ - 1)
        def _():
            y = acc[...]
            y_ref[...] = y
            r = lax.rsqrt(jnp.mean(y * y, axis=-1, keepdims=True) + EPS)
            xo = x_ref[...] + gt_ref[...] * ((y * r) * pg_ref[...])
            xo_ref[...] = xo
            if nxt is not None:
                more[0][...] = _next_input(xo, *nx)

    tile = pl.BlockSpec((tm, D), lambda i, j: (i, 0))
    shard = lambda off, r: pl.BlockSpec((None, r, fs), lambda i, j: (j + off, 0, 0))
    ush = pl.BlockSpec((2, None, tm, fs), lambda i, j: (0, j, i, 0))
    vec = pl.BlockSpec((1, D), lambda i, j: (0, 0))
    return pl.pallas_call(
        body, name=name, grid=(T // tm, NJ),
        in_specs=[tile, pl.BlockSpec((FFN_HALO, D), lambda i, j: (jnp.maximum(i * (tm // FFN_HALO) - 1, 0), 0)),
                  shard(0, D), shard(NJ, D), shard(0, 3), shard(NJ, 3), shard(0, 1), shard(NJ, 1),
                  pl.BlockSpec((None, fs, D), lambda i, j: (j, 0, 0)), tile, vec, vec] + [vec] * n_nxt,
        out_specs=[ush, ush, tile, tile] + [tile] * (n_nxt > 0),
        out_shape=[jax.ShapeDtypeStruct((2, NJ, T, fs), F32), jax.ShapeDtypeStruct((2, NJ, T, fs), F32),
                   jax.ShapeDtypeStruct((T, D), F32), jax.ShapeDtypeStruct((T, D), F32)]
        + [jax.ShapeDtypeStruct((T, D), BF16)] * (n_nxt > 0),
        scratch_shapes=[pltpu.VMEM((tm, D), F32)],
        compiler_params=_cp("parallel", "arbitrary"))(h, h, wup, wup, dww, dww, dwb, dwb, wdn, x, gate, pg,
                                                      *(nxt or ()))


def ffn_bwd_a(dy, uc, wdn, name):
    _, NJ, T, fs = uc.shape
    D = dy.shape[1]
    tm = _rt(T, 512)

    NT = T // tm

    def body(dy_ref, uc_ref, wd_ref, duc_ref, dwd_ref, acc):
        i = pl.program_id(1)

        @pl.when(i == 0)
        def _():
            acc[...] = jnp.zeros_like(acc)
        dyb = _bf(dy_ref[...])
        dz = _nt(dyb, wd_ref[...])
        a = uc_ref[0]
        b = uc_ref[1]
        sa = _sig(a)
        asa = a * sa
        acc[...] += _tn(_bf(asa * b), dyb)
        duc_ref[0] = _bf(dz * b * (sa + asa * (1.0 - sa)))
        duc_ref[1] = _bf(dz * asa)

        @pl.when(i == NT - 1)
        def _():
            dwd_ref[...] = _bf(acc[...])

    ush = pl.BlockSpec((2, None, tm, fs), lambda j, i: (0, j, i, 0))
    return pl.pallas_call(
        body, name=name, grid=(NJ, NT),
        in_specs=[pl.BlockSpec((tm, D), lambda j, i: (i, 0)), ush, pl.BlockSpec((None, fs, D), lambda j, i: (j, 0, 0))],
        out_specs=[ush, pl.BlockSpec((fs, D), lambda j, i: (j, 0))],
        out_shape=[jax.ShapeDtypeStruct((2, NJ, T, fs), BF16), jax.ShapeDtypeStruct((NJ * fs, D), BF16)],
        scratch_shapes=[pltpu.VMEM((fs, D), F32)],
        compiler_params=_cp("parallel", "arbitrary"))(dy, uc, wdn)


def ffn_bwd_b(duc, u, h, dww, wup, name, ex=None):
    NS, T, fs = duc.shape
    D = wup.shape[1]
    tm = _rt(T, 512)
    NT = T // tm

    def body(d_ref, dn_ref, u_ref, h_ref, dw_ref, w_ref, dh_ref, g_ref, dwup_ref, acc, accw, stage):
        i, s = pl.program_id(0), pl.program_id(1)

        @pl.when((i == 0) & (s == 0))
        def _():
            g_ref[...] = jnp.zeros_like(g_ref)

        @pl.when(s == 0)
        def _():
            acc[...] = jnp.zeros_like(acc)
        d = d_ref[...].astype(F32)
        dn = jnp.where(i == NT - 1, 0.0, dn_ref[...].astype(F32)[0:FFN_HALO])
        d1 = _shift_rows(d, dn, 1, False)
        d2 = _shift_rows(d, dn, 2, False)
        du = _bf(dw_ref[pl.ds(2, 1), :] * d + dw_ref[pl.ds(1, 1), :] * d1 + dw_ref[pl.ds(0, 1), :] * d2)
        acc[...] += _nt(du, w_ref[...])
        dw_part = _tn(h_ref[...], du)

        @pl.when(i == 0)
        def _():
            accw[s] = dw_part

        @pl.when(i > 0)
        def _():
            accw[s] += dw_part

        @pl.when(i == NT - 1)
        def _():
            stage[...] = _bf(accw[s])
            pltpu.sync_copy(stage, dwup_ref.at[s])
        u = u_ref[...]
        g_ref[s, pl.ds(0, 1), :] += jnp.sum(d2 * u, axis=0, keepdims=True)
        g_ref[s, pl.ds(1, 1), :] += jnp.sum(d1 * u, axis=0, keepdims=True)
        g_ref[s, pl.ds(2, 1), :] += jnp.sum(d * u, axis=0, keepdims=True)
        g_ref[s, pl.ds(3, 1), :] += jnp.sum(d, axis=0, keepdims=True)

        @pl.when(s == NS - 1)
        def _():
            dh_ref[...] = acc[...]

    ush = pl.BlockSpec((None, tm, fs), lambda i, s: (s, i, 0))
    unext = pl.BlockSpec((None, 2 * FFN_HALO, fs),
                         lambda i, s: (s, jnp.minimum((i + 1) * (tm // (2 * FFN_HALO)), T // (2 * FFN_HALO) - 1), 0))
    tile = pl.BlockSpec((tm, D), lambda i, s: (i, 0))
    own, rider = _call(
        body, name, (NT, NS),
        [ush, unext, ush, tile, pl.BlockSpec((None, 3, fs), lambda i, s: (s, 0, 0)),
         pl.BlockSpec((None, D, fs), lambda i, s: (s, 0, 0))],
        [tile, pl.BlockSpec((NS, 8, fs), lambda i, s: (0, 0, 0)), ANY],
        [jax.ShapeDtypeStruct((T, D), F32), jax.ShapeDtypeStruct((NS, 8, fs), F32),
         jax.ShapeDtypeStruct((NS, D, fs), BF16)],
        [pltpu.VMEM((tm, D), F32), pltpu.VMEM((NS, D, fs), F32), pltpu.VMEM((D, fs), BF16)],
        ("arbitrary", "arbitrary"), (duc, duc, u, h, dww, wup), ex, FFN_BWD_VMEM)
    return (*own, rider) if ex is not None else own


def ada_fwd(c_all, w, bias, name):
    L, D, n = w.shape

    def body(c_ref, w_ref, b_ref, o_ref):
        cv = c_ref[...]
        o_ref[...] = _nn(_bf(cv * _sig(cv)), _bf(w_ref[...])) + b_ref[...]

    return pl.pallas_call(
        body, name=name, grid=(L,),
        in_specs=[pl.BlockSpec((N_DEV, D), lambda l: (0, 0)), pl.BlockSpec((None, D, n), lambda l: (l, 0, 0)),
                  pl.BlockSpec((None, 1, n), lambda l: (l, 0, 0))],
        out_specs=pl.BlockSpec((None, N_DEV, n), lambda l: (l, 0, 0)),
        out_shape=jax.ShapeDtypeStruct((L, N_DEV, n), F32), compiler_params=_cp("parallel"))(c_all, w, bias)


def ada_bwd(c_all, dm, name):
    L, _, n = dm.shape
    D = c_all.shape[1]

    def body(c_ref, d_ref, o_ref):
        cv = c_ref[...]
        o_ref[...] = _tn(_bf(cv * _sig(cv)), _bf(d_ref[...]))

    return pl.pallas_call(
        body, name=name, grid=(L,),
        in_specs=[pl.BlockSpec((N_DEV, D), lambda l: (0, 0)), pl.BlockSpec((None, N_DEV, n), lambda l: (l, 0, 0))],
        out_specs=pl.BlockSpec((None, D, n), lambda l: (l, 0, 0)),
        out_shape=jax.ShapeDtypeStruct((L, D, n), F32), compiler_params=_cp("parallel"))(c_all, dm)


def sum_slots(g, name):
    S, R, C = g.shape

    def body(g_ref, o_ref):
        acc = g_ref[0]
        for s in range(1, S):
            acc = acc + g_ref[s]
        o_ref[...] = acc

    return pl.pallas_call(
        body, name=name, grid=(1,), in_specs=[pl.BlockSpec((S, R, C), lambda i: (0, 0, 0))],
        out_specs=pl.BlockSpec((R, C), lambda i: (0, 0)), out_shape=jax.ShapeDtypeStruct((R, C), F32),
        compiler_params=_cp("arbitrary"))(g)


def lb_softmax(logits, name):
    def body(l_ref, s_ref):
        l = l_ref[...]
        e = jnp.exp(l - jnp.max(l, axis=0, keepdims=True))
        s_ref[...] = e / jnp.sum(e, axis=0, keepdims=True)
    return pl.pallas_call(body, name=name, out_shape=jax.ShapeDtypeStruct(logits.shape, F32))(logits)


def adamw(w, gparts, m, v, name):
    P = len(gparts)
    S, Rp, C = gparts[0].shape
    R = P * Rp
    tr = Rp
    budget = (4 * 1024 * 1024) // (4 * (P * S + 7) * C)
    if Rp > budget:
        tr = max(t for t in range(16, budget + 1, 16) if Rp % t == 0)
    per = Rp // tr

    def body(w_ref, *rest):
        g_refs, (m_ref, v_ref, go_ref, d_ref, mo_ref, vo_ref) = rest[:P], rest[P:]
        part = pl.program_id(0) // per
        g = None
        for p, g_ref in enumerate(g_refs):
            gp = g_ref[0].astype(F32)
            for s in range(1, S):
                gp = gp + g_ref[s].astype(F32)
            g = gp if g is None else jnp.where(part == p, gp, g)
        go_ref[...] = g
        mn = ADAM_B1 * m_ref[...] + (1.0 - ADAM_B1) * g
        vn = ADAM_B2 * v_ref[...] + (1.0 - ADAM_B2) * (g * g)
        mo_ref[...] = mn
        vo_ref[...] = vn
        m_hat = mn / (1.0 - ADAM_B1 ** ADAM_STEP)
        v_hat = vn / (1.0 - ADAM_B2 ** ADAM_STEP)
        d_ref[...] = -ADAM_LR * (m_hat / (jnp.sqrt(v_hat) + ADAM_EPS) + ADAM_WD * w_ref[...])

    tile = pl.BlockSpec((tr, C), lambda i: (i, 0))
    slots = [pl.BlockSpec((S, tr, C), lambda i, p=p: (0, jnp.clip(i - p * per, 0, per - 1), 0)) for p in range(P)]
    out = jax.ShapeDtypeStruct((R, C), F32)
    return pl.pallas_call(
        body, name=name, grid=(R // tr,), in_specs=[tile] + slots + [tile, tile],
        out_specs=[tile] * 4, out_shape=[out] * 4, compiler_params=_cp("parallel"))(w, *gparts, m, v)


def _place():
    return lax.axis_index("x"), lax.axis_index("y"), lax.axis_index("c")


def _slot(p):
    return 4 * p[0] + 2 * p[1] + p[2]


class _Gather:
    def __init__(self, xs):
        self.ins = list(xs)
        n = self.n = len(xs)
        self.out_shapes = [jax.ShapeDtypeStruct((N_DEV,) + a.shape, a.dtype) for a in xs]
        self.scratch = [pltpu.SemaphoreType.DMA((n, 7)), pltpu.SemaphoreType.DMA((n, 7)), pltpu.SemaphoreType.DMA((n,))]

    def _parts(self, ins, outs, sems):
        send_sems, recv_sems, local_sems = sems
        x, y, c = _place()
        me, sib = (x, y, c), (x, y, 1 - c)
        chips = [(1 - x, y), (x, 1 - y), (1 - x, 1 - y)]

        def copy(a, k, block, to, src=None):
            dst = outs[a].at[_slot(block)]
            return pltpu.make_async_remote_copy(
                src_ref=dst if src is None else src, dst_ref=dst, send_sem=send_sems.at[a, k],
                recv_sem=recv_sems.at[a, k], device_id=to, device_id_type=MESH)

        mine = [pltpu.make_async_copy(ins[a], outs[a].at[_slot(me)], local_sems.at[a]) for a in range(self.n)]
        first = []
        for a in range(self.n):
            first.append(copy(a, 0, me, sib, src=ins[a]))
            first += [copy(a, 1 + j, me, (*chip, c), src=ins[a]) for j, chip in enumerate(chips)]
        return copy, mine, first, me, sib, chips, c

    def start(self, ins, outs, sems):
        _, mine, first, *_ = self._parts(ins, outs, sems)
        for cp in mine + first:
            cp.start()

    def finish(self, ins, outs, sems):
        copy, mine, first, me, sib, chips, c = self._parts(ins, outs, sems)
        passed = []
        for j, chip in enumerate(chips):
            for a in range(self.n):
                copy(a, 1 + j, (*chip, c), me).wait_recv()
                fwd = copy(a, 4 + j, (*chip, c), sib)
                fwd.start()
                passed.append(fwd)
        for a in range(self.n):
            copy(a, 0, sib, me).wait_recv()
            for j, chip in enumerate(chips):
                copy(a, 4 + j, (*chip, 1 - c), me).wait_recv()
        for cp in first + passed:
            cp.wait_send()
        for cp in mine:
            cp.wait()


def _run_alone(ex, name):
    def body(*refs):
        ni, no = len(ex.ins), len(ex.out_shapes)
        parts = refs[:ni], refs[ni:ni + no], refs[ni + no:]
        ex.start(*parts)
        ex.finish(*parts)
    return pl.pallas_call(body, name=name, in_specs=[ANY] * len(ex.ins), out_specs=[ANY] * len(ex.out_shapes),
                          out_shape=ex.out_shapes, scratch_shapes=ex.scratch)(*ex.ins)


def _ride(body, n_in, n_out, ex, first, last):
    ni, no = len(ex.ins), len(ex.out_shapes)

    def wrapped(*refs):
        a, r_in = refs[:n_in], refs[n_in:n_in + ni]
        b, r_out = refs[n_in + ni:n_in + ni + n_out], refs[n_in + ni + n_out:n_in + ni + n_out + no]
        rest = refs[n_in + ni + n_out + no:]
        s, r_s = rest[:len(rest) - len(ex.scratch)], rest[len(rest) - len(ex.scratch):]

        @pl.when(first())
        def _():
            ex.start(r_in, r_out, r_s)
        body(*a, *b, *s)

        @pl.when(last())
        def _():
            ex.finish(r_in, r_out, r_s)
    return wrapped


def all_gather(xs, name):
    return _run_alone(_Gather(xs), name)


class _Exchange:
    def __init__(self, groups):
        self.ins = [a for grp in groups for a in grp]
        self.where = [(g, i) for g, grp in enumerate(groups) for i in range(len(grp))]
        n = self.n = len(self.ins)
        self.out_shapes = [jax.ShapeDtypeStruct((N_DEV, len(grp)) + grp[0].shape[1:], grp[0].dtype) for grp in groups]
        self.scratch = [pltpu.SemaphoreType.DMA((n, 7)), pltpu.SemaphoreType.DMA((n, 7)), pltpu.SemaphoreType.DMA((n,))]

    def _parts(self, ins, outs, sems):
        send_sems, recv_sems, local_sems = sems
        x, y, c = _place()
        me = (x, y, c)

        def peer(r):
            return (1 - x if r & 4 else x, 1 - y if r & 2 else y, 1 - c if r & 1 else c)

        def land(a, src):
            g, i = self.where[a]
            return outs[g].at[_slot(src), i]

        def copy(a, r, src_dev):
            p = peer(r)
            return pltpu.make_async_remote_copy(
                src_ref=ins[a].at[_slot(p)], dst_ref=land(a, src_dev), send_sem=send_sems.at[a, r - 1],
                recv_sem=recv_sems.at[a, r - 1], device_id=p, device_id_type=MESH)

        mine = [pltpu.make_async_copy(ins[a].at[_slot(me)], land(a, me), local_sems.at[a]) for a in range(self.n)]
        sends = [copy(a, r, me) for r in range(1, N_DEV) for a in range(self.n)]
        arrivals = [copy(a, r, peer(r)) for r in range(1, N_DEV) for a in range(self.n)]
        return mine, sends, arrivals

    def start(self, ins, outs, sems):
        mine, sends, _ = self._parts(ins, outs, sems)
        for cp in mine + sends:
            cp.start()

    def finish(self, ins, outs, sems):
        mine, sends, arrivals = self._parts(ins, outs, sems)
        for cp in arrivals:
            cp.wait_recv()
        for cp in sends:
            cp.wait_send()
        for cp in mine:
            cp.wait()


def all_to_all(groups, name):
    return _run_alone(_Exchange(groups), name)


def _mods(mod, l, D):
    return [mod[l:l + 1, k * D:(k + 1) * D] for k in range(6)]


def _ffn_backward(l, dxo, dy, hb, u, uc, xin, sc2, gains, W, post, ex=None):
    _, NJ, T, fs = u.shape
    duc, d_wdn = ffn_bwd_a(dy, uc, W["wdn"][l], f"ffn{l}_bwd_a")
    dh, taps, d_wup, *rider = ffn_bwd_b(duc.reshape(2 * NJ, T, fs), u.reshape(2 * NJ, T, fs), hb, W["fdw_w"][l],
                                        W["wup"][l], f"ffn{l}_bwd_b", ex)
    dx, da, dsh, *below = resid_bwd(f"ffn{l}_resid_bwd", pre=(dh, xin, dxo, gains["pre_ffn_g"][l:l + 1], sc2), post=post)
    small = dict(da=da, dsh=dsh, dwb=taps[:, 3], dww=taps[:, 0:3])
    return dx, d_wup, d_wdn, small, below, (rider[0] if rider else None)


def _local_step(xt, tgt, mod, gains, W, mine):
    T, D = xt.shape
    zero_d = jnp.zeros((1, D), F32)
    half = lambda g: g.reshape(N_DEV // 2, 2 * g.shape[1], D)
    whole = lambda g: jnp.transpose(g, (1, 0, 2)).reshape(g.shape[1], -1)
    sh1, sc1, g1, sh2, sc2, g2 = m0 = _mods(mod, 0, D)
    h0, (whin,) = prenorm_mod(xt, gains["pre_mix_g"][0:1], sc1, sh1, "l0_prenorm", _Gather([mine["whin"]]))
    W = dict(W, whin=whin)
    proj, (whout, wup0, wdn0) = mm_nn(h0, W["whin"], jnp.zeros((1, 4 * D), F32), "hgrn_proj",
                                      _Gather([mine["whout"], mine["wup"][0], mine["wdn"][0]]))
    o, og, st, (wcin, wcout, wup1, wdn1) = hgrn_fwd(
        proj, W["lb"], W["gg"], "hgrn_fwd", _Gather([mine["wcin"], mine["wcout"], mine["wup"][1], mine["wdn"][1]]))
    W = dict(W, whout=whout.reshape(D, D), wcin=wcin, wcout=wcout.reshape(D, D), wup=[wup0, wup1],
             wdn=[half(wdn0), half(wdn1)])
    t1, c1, e1, t2, c2, e2 = m1 = _mods(mod, 1, D)
    y0, x1, h0f = mm_post(og, W["whout"], zero_d, xt, g1, gains["post_mix_g"][0:1],
                          (gains["pre_ffn_g"][0:1], sc2, sh2), "hgrn_out")
    u0, uc0, yf0, x2, h1 = ffn_fwd(h0f, W["wup"][0], W["fdw_w"][0], W["fdw_b"][0], W["wdn"][0], x1, g2,
                                   gains["post_ffn_g"][0:1], (gains["pre_mix_g"][1:2], c1, t1), "ffn0_fwd")
    uc = mm_nn(h1, W["wcin"], W["cb_in"], "conv_in")
    cv, va = conv_mid_fwd(uc, W["cw"], W["cdw_b"], W["ln_g"], W["ln_b"], "conv_mid_fwd")
    y1, x3, h1f = mm_post(va, W["wcout"], W["cb_out"], x2, e1, gains["post_mix_g"][1:2],
                          (gains["pre_ffn_g"][1:2], c2, t2), "conv_out")
    u1, uc1, yf1, x4 = ffn_fwd(h1f, W["wup"][1], W["fdw_w"][1], W["fdw_b"][1], W["wdn"][1], x3, e2,
                               gains["post_ffn_g"][1:2], None, "ffn1_fwd")
    dx4, loss_part, dyf1, dgate_f1, dpost_f1, _ = resid_bwd(
        "loss_resid_bwd", loss=(x4, tgt), post=(yf1, e2, gains["post_ffn_g"][1:2]))
    dx3, d_wup1, d_wdn1, sf1, (dy1, dgate_c, dpost_c, dys_c), _ = _ffn_backward(
        1, dx4, dyf1, h1f, u1, uc1, x3, c2, gains, W, (y1, e1, gains["post_mix_g"][1:2]))
    dva = mm_nt(dy1, W["wcout"], "conv_dva")
    d_wcout = mm_tn(va, dy1, "conv_dwout", True, True, D, D)
    dcv, dlg, dlbias, taps = conv_bwd_a(dva, cv, uc, W["ln_g"], W["ln_b"], "conv_bwd_a")
    duc, dbin = conv_bwd_b(dcv, uc, W["cw"], "conv_bwd_b")
    dh1 = mm_nt(duc, whole(W["wcin"]), "conv_dh")
    d_wcin = mm_tn(h1, duc, "conv_dwin", True, True, D, W["wcin"].shape[2])
    dx2, da_c, dsh_c, dyf0, dgate_f0, dpost_f0, _ = resid_bwd(
        "l1_resid_bwd", pre=(dh1, x2, dx3, gains["pre_mix_g"][1:2], c1), post=(yf0, g2, gains["post_ffn_g"][0:1]))
    rows = lambda g: g.reshape(N_DEV, -1, D)
    dx1, d_wup0, d_wdn0, sf0, (dy0, dgate_h, dpost_h, _), (r_wcin, r_wcout, r_wup1, r_wdn1) = _ffn_backward(
        0, dx2, dyf0, h0f, u0, uc0, x1, sc2, gains, W, (y0, g1, gains["post_mix_g"][0:1]),
        _Exchange([[d_wcin], [rows(d_wcout)], [d_wup1], [rows(d_wdn1)]]))
    dog = mm_nt(dy0, W["whout"], "hgrn_dog")
    d_whout = mm_tn(og, dy0, "hgrn_dwout", True, True, D, D)
    dproj, dlb, dgg, (r_wup0, r_wdn0, r_whout) = hgrn_bwd(
        proj, o, dog, st, W["lb"], W["gg"], "hgrn_bwd", _Exchange([[d_wup0], [rows(d_wdn0)], [rows(d_whout)]]))
    d_whin = mm_tn(h0, dproj, "hgrn_dwin", True, True, D, W["whin"].shape[2])
    dh0, (r_whin,) = mm_nt(dproj, whole(W["whin"]), "hgrn_dh", _Exchange([[d_whin]]))
    dx0, da_h, dsh_h = resid_bwd("l0_resid_bwd", pre=(dh0, xt, dx1, gains["pre_mix_g"][0:1], sc1))
    sf0.update(dgate=dgate_f0, dpost=dpost_f0)
    sf1.update(dgate=dgate_f1, dpost=dpost_f1)

    big = dict(hgrn_w_in=r_whin, hgrn_w_out=r_whout, conv_w_in=r_wcin, conv_w_out=r_wcout,
               ffn_w_up=[r_wup0, r_wup1], ffn_w_down=[r_wdn0, r_wdn1])
    pm, pf = gains["pre_mix_g"], gains["pre_ffn_g"]
    dmod = jnp.concatenate([
        dsh_h, da_h * pm[0:1], dgate_h, sf0["dsh"], sf0["da"] * pf[0:1], sf0["dgate"],
        dsh_c, da_c * pm[1:2], dgate_c, sf1["dsh"], sf1["da"] * pf[1:2], sf1["dgate"]], axis=1)
    NCH = D // LANES
    tap = taps.reshape(NCH, 32, 8, LANES).sum(axis=2)
    small = dict(
        dmod=dmod,
        pre_mix_g=jnp.concatenate([da_h * (1.0 + sc1), da_c * (1.0 + c1)], axis=0),
        post_mix_g=jnp.concatenate([dpost_h, dpost_c], axis=0),
        pre_ffn_g=jnp.concatenate([sf0["da"] * (1.0 + sc2), sf1["da"] * (1.0 + c2)], axis=0),
        post_ffn_g=jnp.concatenate([sf0["dpost"], sf1["dpost"]], axis=0),
        lb=dlb, gg=dgg,
        ffn_dw_b=jnp.stack([sf0["dwb"], sf1["dwb"]]),
        conv_b_in=dbin,
        conv_dw_w=jnp.transpose(tap[:, :31], (1, 0, 2)).reshape(31, D),
        conv_dw_b=tap[:, 31].reshape(1, D),
        conv_ln_g=dlg, conv_ln_b=dlbias, conv_b_out=dys_c,
        ffn_dw_w=jnp.stack([sf0["dww"], sf1["dww"]]))
    return loss_part, dx0, big, small


SMALL_ORDER = ["dmod", "pre_mix_g", "post_mix_g", "pre_ffn_g", "post_ffn_g", "lb", "gg", "ffn_dw_b", "conv_b_in",
               "conv_dw_w", "conv_dw_b", "conv_ln_g", "conv_ln_b", "conv_b_out", "ffn_dw_w"]


def _pack(parts):
    flat = jnp.concatenate([p.reshape(-1) for p in parts])
    pad = (-flat.shape[0]) % LANES
    if pad:
        flat = jnp.concatenate([flat, jnp.zeros((pad,), F32)])
    return flat.reshape(-1, LANES)


def _unpack(flat, shapes):
    out, off = [], 0
    for s in shapes:
        n = 1
        for d in s:
            n *= d
        out.append(flat[off:off + n].reshape(s))
        off += n
    return out


def _apply_adamw(name, w, g_slots, m, v):
    shp = w.shape
    C = shp[-1]
    R = w.size // C
    parts = g_slots if isinstance(g_slots, list) else [g_slots]
    parts = [p.reshape(p.shape[0], R // len(parts), C) for p in parts]
    g, d, mn, vn = adamw(w.reshape(R, C), parts, m.reshape(R, C), v.reshape(R, C), "adamw_" + name)
    return g.reshape(shp), d.reshape(shp), mn.reshape(shp), vn.reshape(shp)


WEIGHTS = ['ada_w', 'ada_b', 'pre_mix_g', 'post_mix_g', 'pre_ffn_g', 'post_ffn_g', 'hgrn_w_in', 'hgrn_lb_logits',
           'hgrn_gnorm_g', 'hgrn_w_out', 'conv_w_in', 'conv_b_in', 'conv_dw_w', 'conv_dw_b', 'conv_ln_g', 'conv_ln_b',
           'conv_w_out', 'conv_b_out', 'ffn_w_up', 'ffn_dw_w', 'ffn_dw_b', 'ffn_w_down']


def kernel(x, c, ada_w, ada_b, pre_mix_g, post_mix_g, pre_ffn_g, post_ffn_g, hgrn_w_in, hgrn_lb_logits, hgrn_gnorm_g, hgrn_w_out, conv_w_in, conv_b_in, conv_dw_w, conv_dw_b, conv_ln_g, conv_ln_b, conv_w_out, conv_b_out, ffn_w_up, ffn_dw_w, ffn_dw_b, ffn_w_down, loss_target, m_ada_w, m_ada_b, m_pre_mix_g, m_post_mix_g, m_pre_ffn_g, m_post_ffn_g, m_hgrn_w_in, m_hgrn_lb_logits, m_hgrn_gnorm_g, m_hgrn_w_out, m_conv_w_in, m_conv_b_in, m_conv_dw_w, m_conv_dw_b, m_conv_ln_g, m_conv_ln_b, m_conv_w_out, m_conv_b_out, m_ffn_w_up, m_ffn_dw_w, m_ffn_dw_b, m_ffn_w_down, v_ada_w, v_ada_b, v_pre_mix_g, v_post_mix_g, v_pre_ffn_g, v_post_ffn_g, v_hgrn_w_in, v_hgrn_lb_logits, v_hgrn_gnorm_g, v_hgrn_w_out, v_conv_w_in, v_conv_b_in, v_conv_dw_w, v_conv_dw_b, v_conv_ln_g, v_conv_ln_b, v_conv_w_out, v_conv_b_out, v_ffn_w_up, v_ffn_dw_w, v_ffn_dw_b, v_ffn_w_down):
    P = dict(locals())
    _, T, D = x.shape
    me = _slot(_place())
    L = ada_w.shape[0]
    na = ada_w.shape[2]
    fs = ffn_w_up.shape[2]
    nci = conv_w_in.shape[2]
    nd = conv_dw_b.shape[1]
    NCH = D // LANES

    small_in = [c, conv_b_in, conv_dw_w, conv_dw_b, conv_ln_g, conv_ln_b, conv_b_out, ffn_dw_w]
    (packed,) = all_gather([_pack(small_in)], "gather_first")
    mine = dict(whin=_bf(hgrn_w_in[0]), whout=_bf(hgrn_w_out[0]), wcin=_bf(conv_w_in[0]), wcout=_bf(conv_w_out[0]),
                wup=[_bf(ffn_w_up[l]) for l in range(L)], wdn=[_bf(ffn_w_down[l]) for l in range(L)])
    sm = packed.reshape(N_DEV, -1)
    offs = [0]
    for a in small_in:
        offs.append(offs[-1] + a.size)
    piece = lambda k, shp: sm[:, offs[k]:offs[k + 1]].reshape((N_DEV,) + shp)
    c_all = piece(0, (D,))
    dw_nat = jnp.transpose(piece(2, (31, nd)), (1, 0, 2)).reshape(31, D)
    cw = jnp.pad(jnp.transpose(dw_nat.reshape(31, NCH, LANES), (1, 0, 2)), ((0, 0), (0, 1), (0, 0)))
    fdw = piece(7, (L, 3, fs))
    lb_soft = lb_softmax(hgrn_lb_logits, "lb_softmax")
    W = dict(
        lb=lb_soft[0:1], gg=hgrn_gnorm_g, cb_in=piece(1, (nci,)).reshape(1, N_DEV * nci), cw=cw,
        cdw_b=piece(3, (nd,)).reshape(1, D), ln_g=piece(4, (nd,)).reshape(1, D), ln_b=piece(5, (nd,)).reshape(1, D),
        cb_out=piece(6, (nd,)).reshape(1, D), fdw_w=[fdw[:, l] for l in range(L)],
        fdw_b=[ffn_dw_b[l].reshape(N_DEV, 1, fs) for l in range(L)])

    bias_mine = lax.dynamic_slice(ada_b, (0, me * na), (L, na)).reshape(L, 1, na)
    mod_cols = ada_fwd(c_all, ada_w, bias_mine, "ada_fwd")
    (mod_all,) = all_gather([mod_cols], "gather_mod")
    mod = jnp.transpose(lax.dynamic_index_in_dim(mod_all, me, axis=2, keepdims=False), (1, 0, 2)).reshape(L, N_DEV * na)

    gains = dict(pre_mix_g=pre_mix_g, post_mix_g=post_mix_g, pre_ffn_g=pre_ffn_g, post_ffn_g=post_ffn_g)
    loss_part, dx0, big, small = _local_step(x[0], loss_target[0], mod, gains, W, mine)
    loss = lax.psum((0.5 / D) * jnp.sum(loss_part), ("x", "y", "c"))

    parts = [small[k] for k in SMALL_ORDER]
    (sm_all,) = all_gather([_pack(parts)], "gather_small_grads")
    tot = _unpack(sum_slots(sm_all, "sum_small_grads").reshape(-1), [p.shape for p in parts])
    tot = dict(zip(SMALL_ORDER, tot))
    dmod_all = sm_all.reshape(N_DEV, -1)[:, :L * 6 * D].reshape(N_DEV, L, 6 * D)
    dm_mine = jnp.transpose(lax.dynamic_slice(dmod_all, (0, 0, me * na), (N_DEV, L, na)), (1, 0, 2))
    s0 = lb_soft[0:1]
    onehot = (lax.broadcasted_iota(jnp.int32, lb_soft.shape, 0) == 0).astype(F32)
    col = lambda a, n: lax.dynamic_slice_in_dim(a, me * n, n, axis=a.ndim - 1)
    G = {
        'ada_w': ada_bwd(c_all, dm_mine, "ada_bwd")[None],
        'ada_b': tot["dmod"].reshape(1, L, 6 * D),
        'pre_mix_g': tot["pre_mix_g"][None], 'post_mix_g': tot["post_mix_g"][None],
        'pre_ffn_g': tot["pre_ffn_g"][None], 'post_ffn_g': tot["post_ffn_g"][None],
        'hgrn_lb_logits': (tot["lb"] * s0 * (onehot - lb_soft))[None],
        'hgrn_gnorm_g': tot["gg"][None],
        'ffn_dw_b': tot["ffn_dw_b"].reshape(1, L, N_DEV * fs),
        'conv_b_in': col(tot["conv_b_in"], nci)[None],
        'conv_dw_w': col(tot["conv_dw_w"], nd)[None, None],
        'conv_dw_b': col(tot["conv_dw_b"], nd)[None], 'conv_ln_g': col(tot["conv_ln_g"], nd)[None],
        'conv_ln_b': col(tot["conv_ln_b"], nd)[None], 'conv_b_out': col(tot["conv_b_out"], nd)[None],
        'ffn_dw_w': lax.dynamic_index_in_dim(tot["ffn_dw_w"], me, axis=1, keepdims=False)[None],
    }

    G.update(big)
    res = {n: _apply_adamw(n, P[n], G[n], P["m_" + n], P["v_" + n]) for n in WEIGHTS}
    return (loss, dx0[None], *[res[n][0] for n in WEIGHTS], *[res[n][1] for n in WEIGHTS],
            *[res[n][2] for n in WEIGHTS], *[res[n][3] for n in WEIGHTS])
```

```python
import functools

import jax
import jax.numpy as jnp
from jax import lax
from jax.experimental import pallas as pl
from jax.experimental.pallas import tpu as pltpu

F32 = jnp.float32
BF16 = jnp.bfloat16
EPS = 1e-6
LANES = 128
N_DEV = 8
VMEM_LIMIT = 48 * 1024 * 1024
FFN_BWD_VMEM = 58 * 1024 * 1024
MM_ROWS = 1024
HG_CHUNK = 128
HG_SUB = 32
EXP_CLAMP = 80.0
CONV_HALO = 32
FFN_HALO = 8
CONV_ROWS = 32
ADAM_LR, ADAM_B1, ADAM_B2, ADAM_EPS, ADAM_WD, ADAM_STEP = 0.001, 0.9, 0.999, 1e-08, 0.01, 10
MESH = pl.DeviceIdType.MESH
ANY = pl.BlockSpec(memory_space=pl.ANY)


def _cp(*sem):
    return pltpu.CompilerParams(dimension_semantics=sem, vmem_limit_bytes=VMEM_LIMIT)


def _rt(n, t):
    t = min(n, t)
    assert n % t == 0, (n, t)
    return t


def _sig(x):
    return jax.nn.sigmoid(x)


def _nt(a, b):
    return lax.dot_general(a, b, (((1,), (1,)), ((), ())), preferred_element_type=F32)


def _tn(a, b):
    return lax.dot_general(a, b, (((0,), (0,)), ((), ())), preferred_element_type=F32)


def _nn(a, b):
    return jnp.dot(a, b, preferred_element_type=F32)


def _bf(x):
    return x.astype(BF16)


def _nn_st(a, b):
    return _nn(_bf(a), _bf(b))


def _tn_st(a, b):
    return _tn(_bf(a), _bf(b))


def _row(d):
    return pl.BlockSpec((1, d), lambda *_: (0, 0))


def prenorm_mod(x, g, sc, sh, name, ex=None):
    T, D = x.shape
    tm = _rt(T, 512)

    def body(x_ref, g_ref, sc_ref, sh_ref, h_ref):
        h_ref[...] = _next_input(x_ref[...], g_ref, sc_ref, sh_ref)

    tile = pl.BlockSpec((tm, D), lambda i: (i, 0))
    (h,), rider = _call(body, name, (T // tm,), [tile, _row(D), _row(D), _row(D)], [tile],
                        [jax.ShapeDtypeStruct((T, D), BF16)], [], ("parallel",), (x, g, sc, sh), ex)
    return (h, rider) if ex is not None else h


def resid_bwd(name, pre=None, loss=None, post=None):
    T, D = (pre[1] if pre is not None else loss[0]).shape
    tm = _rt(T, 512)
    n_src = 5 if pre is not None else 2
    n_in = n_src + (3 if post is not None else 0)

    def body(*refs):
        ins, outs = refs[:n_in], refs[n_in:]
        first = pl.program_id(0) == 0
        if pre is not None:
            dh_ref, x_ref, dr_ref, g_ref, sc_ref = ins[:5]
            dx_ref, da_ref, dsh_ref = outs[:3]

            @pl.when(first)
            def _():
                da_ref[...] = jnp.zeros_like(da_ref)
                dsh_ref[...] = jnp.zeros_like(dsh_ref)
            xv = x_ref[...]
            dh = dh_ref[...]
            r = lax.rsqrt(jnp.mean(xv * xv, axis=-1, keepdims=True) + EPS)
            n = xv * r
            da_ref[...] += jnp.sum(dh * n, axis=0, keepdims=True)
            dsh_ref[...] += jnp.sum(dh, axis=0, keepdims=True)
            dn = dh * (g_ref[...] * (1.0 + sc_ref[...]))
            dx = dr_ref[...] + r * (dn - n * jnp.mean(dn * n, axis=-1, keepdims=True))
            dx_ref[...] = dx
            rest = outs[3:]
        else:
            x_ref, t_ref = ins[:2]
            dx_ref, part_ref = outs[:2]

            @pl.when(first)
            def _():
                part_ref[...] = jnp.zeros_like(part_ref)
            e = x_ref[...] - t_ref[...]
            dx = e * (1.0 / D)
            dx_ref[...] = dx
            e2 = (e * e).reshape(tm // 8, 8, D).sum(axis=0)
            acc = e2[:, 0:LANES]
            for j in range(1, D // LANES):
                acc = acc + e2[:, j * LANES:(j + 1) * LANES]
            part_ref[...] += acc
            rest = outs[2:]
        if post is not None:
            y_ref, gt_ref, pg_ref = ins[n_src:]
            dy_ref, dgate_ref, dpg_ref, dys_ref = rest

            @pl.when(first)
            def _():
                dgate_ref[...] = jnp.zeros_like(dgate_ref)
                dpg_ref[...] = jnp.zeros_like(dpg_ref)
                dys_ref[...] = jnp.zeros_like(dys_ref)
            yv = y_ref[...]
            ry = lax.rsqrt(jnp.mean(yv * yv, axis=-1, keepdims=True) + EPS)
            ny = yv * ry
            s = jnp.sum(dx * ny, axis=0, keepdims=True)
            dgate_ref[...] += s * pg_ref[...]
            dpg_ref[...] += s * gt_ref[...]
            dny = dx * (gt_ref[...] * pg_ref[...])
            dy = ry * (dny - ny * jnp.mean(dny * ny, axis=-1, keepdims=True))
            dy_ref[...] = _bf(dy)
            dys_ref[...] += jnp.sum(dy, axis=0, keepdims=True)

    tile = pl.BlockSpec((tm, D), lambda i: (i, 0))
    vec = jax.ShapeDtypeStruct((1, D), F32)
    big = jax.ShapeDtypeStruct((T, D), F32)
    if pre is not None:
        operands, in_specs = list(pre), [tile, tile, tile, _row(D), _row(D)]
        out_specs, out_shape = [tile, _row(D), _row(D)], [big, vec, vec]
    else:
        operands, in_specs = list(loss), [tile, tile]
        out_specs, out_shape = [tile, pl.BlockSpec((8, LANES), lambda i: (0, 0))], [big, jax.ShapeDtypeStruct((8, LANES), F32)]
    if post is not None:
        operands, in_specs = operands + list(post), in_specs + [tile, _row(D), _row(D)]
        out_specs = out_specs + [tile, _row(D), _row(D), _row(D)]
        out_shape = out_shape + [jax.ShapeDtypeStruct((T, D), BF16), vec, vec, vec]
    return pl.pallas_call(body, name=name, grid=(T // tm,), in_specs=in_specs, out_specs=out_specs, out_shape=out_shape,
                          compiler_params=_cp("arbitrary"))(*operands)


def _call(body, name, grid, in_specs, out_specs, out_shape, scratch, sems, operands, ex=None, vmem=VMEM_LIMIT):
    if ex is not None:
        def first():
            return functools.reduce(lambda p, q: p & q, [pl.program_id(k) == 0 for k in range(len(grid))])

        def last():
            return functools.reduce(lambda p, q: p & q, [pl.program_id(k) == grid[k] - 1 for k in range(len(grid))])
        body = _ride(body, len(in_specs), len(out_specs), ex, first, last)
        in_specs = in_specs + [ANY] * len(ex.ins)
        out_specs = out_specs + [ANY] * len(ex.out_shapes)
        out_shape = out_shape + ex.out_shapes
        scratch = scratch + ex.scratch
        operands = list(operands) + ex.ins
        sems = ("arbitrary",) * len(grid)
    res = pl.pallas_call(body, name=name, grid=grid, in_specs=in_specs, out_specs=out_specs, out_shape=out_shape,
                         scratch_shapes=scratch,
                         compiler_params=pltpu.CompilerParams(dimension_semantics=sems, vmem_limit_bytes=vmem))(*operands)
    n_own = len(res) - (len(ex.out_shapes) if ex is not None else 0)
    return res[:n_own], res[n_own:]


def mm_nn(a, bg, bias, name, ex=None):
    M, K = a.shape
    G, _, n = bg.shape
    tm = _rt(M, MM_ROWS)

    def body(a_ref, b_ref, bias_ref, o_ref):
        o_ref[...] = _nn(a_ref[...], b_ref[...]) + bias_ref[...]

    (out,), rider = _call(
        body, name, (M // tm, G),
        [pl.BlockSpec((tm, K), lambda i, j: (i, 0)), pl.BlockSpec((None, K, n), lambda i, j: (j, 0, 0)),
         pl.BlockSpec((1, n), lambda i, j: (0, j))],
        [pl.BlockSpec((tm, n), lambda i, j: (i, j))], [jax.ShapeDtypeStruct((M, G * n), F32)], [],
        ("parallel", "arbitrary"), (a, bg, bias), ex)
    return (out, rider) if ex is not None else out


def _next_input(xo, g_ref, sc_ref, sh_ref):
    r = lax.rsqrt(jnp.mean(xo * xo, axis=-1, keepdims=True) + EPS)
    return _bf((xo * r) * g_ref[...] * (1.0 + sc_ref[...]) + sh_ref[...])


def mm_post(a, b, bias, x, gate, pg, nxt, name):
    T, K = a.shape
    D = b.shape[1]
    tm = _rt(T, 512)

    def body(a_ref, b_ref, bias_ref, x_ref, gt_ref, pg_ref, g_ref, sc_ref, sh_ref, y_ref, xo_ref, h_ref):
        y = _nn(a_ref[...], b_ref[...]) + bias_ref[...]
        y_ref[...] = y
        r = lax.rsqrt(jnp.mean(y * y, axis=-1, keepdims=True) + EPS)
        xo = x_ref[...] + gt_ref[...] * ((y * r) * pg_ref[...])
        xo_ref[...] = xo
        h_ref[...] = _next_input(xo, g_ref, sc_ref, sh_ref)

    tile = pl.BlockSpec((tm, D), lambda i: (i, 0))
    out = jax.ShapeDtypeStruct((T, D), F32)
    return pl.pallas_call(
        body, name=name, grid=(T // tm,),
        in_specs=[pl.BlockSpec((tm, K), lambda i: (i, 0)), pl.BlockSpec((K, D), lambda i: (0, 0)), _row(D), tile]
        + [_row(D)] * 5,
        out_specs=[tile, tile, tile], out_shape=[out, out, jax.ShapeDtypeStruct((T, D), BF16)],
        compiler_params=_cp("parallel"))(a, b, bias, x, gate, pg, *nxt)


def mm_nt(dy, w, name, ex=None):
    T, N = dy.shape
    K = w.shape[0]
    tm = _rt(T, 512)

    def body(dy_ref, w_ref, o_ref):
        o_ref[...] = _nt(_bf(dy_ref[...]), w_ref[...])

    (out,), rider = _call(
        body, name, (T // tm,), [pl.BlockSpec((tm, N), lambda i: (i, 0)), pl.BlockSpec((K, N), lambda i: (0, 0))],
        [pl.BlockSpec((tm, K), lambda i: (i, 0))], [jax.ShapeDtypeStruct((T, K), F32)], [], ("parallel",),
        (dy, w), ex)
    return (out, rider) if ex is not None else out


def mm_tn(xm, gm, name, x_natural, g_natural, kx, n):
    if x_natural:
        T, Gx = xm.shape[0], xm.shape[1] // kx
    else:
        Gx, T = xm.shape[0], xm.shape[1]
    tt = _rt(T, 4 * MM_ROWS)
    if x_natural:
        xspec = pl.BlockSpec((tt, kx), lambda g, k, t: (t, k))
    else:
        xspec = pl.BlockSpec((None, tt, kx), lambda g, k, t: (k, t, 0))
    if g_natural:
        Gg = gm.shape[1] // n
        gspec = pl.BlockSpec((tt, n), lambda g, k, t: (t, g))
    else:
        Gg = gm.shape[0]
        gspec = pl.BlockSpec((None, tt, n), lambda g, k, t: (g, t, 0))

    NT = T // tt

    def body(x_ref, g_ref, o_ref, acc):
        t = pl.program_id(2)

        @pl.when(t == 0)
        def _():
            acc[...] = jnp.zeros_like(acc)
        acc[...] += _tn(_bf(x_ref[...]), _bf(g_ref[...]))

        @pl.when(t == NT - 1)
        def _():
            o_ref[...] = acc[...].astype(o_ref.dtype)

    return pl.pallas_call(
        body, name=name, grid=(Gg, Gx, NT), in_specs=[xspec, gspec],
        out_specs=pl.BlockSpec((None, kx, n), lambda g, k, t: (g, k, 0)),
        out_shape=jax.ShapeDtypeStruct((Gg, Gx * kx, n), BF16), scratch_shapes=[pltpu.VMEM((kx, n), F32)],
        compiler_params=_cp("parallel", "parallel", "arbitrary"))(xm, gm)


def _split3(x):
    h = _bf(x)
    r = x - h.astype(F32)
    m = _bf(r)
    lo = _bf(r - m.astype(F32))
    return h, m, lo


def _tri_mm(tri, x):
    h, m, lo = _split3(x)
    return _nn(tri, lo) + _nn(tri, m) + _nn(tri, h)


def _hgrn_gates(qr, fz, lb):
    sq = _sig(qr)
    q = qr * sq
    sig = _sig(fz)
    f = lb + (1.0 - lb) * sig
    k = 1.0 - f
    return sq, q, sig, f, k, jnp.log(f)


def hgrn_fwd(proj, lb, gg, name, ex=None):
    T, D4 = proj.shape
    D = D4 // 4
    H = D // LANES
    C = _rt(T, HG_CHUNK)
    SB = min(HG_SUB, C)
    NC = T // C

    def body(q_ref, f_ref, v_ref, g_ref, lb_ref, gg_ref, o_ref, og_ref, st_ref, ST, bex):
        @pl.when(pl.program_id(0) == 0)
        def _():
            ST[...] = jnp.zeros_like(ST)
        r_i = lax.broadcasted_iota(jnp.int32, (C, C), 0)
        c_i = lax.broadcasted_iota(jnp.int32, (C, C), 1)
        low = _bf((r_i >= c_i).astype(F32))
        rs = lax.broadcasted_iota(jnp.int32, (SB, C), 0)
        cs = lax.broadcasted_iota(jnp.int32, (SB, C), 1)
        for hd in range(H):
            sl = slice(hd * LANES, (hd + 1) * LANES)
            _, q, _, _, k, lf = _hgrn_gates(q_ref[:, sl], f_ref[:, sl], lb_ref[:, sl])
            v = v_ref[:, sl]
            b = _tri_mm(low, lf)
            bex[hd] = b - lf
            bend = jnp.sum(lf, axis=0, keepdims=True)
            blocks = []
            for I in range(C // SB):
                p = bex[hd, pl.ds(SB * I + SB // 2, 1), :]
                rows = slice(SB * I, SB * (I + 1))
                qI = _bf(q[rows] * jnp.exp(jnp.minimum(b[rows] - p, EXP_CLAMP)))
                kI = _bf(k * jnp.exp(jnp.minimum(p - b, EXP_CLAMP)))
                blocks.append(jnp.where(rs + SB * I >= cs, _nt(qI, kI), 0.0))
            A = jnp.concatenate(blocks, axis=0)
            st0 = ST[hd]
            st_ref[0, hd] = st0
            o = _nn(_bf(A), _bf(v)) + _nt(_bf(q * jnp.exp(b)), _bf(st0))
            ST[hd] = st0 * jnp.exp(bend) + _tn_st(v, k * jnp.exp(bend - b))
            o_ref[:, sl] = o
            r = lax.rsqrt(jnp.mean(o * o, axis=-1, keepdims=True) + EPS)
            gr = g_ref[:, sl]
            og_ref[:, sl] = _bf((o * r) * gg_ref[...] * (gr * _sig(gr)))

    col = lambda j: pl.BlockSpec((C, D), lambda c, j=j: (c, j))
    tile = pl.BlockSpec((C, D), lambda c: (c, 0))
    own, rider = _call(
        body, name, (NC,), [col(0), col(1), col(2), col(3), _row(D), _row(LANES)],
        [tile, tile, pl.BlockSpec((1, H, LANES, LANES), lambda c: (c, 0, 0, 0))],
        [jax.ShapeDtypeStruct((T, D), F32), jax.ShapeDtypeStruct((T, D), BF16),
         jax.ShapeDtypeStruct((NC, H, LANES, LANES), F32)],
        [pltpu.VMEM((H, LANES, LANES), F32), pltpu.VMEM((H, C, LANES), F32)], ("arbitrary",),
        (proj, proj, proj, proj, lb, gg), ex)
    return (*own, rider) if ex is not None else own


def hgrn_bwd(proj, o, dog, st, lb, gg, name, ex=None):
    T, D4 = proj.shape
    D = D4 // 4
    H = D // LANES
    C = _rt(T, HG_CHUNK)
    SB = min(HG_SUB, C)
    NC = T // C

    def body(q_ref, f_ref, v_ref, g_ref, o_ref, dog_ref, st_ref, lb_ref, gg_ref, dp_ref, dlb_ref, dgg_ref,
             DST, bex):
        @pl.when(pl.program_id(0) == 0)
        def _():
            DST[...] = jnp.zeros_like(DST)
            dlb_ref[...] = jnp.zeros_like(dlb_ref)
            dgg_ref[...] = jnp.zeros_like(dgg_ref)
        r_i = lax.broadcasted_iota(jnp.int32, (C, C), 0)
        c_i = lax.broadcasted_iota(jnp.int32, (C, C), 1)
        causal = r_i >= c_i
        low = _bf(causal.astype(F32))
        upp = _bf((r_i <= c_i).astype(F32))
        rs = lax.broadcasted_iota(jnp.int32, (SB, C), 0)
        cs = lax.broadcasted_iota(jnp.int32, (SB, C), 1)
        ggv = gg_ref[...]
        for hd in range(H):
            sl = slice(hd * LANES, (hd + 1) * LANES)
            qr = q_ref[:, sl]
            lbv = lb_ref[:, sl]
            sq, q, sig, f, k, lf = _hgrn_gates(qr, f_ref[:, sl], lbv)
            v = v_ref[:, sl]
            oh = o_ref[:, sl]
            r = lax.rsqrt(jnp.mean(oh * oh, axis=-1, keepdims=True) + EPS)
            n = oh * r
            gr = g_ref[:, sl]
            sg = _sig(gr)
            dgo = dog_ref[:, sl]
            dgr = dgo * (n * ggv) * (sg * (1.0 + gr * (1.0 - sg)))
            don = dgo * (gr * sg)
            dgg_ref[...] += jnp.sum(don * n, axis=0, keepdims=True)
            dn = don * ggv
            do = r * (dn - n * jnp.mean(dn * n, axis=-1, keepdims=True))
            b = _tri_mm(low, lf)
            bex[hd] = b - lf
            bend = jnp.sum(lf, axis=0, keepdims=True)
            dob = _bf(do)
            dA = _bf(jnp.where(causal, _nt(dob, _bf(v)), 0.0))
            blocks, dqs, gqs = [], [], []
            dk = jnp.zeros((C, LANES), F32)
            gk = jnp.zeros((C, LANES), F32)
            for I in range(C // SB):
                p = bex[hd, pl.ds(SB * I + SB // 2, 1), :]
                rows = slice(SB * I, SB * (I + 1))
                eq = jnp.exp(jnp.minimum(b[rows] - p, EXP_CLAMP))
                qI = _bf(q[rows] * eq)
                ek = jnp.exp(jnp.minimum(p - b, EXP_CLAMP))
                kI = _bf(k * ek)
                blocks.append(jnp.where(rs + SB * I >= cs, _nt(qI, kI), 0.0))
                dqI = _nn(dA[rows], kI)
                dkI = _tn(dA[rows], qI)
                dqs.append(dqI * eq)
                gqs.append(qI.astype(F32) * dqI)
                dk = dk + dkI * ek
                gk = gk + kI.astype(F32) * dkI
            A = jnp.concatenate(blocks, axis=0)
            dst = DST[hd]
            st0 = st_ref[0, hd]
            eb = jnp.exp(b)
            ee = jnp.exp(bend - b)
            ebend = jnp.exp(bend)
            dv = _tn(_bf(A), dob) + _nt(_bf(k * ee), _bf(dst))
            dk_state = ee * _nn_st(v, dst)
            dq_state = eb * _nn_st(do, st0)
            dq = jnp.concatenate(dqs, axis=0) + dq_state
            dk = dk + dk_state
            DST[hd] = dst * ebend + _tn_st(do, q * eb)
            later = jnp.sum(k * dk_state, axis=0, keepdims=True) + ebend * jnp.sum(st0 * dst, axis=0, keepdims=True)
            pairs = jnp.concatenate(gqs, axis=0) - gk + (q * dq_state - k * dk_state)
            dlf = _tri_mm(upp, pairs) + later
            df = dlf / f - dk
            dlb_ref[:, sl] += jnp.sum(df * (1.0 - sig), axis=0, keepdims=True)
            dp_ref[:, sl] = _bf(dq * (sq * (1.0 + qr * (1.0 - sq))))
            dp_ref[:, D + hd * LANES:D + (hd + 1) * LANES] = _bf(df * (1.0 - lbv) * (sig * (1.0 - sig)))
            dp_ref[:, 2 * D + hd * LANES:2 * D + (hd + 1) * LANES] = _bf(dv)
            dp_ref[:, 3 * D + hd * LANES:3 * D + (hd + 1) * LANES] = _bf(dgr)

    col = lambda j: pl.BlockSpec((C, D), lambda c, j=j: (NC - 1 - c, j))
    tile = pl.BlockSpec((C, D), lambda c: (NC - 1 - c, 0))
    own, rider = _call(
        body, name, (NC,),
        [col(0), col(1), col(2), col(3), tile, tile,
         pl.BlockSpec((1, H, LANES, LANES), lambda c: (NC - 1 - c, 0, 0, 0)), _row(D), _row(LANES)],
        [pl.BlockSpec((C, D4), lambda c: (NC - 1 - c, 0)), _row(D), _row(LANES)],
        [jax.ShapeDtypeStruct((T, D4), BF16), jax.ShapeDtypeStruct((1, D), F32), jax.ShapeDtypeStruct((1, LANES), F32)],
        [pltpu.VMEM((H, LANES, LANES), F32), pltpu.VMEM((H, C, LANES), F32)],
        ("arbitrary",), (proj, proj, proj, proj, o, dog, st, lb, gg), ex)
    return (*own, rider) if ex is not None else own


def _prev_blk(tm, hb):
    return lambda i: (jnp.maximum(i * (tm // hb) - 1, 0), 0)


def _next_blk(tm, hb, T):
    return lambda i: (jnp.minimum((i + 1) * (tm // hb), T // hb - 1), 0)


def _glu_to_ext(uc_ref, ucp_ref, ext, D, first):
    def glu(u):
        return u[:, :D] * _sig(u[:, D:])
    gh = jnp.where(first, 0.0, glu(ucp_ref[...]))
    gm = glu(uc_ref[...])
    for c in range(D // LANES):
        ext[c, 0:CONV_HALO, :] = gh[:, c * LANES:(c + 1) * LANES]
        ext[c, CONV_HALO:, :] = gm[:, c * LANES:(c + 1) * LANES]


def conv_mid_fwd(uc, cw, cb, lg, lbias, name):
    T, D2 = uc.shape
    D = D2 // 2
    NCH = D // LANES
    W = 31
    tm = _rt(T, 256)
    RS = min(CONV_ROWS, tm)

    def body(uc_ref, ucp_ref, cw_ref, cb_ref, lg_ref, lb_ref, cv_ref, va_ref, ext, outs):
        _glu_to_ext(uc_ref, ucp_ref, ext, D, pl.program_id(0) == 0)

        def chunk(c, carry):
            for rb in range(tm // RS):
                acc = jnp.zeros((RS, LANES), F32)
                for kk in range(W):
                    acc = acc + cw_ref[c, pl.ds(kk, 1), :] * ext[c, pl.ds(rb * RS + CONV_HALO - (W - 1) + kk, RS), :]
                outs[c, pl.ds(rb * RS, RS), :] = acc
            return carry
        lax.fori_loop(0, NCH, chunk, 0)
        cv = jnp.concatenate([outs[c] for c in range(NCH)], axis=1) + cb_ref[...]
        cv_ref[...] = cv
        mu = jnp.mean(cv, axis=-1, keepdims=True)
        xc = cv - mu
        rstd = lax.rsqrt(jnp.mean(xc * xc, axis=-1, keepdims=True) + EPS)
        l = xc * rstd * lg_ref[...] + lb_ref[...]
        va_ref[...] = _bf(l * _sig(l))

    tile = pl.BlockSpec((tm, D), lambda i: (i, 0))
    return pl.pallas_call(
        body, name=name, grid=(T // tm,),
        in_specs=[pl.BlockSpec((tm, D2), lambda i: (i, 0)), pl.BlockSpec((CONV_HALO, D2), _prev_blk(tm, CONV_HALO)),
                  pl.BlockSpec((NCH, 32, LANES), lambda i: (0, 0, 0)), _row(D), _row(D), _row(D)],
        out_specs=[tile, tile],
        out_shape=[jax.ShapeDtypeStruct((T, D), F32), jax.ShapeDtypeStruct((T, D), BF16)],
        scratch_shapes=[pltpu.VMEM((NCH, tm + CONV_HALO, LANES), F32), pltpu.VMEM((NCH, tm, LANES), F32)],
        compiler_params=_cp("parallel"))(uc, uc, cw, cb, lg, lbias)


def conv_bwd_a(dva, cv, uc, lg, lbias, name):
    T, D2 = uc.shape
    D = D2 // 2
    NCH = D // LANES
    W = 31
    tm = _rt(T, 256)
    RS = min(CONV_ROWS, tm)

    def body(dva_ref, cv_ref, uc_ref, ucp_ref, lg_ref, lb_ref, dcv_ref, dlg_ref, dlb_ref, taps_ref, ext, dcs):
        @pl.when(pl.program_id(0) == 0)
        def _():
            dlg_ref[...] = jnp.zeros_like(dlg_ref)
            dlb_ref[...] = jnp.zeros_like(dlb_ref)
            taps_ref[...] = jnp.zeros_like(taps_ref)
        _glu_to_ext(uc_ref, ucp_ref, ext, D, pl.program_id(0) == 0)
        cv = cv_ref[...]
        mu = jnp.mean(cv, axis=-1, keepdims=True)
        xc = cv - mu
        rstd = lax.rsqrt(jnp.mean(xc * xc, axis=-1, keepdims=True) + EPS)
        yn = xc * rstd
        l = yn * lg_ref[...] + lb_ref[...]
        s = _sig(l)
        dl = dva_ref[...] * (s * (1.0 + l * (1.0 - s)))
        dlg_ref[...] += jnp.sum(dl * yn, axis=0, keepdims=True)
        dlb_ref[...] += jnp.sum(dl, axis=0, keepdims=True)
        dyn = dl * lg_ref[...]
        dcv = rstd * (dyn - jnp.mean(dyn, axis=-1, keepdims=True) - yn * jnp.mean(dyn * yn, axis=-1, keepdims=True))
        dcv_ref[...] = dcv
        for c in range(NCH):
            dcs[c] = dcv[:, c * LANES:(c + 1) * LANES]

        def fold(p):
            return p.reshape(RS // 8, 8, LANES).sum(axis=0)

        def chunk(c, carry):
            for kk in range(W):
                tot = jnp.zeros((8, LANES), F32)
                for rb in range(tm // RS):
                    tot = tot + fold(dcs[c, pl.ds(rb * RS, RS), :]
                                     * ext[c, pl.ds(rb * RS + CONV_HALO - (W - 1) + kk, RS), :])
                taps_ref[c, pl.ds(8 * kk, 8), :] += tot
            tot = jnp.zeros((8, LANES), F32)
            for rb in range(tm // RS):
                tot = tot + fold(dcs[c, pl.ds(rb * RS, RS), :])
            taps_ref[c, pl.ds(8 * W, 8), :] += tot
            return carry
        lax.fori_loop(0, NCH, chunk, 0)

    tile = pl.BlockSpec((tm, D), lambda i: (i, 0))
    vec = jax.ShapeDtypeStruct((1, D), F32)
    return pl.pallas_call(
        body, name=name, grid=(T // tm,),
        in_specs=[tile, tile, pl.BlockSpec((tm, D2), lambda i: (i, 0)),
                  pl.BlockSpec((CONV_HALO, D2), _prev_blk(tm, CONV_HALO)), _row(D), _row(D)],
        out_specs=[tile, _row(D), _row(D), pl.BlockSpec((NCH, 256, LANES), lambda i: (0, 0, 0))],
        out_shape=[jax.ShapeDtypeStruct((T, D), F32), vec, vec, jax.ShapeDtypeStruct((NCH, 256, LANES), F32)],
        scratch_shapes=[pltpu.VMEM((NCH, tm + CONV_HALO, LANES), F32), pltpu.VMEM((NCH, tm, LANES), F32)],
        compiler_params=_cp("arbitrary"))(dva, cv, uc, uc, lg, lbias)


def conv_bwd_b(dcv, uc, cw, name):
    T, D2 = uc.shape
    D = D2 // 2
    NCH = D // LANES
    W = 31
    tm = _rt(T, 256)
    RS = min(CONV_ROWS, tm)
    NT = T // tm

    def body(dcv_ref, dcn_ref, uc_ref, cw_ref, du_ref, db_ref, ext, outs):
        i = pl.program_id(0)

        @pl.when(i == 0)
        def _():
            db_ref[...] = jnp.zeros_like(db_ref)
        dm = dcv_ref[...]
        dn = jnp.where(i == NT - 1, 0.0, dcn_ref[...])
        for c in range(NCH):
            ext[c, 0:tm, :] = dm[:, c * LANES:(c + 1) * LANES]
            ext[c, tm:, :] = dn[:, c * LANES:(c + 1) * LANES]

        def chunk(c, carry):
            for rb in range(tm // RS):
                acc = jnp.zeros((RS, LANES), F32)
                for kk in range(W):
                    acc = acc + cw_ref[c, pl.ds(kk, 1), :] * ext[c, pl.ds(rb * RS + (W - 1) - kk, RS), :]
                outs[c, pl.ds(rb * RS, RS), :] = acc
            return carry
        lax.fori_loop(0, NCH, chunk, 0)
        dglu = jnp.concatenate([outs[c] for c in range(NCH)], axis=1)
        u = uc_ref[...]
        a = u[:, :D]
        sg = _sig(u[:, D:])
        da = dglu * sg
        dg = dglu * a * (sg * (1.0 - sg))
        du_ref[:, :D] = _bf(da)
        du_ref[:, D:] = _bf(dg)
        db_ref[:, :D] += jnp.sum(da, axis=0, keepdims=True)
        db_ref[:, D:] += jnp.sum(dg, axis=0, keepdims=True)

    tile = pl.BlockSpec((tm, D), lambda i: (i, 0))
    wide = pl.BlockSpec((tm, D2), lambda i: (i, 0))
    return pl.pallas_call(
        body, name=name, grid=(NT,),
        in_specs=[tile, pl.BlockSpec((CONV_HALO, D), _next_blk(tm, CONV_HALO, T)), wide,
                  pl.BlockSpec((NCH, 32, LANES), lambda i: (0, 0, 0))],
        out_specs=[wide, _row(D2)],
        out_shape=[jax.ShapeDtypeStruct((T, D2), BF16), jax.ShapeDtypeStruct((1, D2), F32)],
        scratch_shapes=[pltpu.VMEM((NCH, tm + CONV_HALO, LANES), F32), pltpu.VMEM((NCH, tm, LANES), F32)],
        compiler_params=_cp("arbitrary"))(dcv, dcv, uc, cw)


def _shift_rows(x, halo, k, back):
    rows, n = x.shape
    x3, h3 = x.reshape(rows // 8, 8, n), halo.reshape(1, 8, n)
    sub = lax.broadcasted_iota(jnp.int32, (1, 8, n), 1)
    if back:
        r = pltpu.roll(jnp.concatenate([h3, x3], axis=0), k, 1)
        y = jnp.where(sub < k, r[:-1], r[1:])
    else:
        r = pltpu.roll(jnp.concatenate([x3, h3], axis=0), 8 - k, 1)
        y = jnp.where(sub < 8 - k, r[:-1], r[1:])
    return y.reshape(rows, n)


def _conv3(u, uh, dw_ref, bias_ref):
    return (dw_ref[pl.ds(0, 1), :] * _shift_rows(u, uh, 2, True) + dw_ref[pl.ds(1, 1), :] * _shift_rows(u, uh, 1, True)
            + dw_ref[pl.ds(2, 1), :] * u + bias_ref[...])


def ffn_fwd(h, wup, dww, dwb, wdn, x, gate, pg, nxt, name):
    T, D = h.shape
    fs = wup.shape[2]
    NJ = wup.shape[0] // 2
    tm = _rt(T, 512)
    n_nxt = 0 if nxt is None else 3

    def body(h_ref, hp_ref, wa_ref, wb_ref, dwa_ref, dwb_ref, ba_ref, bb_ref, wd_ref, x_ref, gt_ref, pg_ref, *rest):
        nx, (u_ref, uc_ref, y_ref, xo_ref), more = rest[:n_nxt], rest[n_nxt:n_nxt + 4], rest[n_nxt + 4:]
        acc = more[-1]
        i, j = pl.program_id(0), pl.program_id(1)

        @pl.when(j == 0)
        def _():
            acc[...] = jnp.zeros_like(acc)
        hb = h_ref[...]
        hp = hp_ref[...]

        def half(s, w_ref, dw_ref, b_ref):
            u = _nn(hb, w_ref[...])
            uh = jnp.where(i == 0, 0.0, _nn(hp, w_ref[...]))
            u_ref[s] = u
            uc = _conv3(u, uh, dw_ref, b_ref)
            uc_ref[s] = _bf(uc)
            return uc
        a = half(0, wa_ref, dwa_ref, ba_ref)
        b = half(1, wb_ref, dwb_ref, bb_ref)
        acc[...] += _nn(_bf(a * _sig(a) * b), wd_ref[...])

        @pl.when(j == NJ - 1)
        def _():
            y = acc[...]
            y_ref[...] = y
            r = lax.rsqrt(jnp.mean(y * y, axis=-1, keepdims=True) + EPS)
            xo = x_ref[...] + gt_ref[...] * ((y * r) * pg_ref[...])
            xo_ref[...] = xo
            if nxt is not None:
                more[0][...] = _next_input(xo, *nx)

    tile = pl.BlockSpec((tm, D), lambda i, j: (i, 0))
    shard = lambda off, r: pl.BlockSpec((None, r, fs), lambda i, j: (j + off, 0, 0))
    ush = pl.BlockSpec((2, None, tm, fs), lambda i, j: (0, j, i, 0))
    vec = pl.BlockSpec((1, D), lambda i, j: (0, 0))
    return pl.pallas_call(
        body, name=name, grid=(T // tm, NJ),
        in_specs=[tile, pl.BlockSpec((FFN_HALO, D), lambda i, j: (jnp.maximum(i * (tm // FFN_HALO) - 1, 0), 0)),
                  shard(0, D), shard(NJ, D), shard(0, 3), shard(NJ, 3), shard(0, 1), shard(NJ, 1),
                  pl.BlockSpec((None, fs, D), lambda i, j: (j, 0, 0)), tile, vec, vec] + [vec] * n_nxt,
        out_specs=[ush, ush, tile, tile] + [tile] * (n_nxt > 0),
        out_shape=[jax.ShapeDtypeStruct((2, NJ, T, fs), F32), jax.ShapeDtypeStruct((2, NJ, T, fs), BF16),
                   jax.ShapeDtypeStruct((T, D), F32), jax.ShapeDtypeStruct((T, D), F32)]
        + [jax.ShapeDtypeStruct((T, D), BF16)] * (n_nxt > 0),
        scratch_shapes=[pltpu.VMEM((tm, D), F32)],
        compiler_params=_cp("parallel", "arbitrary"))(h, h, wup, wup, dww, dww, dwb, dwb, wdn, x, gate, pg,
                                                      *(nxt or ()))


def ffn_bwd_a(dy, uc, wdn, name):
    _, NJ, T, fs = uc.shape
    D = dy.shape[1]
    tm = _rt(T, 512)

    NT = T // tm

    def body(dy_ref, uc_ref, wd_ref, duc_ref, dwd_ref, acc):
        i = pl.program_id(1)

        @pl.when(i == 0)
        def _():
            acc[...] = jnp.zeros_like(acc)
        dyb = _bf(dy_ref[...])
        dz = _nt(dyb, wd_ref[...])
        a = uc_ref[0].astype(F32)
        b = uc_ref[1].astype(F32)
        sa = _sig(a)
        asa = a * sa
        acc[...] += _tn(_bf(asa * b), dyb)
        duc_ref[0] = _bf(dz * b * (sa + asa * (1.0 - sa)))
        duc_ref[1] = _bf(dz * asa)

        @pl.when(i == NT - 1)
        def _():
            dwd_ref[...] = _bf(acc[...])

    ush = pl.BlockSpec((2, None, tm, fs), lambda j, i: (0, j, i, 0))
    return pl.pallas_call(
        body, name=name, grid=(NJ, NT),
        in_specs=[pl.BlockSpec((tm, D), lambda j, i: (i, 0)), ush, pl.BlockSpec((None, fs, D), lambda j, i: (j, 0, 0))],
        out_specs=[ush, pl.BlockSpec((fs, D), lambda j, i: (j, 0))],
        out_shape=[jax.ShapeDtypeStruct((2, NJ, T, fs), BF16), jax.ShapeDtypeStruct((NJ * fs, D), BF16)],
        scratch_shapes=[pltpu.VMEM((fs, D), F32)],
        compiler_params=_cp("parallel", "arbitrary"))(dy, uc, wdn)


def ffn_bwd_b(duc, u, h, dww, wup, name, ex=None):
    NS, T, fs = duc.shape
    D = wup.shape[1]
    tm = _rt(T, 512)
    NT = T // tm

    def body(d_ref, dn_ref, u_ref, h_ref, dw_ref, w_ref, dh_ref, g_ref, dwup_ref, acc, accw, stage):
        i, s = pl.program_id(0), pl.program_id(1)

        @pl.when((i == 0) & (s == 0))
        def _():
            g_ref[...] = jnp.zeros_like(g_ref)

        @pl.when(s == 0)
        def _():
            acc[...] = jnp.zeros_like(acc)
        d = d_ref[...].astype(F32)
        dn = jnp.where(i == NT - 1, 0.0, dn_ref[...].astype(F32)[0:FFN_HALO])
        d1 = _shift_rows(d, dn, 1, False)
        d2 = _shift_rows(d, dn, 2, False)
        du = _bf(dw_ref[pl.ds(2, 1), :] * d + dw_ref[pl.ds(1, 1), :] * d1 + dw_ref[pl.ds(0, 1), :] * d2)
        acc[...] += _nt(du, w_ref[...])
        dw_part = _tn(h_ref[...], du)

        @pl.when(i == 0)
        def _():
            accw[s] = dw_part

        @pl.when(i > 0)
        def _():
            accw[s] += dw_part

        @pl.when(i == NT - 1)
        def _():
            stage[...] = _bf(accw[s])
            pltpu.sync_copy(stage, dwup_ref.at[s])
        u = u_ref[...]
        g_ref[s, pl.ds(0, 1), :] += jnp.sum(d2 * u, axis=0, keepdims=True)
        g_ref[s, pl.ds(1, 1), :] += jnp.sum(d1 * u, axis=0, keepdims=True)
        g_ref[s, pl.ds(2, 1), :] += jnp.sum(d * u, axis=0, keepdims=True)
        g_ref[s, pl.ds(3, 1), :] += jnp.sum(d, axis=0, keepdims=True)

        @pl.when(s == NS - 1)
        def _():
            dh_ref[...] = acc[...]

    ush = pl.BlockSpec((None, tm, fs), lambda i, s: (s, i, 0))
    unext = pl.BlockSpec((None, 2 * FFN_HALO, fs),
                         lambda i, s: (s, jnp.minimum((i + 1) * (tm // (2 * FFN_HALO)), T // (2 * FFN_HALO) - 1), 0))
    tile = pl.BlockSpec((tm, D), lambda i, s: (i, 0))
    own, rider = _call(
        body, name, (NT, NS),
        [ush, unext, ush, tile, pl.BlockSpec((None, 3, fs), lambda i, s: (s, 0, 0)),
         pl.BlockSpec((None, D, fs), lambda i, s: (s, 0, 0))],
        [tile, pl.BlockSpec((NS, 8, fs), lambda i, s: (0, 0, 0)), ANY],
        [jax.ShapeDtypeStruct((T, D), F32), jax.ShapeDtypeStruct((NS, 8, fs), F32),
         jax.ShapeDtypeStruct((NS, D, fs), BF16)],
        [pltpu.VMEM((tm, D), F32), pltpu.VMEM((NS, D, fs), F32), pltpu.VMEM((D, fs), BF16)],
        ("arbitrary", "arbitrary"), (duc, duc, u, h, dww, wup), ex, FFN_BWD_VMEM)
    return (*own, rider) if ex is not None else own


def ada_fwd(c_all, w, bias, name):
    L, D, n = w.shape

    def body(c_ref, w_ref, b_ref, o_ref):
        cv = c_ref[...]
        o_ref[...] = _nn(_bf(cv * _sig(cv)), _bf(w_ref[...])) + b_ref[...]

    return pl.pallas_call(
        body, name=name, grid=(L,),
        in_specs=[pl.BlockSpec((N_DEV, D), lambda l: (0, 0)), pl.BlockSpec((None, D, n), lambda l: (l, 0, 0)),
                  pl.BlockSpec((None, 1, n), lambda l: (l, 0, 0))],
        out_specs=pl.BlockSpec((None, N_DEV, n), lambda l: (l, 0, 0)),
        out_shape=jax.ShapeDtypeStruct((L, N_DEV, n), F32), compiler_params=_cp("parallel"))(c_all, w, bias)


def ada_bwd(c_all, dm, name):
    L, _, n = dm.shape
    D = c_all.shape[1]

    def body(c_ref, d_ref, o_ref):
        cv = c_ref[...]
        o_ref[...] = _tn(_bf(cv * _sig(cv)), _bf(d_ref[...]))

    return pl.pallas_call(
        body, name=name, grid=(L,),
        in_specs=[pl.BlockSpec((N_DEV, D), lambda l: (0, 0)), pl.BlockSpec((None, N_DEV, n), lambda l: (l, 0, 0))],
        out_specs=pl.BlockSpec((None, D, n), lambda l: (l, 0, 0)),
        out_shape=jax.ShapeDtypeStruct((L, D, n), F32), compiler_params=_cp("parallel"))(c_all, dm)


def sum_slots(g, name):
    S, R, C = g.shape

    def body(g_ref, o_ref):
        acc = g_ref[0]
        for s in range(1, S):
            acc = acc + g_ref[s]
        o_ref[...] = acc

    return pl.pallas_call(
        body, name=name, grid=(1,), in_specs=[pl.BlockSpec((S, R, C), lambda i: (0, 0, 0))],
        out_specs=pl.BlockSpec((R, C), lambda i: (0, 0)), out_shape=jax.ShapeDtypeStruct((R, C), F32),
        compiler_params=_cp("arbitrary"))(g)


def lb_softmax(logits, name):
    def body(l_ref, s_ref):
        l = l_ref[...]
        e = jnp.exp(l - jnp.max(l, axis=0, keepdims=True))
        s_ref[...] = e / jnp.sum(e, axis=0, keepdims=True)
    return pl.pallas_call(body, name=name, out_shape=jax.ShapeDtypeStruct(logits.shape, F32))(logits)


def adamw(w, gparts, m, v, name):
    P = len(gparts)
    S, Rp, C = gparts[0].shape
    R = P * Rp
    tr = Rp
    budget = (4 * 1024 * 1024) // (4 * (P * S + 7) * C)
    if Rp > budget:
        tr = max(t for t in range(16, budget + 1, 16) if Rp % t == 0)
    per = Rp // tr

    def body(w_ref, *rest):
        g_refs, (m_ref, v_ref, go_ref, d_ref, mo_ref, vo_ref) = rest[:P], rest[P:]
        part = pl.program_id(0) // per
        g = None
        for p, g_ref in enumerate(g_refs):
            gp = g_ref[0].astype(F32)
            for s in range(1, S):
                gp = gp + g_ref[s].astype(F32)
            g = gp if g is None else jnp.where(part == p, gp, g)
        go_ref[...] = g
        mn = ADAM_B1 * m_ref[...] + (1.0 - ADAM_B1) * g
        vn = ADAM_B2 * v_ref[...] + (1.0 - ADAM_B2) * (g * g)
        mo_ref[...] = mn
        vo_ref[...] = vn
        m_hat = mn / (1.0 - ADAM_B1 ** ADAM_STEP)
        v_hat = vn / (1.0 - ADAM_B2 ** ADAM_STEP)
        d_ref[...] = -ADAM_LR * (m_hat / (jnp.sqrt(v_hat) + ADAM_EPS) + ADAM_WD * w_ref[...])

    tile = pl.BlockSpec((tr, C), lambda i: (i, 0))
    slots = [pl.BlockSpec((S, tr, C), lambda i, p=p: (0, jnp.clip(i - p * per, 0, per - 1), 0)) for p in range(P)]
    out = jax.ShapeDtypeStruct((R, C), F32)
    return pl.pallas_call(
        body, name=name, grid=(R // tr,), in_specs=[tile] + slots + [tile, tile],
        out_specs=[tile] * 4, out_shape=[out] * 4, compiler_params=_cp("parallel"))(w, *gparts, m, v)


def _place():
    return lax.axis_index("x"), lax.axis_index("y"), lax.axis_index("c")


def _slot(p):
    return 4 * p[0] + 2 * p[1] + p[2]


class _Gather:
    def __init__(self, xs):
        self.ins = list(xs)
        n = self.n = len(xs)
        self.out_shapes = [jax.ShapeDtypeStruct((N_DEV,) + a.shape, a.dtype) for a in xs]
        self.scratch = [pltpu.SemaphoreType.DMA((n, 7)), pltpu.SemaphoreType.DMA((n, 7)), pltpu.SemaphoreType.DMA((n,))]

    def _parts(self, ins, outs, sems):
        send_sems, recv_sems, local_sems = sems
        x, y, c = _place()
        me, sib = (x, y, c), (x, y, 1 - c)
        chips = [(1 - x, y), (x, 1 - y), (1 - x, 1 - y)]

        def copy(a, k, block, to, src=None):
            dst = outs[a].at[_slot(block)]
            return pltpu.make_async_remote_copy(
                src_ref=dst if src is None else src, dst_ref=dst, send_sem=send_sems.at[a, k],
                recv_sem=recv_sems.at[a, k], device_id=to, device_id_type=MESH)

        mine = [pltpu.make_async_copy(ins[a], outs[a].at[_slot(me)], local_sems.at[a]) for a in range(self.n)]
        first = []
        for a in range(self.n):
            first.append(copy(a, 0, me, sib, src=ins[a]))
            first += [copy(a, 1 + j, me, (*chip, c), src=ins[a]) for j, chip in enumerate(chips)]
        return copy, mine, first, me, sib, chips, c

    def start(self, ins, outs, sems):
        _, mine, first, *_ = self._parts(ins, outs, sems)
        for cp in mine + first:
            cp.start()

    def finish(self, ins, outs, sems):
        copy, mine, first, me, sib, chips, c = self._parts(ins, outs, sems)
        passed = []
        for j, chip in enumerate(chips):
            for a in range(self.n):
                copy(a, 1 + j, (*chip, c), me).wait_recv()
                fwd = copy(a, 4 + j, (*chip, c), sib)
                fwd.start()
                passed.append(fwd)
        for a in range(self.n):
            copy(a, 0, sib, me).wait_recv()
            for j, chip in enumerate(chips):
                copy(a, 4 + j, (*chip, 1 - c), me).wait_recv()
        for cp in first + passed:
            cp.wait_send()
        for cp in mine:
            cp.wait()


def _run_alone(ex, name):
    def body(*refs):
        ni, no = len(ex.ins), len(ex.out_shapes)
        parts = refs[:ni], refs[ni:ni + no], refs[ni + no:]
        ex.start(*parts)
        ex.finish(*parts)
    return pl.pallas_call(body, name=name, in_specs=[ANY] * len(ex.ins), out_specs=[ANY] * len(ex.out_shapes),
                          out_shape=ex.out_shapes, scratch_shapes=ex.scratch)(*ex.ins)


def _ride(body, n_in, n_out, ex, first, last):
    ni, no = len(ex.ins), len(ex.out_shapes)

    def wrapped(*refs):
        a, r_in = refs[:n_in], refs[n_in:n_in + ni]
        b, r_out = refs[n_in + ni:n_in + ni + n_out], refs[n_in + ni + n_out:n_in + ni + n_out + no]
        rest = refs[n_in + ni + n_out + no:]
        s, r_s = rest[:len(rest) - len(ex.scratch)], rest[len(rest) - len(ex.scratch):]

        @pl.when(first())
        def _():
            ex.start(r_in, r_out, r_s)
        body(*a, *b, *s)

        @pl.when(last())
        def _():
            ex.finish(r_in, r_out, r_s)
    return wrapped


def all_gather(xs, name):
    return _run_alone(_Gather(xs), name)


class _Exchange:
    def __init__(self, groups):
        self.ins = [a for grp in groups for a in grp]
        self.where = [(g, i) for g, grp in enumerate(groups) for i in range(len(grp))]
        n = self.n = len(self.ins)
        self.out_shapes = [jax.ShapeDtypeStruct((N_DEV, len(grp)) + grp[0].shape[1:], grp[0].dtype) for grp in groups]
        self.scratch = [pltpu.SemaphoreType.DMA((n, 7)), pltpu.SemaphoreType.DMA((n, 7)), pltpu.SemaphoreType.DMA((n,))]

    def _parts(self, ins, outs, sems):
        send_sems, recv_sems, local_sems = sems
        x, y, c = _place()
        me = (x, y, c)

        def peer(r):
            return (1 - x if r & 4 else x, 1 - y if r & 2 else y, 1 - c if r & 1 else c)

        def land(a, src):
            g, i = self.where[a]
            return outs[g].at[_slot(src), i]

        def copy(a, r, src_dev):
            p = peer(r)
            return pltpu.make_async_remote_copy(
                src_ref=ins[a].at[_slot(p)], dst_ref=land(a, src_dev), send_sem=send_sems.at[a, r - 1],
                recv_sem=recv_sems.at[a, r - 1], device_id=p, device_id_type=MESH)

        mine = [pltpu.make_async_copy(ins[a].at[_slot(me)], land(a, me), local_sems.at[a]) for a in range(self.n)]
        sends = [copy(a, r, me) for r in range(1, N_DEV) for a in range(self.n)]
        arrivals = [copy(a, r, peer(r)) for r in range(1, N_DEV) for a in range(self.n)]
        return mine, sends, arrivals

    def start(self, ins, outs, sems):
        mine, sends, _ = self._parts(ins, outs, sems)
        for cp in mine + sends:
            cp.start()

    def finish(self, ins, outs, sems):
        mine, sends, arrivals = self._parts(ins, outs, sems)
        for cp in arrivals:
            cp.wait_recv()
        for cp in sends:
            cp.wait_send()
        for cp in mine:
            cp.wait()


def all_to_all(groups, name):
    return _run_alone(_Exchange(groups), name)


def _mods(mod, l, D):
    return [mod[l:l + 1, k * D:(k + 1) * D] for k in range(6)]


def _ffn_backward(l, dxo, dy, hb, u, uc, xin, sc2, gains, W, post, ex=None):
    _, NJ, T, fs = u.shape
    duc, d_wdn = ffn_bwd_a(dy, uc, W["wdn"][l], f"ffn{l}_bwd_a")
    dh, taps, d_wup, *rider = ffn_bwd_b(duc.reshape(2 * NJ, T, fs), u.reshape(2 * NJ, T, fs), hb, W["fdw_w"][l],
                                        W["wup"][l], f"ffn{l}_bwd_b", ex)
    dx, da, dsh, *below = resid_bwd(f"ffn{l}_resid_bwd", pre=(dh, xin, dxo, gains["pre_ffn_g"][l:l + 1], sc2), post=post)
    small = dict(da=da, dsh=dsh, dwb=taps[:, 3], dww=taps[:, 0:3])
    return dx, d_wup, d_wdn, small, below, (rider[0] if rider else None)


def _local_step(xt, tgt, mod, gains, W, mine):
    T, D = xt.shape
    zero_d = jnp.zeros((1, D), F32)
    half = lambda g: g.reshape(N_DEV // 2, 2 * g.shape[1], D)
    whole = lambda g: jnp.transpose(g, (1, 0, 2)).reshape(g.shape[1], -1)
    sh1, sc1, g1, sh2, sc2, g2 = m0 = _mods(mod, 0, D)
    h0, (whin,) = prenorm_mod(xt, gains["pre_mix_g"][0:1], sc1, sh1, "l0_prenorm", _Gather([mine["whin"]]))
    W = dict(W, whin=whin)
    proj, (whout, wup0, wdn0) = mm_nn(h0, W["whin"], jnp.zeros((1, 4 * D), F32), "hgrn_proj",
                                      _Gather([mine["whout"], mine["wup"][0], mine["wdn"][0]]))
    o, og, st, (wcin, wcout, wup1, wdn1) = hgrn_fwd(
        proj, W["lb"], W["gg"], "hgrn_fwd", _Gather([mine["wcin"], mine["wcout"], mine["wup"][1], mine["wdn"][1]]))
    W = dict(W, whout=whout.reshape(D, D), wcin=wcin, wcout=wcout.reshape(D, D), wup=[wup0, wup1],
             wdn=[half(wdn0), half(wdn1)])
    t1, c1, e1, t2, c2, e2 = m1 = _mods(mod, 1, D)
    y0, x1, h0f = mm_post(og, W["whout"], zero_d, xt, g1, gains["post_mix_g"][0:1],
                          (gains["pre_ffn_g"][0:1], sc2, sh2), "hgrn_out")
    u0, uc0, yf0, x2, h1 = ffn_fwd(h0f, W["wup"][0], W["fdw_w"][0], W["fdw_b"][0], W["wdn"][0], x1, g2,
                                   gains["post_ffn_g"][0:1], (gains["pre_mix_g"][1:2], c1, t1), "ffn0_fwd")
    uc = mm_nn(h1, W["wcin"], W["cb_in"], "conv_in")
    cv, va = conv_mid_fwd(uc, W["cw"], W["cdw_b"], W["ln_g"], W["ln_b"], "conv_mid_fwd")
    y1, x3, h1f = mm_post(va, W["wcout"], W["cb_out"], x2, e1, gains["post_mix_g"][1:2],
                          (gains["pre_ffn_g"][1:2], c2, t2), "conv_out")
    u1, uc1, yf1, x4 = ffn_fwd(h1f, W["wup"][1], W["fdw_w"][1], W["fdw_b"][1], W["wdn"][1], x3, e2,
                               gains["post_ffn_g"][1:2], None, "ffn1_fwd")
    dx4, loss_part, dyf1, dgate_f1, dpost_f1, _ = resid_bwd(
        "loss_resid_bwd", loss=(x4, tgt), post=(yf1, e2, gains["post_ffn_g"][1:2]))
    dx3, d_wup1, d_wdn1, sf1, (dy1, dgate_c, dpost_c, dys_c), _ = _ffn_backward(
        1, dx4, dyf1, h1f, u1, uc1, x3, c2, gains, W, (y1, e1, gains["post_mix_g"][1:2]))
    dva = mm_nt(dy1, W["wcout"], "conv_dva")
    d_wcout = mm_tn(va, dy1, "conv_dwout", True, True, D, D)
    dcv, dlg, dlbias, taps = conv_bwd_a(dva, cv, uc, W["ln_g"], W["ln_b"], "conv_bwd_a")
    duc, dbin = conv_bwd_b(dcv, uc, W["cw"], "conv_bwd_b")
    dh1 = mm_nt(duc, whole(W["wcin"]), "conv_dh")
    d_wcin = mm_tn(h1, duc, "conv_dwin", True, True, D, W["wcin"].shape[2])
    dx2, da_c, dsh_c, dyf0, dgate_f0, dpost_f0, _ = resid_bwd(
        "l1_resid_bwd", pre=(dh1, x2, dx3, gains["pre_mix_g"][1:2], c1), post=(yf0, g2, gains["post_ffn_g"][0:1]))
    rows = lambda g: g.reshape(N_DEV, -1, D)
    dx1, d_wup0, d_wdn0, sf0, (dy0, dgate_h, dpost_h, _), (r_wcin, r_wcout, r_wup1, r_wdn1) = _ffn_backward(
        0, dx2, dyf0, h0f, u0, uc0, x1, sc2, gains, W, (y0, g1, gains["post_mix_g"][0:1]),
        _Exchange([[d_wcin], [rows(d_wcout)], [d_wup1], [rows(d_wdn1)]]))
    dog = mm_nt(dy0, W["whout"], "hgrn_dog")
    d_whout = mm_tn(og, dy0, "hgrn_dwout", True, True, D, D)
    dproj, dlb, dgg, (r_wup0, r_wdn0, r_whout) = hgrn_bwd(
        proj, o, dog, st, W["lb"], W["gg"], "hgrn_bwd", _Exchange([[d_wup0], [rows(d_wdn0)], [rows(d_whout)]]))
    d_whin = mm_tn(h0, dproj, "hgrn_dwin", True, True, D, W["whin"].shape[2])
    dh0, (r_whin,) = mm_nt(dproj, whole(W["whin"]), "hgrn_dh", _Exchange([[d_whin]]))
    dx0, da_h, dsh_h = resid_bwd("l0_resid_bwd", pre=(dh0, xt, dx1, gains["pre_mix_g"][0:1], sc1))
    sf0.update(dgate=dgate_f0, dpost=dpost_f0)
    sf1.update(dgate=dgate_f1, dpost=dpost_f1)

    big = dict(hgrn_w_in=r_whin, hgrn_w_out=r_whout, conv_w_in=r_wcin, conv_w_out=r_wcout,
               ffn_w_up=[r_wup0, r_wup1], ffn_w_down=[r_wdn0, r_wdn1])
    pm, pf = gains["pre_mix_g"], gains["pre_ffn_g"]
    dmod = jnp.concatenate([
        dsh_h, da_h * pm[0:1], dgate_h, sf0["dsh"], sf0["da"] * pf[0:1], sf0["dgate"],
        dsh_c, da_c * pm[1:2], dgate_c, sf1["dsh"], sf1["da"] * pf[1:2], sf1["dgate"]], axis=1)
    NCH = D // LANES
    tap = taps.reshape(NCH, 32, 8, LANES).sum(axis=2)
    small = dict(
        dmod=dmod,
        pre_mix_g=jnp.concatenate([da_h * (1.0 + sc1), da_c * (1.0 + c1)], axis=0),
        post_mix_g=jnp.concatenate([dpost_h, dpost_c], axis=0),
        pre_ffn_g=jnp.concatenate([sf0["da"] * (1.0 + sc2), sf1["da"] * (1.0 + c2)], axis=0),
        post_ffn_g=jnp.concatenate([sf0["dpost"], sf1["dpost"]], axis=0),
        lb=dlb, gg=dgg,
        ffn_dw_b=jnp.stack([sf0["dwb"], sf1["dwb"]]),
        conv_b_in=dbin,
        conv_dw_w=jnp.transpose(tap[:, :31], (1, 0, 2)).reshape(31, D),
        conv_dw_b=tap[:, 31].reshape(1, D),
        conv_ln_g=dlg, conv_ln_b=dlbias, conv_b_out=dys_c,
        ffn_dw_w=jnp.stack([sf0["dww"], sf1["dww"]]))
    return loss_part, dx0, big, small


SMALL_ORDER = ["dmod", "pre_mix_g", "post_mix_g", "pre_ffn_g", "post_ffn_g", "lb", "gg", "ffn_dw_b", "conv_b_in",
               "conv_dw_w", "conv_dw_b", "conv_ln_g", "conv_ln_b", "conv_b_out", "ffn_dw_w"]


def _pack(parts):
    flat = jnp.concatenate([p.reshape(-1) for p in parts])
    pad = (-flat.shape[0]) % LANES
    if pad:
        flat = jnp.concatenate([flat, jnp.zeros((pad,), F32)])
    return flat.reshape(-1, LANES)


def _unpack(flat, shapes):
    out, off = [], 0
    for s in shapes:
        n = 1
        for d in s:
            n *= d
        out.append(flat[off:off + n].reshape(s))
        off += n
    return out


def _apply_adamw(name, w, g_slots, m, v):
    shp = w.shape
    C = shp[-1]
    R = w.size // C
    parts = g_slots if isinstance(g_slots, list) else [g_slots]
    parts = [p.reshape(p.shape[0], R // len(parts), C) for p in parts]
    g, d, mn, vn = adamw(w.reshape(R, C), parts, m.reshape(R, C), v.reshape(R, C), "adamw_" + name)
    return g.reshape(shp), d.reshape(shp), mn.reshape(shp), vn.reshape(shp)


WEIGHTS = ['ada_w', 'ada_b', 'pre_mix_g', 'post_mix_g', 'pre_ffn_g', 'post_ffn_g', 'hgrn_w_in', 'hgrn_lb_logits',
           'hgrn_gnorm_g', 'hgrn_w_out', 'conv_w_in', 'conv_b_in', 'conv_dw_w', 'conv_dw_b', 'conv_ln_g', 'conv_ln_b',
           'conv_w_out', 'conv_b_out', 'ffn_w_up', 'ffn_dw_w', 'ffn_dw_b', 'ffn_w_down']


def kernel(x, c, ada_w, ada_b, pre_mix_g, post_mix_g, pre_ffn_g, post_ffn_g, hgrn_w_in, hgrn_lb_logits, hgrn_gnorm_g, hgrn_w_out, conv_w_in, conv_b_in, conv_dw_w, conv_dw_b, conv_ln_g, conv_ln_b, conv_w_out, conv_b_out, ffn_w_up, ffn_dw_w, ffn_dw_b, ffn_w_down, loss_target, m_ada_w, m_ada_b, m_pre_mix_g, m_post_mix_g, m_pre_ffn_g, m_post_ffn_g, m_hgrn_w_in, m_hgrn_lb_logits, m_hgrn_gnorm_g, m_hgrn_w_out, m_conv_w_in, m_conv_b_in, m_conv_dw_w, m_conv_dw_b, m_conv_ln_g, m_conv_ln_b, m_conv_w_out, m_conv_b_out, m_ffn_w_up, m_ffn_dw_w, m_ffn_dw_b, m_ffn_w_down, v_ada_w, v_ada_b, v_pre_mix_g, v_post_mix_g, v_pre_ffn_g, v_post_ffn_g, v_hgrn_w_in, v_hgrn_lb_logits, v_hgrn_gnorm_g, v_hgrn_w_out, v_conv_w_in, v_conv_b_in, v_conv_dw_w, v_conv_dw_b, v_conv_ln_g, v_conv_ln_b, v_conv_w_out, v_conv_b_out, v_ffn_w_up, v_ffn_dw_w, v_ffn_dw_b, v_ffn_w_down):
    P = dict(locals())
    _, T, D = x.shape
    me = _slot(_place())
    L = ada_w.shape[0]
    na = ada_w.shape[2]
    fs = ffn_w_up.shape[2]
    nci = conv_w_in.shape[2]
    nd = conv_dw_b.shape[1]
    NCH = D // LANES

    small_in = [c, conv_b_in, conv_dw_w, conv_dw_b, conv_ln_g, conv_ln_b, conv_b_out, ffn_dw_w]
    (packed,) = all_gather([_pack(small_in)], "gather_first")
    mine = dict(whin=_bf(hgrn_w_in[0]), whout=_bf(hgrn_w_out[0]), wcin=_bf(conv_w_in[0]), wcout=_bf(conv_w_out[0]),
                wup=[_bf(ffn_w_up[l]) for l in range(L)], wdn=[_bf(ffn_w_down[l]) for l in range(L)])
    sm = packed.reshape(N_DEV, -1)
    offs = [0]
    for a in small_in:
        offs.append(offs[-1] + a.size)
    piece = lambda k, shp: sm[:, offs[k]:offs[k + 1]].reshape((N_DEV,) + shp)
    c_all = piece(0, (D,))
    dw_nat = jnp.transpose(piece(2, (31, nd)), (1, 0, 2)).reshape(31, D)
    cw = jnp.pad(jnp.transpose(dw_nat.reshape(31, NCH, LANES), (1, 0, 2)), ((0, 0), (0, 1), (0, 0)))
    fdw = piece(7, (L, 3, fs))
    lb_soft = lb_softmax(hgrn_lb_logits, "lb_softmax")
    W = dict(
        lb=lb_soft[0:1], gg=hgrn_gnorm_g, cb_in=piece(1, (nci,)).reshape(1, N_DEV * nci), cw=cw,
        cdw_b=piece(3, (nd,)).reshape(1, D), ln_g=piece(4, (nd,)).reshape(1, D), ln_b=piece(5, (nd,)).reshape(1, D),
        cb_out=piece(6, (nd,)).reshape(1, D), fdw_w=[fdw[:, l] for l in range(L)],
        fdw_b=[ffn_dw_b[l].reshape(N_DEV, 1, fs) for l in range(L)])

    bias_mine = lax.dynamic_slice(ada_b, (0, me * na), (L, na)).reshape(L, 1, na)
    mod_cols = ada_fwd(c_all, ada_w, bias_mine, "ada_fwd")
    (mod_all,) = all_gather([mod_cols], "gather_mod")
    mod = jnp.transpose(lax.dynamic_index_in_dim(mod_all, me, axis=2, keepdims=False), (1, 0, 2)).reshape(L, N_DEV * na)

    gains = dict(pre_mix_g=pre_mix_g, post_mix_g=post_mix_g, pre_ffn_g=pre_ffn_g, post_ffn_g=post_ffn_g)
    loss_part, dx0, big, small = _local_step(x[0], loss_target[0], mod, gains, W, mine)
    loss = lax.psum((0.5 / D) * jnp.sum(loss_part), ("x", "y", "c"))

    parts = [small[k] for k in SMALL_ORDER]
    (sm_all,) = all_gather([_pack(parts)], "gather_small_grads")
    tot = _unpack(sum_slots(sm_all, "sum_small_grads").reshape(-1), [p.shape for p in parts])
    tot = dict(zip(SMALL_ORDER, tot))
    dmod_all = sm_all.reshape(N_DEV, -1)[:, :L * 6 * D].reshape(N_DEV, L, 6 * D)
    dm_mine = jnp.transpose(lax.dynamic_slice(dmod_all, (0, 0, me * na), (N_DEV, L, na)), (1, 0, 2))
    s0 = lb_soft[0:1]
    onehot = (lax.broadcasted_iota(jnp.int32, lb_soft.shape, 0) == 0).astype(F32)
    col = lambda a, n: lax.dynamic_slice_in_dim(a, me * n, n, axis=a.ndim - 1)
    G = {
        'ada_w': ada_bwd(c_all, dm_mine, "ada_bwd")[None],
        'ada_b': tot["dmod"].reshape(1, L, 6 * D),
        'pre_mix_g': tot["pre_mix_g"][None], 'post_mix_g': tot["post_mix_g"][None],
        'pre_ffn_g': tot["pre_ffn_g"][None], 'post_ffn_g': tot["post_ffn_g"][None],
        'hgrn_lb_logits': (tot["lb"] * s0 * (onehot - lb_soft))[None],
        'hgrn_gnorm_g': tot["gg"][None],
        'ffn_dw_b': tot["ffn_dw_b"].reshape(1, L, N_DEV * fs),
        'conv_b_in': col(tot["conv_b_in"], nci)[None],
        'conv_dw_w': col(tot["conv_dw_w"], nd)[None, None],
        'conv_dw_b': col(tot["conv_dw_b"], nd)[None], 'conv_ln_g': col(tot["conv_ln_g"], nd)[None],
        'conv_ln_b': col(tot["conv_ln_b"], nd)[None], 'conv_b_out': col(tot["conv_b_out"], nd)[None],
        'ffn_dw_w': lax.dynamic_index_in_dim(tot["ffn_dw_w"], me, axis=1, keepdims=False)[None],
    }

    G.update(big)
    res = {n: _apply_adamw(n, P[n], G[n], P["m_" + n], P["v_" + n]) for n in WEIGHTS}
    return (loss, dx0[None], *[res[n][0] for n in WEIGHTS], *[res[n][1] for n in WEIGHTS],
            *[res[n][2] for n in WEIGHTS], *[res[n][3] for n in WEIGHTS])
```
